```python
import functools
import jax, jax.numpy as jnp
from jax import lax
import numpy as np

D_MODEL = 1024
BATCH = 2
SEQ = 8192
DEPTH = 1
DEC_BATCH = 128
DEC_SEQ = 1
PAST_LEN = 8192
PAGE_SIZE = 128

RET_HEADS = 4
RET_DK = 128
RET_DV = 128
RET_CHUNK = 128
SWA_HEADS = 8
SWA_KV_HEADS = 2
SWA_HD = 64
SWA_GROUP = SWA_HEADS // SWA_KV_HEADS
WINDOW = 128
SWA_BLOCK = WINDOW
ROPE_THETA = 10000.0
N_GROUPS = 4
EXPERTS_PER_GROUP = 8
N_EXPERTS = N_GROUPS * EXPERTS_PER_GROUP
TOP_K_INNER = 2
D_EXPERT = 256
MOE_BLOCK = 128
NORM_EPS = 1e-6
IN_SIZES = (RET_HEADS * RET_DK, RET_HEADS * RET_DK, RET_HEADS * RET_DV, RET_HEADS * RET_DV, SWA_HEADS * SWA_HD, SWA_KV_HEADS * SWA_HD, SWA_KV_HEADS * SWA_HD, D_MODEL, D_MODEL)
N_IN = sum(IN_SIZES)

kernel_name = 'hybrid_retention_swa_hmoe_step'


def _rmsnorm(x, g):
    xf = x.astype(jnp.float32)
    y = xf * lax.rsqrt(jnp.mean(xf * xf, axis=-1, keepdims=True) + NORM_EPS)
    return (y * g.astype(jnp.float32)).astype(x.dtype)


def _rope(x, pos):
    d = x.shape[-1]
    inv = ROPE_THETA ** (-jnp.arange(0, d, 2, dtype=jnp.float32) / d)
    ang = pos.astype(jnp.float32)[:, None] * inv[None, :]
    cos = jnp.cos(ang)[None, :, None, :]
    sin = jnp.sin(ang)[None, :, None, :]
    xf = x.astype(jnp.float32)
    x1, x2 = xf[..., : d // 2], xf[..., d // 2:]
    return jnp.concatenate([x1 * cos - x2 * sin, x2 * cos + x1 * sin], axis=-1).astype(x.dtype)


def _retnet_rotate(x, pos):
    d = x.shape[-1]
    inv = 1.0 / (ROPE_THETA ** jnp.linspace(0.0, 1.0, d // 2, dtype=jnp.float32))
    ang = pos.astype(jnp.float32)[:, None] * inv[None, :]
    cos = jnp.cos(ang)[None, :, None, :]
    sin = jnp.sin(ang)[None, :, None, :]
    xf = x.astype(jnp.float32).reshape(x.shape[:-1] + (d // 2, 2))
    x1, x2 = xf[..., 0], xf[..., 1]
    return jnp.stack([x1 * cos - x2 * sin, x2 * cos + x1 * sin], axis=-1).reshape(x.shape).astype(x.dtype)


def _retention(q, k, v, s0, chunk):
    B, T, H, DK = q.shape
    DV = v.shape[-1]
    NC = T // chunk
    ld = jnp.log(1.0 - 2.0 ** (-5.0 - jnp.arange(H, dtype=jnp.float32)))
    qf = q.astype(jnp.float32).reshape(B, NC, chunk, H, DK)
    kf = (k.astype(jnp.float32) * DK ** -0.5).reshape(B, NC, chunk, H, DK)
    vf = v.astype(jnp.float32).reshape(B, NC, chunk, H, DV)
    idx = jnp.arange(chunk, dtype=jnp.float32)
    diff = idx[:, None] - idx[None, :]
    causal = diff >= 0
    dmask = jnp.where(causal[None], jnp.exp(ld[:, None, None] * jnp.where(causal, diff, 0.0)[None]), 0.0)
    att = jnp.einsum('bnihd,bnjhd->bnhij', qf, kf) * dmask[None, None]
    o_intra = jnp.einsum('bnhij,bnjhe->bnihe', att, vf)
    k_dec = jnp.exp(ld[None, :] * (chunk - 1.0 - idx)[:, None])
    kv = jnp.einsum('bnjhd,bnjhe->bnhde', kf * k_dec[:, :, None], vf)
    chunk_decay = jnp.exp(ld * chunk)[None, :, None, None]

    def step(s, kv_c):
        return chunk_decay * s + kv_c, s

    s_final, s_before = lax.scan(step, s0.astype(jnp.float32), jnp.moveaxis(kv, 1, 0))
    s_before = jnp.moveaxis(s_before, 0, 1)
    q_dec = jnp.exp(ld[None, :] * (idx + 1.0)[:, None])
    o_cross = jnp.einsum('bnihd,bnhde->bnihe', qf * q_dec[:, :, None], s_before)
    return (o_intra + o_cross).reshape(B, T, H, DV), s_final


def _sink_probs(s, mask, sink):
    sk = sink.astype(jnp.float32).reshape(SWA_KV_HEADS, SWA_GROUP, 1, 1)
    s = jnp.where(mask, s, -jnp.inf)
    m = jnp.maximum(jnp.max(s, axis=-1, keepdims=True), sk)
    p = jnp.exp(s - m)
    return p / (jnp.sum(p, axis=-1, keepdims=True) + jnp.exp(sk - m))


def _swa_prompt(q, k, v, sink):
    B, T, H, hd = q.shape
    C = SWA_BLOCK
    NB = T // C
    qb = q.reshape(B, NB, C, SWA_KV_HEADS, SWA_GROUP, hd)
    kb = k.reshape(B, NB, C, SWA_KV_HEADS, hd)
    vb = v.reshape(B, NB, C, SWA_KV_HEADS, hd)

    def with_prev(a):
        prev = jnp.concatenate([jnp.zeros_like(a[:, :1]), a[:, :-1]], axis=1)
        return jnp.concatenate([prev, a], axis=2)

    kk, vv = with_prev(kb), with_prev(vb)
    s = jnp.einsum('bnqkgd,bnskd->bnkgqs', qb, kk).astype(jnp.float32) * hd ** -0.5
    blk = jnp.arange(NB)[:, None] * C
    qpos = blk + jnp.arange(C)[None, :]
    kpos = blk - C + jnp.arange(2 * C)[None, :]
    dist = qpos[:, :, None] - kpos[:, None, :]
    mask = (dist >= 0) & (dist < WINDOW) & (kpos[:, None, :] >= 0)
    p = _sink_probs(s, mask[None, :, None, None], sink)
    o = jnp.einsum('bnkgqs,bnskd->bnqkgd', p.astype(vv.dtype), vv).reshape(B, T, H * hd)
    W = min(WINDOW, T)
    return o, k[:, T - W:], v[:, T - W:]


def _swa_sample(q, k, v, k_buf, v_buf, sink):
    DB, L, H, hd = q.shape
    W = k_buf.shape[1]
    kk = jnp.concatenate([k_buf.astype(k.dtype), k], axis=1)
    vv = jnp.concatenate([v_buf.astype(v.dtype), v], axis=1)
    qb = q.reshape(DB, L, SWA_KV_HEADS, SWA_GROUP, hd)
    s = jnp.einsum('bqkgd,bskd->bkgqs', qb, kk).astype(jnp.float32) * hd ** -0.5
    qpos = PAST_LEN + jnp.arange(L)
    kpos = PAST_LEN - W + jnp.arange(W + L)
    dist = qpos[:, None] - kpos[None, :]
    mask = (dist >= 0) & (dist < WINDOW)
    p = _sink_probs(s, mask, sink)
    o = jnp.einsum('bkgqs,bskd->bqkgd', p.astype(vv.dtype), vv).reshape(DB, L, H * hd)
    return o, kk[:, L:], vv[:, L:]


def _moe(h, w_rg, b_rg, w_re, b_re, w1, w3, w2):
    hf = h.astype(jnp.float32)
    g_logits = hf @ w_rg.astype(jnp.float32) + b_rg.astype(jnp.float32)
    g_sel = jnp.argmax(g_logits, axis=-1)
    p_group = jnp.max(jax.nn.softmax(g_logits, axis=-1), axis=-1, keepdims=True)
    e_logits = (hf @ w_re.astype(jnp.float32) + b_re.astype(jnp.float32)).reshape(-1, N_GROUPS, EXPERTS_PER_GROUP)
    e_in = jnp.einsum('tge,tg->te', e_logits, jax.nn.one_hot(g_sel, N_GROUPS, dtype=jnp.float32))
    top_v, top_i = lax.top_k(e_in, TOP_K_INNER)
    w_sel = jax.nn.softmax(top_v, axis=-1) * p_group
    e_id = g_sel[:, None] * EXPERTS_PER_GROUP + top_i
    comb = jnp.sum(jax.nn.one_hot(e_id, N_EXPERTS, dtype=jnp.float32) * w_sel[..., None], axis=1)
    a = jnp.einsum('td,edf->tef', h, w1)
    b = jnp.einsum('td,edf->tef', h, w3)
    hid = jax.nn.silu(a) * b * comb.astype(h.dtype)[:, :, None]
    return jnp.einsum('tef,efd->td', hid, w2)


def _apply_moe(h, blocked, w_rg, b_rg, w_re, b_re, w1, w3, w2):
    B, T, D = h.shape
    flat = h.reshape(-1, D)
    fn = functools.partial(_moe, w_rg=w_rg, b_rg=b_rg, w_re=w_re, b_re=b_re, w1=w1, w3=w3, w2=w2)
    if blocked:
        out = lax.map(fn, flat.reshape(-1, MOE_BLOCK, D))
    else:
        out = fn(flat)
    return out.reshape(B, T, D)


def _split_in(z):
    parts, start = [], 0
    for size in IN_SIZES:
        parts.append(z[..., start:start + size])
        start += size
    return parts


def _layer(x, c, pos, ret_s0, ret_chunk, swa_core, moe_blocked, w_ada, b_ada, n1, n2, w_in, w_up_ret, w_up_swa, w_o, w_rg, b_rg, w_re, b_re, w1, w3, w2):
    B, T, _ = x.shape
    mod = jax.nn.silu(c) @ w_ada + b_ada
    sh1, sc1, g1, sh2, sc2, g2 = [m[:, None, :] for m in jnp.split(mod, 6, axis=-1)]
    h = _rmsnorm(x, n1) * (1 + sc1) + sh1
    rq, rk, rv, rg, sq, sk, sv, za, zb = _split_in(h @ w_in)
    rq = _retnet_rotate(rq.reshape(B, T, RET_HEADS, RET_DK), pos)
    rk = _retnet_rotate(rk.reshape(B, T, RET_HEADS, RET_DK), pos)
    rv = rv.reshape(B, T, RET_HEADS, RET_DV)
    o_ret, ret_state = _retention(rq, rk, rv, ret_s0, ret_chunk)
    o_ret = o_ret * lax.rsqrt(jnp.mean(o_ret * o_ret, axis=-1, keepdims=True) + NORM_EPS)
    y_ret = (o_ret.reshape(B, T, -1).astype(x.dtype) * jax.nn.silu(rg)) @ w_up_ret
    sq = _rope(sq.reshape(B, T, SWA_HEADS, SWA_HD), pos)
    sk = _rope(sk.reshape(B, T, SWA_KV_HEADS, SWA_HD), pos)
    sv = sv.reshape(B, T, SWA_KV_HEADS, SWA_HD)
    o_swa, k_buf, v_buf = swa_core(sq, sk, sv)
    y_swa = o_swa @ w_up_swa
    merged = jax.nn.sigmoid(za) * y_ret + jax.nn.sigmoid(zb) * y_swa
    x = x + g1 * (merged @ w_o)
    h2 = _rmsnorm(x, n2) * (1 + sc2) + sh2
    x = x + g2 * _apply_moe(h2, moe_blocked, w_rg, b_rg, w_re, b_re, w1, w3, w2)
    return x, ret_state, k_buf, v_buf


def setup_inputs(seed: int = 0) -> dict:
    key = jax.random.key(seed)
    ks = jax.random.split(key, 24)
    f32 = jnp.float32
    D = D_MODEL
    W = min(WINDOW, PAST_LEN)

    def nrm(k, shape, scale):
        return jax.random.normal(k, shape, f32) * scale

    return {
        'x_prompt': nrm(ks[0], (BATCH, SEQ, D), 1.0),
        'x_sample': nrm(ks[1], (DEC_BATCH, DEC_SEQ, D), 1.0),
        'c_prompt': nrm(ks[2], (BATCH, D), 1.0),
        'c_sample': nrm(ks[3], (DEC_BATCH, D), 1.0),
        'state_ret': nrm(ks[4], (DEPTH, DEC_BATCH, RET_HEADS, RET_DK, RET_DV), 0.1),
        'cache_swa_k': nrm(ks[5], (DEPTH, DEC_BATCH, W, SWA_KV_HEADS, SWA_HD), 1.0),
        'cache_swa_v': nrm(ks[6], (DEPTH, DEC_BATCH, W, SWA_KV_HEADS, SWA_HD), 1.0),
        'w_ada': nrm(ks[7], (DEPTH, D, 6 * D), 0.2 * D ** -0.5),
        'b_ada': nrm(ks[8], (DEPTH, 6 * D), 0.02),
        'norm1_g': 1.0 + nrm(ks[9], (DEPTH, D), 0.02),
        'norm2_g': 1.0 + nrm(ks[10], (DEPTH, D), 0.02),
        'w_in': nrm(ks[11], (DEPTH, D, N_IN), D ** -0.5),
        'w_up_ret': nrm(ks[12], (DEPTH, RET_HEADS * RET_DV, D), (RET_HEADS * RET_DV) ** -0.5),
        'w_up_swa': nrm(ks[13], (DEPTH, SWA_HEADS * SWA_HD, D), (SWA_HEADS * SWA_HD) ** -0.5),
        'w_o': nrm(ks[14], (DEPTH, D, D), D ** -0.5),
        'sink': nrm(ks[15], (DEPTH, SWA_HEADS), 0.5),
        'w_rg': nrm(ks[16], (DEPTH, D, N_GROUPS), D ** -0.5),
        'b_rg': nrm(ks[17], (DEPTH, N_GROUPS), 0.01),
        'w_re': nrm(ks[18], (DEPTH, D, N_EXPERTS), D ** -0.5),
        'b_re': nrm(ks[19], (DEPTH, N_EXPERTS), 0.01),
        'w1': nrm(ks[20], (DEPTH, N_EXPERTS, D, D_EXPERT), D ** -0.5),
        'w3': nrm(ks[21], (DEPTH, N_EXPERTS, D, D_EXPERT), D ** -0.5),
        'w2': nrm(ks[22], (DEPTH, N_EXPERTS, D_EXPERT, D), D_EXPERT ** -0.5),
        'final_g': 1.0 + nrm(ks[23], (D,), 0.02),
    }


def reference(x_prompt, x_sample, c_prompt, c_sample, state_ret, cache_swa_k, cache_swa_v, w_ada, b_ada, norm1_g, norm2_g, w_in, w_up_ret, w_up_swa, w_o, sink, w_rg, b_rg, w_re, b_re, w1, w3, w2, final_g):
    B, T, _ = x_prompt.shape
    DB, L, _ = x_sample.shape
    pos_p = jnp.arange(T, dtype=jnp.int32)
    pos_s = PAST_LEN + jnp.arange(L, dtype=jnp.int32)
    xp, xs = x_prompt, x_sample
    sp_list, ss_list, kp_list, vp_list, ks_list, vs_list = [], [], [], [], [], []
    for l in range(DEPTH):
        shared = (w_ada[l], b_ada[l], norm1_g[l], norm2_g[l], w_in[l], w_up_ret[l], w_up_swa[l], w_o[l], w_rg[l], b_rg[l], w_re[l], b_re[l], w1[l], w3[l], w2[l])
        s0 = jnp.zeros((B, RET_HEADS, RET_DK, RET_DV), jnp.float32)
        core_p = functools.partial(_swa_prompt, sink=sink[l])
        xp, st_p, kb_p, vb_p = _layer(xp, c_prompt, pos_p, s0, min(RET_CHUNK, T), core_p, True, *shared)
        core_s = functools.partial(_swa_sample, k_buf=cache_swa_k[l], v_buf=cache_swa_v[l], sink=sink[l])
        xs, st_s, kb_s, vb_s = _layer(xs, c_sample, pos_s, state_ret[l], L, core_s, False, *shared)
        sp_list.append(st_p)
        ss_list.append(st_s)
        kp_list.append(kb_p)
        vp_list.append(vb_p)
        ks_list.append(kb_s)
        vs_list.append(vb_s)
    y_prompt = _rmsnorm(xp, final_g)
    y_sample = _rmsnorm(xs, final_g)
    return (y_prompt, y_sample, jnp.stack(sp_list), jnp.stack(ss_list), jnp.stack(kp_list), jnp.stack(vp_list), jnp.stack(ks_list), jnp.stack(vs_list))
```

```python
import functools

import jax
import jax.numpy as jnp
from jax import lax
from jax.experimental import pallas as pl
from jax.experimental.pallas import tpu as pltpu

F32 = jnp.float32
BF16 = jnp.bfloat16
I32 = jnp.int32

PAST_LEN = 8192
RET_HEADS = 4
RET_DK = 128
RET_DV = 128
RET_CHUNK = 128
SWA_HEADS = 8
SWA_KV_HEADS = 2
SWA_HD = 64
WINDOW = 128
ROPE_THETA = 10000.0
N_GROUPS = 4
EXPERTS_PER_GROUP = 8
N_EXPERTS = N_GROUPS * EXPERTS_PER_GROUP
D_EXPERT = 256
NORM_EPS = 1e-6

LANES = 128
SUBLANES = 8
TM = 256
SLOTS = 2 * TM + 2 * LANES
VMEM_LIMIT = 56 * 1024 * 1024

_RET_W = RET_HEADS * RET_DK
_SWA_QW = SWA_HEADS * SWA_HD
_SWA_KW = SWA_KV_HEADS * SWA_HD


def _cparams(*sem):
    return pltpu.CompilerParams(dimension_semantics=sem, vmem_limit_bytes=VMEM_LIMIT)


def _sigmoid(x):
    return 1.0 / (1.0 + jnp.exp(-x))


def _silu(x):
    return x * _sigmoid(x)


def _bdot(a, b):
    return jnp.dot(a.astype(BF16), b.astype(BF16), preferred_element_type=F32)


def _bdot_nt(a, b):
    return lax.dot_general(a.astype(BF16), b.astype(BF16), (((1,), (1,)), ((), ())), preferred_element_type=F32)


def _mod_kernel(c_ref, w_ref, b_ref, o_ref):
    o_ref[...] = _bdot(_silu(c_ref[...]), w_ref[...]) + b_ref[...]


def _modulation(c_all, w_ada, b_ada):
    rows, d = c_all.shape
    n = w_ada.shape[1]
    return pl.pallas_call(
        _mod_kernel,
        out_shape=jax.ShapeDtypeStruct((rows, n), F32),
        grid=(n // d,),
        in_specs=[pl.BlockSpec((rows, d), lambda j: (0, 0)),
                  pl.BlockSpec((d, d), lambda j: (0, j)),
                  pl.BlockSpec((1, d), lambda j: (0, j))],
        out_specs=pl.BlockSpec((rows, d), lambda j: (0, j)),
        compiler_params=_cparams("arbitrary"),
    )(c_all, w_ada, b_ada.reshape(1, n))


def _rms(x, g):
    return x * lax.rsqrt(jnp.mean(x * x, axis=-1, keepdims=True) + NORM_EPS) * g


def _pair_rotate(z, cos, sin_signed):
    n = z.shape[-1]
    lane = lax.broadcasted_iota(I32, z.shape, 1)
    partner = jnp.where((lane & 1) == 0, pltpu.roll(z, n - 1, 1), pltpu.roll(z, 1, 1))
    reps = n // LANES
    cos = jnp.concatenate([cos] * reps, axis=1) if reps > 1 else cos
    sin_signed = jnp.concatenate([sin_signed] * reps, axis=1) if reps > 1 else sin_signed
    return z * cos + partner * sin_signed


def _half_rotate(z, cos, sin_signed):
    n = z.shape[-1]
    half = SWA_HD // 2
    lane = lax.broadcasted_iota(I32, z.shape, 1)
    partner = jnp.where((lane & (SWA_HD - 1)) < half, pltpu.roll(z, n - half, 1), pltpu.roll(z, half, 1))
    reps = n // LANES
    cos = jnp.concatenate([cos] * reps, axis=1) if reps > 1 else cos
    sin_signed = jnp.concatenate([sin_signed] * reps, axis=1) if reps > 1 else sin_signed
    return z * cos + partner * sin_signed


def _inproj_kernel(xp_ref, xs_ref, shp_ref, scp_ref, shs_ref, scs_ref, n1_ref, w_ref,
                   cr_ref, sr_ref, cw_ref, sw_ref,
                   rq_o, rk_o, rv_o, rg_o, sq_o, sk_o, sv_o, za_o, zb_o, *, npt):
    is_s = pl.program_id(0) >= npt
    x = jnp.where(is_s, xs_ref[...], xp_ref[...])
    sh = jnp.where(is_s, shs_ref[...], shp_ref[...])
    sc = jnp.where(is_s, scs_ref[...], scp_ref[...])
    h = (_rms(x, n1_ref[...]) * (1.0 + sc) + sh).astype(BF16)
    cr, sr, cw, sw = cr_ref[...], sr_ref[...], cw_ref[...], sw_ref[...]

    def seg(a, b):
        return jnp.dot(h, w_ref[:, a:b], preferred_element_type=F32)

    o = 0
    rq_o[...] = _pair_rotate(seg(o, o + _RET_W), cr, sr).astype(BF16)
    o += _RET_W
    rk_o[...] = (_pair_rotate(seg(o, o + _RET_W), cr, sr) * (RET_DK ** -0.5)).astype(BF16)
    o += _RET_W
    rv_o[...] = seg(o, o + _RET_W).astype(BF16)
    o += _RET_W
    rg_o[...] = _silu(seg(o, o + _RET_W)).astype(BF16)
    o += _RET_W
    sq_o[...] = (_half_rotate(seg(o, o + _SWA_QW), cw, sw) * (SWA_HD ** -0.5)).astype(BF16)
    o += _SWA_QW
    zkv = seg(o, o + 2 * _SWA_KW)
    sk_o[...] = _half_rotate(zkv[:, :_SWA_KW], cw, sw)
    sv_o[...] = zkv[:, _SWA_KW:]
    o += 2 * _SWA_KW
    d = x.shape[-1]
    za_o[...] = _sigmoid(seg(o, o + d)).astype(BF16)
    o += d
    zb_o[...] = _sigmoid(seg(o, o + d)).astype(BF16)


def _inproj(xp, xs_pad, modp, mods, n1, w_in_b, tabs, *, npt, tpb, nb):
    d = xp.shape[1]
    nt = npt + 1
    nrow = nt * TM
    n_in = w_in_b.shape[1]
    ptile = lambda i: (jnp.minimum(i, npt - 1), 0)
    pbatch = lambda col: (lambda i: (jnp.minimum(i // tpb, nb - 1), 0, col))
    tab_idx = lambda i: (jnp.where(i < npt, i % tpb, tpb), 0)
    out_cols = [(_RET_W, BF16)] * 4 + [(_SWA_QW, BF16), (_SWA_KW, F32), (_SWA_KW, F32), (d, BF16), (d, BF16)]
    return pl.pallas_call(
        functools.partial(_inproj_kernel, npt=npt),
        out_shape=[jax.ShapeDtypeStruct((nrow, c), t) for c, t in out_cols],
        grid=(nt,),
        in_specs=[pl.BlockSpec((TM, d), ptile),
                  pl.BlockSpec((TM, d), lambda i: (0, 0)),
                  pl.BlockSpec((None, 1, d), pbatch(0)),
                  pl.BlockSpec((None, 1, d), pbatch(1)),
                  pl.BlockSpec((TM, d), lambda i: (0, 0)),
                  pl.BlockSpec((TM, d), lambda i: (0, 1)),
                  pl.BlockSpec((1, d), lambda i: (0, 0)),
                  pl.BlockSpec((d, n_in), lambda i: (0, 0))]
                 + [pl.BlockSpec((TM, LANES), tab_idx)] * 4,
        out_specs=[pl.BlockSpec((TM, c), lambda i: (i, 0)) for c, _ in out_cols],
        compiler_params=_cparams("arbitrary"),
    )(xp, xs_pad, modp, modp, mods, mods, n1, w_in_b, *tabs)


def _ret_kernel(q_ref, k_ref, v_ref, g_ref, dm_ref, qd_ref, kd_ref, cd_ref, o_ref, st_ref, s_scr, *, nc):
    c = pl.program_id(1)

    @pl.when(c == 0)
    def _():
        s_scr[...] = jnp.zeros_like(s_scr)

    for h in range(RET_HEADS):
        sl = slice(h * RET_DK, (h + 1) * RET_DK)
        q, k, v = q_ref[:, sl], k_ref[:, sl], v_ref[:, sl]
        att = _bdot_nt(q, k) * dm_ref[h]
        s_prev = s_scr[h]
        o = _bdot(att, v) + _bdot(q.astype(F32) * qd_ref[h], s_prev)
        kd = (k.astype(F32) * kd_ref[h]).astype(BF16)
        kv = lax.dot_general(kd, v, (((0,), (0,)), ((), ())), preferred_element_type=F32)
        s_scr[h] = cd_ref[h] * s_prev + kv
        o = o * lax.rsqrt(jnp.mean(o * o, axis=-1, keepdims=True) + NORM_EPS)
        o_ref[:, sl] = (o * g_ref[:, sl].astype(F32)).astype(BF16)

    @pl.when(c == nc - 1)
    def _():
        st_ref[...] = s_scr[...]


def _ret_tables():
    ld = jnp.log(1.0 - 2.0 ** (-5.0 - jnp.arange(RET_HEADS, dtype=F32)))
    idx = jnp.arange(RET_CHUNK, dtype=F32)
    diff = idx[:, None] - idx[None, :]
    causal = diff >= 0
    dmask = jnp.where(causal[None], jnp.exp(ld[:, None, None] * jnp.where(causal, diff, 0.0)[None]), 0.0)
    k_dec = jnp.exp(ld[None, :] * (RET_CHUNK - 1.0 - idx)[:, None])
    q_dec = jnp.exp(ld[None, :] * (idx + 1.0)[:, None])
    chunk_decay = jnp.exp(ld * RET_CHUNK)
    bc = lambda t: jnp.broadcast_to(t.T[:, :, None], (RET_HEADS, RET_CHUNK, RET_DV))
    cd = jnp.broadcast_to(chunk_decay[:, None, None], (RET_HEADS, 1, RET_DV))
    return dmask, bc(q_dec), bc(k_dec), cd, jnp.exp(ld)


def _retention_prompt(rq, rk, rv, rg, tabs, *, nb, t):
    nc = t // RET_CHUNK
    dm, qd, kd, cd = tabs
    blk = lambda b, c: (b * nc + c, 0)
    full3 = lambda b, c: (0, 0, 0)
    return pl.pallas_call(
        functools.partial(_ret_kernel, nc=nc),
        out_shape=[jax.ShapeDtypeStruct((nb * t, _RET_W), BF16),
                   jax.ShapeDtypeStruct((nb, RET_HEADS, RET_DK, RET_DV), F32)],
        grid=(nb, nc),
        in_specs=[pl.BlockSpec((RET_CHUNK, _RET_W), blk)] * 4
                 + [pl.BlockSpec((RET_HEADS, RET_CHUNK, RET_DV), full3)] * 3
                 + [pl.BlockSpec((RET_HEADS, 1, RET_DV), full3)],
        out_specs=[pl.BlockSpec((RET_CHUNK, _RET_W), blk),
                   pl.BlockSpec((None, RET_HEADS, RET_DK, RET_DV), lambda b, c: (b, 0, 0, 0))],
        scratch_shapes=[pltpu.VMEM((RET_HEADS, RET_DK, RET_DV), F32)],
        compiler_params=_cparams("arbitrary", "arbitrary"),
    )(rq, rk, rv, rg, dm, qd, kd, cd)


def _ret_sample_kernel(gam_ref, q_ref, k_ref, v_ref, g_ref, s0_ref, o_ref, st_ref, *, sb):
    gamma = gam_ref[pl.program_id(1)]
    q = q_ref[...].astype(F32)
    k = k_ref[...].astype(F32)
    v = v_ref[...].astype(F32)
    rows = sb * RET_DK
    s2 = s0_ref[...].reshape(rows, RET_DV)
    col_b = lax.broadcasted_iota(I32, (sb, rows), 1) // RET_DK
    row_b = lax.broadcasted_iota(I32, (sb, rows), 0)
    qexp = jnp.where(col_b == row_b, jnp.concatenate([q * gamma] * sb, axis=1), 0.0)
    o = jnp.sum(q * k, axis=-1, keepdims=True) * v + _bdot(qexp, s2)
    o = o * lax.rsqrt(jnp.mean(o * o, axis=-1, keepdims=True) + NORM_EPS)
    o_ref[...] = (o * g_ref[...].astype(F32)).astype(BF16)
    rep = (lax.broadcasted_iota(I32, (rows, sb), 0) // RET_DK == lax.broadcasted_iota(I32, (rows, sb), 1))
    rep = rep.astype(BF16)
    krep = _bdot(rep, k)
    vrep = _bdot(rep, v)
    eye = (lax.broadcasted_iota(I32, (rows, RET_DK), 0) % RET_DK == lax.broadcasted_iota(I32, (rows, RET_DK), 1))
    kcol = jnp.sum(jnp.where(eye, krep, 0.0), axis=-1, keepdims=True)
    st_ref[...] = (gamma * s2 + kcol * vrep).reshape(sb, RET_DK, RET_DV)


def _retention_sample(rq, rk, rv, rg, s0, gamma, *, row0, ns):
    sb = min(64, ns)
    base = row0 // sb
    blk = lambda i, h: (base + i, h)
    sblk = lambda i, h: (i, h, 0, 0)
    return pl.pallas_call(
        functools.partial(_ret_sample_kernel, sb=sb),
        out_shape=[jax.ShapeDtypeStruct((ns, _RET_W), BF16),
                   jax.ShapeDtypeStruct(s0.shape, F32)],
        grid=(ns // sb, RET_HEADS),
        in_specs=[pl.BlockSpec(memory_space=pltpu.SMEM)]
                 + [pl.BlockSpec((sb, RET_DK), blk)] * 4
                 + [pl.BlockSpec((sb, None, RET_DK, RET_DV), sblk)],
        out_specs=[pl.BlockSpec((sb, RET_DV), lambda i, h: (i, h)),
                   pl.BlockSpec((sb, None, RET_DK, RET_DV), sblk)],
        compiler_params=_cparams("arbitrary", "arbitrary"),
    )(gamma, rq, rk, rv, rg, s0)


def _sink_softmax(s, mask, sink):
    s = jnp.where(mask, s, -jnp.inf)
    m = jnp.maximum(jnp.max(s, axis=-1, keepdims=True), sink)
    p = jnp.exp(s - m)
    return p / (jnp.sum(p, axis=-1, keepdims=True) + jnp.exp(sink - m))


def _split_kv_heads(x):
    lo = lax.broadcasted_iota(I32, x.shape, 1) < SWA_HD
    h0_lo = jnp.where(lo, x, 0.0)
    h1_hi = jnp.where(lo, 0.0, x)
    return ((h0_lo, pltpu.roll(h0_lo, SWA_HD, 1)), (pltpu.roll(h1_hi, SWA_HD, 1), h1_hi))


def _swa_kernel(sink_ref, q_ref, kc_ref, kp_ref, vc_ref, vp_ref, o_ref):
    n = pl.program_id(1)
    c = WINDOW
    kk = jnp.concatenate([kp_ref[...], kc_ref[...]], axis=0)
    vv = jnp.concatenate([vp_ref[...], vc_ref[...]], axis=0)
    qi = lax.broadcasted_iota(I32, (c, 2 * c), 0)
    ki = lax.broadcasted_iota(I32, (c, 2 * c), 1)
    mask = (ki > qi) & (ki <= qi + c) & ((ki >= c) | (n > 0))
    ks = _split_kv_heads(kk)
    vs = _split_kv_heads(vv)
    group = SWA_HEADS // SWA_KV_HEADS
    for j in range(SWA_HEADS // 2):
        kvh = (2 * j) // group
        q2 = q_ref[:, j * LANES:(j + 1) * LANES]
        acc = jnp.zeros((c, LANES), F32)
        for half in range(2):
            p = _sink_softmax(_bdot_nt(q2, ks[kvh][half]), mask, sink_ref[2 * j + half])
            acc = acc + _bdot(p, vs[kvh][half])
        o_ref[:, j * LANES:(j + 1) * LANES] = acc.astype(BF16)


def _swa_prompt(sq, sk, sv, sink, *, nb, t):
    nblk = t // WINDOW
    cur = lambda b, n: (b * nblk + n, 0)
    prev = lambda b, n: (b * nblk + jnp.maximum(n - 1, 0), 0)
    return pl.pallas_call(
        _swa_kernel,
        out_shape=jax.ShapeDtypeStruct((nb * t, _SWA_QW), BF16),
        grid=(nb, nblk),
        in_specs=[pl.BlockSpec(memory_space=pltpu.SMEM),
                  pl.BlockSpec((WINDOW, _SWA_QW), cur),
                  pl.BlockSpec((WINDOW, _SWA_KW), cur),
                  pl.BlockSpec((WINDOW, _SWA_KW), prev),
                  pl.BlockSpec((WINDOW, _SWA_KW), cur),
                  pl.BlockSpec((WINDOW, _SWA_KW), prev)],
        out_specs=pl.BlockSpec((WINDOW, _SWA_QW), cur),
        compiler_params=_cparams("arbitrary", "arbitrary"),
    )(sink, sq, sk, sk, sv, sv)


def _swa_sample_kernel(sink_ref, q_ref, kn_ref, vn_ref, kc_ref, vc_ref, o_ref, *, sb, w):
    kn, vn = kn_ref[...], vn_ref[...]
    pad = jnp.zeros((LANES - sb, _SWA_KW), F32)
    kall = jnp.concatenate([kc_ref[...].reshape(sb * w, _SWA_KW), kn, pad], axis=0)
    vall = jnp.concatenate([vc_ref[...].reshape(sb * w, _SWA_KW), vn, pad], axis=0)
    ncol = sb * w + LANES
    lo = lax.broadcasted_iota(I32, (sb, LANES), 1) < SWA_HD
    group = SWA_HEADS // SWA_KV_HEADS
    pieces = []
    for h in range(SWA_HEADS):
        slab = q_ref[:, (h // 2) * LANES:(h // 2 + 1) * LANES].astype(F32)
        mine = jnp.where(lo, slab, 0.0) if h % 2 == 0 else jnp.where(lo, 0.0, slab)
        pieces.append(mine if (h % 2) == (h // group) else pltpu.roll(mine, SWA_HD, 1))
    qrows = jnp.concatenate(pieces, axis=0)
    nrow = SWA_HEADS * sb
    s = _bdot_nt(qrows, kall)
    rb = lax.broadcasted_iota(I32, (nrow, ncol), 0) % sb
    ci = lax.broadcasted_iota(I32, (nrow, ncol), 1)
    in_cache = (ci < sb * w) & (ci // w == rb) & ((w - ci % w) < WINDOW)
    mask = in_cache | (ci == sb * w + rb)
    sink_col = jnp.concatenate([jnp.full((sb, 1), sink_ref[h], F32) for h in range(SWA_HEADS)], axis=0)
    p = _sink_softmax(s, mask, sink_col)
    o = _bdot(p, vall)
    for j in range(SWA_HEADS // 2):
        acc = jnp.zeros((sb, LANES), F32)
        for half in range(2):
            h = 2 * j + half
            oh = o[h * sb:(h + 1) * sb]
            own = jnp.where(lo, oh, 0.0) if h // group == 0 else jnp.where(lo, 0.0, oh)
            acc = acc + (own if (h // group) == half else pltpu.roll(own, SWA_HD, 1))
        o_ref[:, j * LANES:(j + 1) * LANES] = acc.astype(BF16)


def _swa_sample(sq, sk, sv, cache_k, cache_v, sink, *, row0, ns):
    sb = min(16, ns)
    w = cache_k.shape[1]
    base = row0 // sb
    blk = lambda i: (base + i, 0)
    cblk = lambda i: (i, 0, 0)
    return pl.pallas_call(
        functools.partial(_swa_sample_kernel, sb=sb, w=w),
        out_shape=jax.ShapeDtypeStruct((ns, _SWA_QW), BF16),
        grid=(ns // sb,),
        in_specs=[pl.BlockSpec(memory_space=pltpu.SMEM),
                  pl.BlockSpec((sb, _SWA_QW), blk),
                  pl.BlockSpec((sb, _SWA_KW), blk),
                  pl.BlockSpec((sb, _SWA_KW), blk),
                  pl.BlockSpec((sb, w, _SWA_KW), cblk),
                  pl.BlockSpec((sb, w, _SWA_KW), cblk)],
        out_specs=pl.BlockSpec((sb, _SWA_QW), lambda i: (i, 0)),
        compiler_params=_cparams("arbitrary"),
    )(sink, sq, sk, sv, cache_k, cache_v)


def _route(logits):
    lane = lax.broadcasted_iota(I32, logits.shape, 1)
    big = jnp.int32(1 << 20)
    neg = -jnp.inf

    def top(mask):
        v = jnp.max(jnp.where(mask, logits, neg), axis=-1, keepdims=True)
        i = jnp.min(jnp.where(mask & (logits == v), lane, big), axis=-1, keepdims=True)
        return v, i

    gmask = lane < N_GROUPS
    gmax, gsel = top(gmask)
    p_group = 1.0 / jnp.sum(jnp.where(gmask, jnp.exp(logits - gmax), 0.0), axis=-1, keepdims=True)
    first = N_GROUPS + gsel * EXPERTS_PER_GROUP
    emask = (lane >= first) & (lane < first + EXPERTS_PER_GROUP)
    v1, i1 = top(emask)
    v2, i2 = top(emask & (lane != i1))
    t = jnp.exp(v2 - v1)
    w1 = p_group / (1.0 + t)
    return i1 - N_GROUPS, i2 - N_GROUPS, w1, w1 * t


def _outproj_kernel(gt_ref, os_ref, sa_ref, sb_ref, xp_ref, xs_ref, g1p_ref, shp_ref, scp_ref,
                    g1s_ref, shs_ref, scs_ref, n2_ref, wur_ref, wus_ref, wo_ref, wr_ref, br_ref,
                    x1_o, h2_o, rt_o, *, npt):
    is_s = pl.program_id(0) >= npt
    x = jnp.where(is_s, xs_ref[...], xp_ref[...])
    g1 = jnp.where(is_s, g1s_ref[...], g1p_ref[...])
    sh = jnp.where(is_s, shs_ref[...], shp_ref[...])
    sc = jnp.where(is_s, scs_ref[...], scp_ref[...])
    y_ret = jnp.dot(gt_ref[...], wur_ref[...], preferred_element_type=F32)
    y_swa = jnp.dot(os_ref[...], wus_ref[...], preferred_element_type=F32)
    merged = sa_ref[...].astype(F32) * y_ret + sb_ref[...].astype(F32) * y_swa
    x1 = x + g1 * jnp.dot(merged.astype(BF16), wo_ref[...], preferred_element_type=F32)
    x1_o[...] = x1
    h2 = _rms(x1, n2_ref[...]) * (1.0 + sc) + sh
    h2_o[...] = h2.astype(BF16)
    logits = jnp.dot(h2, wr_ref[...], preferred_element_type=F32, precision=lax.Precision.HIGHEST) + br_ref[...]
    e1, e2, w1, w2 = _route(logits)
    lane = lax.broadcasted_iota(I32, logits.shape, 1)
    rt_o[...] = jnp.where(lane == 0, e1.astype(F32),
                          jnp.where(lane == 1, e2.astype(F32),
                                    jnp.where(lane == 2, w1, jnp.where(lane == 3, w2, 0.0))))


def _outproj(gated, oswa, siga, sigb, xp, xs_pad, modp, mods, n2, wur, wus, wo, wr, br, *, npt, tpb, nb):
    d = xp.shape[1]
    nt = npt + 1
    nrow = nt * TM
    row = lambda i: (i, 0)
    ptile = lambda i: (jnp.minimum(i, npt - 1), 0)
    pbatch = lambda col: (lambda i: (jnp.minimum(i // tpb, nb - 1), 0, col))
    scol = lambda col: (lambda i: (0, col))
    const = lambda i: (0, 0)
    return pl.pallas_call(
        functools.partial(_outproj_kernel, npt=npt),
        out_shape=[jax.ShapeDtypeStruct((nrow, d), F32),
                   jax.ShapeDtypeStruct((nrow, d), BF16),
                   jax.ShapeDtypeStruct((nrow, LANES), F32)],
        grid=(nt,),
        in_specs=[pl.BlockSpec((TM, _RET_W), row), pl.BlockSpec((TM, _SWA_QW), row),
                  pl.BlockSpec((TM, d), row), pl.BlockSpec((TM, d), row),
                  pl.BlockSpec((TM, d), ptile), pl.BlockSpec((TM, d), const),
                  pl.BlockSpec((None, 1, d), pbatch(2)), pl.BlockSpec((None, 1, d), pbatch(3)),
                  pl.BlockSpec((None, 1, d), pbatch(4)),
                  pl.BlockSpec((TM, d), scol(2)), pl.BlockSpec((TM, d), scol(3)), pl.BlockSpec((TM, d), scol(4)),
                  pl.BlockSpec((1, d), const),
                  pl.BlockSpec(wur.shape, const), pl.BlockSpec(wus.shape, const), pl.BlockSpec(wo.shape, const),
                  pl.BlockSpec(wr.shape, const), pl.BlockSpec((1, LANES), const)],
        out_specs=[pl.BlockSpec((TM, d), row), pl.BlockSpec((TM, d), row), pl.BlockSpec((TM, LANES), row)],
        compiler_params=_cparams("arbitrary"),
    )(gated, oswa, siga, sigb, xp, xs_pad, modp, modp, modp, mods, mods, mods, n2, wur, wus, wo, wr, br)


def _plan_kernel(rt_ref, lp_o, cnt_o, ls_o, gb_o, carry, *, npt, ns):
    i = pl.program_id(0)

    @pl.when(i == 0)
    def _():
        carry[...] = jnp.zeros_like(carry)

    rt = rt_ref[...]
    lane = lax.broadcasted_iota(I32, (TM, LANES), 1)
    row = lax.broadcasted_iota(I32, (TM, LANES), 0)
    valid = (i < npt) | (row < ns)
    e1 = rt[:, 0:1].astype(I32)
    e2 = rt[:, 1:2].astype(I32)
    oh1 = ((lane == e1) & valid).astype(F32)
    oh2 = ((lane == e2) & valid).astype(F32)
    oh = oh1 + oh2
    tri = (lax.broadcasted_iota(I32, (TM, TM), 0) > lax.broadcasted_iota(I32, (TM, TM), 1)).astype(BF16)
    before = _bdot(tri, oh)
    cnt = jnp.sum(oh, axis=0, keepdims=True)
    cnt8 = jnp.floor((cnt + (SUBLANES - 1)) * (1.0 / SUBLANES))
    upper = (lax.broadcasted_iota(I32, (LANES, LANES), 0) < lax.broadcasted_iota(I32, (LANES, LANES), 1))
    lstart = SUBLANES * _bdot(jnp.broadcast_to(cnt8, (SUBLANES, LANES)), upper.astype(BF16))[0:1]
    slot = lstart + before
    lp1 = jnp.sum(oh1 * slot, axis=-1, keepdims=True)
    lp2 = jnp.sum(oh2 * slot, axis=-1, keepdims=True)
    vcol = valid[:, 0:1]
    lp_o[...] = jnp.where(lane == 0, jnp.where(vcol, lp1, -1.0), jnp.where(lane == 1, jnp.where(vcol, lp2, -1.0), 0.0))
    cnt_o[...] = (SUBLANES * cnt8).astype(I32)
    ls_o[...] = lstart.astype(I32)
    gb_o[...] = carry[...].astype(I32)
    carry[...] = carry[...] + SUBLANES * cnt8


def _plan(rt, *, npt, ns):
    nt = npt + 1
    meta = jax.ShapeDtypeStruct((nt, 1, LANES), I32)
    mspec = pl.BlockSpec((None, 1, LANES), lambda i: (i, 0, 0))
    return pl.pallas_call(
        functools.partial(_plan_kernel, npt=npt, ns=ns),
        out_shape=[jax.ShapeDtypeStruct((nt * TM, LANES), F32), meta, meta, meta],
        grid=(nt,),
        in_specs=[pl.BlockSpec((TM, LANES), lambda i: (i, 0))],
        out_specs=[pl.BlockSpec((TM, LANES), lambda i: (i, 0)), mspec, mspec, mspec],
        scratch_shapes=[pltpu.VMEM((1, LANES), F32)],
        compiler_params=_cparams("arbitrary"),
    )(rt)


def _aligned(v):
    return v if isinstance(v, int) else pl.multiple_of(v, SUBLANES)


def _run_copy(src, dst, s_start, d_start, n, sem):
    s_start, d_start, n = _aligned(s_start), _aligned(d_start), _aligned(n)
    return pltpu.make_async_copy(src.at[pl.ds(s_start, n)], dst.at[pl.ds(d_start, n)], sem)


def _dispatch_kernel(cnt_ref, ls_ref, gd_ref, ps_ref, pn_ref, nu_ref, h_ref, lp_ref, xs_ref,
                     sorted_scr, zero_scr, sem, zsem, *, nt, maxt):
    i = pl.program_id(0)

    def pad(e):
        return _run_copy(zero_scr, xs_ref, 0, ps_ref[e], pn_ref[e], zsem)

    def tail(j):
        return _run_copy(zero_scr, xs_ref, 0, j * TM, TM, zsem)

    def each_pad(fn):
        def body(e, c):
            @pl.when(pn_ref[e] > 0)
            def _():
                fn(pad(e))
            return c
        lax.fori_loop(0, N_EXPERTS, body, 0)

    def each_tail(fn):
        def body(j, c):
            fn(tail(j))
            return c
        lax.fori_loop(nu_ref[0], maxt, body, 0)

    @pl.when(i == 0)
    def _():
        zero_scr[...] = jnp.zeros_like(zero_scr)
        each_pad(lambda cp: cp.start())
        each_tail(lambda cp: cp.start())

    lpt = lp_ref[...].T
    slot = lax.broadcasted_iota(I32, (SLOTS, TM), 0).astype(F32)
    perm = ((slot == lpt[0:1]) | (slot == lpt[1:2])).astype(BF16)
    sorted_scr[...] = jnp.dot(perm, h_ref[...], preferred_element_type=F32)

    def copy(e):
        return _run_copy(sorted_scr, xs_ref, ls_ref[i * N_EXPERTS + e], gd_ref[i * N_EXPERTS + e],
                         cnt_ref[i * N_EXPERTS + e], sem)

    def start(e, c):
        @pl.when(cnt_ref[i * N_EXPERTS + e] > 0)
        def _():
            copy(e).start()
        return c

    def wait(e, c):
        @pl.when(cnt_ref[i * N_EXPERTS + e] > 0)
        def _():
            copy(e).wait()
        return c

    lax.fori_loop(0, N_EXPERTS, start, 0)
    lax.fori_loop(0, N_EXPERTS, wait, 0)

    @pl.when(i == nt - 1)
    def _():
        each_pad(lambda cp: cp.wait())
        each_tail(lambda cp: cp.wait())


def _dispatch(cnt, ls, gd, ps, pn, nu, h2, lp, *, nt, maxt):
    d = h2.shape[1]
    return pl.pallas_call(
        functools.partial(_dispatch_kernel, nt=nt, maxt=maxt),
        out_shape=jax.ShapeDtypeStruct((maxt * TM, d), F32),
        grid_spec=pltpu.PrefetchScalarGridSpec(
            num_scalar_prefetch=6,
            grid=(nt,),
            in_specs=[pl.BlockSpec((TM, d), lambda i, *_: (i, 0)),
                      pl.BlockSpec((TM, LANES), lambda i, *_: (i, 0))],
            out_specs=pl.BlockSpec(memory_space=pl.ANY),
            scratch_shapes=[pltpu.VMEM((SLOTS, d), F32), pltpu.VMEM((TM, d), F32),
                            pltpu.SemaphoreType.DMA(()), pltpu.SemaphoreType.DMA(())]),
        compiler_params=_cparams("arbitrary"),
    )(cnt, ls, gd, ps, pn, nu, h2, lp)


def _experts_kernel(te_ref, nu_ref, x_ref, w1_ref, w3_ref, w2_ref, y_ref, w1b, w3b, w2b):
    j = pl.program_id(0)
    changed = (j == 0) | (te_ref[j] != te_ref[jnp.maximum(j - 1, 0)])

    @pl.when(changed)
    def _():
        w1b[...] = w1_ref[...].astype(BF16)
        w3b[...] = w3_ref[...].astype(BF16)
        w2b[...] = w2_ref[...].astype(BF16)

    @pl.when(j < nu_ref[0])
    def _():
        x = x_ref[...].astype(BF16)
        a = jnp.dot(x, w1b[...], preferred_element_type=F32)
        b = jnp.dot(x, w3b[...], preferred_element_type=F32)
        y_ref[...] = jnp.dot((_silu(a) * b).astype(BF16), w2b[...], preferred_element_type=F32)

    @pl.when(j >= nu_ref[0])
    def _():
        y_ref[...] = jnp.zeros_like(y_ref)


def _experts(te, nu, xs, w1, w3, w2, *, maxt):
    d = xs.shape[1]
    f = w1.shape[2]
    wsel = lambda j, te, nu: (te[j], 0, 0)
    return pl.pallas_call(
        _experts_kernel,
        out_shape=jax.ShapeDtypeStruct(xs.shape, F32),
        grid_spec=pltpu.PrefetchScalarGridSpec(
            num_scalar_prefetch=2,
            grid=(maxt,),
            in_specs=[pl.BlockSpec((TM, d), lambda j, te, nu: (jnp.minimum(j, nu[0] - 1), 0)),
                      pl.BlockSpec((None, d, f), wsel),
                      pl.BlockSpec((None, d, f), wsel),
                      pl.BlockSpec((None, f, d), wsel)],
            out_specs=pl.BlockSpec((TM, d), lambda j, te, nu: (j, 0)),
            scratch_shapes=[pltpu.VMEM((d, f), BF16), pltpu.VMEM((d, f), BF16), pltpu.VMEM((f, d), BF16)]),
        compiler_params=_cparams("arbitrary"),
    )(te, nu, xs, w1, w3, w2)


def _combine_kernel(cnt_ref, ls_ref, gd_ref, ys_ref, lp_ref, rt_ref, x1_ref, g2p_ref, g2s_ref, fg_ref,
                    y_o, ybuf, sem, *, npt):
    i = pl.program_id(0)
    ybuf[...] = jnp.zeros_like(ybuf)

    def copy(e):
        return _run_copy(ys_ref, ybuf, gd_ref[i * N_EXPERTS + e], ls_ref[i * N_EXPERTS + e],
                         cnt_ref[i * N_EXPERTS + e], sem)

    def start(e, c):
        @pl.when(cnt_ref[i * N_EXPERTS + e] > 0)
        def _():
            copy(e).start()
        return c

    def wait(e, c):
        @pl.when(cnt_ref[i * N_EXPERTS + e] > 0)
        def _():
            copy(e).wait()
        return c

    lax.fori_loop(0, N_EXPERTS, start, 0)
    lax.fori_loop(0, N_EXPERTS, wait, 0)

    yb = ybuf[...].astype(BF16)
    slot = lax.broadcasted_iota(I32, (TM, SLOTS), 1).astype(F32)
    lp = lp_ref[...]
    rt = rt_ref[...]
    m1 = jnp.dot((slot == lp[:, 0:1]).astype(BF16), yb, preferred_element_type=F32)
    m2 = jnp.dot((slot == lp[:, 1:2]).astype(BF16), yb, preferred_element_type=F32)
    moe = rt[:, 2:3] * m1 + rt[:, 3:4] * m2
    g2 = jnp.where(i >= npt, g2s_ref[...], g2p_ref[...])
    y_o[...] = _rms(x1_ref[...] + g2 * moe, fg_ref[...])


def _combine(cnt, ls, gd, ys, lp, rt, x1, modp, mods, fg, *, npt, tpb, nb):
    d = x1.shape[1]
    nt = npt + 1
    row = lambda i, *_: (i, 0)
    return pl.pallas_call(
        functools.partial(_combine_kernel, npt=npt),
        out_shape=jax.ShapeDtypeStruct((nt * TM, d), F32),
        grid_spec=pltpu.PrefetchScalarGridSpec(
            num_scalar_prefetch=3,
            grid=(nt,),
            in_specs=[pl.BlockSpec(memory_space=pl.ANY),
                      pl.BlockSpec((TM, LANES), row), pl.BlockSpec((TM, LANES), row), pl.BlockSpec((TM, d), row),
                      pl.BlockSpec((None, 1, d), lambda i, *_: (jnp.minimum(i // tpb, nb - 1), 0, 5)),
                      pl.BlockSpec((TM, d), lambda i, *_: (0, 5)),
                      pl.BlockSpec((1, d), lambda i, *_: (0, 0))],
            out_specs=pl.BlockSpec((TM, d), row),
            scratch_shapes=[pltpu.VMEM((SLOTS, d), F32), pltpu.SemaphoreType.DMA(())]),
        compiler_params=_cparams("arbitrary"),
    )(cnt, ls, gd, ys, lp, rt, x1, modp, mods, fg)


def _rotation_tables(t):
    pos = jnp.concatenate([jnp.arange(t, dtype=I32), jnp.full((TM,), PAST_LEN, I32)]).astype(F32)
    inv_r = 1.0 / (ROPE_THETA ** jnp.linspace(0.0, 1.0, RET_DK // 2, dtype=F32))
    ang_r = pos[:, None] * inv_r[None, :]
    sign_r = jnp.where(jnp.arange(RET_DK) % 2 == 0, -1.0, 1.0).astype(F32)
    cr = jnp.repeat(jnp.cos(ang_r), 2, axis=1)
    sr = jnp.repeat(jnp.sin(ang_r), 2, axis=1) * sign_r[None, :]
    inv_w = ROPE_THETA ** (-jnp.arange(0, SWA_HD, 2, dtype=F32) / SWA_HD)
    ang_w = pos[:, None] * inv_w[None, :]
    reps = LANES // (SWA_HD // 2)
    sign_w = jnp.where(jnp.arange(LANES) % SWA_HD < SWA_HD // 2, -1.0, 1.0).astype(F32)
    cw = jnp.tile(jnp.cos(ang_w), (1, reps))
    sw = jnp.tile(jnp.sin(ang_w), (1, reps)) * sign_w[None, :]
    return cr, sr, cw, sw


def kernel(x_prompt, x_sample, c_prompt, c_sample, state_ret, cache_swa_k, cache_swa_v, w_ada, b_ada, norm1_g, norm2_g, w_in, w_up_ret, w_up_swa, w_o, sink, w_rg, b_rg, w_re, b_re, w1, w3, w2, final_g):
    nb, t, d = x_prompt.shape
    ns, dec_seq, _ = x_sample.shape
    depth = w_ada.shape[0]
    assert depth == 1 and dec_seq == 1, "single layer, one new token per sequence"
    assert t % TM == 0 and ns <= TM and ns % 16 == 0 and d % LANES == 0
    assert N_GROUPS + N_EXPERTS <= LANES
    w = cache_swa_k.shape[2]
    tpb = t // TM
    npt = nb * tpb
    nt = npt + 1
    np_rows = nb * t
    n_tok = np_rows + ns
    maxt = -(-(2 * n_tok + nt * N_EXPERTS * (SUBLANES - 1) + N_EXPERTS * (TM - 1)) // TM)

    xp = x_prompt.reshape(np_rows, d)
    xs_pad = jnp.pad(x_sample.reshape(ns, d), ((0, TM - ns), (0, 0)))

    c_all = jnp.concatenate([jnp.pad(c_prompt, ((0, SUBLANES - nb % SUBLANES), (0, 0))),
                             jnp.pad(c_sample, ((0, TM - ns), (0, 0)))])
    mod = _modulation(c_all, w_ada[0], b_ada[0])
    modp = mod[:nb].reshape(nb, 1, 6 * d)
    mods = mod[c_all.shape[0] - TM:]

    tabs = _rotation_tables(t)
    rq, rk, rv, rg, sq, sk, sv, siga, sigb = _inproj(
        xp, xs_pad, modp, mods, norm1_g, w_in[0].astype(BF16), tabs, npt=npt, tpb=tpb, nb=nb)

    dm, qd, kd, cd, gamma = _ret_tables()
    gated_p, st_p = _retention_prompt(rq, rk, rv, rg, (dm, qd, kd, cd), nb=nb, t=t)
    gated_s, st_s = _retention_sample(rq, rk, rv, rg, state_ret[0], gamma, row0=np_rows, ns=ns)
    oswa_p = _swa_prompt(sq, sk, sv, sink[0], nb=nb, t=t)
    ck = cache_swa_k[0].reshape(ns, w, _SWA_KW)
    cv = cache_swa_v[0].reshape(ns, w, _SWA_KW)
    oswa_s = _swa_sample(sq, sk, sv, ck, cv, sink[0], row0=np_rows, ns=ns)
    tail = jnp.zeros((TM - ns, _RET_W), BF16)
    gated = jnp.concatenate([gated_p, gated_s, tail])
    oswa = jnp.concatenate([oswa_p, oswa_s, tail])

    wr = jnp.pad(jnp.concatenate([w_rg[0], w_re[0]], axis=1), ((0, 0), (0, LANES - N_GROUPS - N_EXPERTS)))
    br = jnp.pad(jnp.concatenate([b_rg[0], b_re[0]]), (0, LANES - N_GROUPS - N_EXPERTS)).reshape(1, LANES)
    x1, h2, rt = _outproj(gated, oswa, siga, sigb, xp, xs_pad, modp, mods, norm2_g,
                          w_up_ret[0].astype(BF16), w_up_swa[0].astype(BF16), w_o[0].astype(BF16), wr, br,
                          npt=npt, tpb=tpb, nb=nb)

    lp, cnt, ls, gb = _plan(rt, npt=npt, ns=ns)
    cnt = cnt[:, 0, :N_EXPERTS]
    ls = ls[:, 0, :N_EXPERTS]
    gb = gb[:, 0, :N_EXPERTS]
    seg = jnp.sum(cnt, axis=0)
    tiles = (seg + TM - 1) // TM
    tile_end = jnp.cumsum(tiles)
    row_start = (tile_end - tiles) * TM
    gd = (gb + row_start[None, :]).reshape(-1)
    n_used = tile_end[-1:]
    te = jnp.minimum(jnp.searchsorted(tile_end, jnp.minimum(jnp.arange(maxt, dtype=I32), n_used[0] - 1),
                                      side="right"), N_EXPERTS - 1).astype(I32)
    cnt = cnt.reshape(-1)
    ls = ls.reshape(-1)
    n_used = n_used.astype(I32)
    xs = _dispatch(cnt, ls, gd, row_start + seg, tiles * TM - seg, n_used, h2, lp, nt=nt, maxt=maxt)
    ys = _experts(te, n_used, xs, w1[0], w3[0], w2[0], maxt=maxt)
    y_all = _combine(cnt, ls, gd, ys, lp, rt, x1, modp, mods, final_g.reshape(1, d), npt=npt, tpb=tpb, nb=nb)

    y_prompt = y_all[:np_rows].reshape(nb, t, d)
    y_sample = y_all[np_rows:n_tok].reshape(ns, 1, d)
    wk = min(WINDOW, t)
    skp = sk[:np_rows].reshape(nb, t, SWA_KV_HEADS, SWA_HD)[:, t - wk:]
    svp = sv[:np_rows].reshape(nb, t, SWA_KV_HEADS, SWA_HD)[:, t - wk:]
    shp = (ns, 1, SWA_KV_HEADS, SWA_HD)
    ks_new = jnp.concatenate([cache_swa_k[0], sk[np_rows:n_tok].reshape(shp)], axis=1)[:, 1:]
    vs_new = jnp.concatenate([cache_swa_v[0], sv[np_rows:n_tok].reshape(shp)], axis=1)[:, 1:]
    return (y_prompt, y_sample, st_p[None], st_s[None], skp[None], svp[None], ks_new[None], vs_new[None])
```

```python
import functools

import jax
import jax.numpy as jnp
from jax import lax
from jax.experimental import pallas as pl
from jax.experimental.pallas import tpu as pltpu

F32 = jnp.float32
BF16 = jnp.bfloat16
I32 = jnp.int32

PAST_LEN = 8192
RET_HEADS = 4
RET_DK = 128
RET_DV = 128
RET_CHUNK = 128
SWA_HEADS = 8
SWA_KV_HEADS = 2
SWA_HD = 64
WINDOW = 128
ROPE_THETA = 10000.0
N_GROUPS = 4
EXPERTS_PER_GROUP = 8
N_EXPERTS = N_GROUPS * EXPERTS_PER_GROUP
D_EXPERT = 256
NORM_EPS = 1e-6

LANES = 128
SUBLANES = 8
TM = 256
SLOTS = 2 * TM + 2 * LANES
VMEM_LIMIT = 56 * 1024 * 1024

_RET_W = RET_HEADS * RET_DK
_SWA_QW = SWA_HEADS * SWA_HD
_SWA_KW = SWA_KV_HEADS * SWA_HD


def _cparams(*sem):
    return pltpu.CompilerParams(dimension_semantics=sem, vmem_limit_bytes=VMEM_LIMIT)


def _sigmoid(x):
    return 1.0 / (1.0 + jnp.exp(-x))


def _silu(x):
    return x * _sigmoid(x)


def _bdot(a, b):
    return jnp.dot(a.astype(BF16), b.astype(BF16), preferred_element_type=F32)


def _bdot_nt(a, b):
    return lax.dot_general(a.astype(BF16), b.astype(BF16), (((1,), (1,)), ((), ())), preferred_element_type=F32)


def _mod_kernel(c_ref, w_ref, b_ref, o_ref):
    o_ref[...] = _bdot(_silu(c_ref[...]), w_ref[...]) + b_ref[...]


def _modulation(c_all, w_ada, b_ada):
    rows, d = c_all.shape
    n = w_ada.shape[1]
    return pl.pallas_call(
        _mod_kernel,
        out_shape=jax.ShapeDtypeStruct((rows, n), F32),
        grid=(n // d,),
        in_specs=[pl.BlockSpec((rows, d), lambda j: (0, 0)),
                  pl.BlockSpec((d, d), lambda j: (0, j)),
                  pl.BlockSpec((1, d), lambda j: (0, j))],
        out_specs=pl.BlockSpec((rows, d), lambda j: (0, j)),
        compiler_params=_cparams("arbitrary"),
    )(c_all, w_ada, b_ada.reshape(1, n))


def _rms(x, g):
    return x * lax.rsqrt(jnp.mean(x * x, axis=-1, keepdims=True) + NORM_EPS) * g


def _pair_rotate(z, cos, sin_signed):
    n = z.shape[-1]
    lane = lax.broadcasted_iota(I32, z.shape, 1)
    partner = jnp.where((lane & 1) == 0, pltpu.roll(z, n - 1, 1), pltpu.roll(z, 1, 1))
    reps = n // LANES
    cos = jnp.concatenate([cos] * reps, axis=1) if reps > 1 else cos
    sin_signed = jnp.concatenate([sin_signed] * reps, axis=1) if reps > 1 else sin_signed
    return z * cos + partner * sin_signed


def _half_rotate(z, cos, sin_signed):
    n = z.shape[-1]
    half = SWA_HD // 2
    lane = lax.broadcasted_iota(I32, z.shape, 1)
    partner = jnp.where((lane & (SWA_HD - 1)) < half, pltpu.roll(z, n - half, 1), pltpu.roll(z, half, 1))
    reps = n // LANES
    cos = jnp.concatenate([cos] * reps, axis=1) if reps > 1 else cos
    sin_signed = jnp.concatenate([sin_signed] * reps, axis=1) if reps > 1 else sin_signed
    return z * cos + partner * sin_signed


def _inproj_kernel(xp_ref, xs_ref, shp_ref, scp_ref, shs_ref, scs_ref, n1_ref, w_ref,
                   cr_ref, sr_ref, cw_ref, sw_ref,
                   rq_o, rk_o, rv_o, rg_o, sq_o, sk_o, sv_o, za_o, zb_o, *, npt):
    is_s = pl.program_id(0) >= npt
    x = jnp.where(is_s, xs_ref[...], xp_ref[...])
    sh = jnp.where(is_s, shs_ref[...], shp_ref[...])
    sc = jnp.where(is_s, scs_ref[...], scp_ref[...])
    h = (_rms(x, n1_ref[...]) * (1.0 + sc) + sh).astype(BF16)
    cr, sr, cw, sw = cr_ref[...], sr_ref[...], cw_ref[...], sw_ref[...]

    def seg(a, b):
        return jnp.dot(h, w_ref[:, a:b], preferred_element_type=F32)

    o = 0
    rq_o[...] = _pair_rotate(seg(o, o + _RET_W), cr, sr).astype(BF16)
    o += _RET_W
    rk_o[...] = (_pair_rotate(seg(o, o + _RET_W), cr, sr) * (RET_DK ** -0.5)).astype(BF16)
    o += _RET_W
    rv_o[...] = seg(o, o + _RET_W).astype(BF16)
    o += _RET_W
    rg_o[...] = _silu(seg(o, o + _RET_W)).astype(BF16)
    o += _RET_W
    sq_o[...] = (_half_rotate(seg(o, o + _SWA_QW), cw, sw) * (SWA_HD ** -0.5)).astype(BF16)
    o += _SWA_QW
    zkv = seg(o, o + 2 * _SWA_KW)
    sk_o[...] = _half_rotate(zkv[:, :_SWA_KW], cw, sw)
    sv_o[...] = zkv[:, _SWA_KW:]
    o += 2 * _SWA_KW
    d = x.shape[-1]
    za_o[...] = _sigmoid(seg(o, o + d)).astype(BF16)
    o += d
    zb_o[...] = _sigmoid(seg(o, o + d)).astype(BF16)


def _inproj(xp, xs_pad, modp, mods, n1, w_in_b, tabs, *, npt, tpb, nb):
    d = xp.shape[1]
    nt = npt + 1
    nrow = nt * TM
    n_in = w_in_b.shape[1]
    ptile = lambda i: (jnp.minimum(i, npt - 1), 0)
    pbatch = lambda col: (lambda i: (jnp.minimum(i // tpb, nb - 1), 0, col))
    tab_idx = lambda i: (jnp.where(i < npt, i % tpb, tpb), 0)
    out_cols = [(_RET_W, BF16)] * 4 + [(_SWA_QW, BF16), (_SWA_KW, F32), (_SWA_KW, F32), (d, BF16), (d, BF16)]
    return pl.pallas_call(
        functools.partial(_inproj_kernel, npt=npt),
        out_shape=[jax.ShapeDtypeStruct((nrow, c), t) for c, t in out_cols],
        grid=(nt,),
        in_specs=[pl.BlockSpec((TM, d), ptile),
                  pl.BlockSpec((TM, d), lambda i: (0, 0)),
                  pl.BlockSpec((None, 1, d), pbatch(0)),
                  pl.BlockSpec((None, 1, d), pbatch(1)),
                  pl.BlockSpec((TM, d), lambda i: (0, 0)),
                  pl.BlockSpec((TM, d), lambda i: (0, 1)),
                  pl.BlockSpec((1, d), lambda i: (0, 0)),
                  pl.BlockSpec((d, n_in), lambda i: (0, 0))]
                 + [pl.BlockSpec((TM, LANES), tab_idx)] * 4,
        out_specs=[pl.BlockSpec((TM, c), lambda i: (i, 0)) for c, _ in out_cols],
        compiler_params=_cparams("arbitrary"),
    )(xp, xs_pad, modp, modp, mods, mods, n1, w_in_b, *tabs)


def _ret_kernel(q_ref, k_ref, v_ref, g_ref, dm_ref, qd_ref, kd_ref, cd_ref, o_ref, st_ref, s_scr, *, nc):
    c = pl.program_id(1)

    @pl.when(c == 0)
    def _():
        s_scr[...] = jnp.zeros_like(s_scr)

    for h in range(RET_HEADS):
        sl = slice(h * RET_DK, (h + 1) * RET_DK)
        q, k, v = q_ref[:, sl], k_ref[:, sl], v_ref[:, sl]
        att = _bdot_nt(q, k) * dm_ref[h]
        s_prev = s_scr[h]
        o = _bdot(att, v) + _bdot(q.astype(F32) * qd_ref[h], s_prev)
        kd = (k.astype(F32) * kd_ref[h]).astype(BF16)
        kv = lax.dot_general(kd, v, (((0,), (0,)), ((), ())), preferred_element_type=F32)
        s_scr[h] = cd_ref[h] * s_prev + kv
        o = o * lax.rsqrt(jnp.mean(o * o, axis=-1, keepdims=True) + NORM_EPS)
        o_ref[:, sl] = (o * g_ref[:, sl].astype(F32)).astype(BF16)

    @pl.when(c == nc - 1)
    def _():
        st_ref[...] = s_scr[...]


def _ret_tables():
    ld = jnp.log(1.0 - 2.0 ** (-5.0 - jnp.arange(RET_HEADS, dtype=F32)))
    idx = jnp.arange(RET_CHUNK, dtype=F32)
    diff = idx[:, None] - idx[None, :]
    causal = diff >= 0
    dmask = jnp.where(causal[None], jnp.exp(ld[:, None, None] * jnp.where(causal, diff, 0.0)[None]), 0.0)
    k_dec = jnp.exp(ld[None, :] * (RET_CHUNK - 1.0 - idx)[:, None])
    q_dec = jnp.exp(ld[None, :] * (idx + 1.0)[:, None])
    chunk_decay = jnp.exp(ld * RET_CHUNK)
    bc = lambda t: jnp.broadcast_to(t.T[:, :, None], (RET_HEADS, RET_CHUNK, RET_DV))
    cd = jnp.broadcast_to(chunk_decay[:, None, None], (RET_HEADS, 1, RET_DV))
    return dmask, bc(q_dec), bc(k_dec), cd, jnp.exp(ld)


def _retention_prompt(rq, rk, rv, rg, tabs, *, nb, t):
    nc = t // RET_CHUNK
    dm, qd, kd, cd = tabs
    blk = lambda b, c: (b * nc + c, 0)
    full3 = lambda b, c: (0, 0, 0)
    return pl.pallas_call(
        functools.partial(_ret_kernel, nc=nc),
        out_shape=[jax.ShapeDtypeStruct((nb * t, _RET_W), BF16),
                   jax.ShapeDtypeStruct((nb, RET_HEADS, RET_DK, RET_DV), F32)],
        grid=(nb, nc),
        in_specs=[pl.BlockSpec((RET_CHUNK, _RET_W), blk)] * 4
                 + [pl.BlockSpec((RET_HEADS, RET_CHUNK, RET_DV), full3)] * 3
                 + [pl.BlockSpec((RET_HEADS, 1, RET_DV), full3)],
        out_specs=[pl.BlockSpec((RET_CHUNK, _RET_W), blk),
                   pl.BlockSpec((None, RET_HEADS, RET_DK, RET_DV), lambda b, c: (b, 0, 0, 0))],
        scratch_shapes=[pltpu.VMEM((RET_HEADS, RET_DK, RET_DV), F32)],
        compiler_params=_cparams("arbitrary", "arbitrary"),
    )(rq, rk, rv, rg, dm, qd, kd, cd)


def _ret_sample_kernel(gam_ref, q_ref, k_ref, v_ref, g_ref, s0_ref, o_ref, st_ref, *, sb):
    gamma = gam_ref[pl.program_id(1)]
    q = q_ref[...].astype(F32)
    k = k_ref[...].astype(F32)
    v = v_ref[...].astype(F32)
    rows = sb * RET_DK
    s2 = s0_ref[...].reshape(rows, RET_DV)
    col_b = lax.broadcasted_iota(I32, (sb, rows), 1) // RET_DK
    row_b = lax.broadcasted_iota(I32, (sb, rows), 0)
    qexp = jnp.where(col_b == row_b, jnp.concatenate([q * gamma] * sb, axis=1), 0.0)
    o = jnp.sum(q * k, axis=-1, keepdims=True) * v + _bdot(qexp, s2)
    o = o * lax.rsqrt(jnp.mean(o * o, axis=-1, keepdims=True) + NORM_EPS)
    o_ref[...] = (o * g_ref[...].astype(F32)).astype(BF16)
    rep = (lax.broadcasted_iota(I32, (rows, sb), 0) // RET_DK == lax.broadcasted_iota(I32, (rows, sb), 1))
    rep = rep.astype(BF16)
    krep = _bdot(rep, k)
    vrep = _bdot(rep, v)
    eye = (lax.broadcasted_iota(I32, (rows, RET_DK), 0) % RET_DK == lax.broadcasted_iota(I32, (rows, RET_DK), 1))
    kcol = jnp.sum(jnp.where(eye, krep, 0.0), axis=-1, keepdims=True)
    st_ref[...] = (gamma * s2 + kcol * vrep).reshape(sb, RET_DK, RET_DV)


def _retention_sample(rq, rk, rv, rg, s0, gamma, *, row0, ns):
    sb = min(64, ns)
    base = row0 // sb
    blk = lambda i, h: (base + i, h)
    sblk = lambda i, h: (i, h, 0, 0)
    return pl.pallas_call(
        functools.partial(_ret_sample_kernel, sb=sb),
        out_shape=[jax.ShapeDtypeStruct((ns, _RET_W), BF16),
                   jax.ShapeDtypeStruct(s0.shape, F32)],
        grid=(ns // sb, RET_HEADS),
        in_specs=[pl.BlockSpec(memory_space=pltpu.SMEM)]
                 + [pl.BlockSpec((sb, RET_DK), blk)] * 4
                 + [pl.BlockSpec((sb, None, RET_DK, RET_DV), sblk)],
        out_specs=[pl.BlockSpec((sb, RET_DV), lambda i, h: (i, h)),
                   pl.BlockSpec((sb, None, RET_DK, RET_DV), sblk)],
        compiler_params=_cparams("arbitrary", "arbitrary"),
    )(gamma, rq, rk, rv, rg, s0)


def _sink_softmax(s, mask, sink):
    s = jnp.where(mask, s, -jnp.inf)
    m = jnp.maximum(jnp.max(s, axis=-1, keepdims=True), sink)
    p = jnp.exp(s - m)
    return p / (jnp.sum(p, axis=-1, keepdims=True) + jnp.exp(sink - m))


def _split_kv_heads(x):
    lo = lax.broadcasted_iota(I32, x.shape, 1) < SWA_HD
    h0_lo = jnp.where(lo, x, 0.0)
    h1_hi = jnp.where(lo, 0.0, x)
    return ((h0_lo, pltpu.roll(h0_lo, SWA_HD, 1)), (pltpu.roll(h1_hi, SWA_HD, 1), h1_hi))


def _swa_kernel(sink_ref, q_ref, kc_ref, kp_ref, vc_ref, vp_ref, o_ref):
    n = pl.program_id(1)
    c = WINDOW
    kk = jnp.concatenate([kp_ref[...], kc_ref[...]], axis=0)
    vv = jnp.concatenate([vp_ref[...], vc_ref[...]], axis=0)
    qi = lax.broadcasted_iota(I32, (c, 2 * c), 0)
    ki = lax.broadcasted_iota(I32, (c, 2 * c), 1)
    mask = (ki > qi) & (ki <= qi + c) & ((ki >= c) | (n > 0))
    ks = _split_kv_heads(kk)
    vs = _split_kv_heads(vv)
    group = SWA_HEADS // SWA_KV_HEADS
    for j in range(SWA_HEADS // 2):
        kvh = (2 * j) // group
        q2 = q_ref[:, j * LANES:(j + 1) * LANES]
        acc = jnp.zeros((c, LANES), F32)
        for half in range(2):
            p = _sink_softmax(_bdot_nt(q2, ks[kvh][half]), mask, sink_ref[2 * j + half])
            acc = acc + _bdot(p, vs[kvh][half])
        o_ref[:, j * LANES:(j + 1) * LANES] = acc.astype(BF16)


def _swa_prompt(sq, sk, sv, sink, *, nb, t):
    nblk = t // WINDOW
    cur = lambda b, n: (b * nblk + n, 0)
    prev = lambda b, n: (b * nblk + jnp.maximum(n - 1, 0), 0)
    return pl.pallas_call(
        _swa_kernel,
        out_shape=jax.ShapeDtypeStruct((nb * t, _SWA_QW), BF16),
        grid=(nb, nblk),
        in_specs=[pl.BlockSpec(memory_space=pltpu.SMEM),
                  pl.BlockSpec((WINDOW, _SWA_QW), cur),
                  pl.BlockSpec((WINDOW, _SWA_KW), cur),
                  pl.BlockSpec((WINDOW, _SWA_KW), prev),
                  pl.BlockSpec((WINDOW, _SWA_KW), cur),
                  pl.BlockSpec((WINDOW, _SWA_KW), prev)],
        out_specs=pl.BlockSpec((WINDOW, _SWA_QW), cur),
        compiler_params=_cparams("arbitrary", "arbitrary"),
    )(sink, sq, sk, sk, sv, sv)


def _swa_sample_kernel(sink_ref, q_ref, kn_ref, vn_ref, kc_ref, vc_ref, o_ref, *, sb, w):
    kn, vn = kn_ref[...], vn_ref[...]
    pad = jnp.zeros((LANES - sb, _SWA_KW), F32)
    kall = jnp.concatenate([kc_ref[...].reshape(sb * w, _SWA_KW), kn, pad], axis=0)
    vall = jnp.concatenate([vc_ref[...].reshape(sb * w, _SWA_KW), vn, pad], axis=0)
    ncol = sb * w + LANES
    lo = lax.broadcasted_iota(I32, (sb, LANES), 1) < SWA_HD
    group = SWA_HEADS // SWA_KV_HEADS
    pieces = []
    for h in range(SWA_HEADS):
        slab = q_ref[:, (h // 2) * LANES:(h // 2 + 1) * LANES].astype(F32)
        mine = jnp.where(lo, slab, 0.0) if h % 2 == 0 else jnp.where(lo, 0.0, slab)
        pieces.append(mine if (h % 2) == (h // group) else pltpu.roll(mine, SWA_HD, 1))
    qrows = jnp.concatenate(pieces, axis=0)
    nrow = SWA_HEADS * sb
    s = _bdot_nt(qrows, kall)
    rb = lax.broadcasted_iota(I32, (nrow, ncol), 0) % sb
    ci = lax.broadcasted_iota(I32, (nrow, ncol), 1)
    in_cache = (ci < sb * w) & (ci // w == rb) & ((w - ci % w) < WINDOW)
    mask = in_cache | (ci == sb * w + rb)
    sink_col = jnp.concatenate([jnp.full((sb, 1), sink_ref[h], F32) for h in range(SWA_HEADS)], axis=0)
    p = _sink_softmax(s, mask, sink_col)
    o = _bdot(p, vall)
    for j in range(SWA_HEADS // 2):
        acc = jnp.zeros((sb, LANES), F32)
        for half in range(2):
            h = 2 * j + half
            oh = o[h * sb:(h + 1) * sb]
            own = jnp.where(lo, oh, 0.0) if h // group == 0 else jnp.where(lo, 0.0, oh)
            acc = acc + (own if (h // group) == half else pltpu.roll(own, SWA_HD, 1))
        o_ref[:, j * LANES:(j + 1) * LANES] = acc.astype(BF16)


def _swa_sample(sq, sk, sv, cache_k, cache_v, sink, *, row0, ns):
    sb = min(16, ns)
    w = cache_k.shape[1]
    base = row0 // sb
    blk = lambda i: (base + i, 0)
    cblk = lambda i: (i, 0, 0)
    return pl.pallas_call(
        functools.partial(_swa_sample_kernel, sb=sb, w=w),
        out_shape=jax.ShapeDtypeStruct((ns, _SWA_QW), BF16),
        grid=(ns // sb,),
        in_specs=[pl.BlockSpec(memory_space=pltpu.SMEM),
                  pl.BlockSpec((sb, _SWA_QW), blk),
                  pl.BlockSpec((sb, _SWA_KW), blk),
                  pl.BlockSpec((sb, _SWA_KW), blk),
                  pl.BlockSpec((sb, w, _SWA_KW), cblk),
                  pl.BlockSpec((sb, w, _SWA_KW), cblk)],
        out_specs=pl.BlockSpec((sb, _SWA_QW), lambda i: (i, 0)),
        compiler_params=_cparams("arbitrary"),
    )(sink, sq, sk, sv, cache_k, cache_v)


def _route(logits):
    lane = lax.broadcasted_iota(I32, logits.shape, 1)
    big = jnp.int32(1 << 20)
    neg = -jnp.inf

    def top(mask):
        v = jnp.max(jnp.where(mask, logits, neg), axis=-1, keepdims=True)
        i = jnp.min(jnp.where(mask & (logits == v), lane, big), axis=-1, keepdims=True)
        return v, i

    gmask = lane < N_GROUPS
    gmax, gsel = top(gmask)
    p_group = 1.0 / jnp.sum(jnp.where(gmask, jnp.exp(logits - gmax), 0.0), axis=-1, keepdims=True)
    first = N_GROUPS + gsel * EXPERTS_PER_GROUP
    emask = (lane >= first) & (lane < first + EXPERTS_PER_GROUP)
    v1, i1 = top(emask)
    v2, i2 = top(emask & (lane != i1))
    t = jnp.exp(v2 - v1)
    w1 = p_group / (1.0 + t)
    return i1 - N_GROUPS, i2 - N_GROUPS, w1, w1 * t


def _outproj_kernel(gtp_ref, gts_ref, osp_ref, oss_ref, sa_ref, sb_ref, xp_ref, xs_ref, g1p_ref, shp_ref, scp_ref,
                    g1s_ref, shs_ref, scs_ref, n2_ref, wur_ref, wus_ref, wo_ref, wrh_ref, wrl_ref, br_ref,
                    x1_o, h2_o, rt_o, *, npt):
    is_s = pl.program_id(0) >= npt
    x = jnp.where(is_s, xs_ref[...], xp_ref[...])
    g1 = jnp.where(is_s, g1s_ref[...], g1p_ref[...])
    sh = jnp.where(is_s, shs_ref[...], shp_ref[...])
    sc = jnp.where(is_s, scs_ref[...], scp_ref[...])
    gated = jnp.where(is_s, gts_ref[...], gtp_ref[...])
    oswa = jnp.where(is_s, oss_ref[...], osp_ref[...])
    y_ret = jnp.dot(gated, wur_ref[...], preferred_element_type=F32)
    y_swa = jnp.dot(oswa, wus_ref[...], preferred_element_type=F32)
    merged = sa_ref[...].astype(F32) * y_ret + sb_ref[...].astype(F32) * y_swa
    x1 = x + g1 * jnp.dot(merged.astype(BF16), wo_ref[...], preferred_element_type=F32)
    x1_o[...] = x1
    h2 = _rms(x1, n2_ref[...]) * (1.0 + sc) + sh
    hi = h2.astype(BF16)
    h2_o[...] = hi
    lo = (h2 - hi.astype(F32)).astype(BF16)
    wrh = wrh_ref[...]
    logits = (jnp.dot(hi, wrh, preferred_element_type=F32) + jnp.dot(lo, wrh, preferred_element_type=F32)
              + jnp.dot(hi, wrl_ref[...], preferred_element_type=F32) + br_ref[...])
    e1, e2, w1, w2 = _route(logits)
    lane = lax.broadcasted_iota(I32, logits.shape, 1)
    rt_o[...] = jnp.where(lane == 0, e1.astype(F32),
                          jnp.where(lane == 1, e2.astype(F32),
                                    jnp.where(lane == 2, w1, jnp.where(lane == 3, w2, 0.0))))


def _outproj(gated_p, gated_s, oswa_p, oswa_s, siga, sigb, xp, xs_pad, modp, mods, n2, wur, wus, wo, wr, br,
             *, npt, tpb, nb):
    d = xp.shape[1]
    nt = npt + 1
    nrow = nt * TM
    row = lambda i: (i, 0)
    ptile = lambda i: (jnp.minimum(i, npt - 1), 0)
    pbatch = lambda col: (lambda i: (jnp.minimum(i // tpb, nb - 1), 0, col))
    scol = lambda col: (lambda i: (0, col))
    const = lambda i: (0, 0)
    wr_hi = wr.astype(BF16)
    wr_lo = (wr - wr_hi.astype(F32)).astype(BF16)
    return pl.pallas_call(
        functools.partial(_outproj_kernel, npt=npt),
        out_shape=[jax.ShapeDtypeStruct((nrow, d), F32),
                   jax.ShapeDtypeStruct((nrow, d), BF16),
                   jax.ShapeDtypeStruct((nrow, LANES), F32)],
        grid=(nt,),
        in_specs=[pl.BlockSpec((TM, _RET_W), ptile), pl.BlockSpec((TM, _RET_W), const),
                  pl.BlockSpec((TM, _SWA_QW), ptile), pl.BlockSpec((TM, _SWA_QW), const),
                  pl.BlockSpec((TM, d), row), pl.BlockSpec((TM, d), row),
                  pl.BlockSpec((TM, d), ptile), pl.BlockSpec((TM, d), const),
                  pl.BlockSpec((None, 1, d), pbatch(2)), pl.BlockSpec((None, 1, d), pbatch(3)),
                  pl.BlockSpec((None, 1, d), pbatch(4)),
                  pl.BlockSpec((TM, d), scol(2)), pl.BlockSpec((TM, d), scol(3)), pl.BlockSpec((TM, d), scol(4)),
                  pl.BlockSpec((1, d), const),
                  pl.BlockSpec(wur.shape, const), pl.BlockSpec(wus.shape, const), pl.BlockSpec(wo.shape, const),
                  pl.BlockSpec(wr.shape, const), pl.BlockSpec(wr.shape, const), pl.BlockSpec((1, LANES), const)],
        out_specs=[pl.BlockSpec((TM, d), row), pl.BlockSpec((TM, d), row), pl.BlockSpec((TM, LANES), row)],
        compiler_params=_cparams("arbitrary"),
    )(gated_p, gated_s, oswa_p, oswa_s, siga, sigb, xp, xs_pad, modp, modp, modp, mods, mods, mods, n2,
      wur, wus, wo, wr_hi, wr_lo, br)


def _plan_kernel(rt_ref, lp_o, cnt_o, ls_o, gb_o, carry, *, npt, ns):
    i = pl.program_id(0)

    @pl.when(i == 0)
    def _():
        carry[...] = jnp.zeros_like(carry)

    rt = rt_ref[...]
    lane = lax.broadcasted_iota(I32, (TM, LANES), 1)
    row = lax.broadcasted_iota(I32, (TM, LANES), 0)
    valid = (i < npt) | (row < ns)
    e1 = rt[:, 0:1].astype(I32)
    e2 = rt[:, 1:2].astype(I32)
    oh1 = ((lane == e1) & valid).astype(F32)
    oh2 = ((lane == e2) & valid).astype(F32)
    oh = oh1 + oh2
    tri = (lax.broadcasted_iota(I32, (TM, TM), 0) > lax.broadcasted_iota(I32, (TM, TM), 1)).astype(BF16)
    before = _bdot(tri, oh)
    cnt = jnp.sum(oh, axis=0, keepdims=True)
    cnt8 = jnp.floor((cnt + (SUBLANES - 1)) * (1.0 / SUBLANES))
    upper = (lax.broadcasted_iota(I32, (LANES, LANES), 0) < lax.broadcasted_iota(I32, (LANES, LANES), 1))
    lstart = SUBLANES * _bdot(jnp.broadcast_to(cnt8, (SUBLANES, LANES)), upper.astype(BF16))[0:1]
    slot = lstart + before
    lp1 = jnp.sum(oh1 * slot, axis=-1, keepdims=True)
    lp2 = jnp.sum(oh2 * slot, axis=-1, keepdims=True)
    vcol = valid[:, 0:1]
    lp_o[...] = jnp.where(lane == 0, jnp.where(vcol, lp1, -1.0), jnp.where(lane == 1, jnp.where(vcol, lp2, -1.0), 0.0))
    cnt_o[...] = (SUBLANES * cnt8).astype(I32)
    ls_o[...] = lstart.astype(I32)
    gb_o[...] = carry[...].astype(I32)
    carry[...] = carry[...] + SUBLANES * cnt8


def _plan(rt, *, npt, ns):
    nt = npt + 1
    meta = jax.ShapeDtypeStruct((nt, 1, LANES), I32)
    mspec = pl.BlockSpec((None, 1, LANES), lambda i: (i, 0, 0))
    return pl.pallas_call(
        functools.partial(_plan_kernel, npt=npt, ns=ns),
        out_shape=[jax.ShapeDtypeStruct((nt * TM, LANES), F32), meta, meta, meta],
        grid=(nt,),
        in_specs=[pl.BlockSpec((TM, LANES), lambda i: (i, 0))],
        out_specs=[pl.BlockSpec((TM, LANES), lambda i: (i, 0)), mspec, mspec, mspec],
        scratch_shapes=[pltpu.VMEM((1, LANES), F32)],
        compiler_params=_cparams("arbitrary"),
    )(rt)


def _aligned(v):
    return v if isinstance(v, int) else pl.multiple_of(v, SUBLANES)


def _run_copy(src, dst, s_start, d_start, n, sem):
    s_start, d_start, n = _aligned(s_start), _aligned(d_start), _aligned(n)
    return pltpu.make_async_copy(src.at[pl.ds(s_start, n)], dst.at[pl.ds(d_start, n)], sem)


def _dispatch_kernel(cnt_ref, ls_ref, gd_ref, ps_ref, pn_ref, nu_ref, h_ref, lp_ref, xs_ref,
                     sorted_scr, zero_scr, sem, zsem, *, nt, maxt):
    i = pl.program_id(0)

    def pad(e):
        return _run_copy(zero_scr, xs_ref, 0, ps_ref[e], pn_ref[e], zsem)

    def tail(j):
        return _run_copy(zero_scr, xs_ref, 0, j * TM, TM, zsem)

    def each_pad(fn):
        def body(e, c):
            @pl.when(pn_ref[e] > 0)
            def _():
                fn(pad(e))
            return c
        lax.fori_loop(0, N_EXPERTS, body, 0)

    def each_tail(fn):
        def body(j, c):
            fn(tail(j))
            return c
        lax.fori_loop(nu_ref[0], maxt, body, 0)

    @pl.when(i == 0)
    def _():
        zero_scr[...] = jnp.zeros_like(zero_scr)
        each_pad(lambda cp: cp.start())
        each_tail(lambda cp: cp.start())

    lpt = lp_ref[...].T
    slot = lax.broadcasted_iota(I32, (SLOTS, TM), 0).astype(F32)
    perm = ((slot == lpt[0:1]) | (slot == lpt[1:2])).astype(BF16)
    sorted_scr[...] = jnp.dot(perm, h_ref[...], preferred_element_type=F32)

    def copy(e):
        return _run_copy(sorted_scr, xs_ref, ls_ref[i * N_EXPERTS + e], gd_ref[i * N_EXPERTS + e],
                         cnt_ref[i * N_EXPERTS + e], sem)

    def start(e, c):
        @pl.when(cnt_ref[i * N_EXPERTS + e] > 0)
        def _():
            copy(e).start()
        return c

    def wait(e, c):
        @pl.when(cnt_ref[i * N_EXPERTS + e] > 0)
        def _():
            copy(e).wait()
        return c

    lax.fori_loop(0, N_EXPERTS, start, 0)
    lax.fori_loop(0, N_EXPERTS, wait, 0)

    @pl.when(i == nt - 1)
    def _():
        each_pad(lambda cp: cp.wait())
        each_tail(lambda cp: cp.wait())


def _dispatch(cnt, ls, gd, ps, pn, nu, h2, lp, *, nt, maxt):
    d = h2.shape[1]
    return pl.pallas_call(
        functools.partial(_dispatch_kernel, nt=nt, maxt=maxt),
        out_shape=jax.ShapeDtypeStruct((maxt * TM, d), F32),
        grid_spec=pltpu.PrefetchScalarGridSpec(
            num_scalar_prefetch=6,
            grid=(nt,),
            in_specs=[pl.BlockSpec((TM, d), lambda i, *_: (i, 0)),
                      pl.BlockSpec((TM, LANES), lambda i, *_: (i, 0))],
            out_specs=pl.BlockSpec(memory_space=pl.ANY),
            scratch_shapes=[pltpu.VMEM((SLOTS, d), F32), pltpu.VMEM((TM, d), F32),
                            pltpu.SemaphoreType.DMA(()), pltpu.SemaphoreType.DMA(())]),
        compiler_params=_cparams("arbitrary"),
    )(cnt, ls, gd, ps, pn, nu, h2, lp)


def _experts_kernel(te_ref, nu_ref, x_ref, w1_ref, w3_ref, w2_ref, y_ref, w1b, w3b, w2b):
    j = pl.program_id(0)
    changed = (j == 0) | (te_ref[j] != te_ref[jnp.maximum(j - 1, 0)])

    @pl.when(changed)
    def _():
        w1b[...] = w1_ref[...].astype(BF16)
        w3b[...] = w3_ref[...].astype(BF16)
        w2b[...] = w2_ref[...].astype(BF16)

    @pl.when(j < nu_ref[0])
    def _():
        x = x_ref[...].astype(BF16)
        a = jnp.dot(x, w1b[...], preferred_element_type=F32)
        b = jnp.dot(x, w3b[...], preferred_element_type=F32)
        y_ref[...] = jnp.dot((_silu(a) * b).astype(BF16), w2b[...], preferred_element_type=F32)

    @pl.when(j >= nu_ref[0])
    def _():
        y_ref[...] = jnp.zeros_like(y_ref)


def _experts(te, nu, xs, w1, w3, w2, *, maxt):
    d = xs.shape[1]
    f = w1.shape[2]
    wsel = lambda j, te, nu: (te[j], 0, 0)
    return pl.pallas_call(
        _experts_kernel,
        out_shape=jax.ShapeDtypeStruct(xs.shape, F32),
        grid_spec=pltpu.PrefetchScalarGridSpec(
            num_scalar_prefetch=2,
            grid=(maxt,),
            in_specs=[pl.BlockSpec((TM, d), lambda j, te, nu: (jnp.minimum(j, nu[0] - 1), 0)),
                      pl.BlockSpec((None, d, f), wsel),
                      pl.BlockSpec((None, d, f), wsel),
                      pl.BlockSpec((None, f, d), wsel)],
            out_specs=pl.BlockSpec((TM, d), lambda j, te, nu: (j, 0)),
            scratch_shapes=[pltpu.VMEM((d, f), BF16), pltpu.VMEM((d, f), BF16), pltpu.VMEM((f, d), BF16)]),
        compiler_params=_cparams("arbitrary"),
    )(te, nu, xs, w1, w3, w2)


def _combine_kernel(cnt_ref, ls_ref, gd_ref, ys_ref, lp_ref, rt_ref, x1_ref, g2p_ref, g2s_ref, fg_ref,
                    yp_o, ys_o, ybuf, sem, *, npt):
    i = pl.program_id(0)
    ybuf[...] = jnp.zeros_like(ybuf)

    def copy(e):
        return _run_copy(ys_ref, ybuf, gd_ref[i * N_EXPERTS + e], ls_ref[i * N_EXPERTS + e],
                         cnt_ref[i * N_EXPERTS + e], sem)

    def start(e, c):
        @pl.when(cnt_ref[i * N_EXPERTS + e] > 0)
        def _():
            copy(e).start()
        return c

    def wait(e, c):
        @pl.when(cnt_ref[i * N_EXPERTS + e] > 0)
        def _():
            copy(e).wait()
        return c

    lax.fori_loop(0, N_EXPERTS, start, 0)
    lax.fori_loop(0, N_EXPERTS, wait, 0)

    yb = ybuf[...].astype(BF16)
    slot = lax.broadcasted_iota(I32, (TM, SLOTS), 1).astype(F32)
    lp = lp_ref[...]
    rt = rt_ref[...]
    m1 = jnp.dot((slot == lp[:, 0:1]).astype(BF16), yb, preferred_element_type=F32)
    m2 = jnp.dot((slot == lp[:, 1:2]).astype(BF16), yb, preferred_element_type=F32)
    moe = rt[:, 2:3] * m1 + rt[:, 3:4] * m2
    g2 = jnp.where(i >= npt, g2s_ref[...], g2p_ref[...])
    y = _rms(x1_ref[...] + g2 * moe, fg_ref[...])

    @pl.when(i < npt)
    def _():
        yp_o[...] = y

    @pl.when(i >= npt)
    def _():
        ys_o[...] = y


def _combine(cnt, ls, gd, ys, lp, rt, x1, modp, mods, fg, *, npt, tpb, nb):
    d = x1.shape[1]
    nt = npt + 1
    row = lambda i, *_: (i, 0)
    return pl.pallas_call(
        functools.partial(_combine_kernel, npt=npt),
        out_shape=[jax.ShapeDtypeStruct((npt * TM, d), F32), jax.ShapeDtypeStruct((TM, d), F32)],
        grid_spec=pltpu.PrefetchScalarGridSpec(
            num_scalar_prefetch=3,
            grid=(nt,),
            in_specs=[pl.BlockSpec(memory_space=pl.ANY),
                      pl.BlockSpec((TM, LANES), row), pl.BlockSpec((TM, LANES), row), pl.BlockSpec((TM, d), row),
                      pl.BlockSpec((None, 1, d), lambda i, *_: (jnp.minimum(i // tpb, nb - 1), 0, 5)),
                      pl.BlockSpec((TM, d), lambda i, *_: (0, 5)),
                      pl.BlockSpec((1, d), lambda i, *_: (0, 0))],
            out_specs=[pl.BlockSpec((TM, d), lambda i, *_: (jnp.minimum(i, npt - 1), 0)),
                       pl.BlockSpec((TM, d), lambda i, *_: (0, 0))],
            scratch_shapes=[pltpu.VMEM((SLOTS, d), F32), pltpu.SemaphoreType.DMA(())]),
        compiler_params=_cparams("arbitrary"),
    )(cnt, ls, gd, ys, lp, rt, x1, modp, mods, fg)


def _rotation_tables(t):
    pos = jnp.concatenate([jnp.arange(t, dtype=I32), jnp.full((TM,), PAST_LEN, I32)]).astype(F32)
    inv_r = 1.0 / (ROPE_THETA ** jnp.linspace(0.0, 1.0, RET_DK // 2, dtype=F32))
    ang_r = pos[:, None] * inv_r[None, :]
    sign_r = jnp.where(jnp.arange(RET_DK) % 2 == 0, -1.0, 1.0).astype(F32)
    cr = jnp.repeat(jnp.cos(ang_r), 2, axis=1)
    sr = jnp.repeat(jnp.sin(ang_r), 2, axis=1) * sign_r[None, :]
    inv_w = ROPE_THETA ** (-jnp.arange(0, SWA_HD, 2, dtype=F32) / SWA_HD)
    ang_w = pos[:, None] * inv_w[None, :]
    reps = LANES // (SWA_HD // 2)
    sign_w = jnp.where(jnp.arange(LANES) % SWA_HD < SWA_HD // 2, -1.0, 1.0).astype(F32)
    cw = jnp.tile(jnp.cos(ang_w), (1, reps))
    sw = jnp.tile(jnp.sin(ang_w), (1, reps)) * sign_w[None, :]
    return cr, sr, cw, sw


def kernel(x_prompt, x_sample, c_prompt, c_sample, state_ret, cache_swa_k, cache_swa_v, w_ada, b_ada, norm1_g, norm2_g, w_in, w_up_ret, w_up_swa, w_o, sink, w_rg, b_rg, w_re, b_re, w1, w3, w2, final_g):
    nb, t, d = x_prompt.shape
    ns, dec_seq, _ = x_sample.shape
    depth = w_ada.shape[0]
    assert depth == 1 and dec_seq == 1, "single layer, one new token per sequence"
    assert t % TM == 0 and ns <= TM and ns % 16 == 0 and d % LANES == 0
    assert N_GROUPS + N_EXPERTS <= LANES
    w = cache_swa_k.shape[2]
    tpb = t // TM
    npt = nb * tpb
    nt = npt + 1
    np_rows = nb * t
    n_tok = np_rows + ns
    maxt = -(-(2 * n_tok + nt * N_EXPERTS * (SUBLANES - 1) + N_EXPERTS * (TM - 1)) // TM)

    xp = x_prompt.reshape(np_rows, d)
    xs_pad = jnp.pad(x_sample.reshape(ns, d), ((0, TM - ns), (0, 0)))

    c_all = jnp.concatenate([jnp.pad(c_prompt, ((0, SUBLANES - nb % SUBLANES), (0, 0))),
                             jnp.pad(c_sample, ((0, TM - ns), (0, 0)))])
    mod = _modulation(c_all, w_ada[0], b_ada[0])
    modp = mod[:nb].reshape(nb, 1, 6 * d)
    mods = mod[c_all.shape[0] - TM:]

    tabs = _rotation_tables(t)
    rq, rk, rv, rg, sq, sk, sv, siga, sigb = _inproj(
        xp, xs_pad, modp, mods, norm1_g, w_in[0].astype(BF16), tabs, npt=npt, tpb=tpb, nb=nb)

    dm, qd, kd, cd, gamma = _ret_tables()
    gated_p, st_p = _retention_prompt(rq, rk, rv, rg, (dm, qd, kd, cd), nb=nb, t=t)
    gated_s, st_s = _retention_sample(rq, rk, rv, rg, state_ret[0], gamma, row0=np_rows, ns=ns)
    oswa_p = _swa_prompt(sq, sk, sv, sink[0], nb=nb, t=t)
    ck = cache_swa_k[0].reshape(ns, w, _SWA_KW)
    cv = cache_swa_v[0].reshape(ns, w, _SWA_KW)
    oswa_s = _swa_sample(sq, sk, sv, ck, cv, sink[0], row0=np_rows, ns=ns)
    gated_s = jnp.pad(gated_s, ((0, TM - ns), (0, 0)))
    oswa_s = jnp.pad(oswa_s, ((0, TM - ns), (0, 0)))

    wr = jnp.pad(jnp.concatenate([w_rg[0], w_re[0]], axis=1), ((0, 0), (0, LANES - N_GROUPS - N_EXPERTS)))
    br = jnp.pad(jnp.concatenate([b_rg[0], b_re[0]]), (0, LANES - N_GROUPS - N_EXPERTS)).reshape(1, LANES)
    x1, h2, rt = _outproj(gated_p, gated_s, oswa_p, oswa_s, siga, sigb, xp, xs_pad, modp, mods, norm2_g,
                          w_up_ret[0].astype(BF16), w_up_swa[0].astype(BF16), w_o[0].astype(BF16), wr, br,
                          npt=npt, tpb=tpb, nb=nb)

    lp, cnt, ls, gb = _plan(rt, npt=npt, ns=ns)
    cnt = cnt[:, 0, :N_EXPERTS]
    ls = ls[:, 0, :N_EXPERTS]
    gb = gb[:, 0, :N_EXPERTS]
    seg = jnp.sum(cnt, axis=0)
    tiles = (seg + TM - 1) // TM
    tile_end = jnp.cumsum(tiles)
    row_start = (tile_end - tiles) * TM
    gd = (gb + row_start[None, :]).reshape(-1)
    n_used = tile_end[-1:]
    jj = jnp.minimum(jnp.arange(maxt, dtype=I32), n_used[0] - 1)
    te = jnp.minimum(jnp.sum((tile_end[None, :] <= jj[:, None]).astype(I32), axis=1), N_EXPERTS - 1)
    cnt = cnt.reshape(-1)
    ls = ls.reshape(-1)
    n_used = n_used.astype(I32)
    xs = _dispatch(cnt, ls, gd, row_start + seg, tiles * TM - seg, n_used, h2, lp, nt=nt, maxt=maxt)
    ys = _experts(te, n_used, xs, w1[0], w3[0], w2[0], maxt=maxt)
    y_p, y_s = _combine(cnt, ls, gd, ys, lp, rt, x1, modp, mods, final_g.reshape(1, d), npt=npt, tpb=tpb, nb=nb)

    y_prompt = y_p.reshape(nb, t, d)
    y_sample = y_s[:ns].reshape(ns, 1, d)
    wk = min(WINDOW, t)
    last = lambda a: jnp.stack([a[(b + 1) * t - wk:(b + 1) * t] for b in range(nb)]).reshape(
        nb, wk, SWA_KV_HEADS, SWA_HD)
    skp, svp = last(sk), last(sv)
    shp = (ns, 1, SWA_KV_HEADS, SWA_HD)
    ks_new = jnp.concatenate([cache_swa_k[0], sk[np_rows:n_tok].reshape(shp)], axis=1)[:, 1:]
    vs_new = jnp.concatenate([cache_swa_v[0], sv[np_rows:n_tok].reshape(shp)], axis=1)[:, 1:]
    return (y_prompt, y_sample, st_p[None], st_s[None], skp[None], svp[None], ks_new[None], vs_new[None])
```

```python
import functools

import jax
import jax.numpy as jnp
from jax import lax
from jax.experimental import pallas as pl
from jax.experimental.pallas import tpu as pltpu

F32 = jnp.float32
BF16 = jnp.bfloat16
I32 = jnp.int32

PAST_LEN = 8192
RET_HEADS = 4
RET_DK = 128
RET_DV = 128
RET_CHUNK = 128
SWA_HEADS = 8
SWA_KV_HEADS = 2
SWA_HD = 64
WINDOW = 128
ROPE_THETA = 10000.0
N_GROUPS = 4
EXPERTS_PER_GROUP = 8
N_EXPERTS = N_GROUPS * EXPERTS_PER_GROUP
D_EXPERT = 256
NORM_EPS = 1e-6

LANES = 128
SUBLANES = 8
TM = 256
SLOTS = 2 * TM + 2 * LANES
VMEM_LIMIT = 56 * 1024 * 1024

_RET_W = RET_HEADS * RET_DK
_SWA_QW = SWA_HEADS * SWA_HD
_SWA_KW = SWA_KV_HEADS * SWA_HD


def _cparams(*sem):
    return pltpu.CompilerParams(dimension_semantics=sem, vmem_limit_bytes=VMEM_LIMIT)


def _sigmoid(x):
    return 1.0 / (1.0 + jnp.exp(-x))


def _silu(x):
    return x * _sigmoid(x)


def _bdot(a, b):
    return jnp.dot(a.astype(BF16), b.astype(BF16), preferred_element_type=F32)


def _bdot_nt(a, b):
    return lax.dot_general(a.astype(BF16), b.astype(BF16), (((1,), (1,)), ((), ())), preferred_element_type=F32)


def _mod_kernel(c_ref, w_ref, b_ref, o_ref):
    o_ref[...] = _bdot(_silu(c_ref[...]), w_ref[...]) + b_ref[...]


def _modulation(c_all, w_ada, b_ada):
    rows, d = c_all.shape
    n = w_ada.shape[1]
    return pl.pallas_call(
        _mod_kernel,
        out_shape=jax.ShapeDtypeStruct((rows, n), F32),
        grid=(n // d,),
        in_specs=[pl.BlockSpec((rows, d), lambda j: (0, 0)),
                  pl.BlockSpec((d, d), lambda j: (0, j)),
                  pl.BlockSpec((1, d), lambda j: (0, j))],
        out_specs=pl.BlockSpec((rows, d), lambda j: (0, j)),
        compiler_params=_cparams("arbitrary"),
    )(c_all, w_ada, b_ada.reshape(1, n))


def _rms(x, g):
    return x * lax.rsqrt(jnp.mean(x * x, axis=-1, keepdims=True) + NORM_EPS) * g


def _pair_rotate(z, cos, sin_signed):
    n = z.shape[-1]
    lane = lax.broadcasted_iota(I32, z.shape, 1)
    partner = jnp.where((lane & 1) == 0, pltpu.roll(z, n - 1, 1), pltpu.roll(z, 1, 1))
    reps = n // LANES
    cos = jnp.concatenate([cos] * reps, axis=1) if reps > 1 else cos
    sin_signed = jnp.concatenate([sin_signed] * reps, axis=1) if reps > 1 else sin_signed
    return z * cos + partner * sin_signed


def _half_rotate(z, cos, sin_signed):
    n = z.shape[-1]
    half = SWA_HD // 2
    lane = lax.broadcasted_iota(I32, z.shape, 1)
    partner = jnp.where((lane & (SWA_HD - 1)) < half, pltpu.roll(z, n - half, 1), pltpu.roll(z, half, 1))
    reps = n // LANES
    cos = jnp.concatenate([cos] * reps, axis=1) if reps > 1 else cos
    sin_signed = jnp.concatenate([sin_signed] * reps, axis=1) if reps > 1 else sin_signed
    return z * cos + partner * sin_signed


def _inproj_kernel(xp_ref, xs_ref, shp_ref, scp_ref, shs_ref, scs_ref, n1_ref, w_ref,
                   cr_ref, sr_ref, cw_ref, sw_ref,
                   rq_o, rk_o, rv_o, rg_o, sq_o, sk_o, sv_o, za_o, zb_o, *, npt):
    is_s = pl.program_id(0) >= npt
    x = jnp.where(is_s, xs_ref[...], xp_ref[...])
    sh = jnp.where(is_s, shs_ref[...], shp_ref[...])
    sc = jnp.where(is_s, scs_ref[...], scp_ref[...])
    h = (_rms(x, n1_ref[...]) * (1.0 + sc) + sh).astype(BF16)
    cr, sr, cw, sw = cr_ref[...], sr_ref[...], cw_ref[...], sw_ref[...]

    def seg(a, b):
        return jnp.dot(h, w_ref[:, a:b], preferred_element_type=F32)

    o = 0
    rq_o[...] = _pair_rotate(seg(o, o + _RET_W), cr, sr).astype(BF16)
    o += _RET_W
    rk_o[...] = (_pair_rotate(seg(o, o + _RET_W), cr, sr) * (RET_DK ** -0.5)).astype(BF16)
    o += _RET_W
    rv_o[...] = seg(o, o + _RET_W).astype(BF16)
    o += _RET_W
    rg_o[...] = _silu(seg(o, o + _RET_W)).astype(BF16)
    o += _RET_W
    sq_o[...] = (_half_rotate(seg(o, o + _SWA_QW), cw, sw) * (SWA_HD ** -0.5)).astype(BF16)
    o += _SWA_QW
    zkv = seg(o, o + 2 * _SWA_KW)
    sk_o[...] = _half_rotate(zkv[:, :_SWA_KW], cw, sw)
    sv_o[...] = zkv[:, _SWA_KW:]
    o += 2 * _SWA_KW
    d = x.shape[-1]
    za_o[...] = _sigmoid(seg(o, o + d)).astype(BF16)
    o += d
    zb_o[...] = _sigmoid(seg(o, o + d)).astype(BF16)


def _inproj(xp, xs_pad, modp, mods, n1, w_in_b, tabs, *, npt, tpb, nb):
    d = xp.shape[1]
    nt = npt + 1
    nrow = nt * TM
    n_in = w_in_b.shape[1]
    ptile = lambda i: (jnp.minimum(i, npt - 1), 0)
    pbatch = lambda col: (lambda i: (jnp.minimum(i // tpb, nb - 1), 0, col))
    tab_idx = lambda i: (jnp.where(i < npt, i % tpb, tpb), 0)
    out_cols = [(_RET_W, BF16)] * 4 + [(_SWA_QW, BF16), (_SWA_KW, F32), (_SWA_KW, F32), (d, BF16), (d, BF16)]
    return pl.pallas_call(
        functools.partial(_inproj_kernel, npt=npt),
        out_shape=[jax.ShapeDtypeStruct((nrow, c), t) for c, t in out_cols],
        grid=(nt,),
        in_specs=[pl.BlockSpec((TM, d), ptile),
                  pl.BlockSpec((TM, d), lambda i: (0, 0)),
                  pl.BlockSpec((None, 1, d), pbatch(0)),
                  pl.BlockSpec((None, 1, d), pbatch(1)),
                  pl.BlockSpec((TM, d), lambda i: (0, 0)),
                  pl.BlockSpec((TM, d), lambda i: (0, 1)),
                  pl.BlockSpec((1, d), lambda i: (0, 0)),
                  pl.BlockSpec((d, n_in), lambda i: (0, 0))]
                 + [pl.BlockSpec((TM, LANES), tab_idx)] * 4,
        out_specs=[pl.BlockSpec((TM, c), lambda i: (i, 0)) for c, _ in out_cols],
        compiler_params=_cparams("arbitrary"),
    )(xp, xs_pad, modp, modp, mods, mods, n1, w_in_b, *tabs)


RET_STEP_CHUNKS = 4


def _ret_kernel(q_ref, k_ref, v_ref, g_ref, dm_ref, qd_ref, kd_ref, cd_ref, o_ref, st_ref, s_scr, *, nsteps):
    step = pl.program_id(1)

    @pl.when(step == 0)
    def _():
        s_scr[...] = jnp.zeros_like(s_scr)

    for h in range(RET_HEADS):
        sl = slice(h * RET_DK, (h + 1) * RET_DK)
        state = s_scr[h]
        for c in range(RET_STEP_CHUNKS):
            rows = slice(c * RET_CHUNK, (c + 1) * RET_CHUNK)
            q, k, v = q_ref[rows, sl], k_ref[rows, sl], v_ref[rows, sl]
            att = _bdot_nt(q, k) * dm_ref[h]
            o = _bdot(att, v) + _bdot(q.astype(F32) * qd_ref[h], state)
            kd = (k.astype(F32) * kd_ref[h]).astype(BF16)
            kv = lax.dot_general(kd, v, (((0,), (0,)), ((), ())), preferred_element_type=F32)
            state = cd_ref[h] * state + kv
            o = o * lax.rsqrt(jnp.mean(o * o, axis=-1, keepdims=True) + NORM_EPS)
            o_ref[rows, sl] = (o * g_ref[rows, sl].astype(F32)).astype(BF16)
        s_scr[h] = state

    @pl.when(step == nsteps - 1)
    def _():
        st_ref[...] = s_scr[...]


def _ret_tables():
    ld = jnp.log(1.0 - 2.0 ** (-5.0 - jnp.arange(RET_HEADS, dtype=F32)))
    idx = jnp.arange(RET_CHUNK, dtype=F32)
    diff = idx[:, None] - idx[None, :]
    causal = diff >= 0
    dmask = jnp.where(causal[None], jnp.exp(ld[:, None, None] * jnp.where(causal, diff, 0.0)[None]), 0.0)
    k_dec = jnp.exp(ld[None, :] * (RET_CHUNK - 1.0 - idx)[:, None])
    q_dec = jnp.exp(ld[None, :] * (idx + 1.0)[:, None])
    chunk_decay = jnp.exp(ld * RET_CHUNK)
    bc = lambda t: jnp.broadcast_to(t.T[:, :, None], (RET_HEADS, RET_CHUNK, RET_DV))
    cd = jnp.broadcast_to(chunk_decay[:, None, None], (RET_HEADS, 1, RET_DV))
    return dmask, bc(q_dec), bc(k_dec), cd, jnp.exp(ld)


def _retention_prompt(rq, rk, rv, rg, tabs, *, nb, t):
    rows = RET_STEP_CHUNKS * RET_CHUNK
    nsteps = t // rows
    dm, qd, kd, cd = tabs
    blk = lambda b, c: (b * nsteps + c, 0)
    full3 = lambda b, c: (0, 0, 0)
    return pl.pallas_call(
        functools.partial(_ret_kernel, nsteps=nsteps),
        out_shape=[jax.ShapeDtypeStruct((nb * t, _RET_W), BF16),
                   jax.ShapeDtypeStruct((nb, RET_HEADS, RET_DK, RET_DV), F32)],
        grid=(nb, nsteps),
        in_specs=[pl.BlockSpec((rows, _RET_W), blk)] * 4
                 + [pl.BlockSpec((RET_HEADS, RET_CHUNK, RET_DV), full3)] * 3
                 + [pl.BlockSpec((RET_HEADS, 1, RET_DV), full3)],
        out_specs=[pl.BlockSpec((rows, _RET_W), blk),
                   pl.BlockSpec((None, RET_HEADS, RET_DK, RET_DV), lambda b, c: (b, 0, 0, 0))],
        scratch_shapes=[pltpu.VMEM((RET_HEADS, RET_DK, RET_DV), F32)],
        compiler_params=_cparams("arbitrary", "arbitrary"),
    )(rq, rk, rv, rg, dm, qd, kd, cd)


def _ret_sample_kernel(gam_ref, q_ref, k_ref, v_ref, g_ref, s0_ref, o_ref, st_ref, *, sb):
    gamma = gam_ref[pl.program_id(1)]
    q = q_ref[...].astype(F32)
    k = k_ref[...].astype(F32)
    v = v_ref[...].astype(F32)
    rows = sb * RET_DK
    s2 = s0_ref[...].reshape(rows, RET_DV)
    col_b = lax.broadcasted_iota(I32, (sb, rows), 1) // RET_DK
    row_b = lax.broadcasted_iota(I32, (sb, rows), 0)
    qexp = jnp.where(col_b == row_b, jnp.concatenate([q * gamma] * sb, axis=1), 0.0)
    o = jnp.sum(q * k, axis=-1, keepdims=True) * v + _bdot(qexp, s2)
    o = o * lax.rsqrt(jnp.mean(o * o, axis=-1, keepdims=True) + NORM_EPS)
    o_ref[...] = (o * g_ref[...].astype(F32)).astype(BF16)
    rep = (lax.broadcasted_iota(I32, (rows, sb), 0) // RET_DK == lax.broadcasted_iota(I32, (rows, sb), 1))
    rep = rep.astype(BF16)
    krep = _bdot(rep, k)
    vrep = _bdot(rep, v)
    eye = (lax.broadcasted_iota(I32, (rows, RET_DK), 0) % RET_DK == lax.broadcasted_iota(I32, (rows, RET_DK), 1))
    kcol = jnp.sum(jnp.where(eye, krep, 0.0), axis=-1, keepdims=True)
    st_ref[...] = (gamma * s2 + kcol * vrep).reshape(sb, RET_DK, RET_DV)


def _retention_sample(rq, rk, rv, rg, s0, gamma, *, row0, ns):
    sb = min(64, ns)
    base = row0 // sb
    blk = lambda i, h: (base + i, h)
    sblk = lambda i, h: (i, h, 0, 0)
    return pl.pallas_call(
        functools.partial(_ret_sample_kernel, sb=sb),
        out_shape=[jax.ShapeDtypeStruct((ns, _RET_W), BF16),
                   jax.ShapeDtypeStruct(s0.shape, F32)],
        grid=(ns // sb, RET_HEADS),
        in_specs=[pl.BlockSpec(memory_space=pltpu.SMEM)]
                 + [pl.BlockSpec((sb, RET_DK), blk)] * 4
                 + [pl.BlockSpec((sb, None, RET_DK, RET_DV), sblk)],
        out_specs=[pl.BlockSpec((sb, RET_DV), lambda i, h: (i, h)),
                   pl.BlockSpec((sb, None, RET_DK, RET_DV), sblk)],
        compiler_params=_cparams("arbitrary", "arbitrary"),
    )(gamma, rq, rk, rv, rg, s0)


def _sink_softmax(s, mask, sink):
    s = jnp.where(mask, s, -jnp.inf)
    m = jnp.maximum(jnp.max(s, axis=-1, keepdims=True), sink)
    p = jnp.exp(s - m)
    return p / (jnp.sum(p, axis=-1, keepdims=True) + jnp.exp(sink - m))


def _split_kv_heads(x):
    lo = lax.broadcasted_iota(I32, x.shape, 1) < SWA_HD
    h0_lo = jnp.where(lo, x, 0.0)
    h1_hi = jnp.where(lo, 0.0, x)
    return ((h0_lo, pltpu.roll(h0_lo, SWA_HD, 1)), (pltpu.roll(h1_hi, SWA_HD, 1), h1_hi))


SWA_STEP_BLOCKS = 4


def _swa_kernel(sink_ref, q_ref, kc_ref, kp_ref, vc_ref, vp_ref, o_ref):
    n = pl.program_id(1)
    c = WINDOW
    kk = jnp.concatenate([kp_ref[...], kc_ref[...]], axis=0)
    vv = jnp.concatenate([vp_ref[...], vc_ref[...]], axis=0)
    ks = [[a.astype(BF16) for a in pair] for pair in _split_kv_heads(kk)]
    vs = [[a.astype(BF16) for a in pair] for pair in _split_kv_heads(vv)]
    qi = lax.broadcasted_iota(I32, (2 * c, 2 * c), 0) % c
    ki = lax.broadcasted_iota(I32, (2 * c, 2 * c), 1)
    band = (ki > qi) & (ki <= qi + c)
    top = lax.broadcasted_iota(I32, (2 * c, 1), 0) < c
    for s in range(SWA_STEP_BLOCKS):
        rows = slice(s * c, (s + 1) * c)
        keys = slice(s * c, (s + 2) * c)
        mask = (band & ((ki >= c) | (n > 0))) if s == 0 else band
        for kvh in range(SWA_KV_HEADS):
            j0, j1 = 2 * kvh, 2 * kvh + 1
            q2 = jnp.concatenate([q_ref[rows, j0 * LANES:(j0 + 1) * LANES],
                                  q_ref[rows, j1 * LANES:(j1 + 1) * LANES]], axis=0)
            kcat = jnp.concatenate([ks[kvh][0][keys], ks[kvh][1][keys]], axis=0)
            vcat = jnp.concatenate([vs[kvh][0][keys], vs[kvh][1][keys]], axis=0)
            sc = lax.dot_general(q2, kcat, (((1,), (1,)), ((), ())), preferred_element_type=F32)
            ps = []
            for half in range(2):
                sink = jnp.where(top, sink_ref[2 * j0 + half], sink_ref[2 * j1 + half])
                ps.append(_sink_softmax(sc[:, half * 2 * c:(half + 1) * 2 * c], mask, sink).astype(BF16))
            o = jnp.dot(jnp.concatenate(ps, axis=1), vcat, preferred_element_type=F32)
            o_ref[rows, j0 * LANES:(j0 + 1) * LANES] = o[:c].astype(BF16)
            o_ref[rows, j1 * LANES:(j1 + 1) * LANES] = o[c:].astype(BF16)


def _swa_prompt(sq, sk, sv, sink, *, nb, t):
    rows = SWA_STEP_BLOCKS * WINDOW
    nsteps = t // rows
    nblk = t // WINDOW
    cur = lambda b, n: (b * nsteps + n, 0)
    prev = lambda b, n: (b * nblk + jnp.maximum(n * SWA_STEP_BLOCKS - 1, 0), 0)
    return pl.pallas_call(
        _swa_kernel,
        out_shape=jax.ShapeDtypeStruct((nb * t, _SWA_QW), BF16),
        grid=(nb, nsteps),
        in_specs=[pl.BlockSpec(memory_space=pltpu.SMEM),
                  pl.BlockSpec((rows, _SWA_QW), cur),
                  pl.BlockSpec((rows, _SWA_KW), cur),
                  pl.BlockSpec((WINDOW, _SWA_KW), prev),
                  pl.BlockSpec((rows, _SWA_KW), cur),
                  pl.BlockSpec((WINDOW, _SWA_KW), prev)],
        out_specs=pl.BlockSpec((rows, _SWA_QW), cur),
        compiler_params=_cparams("arbitrary", "arbitrary"),
    )(sink, sq, sk, sk, sv, sv)


def _swa_sample_kernel(sink_ref, q_ref, kn_ref, vn_ref, kc_ref, vc_ref, o_ref, *, sb, w):
    kn, vn = kn_ref[...], vn_ref[...]
    pad = jnp.zeros((LANES - sb, _SWA_KW), F32)
    kall = jnp.concatenate([kc_ref[...].reshape(sb * w, _SWA_KW), kn, pad], axis=0)
    vall = jnp.concatenate([vc_ref[...].reshape(sb * w, _SWA_KW), vn, pad], axis=0)
    ncol = sb * w + LANES
    lo = lax.broadcasted_iota(I32, (sb, LANES), 1) < SWA_HD
    group = SWA_HEADS // SWA_KV_HEADS
    pieces = []
    for h in range(SWA_HEADS):
        slab = q_ref[:, (h // 2) * LANES:(h // 2 + 1) * LANES].astype(F32)
        mine = jnp.where(lo, slab, 0.0) if h % 2 == 0 else jnp.where(lo, 0.0, slab)
        pieces.append(mine if (h % 2) == (h // group) else pltpu.roll(mine, SWA_HD, 1))
    qrows = jnp.concatenate(pieces, axis=0)
    nrow = SWA_HEADS * sb
    s = _bdot_nt(qrows, kall)
    rb = lax.broadcasted_iota(I32, (nrow, ncol), 0) % sb
    ci = lax.broadcasted_iota(I32, (nrow, ncol), 1)
    in_cache = (ci < sb * w) & (ci // w == rb) & ((w - ci % w) < WINDOW)
    mask = in_cache | (ci == sb * w + rb)
    sink_col = jnp.concatenate([jnp.full((sb, 1), sink_ref[h], F32) for h in range(SWA_HEADS)], axis=0)
    p = _sink_softmax(s, mask, sink_col)
    o = _bdot(p, vall)
    for j in range(SWA_HEADS // 2):
        acc = jnp.zeros((sb, LANES), F32)
        for half in range(2):
            h = 2 * j + half
            oh = o[h * sb:(h + 1) * sb]
            own = jnp.where(lo, oh, 0.0) if h // group == 0 else jnp.where(lo, 0.0, oh)
            acc = acc + (own if (h // group) == half else pltpu.roll(own, SWA_HD, 1))
        o_ref[:, j * LANES:(j + 1) * LANES] = acc.astype(BF16)


def _swa_sample(sq, sk, sv, cache_k, cache_v, sink, *, row0, ns):
    sb = min(16, ns)
    w = cache_k.shape[1]
    base = row0 // sb
    blk = lambda i: (base + i, 0)
    cblk = lambda i: (i, 0, 0)
    return pl.pallas_call(
        functools.partial(_swa_sample_kernel, sb=sb, w=w),
        out_shape=jax.ShapeDtypeStruct((ns, _SWA_QW), BF16),
        grid=(ns // sb,),
        in_specs=[pl.BlockSpec(memory_space=pltpu.SMEM),
                  pl.BlockSpec((sb, _SWA_QW), blk),
                  pl.BlockSpec((sb, _SWA_KW), blk),
                  pl.BlockSpec((sb, _SWA_KW), blk),
                  pl.BlockSpec((sb, w, _SWA_KW), cblk),
                  pl.BlockSpec((sb, w, _SWA_KW), cblk)],
        out_specs=pl.BlockSpec((sb, _SWA_QW), lambda i: (i, 0)),
        compiler_params=_cparams("arbitrary"),
    )(sink, sq, sk, sv, cache_k, cache_v)


def _route(logits):
    lane = lax.broadcasted_iota(I32, logits.shape, 1)
    big = jnp.int32(1 << 20)
    neg = -jnp.inf

    def top(mask):
        v = jnp.max(jnp.where(mask, logits, neg), axis=-1, keepdims=True)
        i = jnp.min(jnp.where(mask & (logits == v), lane, big), axis=-1, keepdims=True)
        return v, i

    gmask = lane < N_GROUPS
    gmax, gsel = top(gmask)
    p_group = 1.0 / jnp.sum(jnp.where(gmask, jnp.exp(logits - gmax), 0.0), axis=-1, keepdims=True)
    first = N_GROUPS + gsel * EXPERTS_PER_GROUP
    emask = (lane >= first) & (lane < first + EXPERTS_PER_GROUP)
    v1, i1 = top(emask)
    v2, i2 = top(emask & (lane != i1))
    t = jnp.exp(v2 - v1)
    w1 = p_group / (1.0 + t)
    return i1 - N_GROUPS, i2 - N_GROUPS, w1, w1 * t


def _plan_tile(e1, e2, valid, carry):
    lane = lax.broadcasted_iota(I32, (TM, LANES), 1)
    oh1 = ((lane == e1) & valid).astype(F32)
    oh2 = ((lane == e2) & valid).astype(F32)
    oh = oh1 + oh2
    tri = (lax.broadcasted_iota(I32, (TM, TM), 0) > lax.broadcasted_iota(I32, (TM, TM), 1)).astype(BF16)
    before = _bdot(tri, oh)
    cnt = jnp.sum(oh, axis=0, keepdims=True)
    cnt8 = jnp.floor((cnt + (SUBLANES - 1)) * (1.0 / SUBLANES))
    upper = (lax.broadcasted_iota(I32, (LANES, LANES), 0) < lax.broadcasted_iota(I32, (LANES, LANES), 1))
    lstart = SUBLANES * _bdot(jnp.broadcast_to(cnt8, (SUBLANES, LANES)), upper.astype(BF16))[0:1]
    slot = lstart + before
    lp1 = jnp.sum(oh1 * slot, axis=-1, keepdims=True)
    lp2 = jnp.sum(oh2 * slot, axis=-1, keepdims=True)
    vcol = valid[:, 0:1]
    lp = jnp.where(lane == 0, jnp.where(vcol, lp1, -1.0), jnp.where(lane == 1, jnp.where(vcol, lp2, -1.0), 0.0))
    base = carry[...]
    carry[...] = base + SUBLANES * cnt8
    return lp, (SUBLANES * cnt8).astype(I32), lstart.astype(I32), base.astype(I32)


def _outproj_kernel(gtp_ref, gts_ref, osp_ref, oss_ref, sa_ref, sb_ref, xp_ref, xs_ref, g1p_ref, shp_ref, scp_ref,
                    g1s_ref, shs_ref, scs_ref, n2_ref, wur_ref, wus_ref, wo_ref, wrh_ref, wrl_ref, br_ref,
                    x1_o, h2_o, rt_o, lp_o, cnt_o, ls_o, gb_o, carry, *, npt, ns):
    i = pl.program_id(0)
    is_s = i >= npt

    @pl.when(i == 0)
    def _():
        carry[...] = jnp.zeros_like(carry)

    x = jnp.where(is_s, xs_ref[...], xp_ref[...])
    g1 = jnp.where(is_s, g1s_ref[...], g1p_ref[...])
    sh = jnp.where(is_s, shs_ref[...], shp_ref[...])
    sc = jnp.where(is_s, scs_ref[...], scp_ref[...])
    gated = jnp.where(is_s, gts_ref[...], gtp_ref[...])
    oswa = jnp.where(is_s, oss_ref[...], osp_ref[...])
    y_ret = jnp.dot(gated, wur_ref[...], preferred_element_type=F32)
    y_swa = jnp.dot(oswa, wus_ref[...], preferred_element_type=F32)
    merged = sa_ref[...].astype(F32) * y_ret + sb_ref[...].astype(F32) * y_swa
    x1 = x + g1 * jnp.dot(merged.astype(BF16), wo_ref[...], preferred_element_type=F32)
    x1_o[...] = x1
    h2 = _rms(x1, n2_ref[...]) * (1.0 + sc) + sh
    hi = h2.astype(BF16)
    h2_o[...] = hi
    lo = (h2 - hi.astype(F32)).astype(BF16)
    wrh = wrh_ref[...]
    logits = (jnp.dot(hi, wrh, preferred_element_type=F32) + jnp.dot(lo, wrh, preferred_element_type=F32)
              + jnp.dot(hi, wrl_ref[...], preferred_element_type=F32) + br_ref[...])
    e1, e2, w1, w2 = _route(logits)
    lane = lax.broadcasted_iota(I32, logits.shape, 1)
    rt_o[...] = jnp.where(lane == 2, w1, jnp.where(lane == 3, w2, 0.0))
    valid = jnp.logical_not(is_s) | (lax.broadcasted_iota(I32, logits.shape, 0) < ns)
    lp_o[...], cnt_o[...], ls_o[...], gb_o[...] = _plan_tile(e1, e2, valid, carry)


def _outproj(gated_p, gated_s, oswa_p, oswa_s, siga, sigb, xp, xs_pad, modp, mods, n2, wur, wus, wo, wr, br,
             *, npt, tpb, nb, ns):
    d = xp.shape[1]
    nt = npt + 1
    nrow = nt * TM
    row = lambda i: (i, 0)
    ptile = lambda i: (jnp.minimum(i, npt - 1), 0)
    pbatch = lambda col: (lambda i: (jnp.minimum(i // tpb, nb - 1), 0, col))
    scol = lambda col: (lambda i: (0, col))
    const = lambda i: (0, 0)
    wr_hi = wr.astype(BF16)
    wr_lo = (wr - wr_hi.astype(F32)).astype(BF16)
    meta = jax.ShapeDtypeStruct((nt, 1, LANES), I32)
    mspec = pl.BlockSpec((None, 1, LANES), lambda i: (i, 0, 0))
    return pl.pallas_call(
        functools.partial(_outproj_kernel, npt=npt, ns=ns),
        out_shape=[jax.ShapeDtypeStruct((nrow, d), F32),
                   jax.ShapeDtypeStruct((nrow, d), BF16),
                   jax.ShapeDtypeStruct((nrow, LANES), F32),
                   jax.ShapeDtypeStruct((nrow, LANES), F32), meta, meta, meta],
        grid=(nt,),
        in_specs=[pl.BlockSpec((TM, _RET_W), ptile), pl.BlockSpec((TM, _RET_W), const),
                  pl.BlockSpec((TM, _SWA_QW), ptile), pl.BlockSpec((TM, _SWA_QW), const),
                  pl.BlockSpec((TM, d), row), pl.BlockSpec((TM, d), row),
                  pl.BlockSpec((TM, d), ptile), pl.BlockSpec((TM, d), const),
                  pl.BlockSpec((None, 1, d), pbatch(2)), pl.BlockSpec((None, 1, d), pbatch(3)),
                  pl.BlockSpec((None, 1, d), pbatch(4)),
                  pl.BlockSpec((TM, d), scol(2)), pl.BlockSpec((TM, d), scol(3)), pl.BlockSpec((TM, d), scol(4)),
                  pl.BlockSpec((1, d), const),
                  pl.BlockSpec(wur.shape, const), pl.BlockSpec(wus.shape, const), pl.BlockSpec(wo.shape, const),
                  pl.BlockSpec(wr.shape, const), pl.BlockSpec(wr.shape, const), pl.BlockSpec((1, LANES), const)],
        out_specs=[pl.BlockSpec((TM, d), row), pl.BlockSpec((TM, d), row), pl.BlockSpec((TM, LANES), row),
                   pl.BlockSpec((TM, LANES), row), mspec, mspec, mspec],
        scratch_shapes=[pltpu.VMEM((1, LANES), F32)],
        compiler_params=_cparams("arbitrary"),
    )(gated_p, gated_s, oswa_p, oswa_s, siga, sigb, xp, xs_pad, modp, modp, modp, mods, mods, mods, n2,
      wur, wus, wo, wr_hi, wr_lo, br)


_HI_MASK = -65536


def _pack_rows(x):
    bits = pltpu.bitcast(x, I32)
    half = x.shape[1] // 2
    low = bits[:, half:]
    return bits[:, :half] | lax.shift_right_logical(low, jnp.full(low.shape, 16, I32))


def _unpack_rows(w):
    first = pltpu.bitcast(w & _HI_MASK, F32).astype(BF16)
    second = pltpu.bitcast(lax.shift_left(w, jnp.full(w.shape, 16, I32)), F32).astype(BF16)
    return first, second


def _aligned(v):
    return v if isinstance(v, int) else pl.multiple_of(v, SUBLANES)


def _run_copy(src, dst, s_start, d_start, n, sem):
    s_start, d_start, n = _aligned(s_start), _aligned(d_start), _aligned(n)
    return pltpu.make_async_copy(src.at[pl.ds(s_start, n)], dst.at[pl.ds(d_start, n)], sem)


def _each_run(cnt_ref, step, fn):
    def body(e, c):
        k = step * N_EXPERTS + e

        @pl.when(cnt_ref[k] > 0)
        def _():
            fn(k)
        return c
    lax.fori_loop(0, N_EXPERTS, body, 0)


def _dispatch_kernel(cnt_ref, ls_ref, gd_ref, ps_ref, pn_ref, nu_ref, h_ref, lp_ref, xs_ref,
                     sorted_scr, zero_scr, sem, zsem, *, nt, maxt):
    i = pl.program_id(0)

    def pad(e):
        return _run_copy(zero_scr, xs_ref, 0, ps_ref[e], pn_ref[e], zsem)

    def tail(j):
        return _run_copy(zero_scr, xs_ref, 0, j * TM, TM, zsem)

    def each_pad(fn):
        def body(e, c):
            @pl.when(pn_ref[e] > 0)
            def _():
                fn(pad(e))
            return c
        lax.fori_loop(0, N_EXPERTS, body, 0)

    def each_tail(fn):
        def body(j, c):
            fn(tail(j))
            return c
        lax.fori_loop(nu_ref[0], maxt, body, 0)

    @pl.when(i == 0)
    def _():
        zero_scr[...] = jnp.zeros_like(zero_scr)
        each_pad(lambda cp: cp.start())
        each_tail(lambda cp: cp.start())

    lpt = lp_ref[...].T
    slot = lax.broadcasted_iota(I32, (SLOTS, TM), 0).astype(F32)
    perm = ((slot == lpt[0:1]) | (slot == lpt[1:2])).astype(BF16)
    sorted_scr[i % 2] = _pack_rows(jnp.dot(perm, h_ref[...], preferred_element_type=F32))

    def copy(step):
        return lambda k: _run_copy(sorted_scr.at[step % 2], xs_ref, ls_ref[k], gd_ref[k], cnt_ref[k], sem.at[step % 2])

    _each_run(cnt_ref, i, lambda k: copy(i)(k).start())

    @pl.when(i > 0)
    def _():
        _each_run(cnt_ref, i - 1, lambda k: copy(i - 1)(k).wait())

    @pl.when(i == nt - 1)
    def _():
        _each_run(cnt_ref, i, lambda k: copy(i)(k).wait())
        each_pad(lambda cp: cp.wait())
        each_tail(lambda cp: cp.wait())


def _dispatch(cnt, ls, gd, ps, pn, nu, h2, lp, *, nt, maxt):
    half = h2.shape[1] // 2
    return pl.pallas_call(
        functools.partial(_dispatch_kernel, nt=nt, maxt=maxt),
        out_shape=jax.ShapeDtypeStruct((maxt * TM, half), I32),
        grid_spec=pltpu.PrefetchScalarGridSpec(
            num_scalar_prefetch=6,
            grid=(nt,),
            in_specs=[pl.BlockSpec((TM, 2 * half), lambda i, *_: (i, 0)),
                      pl.BlockSpec((TM, LANES), lambda i, *_: (i, 0))],
            out_specs=pl.BlockSpec(memory_space=pl.ANY),
            scratch_shapes=[pltpu.VMEM((2, SLOTS, half), I32), pltpu.VMEM((TM, half), I32),
                            pltpu.SemaphoreType.DMA((2,)), pltpu.SemaphoreType.DMA(())]),
        compiler_params=_cparams("arbitrary"),
    )(cnt, ls, gd, ps, pn, nu, h2, lp)


def _experts_kernel(te_ref, nu_ref, x_ref, w1_ref, w3_ref, w2_ref, y_ref, w1b, w3b, w2b):
    j = pl.program_id(0)
    changed = (j == 0) | (te_ref[j] != te_ref[jnp.maximum(j - 1, 0)])

    @pl.when(changed)
    def _():
        w1b[...] = w1_ref[...].astype(BF16)
        w3b[...] = w3_ref[...].astype(BF16)
        w2b[...] = w2_ref[...].astype(BF16)

    @pl.when(j < nu_ref[0])
    def _():
        xa, xb = _unpack_rows(x_ref[...])
        half = xa.shape[1]
        a = (jnp.dot(xa, w1b[:half], preferred_element_type=F32)
             + jnp.dot(xb, w1b[half:], preferred_element_type=F32))
        b = (jnp.dot(xa, w3b[:half], preferred_element_type=F32)
             + jnp.dot(xb, w3b[half:], preferred_element_type=F32))
        y = jnp.dot((_silu(a) * b).astype(BF16), w2b[...], preferred_element_type=F32)
        y_ref[...] = _pack_rows(y.astype(BF16).astype(F32))

    @pl.when(j >= nu_ref[0])
    def _():
        y_ref[...] = jnp.zeros_like(y_ref)


def _experts(te, nu, xs, w1, w3, w2, *, maxt):
    half = xs.shape[1]
    d = 2 * half
    f = w1.shape[2]
    wsel = lambda j, te, nu: (te[j], 0, 0)
    return pl.pallas_call(
        _experts_kernel,
        out_shape=jax.ShapeDtypeStruct(xs.shape, I32),
        grid_spec=pltpu.PrefetchScalarGridSpec(
            num_scalar_prefetch=2,
            grid=(maxt,),
            in_specs=[pl.BlockSpec((TM, half), lambda j, te, nu: (jnp.minimum(j, nu[0] - 1), 0)),
                      pl.BlockSpec((None, d, f), wsel),
                      pl.BlockSpec((None, d, f), wsel),
                      pl.BlockSpec((None, f, d), wsel)],
            out_specs=pl.BlockSpec((TM, half), lambda j, te, nu: (j, 0)),
            scratch_shapes=[pltpu.VMEM((d, f), BF16), pltpu.VMEM((d, f), BF16), pltpu.VMEM((f, d), BF16)]),
        compiler_params=_cparams("arbitrary"),
    )(te, nu, xs, w1, w3, w2)


def _combine_kernel(cnt_ref, ls_ref, gd_ref, ys_ref, lp_ref, rt_ref, x1_ref, g2p_ref, g2s_ref, fg_ref,
                    yp_o, ys_o, ybuf, sem, *, npt):
    i = pl.program_id(0)
    nt = npt + 1

    def copy(step):
        return lambda k: _run_copy(ys_ref, ybuf.at[step % 2], gd_ref[k], ls_ref[k], cnt_ref[k], sem.at[step % 2])

    def fetch(step):
        ybuf[step % 2] = jnp.zeros(ybuf.shape[1:], I32)
        _each_run(cnt_ref, step, lambda k: copy(step)(k).start())

    @pl.when(i == 0)
    def _():
        fetch(0)

    @pl.when(i + 1 < nt)
    def _():
        fetch(i + 1)

    _each_run(cnt_ref, i, lambda k: copy(i)(k).wait())

    ya, yb = _unpack_rows(ybuf[i % 2])
    slot = lax.broadcasted_iota(I32, (TM, SLOTS), 1).astype(F32)
    lp = lp_ref[...]
    rt = rt_ref[...]

    def unsort(col):
        p = (slot == lp[:, col:col + 1]).astype(BF16)
        return jnp.concatenate([jnp.dot(p, ya, preferred_element_type=F32),
                                jnp.dot(p, yb, preferred_element_type=F32)], axis=1)

    moe = rt[:, 2:3] * unsort(0) + rt[:, 3:4] * unsort(1)
    g2 = jnp.where(i >= npt, g2s_ref[...], g2p_ref[...])
    y = _rms(x1_ref[...] + g2 * moe, fg_ref[...])

    @pl.when(i < npt)
    def _():
        yp_o[...] = y

    @pl.when(i >= npt)
    def _():
        ys_o[...] = y


def _combine(cnt, ls, gd, ys, lp, rt, x1, modp, mods, fg, *, npt, tpb, nb):
    d = x1.shape[1]
    nt = npt + 1
    row = lambda i, *_: (i, 0)
    return pl.pallas_call(
        functools.partial(_combine_kernel, npt=npt),
        out_shape=[jax.ShapeDtypeStruct((npt * TM, d), F32), jax.ShapeDtypeStruct((TM, d), F32)],
        grid_spec=pltpu.PrefetchScalarGridSpec(
            num_scalar_prefetch=3,
            grid=(nt,),
            in_specs=[pl.BlockSpec(memory_space=pl.ANY),
                      pl.BlockSpec((TM, LANES), row), pl.BlockSpec((TM, LANES), row), pl.BlockSpec((TM, d), row),
                      pl.BlockSpec((None, 1, d), lambda i, *_: (jnp.minimum(i // tpb, nb - 1), 0, 5)),
                      pl.BlockSpec((TM, d), lambda i, *_: (0, 5)),
                      pl.BlockSpec((1, d), lambda i, *_: (0, 0))],
            out_specs=[pl.BlockSpec((TM, d), lambda i, *_: (jnp.minimum(i, npt - 1), 0)),
                       pl.BlockSpec((TM, d), lambda i, *_: (0, 0))],
            scratch_shapes=[pltpu.VMEM((2, SLOTS, d // 2), I32), pltpu.SemaphoreType.DMA((2,))]),
        compiler_params=_cparams("arbitrary"),
    )(cnt, ls, gd, ys, lp, rt, x1, modp, mods, fg)


def _rotation_tables(t):
    pos = jnp.concatenate([jnp.arange(t, dtype=I32), jnp.full((TM,), PAST_LEN, I32)]).astype(F32)
    inv_r = 1.0 / (ROPE_THETA ** jnp.linspace(0.0, 1.0, RET_DK // 2, dtype=F32))
    ang_r = pos[:, None] * inv_r[None, :]
    sign_r = jnp.where(jnp.arange(RET_DK) % 2 == 0, -1.0, 1.0).astype(F32)
    cr = jnp.repeat(jnp.cos(ang_r), 2, axis=1)
    sr = jnp.repeat(jnp.sin(ang_r), 2, axis=1) * sign_r[None, :]
    inv_w = ROPE_THETA ** (-jnp.arange(0, SWA_HD, 2, dtype=F32) / SWA_HD)
    ang_w = pos[:, None] * inv_w[None, :]
    reps = LANES // (SWA_HD // 2)
    sign_w = jnp.where(jnp.arange(LANES) % SWA_HD < SWA_HD // 2, -1.0, 1.0).astype(F32)
    cw = jnp.tile(jnp.cos(ang_w), (1, reps))
    sw = jnp.tile(jnp.sin(ang_w), (1, reps)) * sign_w[None, :]
    return cr, sr, cw, sw


def kernel(x_prompt, x_sample, c_prompt, c_sample, state_ret, cache_swa_k, cache_swa_v, w_ada, b_ada, norm1_g, norm2_g, w_in, w_up_ret, w_up_swa, w_o, sink, w_rg, b_rg, w_re, b_re, w1, w3, w2, final_g):
    nb, t, d = x_prompt.shape
    ns, dec_seq, _ = x_sample.shape
    depth = w_ada.shape[0]
    assert depth == 1 and dec_seq == 1, "single layer, one new token per sequence"
    assert t % TM == 0 and ns <= TM and ns % 16 == 0 and d % (2 * LANES) == 0
    assert t % (RET_STEP_CHUNKS * RET_CHUNK) == 0 and t % (SWA_STEP_BLOCKS * WINDOW) == 0
    assert N_GROUPS + N_EXPERTS <= LANES
    w = cache_swa_k.shape[2]
    tpb = t // TM
    npt = nb * tpb
    nt = npt + 1
    np_rows = nb * t
    n_tok = np_rows + ns
    maxt = -(-(2 * n_tok + nt * N_EXPERTS * (SUBLANES - 1) + N_EXPERTS * (TM - 1)) // TM)

    xp = x_prompt.reshape(np_rows, d)
    xs_pad = jnp.pad(x_sample.reshape(ns, d), ((0, TM - ns), (0, 0)))

    c_all = jnp.concatenate([jnp.pad(c_prompt, ((0, SUBLANES - nb % SUBLANES), (0, 0))),
                             jnp.pad(c_sample, ((0, TM - ns), (0, 0)))])
    mod = _modulation(c_all, w_ada[0], b_ada[0])
    modp = mod[:nb].reshape(nb, 1, 6 * d)
    mods = mod[c_all.shape[0] - TM:]

    tabs = _rotation_tables(t)
    rq, rk, rv, rg, sq, sk, sv, siga, sigb = _inproj(
        xp, xs_pad, modp, mods, norm1_g, w_in[0].astype(BF16), tabs, npt=npt, tpb=tpb, nb=nb)

    dm, qd, kd, cd, gamma = _ret_tables()
    gated_p, st_p = _retention_prompt(rq, rk, rv, rg, (dm, qd, kd, cd), nb=nb, t=t)
    gated_s, st_s = _retention_sample(rq, rk, rv, rg, state_ret[0], gamma, row0=np_rows, ns=ns)
    oswa_p = _swa_prompt(sq, sk, sv, sink[0], nb=nb, t=t)
    ck = cache_swa_k[0].reshape(ns, w, _SWA_KW)
    cv = cache_swa_v[0].reshape(ns, w, _SWA_KW)
    oswa_s = _swa_sample(sq, sk, sv, ck, cv, sink[0], row0=np_rows, ns=ns)
    gated_s = jnp.pad(gated_s, ((0, TM - ns), (0, 0)))
    oswa_s = jnp.pad(oswa_s, ((0, TM - ns), (0, 0)))

    wr = jnp.pad(jnp.concatenate([w_rg[0], w_re[0]], axis=1), ((0, 0), (0, LANES - N_GROUPS - N_EXPERTS)))
    br = jnp.pad(jnp.concatenate([b_rg[0], b_re[0]]), (0, LANES - N_GROUPS - N_EXPERTS)).reshape(1, LANES)
    x1, h2, rt, lp, cnt, ls, gb = _outproj(
        gated_p, gated_s, oswa_p, oswa_s, siga, sigb, xp, xs_pad, modp, mods, norm2_g,
        w_up_ret[0].astype(BF16), w_up_swa[0].astype(BF16), w_o[0].astype(BF16), wr, br,
        npt=npt, tpb=tpb, nb=nb, ns=ns)
    cnt = cnt[:, 0, :N_EXPERTS]
    ls = ls[:, 0, :N_EXPERTS]
    gb = gb[:, 0, :N_EXPERTS]
    seg = jnp.sum(cnt, axis=0)
    tiles = (seg + TM - 1) // TM
    tile_end = jnp.cumsum(tiles)
    row_start = (tile_end - tiles) * TM
    gd = (gb + row_start[None, :]).reshape(-1)
    n_used = tile_end[-1:]
    jj = jnp.minimum(jnp.arange(maxt, dtype=I32), n_used[0] - 1)
    te = jnp.minimum(jnp.sum((tile_end[None, :] <= jj[:, None]).astype(I32), axis=1), N_EXPERTS - 1)
    cnt = cnt.reshape(-1)
    ls = ls.reshape(-1)
    n_used = n_used.astype(I32)
    xs = _dispatch(cnt, ls, gd, row_start + seg, tiles * TM - seg, n_used, h2, lp, nt=nt, maxt=maxt)
    ys = _experts(te, n_used, xs, w1[0], w3[0], w2[0], maxt=maxt)
    y_p, y_s = _combine(cnt, ls, gd, ys, lp, rt, x1, modp, mods, final_g.reshape(1, d), npt=npt, tpb=tpb, nb=nb)

    y_prompt = y_p.reshape(nb, t, d)
    y_sample = y_s[:ns].reshape(ns, 1, d)
    wk = min(WINDOW, t)
    last = lambda a: jnp.stack([a[(b + 1) * t - wk:(b + 1) * t] for b in range(nb)]).reshape(
        nb, wk, SWA_KV_HEADS, SWA_HD)
    skp, svp = last(sk), last(sv)
    shp = (ns, 1, SWA_KV_HEADS, SWA_HD)
    ks_new = jnp.concatenate([cache_swa_k[0], sk[np_rows:n_tok].reshape(shp)], axis=1)[:, 1:]
    vs_new = jnp.concatenate([cache_swa_v[0], sv[np_rows:n_tok].reshape(shp)], axis=1)[:, 1:]
    return (y_prompt, y_sample, st_p[None], st_s[None], skp[None], svp[None], ks_new[None], vs_new[None])
```

```python
import functools

import jax
import jax.numpy as jnp
from jax import lax
from jax.experimental import pallas as pl
from jax.experimental.pallas import tpu as pltpu

F32 = jnp.float32
BF16 = jnp.bfloat16
I32 = jnp.int32

PAST_LEN = 8192
RET_HEADS = 4
RET_DK = 128
RET_DV = 128
RET_CHUNK = 128
SWA_HEADS = 8
SWA_KV_HEADS = 2
SWA_HD = 64
WINDOW = 128
ROPE_THETA = 10000.0
N_GROUPS = 4
EXPERTS_PER_GROUP = 8
N_EXPERTS = N_GROUPS * EXPERTS_PER_GROUP
D_EXPERT = 256
NORM_EPS = 1e-6

LANES = 128
SUBLANES = 8
TM = 256
SLOTS = 2 * TM + 2 * LANES
TE = 512
TE_SUB = 256
VMEM_LIMIT = 56 * 1024 * 1024

_RET_W = RET_HEADS * RET_DK
_SWA_QW = SWA_HEADS * SWA_HD
_SWA_KW = SWA_KV_HEADS * SWA_HD


def _cparams(*sem):
    return pltpu.CompilerParams(dimension_semantics=sem, vmem_limit_bytes=VMEM_LIMIT)


def _sigmoid(x):
    return 1.0 / (1.0 + jnp.exp(-x))


def _silu(x):
    return x * _sigmoid(x)


def _bdot(a, b):
    return jnp.dot(a.astype(BF16), b.astype(BF16), preferred_element_type=F32)


def _bdot_nt(a, b):
    return lax.dot_general(a.astype(BF16), b.astype(BF16), (((1,), (1,)), ((), ())), preferred_element_type=F32)


def _mod_kernel(c_ref, w_ref, b_ref, o_ref):
    o_ref[...] = _bdot(_silu(c_ref[...]), w_ref[...]) + b_ref[...]


def _modulation(c_all, w_ada, b_ada):
    rows, d = c_all.shape
    n = w_ada.shape[1]
    return pl.pallas_call(
        _mod_kernel,
        out_shape=jax.ShapeDtypeStruct((rows, n), F32),
        grid=(n // d,),
        in_specs=[pl.BlockSpec((rows, d), lambda j: (0, 0)),
                  pl.BlockSpec((d, d), lambda j: (0, j)),
                  pl.BlockSpec((1, d), lambda j: (0, j))],
        out_specs=pl.BlockSpec((rows, d), lambda j: (0, j)),
        compiler_params=_cparams("arbitrary"),
    )(c_all, w_ada, b_ada.reshape(1, n))


def _rms(x, g):
    return x * lax.rsqrt(jnp.mean(x * x, axis=-1, keepdims=True) + NORM_EPS) * g


def _pair_rotate(z, cos, sin_signed):
    n = z.shape[-1]
    lane = lax.broadcasted_iota(I32, z.shape, 1)
    partner = jnp.where((lane & 1) == 0, pltpu.roll(z, n - 1, 1), pltpu.roll(z, 1, 1))
    reps = n // LANES
    cos = jnp.concatenate([cos] * reps, axis=1) if reps > 1 else cos
    sin_signed = jnp.concatenate([sin_signed] * reps, axis=1) if reps > 1 else sin_signed
    return z * cos + partner * sin_signed


def _half_rotate(z, cos, sin_signed):
    n = z.shape[-1]
    half = SWA_HD // 2
    lane = lax.broadcasted_iota(I32, z.shape, 1)
    partner = jnp.where((lane & (SWA_HD - 1)) < half, pltpu.roll(z, n - half, 1), pltpu.roll(z, half, 1))
    reps = n // LANES
    cos = jnp.concatenate([cos] * reps, axis=1) if reps > 1 else cos
    sin_signed = jnp.concatenate([sin_signed] * reps, axis=1) if reps > 1 else sin_signed
    return z * cos + partner * sin_signed


def _inproj_kernel(xp_ref, xs_ref, shp_ref, scp_ref, shs_ref, scs_ref, n1_ref, w_ref,
                   cr_ref, sr_ref, cw_ref, sw_ref,
                   rq_o, rk_o, rv_o, rg_o, sq_o, sk_o, sv_o, za_o, zb_o, *, npt):
    is_s = pl.program_id(0) >= npt
    x = jnp.where(is_s, xs_ref[...], xp_ref[...])
    sh = jnp.where(is_s, shs_ref[...], shp_ref[...])
    sc = jnp.where(is_s, scs_ref[...], scp_ref[...])
    h = (_rms(x, n1_ref[...]) * (1.0 + sc) + sh).astype(BF16)
    cr, sr, cw, sw = cr_ref[...], sr_ref[...], cw_ref[...], sw_ref[...]

    def seg(a, b):
        return jnp.dot(h, w_ref[:, a:b], preferred_element_type=F32)

    o = 0
    rq_o[...] = _pair_rotate(seg(o, o + _RET_W), cr, sr).astype(BF16)
    o += _RET_W
    rk_o[...] = (_pair_rotate(seg(o, o + _RET_W), cr, sr) * (RET_DK ** -0.5)).astype(BF16)
    o += _RET_W
    rv_o[...] = seg(o, o + _RET_W).astype(BF16)
    o += _RET_W
    rg_o[...] = _silu(seg(o, o + _RET_W)).astype(BF16)
    o += _RET_W
    sq_o[...] = (_half_rotate(seg(o, o + _SWA_QW), cw, sw) * (SWA_HD ** -0.5)).astype(BF16)
    o += _SWA_QW
    zkv = seg(o, o + 2 * _SWA_KW)
    sk_o[...] = _half_rotate(zkv[:, :_SWA_KW], cw, sw)
    sv_o[...] = zkv[:, _SWA_KW:]
    o += 2 * _SWA_KW
    d = x.shape[-1]
    za_o[...] = _sigmoid(seg(o, o + d)).astype(BF16)
    o += d
    zb_o[...] = _sigmoid(seg(o, o + d)).astype(BF16)


def _inproj(xp, xs_pad, modp, mods, n1, w_in_b, tabs, *, npt, tpb, nb):
    d = xp.shape[1]
    nt = npt + 1
    nrow = nt * TM
    n_in = w_in_b.shape[1]
    ptile = lambda i: (jnp.minimum(i, npt - 1), 0)
    pbatch = lambda col: (lambda i: (jnp.minimum(i // tpb, nb - 1), 0, col))
    tab_idx = lambda i: (jnp.where(i < npt, i % tpb, tpb), 0)
    out_cols = [(_RET_W, BF16)] * 4 + [(_SWA_QW, BF16), (_SWA_KW, F32), (_SWA_KW, F32), (d, BF16), (d, BF16)]
    return pl.pallas_call(
        functools.partial(_inproj_kernel, npt=npt),
        out_shape=[jax.ShapeDtypeStruct((nrow, c), t) for c, t in out_cols],
        grid=(nt,),
        in_specs=[pl.BlockSpec((TM, d), ptile),
                  pl.BlockSpec((TM, d), lambda i: (0, 0)),
                  pl.BlockSpec((None, 1, d), pbatch(0)),
                  pl.BlockSpec((None, 1, d), pbatch(1)),
                  pl.BlockSpec((TM, d), lambda i: (0, 0)),
                  pl.BlockSpec((TM, d), lambda i: (0, 1)),
                  pl.BlockSpec((1, d), lambda i: (0, 0)),
                  pl.BlockSpec((d, n_in), lambda i: (0, 0))]
                 + [pl.BlockSpec((TM, LANES), tab_idx)] * 4,
        out_specs=[pl.BlockSpec((TM, c), lambda i: (i, 0)) for c, _ in out_cols],
        compiler_params=_cparams("arbitrary"),
    )(xp, xs_pad, modp, modp, mods, mods, n1, w_in_b, *tabs)


RET_STEP_CHUNKS = 4


def _ret_kernel(q_ref, k_ref, v_ref, g_ref, dm_ref, qd_ref, kd_ref, cd_ref, o_ref, st_ref, s_scr, *, nsteps):
    step = pl.program_id(1)

    @pl.when(step == 0)
    def _():
        s_scr[...] = jnp.zeros_like(s_scr)

    for h in range(RET_HEADS):
        sl = slice(h * RET_DK, (h + 1) * RET_DK)
        state = s_scr[h]
        for c in range(RET_STEP_CHUNKS):
            rows = slice(c * RET_CHUNK, (c + 1) * RET_CHUNK)
            q, k, v = q_ref[rows, sl], k_ref[rows, sl], v_ref[rows, sl]
            att = _bdot_nt(q, k) * dm_ref[h]
            o = _bdot(att, v) + _bdot(q.astype(F32) * qd_ref[h], state)
            kd = (k.astype(F32) * kd_ref[h]).astype(BF16)
            kv = lax.dot_general(kd, v, (((0,), (0,)), ((), ())), preferred_element_type=F32)
            state = cd_ref[h] * state + kv
            o = o * lax.rsqrt(jnp.mean(o * o, axis=-1, keepdims=True) + NORM_EPS)
            o_ref[rows, sl] = (o * g_ref[rows, sl].astype(F32)).astype(BF16)
        s_scr[h] = state

    @pl.when(step == nsteps - 1)
    def _():
        st_ref[...] = s_scr[...]


def _ret_tables():
    ld = jnp.log(1.0 - 2.0 ** (-5.0 - jnp.arange(RET_HEADS, dtype=F32)))
    idx = jnp.arange(RET_CHUNK, dtype=F32)
    diff = idx[:, None] - idx[None, :]
    causal = diff >= 0
    dmask = jnp.where(causal[None], jnp.exp(ld[:, None, None] * jnp.where(causal, diff, 0.0)[None]), 0.0)
    k_dec = jnp.exp(ld[None, :] * (RET_CHUNK - 1.0 - idx)[:, None])
    q_dec = jnp.exp(ld[None, :] * (idx + 1.0)[:, None])
    chunk_decay = jnp.exp(ld * RET_CHUNK)
    bc = lambda t: jnp.broadcast_to(t.T[:, :, None], (RET_HEADS, RET_CHUNK, RET_DV))
    cd = jnp.broadcast_to(chunk_decay[:, None, None], (RET_HEADS, 1, RET_DV))
    return dmask, bc(q_dec), bc(k_dec), cd, jnp.exp(ld)


def _retention_prompt(rq, rk, rv, rg, tabs, *, nb, t):
    rows = RET_STEP_CHUNKS * RET_CHUNK
    nsteps = t // rows
    dm, qd, kd, cd = tabs
    blk = lambda b, c: (b * nsteps + c, 0)
    full3 = lambda b, c: (0, 0, 0)
    return pl.pallas_call(
        functools.partial(_ret_kernel, nsteps=nsteps),
        out_shape=[jax.ShapeDtypeStruct((nb * t, _RET_W), BF16),
                   jax.ShapeDtypeStruct((nb, RET_HEADS, RET_DK, RET_DV), F32)],
        grid=(nb, nsteps),
        in_specs=[pl.BlockSpec((rows, _RET_W), blk)] * 4
                 + [pl.BlockSpec((RET_HEADS, RET_CHUNK, RET_DV), full3)] * 3
                 + [pl.BlockSpec((RET_HEADS, 1, RET_DV), full3)],
        out_specs=[pl.BlockSpec((rows, _RET_W), blk),
                   pl.BlockSpec((None, RET_HEADS, RET_DK, RET_DV), lambda b, c: (b, 0, 0, 0))],
        scratch_shapes=[pltpu.VMEM((RET_HEADS, RET_DK, RET_DV), F32)],
        compiler_params=_cparams("arbitrary", "arbitrary"),
    )(rq, rk, rv, rg, dm, qd, kd, cd)


def _ret_sample_kernel(gam_ref, q_ref, k_ref, v_ref, g_ref, s0_ref, o_ref, st_ref, *, sb):
    gamma = gam_ref[pl.program_id(1)]
    q = q_ref[...].astype(F32)
    k = k_ref[...].astype(F32)
    v = v_ref[...].astype(F32)
    rows = sb * RET_DK
    s2 = s0_ref[...].reshape(rows, RET_DV)
    col_b = lax.broadcasted_iota(I32, (sb, rows), 1) // RET_DK
    row_b = lax.broadcasted_iota(I32, (sb, rows), 0)
    qexp = jnp.where(col_b == row_b, jnp.concatenate([q * gamma] * sb, axis=1), 0.0)
    o = jnp.sum(q * k, axis=-1, keepdims=True) * v + _bdot(qexp, s2)
    o = o * lax.rsqrt(jnp.mean(o * o, axis=-1, keepdims=True) + NORM_EPS)
    o_ref[...] = (o * g_ref[...].astype(F32)).astype(BF16)
    rep = (lax.broadcasted_iota(I32, (rows, sb), 0) // RET_DK == lax.broadcasted_iota(I32, (rows, sb), 1))
    rep = rep.astype(BF16)
    krep = _bdot(rep, k)
    vrep = _bdot(rep, v)
    eye = (lax.broadcasted_iota(I32, (rows, RET_DK), 0) % RET_DK == lax.broadcasted_iota(I32, (rows, RET_DK), 1))
    kcol = jnp.sum(jnp.where(eye, krep, 0.0), axis=-1, keepdims=True)
    st_ref[...] = (gamma * s2 + kcol * vrep).reshape(sb, RET_DK, RET_DV)


def _retention_sample(rq, rk, rv, rg, s0, gamma, *, row0, ns):
    sb = min(64, ns)
    base = row0 // sb
    blk = lambda i, h: (base + i, h)
    sblk = lambda i, h: (i, h, 0, 0)
    return pl.pallas_call(
        functools.partial(_ret_sample_kernel, sb=sb),
        out_shape=[jax.ShapeDtypeStruct((ns, _RET_W), BF16),
                   jax.ShapeDtypeStruct(s0.shape, F32)],
        grid=(ns // sb, RET_HEADS),
        in_specs=[pl.BlockSpec(memory_space=pltpu.SMEM)]
                 + [pl.BlockSpec((sb, RET_DK), blk)] * 4
                 + [pl.BlockSpec((sb, None, RET_DK, RET_DV), sblk)],
        out_specs=[pl.BlockSpec((sb, RET_DV), lambda i, h: (i, h)),
                   pl.BlockSpec((sb, None, RET_DK, RET_DV), sblk)],
        compiler_params=_cparams("arbitrary", "arbitrary"),
    )(gamma, rq, rk, rv, rg, s0)


def _sink_softmax(s, mask, sink):
    s = jnp.where(mask, s, -jnp.inf)
    m = jnp.maximum(jnp.max(s, axis=-1, keepdims=True), sink)
    p = jnp.exp(s - m)
    return p / (jnp.sum(p, axis=-1, keepdims=True) + jnp.exp(sink - m))


def _split_kv_heads(x):
    lo = lax.broadcasted_iota(I32, x.shape, 1) < SWA_HD
    h0_lo = jnp.where(lo, x, 0.0)
    h1_hi = jnp.where(lo, 0.0, x)
    return ((h0_lo, pltpu.roll(h0_lo, SWA_HD, 1)), (pltpu.roll(h1_hi, SWA_HD, 1), h1_hi))


SWA_STEP_BLOCKS = 4


def _swa_kernel(sink_ref, q_ref, kc_ref, kp_ref, vc_ref, vp_ref, o_ref):
    n = pl.program_id(1)
    c = WINDOW
    kk = jnp.concatenate([kp_ref[...], kc_ref[...]], axis=0)
    vv = jnp.concatenate([vp_ref[...], vc_ref[...]], axis=0)
    ks = [[a.astype(BF16) for a in pair] for pair in _split_kv_heads(kk)]
    vs = [[a.astype(BF16) for a in pair] for pair in _split_kv_heads(vv)]
    qi = lax.broadcasted_iota(I32, (2 * c, 2 * c), 0) % c
    ki = lax.broadcasted_iota(I32, (2 * c, 2 * c), 1)
    band = (ki > qi) & (ki <= qi + c)
    top = lax.broadcasted_iota(I32, (2 * c, 1), 0) < c
    for s in range(SWA_STEP_BLOCKS):
        rows = slice(s * c, (s + 1) * c)
        keys = slice(s * c, (s + 2) * c)
        mask = (band & ((ki >= c) | (n > 0))) if s == 0 else band
        for kvh in range(SWA_KV_HEADS):
            j0, j1 = 2 * kvh, 2 * kvh + 1
            q2 = jnp.concatenate([q_ref[rows, j0 * LANES:(j0 + 1) * LANES],
                                  q_ref[rows, j1 * LANES:(j1 + 1) * LANES]], axis=0)
            kcat = jnp.concatenate([ks[kvh][0][keys], ks[kvh][1][keys]], axis=0)
            vcat = jnp.concatenate([vs[kvh][0][keys], vs[kvh][1][keys]], axis=0)
            sc = lax.dot_general(q2, kcat, (((1,), (1,)), ((), ())), preferred_element_type=F32)
            ps = []
            for half in range(2):
                sink = jnp.where(top, sink_ref[2 * j0 + half], sink_ref[2 * j1 + half])
                ps.append(_sink_softmax(sc[:, half * 2 * c:(half + 1) * 2 * c], mask, sink).astype(BF16))
            o = jnp.dot(jnp.concatenate(ps, axis=1), vcat, preferred_element_type=F32)
            o_ref[rows, j0 * LANES:(j0 + 1) * LANES] = o[:c].astype(BF16)
            o_ref[rows, j1 * LANES:(j1 + 1) * LANES] = o[c:].astype(BF16)


def _swa_prompt(sq, sk, sv, sink, *, nb, t):
    rows = SWA_STEP_BLOCKS * WINDOW
    nsteps = t // rows
    nblk = t // WINDOW
    cur = lambda b, n: (b * nsteps + n, 0)
    prev = lambda b, n: (b * nblk + jnp.maximum(n * SWA_STEP_BLOCKS - 1, 0), 0)
    return pl.pallas_call(
        _swa_kernel,
        out_shape=jax.ShapeDtypeStruct((nb * t, _SWA_QW), BF16),
        grid=(nb, nsteps),
        in_specs=[pl.BlockSpec(memory_space=pltpu.SMEM),
                  pl.BlockSpec((rows, _SWA_QW), cur),
                  pl.BlockSpec((rows, _SWA_KW), cur),
                  pl.BlockSpec((WINDOW, _SWA_KW), prev),
                  pl.BlockSpec((rows, _SWA_KW), cur),
                  pl.BlockSpec((WINDOW, _SWA_KW), prev)],
        out_specs=pl.BlockSpec((rows, _SWA_QW), cur),
        compiler_params=_cparams("arbitrary", "arbitrary"),
    )(sink, sq, sk, sk, sv, sv)


def _swa_sample_kernel(sink_ref, q_ref, kn_ref, vn_ref, kc_ref, vc_ref, o_ref, ko_ref, vo_ref, *, sb, w):
    pad = jnp.zeros((LANES - sb, _SWA_KW), F32)
    knt = jnp.concatenate([kn_ref[...], pad], axis=0).T
    vnt = jnp.concatenate([vn_ref[...], pad], axis=0).T
    kall = jnp.concatenate([kc_ref[b] for b in range(sb)] + [knt], axis=1)
    vall = jnp.concatenate([vc_ref[b] for b in range(sb)] + [vnt], axis=1)
    ncol = sb * w + LANES
    lo = lax.broadcasted_iota(I32, (sb, LANES), 1) < SWA_HD
    group = SWA_HEADS // SWA_KV_HEADS
    pieces = []
    for h in range(SWA_HEADS):
        slab = q_ref[:, (h // 2) * LANES:(h // 2 + 1) * LANES].astype(F32)
        mine = jnp.where(lo, slab, 0.0) if h % 2 == 0 else jnp.where(lo, 0.0, slab)
        pieces.append(mine if (h % 2) == (h // group) else pltpu.roll(mine, SWA_HD, 1))
    qrows = jnp.concatenate(pieces, axis=0)
    nrow = SWA_HEADS * sb
    s = _bdot(qrows, kall)
    rb = lax.broadcasted_iota(I32, (nrow, ncol), 0) % sb
    ci = lax.broadcasted_iota(I32, (nrow, ncol), 1)
    in_cache = (ci < sb * w) & (ci // w == rb) & ((w - ci % w) < WINDOW)
    mask = in_cache | (ci == sb * w + rb)
    sink_col = jnp.concatenate([jnp.full((sb, 1), sink_ref[h], F32) for h in range(SWA_HEADS)], axis=0)
    p = _sink_softmax(s, mask, sink_col)
    o = _bdot_nt(p, vall)
    for j in range(SWA_HEADS // 2):
        acc = jnp.zeros((sb, LANES), F32)
        for half in range(2):
            h = 2 * j + half
            oh = o[h * sb:(h + 1) * sb]
            own = jnp.where(lo, oh, 0.0) if h // group == 0 else jnp.where(lo, 0.0, oh)
            acc = acc + (own if (h // group) == half else pltpu.roll(own, SWA_HD, 1))
        o_ref[:, j * LANES:(j + 1) * LANES] = acc.astype(BF16)
    newest = lax.broadcasted_iota(I32, (_SWA_KW, w), 1) == w - 1
    for b in range(sb):
        ko_ref[b] = jnp.where(newest, knt[:, b:b + 1], pltpu.roll(kc_ref[b], w - 1, 1))
        vo_ref[b] = jnp.where(newest, vnt[:, b:b + 1], pltpu.roll(vc_ref[b], w - 1, 1))


def _swa_sample(sq, sk, sv, cache_kt, cache_vt, sink, *, row0, ns):
    sb = min(16, ns)
    w = cache_kt.shape[2]
    base = row0 // sb
    blk = lambda i: (base + i, 0)
    cblk = lambda i: (i, 0, 0)
    cspec = pl.BlockSpec((sb, _SWA_KW, w), cblk)
    return pl.pallas_call(
        functools.partial(_swa_sample_kernel, sb=sb, w=w),
        out_shape=[jax.ShapeDtypeStruct((ns, _SWA_QW), BF16),
                   jax.ShapeDtypeStruct(cache_kt.shape, F32), jax.ShapeDtypeStruct(cache_vt.shape, F32)],
        grid=(ns // sb,),
        in_specs=[pl.BlockSpec(memory_space=pltpu.SMEM),
                  pl.BlockSpec((sb, _SWA_QW), blk),
                  pl.BlockSpec((sb, _SWA_KW), blk),
                  pl.BlockSpec((sb, _SWA_KW), blk),
                  cspec, cspec],
        out_specs=[pl.BlockSpec((sb, _SWA_QW), lambda i: (i, 0)), cspec, cspec],
        compiler_params=_cparams("arbitrary"),
    )(sink, sq, sk, sv, cache_kt, cache_vt)


def _route(logits):
    lane = lax.broadcasted_iota(I32, logits.shape, 1)
    big = jnp.int32(1 << 20)
    neg = -jnp.inf

    def top(mask):
        v = jnp.max(jnp.where(mask, logits, neg), axis=-1, keepdims=True)
        i = jnp.min(jnp.where(mask & (logits == v), lane, big), axis=-1, keepdims=True)
        return v, i

    gmask = lane < N_GROUPS
    gmax, gsel = top(gmask)
    p_group = 1.0 / jnp.sum(jnp.where(gmask, jnp.exp(logits - gmax), 0.0), axis=-1, keepdims=True)
    first = N_GROUPS + gsel * EXPERTS_PER_GROUP
    emask = (lane >= first) & (lane < first + EXPERTS_PER_GROUP)
    v1, i1 = top(emask)
    v2, i2 = top(emask & (lane != i1))
    t = jnp.exp(v2 - v1)
    w1 = p_group / (1.0 + t)
    return i1 - N_GROUPS, i2 - N_GROUPS, w1, w1 * t


def _plan_tile(e1, e2, valid, carry):
    lane = lax.broadcasted_iota(I32, (TM, LANES), 1)
    oh1 = ((lane == e1) & valid).astype(F32)
    oh2 = ((lane == e2) & valid).astype(F32)
    oh = oh1 + oh2
    tri = (lax.broadcasted_iota(I32, (TM, TM), 0) > lax.broadcasted_iota(I32, (TM, TM), 1)).astype(BF16)
    before = _bdot(tri, oh)
    cnt = jnp.sum(oh, axis=0, keepdims=True)
    cnt8 = jnp.floor((cnt + (SUBLANES - 1)) * (1.0 / SUBLANES))
    upper = (lax.broadcasted_iota(I32, (LANES, LANES), 0) < lax.broadcasted_iota(I32, (LANES, LANES), 1))
    lstart = SUBLANES * _bdot(jnp.broadcast_to(cnt8, (SUBLANES, LANES)), upper.astype(BF16))[0:1]
    slot = lstart + before
    lp1 = jnp.sum(oh1 * slot, axis=-1, keepdims=True)
    lp2 = jnp.sum(oh2 * slot, axis=-1, keepdims=True)
    vcol = valid[:, 0:1]
    lp = jnp.where(lane == 0, jnp.where(vcol, lp1, -1.0), jnp.where(lane == 1, jnp.where(vcol, lp2, -1.0), 0.0))
    base = carry[...]
    carry[...] = base + SUBLANES * cnt8
    return lp, (SUBLANES * cnt8).astype(I32), lstart.astype(I32), base.astype(I32)


def _outproj_kernel(gtp_ref, gts_ref, osp_ref, oss_ref, sa_ref, sb_ref, xp_ref, xs_ref, g1p_ref, shp_ref, scp_ref,
                    g1s_ref, shs_ref, scs_ref, n2_ref, wur_ref, wus_ref, wo_ref, wrh_ref, wrl_ref, br_ref,
                    x1_o, h2_o, rt_o, lp_o, cnt_o, ls_o, gb_o, carry, *, npt, ns):
    i = pl.program_id(0)
    is_s = i >= npt

    @pl.when(i == 0)
    def _():
        carry[...] = jnp.zeros_like(carry)

    x = jnp.where(is_s, xs_ref[...], xp_ref[...])
    g1 = jnp.where(is_s, g1s_ref[...], g1p_ref[...])
    sh = jnp.where(is_s, shs_ref[...], shp_ref[...])
    sc = jnp.where(is_s, scs_ref[...], scp_ref[...])
    gated = jnp.where(is_s, gts_ref[...], gtp_ref[...])
    oswa = jnp.where(is_s, oss_ref[...], osp_ref[...])
    y_ret = jnp.dot(gated, wur_ref[...], preferred_element_type=F32)
    y_swa = jnp.dot(oswa, wus_ref[...], preferred_element_type=F32)
    merged = sa_ref[...].astype(F32) * y_ret + sb_ref[...].astype(F32) * y_swa
    x1 = x + g1 * jnp.dot(merged.astype(BF16), wo_ref[...], preferred_element_type=F32)
    x1_o[...] = x1
    h2 = _rms(x1, n2_ref[...]) * (1.0 + sc) + sh
    hi = h2.astype(BF16)
    h2_o[...] = hi
    lo = (h2 - hi.astype(F32)).astype(BF16)
    wrh = wrh_ref[...]
    logits = (jnp.dot(hi, wrh, preferred_element_type=F32) + jnp.dot(lo, wrh, preferred_element_type=F32)
              + jnp.dot(hi, wrl_ref[...], preferred_element_type=F32) + br_ref[...])
    e1, e2, w1, w2 = _route(logits)
    lane = lax.broadcasted_iota(I32, logits.shape, 1)
    rt_o[...] = jnp.where(lane == 2, w1, jnp.where(lane == 3, w2, 0.0))
    valid = jnp.logical_not(is_s) | (lax.broadcasted_iota(I32, logits.shape, 0) < ns)
    lp_o[...], cnt_o[...], ls_o[...], gb_o[...] = _plan_tile(e1, e2, valid, carry)


def _outproj(gated_p, gated_s, oswa_p, oswa_s, siga, sigb, xp, xs_pad, modp, mods, n2, wur, wus, wo, wr, br,
             *, npt, tpb, nb, ns):
    d = xp.shape[1]
    nt = npt + 1
    nrow = nt * TM
    row = lambda i: (i, 0)
    ptile = lambda i: (jnp.minimum(i, npt - 1), 0)
    pbatch = lambda col: (lambda i: (jnp.minimum(i // tpb, nb - 1), 0, col))
    scol = lambda col: (lambda i: (0, col))
    const = lambda i: (0, 0)
    wr_hi = wr.astype(BF16)
    wr_lo = (wr - wr_hi.astype(F32)).astype(BF16)
    meta = jax.ShapeDtypeStruct((nt, 1, LANES), I32)
    mspec = pl.BlockSpec((None, 1, LANES), lambda i: (i, 0, 0))
    return pl.pallas_call(
        functools.partial(_outproj_kernel, npt=npt, ns=ns),
        out_shape=[jax.ShapeDtypeStruct((nrow, d), F32),
                   jax.ShapeDtypeStruct((nrow, d), BF16),
                   jax.ShapeDtypeStruct((nrow, LANES), F32),
                   jax.ShapeDtypeStruct((nrow, LANES), F32), meta, meta, meta],
        grid=(nt,),
        in_specs=[pl.BlockSpec((TM, _RET_W), ptile), pl.BlockSpec((TM, _RET_W), const),
                  pl.BlockSpec((TM, _SWA_QW), ptile), pl.BlockSpec((TM, _SWA_QW), const),
                  pl.BlockSpec((TM, d), row), pl.BlockSpec((TM, d), row),
                  pl.BlockSpec((TM, d), ptile), pl.BlockSpec((TM, d), const),
                  pl.BlockSpec((None, 1, d), pbatch(2)), pl.BlockSpec((None, 1, d), pbatch(3)),
                  pl.BlockSpec((None, 1, d), pbatch(4)),
                  pl.BlockSpec((TM, d), scol(2)), pl.BlockSpec((TM, d), scol(3)), pl.BlockSpec((TM, d), scol(4)),
                  pl.BlockSpec((1, d), const),
                  pl.BlockSpec(wur.shape, const), pl.BlockSpec(wus.shape, const), pl.BlockSpec(wo.shape, const),
                  pl.BlockSpec(wr.shape, const), pl.BlockSpec(wr.shape, const), pl.BlockSpec((1, LANES), const)],
        out_specs=[pl.BlockSpec((TM, d), row), pl.BlockSpec((TM, d), row), pl.BlockSpec((TM, LANES), row),
                   pl.BlockSpec((TM, LANES), row), mspec, mspec, mspec],
        scratch_shapes=[pltpu.VMEM((1, LANES), F32)],
        compiler_params=_cparams("arbitrary"),
    )(gated_p, gated_s, oswa_p, oswa_s, siga, sigb, xp, xs_pad, modp, modp, modp, mods, mods, mods, n2,
      wur, wus, wo, wr_hi, wr_lo, br)


U32 = jnp.uint32


def _pack_rows(x):
    half = x.shape[1] // 2
    return pltpu.pack_elementwise([x[:, :half], x[:, half:]], packed_dtype=BF16)


def _unpack_rows(w):
    return tuple(pltpu.unpack_elementwise(w, index=k, packed_dtype=BF16, unpacked_dtype=F32).astype(BF16)
                 for k in range(2))


def _aligned(v):
    return v if isinstance(v, int) else pl.multiple_of(v, SUBLANES)


def _run_copy(src, dst, s_start, d_start, n, sem):
    s_start, d_start, n = _aligned(s_start), _aligned(d_start), _aligned(n)
    return pltpu.make_async_copy(src.at[pl.ds(s_start, n)], dst.at[pl.ds(d_start, n)], sem)


def _each_run(cnt_ref, step, fn):
    def body(e, c):
        k = step * N_EXPERTS + e

        @pl.when(cnt_ref[k] > 0)
        def _():
            fn(k)
        return c
    lax.fori_loop(0, N_EXPERTS, body, 0)


def _dispatch_kernel(cnt_ref, ls_ref, gd_ref, ps_ref, pn_ref, nu_ref, h_ref, lp_ref, xs_ref,
                     sorted_scr, zero_scr, sem, zsem, *, nt, maxt):
    i = pl.program_id(0)

    def pad(e):
        return _run_copy(zero_scr, xs_ref, 0, ps_ref[e], pn_ref[e], zsem)

    def tail(j):
        return _run_copy(zero_scr, xs_ref, 0, j * TE, TE, zsem)

    def each_pad(fn):
        def body(e, c):
            @pl.when(pn_ref[e] > 0)
            def _():
                fn(pad(e))
            return c
        lax.fori_loop(0, N_EXPERTS, body, 0)

    def each_tail(fn):
        def body(j, c):
            fn(tail(j))
            return c
        lax.fori_loop(nu_ref[0], maxt, body, 0)

    @pl.when(i == 0)
    def _():
        zero_scr[...] = jnp.zeros_like(zero_scr)
        each_pad(lambda cp: cp.start())
        each_tail(lambda cp: cp.start())

    lpt = lp_ref[...].T
    slot = lax.broadcasted_iota(I32, (SLOTS, TM), 0).astype(F32)
    perm = ((slot == lpt[0:1]) | (slot == lpt[1:2])).astype(BF16)
    sorted_scr[i % 2] = _pack_rows(jnp.dot(perm, h_ref[...], preferred_element_type=F32))

    def copy(step):
        return lambda k: _run_copy(sorted_scr.at[step % 2], xs_ref, ls_ref[k], gd_ref[k], cnt_ref[k], sem.at[step % 2])

    _each_run(cnt_ref, i, lambda k: copy(i)(k).start())

    @pl.when(i > 0)
    def _():
        _each_run(cnt_ref, i - 1, lambda k: copy(i - 1)(k).wait())

    @pl.when(i == nt - 1)
    def _():
        _each_run(cnt_ref, i, lambda k: copy(i)(k).wait())
        each_pad(lambda cp: cp.wait())
        each_tail(lambda cp: cp.wait())


def _dispatch(cnt, ls, gd, ps, pn, nu, h2, lp, *, nt, maxt):
    half = h2.shape[1] // 2
    return pl.pallas_call(
        functools.partial(_dispatch_kernel, nt=nt, maxt=maxt),
        out_shape=jax.ShapeDtypeStruct((maxt * TE, half), U32),
        grid_spec=pltpu.PrefetchScalarGridSpec(
            num_scalar_prefetch=6,
            grid=(nt,),
            in_specs=[pl.BlockSpec((TM, 2 * half), lambda i, *_: (i, 0)),
                      pl.BlockSpec((TM, LANES), lambda i, *_: (i, 0))],
            out_specs=pl.BlockSpec(memory_space=pl.ANY),
            scratch_shapes=[pltpu.VMEM((2, SLOTS, half), U32), pltpu.VMEM((TE, half), U32),
                            pltpu.SemaphoreType.DMA((2,)), pltpu.SemaphoreType.DMA(())]),
        compiler_params=_cparams("arbitrary"),
    )(cnt, ls, gd, ps, pn, nu, h2, lp)


def _experts_kernel(te_ref, nu_ref, x_ref, w1_ref, w3_ref, w2_ref, y_ref, w1b, w3b, w2b):
    j = pl.program_id(0)
    changed = (j == 0) | (te_ref[j] != te_ref[jnp.maximum(j - 1, 0)])

    @pl.when(changed)
    def _():
        w1b[...] = w1_ref[...].astype(BF16)
        w3b[...] = w3_ref[...].astype(BF16)
        w2b[...] = w2_ref[...].astype(BF16)

    @pl.when(j < nu_ref[0])
    def _():
        half = x_ref.shape[1]
        for r in range(0, TE, TE_SUB):
            xa, xb = _unpack_rows(x_ref[r:r + TE_SUB])
            a = (jnp.dot(xa, w1b[:half], preferred_element_type=F32)
                 + jnp.dot(xb, w1b[half:], preferred_element_type=F32))
            b = (jnp.dot(xa, w3b[:half], preferred_element_type=F32)
                 + jnp.dot(xb, w3b[half:], preferred_element_type=F32))
            y = jnp.dot((_silu(a) * b).astype(BF16), w2b[...], preferred_element_type=F32)
            y_ref[r:r + TE_SUB] = _pack_rows(y)

    @pl.when(j >= nu_ref[0])
    def _():
        y_ref[...] = jnp.zeros_like(y_ref)


def _experts(te, nu, xs, w1, w3, w2, *, maxt):
    half = xs.shape[1]
    d = 2 * half
    f = w1.shape[2]
    wsel = lambda j, te, nu: (te[j], 0, 0)
    return pl.pallas_call(
        _experts_kernel,
        out_shape=jax.ShapeDtypeStruct(xs.shape, U32),
        grid_spec=pltpu.PrefetchScalarGridSpec(
            num_scalar_prefetch=2,
            grid=(maxt,),
            in_specs=[pl.BlockSpec((TE, half), lambda j, te, nu: (jnp.minimum(j, nu[0] - 1), 0)),
                      pl.BlockSpec((None, d, f), wsel),
                      pl.BlockSpec((None, d, f), wsel),
                      pl.BlockSpec((None, f, d), wsel)],
            out_specs=pl.BlockSpec((TE, half), lambda j, te, nu: (j, 0)),
            scratch_shapes=[pltpu.VMEM((d, f), BF16), pltpu.VMEM((d, f), BF16), pltpu.VMEM((f, d), BF16)]),
        compiler_params=_cparams("arbitrary"),
    )(te, nu, xs, w1, w3, w2)


def _combine_kernel(cnt_ref, ls_ref, gd_ref, ys_ref, lp_ref, rt_ref, x1_ref, g2p_ref, g2s_ref, fg_ref,
                    yp_o, ys_o, ybuf, sem, *, npt):
    i = pl.program_id(0)
    nt = npt + 1

    def copy(step):
        return lambda k: _run_copy(ys_ref, ybuf.at[step % 2], gd_ref[k], ls_ref[k], cnt_ref[k], sem.at[step % 2])

    def fetch(step):
        ybuf[step % 2] = jnp.zeros(ybuf.shape[1:], U32)
        _each_run(cnt_ref, step, lambda k: copy(step)(k).start())

    @pl.when(i == 0)
    def _():
        fetch(0)

    @pl.when(i + 1 < nt)
    def _():
        fetch(i + 1)

    _each_run(cnt_ref, i, lambda k: copy(i)(k).wait())

    ya, yb = _unpack_rows(ybuf[i % 2])
    slot = lax.broadcasted_iota(I32, (TM, SLOTS), 1).astype(F32)
    lp = lp_ref[...]
    rt = rt_ref[...]

    def unsort(col):
        p = (slot == lp[:, col:col + 1]).astype(BF16)
        return jnp.concatenate([jnp.dot(p, ya, preferred_element_type=F32),
                                jnp.dot(p, yb, preferred_element_type=F32)], axis=1)

    moe = rt[:, 2:3] * unsort(0) + rt[:, 3:4] * unsort(1)
    g2 = jnp.where(i >= npt, g2s_ref[...], g2p_ref[...])
    y = _rms(x1_ref[...] + g2 * moe, fg_ref[...])

    @pl.when(i < npt)
    def _():
        yp_o[...] = y

    @pl.when(i >= npt)
    def _():
        ys_o[...] = y


def _combine(cnt, ls, gd, ys, lp, rt, x1, modp, mods, fg, *, npt, tpb, nb):
    d = x1.shape[1]
    nt = npt + 1
    row = lambda i, *_: (i, 0)
    return pl.pallas_call(
        functools.partial(_combine_kernel, npt=npt),
        out_shape=[jax.ShapeDtypeStruct((npt * TM, d), F32), jax.ShapeDtypeStruct((TM, d), F32)],
        grid_spec=pltpu.PrefetchScalarGridSpec(
            num_scalar_prefetch=3,
            grid=(nt,),
            in_specs=[pl.BlockSpec(memory_space=pl.ANY),
                      pl.BlockSpec((TM, LANES), row), pl.BlockSpec((TM, LANES), row), pl.BlockSpec((TM, d), row),
                      pl.BlockSpec((None, 1, d), lambda i, *_: (jnp.minimum(i // tpb, nb - 1), 0, 5)),
                      pl.BlockSpec((TM, d), lambda i, *_: (0, 5)),
                      pl.BlockSpec((1, d), lambda i, *_: (0, 0))],
            out_specs=[pl.BlockSpec((TM, d), lambda i, *_: (jnp.minimum(i, npt - 1), 0)),
                       pl.BlockSpec((TM, d), lambda i, *_: (0, 0))],
            scratch_shapes=[pltpu.VMEM((2, SLOTS, d // 2), U32), pltpu.SemaphoreType.DMA((2,))]),
        compiler_params=_cparams("arbitrary"),
    )(cnt, ls, gd, ys, lp, rt, x1, modp, mods, fg)


def _rotation_tables(t):
    pos = jnp.concatenate([jnp.arange(t, dtype=I32), jnp.full((TM,), PAST_LEN, I32)]).astype(F32)
    inv_r = jnp.repeat(1.0 / (ROPE_THETA ** jnp.linspace(0.0, 1.0, RET_DK // 2, dtype=F32)), 2)
    ang_r = pos[:, None] * inv_r[None, :]
    sign_r = jnp.where(jnp.arange(RET_DK) % 2 == 0, -1.0, 1.0).astype(F32)
    inv_w = jnp.tile(ROPE_THETA ** (-jnp.arange(0, SWA_HD, 2, dtype=F32) / SWA_HD), LANES // (SWA_HD // 2))
    ang_w = pos[:, None] * inv_w[None, :]
    sign_w = jnp.where(jnp.arange(LANES) % SWA_HD < SWA_HD // 2, -1.0, 1.0).astype(F32)
    return jnp.cos(ang_r), jnp.sin(ang_r) * sign_r[None, :], jnp.cos(ang_w), jnp.sin(ang_w) * sign_w[None, :]


def kernel(x_prompt, x_sample, c_prompt, c_sample, state_ret, cache_swa_k, cache_swa_v, w_ada, b_ada, norm1_g, norm2_g, w_in, w_up_ret, w_up_swa, w_o, sink, w_rg, b_rg, w_re, b_re, w1, w3, w2, final_g):
    nb, t, d = x_prompt.shape
    ns, dec_seq, _ = x_sample.shape
    depth = w_ada.shape[0]
    assert depth == 1 and dec_seq == 1, "single layer, one new token per sequence"
    assert t % TM == 0 and ns <= TM and ns % 16 == 0 and d % (2 * LANES) == 0
    assert t % (RET_STEP_CHUNKS * RET_CHUNK) == 0 and t % (SWA_STEP_BLOCKS * WINDOW) == 0
    assert N_GROUPS + N_EXPERTS <= LANES
    w = cache_swa_k.shape[2]
    tpb = t // TM
    npt = nb * tpb
    nt = npt + 1
    np_rows = nb * t
    n_tok = np_rows + ns
    maxt = -(-(2 * n_tok + nt * N_EXPERTS * (SUBLANES - 1) + N_EXPERTS * (TE - 1)) // TE)

    xp = x_prompt.reshape(np_rows, d)
    xs_pad = jnp.pad(x_sample.reshape(ns, d), ((0, TM - ns), (0, 0)))

    c_all = jnp.concatenate([jnp.pad(c_prompt, ((0, SUBLANES - nb % SUBLANES), (0, 0))),
                             jnp.pad(c_sample, ((0, TM - ns), (0, 0)))])
    mod = _modulation(c_all, w_ada[0], b_ada[0])
    modp = mod[:nb].reshape(nb, 1, 6 * d)
    mods = mod[c_all.shape[0] - TM:]

    tabs = _rotation_tables(t)
    rq, rk, rv, rg, sq, sk, sv, siga, sigb = _inproj(
        xp, xs_pad, modp, mods, norm1_g, w_in[0].astype(BF16), tabs, npt=npt, tpb=tpb, nb=nb)

    dm, qd, kd, cd, gamma = _ret_tables()
    gated_p, st_p = _retention_prompt(rq, rk, rv, rg, (dm, qd, kd, cd), nb=nb, t=t)
    gated_s, st_s = _retention_sample(rq, rk, rv, rg, state_ret[0], gamma, row0=np_rows, ns=ns)
    oswa_p = _swa_prompt(sq, sk, sv, sink[0], nb=nb, t=t)
    to_t = lambda c: jnp.transpose(c[0], (0, 2, 3, 1)).reshape(ns, _SWA_KW, w)
    from_t = lambda c: jnp.transpose(c.reshape(ns, SWA_KV_HEADS, SWA_HD, w), (0, 3, 1, 2))[None]
    oswa_s, ks_new, vs_new = _swa_sample(sq, sk, sv, to_t(cache_swa_k), to_t(cache_swa_v), sink[0],
                                         row0=np_rows, ns=ns)
    gated_s = jnp.pad(gated_s, ((0, TM - ns), (0, 0)))
    oswa_s = jnp.pad(oswa_s, ((0, TM - ns), (0, 0)))

    wr = jnp.pad(jnp.concatenate([w_rg[0], w_re[0]], axis=1), ((0, 0), (0, LANES - N_GROUPS - N_EXPERTS)))
    br = jnp.pad(jnp.concatenate([b_rg[0], b_re[0]]), (0, LANES - N_GROUPS - N_EXPERTS)).reshape(1, LANES)
    x1, h2, rt, lp, cnt, ls, gb = _outproj(
        gated_p, gated_s, oswa_p, oswa_s, siga, sigb, xp, xs_pad, modp, mods, norm2_g,
        w_up_ret[0].astype(BF16), w_up_swa[0].astype(BF16), w_o[0].astype(BF16), wr, br,
        npt=npt, tpb=tpb, nb=nb, ns=ns)
    cnt = cnt[:, 0, :N_EXPERTS]
    ls = ls[:, 0, :N_EXPERTS]
    gb = gb[:, 0, :N_EXPERTS]
    seg = jnp.sum(cnt, axis=0)
    tiles = (seg + TE - 1) // TE
    tile_end = jnp.cumsum(tiles)
    row_start = (tile_end - tiles) * TE
    gd = (gb + row_start[None, :]).reshape(-1)
    n_used = tile_end[-1:]
    jj = jnp.minimum(jnp.arange(maxt, dtype=I32), n_used[0] - 1)
    te = jnp.minimum(jnp.sum((tile_end[None, :] <= jj[:, None]).astype(I32), axis=1), N_EXPERTS - 1)
    cnt = cnt.reshape(-1)
    ls = ls.reshape(-1)
    n_used = n_used.astype(I32)
    xs = _dispatch(cnt, ls, gd, row_start + seg, tiles * TE - seg, n_used, h2, lp, nt=nt, maxt=maxt)
    ys = _experts(te, n_used, xs, w1[0], w3[0], w2[0], maxt=maxt)
    y_p, y_s = _combine(cnt, ls, gd, ys, lp, rt, x1, modp, mods, final_g.reshape(1, d), npt=npt, tpb=tpb, nb=nb)

    y_prompt = y_p.reshape(nb, t, d)
    y_sample = y_s[:ns].reshape(ns, 1, d)
    wk = min(WINDOW, t)
    last = lambda a: jnp.stack([a[(b + 1) * t - wk:(b + 1) * t] for b in range(nb)]).reshape(
        nb, wk, SWA_KV_HEADS, SWA_HD)
    skp, svp = last(sk), last(sv)
    return (y_prompt, y_sample, st_p[None], st_s[None], skp[None], svp[None], from_t(ks_new), from_t(vs_new))
```

```python
import functools

import jax
import jax.numpy as jnp
from jax import lax
from jax.experimental import pallas as pl
from jax.experimental.pallas import tpu as pltpu

F32 = jnp.float32
BF16 = jnp.bfloat16
I32 = jnp.int32

PAST_LEN = 8192
RET_HEADS = 4
RET_DK = 128
RET_DV = 128
RET_CHUNK = 128
SWA_HEADS = 8
SWA_KV_HEADS = 2
SWA_HD = 64
WINDOW = 128
ROPE_THETA = 10000.0
N_GROUPS = 4
EXPERTS_PER_GROUP = 8
N_EXPERTS = N_GROUPS * EXPERTS_PER_GROUP
D_EXPERT = 256
NORM_EPS = 1e-6

LANES = 128
SUBLANES = 8
TM = 256
SLOTS = 2 * TM + 2 * LANES
TE = 512
TE_SUB = 256
VMEM_LIMIT = 56 * 1024 * 1024

_RET_W = RET_HEADS * RET_DK
_SWA_QW = SWA_HEADS * SWA_HD
_SWA_KW = SWA_KV_HEADS * SWA_HD


def _cparams(*sem):
    return pltpu.CompilerParams(dimension_semantics=sem, vmem_limit_bytes=VMEM_LIMIT)


def _sigmoid(x):
    return 1.0 / (1.0 + jnp.exp(-x))


def _silu(x):
    return x * _sigmoid(x)


def _bdot(a, b):
    return jnp.dot(a.astype(BF16), b.astype(BF16), preferred_element_type=F32)


def _bdot_nt(a, b):
    return lax.dot_general(a.astype(BF16), b.astype(BF16), (((1,), (1,)), ((), ())), preferred_element_type=F32)


def _mod_kernel(c_ref, w_ref, b_ref, o_ref):
    o_ref[...] = _bdot(_silu(c_ref[...]), w_ref[...]) + b_ref[...]


def _modulation(c_all, w_ada, b_ada):
    rows, d = c_all.shape
    n = w_ada.shape[1]
    return pl.pallas_call(
        _mod_kernel,
        out_shape=jax.ShapeDtypeStruct((rows, n), F32),
        grid=(n // d,),
        in_specs=[pl.BlockSpec((rows, d), lambda j: (0, 0)),
                  pl.BlockSpec((d, d), lambda j: (0, j)),
                  pl.BlockSpec((1, d), lambda j: (0, j))],
        out_specs=pl.BlockSpec((rows, d), lambda j: (0, j)),
        compiler_params=_cparams("arbitrary"),
    )(c_all, w_ada, b_ada.reshape(1, n))


def _rms(x, g):
    return x * lax.rsqrt(jnp.mean(x * x, axis=-1, keepdims=True) + NORM_EPS) * g


def _pair_rotate(z, cos, sin_signed):
    n = z.shape[-1]
    lane = lax.broadcasted_iota(I32, z.shape, 1)
    partner = jnp.where((lane & 1) == 0, pltpu.roll(z, n - 1, 1), pltpu.roll(z, 1, 1))
    reps = n // LANES
    cos = jnp.concatenate([cos] * reps, axis=1) if reps > 1 else cos
    sin_signed = jnp.concatenate([sin_signed] * reps, axis=1) if reps > 1 else sin_signed
    return z * cos + partner * sin_signed


def _half_rotate(z, cos, sin_signed):
    n = z.shape[-1]
    half = SWA_HD // 2
    lane = lax.broadcasted_iota(I32, z.shape, 1)
    partner = jnp.where((lane & (SWA_HD - 1)) < half, pltpu.roll(z, n - half, 1), pltpu.roll(z, half, 1))
    reps = n // LANES
    cos = jnp.concatenate([cos] * reps, axis=1) if reps > 1 else cos
    sin_signed = jnp.concatenate([sin_signed] * reps, axis=1) if reps > 1 else sin_signed
    return z * cos + partner * sin_signed


def _inproj_kernel(xp_ref, xs_ref, shp_ref, scp_ref, shs_ref, scs_ref, n1_ref, w_ref,
                   cr_ref, sr_ref, cw_ref, sw_ref,
                   rq_o, rk_o, rv_o, rg_o, sq_o, sk_o, sv_o, za_o, zb_o, *, npt):
    is_s = pl.program_id(0) >= npt
    x = jnp.where(is_s, xs_ref[...], xp_ref[...])
    sh = jnp.where(is_s, shs_ref[...], shp_ref[...])
    sc = jnp.where(is_s, scs_ref[...], scp_ref[...])
    h = (_rms(x, n1_ref[...]) * (1.0 + sc) + sh).astype(BF16)
    cr, sr, cw, sw = cr_ref[...], sr_ref[...], cw_ref[...], sw_ref[...]

    def seg(a, b):
        return jnp.dot(h, w_ref[:, a:b], preferred_element_type=F32)

    o = 0
    rq_o[...] = _pair_rotate(seg(o, o + _RET_W), cr, sr).astype(BF16)
    o += _RET_W
    rk_o[...] = (_pair_rotate(seg(o, o + _RET_W), cr, sr) * (RET_DK ** -0.5)).astype(BF16)
    o += _RET_W
    rv_o[...] = seg(o, o + _RET_W).astype(BF16)
    o += _RET_W
    rg_o[...] = _silu(seg(o, o + _RET_W)).astype(BF16)
    o += _RET_W
    sq_o[...] = (_half_rotate(seg(o, o + _SWA_QW), cw, sw) * (SWA_HD ** -0.5)).astype(BF16)
    o += _SWA_QW
    zkv = seg(o, o + 2 * _SWA_KW)
    sk_o[...] = _half_rotate(zkv[:, :_SWA_KW], cw, sw)
    sv_o[...] = zkv[:, _SWA_KW:]
    o += 2 * _SWA_KW
    d = x.shape[-1]
    za_o[...] = _sigmoid(seg(o, o + d)).astype(BF16)
    o += d
    zb_o[...] = _sigmoid(seg(o, o + d)).astype(BF16)


def _inproj(xp, xs_pad, modp, mods, n1, w_in_b, tabs, *, npt, tpb, nb):
    d = xp.shape[1]
    nt = npt + 1
    nrow = nt * TM
    n_in = w_in_b.shape[1]
    ptile = lambda i: (jnp.minimum(i, npt - 1), 0)
    pbatch = lambda col: (lambda i: (jnp.minimum(i // tpb, nb - 1), 0, col))
    tab_idx = lambda i: (jnp.where(i < npt, i % tpb, tpb), 0)
    out_cols = [(_RET_W, BF16)] * 4 + [(_SWA_QW, BF16), (_SWA_KW, F32), (_SWA_KW, F32), (d, BF16), (d, BF16)]
    return pl.pallas_call(
        functools.partial(_inproj_kernel, npt=npt),
        out_shape=[jax.ShapeDtypeStruct((nrow, c), t) for c, t in out_cols],
        grid=(nt,),
        in_specs=[pl.BlockSpec((TM, d), ptile),
                  pl.BlockSpec((TM, d), lambda i: (0, 0)),
                  pl.BlockSpec((None, 1, d), pbatch(0)),
                  pl.BlockSpec((None, 1, d), pbatch(1)),
                  pl.BlockSpec((TM, d), lambda i: (0, 0)),
                  pl.BlockSpec((TM, d), lambda i: (0, 1)),
                  pl.BlockSpec((1, d), lambda i: (0, 0)),
                  pl.BlockSpec((d, n_in), lambda i: (0, 0))]
                 + [pl.BlockSpec((TM, LANES), tab_idx)] * 4,
        out_specs=[pl.BlockSpec((TM, c), lambda i: (i, 0)) for c, _ in out_cols],
        compiler_params=_cparams("arbitrary"),
    )(xp, xs_pad, modp, modp, mods, mods, n1, w_in_b, *tabs)


RET_STEP_CHUNKS = 4


def _ret_kernel(q_ref, k_ref, v_ref, g_ref, dm_ref, qd_ref, kd_ref, cd_ref, o_ref, st_ref, s_scr, *, nsteps):
    step = pl.program_id(1)

    @pl.when(step == 0)
    def _():
        s_scr[...] = jnp.zeros_like(s_scr)

    for h in range(RET_HEADS):
        sl = slice(h * RET_DK, (h + 1) * RET_DK)
        state = s_scr[h]
        for c in range(RET_STEP_CHUNKS):
            rows = slice(c * RET_CHUNK, (c + 1) * RET_CHUNK)
            q, k, v = q_ref[rows, sl], k_ref[rows, sl], v_ref[rows, sl]
            att = _bdot_nt(q, k) * dm_ref[h]
            o = _bdot(att, v) + _bdot(q.astype(F32) * qd_ref[h], state)
            kd = (k.astype(F32) * kd_ref[h]).astype(BF16)
            kv = lax.dot_general(kd, v, (((0,), (0,)), ((), ())), preferred_element_type=F32)
            state = cd_ref[h] * state + kv
            o = o * lax.rsqrt(jnp.mean(o * o, axis=-1, keepdims=True) + NORM_EPS)
            o_ref[rows, sl] = (o * g_ref[rows, sl].astype(F32)).astype(BF16)
        s_scr[h] = state

    @pl.when(step == nsteps - 1)
    def _():
        st_ref[...] = s_scr[...]


def _ret_tables():
    ld = jnp.log(1.0 - 2.0 ** (-5.0 - jnp.arange(RET_HEADS, dtype=F32)))
    idx = jnp.arange(RET_CHUNK, dtype=F32)
    diff = idx[:, None] - idx[None, :]
    causal = diff >= 0
    dmask = jnp.where(causal[None], jnp.exp(ld[:, None, None] * jnp.where(causal, diff, 0.0)[None]), 0.0)
    k_dec = jnp.exp(ld[None, :] * (RET_CHUNK - 1.0 - idx)[:, None])
    q_dec = jnp.exp(ld[None, :] * (idx + 1.0)[:, None])
    chunk_decay = jnp.exp(ld * RET_CHUNK)
    bc = lambda t: jnp.broadcast_to(t.T[:, :, None], (RET_HEADS, RET_CHUNK, RET_DV))
    cd = jnp.broadcast_to(chunk_decay[:, None, None], (RET_HEADS, 1, RET_DV))
    return dmask, bc(q_dec), bc(k_dec), cd, jnp.exp(ld)


def _retention_prompt(rq, rk, rv, rg, tabs, *, nb, t):
    rows = RET_STEP_CHUNKS * RET_CHUNK
    nsteps = t // rows
    dm, qd, kd, cd = tabs
    blk = lambda b, c: (b * nsteps + c, 0)
    full3 = lambda b, c: (0, 0, 0)
    return pl.pallas_call(
        functools.partial(_ret_kernel, nsteps=nsteps),
        out_shape=[jax.ShapeDtypeStruct((nb * t, _RET_W), BF16),
                   jax.ShapeDtypeStruct((nb, RET_HEADS, RET_DK, RET_DV), F32)],
        grid=(nb, nsteps),
        in_specs=[pl.BlockSpec((rows, _RET_W), blk)] * 4
                 + [pl.BlockSpec((RET_HEADS, RET_CHUNK, RET_DV), full3)] * 3
                 + [pl.BlockSpec((RET_HEADS, 1, RET_DV), full3)],
        out_specs=[pl.BlockSpec((rows, _RET_W), blk),
                   pl.BlockSpec((None, RET_HEADS, RET_DK, RET_DV), lambda b, c: (b, 0, 0, 0))],
        scratch_shapes=[pltpu.VMEM((RET_HEADS, RET_DK, RET_DV), F32)],
        compiler_params=_cparams("arbitrary", "arbitrary"),
    )(rq, rk, rv, rg, dm, qd, kd, cd)


def _ret_sample_kernel(gam_ref, q_ref, k_ref, v_ref, g_ref, s0_ref, o_ref, st_ref, *, sb):
    gamma = gam_ref[pl.program_id(1)]
    q = q_ref[...].astype(F32)
    k = k_ref[...].astype(F32)
    v = v_ref[...].astype(F32)
    rows = sb * RET_DK
    s2 = s0_ref[...].reshape(rows, RET_DV)
    col_b = lax.broadcasted_iota(I32, (sb, rows), 1) // RET_DK
    row_b = lax.broadcasted_iota(I32, (sb, rows), 0)
    qexp = jnp.where(col_b == row_b, jnp.concatenate([q * gamma] * sb, axis=1), 0.0)
    o = jnp.sum(q * k, axis=-1, keepdims=True) * v + _bdot(qexp, s2)
    o = o * lax.rsqrt(jnp.mean(o * o, axis=-1, keepdims=True) + NORM_EPS)
    o_ref[...] = (o * g_ref[...].astype(F32)).astype(BF16)
    rep = (lax.broadcasted_iota(I32, (rows, sb), 0) // RET_DK == lax.broadcasted_iota(I32, (rows, sb), 1))
    rep = rep.astype(BF16)
    krep = _bdot(rep, k)
    vrep = _bdot(rep, v)
    eye = (lax.broadcasted_iota(I32, (rows, RET_DK), 0) % RET_DK == lax.broadcasted_iota(I32, (rows, RET_DK), 1))
    kcol = jnp.sum(jnp.where(eye, krep, 0.0), axis=-1, keepdims=True)
    st_ref[...] = (gamma * s2 + kcol * vrep).reshape(sb, RET_DK, RET_DV)


def _retention_sample(rq, rk, rv, rg, s0, gamma, *, row0, ns):
    sb = min(64, ns)
    base = row0 // sb
    blk = lambda i, h: (base + i, h)
    sblk = lambda i, h: (i, h, 0, 0)
    return pl.pallas_call(
        functools.partial(_ret_sample_kernel, sb=sb),
        out_shape=[jax.ShapeDtypeStruct((ns, _RET_W), BF16),
                   jax.ShapeDtypeStruct(s0.shape, F32)],
        grid=(ns // sb, RET_HEADS),
        in_specs=[pl.BlockSpec(memory_space=pltpu.SMEM)]
                 + [pl.BlockSpec((sb, RET_DK), blk)] * 4
                 + [pl.BlockSpec((sb, None, RET_DK, RET_DV), sblk)],
        out_specs=[pl.BlockSpec((sb, RET_DV), lambda i, h: (i, h)),
                   pl.BlockSpec((sb, None, RET_DK, RET_DV), sblk)],
        compiler_params=_cparams("arbitrary", "arbitrary"),
    )(gamma, rq, rk, rv, rg, s0)


def _sink_softmax(s, mask, sink):
    s = jnp.where(mask, s, -jnp.inf)
    m = jnp.maximum(jnp.max(s, axis=-1, keepdims=True), sink)
    p = jnp.exp(s - m)
    return p / (jnp.sum(p, axis=-1, keepdims=True) + jnp.exp(sink - m))


def _split_kv_heads(x):
    lo = lax.broadcasted_iota(I32, x.shape, 1) < SWA_HD
    h0_lo = jnp.where(lo, x, 0.0)
    h1_hi = jnp.where(lo, 0.0, x)
    return ((h0_lo, pltpu.roll(h0_lo, SWA_HD, 1)), (pltpu.roll(h1_hi, SWA_HD, 1), h1_hi))


SWA_STEP_BLOCKS = 4


def _swa_kernel(sink_ref, q_ref, kc_ref, kp_ref, vc_ref, vp_ref, o_ref):
    n = pl.program_id(1)
    c = WINDOW
    kk = jnp.concatenate([kp_ref[...], kc_ref[...]], axis=0)
    vv = jnp.concatenate([vp_ref[...], vc_ref[...]], axis=0)
    ks = [[a.astype(BF16) for a in pair] for pair in _split_kv_heads(kk)]
    vs = [[a.astype(BF16) for a in pair] for pair in _split_kv_heads(vv)]
    qi = lax.broadcasted_iota(I32, (2 * c, 2 * c), 0) % c
    ki = lax.broadcasted_iota(I32, (2 * c, 2 * c), 1)
    band = (ki > qi) & (ki <= qi + c)
    top = lax.broadcasted_iota(I32, (2 * c, 1), 0) < c
    for s in range(SWA_STEP_BLOCKS):
        rows = slice(s * c, (s + 1) * c)
        keys = slice(s * c, (s + 2) * c)
        mask = (band & ((ki >= c) | (n > 0))) if s == 0 else band
        for kvh in range(SWA_KV_HEADS):
            j0, j1 = 2 * kvh, 2 * kvh + 1
            q2 = jnp.concatenate([q_ref[rows, j0 * LANES:(j0 + 1) * LANES],
                                  q_ref[rows, j1 * LANES:(j1 + 1) * LANES]], axis=0)
            kcat = jnp.concatenate([ks[kvh][0][keys], ks[kvh][1][keys]], axis=0)
            vcat = jnp.concatenate([vs[kvh][0][keys], vs[kvh][1][keys]], axis=0)
            sc = lax.dot_general(q2, kcat, (((1,), (1,)), ((), ())), preferred_element_type=F32)
            ps = []
            for half in range(2):
                sink = jnp.where(top, sink_ref[2 * j0 + half], sink_ref[2 * j1 + half])
                ps.append(_sink_softmax(sc[:, half * 2 * c:(half + 1) * 2 * c], mask, sink).astype(BF16))
            o = jnp.dot(jnp.concatenate(ps, axis=1), vcat, preferred_element_type=F32)
            o_ref[rows, j0 * LANES:(j0 + 1) * LANES] = o[:c].astype(BF16)
            o_ref[rows, j1 * LANES:(j1 + 1) * LANES] = o[c:].astype(BF16)


def _swa_prompt(sq, sk, sv, sink, *, nb, t):
    rows = SWA_STEP_BLOCKS * WINDOW
    nsteps = t // rows
    nblk = t // WINDOW
    cur = lambda b, n: (b * nsteps + n, 0)
    prev = lambda b, n: (b * nblk + jnp.maximum(n * SWA_STEP_BLOCKS - 1, 0), 0)
    return pl.pallas_call(
        _swa_kernel,
        out_shape=jax.ShapeDtypeStruct((nb * t, _SWA_QW), BF16),
        grid=(nb, nsteps),
        in_specs=[pl.BlockSpec(memory_space=pltpu.SMEM),
                  pl.BlockSpec((rows, _SWA_QW), cur),
                  pl.BlockSpec((rows, _SWA_KW), cur),
                  pl.BlockSpec((WINDOW, _SWA_KW), prev),
                  pl.BlockSpec((rows, _SWA_KW), cur),
                  pl.BlockSpec((WINDOW, _SWA_KW), prev)],
        out_specs=pl.BlockSpec((rows, _SWA_QW), cur),
        compiler_params=_cparams("arbitrary", "arbitrary"),
    )(sink, sq, sk, sk, sv, sv)


def _swa_sample_kernel(sink_ref, q_ref, kn_ref, vn_ref, kc_ref, vc_ref, o_ref, ko_ref, vo_ref, *, sb, w):
    pad = jnp.zeros((LANES - sb, _SWA_KW), F32)
    knt = jnp.concatenate([kn_ref[...], pad], axis=0).T
    vnt = jnp.concatenate([vn_ref[...], pad], axis=0).T
    kall = jnp.concatenate([kc_ref[b] for b in range(sb)] + [knt], axis=1)
    vall = jnp.concatenate([vc_ref[b] for b in range(sb)] + [vnt], axis=1)
    ncol = sb * w + LANES
    lo = lax.broadcasted_iota(I32, (sb, LANES), 1) < SWA_HD
    group = SWA_HEADS // SWA_KV_HEADS
    pieces = []
    for h in range(SWA_HEADS):
        slab = q_ref[:, (h // 2) * LANES:(h // 2 + 1) * LANES].astype(F32)
        mine = jnp.where(lo, slab, 0.0) if h % 2 == 0 else jnp.where(lo, 0.0, slab)
        pieces.append(mine if (h % 2) == (h // group) else pltpu.roll(mine, SWA_HD, 1))
    qrows = jnp.concatenate(pieces, axis=0)
    nrow = SWA_HEADS * sb
    s = _bdot(qrows, kall)
    rb = lax.broadcasted_iota(I32, (nrow, ncol), 0) % sb
    ci = lax.broadcasted_iota(I32, (nrow, ncol), 1)
    in_cache = (ci < sb * w) & (ci // w == rb) & ((w - ci % w) < WINDOW)
    mask = in_cache | (ci == sb * w + rb)
    sink_col = jnp.concatenate([jnp.full((sb, 1), sink_ref[h], F32) for h in range(SWA_HEADS)], axis=0)
    p = _sink_softmax(s, mask, sink_col)
    o = _bdot_nt(p, vall)
    for j in range(SWA_HEADS // 2):
        acc = jnp.zeros((sb, LANES), F32)
        for half in range(2):
            h = 2 * j + half
            oh = o[h * sb:(h + 1) * sb]
            own = jnp.where(lo, oh, 0.0) if h // group == 0 else jnp.where(lo, 0.0, oh)
            acc = acc + (own if (h // group) == half else pltpu.roll(own, SWA_HD, 1))
        o_ref[:, j * LANES:(j + 1) * LANES] = acc.astype(BF16)
    newest = lax.broadcasted_iota(I32, (_SWA_KW, w), 1) == w - 1
    for b in range(sb):
        ko_ref[b] = jnp.where(newest, knt[:, b:b + 1], pltpu.roll(kc_ref[b], w - 1, 1))
        vo_ref[b] = jnp.where(newest, vnt[:, b:b + 1], pltpu.roll(vc_ref[b], w - 1, 1))


def _swa_sample(sq, sk, sv, cache_kt, cache_vt, sink, *, row0, ns):
    sb = min(16, ns)
    w = cache_kt.shape[2]
    base = row0 // sb
    blk = lambda i: (base + i, 0)
    cblk = lambda i: (i, 0, 0)
    cspec = pl.BlockSpec((sb, _SWA_KW, w), cblk)
    return pl.pallas_call(
        functools.partial(_swa_sample_kernel, sb=sb, w=w),
        out_shape=[jax.ShapeDtypeStruct((ns, _SWA_QW), BF16),
                   jax.ShapeDtypeStruct(cache_kt.shape, F32), jax.ShapeDtypeStruct(cache_vt.shape, F32)],
        grid=(ns // sb,),
        in_specs=[pl.BlockSpec(memory_space=pltpu.SMEM),
                  pl.BlockSpec((sb, _SWA_QW), blk),
                  pl.BlockSpec((sb, _SWA_KW), blk),
                  pl.BlockSpec((sb, _SWA_KW), blk),
                  cspec, cspec],
        out_specs=[pl.BlockSpec((sb, _SWA_QW), lambda i: (i, 0)), cspec, cspec],
        compiler_params=_cparams("arbitrary"),
    )(sink, sq, sk, sv, cache_kt, cache_vt)


def _route(logits):
    lane = lax.broadcasted_iota(I32, logits.shape, 1)
    big = jnp.int32(1 << 20)
    neg = -jnp.inf

    def top(mask):
        v = jnp.max(jnp.where(mask, logits, neg), axis=-1, keepdims=True)
        i = jnp.min(jnp.where(mask & (logits == v), lane, big), axis=-1, keepdims=True)
        return v, i

    gmask = lane < N_GROUPS
    gmax, gsel = top(gmask)
    p_group = 1.0 / jnp.sum(jnp.where(gmask, jnp.exp(logits - gmax), 0.0), axis=-1, keepdims=True)
    first = N_GROUPS + gsel * EXPERTS_PER_GROUP
    emask = (lane >= first) & (lane < first + EXPERTS_PER_GROUP)
    v1, i1 = top(emask)
    v2, i2 = top(emask & (lane != i1))
    t = jnp.exp(v2 - v1)
    w1 = p_group / (1.0 + t)
    return i1 - N_GROUPS, i2 - N_GROUPS, w1, w1 * t


def _plan_tile(e1, e2, valid, carry):
    lane = lax.broadcasted_iota(I32, (TM, LANES), 1)
    oh1 = ((lane == e1) & valid).astype(F32)
    oh2 = ((lane == e2) & valid).astype(F32)
    oh = oh1 + oh2
    tri = (lax.broadcasted_iota(I32, (TM, TM), 0) > lax.broadcasted_iota(I32, (TM, TM), 1)).astype(BF16)
    before = _bdot(tri, oh)
    cnt = jnp.sum(oh, axis=0, keepdims=True)
    cnt8 = jnp.maximum(jnp.floor((cnt + (SUBLANES - 1)) * (1.0 / SUBLANES)), 1.0)
    upper = (lax.broadcasted_iota(I32, (LANES, LANES), 0) < lax.broadcasted_iota(I32, (LANES, LANES), 1))
    lstart = SUBLANES * _bdot(jnp.broadcast_to(cnt8, (SUBLANES, LANES)), upper.astype(BF16))[0:1]
    slot = lstart + before
    lp1 = jnp.sum(oh1 * slot, axis=-1, keepdims=True)
    lp2 = jnp.sum(oh2 * slot, axis=-1, keepdims=True)
    vcol = valid[:, 0:1]
    lp = jnp.where(lane == 0, jnp.where(vcol, lp1, -1.0), jnp.where(lane == 1, jnp.where(vcol, lp2, -1.0), 0.0))
    base = carry[...]
    carry[...] = base + SUBLANES * cnt8
    return lp, (SUBLANES * cnt8).astype(I32), lstart.astype(I32), base.astype(I32)


def _outproj_kernel(gtp_ref, gts_ref, osp_ref, oss_ref, sa_ref, sb_ref, xp_ref, xs_ref, g1p_ref, shp_ref, scp_ref,
                    g1s_ref, shs_ref, scs_ref, n2_ref, wur_ref, wus_ref, wo_ref, wrh_ref, wrl_ref, br_ref,
                    x1_o, h2_o, rt_o, lp_o, cnt_o, ls_o, gb_o, carry, *, npt, ns):
    i = pl.program_id(0)
    is_s = i >= npt

    @pl.when(i == 0)
    def _():
        carry[...] = jnp.zeros_like(carry)

    x = jnp.where(is_s, xs_ref[...], xp_ref[...])
    g1 = jnp.where(is_s, g1s_ref[...], g1p_ref[...])
    sh = jnp.where(is_s, shs_ref[...], shp_ref[...])
    sc = jnp.where(is_s, scs_ref[...], scp_ref[...])
    gated = jnp.where(is_s, gts_ref[...], gtp_ref[...])
    oswa = jnp.where(is_s, oss_ref[...], osp_ref[...])
    y_ret = jnp.dot(gated, wur_ref[...], preferred_element_type=F32)
    y_swa = jnp.dot(oswa, wus_ref[...], preferred_element_type=F32)
    merged = sa_ref[...].astype(F32) * y_ret + sb_ref[...].astype(F32) * y_swa
    x1 = x + g1 * jnp.dot(merged.astype(BF16), wo_ref[...], preferred_element_type=F32)
    x1_o[...] = x1
    h2 = _rms(x1, n2_ref[...]) * (1.0 + sc) + sh
    hi = h2.astype(BF16)
    h2_o[...] = hi
    lo = (h2 - hi.astype(F32)).astype(BF16)
    wrh = wrh_ref[...]
    logits = (jnp.dot(hi, wrh, preferred_element_type=F32) + jnp.dot(lo, wrh, preferred_element_type=F32)
              + jnp.dot(hi, wrl_ref[...], preferred_element_type=F32) + br_ref[...])
    e1, e2, w1, w2 = _route(logits)
    lane = lax.broadcasted_iota(I32, logits.shape, 1)
    rt_o[...] = jnp.where(lane == 2, w1, jnp.where(lane == 3, w2, 0.0))
    valid = jnp.logical_not(is_s) | (lax.broadcasted_iota(I32, logits.shape, 0) < ns)
    lp_o[...], cnt_o[...], ls_o[...], gb_o[...] = _plan_tile(e1, e2, valid, carry)


def _outproj(gated_p, gated_s, oswa_p, oswa_s, siga, sigb, xp, xs_pad, modp, mods, n2, wur, wus, wo, wr, br,
             *, npt, tpb, nb, ns):
    d = xp.shape[1]
    nt = npt + 1
    nrow = nt * TM
    row = lambda i: (i, 0)
    ptile = lambda i: (jnp.minimum(i, npt - 1), 0)
    pbatch = lambda col: (lambda i: (jnp.minimum(i // tpb, nb - 1), 0, col))
    scol = lambda col: (lambda i: (0, col))
    const = lambda i: (0, 0)
    wr_hi = wr.astype(BF16)
    wr_lo = (wr - wr_hi.astype(F32)).astype(BF16)
    meta = jax.ShapeDtypeStruct((nt, 1, LANES), I32)
    mspec = pl.BlockSpec((None, 1, LANES), lambda i: (i, 0, 0))
    return pl.pallas_call(
        functools.partial(_outproj_kernel, npt=npt, ns=ns),
        out_shape=[jax.ShapeDtypeStruct((nrow, d), F32),
                   jax.ShapeDtypeStruct((nrow, d), BF16),
                   jax.ShapeDtypeStruct((nrow, LANES), F32),
                   jax.ShapeDtypeStruct((nrow, LANES), F32), meta, meta, meta],
        grid=(nt,),
        in_specs=[pl.BlockSpec((TM, _RET_W), ptile), pl.BlockSpec((TM, _RET_W), const),
                  pl.BlockSpec((TM, _SWA_QW), ptile), pl.BlockSpec((TM, _SWA_QW), const),
                  pl.BlockSpec((TM, d), row), pl.BlockSpec((TM, d), row),
                  pl.BlockSpec((TM, d), ptile), pl.BlockSpec((TM, d), const),
                  pl.BlockSpec((None, 1, d), pbatch(2)), pl.BlockSpec((None, 1, d), pbatch(3)),
                  pl.BlockSpec((None, 1, d), pbatch(4)),
                  pl.BlockSpec((TM, d), scol(2)), pl.BlockSpec((TM, d), scol(3)), pl.BlockSpec((TM, d), scol(4)),
                  pl.BlockSpec((1, d), const),
                  pl.BlockSpec(wur.shape, const), pl.BlockSpec(wus.shape, const), pl.BlockSpec(wo.shape, const),
                  pl.BlockSpec(wr.shape, const), pl.BlockSpec(wr.shape, const), pl.BlockSpec((1, LANES), const)],
        out_specs=[pl.BlockSpec((TM, d), row), pl.BlockSpec((TM, d), row), pl.BlockSpec((TM, LANES), row),
                   pl.BlockSpec((TM, LANES), row), mspec, mspec, mspec],
        scratch_shapes=[pltpu.VMEM((1, LANES), F32)],
        compiler_params=_cparams("arbitrary"),
    )(gated_p, gated_s, oswa_p, oswa_s, siga, sigb, xp, xs_pad, modp, modp, modp, mods, mods, mods, n2,
      wur, wus, wo, wr_hi, wr_lo, br)


def _aligned(v):
    return v if isinstance(v, int) else pl.multiple_of(v, SUBLANES)


def _run_copy(src, dst, s_start, d_start, n, sem):
    s_start, d_start, n = _aligned(s_start), _aligned(d_start), _aligned(n)
    return pltpu.make_async_copy(src.at[pl.ds(s_start, n)], dst.at[pl.ds(d_start, n)], sem)


def _each_run(step, fn):
    for e in range(N_EXPERTS):
        fn(step * N_EXPERTS + e)


def _dispatch_kernel(cnt_ref, ls_ref, gd_ref, ps_ref, pn_ref, nu_ref, h_ref, lp_ref, xs_ref,
                     sorted_scr, zero_scr, sem, zsem, *, nt, maxt):
    i = pl.program_id(0)

    def pad(e):
        return _run_copy(zero_scr, xs_ref, 0, ps_ref[e], pn_ref[e], zsem)

    def tail(j):
        return _run_copy(zero_scr, xs_ref, 0, j * TE, TE, zsem)

    def each_pad(fn):
        def body(e, c):
            @pl.when(pn_ref[e] > 0)
            def _():
                fn(pad(e))
            return c
        lax.fori_loop(0, N_EXPERTS, body, 0)

    def each_tail(fn):
        def body(j, c):
            fn(tail(j))
            return c
        lax.fori_loop(nu_ref[0], maxt, body, 0)

    @pl.when(i == 0)
    def _():
        zero_scr[...] = jnp.zeros_like(zero_scr)
        each_pad(lambda cp: cp.start())
        each_tail(lambda cp: cp.start())

    lpt = lp_ref[...].T
    slot = lax.broadcasted_iota(I32, (SLOTS, TM), 0).astype(F32)
    perm = ((slot == lpt[0:1]) | (slot == lpt[1:2])).astype(BF16)
    sorted_scr[i % 2] = jnp.dot(perm, h_ref[...], preferred_element_type=F32)

    def copy(step):
        return lambda k: _run_copy(sorted_scr.at[step % 2], xs_ref, ls_ref[k], gd_ref[k], cnt_ref[k], sem.at[step % 2])

    _each_run(i, lambda k: copy(i)(k).start())

    @pl.when(i > 0)
    def _():
        _each_run(i - 1, lambda k: copy(i - 1)(k).wait())

    @pl.when(i == nt - 1)
    def _():
        _each_run(i, lambda k: copy(i)(k).wait())
        each_pad(lambda cp: cp.wait())
        each_tail(lambda cp: cp.wait())


def _dispatch(cnt, ls, gd, ps, pn, nu, h2, lp, *, nt, maxt):
    d = h2.shape[1]
    return pl.pallas_call(
        functools.partial(_dispatch_kernel, nt=nt, maxt=maxt),
        out_shape=jax.ShapeDtypeStruct((maxt * TE, d), F32),
        grid_spec=pltpu.PrefetchScalarGridSpec(
            num_scalar_prefetch=6,
            grid=(nt,),
            in_specs=[pl.BlockSpec((TM, d), lambda i, *_: (i, 0)),
                      pl.BlockSpec((TM, LANES), lambda i, *_: (i, 0))],
            out_specs=pl.BlockSpec(memory_space=pl.ANY),
            scratch_shapes=[pltpu.VMEM((2, SLOTS, d), F32), pltpu.VMEM((TE, d), F32),
                            pltpu.SemaphoreType.DMA((2,)), pltpu.SemaphoreType.DMA(())]),
        compiler_params=_cparams("arbitrary"),
    )(cnt, ls, gd, ps, pn, nu, h2, lp)


def _experts_kernel(te_ref, nu_ref, x_ref, w1_ref, w3_ref, w2_ref, y_ref, w1b, w3b, w2b):
    j = pl.program_id(0)
    changed = (j == 0) | (te_ref[j] != te_ref[jnp.maximum(j - 1, 0)])

    @pl.when(changed)
    def _():
        w1b[...] = w1_ref[...].astype(BF16)
        w3b[...] = w3_ref[...].astype(BF16)
        w2b[...] = w2_ref[...].astype(BF16)

    @pl.when(j < nu_ref[0])
    def _():
        for r in range(0, TE, TE_SUB):
            x = x_ref[r:r + TE_SUB].astype(BF16)
            a = jnp.dot(x, w1b[...], preferred_element_type=F32)
            b = jnp.dot(x, w3b[...], preferred_element_type=F32)
            y_ref[r:r + TE_SUB] = jnp.dot((_silu(a) * b).astype(BF16), w2b[...], preferred_element_type=F32)

    @pl.when(j >= nu_ref[0])
    def _():
        y_ref[...] = jnp.zeros_like(y_ref)


def _experts(te, nu, xs, w1, w3, w2, *, maxt):
    d = xs.shape[1]
    f = w1.shape[2]
    wsel = lambda j, te, nu: (te[j], 0, 0)
    return pl.pallas_call(
        _experts_kernel,
        out_shape=jax.ShapeDtypeStruct(xs.shape, F32),
        grid_spec=pltpu.PrefetchScalarGridSpec(
            num_scalar_prefetch=2,
            grid=(maxt,),
            in_specs=[pl.BlockSpec((TE, d), lambda j, te, nu: (jnp.minimum(j, nu[0] - 1), 0)),
                      pl.BlockSpec((None, d, f), wsel),
                      pl.BlockSpec((None, d, f), wsel),
                      pl.BlockSpec((None, f, d), wsel)],
            out_specs=pl.BlockSpec((TE, d), lambda j, te, nu: (j, 0)),
            scratch_shapes=[pltpu.VMEM((d, f), BF16), pltpu.VMEM((d, f), BF16), pltpu.VMEM((f, d), BF16)]),
        compiler_params=_cparams("arbitrary"),
    )(te, nu, xs, w1, w3, w2)


def _combine_kernel(cnt_ref, ls_ref, gd_ref, ys_ref, lp_ref, rt_ref, x1_ref, g2p_ref, g2s_ref, fg_ref,
                    yp_o, ys_o, ybuf, sem, *, npt):
    i = pl.program_id(0)
    nt = npt + 1

    def copy(step):
        return lambda k: _run_copy(ys_ref, ybuf.at[step % 2], gd_ref[k], ls_ref[k], cnt_ref[k], sem.at[step % 2])

    def fetch(step):
        ybuf[step % 2] = jnp.zeros(ybuf.shape[1:], F32)
        _each_run(step, lambda k: copy(step)(k).start())

    @pl.when(i == 0)
    def _():
        fetch(0)

    @pl.when(i + 1 < nt)
    def _():
        fetch(i + 1)

    _each_run(i, lambda k: copy(i)(k).wait())

    yb = ybuf[i % 2].astype(BF16)
    slot = lax.broadcasted_iota(I32, (TM, SLOTS), 1).astype(F32)
    lp = lp_ref[...]
    rt = rt_ref[...]

    def unsort(col):
        return jnp.dot((slot == lp[:, col:col + 1]).astype(BF16), yb, preferred_element_type=F32)

    moe = rt[:, 2:3] * unsort(0) + rt[:, 3:4] * unsort(1)
    g2 = jnp.where(i >= npt, g2s_ref[...], g2p_ref[...])
    y = _rms(x1_ref[...] + g2 * moe, fg_ref[...])

    @pl.when(i < npt)
    def _():
        yp_o[...] = y

    @pl.when(i >= npt)
    def _():
        ys_o[...] = y


def _combine(cnt, ls, gd, ys, lp, rt, x1, modp, mods, fg, *, npt, tpb, nb):
    d = x1.shape[1]
    nt = npt + 1
    row = lambda i, *_: (i, 0)
    return pl.pallas_call(
        functools.partial(_combine_kernel, npt=npt),
        out_shape=[jax.ShapeDtypeStruct((npt * TM, d), F32), jax.ShapeDtypeStruct((TM, d), F32)],
        grid_spec=pltpu.PrefetchScalarGridSpec(
            num_scalar_prefetch=3,
            grid=(nt,),
            in_specs=[pl.BlockSpec(memory_space=pl.ANY),
                      pl.BlockSpec((TM, LANES), row), pl.BlockSpec((TM, LANES), row), pl.BlockSpec((TM, d), row),
                      pl.BlockSpec((None, 1, d), lambda i, *_: (jnp.minimum(i // tpb, nb - 1), 0, 5)),
                      pl.BlockSpec((TM, d), lambda i, *_: (0, 5)),
                      pl.BlockSpec((1, d), lambda i, *_: (0, 0))],
            out_specs=[pl.BlockSpec((TM, d), lambda i, *_: (jnp.minimum(i, npt - 1), 0)),
                       pl.BlockSpec((TM, d), lambda i, *_: (0, 0))],
            scratch_shapes=[pltpu.VMEM((2, SLOTS, d), F32), pltpu.SemaphoreType.DMA((2,))]),
        compiler_params=_cparams("arbitrary"),
    )(cnt, ls, gd, ys, lp, rt, x1, modp, mods, fg)


def _rotation_tables(t):
    pos = jnp.concatenate([jnp.arange(t, dtype=I32), jnp.full((TM,), PAST_LEN, I32)]).astype(F32)
    inv_r = jnp.repeat(1.0 / (ROPE_THETA ** jnp.linspace(0.0, 1.0, RET_DK // 2, dtype=F32)), 2)
    ang_r = pos[:, None] * inv_r[None, :]
    sign_r = jnp.where(jnp.arange(RET_DK) % 2 == 0, -1.0, 1.0).astype(F32)
    inv_w = jnp.tile(ROPE_THETA ** (-jnp.arange(0, SWA_HD, 2, dtype=F32) / SWA_HD), LANES // (SWA_HD // 2))
    ang_w = pos[:, None] * inv_w[None, :]
    sign_w = jnp.where(jnp.arange(LANES) % SWA_HD < SWA_HD // 2, -1.0, 1.0).astype(F32)
    return jnp.cos(ang_r), jnp.sin(ang_r) * sign_r[None, :], jnp.cos(ang_w), jnp.sin(ang_w) * sign_w[None, :]


def kernel(x_prompt, x_sample, c_prompt, c_sample, state_ret, cache_swa_k, cache_swa_v, w_ada, b_ada, norm1_g, norm2_g, w_in, w_up_ret, w_up_swa, w_o, sink, w_rg, b_rg, w_re, b_re, w1, w3, w2, final_g):
    nb, t, d = x_prompt.shape
    ns, dec_seq, _ = x_sample.shape
    depth = w_ada.shape[0]
    assert depth == 1 and dec_seq == 1, "single layer, one new token per sequence"
    assert t % TM == 0 and ns <= TM and ns % 16 == 0 and d % (2 * LANES) == 0
    assert t % (RET_STEP_CHUNKS * RET_CHUNK) == 0 and t % (SWA_STEP_BLOCKS * WINDOW) == 0
    assert N_GROUPS + N_EXPERTS <= LANES
    w = cache_swa_k.shape[2]
    tpb = t // TM
    npt = nb * tpb
    nt = npt + 1
    np_rows = nb * t
    n_tok = np_rows + ns
    maxt = -(-(2 * n_tok + nt * N_EXPERTS * SUBLANES + N_EXPERTS * (TE - 1)) // TE)

    xp = x_prompt.reshape(np_rows, d)
    xs_pad = jnp.pad(x_sample.reshape(ns, d), ((0, TM - ns), (0, 0)))

    c_all = jnp.concatenate([jnp.pad(c_prompt, ((0, SUBLANES - nb % SUBLANES), (0, 0))),
                             jnp.pad(c_sample, ((0, TM - ns), (0, 0)))])
    mod = _modulation(c_all, w_ada[0], b_ada[0])
    modp = mod[:nb].reshape(nb, 1, 6 * d)
    mods = mod[c_all.shape[0] - TM:]

    tabs = _rotation_tables(t)
    rq, rk, rv, rg, sq, sk, sv, siga, sigb = _inproj(
        xp, xs_pad, modp, mods, norm1_g, w_in[0].astype(BF16), tabs, npt=npt, tpb=tpb, nb=nb)

    dm, qd, kd, cd, gamma = _ret_tables()
    gated_p, st_p = _retention_prompt(rq, rk, rv, rg, (dm, qd, kd, cd), nb=nb, t=t)
    gated_s, st_s = _retention_sample(rq, rk, rv, rg, state_ret[0], gamma, row0=np_rows, ns=ns)
    oswa_p = _swa_prompt(sq, sk, sv, sink[0], nb=nb, t=t)
    to_t = lambda c: jnp.transpose(c[0], (0, 2, 3, 1)).reshape(ns, _SWA_KW, w)
    from_t = lambda c: jnp.transpose(c.reshape(ns, SWA_KV_HEADS, SWA_HD, w), (0, 3, 1, 2))[None]
    oswa_s, ks_new, vs_new = _swa_sample(sq, sk, sv, to_t(cache_swa_k), to_t(cache_swa_v), sink[0],
                                         row0=np_rows, ns=ns)
    gated_s = jnp.pad(gated_s, ((0, TM - ns), (0, 0)))
    oswa_s = jnp.pad(oswa_s, ((0, TM - ns), (0, 0)))

    wr = jnp.pad(jnp.concatenate([w_rg[0], w_re[0]], axis=1), ((0, 0), (0, LANES - N_GROUPS - N_EXPERTS)))
    br = jnp.pad(jnp.concatenate([b_rg[0], b_re[0]]), (0, LANES - N_GROUPS - N_EXPERTS)).reshape(1, LANES)
    x1, h2, rt, lp, cnt, ls, gb = _outproj(
        gated_p, gated_s, oswa_p, oswa_s, siga, sigb, xp, xs_pad, modp, mods, norm2_g,
        w_up_ret[0].astype(BF16), w_up_swa[0].astype(BF16), w_o[0].astype(BF16), wr, br,
        npt=npt, tpb=tpb, nb=nb, ns=ns)
    cnt = cnt[:, 0, :N_EXPERTS]
    ls = ls[:, 0, :N_EXPERTS]
    gb = gb[:, 0, :N_EXPERTS]
    seg = jnp.sum(cnt, axis=0)
    tiles = (seg + TE - 1) // TE
    tile_end = jnp.cumsum(tiles)
    row_start = (tile_end - tiles) * TE
    gd = (gb + row_start[None, :]).reshape(-1)
    n_used = tile_end[-1:]
    jj = jnp.minimum(jnp.arange(maxt, dtype=I32), n_used[0] - 1)
    te = jnp.minimum(jnp.sum((tile_end[None, :] <= jj[:, None]).astype(I32), axis=1), N_EXPERTS - 1)
    cnt = cnt.reshape(-1)
    ls = ls.reshape(-1)
    n_used = n_used.astype(I32)
    xs = _dispatch(cnt, ls, gd, row_start + seg, tiles * TE - seg, n_used, h2, lp, nt=nt, maxt=maxt)
    ys = _experts(te, n_used, xs, w1[0], w3[0], w2[0], maxt=maxt)
    y_p, y_s = _combine(cnt, ls, gd, ys, lp, rt, x1, modp, mods, final_g.reshape(1, d), npt=npt, tpb=tpb, nb=nb)

    y_prompt = y_p.reshape(nb, t, d)
    y_sample = y_s[:ns].reshape(ns, 1, d)
    wk = min(WINDOW, t)
    last = lambda a: jnp.stack([a[(b + 1) * t - wk:(b + 1) * t] for b in range(nb)]).reshape(
        nb, wk, SWA_KV_HEADS, SWA_HD)
    skp, svp = last(sk), last(sv)
    return (y_prompt, y_sample, st_p[None], st_s[None], skp[None], svp[None], from_t(ks_new), from_t(vs_new))
```

```python
import functools

import jax
import jax.numpy as jnp
from jax import lax
from jax.experimental import pallas as pl
from jax.experimental.pallas import tpu as pltpu

F32 = jnp.float32
BF16 = jnp.bfloat16
I32 = jnp.int32

PAST_LEN = 8192
RET_HEADS = 4
RET_DK = 128
RET_DV = 128
RET_CHUNK = 128
SWA_HEADS = 8
SWA_KV_HEADS = 2
SWA_HD = 64
WINDOW = 128
ROPE_THETA = 10000.0
N_GROUPS = 4
EXPERTS_PER_GROUP = 8
N_EXPERTS = N_GROUPS * EXPERTS_PER_GROUP
D_EXPERT = 256
NORM_EPS = 1e-6

LANES = 128
SUBLANES = 8
TM = 256
RUN = 16
SLOTS = 2 * TM + N_EXPERTS * RUN
TE = 512
TE_SUB = 256
VMEM_LIMIT = 56 * 1024 * 1024

_RET_W = RET_HEADS * RET_DK
_SWA_QW = SWA_HEADS * SWA_HD
_SWA_KW = SWA_KV_HEADS * SWA_HD


def _cparams(*sem):
    return pltpu.CompilerParams(dimension_semantics=sem, vmem_limit_bytes=VMEM_LIMIT)


def _sigmoid(x):
    return 1.0 / (1.0 + jnp.exp(-x))


def _silu(x):
    return x * _sigmoid(x)


def _bdot(a, b):
    return jnp.dot(a.astype(BF16), b.astype(BF16), preferred_element_type=F32)


def _bdot_nt(a, b):
    return lax.dot_general(a.astype(BF16), b.astype(BF16), (((1,), (1,)), ((), ())), preferred_element_type=F32)


def _mod_kernel(c_ref, w_ref, b_ref, o_ref):
    o_ref[...] = _bdot(_silu(c_ref[...]), w_ref[...]) + b_ref[...]


def _modulation(c_all, w_ada, b_ada):
    rows, d = c_all.shape
    n = w_ada.shape[1]
    return pl.pallas_call(
        _mod_kernel,
        out_shape=jax.ShapeDtypeStruct((rows, n), F32),
        grid=(n // d,),
        in_specs=[pl.BlockSpec((rows, d), lambda j: (0, 0)),
                  pl.BlockSpec((d, d), lambda j: (0, j)),
                  pl.BlockSpec((1, d), lambda j: (0, j))],
        out_specs=pl.BlockSpec((rows, d), lambda j: (0, j)),
        compiler_params=_cparams("arbitrary"),
    )(c_all, w_ada, b_ada.reshape(1, n))


def _rms(x, g):
    return x * lax.rsqrt(jnp.mean(x * x, axis=-1, keepdims=True) + NORM_EPS) * g


def _pair_rotate(z, cos, sin_signed):
    n = z.shape[-1]
    lane = lax.broadcasted_iota(I32, z.shape, 1)
    partner = jnp.where((lane & 1) == 0, pltpu.roll(z, n - 1, 1), pltpu.roll(z, 1, 1))
    reps = n // LANES
    cos = jnp.concatenate([cos] * reps, axis=1) if reps > 1 else cos
    sin_signed = jnp.concatenate([sin_signed] * reps, axis=1) if reps > 1 else sin_signed
    return z * cos + partner * sin_signed


def _half_rotate(z, cos, sin_signed):
    n = z.shape[-1]
    half = SWA_HD // 2
    lane = lax.broadcasted_iota(I32, z.shape, 1)
    partner = jnp.where((lane & (SWA_HD - 1)) < half, pltpu.roll(z, n - half, 1), pltpu.roll(z, half, 1))
    reps = n // LANES
    cos = jnp.concatenate([cos] * reps, axis=1) if reps > 1 else cos
    sin_signed = jnp.concatenate([sin_signed] * reps, axis=1) if reps > 1 else sin_signed
    return z * cos + partner * sin_signed


def _inproj_kernel(xp_ref, xs_ref, shp_ref, scp_ref, shs_ref, scs_ref, n1_ref, w_ref,
                   cr_ref, sr_ref, cw_ref, sw_ref,
                   rq_o, rk_o, rv_o, rg_o, sq_o, sk_o, sv_o, za_o, zb_o, *, npt):
    is_s = pl.program_id(0) >= npt
    x = jnp.where(is_s, xs_ref[...], xp_ref[...])
    sh = jnp.where(is_s, shs_ref[...], shp_ref[...])
    sc = jnp.where(is_s, scs_ref[...], scp_ref[...])
    h = (_rms(x, n1_ref[...]) * (1.0 + sc) + sh).astype(BF16)
    cr, sr, cw, sw = cr_ref[...], sr_ref[...], cw_ref[...], sw_ref[...]

    def seg(a, b):
        return jnp.dot(h, w_ref[:, a:b], preferred_element_type=F32)

    o = 0
    rq_o[...] = _pair_rotate(seg(o, o + _RET_W), cr, sr).astype(BF16)
    o += _RET_W
    rk_o[...] = (_pair_rotate(seg(o, o + _RET_W), cr, sr) * (RET_DK ** -0.5)).astype(BF16)
    o += _RET_W
    rv_o[...] = seg(o, o + _RET_W).astype(BF16)
    o += _RET_W
    rg_o[...] = _silu(seg(o, o + _RET_W)).astype(BF16)
    o += _RET_W
    sq_o[...] = (_half_rotate(seg(o, o + _SWA_QW), cw, sw) * (SWA_HD ** -0.5)).astype(BF16)
    o += _SWA_QW
    zkv = seg(o, o + 2 * _SWA_KW)
    sk_o[...] = _half_rotate(zkv[:, :_SWA_KW], cw, sw)
    sv_o[...] = zkv[:, _SWA_KW:]
    o += 2 * _SWA_KW
    d = x.shape[-1]
    za_o[...] = _sigmoid(seg(o, o + d)).astype(BF16)
    o += d
    zb_o[...] = _sigmoid(seg(o, o + d)).astype(BF16)


def _inproj(xp, xs_pad, modp, mods, n1, w_in_b, tabs, *, npt, tpb, nb):
    d = xp.shape[1]
    nt = npt + 1
    nrow = nt * TM
    n_in = w_in_b.shape[1]
    ptile = lambda i: (jnp.minimum(i, npt - 1), 0)
    pbatch = lambda col: (lambda i: (jnp.minimum(i // tpb, nb - 1), 0, col))
    tab_idx = lambda i: (jnp.where(i < npt, i % tpb, tpb), 0)
    out_cols = [(_RET_W, BF16)] * 4 + [(_SWA_QW, BF16), (_SWA_KW, F32), (_SWA_KW, F32), (d, BF16), (d, BF16)]
    return pl.pallas_call(
        functools.partial(_inproj_kernel, npt=npt),
        out_shape=[jax.ShapeDtypeStruct((nrow, c), t) for c, t in out_cols],
        grid=(nt,),
        in_specs=[pl.BlockSpec((TM, d), ptile),
                  pl.BlockSpec((TM, d), lambda i: (0, 0)),
                  pl.BlockSpec((None, 1, d), pbatch(0)),
                  pl.BlockSpec((None, 1, d), pbatch(1)),
                  pl.BlockSpec((TM, d), lambda i: (0, 0)),
                  pl.BlockSpec((TM, d), lambda i: (0, 1)),
                  pl.BlockSpec((1, d), lambda i: (0, 0)),
                  pl.BlockSpec((d, n_in), lambda i: (0, 0))]
                 + [pl.BlockSpec((TM, LANES), tab_idx)] * 4,
        out_specs=[pl.BlockSpec((TM, c), lambda i: (i, 0)) for c, _ in out_cols],
        compiler_params=_cparams("arbitrary"),
    )(xp, xs_pad, modp, modp, mods, mods, n1, w_in_b, *tabs)


RET_STEP_CHUNKS = 4


def _ret_kernel(q_ref, k_ref, v_ref, g_ref, dm_ref, qd_ref, kd_ref, cd_ref, o_ref, st_ref, s_scr, *, nsteps):
    step = pl.program_id(1)

    @pl.when(step == 0)
    def _():
        s_scr[...] = jnp.zeros_like(s_scr)

    for h in range(RET_HEADS):
        sl = slice(h * RET_DK, (h + 1) * RET_DK)
        state = s_scr[h]
        for c in range(RET_STEP_CHUNKS):
            rows = slice(c * RET_CHUNK, (c + 1) * RET_CHUNK)
            q, k, v = q_ref[rows, sl], k_ref[rows, sl], v_ref[rows, sl]
            att = _bdot_nt(q, k) * dm_ref[h]
            o = _bdot(att, v) + _bdot(q.astype(F32) * qd_ref[h], state)
            kd = (k.astype(F32) * kd_ref[h]).astype(BF16)
            kv = lax.dot_general(kd, v, (((0,), (0,)), ((), ())), preferred_element_type=F32)
            state = cd_ref[h] * state + kv
            o = o * lax.rsqrt(jnp.mean(o * o, axis=-1, keepdims=True) + NORM_EPS)
            o_ref[rows, sl] = (o * g_ref[rows, sl].astype(F32)).astype(BF16)
        s_scr[h] = state

    @pl.when(step == nsteps - 1)
    def _():
        st_ref[...] = s_scr[...]


def _ret_tables():
    ld = jnp.log(1.0 - 2.0 ** (-5.0 - jnp.arange(RET_HEADS, dtype=F32)))
    idx = jnp.arange(RET_CHUNK, dtype=F32)
    diff = idx[:, None] - idx[None, :]
    causal = diff >= 0
    dmask = jnp.where(causal[None], jnp.exp(ld[:, None, None] * jnp.where(causal, diff, 0.0)[None]), 0.0)
    k_dec = jnp.exp(ld[None, :] * (RET_CHUNK - 1.0 - idx)[:, None])
    q_dec = jnp.exp(ld[None, :] * (idx + 1.0)[:, None])
    chunk_decay = jnp.exp(ld * RET_CHUNK)
    bc = lambda t: jnp.broadcast_to(t.T[:, :, None], (RET_HEADS, RET_CHUNK, RET_DV))
    cd = jnp.broadcast_to(chunk_decay[:, None, None], (RET_HEADS, 1, RET_DV))
    return dmask, bc(q_dec), bc(k_dec), cd, jnp.exp(ld)


def _retention_prompt(rq, rk, rv, rg, tabs, *, nb, t):
    rows = RET_STEP_CHUNKS * RET_CHUNK
    nsteps = t // rows
    dm, qd, kd, cd = tabs
    blk = lambda b, c: (b * nsteps + c, 0)
    full3 = lambda b, c: (0, 0, 0)
    return pl.pallas_call(
        functools.partial(_ret_kernel, nsteps=nsteps),
        out_shape=[jax.ShapeDtypeStruct((nb * t, _RET_W), BF16),
                   jax.ShapeDtypeStruct((nb, RET_HEADS, RET_DK, RET_DV), F32)],
        grid=(nb, nsteps),
        in_specs=[pl.BlockSpec((rows, _RET_W), blk)] * 4
                 + [pl.BlockSpec((RET_HEADS, RET_CHUNK, RET_DV), full3)] * 3
                 + [pl.BlockSpec((RET_HEADS, 1, RET_DV), full3)],
        out_specs=[pl.BlockSpec((rows, _RET_W), blk),
                   pl.BlockSpec((None, RET_HEADS, RET_DK, RET_DV), lambda b, c: (b, 0, 0, 0))],
        scratch_shapes=[pltpu.VMEM((RET_HEADS, RET_DK, RET_DV), F32)],
        compiler_params=_cparams("arbitrary", "arbitrary"),
    )(rq, rk, rv, rg, dm, qd, kd, cd)


def _ret_sample_kernel(gam_ref, q_ref, k_ref, v_ref, g_ref, s0_ref, o_ref, st_ref, *, sb):
    gamma = gam_ref[pl.program_id(1)]
    q = q_ref[...].astype(F32)
    k = k_ref[...].astype(F32)
    v = v_ref[...].astype(F32)
    rows = sb * RET_DK
    s2 = s0_ref[...].reshape(rows, RET_DV)
    col_b = lax.broadcasted_iota(I32, (sb, rows), 1) // RET_DK
    row_b = lax.broadcasted_iota(I32, (sb, rows), 0)
    qexp = jnp.where(col_b == row_b, jnp.concatenate([q * gamma] * sb, axis=1), 0.0)
    o = jnp.sum(q * k, axis=-1, keepdims=True) * v + _bdot(qexp, s2)
    o = o * lax.rsqrt(jnp.mean(o * o, axis=-1, keepdims=True) + NORM_EPS)
    o_ref[...] = (o * g_ref[...].astype(F32)).astype(BF16)
    rep = (lax.broadcasted_iota(I32, (rows, sb), 0) // RET_DK == lax.broadcasted_iota(I32, (rows, sb), 1))
    rep = rep.astype(BF16)
    krep = _bdot(rep, k)
    vrep = _bdot(rep, v)
    eye = (lax.broadcasted_iota(I32, (rows, RET_DK), 0) % RET_DK == lax.broadcasted_iota(I32, (rows, RET_DK), 1))
    kcol = jnp.sum(jnp.where(eye, krep, 0.0), axis=-1, keepdims=True)
    st_ref[...] = (gamma * s2 + kcol * vrep).reshape(sb, RET_DK, RET_DV)


def _retention_sample(rq, rk, rv, rg, s0, gamma, *, row0, ns):
    sb = min(64, ns)
    base = row0 // sb
    blk = lambda i, h: (base + i, h)
    sblk = lambda i, h: (i, h, 0, 0)
    return pl.pallas_call(
        functools.partial(_ret_sample_kernel, sb=sb),
        out_shape=[jax.ShapeDtypeStruct((ns, _RET_W), BF16),
                   jax.ShapeDtypeStruct(s0.shape, F32)],
        grid=(ns // sb, RET_HEADS),
        in_specs=[pl.BlockSpec(memory_space=pltpu.SMEM)]
                 + [pl.BlockSpec((sb, RET_DK), blk)] * 4
                 + [pl.BlockSpec((sb, None, RET_DK, RET_DV), sblk)],
        out_specs=[pl.BlockSpec((sb, RET_DV), lambda i, h: (i, h)),
                   pl.BlockSpec((sb, None, RET_DK, RET_DV), sblk)],
        compiler_params=_cparams("arbitrary", "arbitrary"),
    )(gamma, rq, rk, rv, rg, s0)


def _sink_softmax(s, mask, sink):
    s = jnp.where(mask, s, -jnp.inf)
    m = jnp.maximum(jnp.max(s, axis=-1, keepdims=True), sink)
    p = jnp.exp(s - m)
    return p / (jnp.sum(p, axis=-1, keepdims=True) + jnp.exp(sink - m))


def _split_kv_heads(x):
    lo = lax.broadcasted_iota(I32, x.shape, 1) < SWA_HD
    h0_lo = jnp.where(lo, x, 0.0)
    h1_hi = jnp.where(lo, 0.0, x)
    return ((h0_lo, pltpu.roll(h0_lo, SWA_HD, 1)), (pltpu.roll(h1_hi, SWA_HD, 1), h1_hi))


SWA_STEP_BLOCKS = 4


def _swa_kernel(sink_ref, q_ref, kc_ref, kp_ref, vc_ref, vp_ref, o_ref):
    n = pl.program_id(1)
    c = WINDOW
    kk = jnp.concatenate([kp_ref[...], kc_ref[...]], axis=0)
    vv = jnp.concatenate([vp_ref[...], vc_ref[...]], axis=0)
    ks = [[a.astype(BF16) for a in pair] for pair in _split_kv_heads(kk)]
    vs = [[a.astype(BF16) for a in pair] for pair in _split_kv_heads(vv)]
    qi = lax.broadcasted_iota(I32, (2 * c, 2 * c), 0) % c
    ki = lax.broadcasted_iota(I32, (2 * c, 2 * c), 1)
    band = (ki > qi) & (ki <= qi + c)
    top = lax.broadcasted_iota(I32, (2 * c, 1), 0) < c
    for s in range(SWA_STEP_BLOCKS):
        rows = slice(s * c, (s + 1) * c)
        keys = slice(s * c, (s + 2) * c)
        mask = (band & ((ki >= c) | (n > 0))) if s == 0 else band
        for kvh in range(SWA_KV_HEADS):
            j0, j1 = 2 * kvh, 2 * kvh + 1
            q2 = jnp.concatenate([q_ref[rows, j0 * LANES:(j0 + 1) * LANES],
                                  q_ref[rows, j1 * LANES:(j1 + 1) * LANES]], axis=0)
            kcat = jnp.concatenate([ks[kvh][0][keys], ks[kvh][1][keys]], axis=0)
            vcat = jnp.concatenate([vs[kvh][0][keys], vs[kvh][1][keys]], axis=0)
            sc = lax.dot_general(q2, kcat, (((1,), (1,)), ((), ())), preferred_element_type=F32)
            ps = []
            for half in range(2):
                sink = jnp.where(top, sink_ref[2 * j0 + half], sink_ref[2 * j1 + half])
                ps.append(_sink_softmax(sc[:, half * 2 * c:(half + 1) * 2 * c], mask, sink).astype(BF16))
            o = jnp.dot(jnp.concatenate(ps, axis=1), vcat, preferred_element_type=F32)
            o_ref[rows, j0 * LANES:(j0 + 1) * LANES] = o[:c].astype(BF16)
            o_ref[rows, j1 * LANES:(j1 + 1) * LANES] = o[c:].astype(BF16)


def _swa_prompt(sq, sk, sv, sink, *, nb, t):
    rows = SWA_STEP_BLOCKS * WINDOW
    nsteps = t // rows
    nblk = t // WINDOW
    cur = lambda b, n: (b * nsteps + n, 0)
    prev = lambda b, n: (b * nblk + jnp.maximum(n * SWA_STEP_BLOCKS - 1, 0), 0)
    return pl.pallas_call(
        _swa_kernel,
        out_shape=jax.ShapeDtypeStruct((nb * t, _SWA_QW), BF16),
        grid=(nb, nsteps),
        in_specs=[pl.BlockSpec(memory_space=pltpu.SMEM),
                  pl.BlockSpec((rows, _SWA_QW), cur),
                  pl.BlockSpec((rows, _SWA_KW), cur),
                  pl.BlockSpec((WINDOW, _SWA_KW), prev),
                  pl.BlockSpec((rows, _SWA_KW), cur),
                  pl.BlockSpec((WINDOW, _SWA_KW), prev)],
        out_specs=pl.BlockSpec((rows, _SWA_QW), cur),
        compiler_params=_cparams("arbitrary", "arbitrary"),
    )(sink, sq, sk, sk, sv, sv)


def _swa_sample_kernel(sink_ref, q_ref, kn_ref, vn_ref, kc_ref, vc_ref, o_ref, ko_ref, vo_ref, *, sb, w):
    pad = jnp.zeros((LANES - sb, _SWA_KW), F32)
    knt = jnp.concatenate([kn_ref[...], pad], axis=0).T
    vnt = jnp.concatenate([vn_ref[...], pad], axis=0).T
    kall = jnp.concatenate([kc_ref[b] for b in range(sb)] + [knt], axis=1)
    vall = jnp.concatenate([vc_ref[b] for b in range(sb)] + [vnt], axis=1)
    ncol = sb * w + LANES
    lo = lax.broadcasted_iota(I32, (sb, LANES), 1) < SWA_HD
    group = SWA_HEADS // SWA_KV_HEADS
    pieces = []
    for h in range(SWA_HEADS):
        slab = q_ref[:, (h // 2) * LANES:(h // 2 + 1) * LANES].astype(F32)
        mine = jnp.where(lo, slab, 0.0) if h % 2 == 0 else jnp.where(lo, 0.0, slab)
        pieces.append(mine if (h % 2) == (h // group) else pltpu.roll(mine, SWA_HD, 1))
    qrows = jnp.concatenate(pieces, axis=0)
    nrow = SWA_HEADS * sb
    s = _bdot(qrows, kall)
    rb = lax.broadcasted_iota(I32, (nrow, ncol), 0) % sb
    ci = lax.broadcasted_iota(I32, (nrow, ncol), 1)
    in_cache = (ci < sb * w) & (ci // w == rb) & ((w - ci % w) < WINDOW)
    mask = in_cache | (ci == sb * w + rb)
    sink_col = jnp.concatenate([jnp.full((sb, 1), sink_ref[h], F32) for h in range(SWA_HEADS)], axis=0)
    p = _sink_softmax(s, mask, sink_col)
    o = _bdot_nt(p, vall)
    for j in range(SWA_HEADS // 2):
        acc = jnp.zeros((sb, LANES), F32)
        for half in range(2):
            h = 2 * j + half
            oh = o[h * sb:(h + 1) * sb]
            own = jnp.where(lo, oh, 0.0) if h // group == 0 else jnp.where(lo, 0.0, oh)
            acc = acc + (own if (h // group) == half else pltpu.roll(own, SWA_HD, 1))
        o_ref[:, j * LANES:(j + 1) * LANES] = acc.astype(BF16)
    newest = lax.broadcasted_iota(I32, (_SWA_KW, w), 1) == w - 1
    for b in range(sb):
        ko_ref[b] = jnp.where(newest, knt[:, b:b + 1], pltpu.roll(kc_ref[b], w - 1, 1))
        vo_ref[b] = jnp.where(newest, vnt[:, b:b + 1], pltpu.roll(vc_ref[b], w - 1, 1))


def _swa_sample(sq, sk, sv, cache_kt, cache_vt, sink, *, row0, ns):
    sb = min(16, ns)
    w = cache_kt.shape[2]
    base = row0 // sb
    blk = lambda i: (base + i, 0)
    cblk = lambda i: (i, 0, 0)
    cspec = pl.BlockSpec((sb, _SWA_KW, w), cblk)
    return pl.pallas_call(
        functools.partial(_swa_sample_kernel, sb=sb, w=w),
        out_shape=[jax.ShapeDtypeStruct((ns, _SWA_QW), BF16),
                   jax.ShapeDtypeStruct(cache_kt.shape, F32), jax.ShapeDtypeStruct(cache_vt.shape, F32)],
        grid=(ns // sb,),
        in_specs=[pl.BlockSpec(memory_space=pltpu.SMEM),
                  pl.BlockSpec((sb, _SWA_QW), blk),
                  pl.BlockSpec((sb, _SWA_KW), blk),
                  pl.BlockSpec((sb, _SWA_KW), blk),
                  cspec, cspec],
        out_specs=[pl.BlockSpec((sb, _SWA_QW), lambda i: (i, 0)), cspec, cspec],
        compiler_params=_cparams("arbitrary"),
    )(sink, sq, sk, sv, cache_kt, cache_vt)


def _route(logits):
    lane = lax.broadcasted_iota(I32, logits.shape, 1)
    big = jnp.int32(1 << 20)
    neg = -jnp.inf

    def top(mask):
        v = jnp.max(jnp.where(mask, logits, neg), axis=-1, keepdims=True)
        i = jnp.min(jnp.where(mask & (logits == v), lane, big), axis=-1, keepdims=True)
        return v, i

    gmask = lane < N_GROUPS
    gmax, gsel = top(gmask)
    p_group = 1.0 / jnp.sum(jnp.where(gmask, jnp.exp(logits - gmax), 0.0), axis=-1, keepdims=True)
    first = N_GROUPS + gsel * EXPERTS_PER_GROUP
    emask = (lane >= first) & (lane < first + EXPERTS_PER_GROUP)
    v1, i1 = top(emask)
    v2, i2 = top(emask & (lane != i1))
    t = jnp.exp(v2 - v1)
    w1 = p_group / (1.0 + t)
    return i1 - N_GROUPS, i2 - N_GROUPS, w1, w1 * t


def _plan_tile(e1, e2, valid, carry):
    lane = lax.broadcasted_iota(I32, (TM, LANES), 1)
    oh1 = ((lane == e1) & valid).astype(F32)
    oh2 = ((lane == e2) & valid).astype(F32)
    oh = oh1 + oh2
    tri = (lax.broadcasted_iota(I32, (TM, TM), 0) > lax.broadcasted_iota(I32, (TM, TM), 1)).astype(BF16)
    before = _bdot(tri, oh)
    cnt = jnp.sum(oh, axis=0, keepdims=True)
    units = jnp.maximum(jnp.floor((cnt + (RUN - 1)) * (1.0 / RUN)), 1.0)
    upper = (lax.broadcasted_iota(I32, (LANES, LANES), 0) < lax.broadcasted_iota(I32, (LANES, LANES), 1))
    lstart = RUN * _bdot(jnp.broadcast_to(units, (SUBLANES, LANES)), upper.astype(BF16))[0:1]
    slot = lstart + before
    lp1 = jnp.sum(oh1 * slot, axis=-1, keepdims=True)
    lp2 = jnp.sum(oh2 * slot, axis=-1, keepdims=True)
    vcol = valid[:, 0:1]
    lp = jnp.where(lane == 0, jnp.where(vcol, lp1, -1.0), jnp.where(lane == 1, jnp.where(vcol, lp2, -1.0), 0.0))
    base = carry[...]
    carry[...] = base + RUN * units
    return lp, (RUN * units).astype(I32), lstart.astype(I32), base.astype(I32)


def _outproj_kernel(gtp_ref, gts_ref, osp_ref, oss_ref, sa_ref, sb_ref, xp_ref, xs_ref, g1p_ref, shp_ref, scp_ref,
                    g1s_ref, shs_ref, scs_ref, n2_ref, wur_ref, wus_ref, wo_ref, wrh_ref, wrl_ref, br_ref,
                    x1_o, h2_o, rt_o, lp_o, cnt_o, ls_o, gb_o, carry, *, npt, ns):
    i = pl.program_id(0)
    is_s = i >= npt

    @pl.when(i == 0)
    def _():
        carry[...] = jnp.zeros_like(carry)

    x = jnp.where(is_s, xs_ref[...], xp_ref[...])
    g1 = jnp.where(is_s, g1s_ref[...], g1p_ref[...])
    sh = jnp.where(is_s, shs_ref[...], shp_ref[...])
    sc = jnp.where(is_s, scs_ref[...], scp_ref[...])
    gated = jnp.where(is_s, gts_ref[...], gtp_ref[...])
    oswa = jnp.where(is_s, oss_ref[...], osp_ref[...])
    y_ret = jnp.dot(gated, wur_ref[...], preferred_element_type=F32)
    y_swa = jnp.dot(oswa, wus_ref[...], preferred_element_type=F32)
    merged = sa_ref[...].astype(F32) * y_ret + sb_ref[...].astype(F32) * y_swa
    x1 = x + g1 * jnp.dot(merged.astype(BF16), wo_ref[...], preferred_element_type=F32)
    x1_o[...] = x1
    h2 = _rms(x1, n2_ref[...]) * (1.0 + sc) + sh
    hi = h2.astype(BF16)
    h2_o[...] = hi
    lo = (h2 - hi.astype(F32)).astype(BF16)
    wrh = wrh_ref[...]
    logits = (jnp.dot(hi, wrh, preferred_element_type=F32) + jnp.dot(lo, wrh, preferred_element_type=F32)
              + jnp.dot(hi, wrl_ref[...], preferred_element_type=F32) + br_ref[...])
    e1, e2, w1, w2 = _route(logits)
    lane = lax.broadcasted_iota(I32, logits.shape, 1)
    rt_o[...] = jnp.where(lane == 2, w1, jnp.where(lane == 3, w2, 0.0))
    valid = jnp.logical_not(is_s) | (lax.broadcasted_iota(I32, logits.shape, 0) < ns)
    lp_o[...], cnt_o[...], ls_o[...], gb_o[...] = _plan_tile(e1, e2, valid, carry)


def _outproj(gated_p, gated_s, oswa_p, oswa_s, siga, sigb, xp, xs_pad, modp, mods, n2, wur, wus, wo, wr, br,
             *, npt, tpb, nb, ns):
    d = xp.shape[1]
    nt = npt + 1
    nrow = nt * TM
    row = lambda i: (i, 0)
    ptile = lambda i: (jnp.minimum(i, npt - 1), 0)
    pbatch = lambda col: (lambda i: (jnp.minimum(i // tpb, nb - 1), 0, col))
    scol = lambda col: (lambda i: (0, col))
    const = lambda i: (0, 0)
    wr_hi = wr.astype(BF16)
    wr_lo = (wr - wr_hi.astype(F32)).astype(BF16)
    meta = jax.ShapeDtypeStruct((nt, 1, LANES), I32)
    mspec = pl.BlockSpec((None, 1, LANES), lambda i: (i, 0, 0))
    return pl.pallas_call(
        functools.partial(_outproj_kernel, npt=npt, ns=ns),
        out_shape=[jax.ShapeDtypeStruct((nrow, d), F32),
                   jax.ShapeDtypeStruct((nrow, d), BF16),
                   jax.ShapeDtypeStruct((nrow, LANES), F32),
                   jax.ShapeDtypeStruct((nrow, LANES), F32), meta, meta, meta],
        grid=(nt,),
        in_specs=[pl.BlockSpec((TM, _RET_W), ptile), pl.BlockSpec((TM, _RET_W), const),
                  pl.BlockSpec((TM, _SWA_QW), ptile), pl.BlockSpec((TM, _SWA_QW), const),
                  pl.BlockSpec((TM, d), row), pl.BlockSpec((TM, d), row),
                  pl.BlockSpec((TM, d), ptile), pl.BlockSpec((TM, d), const),
                  pl.BlockSpec((None, 1, d), pbatch(2)), pl.BlockSpec((None, 1, d), pbatch(3)),
                  pl.BlockSpec((None, 1, d), pbatch(4)),
                  pl.BlockSpec((TM, d), scol(2)), pl.BlockSpec((TM, d), scol(3)), pl.BlockSpec((TM, d), scol(4)),
                  pl.BlockSpec((1, d), const),
                  pl.BlockSpec(wur.shape, const), pl.BlockSpec(wus.shape, const), pl.BlockSpec(wo.shape, const),
                  pl.BlockSpec(wr.shape, const), pl.BlockSpec(wr.shape, const), pl.BlockSpec((1, LANES), const)],
        out_specs=[pl.BlockSpec((TM, d), row), pl.BlockSpec((TM, d), row), pl.BlockSpec((TM, LANES), row),
                   pl.BlockSpec((TM, LANES), row), mspec, mspec, mspec],
        scratch_shapes=[pltpu.VMEM((1, LANES), F32)],
        compiler_params=_cparams("arbitrary"),
    )(gated_p, gated_s, oswa_p, oswa_s, siga, sigb, xp, xs_pad, modp, modp, modp, mods, mods, mods, n2,
      wur, wus, wo, wr_hi, wr_lo, br)


def _aligned(v):
    return v if isinstance(v, int) else pl.multiple_of(v, RUN)


def _run_copy(src, dst, s_start, d_start, n, sem):
    s_start, d_start, n = _aligned(s_start), _aligned(d_start), _aligned(n)
    return pltpu.make_async_copy(src.at[pl.ds(s_start, n)], dst.at[pl.ds(d_start, n)], sem)


def _each_run(step, fn):
    for e in range(N_EXPERTS):
        fn(step * N_EXPERTS + e)


def _dispatch_kernel(cnt_ref, ls_ref, gd_ref, ps_ref, pn_ref, nu_ref, h_ref, lp_ref, xs_ref,
                     sorted_scr, zero_scr, sem, zsem, *, nt, maxt):
    i = pl.program_id(0)

    def pad(e):
        return _run_copy(zero_scr, xs_ref, 0, ps_ref[e], pn_ref[e], zsem)

    def tail(j):
        return _run_copy(zero_scr, xs_ref, 0, j * TE, TE, zsem)

    def each_pad(fn):
        def body(e, c):
            @pl.when(pn_ref[e] > 0)
            def _():
                fn(pad(e))
            return c
        lax.fori_loop(0, N_EXPERTS, body, 0)

    def each_tail(fn):
        def body(j, c):
            fn(tail(j))
            return c
        lax.fori_loop(nu_ref[0], maxt, body, 0)

    @pl.when(i == 0)
    def _():
        zero_scr[...] = jnp.zeros_like(zero_scr)
        each_pad(lambda cp: cp.start())
        each_tail(lambda cp: cp.start())

    lpt = lp_ref[...].T
    slot = lax.broadcasted_iota(I32, (SLOTS, TM), 0).astype(F32)
    perm = ((slot == lpt[0:1]) | (slot == lpt[1:2])).astype(BF16)
    sorted_scr[i % 2] = jnp.dot(perm, h_ref[...], preferred_element_type=F32).astype(BF16)

    def copy(step):
        return lambda k: _run_copy(sorted_scr.at[step % 2], xs_ref, ls_ref[k], gd_ref[k], cnt_ref[k], sem.at[step % 2])

    _each_run(i, lambda k: copy(i)(k).start())

    @pl.when(i > 0)
    def _():
        _each_run(i - 1, lambda k: copy(i - 1)(k).wait())

    @pl.when(i == nt - 1)
    def _():
        _each_run(i, lambda k: copy(i)(k).wait())
        each_pad(lambda cp: cp.wait())
        each_tail(lambda cp: cp.wait())


def _dispatch(cnt, ls, gd, ps, pn, nu, h2, lp, *, nt, maxt):
    d = h2.shape[1]
    return pl.pallas_call(
        functools.partial(_dispatch_kernel, nt=nt, maxt=maxt),
        out_shape=jax.ShapeDtypeStruct((maxt * TE, d), BF16),
        grid_spec=pltpu.PrefetchScalarGridSpec(
            num_scalar_prefetch=6,
            grid=(nt,),
            in_specs=[pl.BlockSpec((TM, d), lambda i, *_: (i, 0)),
                      pl.BlockSpec((TM, LANES), lambda i, *_: (i, 0))],
            out_specs=pl.BlockSpec(memory_space=pl.ANY),
            scratch_shapes=[pltpu.VMEM((2, SLOTS, d), BF16), pltpu.VMEM((TE, d), BF16),
                            pltpu.SemaphoreType.DMA((2,)), pltpu.SemaphoreType.DMA(())]),
        compiler_params=_cparams("arbitrary"),
    )(cnt, ls, gd, ps, pn, nu, h2, lp)


def _experts_kernel(te_ref, nu_ref, x_ref, w1_ref, w3_ref, w2_ref, y_ref, w1b, w3b, w2b):
    j = pl.program_id(0)
    changed = (j == 0) | (te_ref[j] != te_ref[jnp.maximum(j - 1, 0)])

    @pl.when(changed)
    def _():
        w1b[...] = w1_ref[...].astype(BF16)
        w3b[...] = w3_ref[...].astype(BF16)
        w2b[...] = w2_ref[...].astype(BF16)

    @pl.when(j < nu_ref[0])
    def _():
        for r in range(0, TE, TE_SUB):
            x = x_ref[r:r + TE_SUB]
            a = jnp.dot(x, w1b[...], preferred_element_type=F32)
            b = jnp.dot(x, w3b[...], preferred_element_type=F32)
            y = jnp.dot((_silu(a) * b).astype(BF16), w2b[...], preferred_element_type=F32)
            y_ref[r:r + TE_SUB] = y.astype(BF16)


def _experts(te, nu, xs, w1, w3, w2, *, maxt):
    d = xs.shape[1]
    f = w1.shape[2]
    wsel = lambda j, te, nu: (te[j], 0, 0)
    used = lambda j, te, nu: (jnp.minimum(j, nu[0] - 1), 0)
    return pl.pallas_call(
        _experts_kernel,
        out_shape=jax.ShapeDtypeStruct(xs.shape, BF16),
        grid_spec=pltpu.PrefetchScalarGridSpec(
            num_scalar_prefetch=2,
            grid=(maxt,),
            in_specs=[pl.BlockSpec((TE, d), used),
                      pl.BlockSpec((None, d, f), wsel),
                      pl.BlockSpec((None, d, f), wsel),
                      pl.BlockSpec((None, f, d), wsel)],
            out_specs=pl.BlockSpec((TE, d), used),
            scratch_shapes=[pltpu.VMEM((d, f), BF16), pltpu.VMEM((d, f), BF16), pltpu.VMEM((f, d), BF16)]),
        input_output_aliases={2: 0},
        compiler_params=_cparams("arbitrary"),
    )(te, nu, xs, w1, w3, w2)


def _combine_kernel(cnt_ref, ls_ref, gd_ref, ys_ref, lp_ref, rt_ref, x1_ref, g2p_ref, g2s_ref, fg_ref,
                    yp_o, ys_o, ybuf, sem, *, npt):
    i = pl.program_id(0)
    nt = npt + 1

    def copy(step):
        return lambda k: _run_copy(ys_ref, ybuf.at[step % 2], gd_ref[k], ls_ref[k], cnt_ref[k], sem.at[step % 2])

    def fetch(step):
        ybuf[step % 2] = jnp.zeros(ybuf.shape[1:], BF16)
        _each_run(step, lambda k: copy(step)(k).start())

    @pl.when(i == 0)
    def _():
        fetch(0)

    @pl.when(i + 1 < nt)
    def _():
        fetch(i + 1)

    _each_run(i, lambda k: copy(i)(k).wait())

    yb = ybuf[i % 2]
    slot = lax.broadcasted_iota(I32, (TM, SLOTS), 1).astype(F32)
    lp = lp_ref[...]
    rt = rt_ref[...]

    def unsort(col):
        return jnp.dot((slot == lp[:, col:col + 1]).astype(BF16), yb, preferred_element_type=F32)

    moe = rt[:, 2:3] * unsort(0) + rt[:, 3:4] * unsort(1)
    g2 = jnp.where(i >= npt, g2s_ref[...], g2p_ref[...])
    y = _rms(x1_ref[...] + g2 * moe, fg_ref[...])

    @pl.when(i < npt)
    def _():
        yp_o[...] = y

    @pl.when(i >= npt)
    def _():
        ys_o[...] = y


def _combine(cnt, ls, gd, ys, lp, rt, x1, modp, mods, fg, *, npt, tpb, nb):
    d = x1.shape[1]
    nt = npt + 1
    row = lambda i, *_: (i, 0)
    return pl.pallas_call(
        functools.partial(_combine_kernel, npt=npt),
        out_shape=[jax.ShapeDtypeStruct((npt * TM, d), F32), jax.ShapeDtypeStruct((TM, d), F32)],
        grid_spec=pltpu.PrefetchScalarGridSpec(
            num_scalar_prefetch=3,
            grid=(nt,),
            in_specs=[pl.BlockSpec(memory_space=pl.ANY),
                      pl.BlockSpec((TM, LANES), row), pl.BlockSpec((TM, LANES), row), pl.BlockSpec((TM, d), row),
                      pl.BlockSpec((None, 1, d), lambda i, *_: (jnp.minimum(i // tpb, nb - 1), 0, 5)),
                      pl.BlockSpec((TM, d), lambda i, *_: (0, 5)),
                      pl.BlockSpec((1, d), lambda i, *_: (0, 0))],
            out_specs=[pl.BlockSpec((TM, d), lambda i, *_: (jnp.minimum(i, npt - 1), 0)),
                       pl.BlockSpec((TM, d), lambda i, *_: (0, 0))],
            scratch_shapes=[pltpu.VMEM((2, SLOTS, d), BF16), pltpu.SemaphoreType.DMA((2,))]),
        compiler_params=_cparams("arbitrary"),
    )(cnt, ls, gd, ys, lp, rt, x1, modp, mods, fg)


def _rotation_tables(t):
    pos = jnp.concatenate([jnp.arange(t, dtype=I32), jnp.full((TM,), PAST_LEN, I32)]).astype(F32)
    inv_r = jnp.repeat(1.0 / (ROPE_THETA ** jnp.linspace(0.0, 1.0, RET_DK // 2, dtype=F32)), 2)
    ang_r = pos[:, None] * inv_r[None, :]
    sign_r = jnp.where(jnp.arange(RET_DK) % 2 == 0, -1.0, 1.0).astype(F32)
    inv_w = jnp.tile(ROPE_THETA ** (-jnp.arange(0, SWA_HD, 2, dtype=F32) / SWA_HD), LANES // (SWA_HD // 2))
    ang_w = pos[:, None] * inv_w[None, :]
    sign_w = jnp.where(jnp.arange(LANES) % SWA_HD < SWA_HD // 2, -1.0, 1.0).astype(F32)
    return jnp.cos(ang_r), jnp.sin(ang_r) * sign_r[None, :], jnp.cos(ang_w), jnp.sin(ang_w) * sign_w[None, :]


def kernel(x_prompt, x_sample, c_prompt, c_sample, state_ret, cache_swa_k, cache_swa_v, w_ada, b_ada, norm1_g, norm2_g, w_in, w_up_ret, w_up_swa, w_o, sink, w_rg, b_rg, w_re, b_re, w1, w3, w2, final_g):
    nb, t, d = x_prompt.shape
    ns, dec_seq, _ = x_sample.shape
    depth = w_ada.shape[0]
    assert depth == 1 and dec_seq == 1, "single layer, one new token per sequence"
    assert t % TM == 0 and ns <= TM and ns % 16 == 0 and d % (2 * LANES) == 0
    assert t % (RET_STEP_CHUNKS * RET_CHUNK) == 0 and t % (SWA_STEP_BLOCKS * WINDOW) == 0
    assert N_GROUPS + N_EXPERTS <= LANES
    w = cache_swa_k.shape[2]
    tpb = t // TM
    npt = nb * tpb
    nt = npt + 1
    np_rows = nb * t
    n_tok = np_rows + ns
    maxt = -(-(2 * n_tok + nt * N_EXPERTS * RUN + N_EXPERTS * (TE - 1)) // TE)

    xp = x_prompt.reshape(np_rows, d)
    xs_pad = jnp.pad(x_sample.reshape(ns, d), ((0, TM - ns), (0, 0)))

    c_all = jnp.concatenate([jnp.pad(c_prompt, ((0, SUBLANES - nb % SUBLANES), (0, 0))),
                             jnp.pad(c_sample, ((0, TM - ns), (0, 0)))])
    mod = _modulation(c_all, w_ada[0], b_ada[0])
    modp = mod[:nb].reshape(nb, 1, 6 * d)
    mods = mod[c_all.shape[0] - TM:]

    tabs = _rotation_tables(t)
    rq, rk, rv, rg, sq, sk, sv, siga, sigb = _inproj(
        xp, xs_pad, modp, mods, norm1_g, w_in[0].astype(BF16), tabs, npt=npt, tpb=tpb, nb=nb)

    dm, qd, kd, cd, gamma = _ret_tables()
    gated_p, st_p = _retention_prompt(rq, rk, rv, rg, (dm, qd, kd, cd), nb=nb, t=t)
    gated_s, st_s = _retention_sample(rq, rk, rv, rg, state_ret[0], gamma, row0=np_rows, ns=ns)
    oswa_p = _swa_prompt(sq, sk, sv, sink[0], nb=nb, t=t)
    to_t = lambda c: jnp.transpose(c[0], (0, 2, 3, 1)).reshape(ns, _SWA_KW, w)
    from_t = lambda c: jnp.transpose(c.reshape(ns, SWA_KV_HEADS, SWA_HD, w), (0, 3, 1, 2))[None]
    oswa_s, ks_new, vs_new = _swa_sample(sq, sk, sv, to_t(cache_swa_k), to_t(cache_swa_v), sink[0],
                                         row0=np_rows, ns=ns)
    gated_s = jnp.pad(gated_s, ((0, TM - ns), (0, 0)))
    oswa_s = jnp.pad(oswa_s, ((0, TM - ns), (0, 0)))

    wr = jnp.pad(jnp.concatenate([w_rg[0], w_re[0]], axis=1), ((0, 0), (0, LANES - N_GROUPS - N_EXPERTS)))
    br = jnp.pad(jnp.concatenate([b_rg[0], b_re[0]]), (0, LANES - N_GROUPS - N_EXPERTS)).reshape(1, LANES)
    x1, h2, rt, lp, cnt, ls, gb = _outproj(
        gated_p, gated_s, oswa_p, oswa_s, siga, sigb, xp, xs_pad, modp, mods, norm2_g,
        w_up_ret[0].astype(BF16), w_up_swa[0].astype(BF16), w_o[0].astype(BF16), wr, br,
        npt=npt, tpb=tpb, nb=nb, ns=ns)
    cnt = cnt[:, 0, :N_EXPERTS]
    ls = ls[:, 0, :N_EXPERTS]
    gb = gb[:, 0, :N_EXPERTS]
    seg = jnp.sum(cnt, axis=0)
    tiles = (seg + TE - 1) // TE
    tile_end = jnp.cumsum(tiles)
    row_start = (tile_end - tiles) * TE
    gd = (gb + row_start[None, :]).reshape(-1)
    n_used = tile_end[-1:]
    jj = jnp.minimum(jnp.arange(maxt, dtype=I32), n_used[0] - 1)
    te = jnp.minimum(jnp.sum((tile_end[None, :] <= jj[:, None]).astype(I32), axis=1), N_EXPERTS - 1)
    cnt = cnt.reshape(-1)
    ls = ls.reshape(-1)
    n_used = n_used.astype(I32)
    xs = _dispatch(cnt, ls, gd, row_start + seg, tiles * TE - seg, n_used, h2, lp, nt=nt, maxt=maxt)
    ys = _experts(te, n_used, xs, w1[0], w3[0], w2[0], maxt=maxt)
    y_p, y_s = _combine(cnt, ls, gd, ys, lp, rt, x1, modp, mods, final_g.reshape(1, d), npt=npt, tpb=tpb, nb=nb)

    y_prompt = y_p.reshape(nb, t, d)
    y_sample = y_s[:ns].reshape(ns, 1, d)
    wk = min(WINDOW, t)
    last = lambda a: jnp.stack([a[(b + 1) * t - wk:(b + 1) * t] for b in range(nb)]).reshape(
        nb, wk, SWA_KV_HEADS, SWA_HD)
    skp, svp = last(sk), last(sv)
    return (y_prompt, y_sample, st_p[None], st_s[None], skp[None], svp[None], from_t(ks_new), from_t(vs_new))
```

```python
import functools

import jax
import jax.numpy as jnp
from jax import lax
from jax.experimental import pallas as pl
from jax.experimental.pallas import tpu as pltpu

F32 = jnp.float32
BF16 = jnp.bfloat16
I32 = jnp.int32

PAST_LEN = 8192
RET_HEADS = 4
RET_DK = 128
RET_DV = 128
RET_CHUNK = 128
SWA_HEADS = 8
SWA_KV_HEADS = 2
SWA_HD = 64
WINDOW = 128
ROPE_THETA = 10000.0
N_GROUPS = 4
EXPERTS_PER_GROUP = 8
N_EXPERTS = N_GROUPS * EXPERTS_PER_GROUP
D_EXPERT = 256
NORM_EPS = 1e-6

LANES = 128
SUBLANES = 8
TM = 256
TT = 2 * TM
RUN = 16
SLOTS = 2 * TM + N_EXPERTS * RUN
TE = 512
TE_SUB = 256
VMEM_LIMIT = 56 * 1024 * 1024

_RET_W = RET_HEADS * RET_DK
_SWA_QW = SWA_HEADS * SWA_HD
_SWA_KW = SWA_KV_HEADS * SWA_HD


def _cparams(*sem):
    return pltpu.CompilerParams(dimension_semantics=sem, vmem_limit_bytes=VMEM_LIMIT)


def _sigmoid(x):
    return 1.0 / (1.0 + jnp.exp(-x))


def _silu(x):
    return x * _sigmoid(x)


def _bdot(a, b):
    return jnp.dot(a.astype(BF16), b.astype(BF16), preferred_element_type=F32)


def _bdot_nt(a, b):
    return lax.dot_general(a.astype(BF16), b.astype(BF16), (((1,), (1,)), ((), ())), preferred_element_type=F32)


def _mod_kernel(c_ref, w_ref, b_ref, o_ref):
    o_ref[...] = _bdot(_silu(c_ref[...]), w_ref[...]) + b_ref[...]


def _modulation(c_all, w_ada, b_ada):
    rows, d = c_all.shape
    n = w_ada.shape[1]
    return pl.pallas_call(
        _mod_kernel,
        out_shape=jax.ShapeDtypeStruct((rows, n), F32),
        grid=(n // d,),
        in_specs=[pl.BlockSpec((rows, d), lambda j: (0, 0)),
                  pl.BlockSpec((d, d), lambda j: (0, j)),
                  pl.BlockSpec((1, d), lambda j: (0, j))],
        out_specs=pl.BlockSpec((rows, d), lambda j: (0, j)),
        compiler_params=_cparams("arbitrary"),
    )(c_all, w_ada, b_ada.reshape(1, n))


def _rms(x, g):
    return x * lax.rsqrt(jnp.mean(x * x, axis=-1, keepdims=True) + NORM_EPS) * g


def _pair_rotate(z, cos, sin_signed):
    n = z.shape[-1]
    lane = lax.broadcasted_iota(I32, z.shape, 1)
    partner = jnp.where((lane & 1) == 0, pltpu.roll(z, n - 1, 1), pltpu.roll(z, 1, 1))
    reps = n // LANES
    cos = jnp.concatenate([cos] * reps, axis=1) if reps > 1 else cos
    sin_signed = jnp.concatenate([sin_signed] * reps, axis=1) if reps > 1 else sin_signed
    return z * cos + partner * sin_signed


def _half_rotate(z, cos, sin_signed):
    n = z.shape[-1]
    half = SWA_HD // 2
    lane = lax.broadcasted_iota(I32, z.shape, 1)
    partner = jnp.where((lane & (SWA_HD - 1)) < half, pltpu.roll(z, n - half, 1), pltpu.roll(z, half, 1))
    reps = n // LANES
    cos = jnp.concatenate([cos] * reps, axis=1) if reps > 1 else cos
    sin_signed = jnp.concatenate([sin_signed] * reps, axis=1) if reps > 1 else sin_signed
    return z * cos + partner * sin_signed


def _inproj_kernel(xp_ref, xs_ref, shp_ref, scp_ref, shs_ref, scs_ref, n1_ref, w_ref,
                   cr_ref, sr_ref, cw_ref, sw_ref,
                   rq_o, rk_o, rv_o, rg_o, sq_o, sk_o, sv_o, za_o, zb_o, *, nps):
    is_s = pl.program_id(0) >= nps
    d = xp_ref.shape[-1]
    for r in range(0, TT, TM):
        rows = slice(r, r + TM)
        x = jnp.where(is_s, xs_ref[rows], xp_ref[rows])
        sh = jnp.where(is_s, shs_ref[rows], shp_ref[...])
        sc = jnp.where(is_s, scs_ref[rows], scp_ref[...])
        h = (_rms(x, n1_ref[...]) * (1.0 + sc) + sh).astype(BF16)
        cr, sr, cw, sw = cr_ref[rows], sr_ref[rows], cw_ref[rows], sw_ref[rows]

        def seg(a, b):
            return jnp.dot(h, w_ref[:, a:b], preferred_element_type=F32)

        o = 0
        rq_o[rows] = _pair_rotate(seg(o, o + _RET_W), cr, sr).astype(BF16)
        o += _RET_W
        rk_o[rows] = (_pair_rotate(seg(o, o + _RET_W), cr, sr) * (RET_DK ** -0.5)).astype(BF16)
        o += _RET_W
        rv_o[rows] = seg(o, o + _RET_W).astype(BF16)
        o += _RET_W
        rg_o[rows] = _silu(seg(o, o + _RET_W)).astype(BF16)
        o += _RET_W
        sq_o[rows] = (_half_rotate(seg(o, o + _SWA_QW), cw, sw) * (SWA_HD ** -0.5)).astype(BF16)
        o += _SWA_QW
        zkv = seg(o, o + 2 * _SWA_KW)
        sk_o[rows] = _half_rotate(zkv[:, :_SWA_KW], cw, sw)
        sv_o[rows] = zkv[:, _SWA_KW:]
        o += 2 * _SWA_KW
        za_o[rows] = _sigmoid(seg(o, o + d)).astype(BF16)
        o += d
        zb_o[rows] = _sigmoid(seg(o, o + d)).astype(BF16)


def _inproj(xp, xs_pad, modp, mods, n1, w_in_b, tabs, *, nps, spb, nb):
    d = xp.shape[1]
    nrow = (nps + 1) * TT
    n_in = w_in_b.shape[1]
    pstep = lambda i: (jnp.minimum(i, nps - 1), 0)
    pbatch = lambda col: (lambda i: (jnp.minimum(i // spb, nb - 1), 0, col))
    tab_idx = lambda i: (jnp.where(i < nps, i % spb, spb), 0)
    out_cols = [(_RET_W, BF16)] * 4 + [(_SWA_QW, BF16), (_SWA_KW, F32), (_SWA_KW, F32), (d, BF16), (d, BF16)]
    return pl.pallas_call(
        functools.partial(_inproj_kernel, nps=nps),
        out_shape=[jax.ShapeDtypeStruct((nrow, c), t) for c, t in out_cols],
        grid=(nps + 1,),
        in_specs=[pl.BlockSpec((TT, d), pstep),
                  pl.BlockSpec((TT, d), lambda i: (0, 0)),
                  pl.BlockSpec((None, 1, d), pbatch(0)),
                  pl.BlockSpec((None, 1, d), pbatch(1)),
                  pl.BlockSpec((TT, d), lambda i: (0, 0)),
                  pl.BlockSpec((TT, d), lambda i: (0, 1)),
                  pl.BlockSpec((1, d), lambda i: (0, 0)),
                  pl.BlockSpec((d, n_in), lambda i: (0, 0))]
                 + [pl.BlockSpec((TT, LANES), tab_idx)] * 4,
        out_specs=[pl.BlockSpec((TT, c), lambda i: (i, 0)) for c, _ in out_cols],
        compiler_params=_cparams("arbitrary"),
    )(xp, xs_pad, modp, modp, mods, mods, n1, w_in_b, *tabs)


RET_STEP_CHUNKS = 4


def _ret_kernel(q_ref, k_ref, v_ref, g_ref, dm_ref, qd_ref, kd_ref, cd_ref, o_ref, st_ref, s_scr, *, nsteps):
    step = pl.program_id(1)

    @pl.when(step == 0)
    def _():
        s_scr[...] = jnp.zeros_like(s_scr)

    for h in range(RET_HEADS):
        sl = slice(h * RET_DK, (h + 1) * RET_DK)
        state = s_scr[h]
        for c in range(RET_STEP_CHUNKS):
            rows = slice(c * RET_CHUNK, (c + 1) * RET_CHUNK)
            q, k, v = q_ref[rows, sl], k_ref[rows, sl], v_ref[rows, sl]
            att = _bdot_nt(q, k) * dm_ref[h]
            o = _bdot(att, v) + _bdot(q.astype(F32) * qd_ref[h], state)
            kd = (k.astype(F32) * kd_ref[h]).astype(BF16)
            kv = lax.dot_general(kd, v, (((0,), (0,)), ((), ())), preferred_element_type=F32)
            state = cd_ref[h] * state + kv
            o = o * lax.rsqrt(jnp.mean(o * o, axis=-1, keepdims=True) + NORM_EPS)
            o_ref[rows, sl] = (o * g_ref[rows, sl].astype(F32)).astype(BF16)
        s_scr[h] = state

    @pl.when(step == nsteps - 1)
    def _():
        st_ref[...] = s_scr[...]


def _ret_tables():
    ld = jnp.log(1.0 - 2.0 ** (-5.0 - jnp.arange(RET_HEADS, dtype=F32)))
    idx = jnp.arange(RET_CHUNK, dtype=F32)
    diff = idx[:, None] - idx[None, :]
    causal = diff >= 0
    dmask = jnp.where(causal[None], jnp.exp(ld[:, None, None] * jnp.where(causal, diff, 0.0)[None]), 0.0)
    k_dec = jnp.exp(ld[None, :] * (RET_CHUNK - 1.0 - idx)[:, None])
    q_dec = jnp.exp(ld[None, :] * (idx + 1.0)[:, None])
    chunk_decay = jnp.exp(ld * RET_CHUNK)
    bc = lambda t: jnp.broadcast_to(t.T[:, :, None], (RET_HEADS, RET_CHUNK, RET_DV))
    cd = jnp.broadcast_to(chunk_decay[:, None, None], (RET_HEADS, 1, RET_DV))
    return dmask, bc(q_dec), bc(k_dec), cd, jnp.exp(ld)


def _retention_prompt(rq, rk, rv, rg, tabs, *, nb, t):
    rows = RET_STEP_CHUNKS * RET_CHUNK
    nsteps = t // rows
    dm, qd, kd, cd = tabs
    blk = lambda b, c: (b * nsteps + c, 0)
    full3 = lambda b, c: (0, 0, 0)
    return pl.pallas_call(
        functools.partial(_ret_kernel, nsteps=nsteps),
        out_shape=[jax.ShapeDtypeStruct((nb * t, _RET_W), BF16),
                   jax.ShapeDtypeStruct((nb, RET_HEADS, RET_DK, RET_DV), F32)],
        grid=(nb, nsteps),
        in_specs=[pl.BlockSpec((rows, _RET_W), blk)] * 4
                 + [pl.BlockSpec((RET_HEADS, RET_CHUNK, RET_DV), full3)] * 3
                 + [pl.BlockSpec((RET_HEADS, 1, RET_DV), full3)],
        out_specs=[pl.BlockSpec((rows, _RET_W), blk),
                   pl.BlockSpec((None, RET_HEADS, RET_DK, RET_DV), lambda b, c: (b, 0, 0, 0))],
        scratch_shapes=[pltpu.VMEM((RET_HEADS, RET_DK, RET_DV), F32)],
        compiler_params=_cparams("arbitrary", "arbitrary"),
    )(rq, rk, rv, rg, dm, qd, kd, cd)


def _ret_sample_kernel(gam_ref, q_ref, k_ref, v_ref, g_ref, s0_ref, o_ref, st_ref, *, sb):
    gamma = gam_ref[pl.program_id(1)]
    q = q_ref[...].astype(F32)
    k = k_ref[...].astype(F32)
    v = v_ref[...].astype(F32)
    rows = sb * RET_DK
    s2 = s0_ref[...].reshape(rows, RET_DV)
    col_b = lax.broadcasted_iota(I32, (sb, rows), 1) // RET_DK
    row_b = lax.broadcasted_iota(I32, (sb, rows), 0)
    qexp = jnp.where(col_b == row_b, jnp.concatenate([q * gamma] * sb, axis=1), 0.0)
    o = jnp.sum(q * k, axis=-1, keepdims=True) * v + _bdot(qexp, s2)
    o = o * lax.rsqrt(jnp.mean(o * o, axis=-1, keepdims=True) + NORM_EPS)
    o_ref[...] = (o * g_ref[...].astype(F32)).astype(BF16)
    rep = (lax.broadcasted_iota(I32, (rows, sb), 0) // RET_DK == lax.broadcasted_iota(I32, (rows, sb), 1))
    rep = rep.astype(BF16)
    krep = _bdot(rep, k)
    vrep = _bdot(rep, v)
    eye = (lax.broadcasted_iota(I32, (rows, RET_DK), 0) % RET_DK == lax.broadcasted_iota(I32, (rows, RET_DK), 1))
    kcol = jnp.sum(jnp.where(eye, krep, 0.0), axis=-1, keepdims=True)
    st_ref[...] = (gamma * s2 + kcol * vrep).reshape(sb, RET_DK, RET_DV)


def _retention_sample(rq, rk, rv, rg, s0, gamma, *, row0, ns):
    sb = min(64, ns)
    base = row0 // sb
    blk = lambda i, h: (base + i, h)
    sblk = lambda i, h: (i, h, 0, 0)
    return pl.pallas_call(
        functools.partial(_ret_sample_kernel, sb=sb),
        out_shape=[jax.ShapeDtypeStruct((ns, _RET_W), BF16),
                   jax.ShapeDtypeStruct(s0.shape, F32)],
        grid=(ns // sb, RET_HEADS),
        in_specs=[pl.BlockSpec(memory_space=pltpu.SMEM)]
                 + [pl.BlockSpec((sb, RET_DK), blk)] * 4
                 + [pl.BlockSpec((sb, None, RET_DK, RET_DV), sblk)],
        out_specs=[pl.BlockSpec((sb, RET_DV), lambda i, h: (i, h)),
                   pl.BlockSpec((sb, None, RET_DK, RET_DV), sblk)],
        compiler_params=_cparams("arbitrary", "arbitrary"),
    )(gamma, rq, rk, rv, rg, s0)


def _sink_softmax(s, mask, sink):
    s = jnp.where(mask, s, -jnp.inf)
    m = jnp.maximum(jnp.max(s, axis=-1, keepdims=True), sink)
    p = jnp.exp(s - m)
    return p / (jnp.sum(p, axis=-1, keepdims=True) + jnp.exp(sink - m))


def _split_kv_heads(x):
    lo = lax.broadcasted_iota(I32, x.shape, 1) < SWA_HD
    h0_lo = jnp.where(lo, x, 0.0)
    h1_hi = jnp.where(lo, 0.0, x)
    return ((h0_lo, pltpu.roll(h0_lo, SWA_HD, 1)), (pltpu.roll(h1_hi, SWA_HD, 1), h1_hi))


SWA_STEP_BLOCKS = 4


def _swa_kernel(sink_ref, q_ref, kc_ref, kp_ref, vc_ref, vp_ref, o_ref):
    n = pl.program_id(1)
    c = WINDOW
    kk = jnp.concatenate([kp_ref[...], kc_ref[...]], axis=0)
    vv = jnp.concatenate([vp_ref[...], vc_ref[...]], axis=0)
    ks = [[a.astype(BF16) for a in pair] for pair in _split_kv_heads(kk)]
    vs = [[a.astype(BF16) for a in pair] for pair in _split_kv_heads(vv)]
    qi = lax.broadcasted_iota(I32, (2 * c, 2 * c), 0) % c
    ki = lax.broadcasted_iota(I32, (2 * c, 2 * c), 1)
    band = (ki > qi) & (ki <= qi + c)
    top = lax.broadcasted_iota(I32, (2 * c, 1), 0) < c
    for s in range(SWA_STEP_BLOCKS):
        rows = slice(s * c, (s + 1) * c)
        keys = slice(s * c, (s + 2) * c)
        mask = (band & ((ki >= c) | (n > 0))) if s == 0 else band
        for kvh in range(SWA_KV_HEADS):
            j0, j1 = 2 * kvh, 2 * kvh + 1
            q2 = jnp.concatenate([q_ref[rows, j0 * LANES:(j0 + 1) * LANES],
                                  q_ref[rows, j1 * LANES:(j1 + 1) * LANES]], axis=0)
            kcat = jnp.concatenate([ks[kvh][0][keys], ks[kvh][1][keys]], axis=0)
            vcat = jnp.concatenate([vs[kvh][0][keys], vs[kvh][1][keys]], axis=0)
            sc = lax.dot_general(q2, kcat, (((1,), (1,)), ((), ())), preferred_element_type=F32)
            ps = []
            for half in range(2):
                sink = jnp.where(top, sink_ref[2 * j0 + half], sink_ref[2 * j1 + half])
                ps.append(_sink_softmax(sc[:, half * 2 * c:(half + 1) * 2 * c], mask, sink).astype(BF16))
            o = jnp.dot(jnp.concatenate(ps, axis=1), vcat, preferred_element_type=F32)
            o_ref[rows, j0 * LANES:(j0 + 1) * LANES] = o[:c].astype(BF16)
            o_ref[rows, j1 * LANES:(j1 + 1) * LANES] = o[c:].astype(BF16)


def _swa_prompt(sq, sk, sv, sink, *, nb, t):
    rows = SWA_STEP_BLOCKS * WINDOW
    nsteps = t // rows
    nblk = t // WINDOW
    cur = lambda b, n: (b * nsteps + n, 0)
    prev = lambda b, n: (b * nblk + jnp.maximum(n * SWA_STEP_BLOCKS - 1, 0), 0)
    return pl.pallas_call(
        _swa_kernel,
        out_shape=jax.ShapeDtypeStruct((nb * t, _SWA_QW), BF16),
        grid=(nb, nsteps),
        in_specs=[pl.BlockSpec(memory_space=pltpu.SMEM),
                  pl.BlockSpec((rows, _SWA_QW), cur),
                  pl.BlockSpec((rows, _SWA_KW), cur),
                  pl.BlockSpec((WINDOW, _SWA_KW), prev),
                  pl.BlockSpec((rows, _SWA_KW), cur),
                  pl.BlockSpec((WINDOW, _SWA_KW), prev)],
        out_specs=pl.BlockSpec((rows, _SWA_QW), cur),
        compiler_params=_cparams("arbitrary", "arbitrary"),
    )(sink, sq, sk, sk, sv, sv)


def _swa_sample_kernel(sink_ref, q_ref, kn_ref, vn_ref, kc_ref, vc_ref, o_ref, ko_ref, vo_ref, *, sb, w):
    pad = jnp.zeros((LANES - sb, _SWA_KW), F32)
    knt = jnp.concatenate([kn_ref[...], pad], axis=0).T
    vnt = jnp.concatenate([vn_ref[...], pad], axis=0).T
    kall = jnp.concatenate([kc_ref[b] for b in range(sb)] + [knt], axis=1)
    vall = jnp.concatenate([vc_ref[b] for b in range(sb)] + [vnt], axis=1)
    ncol = sb * w + LANES
    lo = lax.broadcasted_iota(I32, (sb, LANES), 1) < SWA_HD
    group = SWA_HEADS // SWA_KV_HEADS
    pieces = []
    for h in range(SWA_HEADS):
        slab = q_ref[:, (h // 2) * LANES:(h // 2 + 1) * LANES].astype(F32)
        mine = jnp.where(lo, slab, 0.0) if h % 2 == 0 else jnp.where(lo, 0.0, slab)
        pieces.append(mine if (h % 2) == (h // group) else pltpu.roll(mine, SWA_HD, 1))
    qrows = jnp.concatenate(pieces, axis=0)
    nrow = SWA_HEADS * sb
    s = _bdot(qrows, kall)
    rb = lax.broadcasted_iota(I32, (nrow, ncol), 0) % sb
    ci = lax.broadcasted_iota(I32, (nrow, ncol), 1)
    in_cache = (ci < sb * w) & (ci // w == rb) & ((w - ci % w) < WINDOW)
    mask = in_cache | (ci == sb * w + rb)
    sink_col = jnp.concatenate([jnp.full((sb, 1), sink_ref[h], F32) for h in range(SWA_HEADS)], axis=0)
    p = _sink_softmax(s, mask, sink_col)
    o = _bdot_nt(p, vall)
    for j in range(SWA_HEADS // 2):
        acc = jnp.zeros((sb, LANES), F32)
        for half in range(2):
            h = 2 * j + half
            oh = o[h * sb:(h + 1) * sb]
            own = jnp.where(lo, oh, 0.0) if h // group == 0 else jnp.where(lo, 0.0, oh)
            acc = acc + (own if (h // group) == half else pltpu.roll(own, SWA_HD, 1))
        o_ref[:, j * LANES:(j + 1) * LANES] = acc.astype(BF16)
    newest = lax.broadcasted_iota(I32, (_SWA_KW, w), 1) == w - 1
    for b in range(sb):
        ko_ref[b] = jnp.where(newest, knt[:, b:b + 1], pltpu.roll(kc_ref[b], w - 1, 1))
        vo_ref[b] = jnp.where(newest, vnt[:, b:b + 1], pltpu.roll(vc_ref[b], w - 1, 1))


def _swa_sample(sq, sk, sv, cache_kt, cache_vt, sink, *, row0, ns):
    sb = min(16, ns)
    w = cache_kt.shape[2]
    base = row0 // sb
    blk = lambda i: (base + i, 0)
    cblk = lambda i: (i, 0, 0)
    cspec = pl.BlockSpec((sb, _SWA_KW, w), cblk)
    return pl.pallas_call(
        functools.partial(_swa_sample_kernel, sb=sb, w=w),
        out_shape=[jax.ShapeDtypeStruct((ns, _SWA_QW), BF16),
                   jax.ShapeDtypeStruct(cache_kt.shape, F32), jax.ShapeDtypeStruct(cache_vt.shape, F32)],
        grid=(ns // sb,),
        in_specs=[pl.BlockSpec(memory_space=pltpu.SMEM),
                  pl.BlockSpec((sb, _SWA_QW), blk),
                  pl.BlockSpec((sb, _SWA_KW), blk),
                  pl.BlockSpec((sb, _SWA_KW), blk),
                  cspec, cspec],
        out_specs=[pl.BlockSpec((sb, _SWA_QW), lambda i: (i, 0)), cspec, cspec],
        compiler_params=_cparams("arbitrary"),
    )(sink, sq, sk, sv, cache_kt, cache_vt)


def _route(logits):
    lane = lax.broadcasted_iota(I32, logits.shape, 1)
    big = jnp.int32(1 << 20)
    neg = -jnp.inf

    def top(mask):
        v = jnp.max(jnp.where(mask, logits, neg), axis=-1, keepdims=True)
        i = jnp.min(jnp.where(mask & (logits == v), lane, big), axis=-1, keepdims=True)
        return v, i

    gmask = lane < N_GROUPS
    gmax, gsel = top(gmask)
    p_group = 1.0 / jnp.sum(jnp.where(gmask, jnp.exp(logits - gmax), 0.0), axis=-1, keepdims=True)
    first = N_GROUPS + gsel * EXPERTS_PER_GROUP
    emask = (lane >= first) & (lane < first + EXPERTS_PER_GROUP)
    v1, i1 = top(emask)
    v2, i2 = top(emask & (lane != i1))
    t = jnp.exp(v2 - v1)
    w1 = p_group / (1.0 + t)
    return i1 - N_GROUPS, i2 - N_GROUPS, w1, w1 * t


def _plan_tile(e1, e2, valid, carry):
    lane = lax.broadcasted_iota(I32, (TM, LANES), 1)
    oh1 = ((lane == e1) & valid).astype(F32)
    oh2 = ((lane == e2) & valid).astype(F32)
    oh = oh1 + oh2
    tri = (lax.broadcasted_iota(I32, (TM, TM), 0) > lax.broadcasted_iota(I32, (TM, TM), 1)).astype(BF16)
    before = _bdot(tri, oh)
    cnt = jnp.sum(oh, axis=0, keepdims=True)
    units = jnp.maximum(jnp.floor((cnt + (RUN - 1)) * (1.0 / RUN)), 1.0)
    upper = (lax.broadcasted_iota(I32, (LANES, LANES), 0) < lax.broadcasted_iota(I32, (LANES, LANES), 1))
    lstart = RUN * _bdot(jnp.broadcast_to(units, (SUBLANES, LANES)), upper.astype(BF16))[0:1]
    slot = lstart + before
    lp1 = jnp.sum(oh1 * slot, axis=-1, keepdims=True)
    lp2 = jnp.sum(oh2 * slot, axis=-1, keepdims=True)
    vcol = valid[:, 0:1]
    lp = jnp.where(lane == 0, jnp.where(vcol, lp1, -1.0), jnp.where(lane == 1, jnp.where(vcol, lp2, -1.0), 0.0))
    base = carry[...]
    carry[...] = base + RUN * units
    return lp, (RUN * units).astype(I32), lstart.astype(I32), base.astype(I32)


def _outproj_kernel(gtp_ref, gts_ref, osp_ref, oss_ref, sa_ref, sb_ref, xp_ref, xs_ref, g1p_ref, shp_ref, scp_ref,
                    g1s_ref, shs_ref, scs_ref, n2_ref, wur_ref, wus_ref, wo_ref, wrh_ref, wrl_ref, br_ref,
                    x1_o, h2_o, rt_o, lp_o, cnt_o, ls_o, gb_o, carry, *, nps, ns):
    i = pl.program_id(0)
    is_s = i >= nps

    @pl.when(i == 0)
    def _():
        carry[...] = jnp.zeros_like(carry)

    routed = []
    for r in range(0, TT, TM):
        rows = slice(r, r + TM)
        x = jnp.where(is_s, xs_ref[rows], xp_ref[rows])
        g1 = jnp.where(is_s, g1s_ref[rows], g1p_ref[...])
        sh = jnp.where(is_s, shs_ref[rows], shp_ref[...])
        sc = jnp.where(is_s, scs_ref[rows], scp_ref[...])
        gated = jnp.where(is_s, gts_ref[rows], gtp_ref[rows])
        oswa = jnp.where(is_s, oss_ref[rows], osp_ref[rows])
        y_ret = jnp.dot(gated, wur_ref[...], preferred_element_type=F32)
        y_swa = jnp.dot(oswa, wus_ref[...], preferred_element_type=F32)
        merged = sa_ref[rows].astype(F32) * y_ret + sb_ref[rows].astype(F32) * y_swa
        x1 = x + g1 * jnp.dot(merged.astype(BF16), wo_ref[...], preferred_element_type=F32)
        x1_o[rows] = x1
        h2 = _rms(x1, n2_ref[...]) * (1.0 + sc) + sh
        hi = h2.astype(BF16)
        h2_o[rows] = hi
        lo = (h2 - hi.astype(F32)).astype(BF16)
        wrh = wrh_ref[...]
        logits = (jnp.dot(hi, wrh, preferred_element_type=F32) + jnp.dot(lo, wrh, preferred_element_type=F32)
                  + jnp.dot(hi, wrl_ref[...], preferred_element_type=F32) + br_ref[...])
        e1, e2, w1, w2 = _route(logits)
        lane = lax.broadcasted_iota(I32, logits.shape, 1)
        rt_o[rows] = jnp.where(lane == 2, w1, jnp.where(lane == 3, w2, 0.0))
        routed.append((e1, e2))
    for sub, (e1, e2) in enumerate(routed):
        row = lax.broadcasted_iota(I32, (TM, LANES), 0) + sub * TM
        valid = jnp.logical_not(is_s) | (row < ns)
        lp_o[sub * TM:(sub + 1) * TM], cnt_o[sub], ls_o[sub], gb_o[sub] = _plan_tile(e1, e2, valid, carry)


def _outproj(gated_p, gated_s, oswa_p, oswa_s, siga, sigb, xp, xs_pad, modp, mods, n2, wur, wus, wo, wr, br,
             *, nps, spb, nb, ns):
    d = xp.shape[1]
    sub = TT // TM
    nrow = (nps + 1) * TT
    row = lambda i: (i, 0)
    pstep = lambda i: (jnp.minimum(i, nps - 1), 0)
    pbatch = lambda col: (lambda i: (jnp.minimum(i // spb, nb - 1), 0, col))
    scol = lambda col: (lambda i: (0, col))
    const = lambda i: (0, 0)
    wr_hi = wr.astype(BF16)
    wr_lo = (wr - wr_hi.astype(F32)).astype(BF16)
    meta = jax.ShapeDtypeStruct(((nps + 1) * sub, 1, LANES), I32)
    mspec = pl.BlockSpec((sub, 1, LANES), lambda i: (i, 0, 0))
    return pl.pallas_call(
        functools.partial(_outproj_kernel, nps=nps, ns=ns),
        out_shape=[jax.ShapeDtypeStruct((nrow, d), F32),
                   jax.ShapeDtypeStruct((nrow, d), BF16),
                   jax.ShapeDtypeStruct((nrow, LANES), F32),
                   jax.ShapeDtypeStruct((nrow, LANES), F32), meta, meta, meta],
        grid=(nps + 1,),
        in_specs=[pl.BlockSpec((TT, _RET_W), pstep), pl.BlockSpec((TT, _RET_W), const),
                  pl.BlockSpec((TT, _SWA_QW), pstep), pl.BlockSpec((TT, _SWA_QW), const),
                  pl.BlockSpec((TT, d), row), pl.BlockSpec((TT, d), row),
                  pl.BlockSpec((TT, d), pstep), pl.BlockSpec((TT, d), const),
                  pl.BlockSpec((None, 1, d), pbatch(2)), pl.BlockSpec((None, 1, d), pbatch(3)),
                  pl.BlockSpec((None, 1, d), pbatch(4)),
                  pl.BlockSpec((TT, d), scol(2)), pl.BlockSpec((TT, d), scol(3)), pl.BlockSpec((TT, d), scol(4)),
                  pl.BlockSpec((1, d), const),
                  pl.BlockSpec(wur.shape, const), pl.BlockSpec(wus.shape, const), pl.BlockSpec(wo.shape, const),
                  pl.BlockSpec(wr.shape, const), pl.BlockSpec(wr.shape, const), pl.BlockSpec((1, LANES), const)],
        out_specs=[pl.BlockSpec((TT, d), row), pl.BlockSpec((TT, d), row), pl.BlockSpec((TT, LANES), row),
                   pl.BlockSpec((TT, LANES), row), mspec, mspec, mspec],
        scratch_shapes=[pltpu.VMEM((1, LANES), F32)],
        compiler_params=_cparams("arbitrary"),
    )(gated_p, gated_s, oswa_p, oswa_s, siga, sigb, xp, xs_pad, modp, modp, modp, mods, mods, mods, n2,
      wur, wus, wo, wr_hi, wr_lo, br)


def _aligned(v):
    return v if isinstance(v, int) else pl.multiple_of(v, RUN)


def _run_copy(src, dst, s_start, d_start, n, sem):
    s_start, d_start, n = _aligned(s_start), _aligned(d_start), _aligned(n)
    return pltpu.make_async_copy(src.at[pl.ds(s_start, n)], dst.at[pl.ds(d_start, n)], sem)


def _each_run(step, fn):
    for e in range(N_EXPERTS):
        fn(step * N_EXPERTS + e)


def _dispatch_kernel(cnt_ref, ls_ref, gd_ref, ps_ref, pn_ref, nu_ref, h_ref, lp_ref, xs_ref,
                     sorted_scr, zero_scr, sem, zsem, *, nt, maxt):
    i = pl.program_id(0)

    def pad(e):
        return _run_copy(zero_scr, xs_ref, 0, ps_ref[e], pn_ref[e], zsem)

    def tail(j):
        return _run_copy(zero_scr, xs_ref, 0, j * TE, TE, zsem)

    def each_pad(fn):
        def body(e, c):
            @pl.when(pn_ref[e] > 0)
            def _():
                fn(pad(e))
            return c
        lax.fori_loop(0, N_EXPERTS, body, 0)

    def each_tail(fn):
        def body(j, c):
            fn(tail(j))
            return c
        lax.fori_loop(nu_ref[0], maxt, body, 0)

    @pl.when(i == 0)
    def _():
        zero_scr[...] = jnp.zeros_like(zero_scr)
        each_pad(lambda cp: cp.start())
        each_tail(lambda cp: cp.start())

    lpt = lp_ref[...].T
    slot = lax.broadcasted_iota(I32, (SLOTS, TM), 0).astype(F32)
    perm = ((slot == lpt[0:1]) | (slot == lpt[1:2])).astype(BF16)
    sorted_scr[i % 2] = jnp.dot(perm, h_ref[...], preferred_element_type=F32).astype(BF16)

    def copy(step):
        return lambda k: _run_copy(sorted_scr.at[step % 2], xs_ref, ls_ref[k], gd_ref[k], cnt_ref[k], sem.at[step % 2])

    _each_run(i, lambda k: copy(i)(k).start())

    @pl.when(i > 0)
    def _():
        _each_run(i - 1, lambda k: copy(i - 1)(k).wait())

    @pl.when(i == nt - 1)
    def _():
        _each_run(i, lambda k: copy(i)(k).wait())
        each_pad(lambda cp: cp.wait())
        each_tail(lambda cp: cp.wait())


def _dispatch(cnt, ls, gd, ps, pn, nu, h2, lp, *, nt, maxt):
    d = h2.shape[1]
    return pl.pallas_call(
        functools.partial(_dispatch_kernel, nt=nt, maxt=maxt),
        out_shape=jax.ShapeDtypeStruct((maxt * TE, d), BF16),
        grid_spec=pltpu.PrefetchScalarGridSpec(
            num_scalar_prefetch=6,
            grid=(nt,),
            in_specs=[pl.BlockSpec((TM, d), lambda i, *_: (i, 0)),
                      pl.BlockSpec((TM, LANES), lambda i, *_: (i, 0))],
            out_specs=pl.BlockSpec(memory_space=pl.ANY),
            scratch_shapes=[pltpu.VMEM((2, SLOTS, d), BF16), pltpu.VMEM((TE, d), BF16),
                            pltpu.SemaphoreType.DMA((2,)), pltpu.SemaphoreType.DMA(())]),
        compiler_params=_cparams("arbitrary"),
    )(cnt, ls, gd, ps, pn, nu, h2, lp)


def _experts_kernel(te_ref, nu_ref, x_ref, w1_ref, w3_ref, w2_ref, y_ref, w1b, w3b, w2b):
    j = pl.program_id(0)
    changed = (j == 0) | (te_ref[j] != te_ref[jnp.maximum(j - 1, 0)])

    @pl.when(changed)
    def _():
        w1b[...] = w1_ref[...].astype(BF16)
        w3b[...] = w3_ref[...].astype(BF16)
        w2b[...] = w2_ref[...].astype(BF16)

    @pl.when(j < nu_ref[0])
    def _():
        for r in range(0, TE, TE_SUB):
            x = x_ref[r:r + TE_SUB]
            a = jnp.dot(x, w1b[...], preferred_element_type=F32)
            b = jnp.dot(x, w3b[...], preferred_element_type=F32)
            y = jnp.dot((_silu(a) * b).astype(BF16), w2b[...], preferred_element_type=F32)
            y_ref[r:r + TE_SUB] = y.astype(BF16)


def _experts(te, nu, xs, w1, w3, w2, *, maxt):
    d = xs.shape[1]
    f = w1.shape[2]
    wsel = lambda j, te, nu: (te[j], 0, 0)
    used = lambda j, te, nu: (jnp.minimum(j, nu[0] - 1), 0)
    return pl.pallas_call(
        _experts_kernel,
        out_shape=jax.ShapeDtypeStruct(xs.shape, BF16),
        grid_spec=pltpu.PrefetchScalarGridSpec(
            num_scalar_prefetch=2,
            grid=(maxt,),
            in_specs=[pl.BlockSpec((TE, d), used),
                      pl.BlockSpec((None, d, f), wsel),
                      pl.BlockSpec((None, d, f), wsel),
                      pl.BlockSpec((None, f, d), wsel)],
            out_specs=pl.BlockSpec((TE, d), used),
            scratch_shapes=[pltpu.VMEM((d, f), BF16), pltpu.VMEM((d, f), BF16), pltpu.VMEM((f, d), BF16)]),
        input_output_aliases={2: 0},
        compiler_params=_cparams("arbitrary"),
    )(te, nu, xs, w1, w3, w2)


def _combine_kernel(cnt_ref, ls_ref, gd_ref, ys_ref, lp_ref, rt_ref, x1_ref, g2p_ref, g2s_ref, fg_ref,
                    yp_o, ys_o, ybuf, sem, *, npt):
    i = pl.program_id(0)
    nt = npt + 1

    def copy(step):
        return lambda k: _run_copy(ys_ref, ybuf.at[step % 2], gd_ref[k], ls_ref[k], cnt_ref[k], sem.at[step % 2])

    def fetch(step):
        ybuf[step % 2] = jnp.zeros(ybuf.shape[1:], BF16)
        _each_run(step, lambda k: copy(step)(k).start())

    @pl.when(i == 0)
    def _():
        fetch(0)

    @pl.when(i + 1 < nt)
    def _():
        fetch(i + 1)

    _each_run(i, lambda k: copy(i)(k).wait())

    yb = ybuf[i % 2]
    slot = lax.broadcasted_iota(I32, (TM, SLOTS), 1).astype(F32)
    lp = lp_ref[...]
    rt = rt_ref[...]

    def unsort(col):
        return jnp.dot((slot == lp[:, col:col + 1]).astype(BF16), yb, preferred_element_type=F32)

    moe = rt[:, 2:3] * unsort(0) + rt[:, 3:4] * unsort(1)
    g2 = jnp.where(i >= npt, g2s_ref[...], g2p_ref[...])
    y = _rms(x1_ref[...] + g2 * moe, fg_ref[...])

    @pl.when(i < npt)
    def _():
        yp_o[...] = y

    @pl.when(i >= npt)
    def _():
        ys_o[...] = y


def _combine(cnt, ls, gd, ys, lp, rt, x1, modp, mods, fg, *, npt, tpb, nb):
    d = x1.shape[1]
    nt = npt + 1
    row = lambda i, *_: (i, 0)
    return pl.pallas_call(
        functools.partial(_combine_kernel, npt=npt),
        out_shape=[jax.ShapeDtypeStruct((npt * TM, d), F32), jax.ShapeDtypeStruct((TM, d), F32)],
        grid_spec=pltpu.PrefetchScalarGridSpec(
            num_scalar_prefetch=3,
            grid=(nt,),
            in_specs=[pl.BlockSpec(memory_space=pl.ANY),
                      pl.BlockSpec((TM, LANES), row), pl.BlockSpec((TM, LANES), row), pl.BlockSpec((TM, d), row),
                      pl.BlockSpec((None, 1, d), lambda i, *_: (jnp.minimum(i // tpb, nb - 1), 0, 5)),
                      pl.BlockSpec((TM, d), lambda i, *_: (0, 5)),
                      pl.BlockSpec((1, d), lambda i, *_: (0, 0))],
            out_specs=[pl.BlockSpec((TM, d), lambda i, *_: (jnp.minimum(i, npt - 1), 0)),
                       pl.BlockSpec((TM, d), lambda i, *_: (0, 0))],
            scratch_shapes=[pltpu.VMEM((2, SLOTS, d), BF16), pltpu.SemaphoreType.DMA((2,))]),
        compiler_params=_cparams("arbitrary"),
    )(cnt, ls, gd, ys, lp, rt, x1, modp, mods, fg)


def _rotation_tables(t):
    pos = jnp.concatenate([jnp.arange(t, dtype=I32), jnp.full((TT,), PAST_LEN, I32)]).astype(F32)
    inv_r = jnp.repeat(1.0 / (ROPE_THETA ** jnp.linspace(0.0, 1.0, RET_DK // 2, dtype=F32)), 2)
    ang_r = pos[:, None] * inv_r[None, :]
    sign_r = jnp.where(jnp.arange(RET_DK) % 2 == 0, -1.0, 1.0).astype(F32)
    inv_w = jnp.tile(ROPE_THETA ** (-jnp.arange(0, SWA_HD, 2, dtype=F32) / SWA_HD), LANES // (SWA_HD // 2))
    ang_w = pos[:, None] * inv_w[None, :]
    sign_w = jnp.where(jnp.arange(LANES) % SWA_HD < SWA_HD // 2, -1.0, 1.0).astype(F32)
    return jnp.cos(ang_r), jnp.sin(ang_r) * sign_r[None, :], jnp.cos(ang_w), jnp.sin(ang_w) * sign_w[None, :]


def kernel(x_prompt, x_sample, c_prompt, c_sample, state_ret, cache_swa_k, cache_swa_v, w_ada, b_ada, norm1_g, norm2_g, w_in, w_up_ret, w_up_swa, w_o, sink, w_rg, b_rg, w_re, b_re, w1, w3, w2, final_g):
    nb, t, d = x_prompt.shape
    ns, dec_seq, _ = x_sample.shape
    depth = w_ada.shape[0]
    assert depth == 1 and dec_seq == 1, "single layer, one new token per sequence"
    assert t % TT == 0 and ns <= TM and ns % 16 == 0 and d % LANES == 0
    assert t % (RET_STEP_CHUNKS * RET_CHUNK) == 0 and t % (SWA_STEP_BLOCKS * WINDOW) == 0
    assert N_GROUPS + N_EXPERTS <= LANES
    w = cache_swa_k.shape[2]
    tpb = t // TM
    npt = nb * tpb
    spb = t // TT
    nps = nb * spb
    nt = (nps + 1) * (TT // TM)
    np_rows = nb * t
    n_tok = np_rows + ns
    maxt = -(-(2 * n_tok + nt * N_EXPERTS * RUN + N_EXPERTS * (TE - 1)) // TE)

    xp = x_prompt.reshape(np_rows, d)
    xs_pad = jnp.pad(x_sample.reshape(ns, d), ((0, TT - ns), (0, 0)))

    c_all = jnp.concatenate([jnp.pad(c_prompt, ((0, SUBLANES - nb % SUBLANES), (0, 0))),
                             jnp.pad(c_sample, ((0, TT - ns), (0, 0)))])
    mod = _modulation(c_all, w_ada[0], b_ada[0])
    modp = mod[:nb].reshape(nb, 1, 6 * d)
    mods = mod[c_all.shape[0] - TT:]

    tabs = _rotation_tables(t)
    rq, rk, rv, rg, sq, sk, sv, siga, sigb = _inproj(
        xp, xs_pad, modp, mods, norm1_g, w_in[0].astype(BF16), tabs, nps=nps, spb=spb, nb=nb)

    dm, qd, kd, cd, gamma = _ret_tables()
    gated_p, st_p = _retention_prompt(rq, rk, rv, rg, (dm, qd, kd, cd), nb=nb, t=t)
    gated_s, st_s = _retention_sample(rq, rk, rv, rg, state_ret[0], gamma, row0=np_rows, ns=ns)
    oswa_p = _swa_prompt(sq, sk, sv, sink[0], nb=nb, t=t)
    to_t = lambda c: jnp.transpose(c[0], (0, 2, 3, 1)).reshape(ns, _SWA_KW, w)
    from_t = lambda c: jnp.transpose(c.reshape(ns, SWA_KV_HEADS, SWA_HD, w), (0, 3, 1, 2))[None]
    oswa_s, ks_new, vs_new = _swa_sample(sq, sk, sv, to_t(cache_swa_k), to_t(cache_swa_v), sink[0],
                                         row0=np_rows, ns=ns)
    gated_s = jnp.pad(gated_s, ((0, TT - ns), (0, 0)))
    oswa_s = jnp.pad(oswa_s, ((0, TT - ns), (0, 0)))

    wr = jnp.pad(jnp.concatenate([w_rg[0], w_re[0]], axis=1), ((0, 0), (0, LANES - N_GROUPS - N_EXPERTS)))
    br = jnp.pad(jnp.concatenate([b_rg[0], b_re[0]]), (0, LANES - N_GROUPS - N_EXPERTS)).reshape(1, LANES)
    x1, h2, rt, lp, cnt, ls, gb = _outproj(
        gated_p, gated_s, oswa_p, oswa_s, siga, sigb, xp, xs_pad, modp, mods, norm2_g,
        w_up_ret[0].astype(BF16), w_up_swa[0].astype(BF16), w_o[0].astype(BF16), wr, br,
        nps=nps, spb=spb, nb=nb, ns=ns)
    cnt = cnt[:, 0, :N_EXPERTS]
    ls = ls[:, 0, :N_EXPERTS]
    gb = gb[:, 0, :N_EXPERTS]
    seg = jnp.sum(cnt, axis=0)
    tiles = (seg + TE - 1) // TE
    tile_end = jnp.cumsum(tiles)
    row_start = (tile_end - tiles) * TE
    gd = (gb + row_start[None, :]).reshape(-1)
    n_used = tile_end[-1:]
    jj = jnp.minimum(jnp.arange(maxt, dtype=I32), n_used[0] - 1)
    te = jnp.minimum(jnp.sum((tile_end[None, :] <= jj[:, None]).astype(I32), axis=1), N_EXPERTS - 1)
    cnt = cnt.reshape(-1)
    ls = ls.reshape(-1)
    n_used = n_used.astype(I32)
    xs = _dispatch(cnt, ls, gd, row_start + seg, tiles * TE - seg, n_used, h2, lp, nt=nt, maxt=maxt)
    ys = _experts(te, n_used, xs, w1[0], w3[0], w2[0], maxt=maxt)
    y_p, y_s = _combine(cnt, ls, gd, ys, lp, rt, x1, modp, mods, final_g.reshape(1, d), npt=npt, tpb=tpb, nb=nb)

    y_prompt = y_p.reshape(nb, t, d)
    y_sample = y_s[:ns].reshape(ns, 1, d)
    wk = min(WINDOW, t)
    last = lambda a: jnp.stack([a[(b + 1) * t - wk:(b + 1) * t] for b in range(nb)]).reshape(
        nb, wk, SWA_KV_HEADS, SWA_HD)
    skp, svp = last(sk), last(sv)
    return (y_prompt, y_sample, st_p[None], st_s[None], skp[None], svp[None], from_t(ks_new), from_t(vs_new))
```

```python
import functools

import jax
import jax.numpy as jnp
from jax import lax
from jax.experimental import pallas as pl
from jax.experimental.pallas import tpu as pltpu

F32 = jnp.float32
BF16 = jnp.bfloat16
I32 = jnp.int32

PAST_LEN = 8192
RET_HEADS = 4
RET_DK = 128
RET_DV = 128
RET_CHUNK = 128
SWA_HEADS = 8
SWA_KV_HEADS = 2
SWA_HD = 64
WINDOW = 128
ROPE_THETA = 10000.0
N_GROUPS = 4
EXPERTS_PER_GROUP = 8
N_EXPERTS = N_GROUPS * EXPERTS_PER_GROUP
D_EXPERT = 256
NORM_EPS = 1e-6

LANES = 128
SUBLANES = 8
TM = 256
TT = 2 * TM
RUN = 16
SLOTS = 2 * TM + N_EXPERTS * RUN
TE = 512
TE_SUB = 256
VMEM_LIMIT = 56 * 1024 * 1024

_RET_W = RET_HEADS * RET_DK
_SWA_QW = SWA_HEADS * SWA_HD
_SWA_KW = SWA_KV_HEADS * SWA_HD


def _cparams(*sem):
    return pltpu.CompilerParams(dimension_semantics=sem, vmem_limit_bytes=VMEM_LIMIT)


def _sigmoid(x):
    return 1.0 / (1.0 + jnp.exp(-x))


def _silu(x):
    return x * _sigmoid(x)


def _bdot(a, b):
    return jnp.dot(a.astype(BF16), b.astype(BF16), preferred_element_type=F32)


def _bdot_nt(a, b):
    return lax.dot_general(a.astype(BF16), b.astype(BF16), (((1,), (1,)), ((), ())), preferred_element_type=F32)


def _mod_kernel(c_ref, w_ref, b_ref, o_ref):
    o_ref[...] = _bdot(_silu(c_ref[...]), w_ref[...]) + b_ref[...]


def _modulation(c_all, w_ada, b_ada):
    rows, d = c_all.shape
    n = w_ada.shape[1]
    return pl.pallas_call(
        _mod_kernel,
        out_shape=jax.ShapeDtypeStruct((rows, n), F32),
        grid=(n // d,),
        in_specs=[pl.BlockSpec((rows, d), lambda j: (0, 0)),
                  pl.BlockSpec((d, d), lambda j: (0, j)),
                  pl.BlockSpec((1, d), lambda j: (0, j))],
        out_specs=pl.BlockSpec((rows, d), lambda j: (0, j)),
        compiler_params=_cparams("arbitrary"),
    )(c_all, w_ada, b_ada.reshape(1, n))


def _rms(x, g):
    return x * lax.rsqrt(jnp.mean(x * x, axis=-1, keepdims=True) + NORM_EPS) * g


def _pair_rotate(z, cos, sin_signed):
    n = z.shape[-1]
    lane = lax.broadcasted_iota(I32, z.shape, 1)
    partner = jnp.where((lane & 1) == 0, pltpu.roll(z, n - 1, 1), pltpu.roll(z, 1, 1))
    reps = n // LANES
    cos = jnp.concatenate([cos] * reps, axis=1) if reps > 1 else cos
    sin_signed = jnp.concatenate([sin_signed] * reps, axis=1) if reps > 1 else sin_signed
    return z * cos + partner * sin_signed


def _half_rotate(z, cos, sin_signed):
    n = z.shape[-1]
    half = SWA_HD // 2
    lane = lax.broadcasted_iota(I32, z.shape, 1)
    partner = jnp.where((lane & (SWA_HD - 1)) < half, pltpu.roll(z, n - half, 1), pltpu.roll(z, half, 1))
    reps = n // LANES
    cos = jnp.concatenate([cos] * reps, axis=1) if reps > 1 else cos
    sin_signed = jnp.concatenate([sin_signed] * reps, axis=1) if reps > 1 else sin_signed
    return z * cos + partner * sin_signed


def _inproj_kernel(xp_ref, xs_ref, shp_ref, scp_ref, shs_ref, scs_ref, n1_ref, w_ref,
                   crp_ref, srp_ref, cwp_ref, swp_ref, crs_ref, srs_ref, cws_ref, sws_ref,
                   rq_o, rk_o, rv_o, rg_o, sq_o, sk_o, sv_o, za_o, zb_o, *, nps):
    is_s = pl.program_id(0) >= nps
    d = xp_ref.shape[-1]
    for r in range(0, TT, TM):
        rows = slice(r, r + TM)
        x = jnp.where(is_s, xs_ref[rows], xp_ref[rows])
        sh = jnp.where(is_s, shs_ref[rows], shp_ref[...])
        sc = jnp.where(is_s, scs_ref[rows], scp_ref[...])
        h = (_rms(x, n1_ref[...]) * (1.0 + sc) + sh).astype(BF16)
        cr, sr, cw, sw = (jnp.where(is_s, s_ref[...], p_ref[rows]) for s_ref, p_ref in
                          ((crs_ref, crp_ref), (srs_ref, srp_ref), (cws_ref, cwp_ref), (sws_ref, swp_ref)))

        def seg(a, b):
            return jnp.dot(h, w_ref[:, a:b], preferred_element_type=F32)

        o = 0
        rq_o[rows] = _pair_rotate(seg(o, o + _RET_W), cr, sr).astype(BF16)
        o += _RET_W
        rk_o[rows] = (_pair_rotate(seg(o, o + _RET_W), cr, sr) * (RET_DK ** -0.5)).astype(BF16)
        o += _RET_W
        rv_o[rows] = seg(o, o + _RET_W).astype(BF16)
        o += _RET_W
        rg_o[rows] = _silu(seg(o, o + _RET_W)).astype(BF16)
        o += _RET_W
        sq_o[rows] = (_half_rotate(seg(o, o + _SWA_QW), cw, sw) * (SWA_HD ** -0.5)).astype(BF16)
        o += _SWA_QW
        zkv = seg(o, o + 2 * _SWA_KW)
        sk_o[rows] = _half_rotate(zkv[:, :_SWA_KW], cw, sw)
        sv_o[rows] = zkv[:, _SWA_KW:]
        o += 2 * _SWA_KW
        za_o[rows] = _sigmoid(seg(o, o + d)).astype(BF16)
        o += d
        zb_o[rows] = _sigmoid(seg(o, o + d)).astype(BF16)


def _inproj(xp, xs_pad, modp, mods, n1, w_in_b, tabs_p, tabs_s, *, nps, spb, nb):
    d = xp.shape[1]
    nrow = (nps + 1) * TT
    n_in = w_in_b.shape[1]
    pstep = lambda i: (jnp.minimum(i, nps - 1), 0)
    pbatch = lambda col: (lambda i: (jnp.minimum(i // spb, nb - 1), 0, col))
    tab_idx = lambda i: (jnp.where(i < nps, i % spb, 0), 0)
    out_cols = [(_RET_W, BF16)] * 4 + [(_SWA_QW, BF16), (_SWA_KW, F32), (_SWA_KW, F32), (d, BF16), (d, BF16)]
    return pl.pallas_call(
        functools.partial(_inproj_kernel, nps=nps),
        out_shape=[jax.ShapeDtypeStruct((nrow, c), t) for c, t in out_cols],
        grid=(nps + 1,),
        in_specs=[pl.BlockSpec((TT, d), pstep),
                  pl.BlockSpec((TT, d), lambda i: (0, 0)),
                  pl.BlockSpec((None, 1, d), pbatch(0)),
                  pl.BlockSpec((None, 1, d), pbatch(1)),
                  pl.BlockSpec((TT, d), lambda i: (0, 0)),
                  pl.BlockSpec((TT, d), lambda i: (0, 1)),
                  pl.BlockSpec((1, d), lambda i: (0, 0)),
                  pl.BlockSpec((d, n_in), lambda i: (0, 0))]
                 + [pl.BlockSpec((TT, LANES), tab_idx)] * 4
                 + [pl.BlockSpec((1, LANES), lambda i: (0, 0))] * 4,
        out_specs=[pl.BlockSpec((TT, c), lambda i: (i, 0)) for c, _ in out_cols],
        compiler_params=_cparams("arbitrary"),
    )(xp, xs_pad, modp, modp, mods, mods, n1, w_in_b, *tabs_p, *tabs_s)


RET_BLOCK = 512


def _ret_kernel(q_ref, k_ref, v_ref, g_ref, dm_ref, qd_ref, kd_ref, cd_ref, o_ref, st_ref, s_scr, *, nsteps):
    step = pl.program_id(1)

    @pl.when(step == 0)
    def _():
        s_scr[...] = jnp.zeros_like(s_scr)

    for h in range(RET_HEADS):
        sl = slice(h * RET_DK, (h + 1) * RET_DK)
        state = s_scr[h]
        q, k, v = q_ref[:, sl], k_ref[:, sl], v_ref[:, sl]
        att = _bdot_nt(q, k) * dm_ref[h]
        o = _bdot(att, v) + _bdot(q.astype(F32) * qd_ref[h], state)
        kd = (k.astype(F32) * kd_ref[h]).astype(BF16)
        kv = lax.dot_general(kd, v, (((0,), (0,)), ((), ())), preferred_element_type=F32)
        s_scr[h] = cd_ref[h] * state + kv
        o = o * lax.rsqrt(jnp.mean(o * o, axis=-1, keepdims=True) + NORM_EPS)
        o_ref[:, sl] = (o * g_ref[:, sl].astype(F32)).astype(BF16)

    @pl.when(step == nsteps - 1)
    def _():
        st_ref[...] = s_scr[...]


def _ret_tables(chunk):
    ld = jnp.log(1.0 - 2.0 ** (-5.0 - jnp.arange(RET_HEADS, dtype=F32)))
    idx = jnp.arange(chunk, dtype=F32)
    diff = idx[:, None] - idx[None, :]
    causal = diff >= 0
    dmask = jnp.where(causal[None], jnp.exp(ld[:, None, None] * jnp.where(causal, diff, 0.0)[None]), 0.0)
    k_dec = jnp.exp(ld[None, :] * (chunk - 1.0 - idx)[:, None])
    q_dec = jnp.exp(ld[None, :] * (idx + 1.0)[:, None])
    chunk_decay = jnp.exp(ld * chunk)
    bc = lambda t: jnp.broadcast_to(t.T[:, :, None], (RET_HEADS, chunk, RET_DV))
    cd = jnp.broadcast_to(chunk_decay[:, None, None], (RET_HEADS, 1, RET_DV))
    return dmask, bc(q_dec), bc(k_dec), cd, jnp.exp(ld)


def _retention_prompt(rq, rk, rv, rg, tabs, *, nb, t):
    rows = RET_BLOCK
    nsteps = t // rows
    dm, qd, kd, cd = tabs
    blk = lambda b, c: (b * nsteps + c, 0)
    full3 = lambda b, c: (0, 0, 0)
    return pl.pallas_call(
        functools.partial(_ret_kernel, nsteps=nsteps),
        out_shape=[jax.ShapeDtypeStruct((nb * t, _RET_W), BF16),
                   jax.ShapeDtypeStruct((nb, RET_HEADS, RET_DK, RET_DV), F32)],
        grid=(nb, nsteps),
        in_specs=[pl.BlockSpec((rows, _RET_W), blk)] * 4
                 + [pl.BlockSpec((RET_HEADS, rows, rows), full3)]
                 + [pl.BlockSpec((RET_HEADS, rows, RET_DV), full3)] * 2
                 + [pl.BlockSpec((RET_HEADS, 1, RET_DV), full3)],
        out_specs=[pl.BlockSpec((rows, _RET_W), blk),
                   pl.BlockSpec((None, RET_HEADS, RET_DK, RET_DV), lambda b, c: (b, 0, 0, 0))],
        scratch_shapes=[pltpu.VMEM((RET_HEADS, RET_DK, RET_DV), F32)],
        compiler_params=_cparams("arbitrary", "arbitrary"),
    )(rq, rk, rv, rg, dm, qd, kd, cd)


def _ret_sample_kernel(gam_ref, q_ref, k_ref, v_ref, g_ref, s0_ref, o_ref, st_ref, *, sb):
    gamma = gam_ref[pl.program_id(1)]
    q = q_ref[...].astype(F32)
    k = k_ref[...].astype(F32)
    v = v_ref[...].astype(F32)
    rows = sb * RET_DK
    s2 = s0_ref[...].reshape(rows, RET_DV)
    col_b = lax.broadcasted_iota(I32, (sb, rows), 1) // RET_DK
    row_b = lax.broadcasted_iota(I32, (sb, rows), 0)
    qexp = jnp.where(col_b == row_b, jnp.concatenate([q * gamma] * sb, axis=1), 0.0)
    o = jnp.sum(q * k, axis=-1, keepdims=True) * v + _bdot(qexp, s2)
    o = o * lax.rsqrt(jnp.mean(o * o, axis=-1, keepdims=True) + NORM_EPS)
    o_ref[...] = (o * g_ref[...].astype(F32)).astype(BF16)
    rep = (lax.broadcasted_iota(I32, (rows, sb), 0) // RET_DK == lax.broadcasted_iota(I32, (rows, sb), 1))
    rep = rep.astype(BF16)
    krep = _bdot(rep, k)
    vrep = _bdot(rep, v)
    eye = (lax.broadcasted_iota(I32, (rows, RET_DK), 0) % RET_DK == lax.broadcasted_iota(I32, (rows, RET_DK), 1))
    kcol = jnp.sum(jnp.where(eye, krep, 0.0), axis=-1, keepdims=True)
    st_ref[...] = (gamma * s2 + kcol * vrep).reshape(sb, RET_DK, RET_DV)


def _retention_sample(rq, rk, rv, rg, s0, gamma, *, row0, ns):
    sb = min(64, ns)
    base = row0 // sb
    blk = lambda i, h: (base + i, h)
    sblk = lambda i, h: (i, h, 0, 0)
    return pl.pallas_call(
        functools.partial(_ret_sample_kernel, sb=sb),
        out_shape=[jax.ShapeDtypeStruct((ns, _RET_W), BF16),
                   jax.ShapeDtypeStruct(s0.shape, F32)],
        grid=(ns // sb, RET_HEADS),
        in_specs=[pl.BlockSpec(memory_space=pltpu.SMEM)]
                 + [pl.BlockSpec((sb, RET_DK), blk)] * 4
                 + [pl.BlockSpec((sb, None, RET_DK, RET_DV), sblk)],
        out_specs=[pl.BlockSpec((sb, RET_DV), lambda i, h: (i, h)),
                   pl.BlockSpec((sb, None, RET_DK, RET_DV), sblk)],
        compiler_params=_cparams("arbitrary", "arbitrary"),
    )(gamma, rq, rk, rv, rg, s0)


def _sink_softmax(s, mask, sink):
    s = jnp.where(mask, s, -jnp.inf)
    m = jnp.maximum(jnp.max(s, axis=-1, keepdims=True), sink)
    p = jnp.exp(s - m)
    return p / (jnp.sum(p, axis=-1, keepdims=True) + jnp.exp(sink - m))


def _split_kv_heads(x):
    lo = lax.broadcasted_iota(I32, x.shape, 1) < SWA_HD
    h0_lo = jnp.where(lo, x, 0.0)
    h1_hi = jnp.where(lo, 0.0, x)
    return ((h0_lo, pltpu.roll(h0_lo, SWA_HD, 1)), (pltpu.roll(h1_hi, SWA_HD, 1), h1_hi))


SWA_STEP_BLOCKS = 4


def _swa_kernel(sink_ref, q_ref, kc_ref, kp_ref, vc_ref, vp_ref, o_ref):
    n = pl.program_id(1)
    c = WINDOW
    kk = jnp.concatenate([kp_ref[...], kc_ref[...]], axis=0)
    vv = jnp.concatenate([vp_ref[...], vc_ref[...]], axis=0)
    ks = [[a.astype(BF16) for a in pair] for pair in _split_kv_heads(kk)]
    vs = [[a.astype(BF16) for a in pair] for pair in _split_kv_heads(vv)]
    qi = lax.broadcasted_iota(I32, (2 * c, 2 * c), 0) % c
    ki = lax.broadcasted_iota(I32, (2 * c, 2 * c), 1)
    band = (ki > qi) & (ki <= qi + c)
    top = lax.broadcasted_iota(I32, (2 * c, 1), 0) < c
    for s in range(SWA_STEP_BLOCKS):
        rows = slice(s * c, (s + 1) * c)
        keys = slice(s * c, (s + 2) * c)
        mask = (band & ((ki >= c) | (n > 0))) if s == 0 else band
        for kvh in range(SWA_KV_HEADS):
            j0, j1 = 2 * kvh, 2 * kvh + 1
            q2 = jnp.concatenate([q_ref[rows, j0 * LANES:(j0 + 1) * LANES],
                                  q_ref[rows, j1 * LANES:(j1 + 1) * LANES]], axis=0)
            kcat = jnp.concatenate([ks[kvh][0][keys], ks[kvh][1][keys]], axis=0)
            vcat = jnp.concatenate([vs[kvh][0][keys], vs[kvh][1][keys]], axis=0)
            sc = lax.dot_general(q2, kcat, (((1,), (1,)), ((), ())), preferred_element_type=F32)
            ps = []
            for half in range(2):
                sink = jnp.where(top, sink_ref[2 * j0 + half], sink_ref[2 * j1 + half])
                ps.append(_sink_softmax(sc[:, half * 2 * c:(half + 1) * 2 * c], mask, sink).astype(BF16))
            o = jnp.dot(jnp.concatenate(ps, axis=1), vcat, preferred_element_type=F32)
            o_ref[rows, j0 * LANES:(j0 + 1) * LANES] = o[:c].astype(BF16)
            o_ref[rows, j1 * LANES:(j1 + 1) * LANES] = o[c:].astype(BF16)


def _swa_prompt(sq, sk, sv, sink, *, nb, t):
    rows = SWA_STEP_BLOCKS * WINDOW
    nsteps = t // rows
    nblk = t // WINDOW
    cur = lambda b, n: (b * nsteps + n, 0)
    prev = lambda b, n: (b * nblk + jnp.maximum(n * SWA_STEP_BLOCKS - 1, 0), 0)
    return pl.pallas_call(
        _swa_kernel,
        out_shape=jax.ShapeDtypeStruct((nb * t, _SWA_QW), BF16),
        grid=(nb, nsteps),
        in_specs=[pl.BlockSpec(memory_space=pltpu.SMEM),
                  pl.BlockSpec((rows, _SWA_QW), cur),
                  pl.BlockSpec((rows, _SWA_KW), cur),
                  pl.BlockSpec((WINDOW, _SWA_KW), prev),
                  pl.BlockSpec((rows, _SWA_KW), cur),
                  pl.BlockSpec((WINDOW, _SWA_KW), prev)],
        out_specs=pl.BlockSpec((rows, _SWA_QW), cur),
        compiler_params=_cparams("arbitrary", "arbitrary"),
    )(sink, sq, sk, sk, sv, sv)


def _swa_sample_kernel(sink_ref, q_ref, kn_ref, vn_ref, kc_ref, vc_ref, o_ref, ko_ref, vo_ref, *, sb, w):
    pad = jnp.zeros((LANES - sb, _SWA_KW), F32)
    knt = jnp.concatenate([kn_ref[...], pad], axis=0).T
    vnt = jnp.concatenate([vn_ref[...], pad], axis=0).T
    kall = jnp.concatenate([kc_ref[b] for b in range(sb)] + [knt], axis=1)
    vall = jnp.concatenate([vc_ref[b] for b in range(sb)] + [vnt], axis=1)
    ncol = sb * w + LANES
    lo = lax.broadcasted_iota(I32, (sb, LANES), 1) < SWA_HD
    group = SWA_HEADS // SWA_KV_HEADS
    pieces = []
    for h in range(SWA_HEADS):
        slab = q_ref[:, (h // 2) * LANES:(h // 2 + 1) * LANES].astype(F32)
        mine = jnp.where(lo, slab, 0.0) if h % 2 == 0 else jnp.where(lo, 0.0, slab)
        pieces.append(mine if (h % 2) == (h // group) else pltpu.roll(mine, SWA_HD, 1))
    qrows = jnp.concatenate(pieces, axis=0)
    nrow = SWA_HEADS * sb
    s = _bdot(qrows, kall)
    rb = lax.broadcasted_iota(I32, (nrow, ncol), 0) % sb
    ci = lax.broadcasted_iota(I32, (nrow, ncol), 1)
    in_cache = (ci < sb * w) & (ci // w == rb) & ((w - ci % w) < WINDOW)
    mask = in_cache | (ci == sb * w + rb)
    sink_col = jnp.concatenate([jnp.full((sb, 1), sink_ref[h], F32) for h in range(SWA_HEADS)], axis=0)
    p = _sink_softmax(s, mask, sink_col)
    o = _bdot_nt(p, vall)
    for j in range(SWA_HEADS // 2):
        acc = jnp.zeros((sb, LANES), F32)
        for half in range(2):
            h = 2 * j + half
            oh = o[h * sb:(h + 1) * sb]
            own = jnp.where(lo, oh, 0.0) if h // group == 0 else jnp.where(lo, 0.0, oh)
            acc = acc + (own if (h // group) == half else pltpu.roll(own, SWA_HD, 1))
        o_ref[:, j * LANES:(j + 1) * LANES] = acc.astype(BF16)
    newest = lax.broadcasted_iota(I32, (_SWA_KW, w), 1) == w - 1
    for b in range(sb):
        ko_ref[b] = jnp.where(newest, knt[:, b:b + 1], pltpu.roll(kc_ref[b], w - 1, 1))
        vo_ref[b] = jnp.where(newest, vnt[:, b:b + 1], pltpu.roll(vc_ref[b], w - 1, 1))


def _swa_sample(sq, sk, sv, cache_kt, cache_vt, sink, *, row0, ns):
    sb = min(16, ns)
    w = cache_kt.shape[2]
    base = row0 // sb
    blk = lambda i: (base + i, 0)
    cblk = lambda i: (i, 0, 0)
    cspec = pl.BlockSpec((sb, _SWA_KW, w), cblk)
    return pl.pallas_call(
        functools.partial(_swa_sample_kernel, sb=sb, w=w),
        out_shape=[jax.ShapeDtypeStruct((ns, _SWA_QW), BF16),
                   jax.ShapeDtypeStruct(cache_kt.shape, F32), jax.ShapeDtypeStruct(cache_vt.shape, F32)],
        grid=(ns // sb,),
        in_specs=[pl.BlockSpec(memory_space=pltpu.SMEM),
                  pl.BlockSpec((sb, _SWA_QW), blk),
                  pl.BlockSpec((sb, _SWA_KW), blk),
                  pl.BlockSpec((sb, _SWA_KW), blk),
                  cspec, cspec],
        out_specs=[pl.BlockSpec((sb, _SWA_QW), lambda i: (i, 0)), cspec, cspec],
        compiler_params=_cparams("arbitrary"),
    )(sink, sq, sk, sv, cache_kt, cache_vt)


def _route(logits):
    lane = lax.broadcasted_iota(I32, logits.shape, 1)
    big = jnp.int32(1 << 20)
    neg = -jnp.inf

    def top(mask):
        v = jnp.max(jnp.where(mask, logits, neg), axis=-1, keepdims=True)
        i = jnp.min(jnp.where(mask & (logits == v), lane, big), axis=-1, keepdims=True)
        return v, i

    gmask = lane < N_GROUPS
    gmax, gsel = top(gmask)
    p_group = 1.0 / jnp.sum(jnp.where(gmask, jnp.exp(logits - gmax), 0.0), axis=-1, keepdims=True)
    first = N_GROUPS + gsel * EXPERTS_PER_GROUP
    emask = (lane >= first) & (lane < first + EXPERTS_PER_GROUP)
    v1, i1 = top(emask)
    v2, i2 = top(emask & (lane != i1))
    t = jnp.exp(v2 - v1)
    w1 = p_group / (1.0 + t)
    return i1 - N_GROUPS, i2 - N_GROUPS, w1, w1 * t


def _plan_tile(e1, e2, valid, carry):
    lane = lax.broadcasted_iota(I32, (TM, LANES), 1)
    oh1 = ((lane == e1) & valid).astype(F32)
    oh2 = ((lane == e2) & valid).astype(F32)
    oh = oh1 + oh2
    tri = (lax.broadcasted_iota(I32, (TM, TM), 0) > lax.broadcasted_iota(I32, (TM, TM), 1)).astype(BF16)
    before = _bdot(tri, oh)
    cnt = jnp.sum(oh, axis=0, keepdims=True)
    units = jnp.maximum(jnp.floor((cnt + (RUN - 1)) * (1.0 / RUN)), 1.0)
    upper = (lax.broadcasted_iota(I32, (LANES, LANES), 0) < lax.broadcasted_iota(I32, (LANES, LANES), 1))
    lstart = RUN * _bdot(jnp.broadcast_to(units, (SUBLANES, LANES)), upper.astype(BF16))[0:1]
    slot = lstart + before
    lp1 = jnp.sum(oh1 * slot, axis=-1, keepdims=True)
    lp2 = jnp.sum(oh2 * slot, axis=-1, keepdims=True)
    vcol = valid[:, 0:1]
    lp = jnp.where(lane == 0, jnp.where(vcol, lp1, -1.0), jnp.where(lane == 1, jnp.where(vcol, lp2, -1.0), 0.0))
    base = carry[...]
    carry[...] = base + RUN * units
    return lp, (RUN * units).astype(I32), lstart.astype(I32), base.astype(I32)


def _outproj_kernel(gtp_ref, gts_ref, osp_ref, oss_ref, sa_ref, sb_ref, xp_ref, xs_ref, g1p_ref, shp_ref, scp_ref,
                    g1s_ref, shs_ref, scs_ref, n2_ref, wur_ref, wus_ref, wo_ref, wrh_ref, wrl_ref, br_ref,
                    x1_o, h2_o, rt_o, lp_o, cnt_o, ls_o, gb_o, carry, *, nps, ns):
    i = pl.program_id(0)
    is_s = i >= nps

    @pl.when(i == 0)
    def _():
        carry[...] = jnp.zeros_like(carry)

    routed = []
    for r in range(0, TT, TM):
        rows = slice(r, r + TM)
        x = jnp.where(is_s, xs_ref[rows], xp_ref[rows])
        g1 = jnp.where(is_s, g1s_ref[rows], g1p_ref[...])
        sh = jnp.where(is_s, shs_ref[rows], shp_ref[...])
        sc = jnp.where(is_s, scs_ref[rows], scp_ref[...])
        gated = jnp.where(is_s, gts_ref[rows], gtp_ref[rows])
        oswa = jnp.where(is_s, oss_ref[rows], osp_ref[rows])
        y_ret = jnp.dot(gated, wur_ref[...], preferred_element_type=F32)
        y_swa = jnp.dot(oswa, wus_ref[...], preferred_element_type=F32)
        merged = sa_ref[rows].astype(F32) * y_ret + sb_ref[rows].astype(F32) * y_swa
        x1 = x + g1 * jnp.dot(merged.astype(BF16), wo_ref[...], preferred_element_type=F32)
        x1_o[rows] = x1
        h2 = _rms(x1, n2_ref[...]) * (1.0 + sc) + sh
        hi = h2.astype(BF16)
        h2_o[rows] = hi
        lo = (h2 - hi.astype(F32)).astype(BF16)
        wrh = wrh_ref[...]
        logits = (jnp.dot(hi, wrh, preferred_element_type=F32) + jnp.dot(lo, wrh, preferred_element_type=F32)
                  + jnp.dot(hi, wrl_ref[...], preferred_element_type=F32) + br_ref[...])
        e1, e2, w1, w2 = _route(logits)
        lane = lax.broadcasted_iota(I32, logits.shape, 1)
        rt_o[rows] = jnp.where(lane == 2, w1, jnp.where(lane == 3, w2, 0.0))
        routed.append((e1, e2))
    for sub, (e1, e2) in enumerate(routed):
        row = lax.broadcasted_iota(I32, (TM, LANES), 0) + sub * TM
        valid = jnp.logical_not(is_s) | (row < ns)
        lp_o[sub * TM:(sub + 1) * TM], cnt_o[sub], ls_o[sub], gb_o[sub] = _plan_tile(e1, e2, valid, carry)


def _outproj(gated_p, gated_s, oswa_p, oswa_s, siga, sigb, xp, xs_pad, modp, mods, n2, wur, wus, wo, wr, br,
             *, nps, spb, nb, ns):
    d = xp.shape[1]
    sub = TT // TM
    nrow = (nps + 1) * TT
    row = lambda i: (i, 0)
    pstep = lambda i: (jnp.minimum(i, nps - 1), 0)
    pbatch = lambda col: (lambda i: (jnp.minimum(i // spb, nb - 1), 0, col))
    scol = lambda col: (lambda i: (0, col))
    const = lambda i: (0, 0)
    wr_hi = wr.astype(BF16)
    wr_lo = (wr - wr_hi.astype(F32)).astype(BF16)
    meta = jax.ShapeDtypeStruct(((nps + 1) * sub, 1, LANES), I32)
    mspec = pl.BlockSpec((sub, 1, LANES), lambda i: (i, 0, 0))
    return pl.pallas_call(
        functools.partial(_outproj_kernel, nps=nps, ns=ns),
        out_shape=[jax.ShapeDtypeStruct((nrow, d), F32),
                   jax.ShapeDtypeStruct((nrow, d), BF16),
                   jax.ShapeDtypeStruct((nrow, LANES), F32),
                   jax.ShapeDtypeStruct((nrow, LANES), F32), meta, meta, meta],
        grid=(nps + 1,),
        in_specs=[pl.BlockSpec((TT, _RET_W), pstep), pl.BlockSpec((TT, _RET_W), const),
                  pl.BlockSpec((TT, _SWA_QW), pstep), pl.BlockSpec((TT, _SWA_QW), const),
                  pl.BlockSpec((TT, d), row), pl.BlockSpec((TT, d), row),
                  pl.BlockSpec((TT, d), pstep), pl.BlockSpec((TT, d), const),
                  pl.BlockSpec((None, 1, d), pbatch(2)), pl.BlockSpec((None, 1, d), pbatch(3)),
                  pl.BlockSpec((None, 1, d), pbatch(4)),
                  pl.BlockSpec((TT, d), scol(2)), pl.BlockSpec((TT, d), scol(3)), pl.BlockSpec((TT, d), scol(4)),
                  pl.BlockSpec((1, d), const),
                  pl.BlockSpec(wur.shape, const), pl.BlockSpec(wus.shape, const), pl.BlockSpec(wo.shape, const),
                  pl.BlockSpec(wr.shape, const), pl.BlockSpec(wr.shape, const), pl.BlockSpec((1, LANES), const)],
        out_specs=[pl.BlockSpec((TT, d), row), pl.BlockSpec((TT, d), row), pl.BlockSpec((TT, LANES), row),
                   pl.BlockSpec((TT, LANES), row), mspec, mspec, mspec],
        scratch_shapes=[pltpu.VMEM((1, LANES), F32)],
        compiler_params=_cparams("arbitrary"),
    )(gated_p, gated_s, oswa_p, oswa_s, siga, sigb, xp, xs_pad, modp, modp, modp, mods, mods, mods, n2,
      wur, wus, wo, wr_hi, wr_lo, br)


def _aligned(v):
    return v if isinstance(v, int) else pl.multiple_of(v, RUN)


def _run_copy(src, dst, s_start, d_start, n, sem):
    s_start, d_start, n = _aligned(s_start), _aligned(d_start), _aligned(n)
    return pltpu.make_async_copy(src.at[pl.ds(s_start, n)], dst.at[pl.ds(d_start, n)], sem)


def _each_run(step, fn):
    for e in range(N_EXPERTS):
        fn(step * N_EXPERTS + e)


def _dispatch_kernel(cnt_ref, ls_ref, gd_ref, ps_ref, pn_ref, nu_ref, h_ref, lp_ref, xs_ref,
                     sorted_scr, zero_scr, sem, zsem, *, nt, maxt):
    i = pl.program_id(0)

    def pad(e):
        return _run_copy(zero_scr, xs_ref, 0, ps_ref[e], pn_ref[e], zsem)

    def tail(j):
        return _run_copy(zero_scr, xs_ref, 0, j * TE, TE, zsem)

    def each_pad(fn):
        def body(e, c):
            @pl.when(pn_ref[e] > 0)
            def _():
                fn(pad(e))
            return c
        lax.fori_loop(0, N_EXPERTS, body, 0)

    def each_tail(fn):
        def body(j, c):
            fn(tail(j))
            return c
        lax.fori_loop(nu_ref[0], maxt, body, 0)

    @pl.when(i == 0)
    def _():
        zero_scr[...] = jnp.zeros_like(zero_scr)
        each_pad(lambda cp: cp.start())
        each_tail(lambda cp: cp.start())

    lpt = lp_ref[...].T
    slot = lax.broadcasted_iota(I32, (SLOTS, TM), 0).astype(F32)
    perm = ((slot == lpt[0:1]) | (slot == lpt[1:2])).astype(BF16)
    sorted_scr[i % 2] = jnp.dot(perm, h_ref[...], preferred_element_type=F32).astype(BF16)

    def copy(step):
        return lambda k: _run_copy(sorted_scr.at[step % 2], xs_ref, ls_ref[k], gd_ref[k], cnt_ref[k], sem.at[step % 2])

    _each_run(i, lambda k: copy(i)(k).start())

    @pl.when(i > 0)
    def _():
        _each_run(i - 1, lambda k: copy(i - 1)(k).wait())

    @pl.when(i == nt - 1)
    def _():
        _each_run(i, lambda k: copy(i)(k).wait())
        each_pad(lambda cp: cp.wait())
        each_tail(lambda cp: cp.wait())


def _dispatch(cnt, ls, gd, ps, pn, nu, h2, lp, *, nt, maxt):
    d = h2.shape[1]
    return pl.pallas_call(
        functools.partial(_dispatch_kernel, nt=nt, maxt=maxt),
        out_shape=jax.ShapeDtypeStruct((maxt * TE, d), BF16),
        grid_spec=pltpu.PrefetchScalarGridSpec(
            num_scalar_prefetch=6,
            grid=(nt,),
            in_specs=[pl.BlockSpec((TM, d), lambda i, *_: (i, 0)),
                      pl.BlockSpec((TM, LANES), lambda i, *_: (i, 0))],
            out_specs=pl.BlockSpec(memory_space=pl.ANY),
            scratch_shapes=[pltpu.VMEM((2, SLOTS, d), BF16), pltpu.VMEM((TE, d), BF16),
                            pltpu.SemaphoreType.DMA((2,)), pltpu.SemaphoreType.DMA(())]),
        compiler_params=_cparams("arbitrary"),
    )(cnt, ls, gd, ps, pn, nu, h2, lp)


def _experts_kernel(te_ref, nu_ref, x_ref, w1_ref, w3_ref, w2_ref, y_ref, w1b, w3b, w2b):
    j = pl.program_id(0)
    changed = (j == 0) | (te_ref[j] != te_ref[jnp.maximum(j - 1, 0)])

    @pl.when(changed)
    def _():
        w1b[...] = w1_ref[...].astype(BF16)
        w3b[...] = w3_ref[...].astype(BF16)
        w2b[...] = w2_ref[...].astype(BF16)

    @pl.when(j < nu_ref[0])
    def _():
        for r in range(0, TE, TE_SUB):
            x = x_ref[r:r + TE_SUB]
            a = jnp.dot(x, w1b[...], preferred_element_type=F32)
            b = jnp.dot(x, w3b[...], preferred_element_type=F32)
            y = jnp.dot((_silu(a) * b).astype(BF16), w2b[...], preferred_element_type=F32)
            y_ref[r:r + TE_SUB] = y.astype(BF16)


def _experts(te, nu, xs, w1, w3, w2, *, maxt):
    d = xs.shape[1]
    f = w1.shape[2]
    wsel = lambda j, te, nu: (te[j], 0, 0)
    used = lambda j, te, nu: (jnp.minimum(j, nu[0] - 1), 0)
    return pl.pallas_call(
        _experts_kernel,
        out_shape=jax.ShapeDtypeStruct(xs.shape, BF16),
        grid_spec=pltpu.PrefetchScalarGridSpec(
            num_scalar_prefetch=2,
            grid=(maxt,),
            in_specs=[pl.BlockSpec((TE, d), used),
                      pl.BlockSpec((None, d, f), wsel),
                      pl.BlockSpec((None, d, f), wsel),
                      pl.BlockSpec((None, f, d), wsel)],
            out_specs=pl.BlockSpec((TE, d), used),
            scratch_shapes=[pltpu.VMEM((d, f), BF16), pltpu.VMEM((d, f), BF16), pltpu.VMEM((f, d), BF16)]),
        input_output_aliases={2: 0},
        compiler_params=_cparams("arbitrary"),
    )(te, nu, xs, w1, w3, w2)


def _combine_kernel(cnt_ref, ls_ref, gd_ref, ys_ref, lp_ref, rt_ref, x1_ref, g2p_ref, g2s_ref, fg_ref,
                    yp_o, ys_o, ybuf, sem, *, npt):
    i = pl.program_id(0)
    nt = npt + 1

    def copy(step):
        return lambda k: _run_copy(ys_ref, ybuf.at[step % 2], gd_ref[k], ls_ref[k], cnt_ref[k], sem.at[step % 2])

    def fetch(step):
        ybuf[step % 2] = jnp.zeros(ybuf.shape[1:], BF16)
        _each_run(step, lambda k: copy(step)(k).start())

    @pl.when(i == 0)
    def _():
        fetch(0)

    @pl.when(i + 1 < nt)
    def _():
        fetch(i + 1)

    _each_run(i, lambda k: copy(i)(k).wait())

    yb = ybuf[i % 2]
    slot = lax.broadcasted_iota(I32, (TM, SLOTS), 1).astype(F32)
    lp = lp_ref[...]
    rt = rt_ref[...]

    def unsort(col):
        return jnp.dot((slot == lp[:, col:col + 1]).astype(BF16), yb, preferred_element_type=F32)

    moe = rt[:, 2:3] * unsort(0) + rt[:, 3:4] * unsort(1)
    g2 = jnp.where(i >= npt, g2s_ref[...], g2p_ref[...])
    y = _rms(x1_ref[...] + g2 * moe, fg_ref[...])

    @pl.when(i < npt)
    def _():
        yp_o[...] = y

    @pl.when(i >= npt)
    def _():
        ys_o[...] = y


def _combine(cnt, ls, gd, ys, lp, rt, x1, modp, mods, fg, *, npt, tpb, nb):
    d = x1.shape[1]
    nt = npt + 1
    row = lambda i, *_: (i, 0)
    return pl.pallas_call(
        functools.partial(_combine_kernel, npt=npt),
        out_shape=[jax.ShapeDtypeStruct((npt * TM, d), F32), jax.ShapeDtypeStruct((TM, d), F32)],
        grid_spec=pltpu.PrefetchScalarGridSpec(
            num_scalar_prefetch=3,
            grid=(nt,),
            in_specs=[pl.BlockSpec(memory_space=pl.ANY),
                      pl.BlockSpec((TM, LANES), row), pl.BlockSpec((TM, LANES), row), pl.BlockSpec((TM, d), row),
                      pl.BlockSpec((None, 1, d), lambda i, *_: (jnp.minimum(i // tpb, nb - 1), 0, 5)),
                      pl.BlockSpec((TM, d), lambda i, *_: (0, 5)),
                      pl.BlockSpec((1, d), lambda i, *_: (0, 0))],
            out_specs=[pl.BlockSpec((TM, d), lambda i, *_: (jnp.minimum(i, npt - 1), 0)),
                       pl.BlockSpec((TM, d), lambda i, *_: (0, 0))],
            scratch_shapes=[pltpu.VMEM((2, SLOTS, d), BF16), pltpu.SemaphoreType.DMA((2,))]),
        compiler_params=_cparams("arbitrary"),
    )(cnt, ls, gd, ys, lp, rt, x1, modp, mods, fg)


TAB_LO = 64


def _rotation_tables(t):
    inv_r = jnp.repeat(1.0 / (ROPE_THETA ** jnp.linspace(0.0, 1.0, RET_DK // 2, dtype=F32)), 2)
    sign_r = jnp.where(jnp.arange(RET_DK) % 2 == 0, -1.0, 1.0).astype(F32)
    inv_w = jnp.tile(ROPE_THETA ** (-jnp.arange(0, SWA_HD, 2, dtype=F32) / SWA_HD), LANES // (SWA_HD // 2))
    sign_w = jnp.where(jnp.arange(LANES) % SWA_HD < SWA_HD // 2, -1.0, 1.0).astype(F32)
    hi = (jnp.arange(t // TAB_LO, dtype=I32) * TAB_LO).astype(F32)[:, None]
    lo = jnp.arange(TAB_LO, dtype=I32).astype(F32)[:, None]
    past = jnp.full((1, 1), PAST_LEN, F32)

    def pair(inv, sign):
        a, b = hi * inv[None, :], lo * inv[None, :]
        ca, sa, cb, sb = jnp.cos(a)[:, None], jnp.sin(a)[:, None], jnp.cos(b)[None], jnp.sin(b)[None]
        cos = (ca * cb - sa * sb).reshape(t, LANES)
        sin = (sa * cb + ca * sb).reshape(t, LANES) * sign[None, :]
        ang = past * inv[None, :]
        return (cos, sin), (jnp.cos(ang), jnp.sin(ang) * sign[None, :])

    (pr, sr_), (pw, sw_) = pair(inv_r, sign_r), pair(inv_w, sign_w)
    return pr + pw, sr_ + sw_


def kernel(x_prompt, x_sample, c_prompt, c_sample, state_ret, cache_swa_k, cache_swa_v, w_ada, b_ada, norm1_g, norm2_g, w_in, w_up_ret, w_up_swa, w_o, sink, w_rg, b_rg, w_re, b_re, w1, w3, w2, final_g):
    nb, t, d = x_prompt.shape
    ns, dec_seq, _ = x_sample.shape
    depth = w_ada.shape[0]
    assert depth == 1 and dec_seq == 1, "single layer, one new token per sequence"
    assert t % TT == 0 and ns <= TM and ns % 16 == 0 and d % LANES == 0
    assert t % RET_BLOCK == 0 and t % (SWA_STEP_BLOCKS * WINDOW) == 0 and t % TAB_LO == 0
    assert N_GROUPS + N_EXPERTS <= LANES
    w = cache_swa_k.shape[2]
    tpb = t // TM
    npt = nb * tpb
    spb = t // TT
    nps = nb * spb
    nt = (nps + 1) * (TT // TM)
    np_rows = nb * t
    n_tok = np_rows + ns
    maxt = -(-(2 * n_tok + nt * N_EXPERTS * RUN + N_EXPERTS * (TE - 1)) // TE)

    xp = x_prompt.reshape(np_rows, d)
    xs_pad = jnp.pad(x_sample.reshape(ns, d), ((0, TT - ns), (0, 0)))

    c_all = jnp.concatenate([jnp.pad(c_prompt, ((0, SUBLANES - nb % SUBLANES), (0, 0))),
                             jnp.pad(c_sample, ((0, TT - ns), (0, 0)))])
    mod = _modulation(c_all, w_ada[0], b_ada[0])
    modp = mod[:nb].reshape(nb, 1, 6 * d)
    mods = mod[c_all.shape[0] - TT:]

    tabs_p, tabs_s = _rotation_tables(t)
    rq, rk, rv, rg, sq, sk, sv, siga, sigb = _inproj(
        xp, xs_pad, modp, mods, norm1_g, w_in[0].astype(BF16), tabs_p, tabs_s, nps=nps, spb=spb, nb=nb)

    dm, qd, kd, cd, gamma = _ret_tables(RET_BLOCK)
    gated_p, st_p = _retention_prompt(rq, rk, rv, rg, (dm, qd, kd, cd), nb=nb, t=t)
    gated_s, st_s = _retention_sample(rq, rk, rv, rg, state_ret[0], gamma, row0=np_rows, ns=ns)
    oswa_p = _swa_prompt(sq, sk, sv, sink[0], nb=nb, t=t)
    to_t = lambda c: jnp.transpose(c[0], (0, 2, 3, 1)).reshape(ns, _SWA_KW, w)
    from_t = lambda c: jnp.transpose(c.reshape(ns, SWA_KV_HEADS, SWA_HD, w), (0, 3, 1, 2))[None]
    oswa_s, ks_new, vs_new = _swa_sample(sq, sk, sv, to_t(cache_swa_k), to_t(cache_swa_v), sink[0],
                                         row0=np_rows, ns=ns)
    gated_s = jnp.pad(gated_s, ((0, TT - ns), (0, 0)))
    oswa_s = jnp.pad(oswa_s, ((0, TT - ns), (0, 0)))

    wr = jnp.pad(jnp.concatenate([w_rg[0], w_re[0]], axis=1), ((0, 0), (0, LANES - N_GROUPS - N_EXPERTS)))
    br = jnp.pad(jnp.concatenate([b_rg[0], b_re[0]]), (0, LANES - N_GROUPS - N_EXPERTS)).reshape(1, LANES)
    x1, h2, rt, lp, cnt, ls, gb = _outproj(
        gated_p, gated_s, oswa_p, oswa_s, siga, sigb, xp, xs_pad, modp, mods, norm2_g,
        w_up_ret[0].astype(BF16), w_up_swa[0].astype(BF16), w_o[0].astype(BF16), wr, br,
        nps=nps, spb=spb, nb=nb, ns=ns)
    cnt = cnt[:, 0, :N_EXPERTS]
    ls = ls[:, 0, :N_EXPERTS]
    gb = gb[:, 0, :N_EXPERTS]
    seg = jnp.sum(cnt, axis=0)
    tiles = (seg + TE - 1) // TE
    tile_end = jnp.cumsum(tiles)
    row_start = (tile_end - tiles) * TE
    gd = (gb + row_start[None, :]).reshape(-1)
    n_used = tile_end[-1:]
    jj = jnp.minimum(jnp.arange(maxt, dtype=I32), n_used[0] - 1)
    te = jnp.minimum(jnp.sum((tile_end[None, :] <= jj[:, None]).astype(I32), axis=1), N_EXPERTS - 1)
    cnt = cnt.reshape(-1)
    ls = ls.reshape(-1)
    n_used = n_used.astype(I32)
    xs = _dispatch(cnt, ls, gd, row_start + seg, tiles * TE - seg, n_used, h2, lp, nt=nt, maxt=maxt)
    ys = _experts(te, n_used, xs, w1[0], w3[0], w2[0], maxt=maxt)
    y_p, y_s = _combine(cnt, ls, gd, ys, lp, rt, x1, modp, mods, final_g.reshape(1, d), npt=npt, tpb=tpb, nb=nb)

    y_prompt = y_p.reshape(nb, t, d)
    y_sample = y_s[:ns].reshape(ns, 1, d)
    wk = min(WINDOW, t)
    last = lambda a: jnp.stack([a[(b + 1) * t - wk:(b + 1) * t] for b in range(nb)]).reshape(
        nb, wk, SWA_KV_HEADS, SWA_HD)
    skp, svp = last(sk), last(sv)
    return (y_prompt, y_sample, st_p[None], st_s[None], skp[None], svp[None], from_t(ks_new), from_t(vs_new))
```

```python
import functools

import jax
import jax.numpy as jnp
from jax import lax
from jax.experimental import pallas as pl
from jax.experimental.pallas import tpu as pltpu

F32 = jnp.float32
BF16 = jnp.bfloat16
I32 = jnp.int32

PAST_LEN = 8192
RET_HEADS = 4
RET_DK = 128
RET_DV = 128
RET_CHUNK = 128
SWA_HEADS = 8
SWA_KV_HEADS = 2
SWA_HD = 64
WINDOW = 128
ROPE_THETA = 10000.0
N_GROUPS = 4
EXPERTS_PER_GROUP = 8
N_EXPERTS = N_GROUPS * EXPERTS_PER_GROUP
D_EXPERT = 256
NORM_EPS = 1e-6

LANES = 128
SUBLANES = 8
TM = 256
TT = 2 * TM
RUN = 16
SLOTS = 2 * TM + N_EXPERTS * RUN
TE = 512
TE_SUB = 256
VMEM_LIMIT = 56 * 1024 * 1024

_RET_W = RET_HEADS * RET_DK
_SWA_QW = SWA_HEADS * SWA_HD
_SWA_KW = SWA_KV_HEADS * SWA_HD


def _cparams(*sem):
    return pltpu.CompilerParams(dimension_semantics=sem, vmem_limit_bytes=VMEM_LIMIT)


def _sigmoid(x):
    return 1.0 / (1.0 + jnp.exp(-x))


def _silu(x):
    return x * _sigmoid(x)


def _bdot(a, b):
    return jnp.dot(a.astype(BF16), b.astype(BF16), preferred_element_type=F32)


def _bdot_nt(a, b):
    return lax.dot_general(a.astype(BF16), b.astype(BF16), (((1,), (1,)), ((), ())), preferred_element_type=F32)


def _mod_kernel(c_ref, w_ref, b_ref, o_ref):
    o_ref[...] = _bdot(_silu(c_ref[...]), w_ref[...]) + b_ref[...]


def _modulation(c_all, w_ada, b_ada):
    rows, d = c_all.shape
    n = w_ada.shape[1]
    return pl.pallas_call(
        _mod_kernel,
        out_shape=jax.ShapeDtypeStruct((rows, n), F32),
        grid=(n // d,),
        in_specs=[pl.BlockSpec((rows, d), lambda j: (0, 0)),
                  pl.BlockSpec((d, d), lambda j: (0, j)),
                  pl.BlockSpec((1, d), lambda j: (0, j))],
        out_specs=pl.BlockSpec((rows, d), lambda j: (0, j)),
        compiler_params=_cparams("arbitrary"),
    )(c_all, w_ada, b_ada.reshape(1, n))


def _rms(x, g):
    return x * lax.rsqrt(jnp.mean(x * x, axis=-1, keepdims=True) + NORM_EPS) * g


def _pair_rotate(z, cos, sin_signed):
    n = z.shape[-1]
    lane = lax.broadcasted_iota(I32, z.shape, 1)
    partner = jnp.where((lane & 1) == 0, pltpu.roll(z, n - 1, 1), pltpu.roll(z, 1, 1))
    reps = n // LANES
    cos = jnp.concatenate([cos] * reps, axis=1) if reps > 1 else cos
    sin_signed = jnp.concatenate([sin_signed] * reps, axis=1) if reps > 1 else sin_signed
    return z * cos + partner * sin_signed


def _half_rotate(z, cos, sin_signed):
    n = z.shape[-1]
    half = SWA_HD // 2
    lane = lax.broadcasted_iota(I32, z.shape, 1)
    partner = jnp.where((lane & (SWA_HD - 1)) < half, pltpu.roll(z, n - half, 1), pltpu.roll(z, half, 1))
    reps = n // LANES
    cos = jnp.concatenate([cos] * reps, axis=1) if reps > 1 else cos
    sin_signed = jnp.concatenate([sin_signed] * reps, axis=1) if reps > 1 else sin_signed
    return z * cos + partner * sin_signed


def _inproj_kernel(xp_ref, xs_ref, shp_ref, scp_ref, shs_ref, scs_ref, n1_ref, w_ref,
                   crp_ref, srp_ref, cwp_ref, swp_ref, crs_ref, srs_ref, cws_ref, sws_ref,
                   rq_o, rk_o, rv_o, rg_o, sq_o, sk_o, sv_o, za_o, zb_o, *, nps):
    is_s = pl.program_id(0) >= nps
    d = xp_ref.shape[-1]
    for r in range(0, TT, TM):
        rows = slice(r, r + TM)
        x = jnp.where(is_s, xs_ref[rows], xp_ref[rows])
        sh = jnp.where(is_s, shs_ref[rows], shp_ref[...])
        sc = jnp.where(is_s, scs_ref[rows], scp_ref[...])
        h = (_rms(x, n1_ref[...]) * (1.0 + sc) + sh).astype(BF16)
        cr, sr, cw, sw = (jnp.where(is_s, s_ref[...], p_ref[rows]) for s_ref, p_ref in
                          ((crs_ref, crp_ref), (srs_ref, srp_ref), (cws_ref, cwp_ref), (sws_ref, swp_ref)))

        def seg(a, b):
            return jnp.dot(h, w_ref[:, a:b], preferred_element_type=F32)

        o = 0
        rq_o[rows] = _pair_rotate(seg(o, o + _RET_W), cr, sr).astype(BF16)
        o += _RET_W
        rk_o[rows] = (_pair_rotate(seg(o, o + _RET_W), cr, sr) * (RET_DK ** -0.5)).astype(BF16)
        o += _RET_W
        rv_o[rows] = seg(o, o + _RET_W).astype(BF16)
        o += _RET_W
        rg_o[rows] = _silu(seg(o, o + _RET_W)).astype(BF16)
        o += _RET_W
        sq_o[rows] = (_half_rotate(seg(o, o + _SWA_QW), cw, sw) * (SWA_HD ** -0.5)).astype(BF16)
        o += _SWA_QW
        zkv = seg(o, o + 2 * _SWA_KW)
        sk_o[rows] = _half_rotate(zkv[:, :_SWA_KW], cw, sw)
        sv_o[rows] = zkv[:, _SWA_KW:]
        o += 2 * _SWA_KW
        za_o[rows] = _sigmoid(seg(o, o + d)).astype(BF16)
        o += d
        zb_o[rows] = _sigmoid(seg(o, o + d)).astype(BF16)


def _inproj(xp, xs_pad, modp, mods, n1, w_in_b, tabs_p, tabs_s, *, nps, spb, nb):
    d = xp.shape[1]
    nrow = (nps + 1) * TT
    n_in = w_in_b.shape[1]
    pstep = lambda i: (jnp.minimum(i, nps - 1), 0)
    pbatch = lambda col: (lambda i: (jnp.minimum(i // spb, nb - 1), 0, col))
    tab_idx = lambda i: (jnp.where(i < nps, i % spb, 0), 0)
    out_cols = [(_RET_W, BF16)] * 4 + [(_SWA_QW, BF16), (_SWA_KW, F32), (_SWA_KW, F32), (d, BF16), (d, BF16)]
    return pl.pallas_call(
        functools.partial(_inproj_kernel, nps=nps),
        out_shape=[jax.ShapeDtypeStruct((nrow, c), t) for c, t in out_cols],
        grid=(nps + 1,),
        in_specs=[pl.BlockSpec((TT, d), pstep),
                  pl.BlockSpec((TT, d), lambda i: (0, 0)),
                  pl.BlockSpec((None, 1, d), pbatch(0)),
                  pl.BlockSpec((None, 1, d), pbatch(1)),
                  pl.BlockSpec((TT, d), lambda i: (0, 0)),
                  pl.BlockSpec((TT, d), lambda i: (0, 1)),
                  pl.BlockSpec((1, d), lambda i: (0, 0)),
                  pl.BlockSpec((d, n_in), lambda i: (0, 0))]
                 + [pl.BlockSpec((TT, LANES), tab_idx)] * 4
                 + [pl.BlockSpec((1, LANES), lambda i: (0, 0))] * 4,
        out_specs=[pl.BlockSpec((TT, c), lambda i: (i, 0)) for c, _ in out_cols],
        compiler_params=_cparams("arbitrary"),
    )(xp, xs_pad, modp, modp, mods, mods, n1, w_in_b, *tabs_p, *tabs_s)


RET_BLOCK = 512


def _ret_kernel(q_ref, k_ref, v_ref, g_ref, dm_ref, qd_ref, kd_ref, cd_ref, o_ref, st_ref, s_scr, *, nsteps):
    step = pl.program_id(1)

    @pl.when(step == 0)
    def _():
        s_scr[...] = jnp.zeros_like(s_scr)

    for h in range(RET_HEADS):
        sl = slice(h * RET_DK, (h + 1) * RET_DK)
        state = s_scr[h]
        q, k, v = q_ref[:, sl], k_ref[:, sl], v_ref[:, sl]
        att = _bdot_nt(q, k) * dm_ref[h]
        o = _bdot(att, v) + _bdot(q.astype(F32) * qd_ref[h], state)
        kd = (k.astype(F32) * kd_ref[h]).astype(BF16)
        kv = lax.dot_general(kd, v, (((0,), (0,)), ((), ())), preferred_element_type=F32)
        s_scr[h] = cd_ref[h] * state + kv
        o = o * lax.rsqrt(jnp.mean(o * o, axis=-1, keepdims=True) + NORM_EPS)
        o_ref[:, sl] = (o * g_ref[:, sl].astype(F32)).astype(BF16)

    @pl.when(step == nsteps - 1)
    def _():
        st_ref[...] = s_scr[...]


def _ret_tables(chunk):
    ld = jnp.log(1.0 - 2.0 ** (-5.0 - jnp.arange(RET_HEADS, dtype=F32)))
    idx = jnp.arange(chunk, dtype=F32)
    diff = idx[:, None] - idx[None, :]
    causal = diff >= 0
    dmask = jnp.where(causal[None], jnp.exp(ld[:, None, None] * jnp.where(causal, diff, 0.0)[None]), 0.0)
    k_dec = jnp.exp(ld[None, :] * (chunk - 1.0 - idx)[:, None])
    q_dec = jnp.exp(ld[None, :] * (idx + 1.0)[:, None])
    chunk_decay = jnp.exp(ld * chunk)
    bc = lambda t: jnp.broadcast_to(t.T[:, :, None], (RET_HEADS, chunk, RET_DV))
    cd = jnp.broadcast_to(chunk_decay[:, None, None], (RET_HEADS, 1, RET_DV))
    return dmask, bc(q_dec), bc(k_dec), cd, jnp.exp(ld)


def _retention_prompt(rq, rk, rv, rg, tabs, *, nb, t):
    rows = RET_BLOCK
    nsteps = t // rows
    dm, qd, kd, cd = tabs
    blk = lambda b, c: (b * nsteps + c, 0)
    full3 = lambda b, c: (0, 0, 0)
    return pl.pallas_call(
        functools.partial(_ret_kernel, nsteps=nsteps),
        out_shape=[jax.ShapeDtypeStruct((nb * t, _RET_W), BF16),
                   jax.ShapeDtypeStruct((nb, RET_HEADS, RET_DK, RET_DV), F32)],
        grid=(nb, nsteps),
        in_specs=[pl.BlockSpec((rows, _RET_W), blk)] * 4
                 + [pl.BlockSpec((RET_HEADS, rows, rows), full3)]
                 + [pl.BlockSpec((RET_HEADS, rows, RET_DV), full3)] * 2
                 + [pl.BlockSpec((RET_HEADS, 1, RET_DV), full3)],
        out_specs=[pl.BlockSpec((rows, _RET_W), blk),
                   pl.BlockSpec((None, RET_HEADS, RET_DK, RET_DV), lambda b, c: (b, 0, 0, 0))],
        scratch_shapes=[pltpu.VMEM((RET_HEADS, RET_DK, RET_DV), F32)],
        compiler_params=_cparams("arbitrary", "arbitrary"),
    )(rq, rk, rv, rg, dm, qd, kd, cd)


def _ret_sample_kernel(gam_ref, q_ref, k_ref, v_ref, g_ref, s0_ref, o_ref, st_ref, *, sb):
    gamma = gam_ref[pl.program_id(1)]
    q = q_ref[...].astype(F32)
    k = k_ref[...].astype(F32)
    v = v_ref[...].astype(F32)
    rows = sb * RET_DK
    s2 = s0_ref[...].reshape(rows, RET_DV)
    col_b = lax.broadcasted_iota(I32, (sb, rows), 1) // RET_DK
    row_b = lax.broadcasted_iota(I32, (sb, rows), 0)
    qexp = jnp.where(col_b == row_b, jnp.concatenate([q * gamma] * sb, axis=1), 0.0)
    o = jnp.sum(q * k, axis=-1, keepdims=True) * v + _bdot(qexp, s2)
    o = o * lax.rsqrt(jnp.mean(o * o, axis=-1, keepdims=True) + NORM_EPS)
    o_ref[...] = (o * g_ref[...].astype(F32)).astype(BF16)
    rep = (lax.broadcasted_iota(I32, (rows, sb), 0) // RET_DK == lax.broadcasted_iota(I32, (rows, sb), 1))
    rep = rep.astype(BF16)
    krep = _bdot(rep, k)
    vrep = _bdot(rep, v)
    eye = (lax.broadcasted_iota(I32, (rows, RET_DK), 0) % RET_DK == lax.broadcasted_iota(I32, (rows, RET_DK), 1))
    kcol = jnp.sum(jnp.where(eye, krep, 0.0), axis=-1, keepdims=True)
    st_ref[...] = (gamma * s2 + kcol * vrep).reshape(sb, RET_DK, RET_DV)


def _retention_sample(rq, rk, rv, rg, s0, gamma, *, row0, ns):
    sb = min(64, ns)
    base = row0 // sb
    blk = lambda i, h: (base + i, h)
    sblk = lambda i, h: (i, h, 0, 0)
    return pl.pallas_call(
        functools.partial(_ret_sample_kernel, sb=sb),
        out_shape=[jax.ShapeDtypeStruct((ns, _RET_W), BF16),
                   jax.ShapeDtypeStruct(s0.shape, F32)],
        grid=(ns // sb, RET_HEADS),
        in_specs=[pl.BlockSpec(memory_space=pltpu.SMEM)]
                 + [pl.BlockSpec((sb, RET_DK), blk)] * 4
                 + [pl.BlockSpec((sb, None, RET_DK, RET_DV), sblk)],
        out_specs=[pl.BlockSpec((sb, RET_DV), lambda i, h: (i, h)),
                   pl.BlockSpec((sb, None, RET_DK, RET_DV), sblk)],
        compiler_params=_cparams("arbitrary", "arbitrary"),
    )(gamma, rq, rk, rv, rg, s0)


def _sink_softmax(s, mask, sink):
    s = jnp.where(mask, s, -jnp.inf)
    m = jnp.maximum(jnp.max(s, axis=-1, keepdims=True), sink)
    p = jnp.exp(s - m)
    return p / (jnp.sum(p, axis=-1, keepdims=True) + jnp.exp(sink - m))


def _split_kv_heads(x):
    lo = lax.broadcasted_iota(I32, x.shape, 1) < SWA_HD
    h0_lo = jnp.where(lo, x, 0.0)
    h1_hi = jnp.where(lo, 0.0, x)
    return ((h0_lo, pltpu.roll(h0_lo, SWA_HD, 1)), (pltpu.roll(h1_hi, SWA_HD, 1), h1_hi))


SWA_STEP_BLOCKS = 4


def _swa_kernel(sink_ref, q_ref, kc_ref, kp_ref, vc_ref, vp_ref, o_ref):
    n = pl.program_id(1)
    c = WINDOW
    kk = jnp.concatenate([kp_ref[...], kc_ref[...]], axis=0)
    vv = jnp.concatenate([vp_ref[...], vc_ref[...]], axis=0)
    ks = [[a.astype(BF16) for a in pair] for pair in _split_kv_heads(kk)]
    vs = [[a.astype(BF16) for a in pair] for pair in _split_kv_heads(vv)]
    qi = lax.broadcasted_iota(I32, (2 * c, 2 * c), 0) % c
    ki = lax.broadcasted_iota(I32, (2 * c, 2 * c), 1)
    band = (ki > qi) & (ki <= qi + c)
    top = lax.broadcasted_iota(I32, (2 * c, 1), 0) < c
    for s in range(SWA_STEP_BLOCKS):
        rows = slice(s * c, (s + 1) * c)
        keys = slice(s * c, (s + 2) * c)
        mask = (band & ((ki >= c) | (n > 0))) if s == 0 else band
        for kvh in range(SWA_KV_HEADS):
            j0, j1 = 2 * kvh, 2 * kvh + 1
            q2 = jnp.concatenate([q_ref[rows, j0 * LANES:(j0 + 1) * LANES],
                                  q_ref[rows, j1 * LANES:(j1 + 1) * LANES]], axis=0)
            kcat = jnp.concatenate([ks[kvh][0][keys], ks[kvh][1][keys]], axis=0)
            vcat = jnp.concatenate([vs[kvh][0][keys], vs[kvh][1][keys]], axis=0)
            sc = lax.dot_general(q2, kcat, (((1,), (1,)), ((), ())), preferred_element_type=F32)
            ps = []
            for half in range(2):
                sink = jnp.where(top, sink_ref[2 * j0 + half], sink_ref[2 * j1 + half])
                ps.append(_sink_softmax(sc[:, half * 2 * c:(half + 1) * 2 * c], mask, sink).astype(BF16))
            o = jnp.dot(jnp.concatenate(ps, axis=1), vcat, preferred_element_type=F32)
            o_ref[rows, j0 * LANES:(j0 + 1) * LANES] = o[:c].astype(BF16)
            o_ref[rows, j1 * LANES:(j1 + 1) * LANES] = o[c:].astype(BF16)


def _swa_prompt(sq, sk, sv, sink, *, nb, t):
    rows = SWA_STEP_BLOCKS * WINDOW
    nsteps = t // rows
    nblk = t // WINDOW
    cur = lambda b, n: (b * nsteps + n, 0)
    prev = lambda b, n: (b * nblk + jnp.maximum(n * SWA_STEP_BLOCKS - 1, 0), 0)
    return pl.pallas_call(
        _swa_kernel,
        out_shape=jax.ShapeDtypeStruct((nb * t, _SWA_QW), BF16),
        grid=(nb, nsteps),
        in_specs=[pl.BlockSpec(memory_space=pltpu.SMEM),
                  pl.BlockSpec((rows, _SWA_QW), cur),
                  pl.BlockSpec((rows, _SWA_KW), cur),
                  pl.BlockSpec((WINDOW, _SWA_KW), prev),
                  pl.BlockSpec((rows, _SWA_KW), cur),
                  pl.BlockSpec((WINDOW, _SWA_KW), prev)],
        out_specs=pl.BlockSpec((rows, _SWA_QW), cur),
        compiler_params=_cparams("arbitrary", "arbitrary"),
    )(sink, sq, sk, sk, sv, sv)


def _swa_sample_kernel(sink_ref, q_ref, kn_ref, vn_ref, kc_ref, vc_ref, o_ref, ko_ref, vo_ref, *, sb, w):
    pad = jnp.zeros((LANES - sb, _SWA_KW), F32)
    knt = jnp.concatenate([kn_ref[...], pad], axis=0).T
    vnt = jnp.concatenate([vn_ref[...], pad], axis=0).T
    kall = jnp.concatenate([kc_ref[b] for b in range(sb)] + [knt], axis=1)
    vall = jnp.concatenate([vc_ref[b] for b in range(sb)] + [vnt], axis=1)
    ncol = sb * w + LANES
    lo = lax.broadcasted_iota(I32, (sb, LANES), 1) < SWA_HD
    group = SWA_HEADS // SWA_KV_HEADS
    pieces = []
    for h in range(SWA_HEADS):
        slab = q_ref[:, (h // 2) * LANES:(h // 2 + 1) * LANES].astype(F32)
        mine = jnp.where(lo, slab, 0.0) if h % 2 == 0 else jnp.where(lo, 0.0, slab)
        pieces.append(mine if (h % 2) == (h // group) else pltpu.roll(mine, SWA_HD, 1))
    qrows = jnp.concatenate(pieces, axis=0)
    nrow = SWA_HEADS * sb
    s = _bdot(qrows, kall)
    rb = lax.broadcasted_iota(I32, (nrow, ncol), 0) % sb
    ci = lax.broadcasted_iota(I32, (nrow, ncol), 1)
    in_cache = (ci < sb * w) & (ci // w == rb) & ((w - ci % w) < WINDOW)
    mask = in_cache | (ci == sb * w + rb)
    sink_col = jnp.concatenate([jnp.full((sb, 1), sink_ref[h], F32) for h in range(SWA_HEADS)], axis=0)
    p = _sink_softmax(s, mask, sink_col)
    o = _bdot_nt(p, vall)
    for j in range(SWA_HEADS // 2):
        acc = jnp.zeros((sb, LANES), F32)
        for half in range(2):
            h = 2 * j + half
            oh = o[h * sb:(h + 1) * sb]
            own = jnp.where(lo, oh, 0.0) if h // group == 0 else jnp.where(lo, 0.0, oh)
            acc = acc + (own if (h // group) == half else pltpu.roll(own, SWA_HD, 1))
        o_ref[:, j * LANES:(j + 1) * LANES] = acc.astype(BF16)
    newest = lax.broadcasted_iota(I32, (_SWA_KW, w), 1) == w - 1
    for b in range(sb):
        ko_ref[b] = jnp.where(newest, knt[:, b:b + 1], pltpu.roll(kc_ref[b], w - 1, 1))
        vo_ref[b] = jnp.where(newest, vnt[:, b:b + 1], pltpu.roll(vc_ref[b], w - 1, 1))


def _swa_sample(sq, sk, sv, cache_kt, cache_vt, sink, *, row0, ns):
    sb = min(16, ns)
    w = cache_kt.shape[2]
    base = row0 // sb
    blk = lambda i: (base + i, 0)
    cblk = lambda i: (i, 0, 0)
    cspec = pl.BlockSpec((sb, _SWA_KW, w), cblk)
    return pl.pallas_call(
        functools.partial(_swa_sample_kernel, sb=sb, w=w),
        out_shape=[jax.ShapeDtypeStruct((ns, _SWA_QW), BF16),
                   jax.ShapeDtypeStruct(cache_kt.shape, F32), jax.ShapeDtypeStruct(cache_vt.shape, F32)],
        grid=(ns // sb,),
        in_specs=[pl.BlockSpec(memory_space=pltpu.SMEM),
                  pl.BlockSpec((sb, _SWA_QW), blk),
                  pl.BlockSpec((sb, _SWA_KW), blk),
                  pl.BlockSpec((sb, _SWA_KW), blk),
                  cspec, cspec],
        out_specs=[pl.BlockSpec((sb, _SWA_QW), lambda i: (i, 0)), cspec, cspec],
        compiler_params=_cparams("arbitrary"),
    )(sink, sq, sk, sv, cache_kt, cache_vt)


def _route(logits):
    lane = lax.broadcasted_iota(I32, logits.shape, 1)
    big = jnp.int32(1 << 20)
    neg = -jnp.inf

    def top(mask):
        v = jnp.max(jnp.where(mask, logits, neg), axis=-1, keepdims=True)
        i = jnp.min(jnp.where(mask & (logits == v), lane, big), axis=-1, keepdims=True)
        return v, i

    gmask = lane < N_GROUPS
    gmax, gsel = top(gmask)
    p_group = 1.0 / jnp.sum(jnp.where(gmask, jnp.exp(logits - gmax), 0.0), axis=-1, keepdims=True)
    first = N_GROUPS + gsel * EXPERTS_PER_GROUP
    emask = (lane >= first) & (lane < first + EXPERTS_PER_GROUP)
    v1, i1 = top(emask)
    v2, i2 = top(emask & (lane != i1))
    t = jnp.exp(v2 - v1)
    w1 = p_group / (1.0 + t)
    return i1 - N_GROUPS, i2 - N_GROUPS, w1, w1 * t


def _plan_tile(e1, e2, valid, carry):
    lane = lax.broadcasted_iota(I32, (TM, LANES), 1)
    oh1 = ((lane == e1) & valid).astype(F32)
    oh2 = ((lane == e2) & valid).astype(F32)
    oh = oh1 + oh2
    tri = (lax.broadcasted_iota(I32, (TM, TM), 0) > lax.broadcasted_iota(I32, (TM, TM), 1)).astype(BF16)
    before = _bdot(tri, oh)
    cnt = jnp.sum(oh, axis=0, keepdims=True)
    units = jnp.maximum(jnp.floor((cnt + (RUN - 1)) * (1.0 / RUN)), 1.0)
    upper = (lax.broadcasted_iota(I32, (LANES, LANES), 0) < lax.broadcasted_iota(I32, (LANES, LANES), 1))
    lstart = RUN * _bdot(jnp.broadcast_to(units, (SUBLANES, LANES)), upper.astype(BF16))[0:1]
    slot = lstart + before
    lp1 = jnp.sum(oh1 * slot, axis=-1, keepdims=True)
    lp2 = jnp.sum(oh2 * slot, axis=-1, keepdims=True)
    vcol = valid[:, 0:1]
    lp = jnp.where(lane == 0, jnp.where(vcol, lp1, -1.0), jnp.where(lane == 1, jnp.where(vcol, lp2, -1.0), 0.0))
    base = carry[...]
    carry[...] = base + RUN * units
    return lp, (RUN * units).astype(I32), lstart.astype(I32), base.astype(I32)


def _outproj_kernel(gtp_ref, gts_ref, osp_ref, oss_ref, sa_ref, sb_ref, xp_ref, xs_ref, g1p_ref, shp_ref, scp_ref,
                    g1s_ref, shs_ref, scs_ref, n2_ref, wur_ref, wus_ref, wo_ref, wrh_ref, wrl_ref, br_ref,
                    x1_o, h2_o, rt_o, lp_o, cnt_o, ls_o, gb_o, carry, *, nps, ns):
    i = pl.program_id(0)
    is_s = i >= nps

    @pl.when(i == 0)
    def _():
        carry[...] = jnp.zeros_like(carry)

    routed = []
    for r in range(0, TT, TM):
        rows = slice(r, r + TM)
        x = jnp.where(is_s, xs_ref[rows], xp_ref[rows])
        g1 = jnp.where(is_s, g1s_ref[rows], g1p_ref[...])
        sh = jnp.where(is_s, shs_ref[rows], shp_ref[...])
        sc = jnp.where(is_s, scs_ref[rows], scp_ref[...])
        gated = jnp.where(is_s, gts_ref[rows], gtp_ref[rows])
        oswa = jnp.where(is_s, oss_ref[rows], osp_ref[rows])
        y_ret = jnp.dot(gated, wur_ref[...], preferred_element_type=F32)
        y_swa = jnp.dot(oswa, wus_ref[...], preferred_element_type=F32)
        merged = sa_ref[rows].astype(F32) * y_ret + sb_ref[rows].astype(F32) * y_swa
        x1 = x + g1 * jnp.dot(merged.astype(BF16), wo_ref[...], preferred_element_type=F32)
        x1_o[rows] = x1
        h2 = _rms(x1, n2_ref[...]) * (1.0 + sc) + sh
        hi = h2.astype(BF16)
        h2_o[rows] = hi
        lo = (h2 - hi.astype(F32)).astype(BF16)
        wrh = wrh_ref[...]
        logits = (jnp.dot(hi, wrh, preferred_element_type=F32) + jnp.dot(lo, wrh, preferred_element_type=F32)
                  + jnp.dot(hi, wrl_ref[...], preferred_element_type=F32) + br_ref[...])
        e1, e2, w1, w2 = _route(logits)
        lane = lax.broadcasted_iota(I32, logits.shape, 1)
        rt_o[rows] = jnp.where(lane == 2, w1, jnp.where(lane == 3, w2, 0.0))
        routed.append((e1, e2))
    for sub, (e1, e2) in enumerate(routed):
        row = lax.broadcasted_iota(I32, (TM, LANES), 0) + sub * TM
        valid = jnp.logical_not(is_s) | (row < ns)
        lp_o[sub * TM:(sub + 1) * TM], cnt_o[sub], ls_o[sub], gb_o[sub] = _plan_tile(e1, e2, valid, carry)


def _outproj(gated_p, gated_s, oswa_p, oswa_s, siga, sigb, xp, xs_pad, modp, mods, n2, wur, wus, wo, wr, br,
             *, nps, spb, nb, ns):
    d = xp.shape[1]
    sub = TT // TM
    nrow = (nps + 1) * TT
    row = lambda i: (i, 0)
    pstep = lambda i: (jnp.minimum(i, nps - 1), 0)
    pbatch = lambda col: (lambda i: (jnp.minimum(i // spb, nb - 1), 0, col))
    scol = lambda col: (lambda i: (0, col))
    const = lambda i: (0, 0)
    wr_hi = wr.astype(BF16)
    wr_lo = (wr - wr_hi.astype(F32)).astype(BF16)
    meta = jax.ShapeDtypeStruct(((nps + 1) * sub, 1, LANES), I32)
    mspec = pl.BlockSpec((sub, 1, LANES), lambda i: (i, 0, 0))
    return pl.pallas_call(
        functools.partial(_outproj_kernel, nps=nps, ns=ns),
        out_shape=[jax.ShapeDtypeStruct((nrow, d), F32),
                   jax.ShapeDtypeStruct((nrow, d), BF16),
                   jax.ShapeDtypeStruct((nrow, LANES), F32),
                   jax.ShapeDtypeStruct((nrow, LANES), F32), meta, meta, meta],
        grid=(nps + 1,),
        in_specs=[pl.BlockSpec((TT, _RET_W), pstep), pl.BlockSpec((TT, _RET_W), const),
                  pl.BlockSpec((TT, _SWA_QW), pstep), pl.BlockSpec((TT, _SWA_QW), const),
                  pl.BlockSpec((TT, d), row), pl.BlockSpec((TT, d), row),
                  pl.BlockSpec((TT, d), pstep), pl.BlockSpec((TT, d), const),
                  pl.BlockSpec((None, 1, d), pbatch(2)), pl.BlockSpec((None, 1, d), pbatch(3)),
                  pl.BlockSpec((None, 1, d), pbatch(4)),
                  pl.BlockSpec((TT, d), scol(2)), pl.BlockSpec((TT, d), scol(3)), pl.BlockSpec((TT, d), scol(4)),
                  pl.BlockSpec((1, d), const),
                  pl.BlockSpec(wur.shape, const), pl.BlockSpec(wus.shape, const), pl.BlockSpec(wo.shape, const),
                  pl.BlockSpec(wr.shape, const), pl.BlockSpec(wr.shape, const), pl.BlockSpec((1, LANES), const)],
        out_specs=[pl.BlockSpec((TT, d), row), pl.BlockSpec((TT, d), row), pl.BlockSpec((TT, LANES), row),
                   pl.BlockSpec((TT, LANES), row), mspec, mspec, mspec],
        scratch_shapes=[pltpu.VMEM((1, LANES), F32)],
        compiler_params=_cparams("arbitrary"),
    )(gated_p, gated_s, oswa_p, oswa_s, siga, sigb, xp, xs_pad, modp, modp, modp, mods, mods, mods, n2,
      wur, wus, wo, wr_hi, wr_lo, br)


def _aligned(v):
    return v if isinstance(v, int) else pl.multiple_of(v, RUN)


def _run_copy(src, dst, s_start, d_start, n, sem):
    s_start, d_start, n = _aligned(s_start), _aligned(d_start), _aligned(n)
    return pltpu.make_async_copy(src.at[pl.ds(s_start, n)], dst.at[pl.ds(d_start, n)], sem)


def _each_run(step, fn):
    for e in range(N_EXPERTS):
        fn(step * N_EXPERTS + e)


def _dispatch_kernel(cnt_ref, ls_ref, gd_ref, ps_ref, pn_ref, nu_ref, h_ref, lp_ref, xs_ref,
                     sorted_scr, zero_scr, sem, zsem, *, nt, maxt):
    i = pl.program_id(0)

    def pad(e):
        return _run_copy(zero_scr, xs_ref, 0, ps_ref[e], pn_ref[e], zsem)

    def tail(j):
        return _run_copy(zero_scr, xs_ref, 0, j * TE, TE, zsem)

    def each_pad(fn):
        def body(e, c):
            @pl.when(pn_ref[e] > 0)
            def _():
                fn(pad(e))
            return c
        lax.fori_loop(0, N_EXPERTS, body, 0)

    def each_tail(fn):
        def body(j, c):
            fn(tail(j))
            return c
        lax.fori_loop(nu_ref[0], maxt, body, 0)

    @pl.when(i == 0)
    def _():
        zero_scr[...] = jnp.zeros_like(zero_scr)
        each_pad(lambda cp: cp.start())
        each_tail(lambda cp: cp.start())

    lpt = lp_ref[...].T
    slot = lax.broadcasted_iota(I32, (SLOTS, TM), 0).astype(F32)
    perm = ((slot == lpt[0:1]) | (slot == lpt[1:2])).astype(BF16)
    sorted_scr[i % 2] = jnp.dot(perm, h_ref[...], preferred_element_type=F32).astype(BF16)

    def copy(step):
        return lambda k: _run_copy(sorted_scr.at[step % 2], xs_ref, ls_ref[k], gd_ref[k], cnt_ref[k], sem.at[step % 2])

    _each_run(i, lambda k: copy(i)(k).start())

    @pl.when(i > 0)
    def _():
        _each_run(i - 1, lambda k: copy(i - 1)(k).wait())

    @pl.when(i == nt - 1)
    def _():
        _each_run(i, lambda k: copy(i)(k).wait())
        each_pad(lambda cp: cp.wait())
        each_tail(lambda cp: cp.wait())


def _dispatch(cnt, ls, gd, ps, pn, nu, h2, lp, *, nt, maxt):
    d = h2.shape[1]
    return pl.pallas_call(
        functools.partial(_dispatch_kernel, nt=nt, maxt=maxt),
        out_shape=jax.ShapeDtypeStruct((maxt * TE, d), BF16),
        grid_spec=pltpu.PrefetchScalarGridSpec(
            num_scalar_prefetch=6,
            grid=(nt,),
            in_specs=[pl.BlockSpec((TM, d), lambda i, *_: (i, 0)),
                      pl.BlockSpec((TM, LANES), lambda i, *_: (i, 0))],
            out_specs=pl.BlockSpec(memory_space=pl.ANY),
            scratch_shapes=[pltpu.VMEM((2, SLOTS, d), BF16), pltpu.VMEM((TE, d), BF16),
                            pltpu.SemaphoreType.DMA((2,)), pltpu.SemaphoreType.DMA(())]),
        compiler_params=_cparams("arbitrary"),
    )(cnt, ls, gd, ps, pn, nu, h2, lp)


def _experts_kernel(te_ref, nu_ref, first_ref, nxt_ref, par_ref, x_ref, w1_hbm, w3_hbm, w2_hbm, y_ref,
                    w1f, w3f, w2f, w1b, w3b, w2b, sem):
    j = pl.program_id(0)

    def fetch(e, slot):
        return [pltpu.make_async_copy(src.at[e], dst.at[slot], sem.at[slot, n])
                for n, (src, dst) in enumerate(((w1_hbm, w1f), (w3_hbm, w3f), (w2_hbm, w2f)))]

    @pl.when(j == 0)
    def _():
        for cp in fetch(te_ref[0], par_ref[0]):
            cp.start()

    @pl.when(first_ref[j] == 1)
    def _():
        slot = par_ref[j]

        @pl.when(nxt_ref[j] >= 0)
        def _():
            for cp in fetch(nxt_ref[j], 1 - slot):
                cp.start()

        for cp in fetch(te_ref[j], slot):
            cp.wait()
        w1b[...] = w1f[slot].astype(BF16)
        w3b[...] = w3f[slot].astype(BF16)
        w2b[...] = w2f[slot].astype(BF16)

    @pl.when(j < nu_ref[0])
    def _():
        for r in range(0, TE, TE_SUB):
            x = x_ref[r:r + TE_SUB]
            a = jnp.dot(x, w1b[...], preferred_element_type=F32)
            b = jnp.dot(x, w3b[...], preferred_element_type=F32)
            y = jnp.dot((_silu(a) * b).astype(BF16), w2b[...], preferred_element_type=F32)
            y_ref[r:r + TE_SUB] = y.astype(BF16)


def _experts(te, nu, first, nxt, par, xs, w1, w3, w2, *, maxt):
    d = xs.shape[1]
    f = w1.shape[2]
    used = lambda j, te, nu, *_: (jnp.minimum(j, nu[0] - 1), 0)
    hbm = pl.BlockSpec(memory_space=pl.ANY)
    return pl.pallas_call(
        _experts_kernel,
        out_shape=jax.ShapeDtypeStruct(xs.shape, BF16),
        grid_spec=pltpu.PrefetchScalarGridSpec(
            num_scalar_prefetch=5,
            grid=(maxt,),
            in_specs=[pl.BlockSpec((TE, d), used), hbm, hbm, hbm],
            out_specs=pl.BlockSpec((TE, d), used),
            scratch_shapes=[pltpu.VMEM((2, d, f), F32), pltpu.VMEM((2, d, f), F32), pltpu.VMEM((2, f, d), F32),
                            pltpu.VMEM((d, f), BF16), pltpu.VMEM((d, f), BF16), pltpu.VMEM((f, d), BF16),
                            pltpu.SemaphoreType.DMA((2, 3))]),
        input_output_aliases={5: 0},
        compiler_params=_cparams("arbitrary"),
    )(te, nu, first, nxt, par, xs, w1, w3, w2)


def _combine_kernel(cnt_ref, ls_ref, gd_ref, ys_ref, lp_ref, rt_ref, x1_ref, g2p_ref, g2s_ref, fg_ref,
                    yp_o, ys_o, ybuf, sem, *, npt):
    i = pl.program_id(0)
    nt = npt + 1

    def copy(step):
        return lambda k: _run_copy(ys_ref, ybuf.at[step % 2], gd_ref[k], ls_ref[k], cnt_ref[k], sem.at[step % 2])

    def fetch(step):
        ybuf[step % 2] = jnp.zeros(ybuf.shape[1:], BF16)
        _each_run(step, lambda k: copy(step)(k).start())

    @pl.when(i == 0)
    def _():
        fetch(0)

    @pl.when(i + 1 < nt)
    def _():
        fetch(i + 1)

    _each_run(i, lambda k: copy(i)(k).wait())

    yb = ybuf[i % 2]
    slot = lax.broadcasted_iota(I32, (TM, SLOTS), 1).astype(F32)
    lp = lp_ref[...]
    rt = rt_ref[...]

    def unsort(col):
        return jnp.dot((slot == lp[:, col:col + 1]).astype(BF16), yb, preferred_element_type=F32)

    moe = rt[:, 2:3] * unsort(0) + rt[:, 3:4] * unsort(1)
    g2 = jnp.where(i >= npt, g2s_ref[...], g2p_ref[...])
    y = _rms(x1_ref[...] + g2 * moe, fg_ref[...])

    @pl.when(i < npt)
    def _():
        yp_o[...] = y

    @pl.when(i >= npt)
    def _():
        ys_o[...] = y


def _combine(cnt, ls, gd, ys, lp, rt, x1, modp, mods, fg, *, npt, tpb, nb):
    d = x1.shape[1]
    nt = npt + 1
    row = lambda i, *_: (i, 0)
    return pl.pallas_call(
        functools.partial(_combine_kernel, npt=npt),
        out_shape=[jax.ShapeDtypeStruct((npt * TM, d), F32), jax.ShapeDtypeStruct((TM, d), F32)],
        grid_spec=pltpu.PrefetchScalarGridSpec(
            num_scalar_prefetch=3,
            grid=(nt,),
            in_specs=[pl.BlockSpec(memory_space=pl.ANY),
                      pl.BlockSpec((TM, LANES), row), pl.BlockSpec((TM, LANES), row), pl.BlockSpec((TM, d), row),
                      pl.BlockSpec((None, 1, d), lambda i, *_: (jnp.minimum(i // tpb, nb - 1), 0, 5)),
                      pl.BlockSpec((TM, d), lambda i, *_: (0, 5)),
                      pl.BlockSpec((1, d), lambda i, *_: (0, 0))],
            out_specs=[pl.BlockSpec((TM, d), lambda i, *_: (jnp.minimum(i, npt - 1), 0)),
                       pl.BlockSpec((TM, d), lambda i, *_: (0, 0))],
            scratch_shapes=[pltpu.VMEM((2, SLOTS, d), BF16), pltpu.SemaphoreType.DMA((2,))]),
        compiler_params=_cparams("arbitrary"),
    )(cnt, ls, gd, ys, lp, rt, x1, modp, mods, fg)


TAB_LO = 64


def _rotation_tables(t):
    inv_r = jnp.repeat(1.0 / (ROPE_THETA ** jnp.linspace(0.0, 1.0, RET_DK // 2, dtype=F32)), 2)
    sign_r = jnp.where(jnp.arange(RET_DK) % 2 == 0, -1.0, 1.0).astype(F32)
    inv_w = jnp.tile(ROPE_THETA ** (-jnp.arange(0, SWA_HD, 2, dtype=F32) / SWA_HD), LANES // (SWA_HD // 2))
    sign_w = jnp.where(jnp.arange(LANES) % SWA_HD < SWA_HD // 2, -1.0, 1.0).astype(F32)
    hi = (jnp.arange(t // TAB_LO, dtype=I32) * TAB_LO).astype(F32)[:, None]
    lo = jnp.arange(TAB_LO, dtype=I32).astype(F32)[:, None]
    past = jnp.full((1, 1), PAST_LEN, F32)

    def pair(inv, sign):
        a, b = hi * inv[None, :], lo * inv[None, :]
        ca, sa, cb, sb = jnp.cos(a)[:, None], jnp.sin(a)[:, None], jnp.cos(b)[None], jnp.sin(b)[None]
        cos = (ca * cb - sa * sb).reshape(t, LANES)
        sin = (sa * cb + ca * sb).reshape(t, LANES) * sign[None, :]
        ang = past * inv[None, :]
        return (cos, sin), (jnp.cos(ang), jnp.sin(ang) * sign[None, :])

    (pr, sr_), (pw, sw_) = pair(inv_r, sign_r), pair(inv_w, sign_w)
    return pr + pw, sr_ + sw_


def kernel(x_prompt, x_sample, c_prompt, c_sample, state_ret, cache_swa_k, cache_swa_v, w_ada, b_ada, norm1_g, norm2_g, w_in, w_up_ret, w_up_swa, w_o, sink, w_rg, b_rg, w_re, b_re, w1, w3, w2, final_g):
    nb, t, d = x_prompt.shape
    ns, dec_seq, _ = x_sample.shape
    depth = w_ada.shape[0]
    assert depth == 1 and dec_seq == 1, "single layer, one new token per sequence"
    assert t % TT == 0 and ns <= TM and ns % 16 == 0 and d % LANES == 0
    assert t % RET_BLOCK == 0 and t % (SWA_STEP_BLOCKS * WINDOW) == 0 and t % TAB_LO == 0
    assert N_GROUPS + N_EXPERTS <= LANES
    w = cache_swa_k.shape[2]
    tpb = t // TM
    npt = nb * tpb
    spb = t // TT
    nps = nb * spb
    nt = (nps + 1) * (TT // TM)
    np_rows = nb * t
    n_tok = np_rows + ns
    maxt = -(-(2 * n_tok + nt * N_EXPERTS * RUN + N_EXPERTS * (TE - 1)) // TE)

    xp = x_prompt.reshape(np_rows, d)
    xs_pad = jnp.pad(x_sample.reshape(ns, d), ((0, TT - ns), (0, 0)))

    c_all = jnp.concatenate([jnp.pad(c_prompt, ((0, SUBLANES - nb % SUBLANES), (0, 0))),
                             jnp.pad(c_sample, ((0, TT - ns), (0, 0)))])
    mod = _modulation(c_all, w_ada[0], b_ada[0])
    modp = mod[:nb].reshape(nb, 1, 6 * d)
    mods = mod[c_all.shape[0] - TT:]

    tabs_p, tabs_s = _rotation_tables(t)
    rq, rk, rv, rg, sq, sk, sv, siga, sigb = _inproj(
        xp, xs_pad, modp, mods, norm1_g, w_in[0].astype(BF16), tabs_p, tabs_s, nps=nps, spb=spb, nb=nb)

    dm, qd, kd, cd, gamma = _ret_tables(RET_BLOCK)
    gated_p, st_p = _retention_prompt(rq, rk, rv, rg, (dm, qd, kd, cd), nb=nb, t=t)
    gated_s, st_s = _retention_sample(rq, rk, rv, rg, state_ret[0], gamma, row0=np_rows, ns=ns)
    oswa_p = _swa_prompt(sq, sk, sv, sink[0], nb=nb, t=t)
    to_t = lambda c: jnp.transpose(c[0], (0, 2, 3, 1)).reshape(ns, _SWA_KW, w)
    from_t = lambda c: jnp.transpose(c.reshape(ns, SWA_KV_HEADS, SWA_HD, w), (0, 3, 1, 2))[None]
    oswa_s, ks_new, vs_new = _swa_sample(sq, sk, sv, to_t(cache_swa_k), to_t(cache_swa_v), sink[0],
                                         row0=np_rows, ns=ns)
    gated_s = jnp.pad(gated_s, ((0, TT - ns), (0, 0)))
    oswa_s = jnp.pad(oswa_s, ((0, TT - ns), (0, 0)))

    wr = jnp.pad(jnp.concatenate([w_rg[0], w_re[0]], axis=1), ((0, 0), (0, LANES - N_GROUPS - N_EXPERTS)))
    br = jnp.pad(jnp.concatenate([b_rg[0], b_re[0]]), (0, LANES - N_GROUPS - N_EXPERTS)).reshape(1, LANES)
    x1, h2, rt, lp, cnt, ls, gb = _outproj(
        gated_p, gated_s, oswa_p, oswa_s, siga, sigb, xp, xs_pad, modp, mods, norm2_g,
        w_up_ret[0].astype(BF16), w_up_swa[0].astype(BF16), w_o[0].astype(BF16), wr, br,
        nps=nps, spb=spb, nb=nb, ns=ns)
    cnt = cnt[:, 0, :N_EXPERTS]
    ls = ls[:, 0, :N_EXPERTS]
    gb = gb[:, 0, :N_EXPERTS]
    seg = jnp.sum(cnt, axis=0)
    tiles = (seg + TE - 1) // TE
    tile_end = jnp.cumsum(tiles)
    row_start = (tile_end - tiles) * TE
    gd = (gb + row_start[None, :]).reshape(-1)
    n_used = tile_end[-1:]
    jj = jnp.minimum(jnp.arange(maxt, dtype=I32), n_used[0] - 1)
    te = jnp.minimum(jnp.sum((tile_end[None, :] <= jj[:, None]).astype(I32), axis=1), N_EXPERTS - 1)
    cnt = cnt.reshape(-1)
    ls = ls.reshape(-1)
    n_used = n_used.astype(I32)
    xs = _dispatch(cnt, ls, gd, row_start + seg, tiles * TE - seg, n_used, h2, lp, nt=nt, maxt=maxt)
    has = tiles > 0
    eidx = jnp.arange(N_EXPERTS, dtype=I32)
    later = jnp.where(has[None, :] & (eidx[None, :] > eidx[:, None]), eidx[None, :], N_EXPERTS)
    nxt_e = jnp.min(later, axis=1)
    nxt_e = jnp.where(nxt_e < N_EXPERTS, nxt_e, -1).astype(I32)
    par_e = ((jnp.cumsum(has.astype(I32)) - 1) % 2).astype(I32)
    tile_idx = jnp.arange(maxt, dtype=I32)
    first = ((tile_idx == (tile_end - tiles)[te]) & (tile_idx < n_used[0])).astype(I32)
    ys = _experts(te, n_used, first, nxt_e[te], par_e[te], xs, w1[0], w3[0], w2[0], maxt=maxt)
    y_p, y_s = _combine(cnt, ls, gd, ys, lp, rt, x1, modp, mods, final_g.reshape(1, d), npt=npt, tpb=tpb, nb=nb)

    y_prompt = y_p.reshape(nb, t, d)
    y_sample = y_s[:ns].reshape(ns, 1, d)
    wk = min(WINDOW, t)
    last = lambda a: jnp.stack([a[(b + 1) * t - wk:(b + 1) * t] for b in range(nb)]).reshape(
        nb, wk, SWA_KV_HEADS, SWA_HD)
    skp, svp = last(sk), last(sv)
    return (y_prompt, y_sample, st_p[None], st_s[None], skp[None], svp[None], from_t(ks_new), from_t(vs_new))
```

```python
import functools

import jax
import jax.numpy as jnp
from jax import lax
from jax.experimental import pallas as pl
from jax.experimental.pallas import tpu as pltpu

F32 = jnp.float32
BF16 = jnp.bfloat16
I32 = jnp.int32

PAST_LEN = 8192
RET_HEADS = 4
RET_DK = 128
RET_DV = 128
RET_CHUNK = 128
SWA_HEADS = 8
SWA_KV_HEADS = 2
SWA_HD = 64
WINDOW = 128
ROPE_THETA = 10000.0
N_GROUPS = 4
EXPERTS_PER_GROUP = 8
N_EXPERTS = N_GROUPS * EXPERTS_PER_GROUP
D_EXPERT = 256
NORM_EPS = 1e-6

LANES = 128
SUBLANES = 8
TM = 256
TT = 2 * TM
RUN = 16
SLOTS = 2 * TM + N_EXPERTS * RUN
TE = 512
TE_SUB = 256
VMEM_LIMIT = 56 * 1024 * 1024

_RET_W = RET_HEADS * RET_DK
_SWA_QW = SWA_HEADS * SWA_HD
_SWA_KW = SWA_KV_HEADS * SWA_HD


def _cparams(*sem):
    return pltpu.CompilerParams(dimension_semantics=sem, vmem_limit_bytes=VMEM_LIMIT)


def _sigmoid(x):
    return 1.0 / (1.0 + jnp.exp(-x))


def _silu(x):
    return x * _sigmoid(x)


def _bdot(a, b):
    return jnp.dot(a.astype(BF16), b.astype(BF16), preferred_element_type=F32)


def _bdot_nt(a, b):
    return lax.dot_general(a.astype(BF16), b.astype(BF16), (((1,), (1,)), ((), ())), preferred_element_type=F32)


def _mod_kernel(c_ref, w_ref, b_ref, o_ref):
    o_ref[...] = _bdot(_silu(c_ref[...]), w_ref[...]) + b_ref[...]


def _modulation(c_all, w_ada, b_ada):
    rows, d = c_all.shape
    n = w_ada.shape[1]
    return pl.pallas_call(
        _mod_kernel,
        out_shape=jax.ShapeDtypeStruct((rows, n), F32),
        grid=(n // d,),
        in_specs=[pl.BlockSpec((rows, d), lambda j: (0, 0)),
                  pl.BlockSpec((d, d), lambda j: (0, j)),
                  pl.BlockSpec((1, d), lambda j: (0, j))],
        out_specs=pl.BlockSpec((rows, d), lambda j: (0, j)),
        compiler_params=_cparams("arbitrary"),
    )(c_all, w_ada, b_ada.reshape(1, n))


def _rms(x, g):
    return x * lax.rsqrt(jnp.mean(x * x, axis=-1, keepdims=True) + NORM_EPS) * g


def _pair_rotate(z, cos, sin_signed):
    n = z.shape[-1]
    lane = lax.broadcasted_iota(I32, z.shape, 1)
    partner = jnp.where((lane & 1) == 0, pltpu.roll(z, n - 1, 1), pltpu.roll(z, 1, 1))
    reps = n // LANES
    cos = jnp.concatenate([cos] * reps, axis=1) if reps > 1 else cos
    sin_signed = jnp.concatenate([sin_signed] * reps, axis=1) if reps > 1 else sin_signed
    return z * cos + partner * sin_signed


def _half_rotate(z, cos, sin_signed):
    n = z.shape[-1]
    half = SWA_HD // 2
    lane = lax.broadcasted_iota(I32, z.shape, 1)
    partner = jnp.where((lane & (SWA_HD - 1)) < half, pltpu.roll(z, n - half, 1), pltpu.roll(z, half, 1))
    reps = n // LANES
    cos = jnp.concatenate([cos] * reps, axis=1) if reps > 1 else cos
    sin_signed = jnp.concatenate([sin_signed] * reps, axis=1) if reps > 1 else sin_signed
    return z * cos + partner * sin_signed


def _inproj_kernel(xp_ref, xs_ref, shp_ref, scp_ref, shs_ref, scs_ref, n1_ref, w_ref,
                   crp_ref, srp_ref, cwp_ref, swp_ref, crs_ref, srs_ref, cws_ref, sws_ref,
                   rq_o, rk_o, rv_o, rg_o, sq_o, sk_o, sv_o, za_o, zb_o, *, nps):
    is_s = pl.program_id(0) >= nps
    d = xp_ref.shape[-1]
    for r in range(0, TT, TM):
        rows = slice(r, r + TM)
        x = jnp.where(is_s, xs_ref[rows], xp_ref[rows])
        sh = jnp.where(is_s, shs_ref[rows], shp_ref[...])
        sc = jnp.where(is_s, scs_ref[rows], scp_ref[...])
        h = (_rms(x, n1_ref[...]) * (1.0 + sc) + sh).astype(BF16)
        cr, sr, cw, sw = (jnp.where(is_s, s_ref[...], p_ref[rows]) for s_ref, p_ref in
                          ((crs_ref, crp_ref), (srs_ref, srp_ref), (cws_ref, cwp_ref), (sws_ref, swp_ref)))

        def seg(a, b):
            return jnp.dot(h, w_ref[:, a:b], preferred_element_type=F32)

        o = 0
        rq_o[rows] = _pair_rotate(seg(o, o + _RET_W), cr, sr).astype(BF16)
        o += _RET_W
        rk_o[rows] = (_pair_rotate(seg(o, o + _RET_W), cr, sr) * (RET_DK ** -0.5)).astype(BF16)
        o += _RET_W
        rv_o[rows] = seg(o, o + _RET_W).astype(BF16)
        o += _RET_W
        rg_o[rows] = _silu(seg(o, o + _RET_W)).astype(BF16)
        o += _RET_W
        sq_o[rows] = (_half_rotate(seg(o, o + _SWA_QW), cw, sw) * (SWA_HD ** -0.5)).astype(BF16)
        o += _SWA_QW
        zkv = seg(o, o + 2 * _SWA_KW)
        sk_o[rows] = _half_rotate(zkv[:, :_SWA_KW], cw, sw)
        sv_o[rows] = zkv[:, _SWA_KW:]
        o += 2 * _SWA_KW
        za_o[rows] = _sigmoid(seg(o, o + d)).astype(BF16)
        o += d
        zb_o[rows] = _sigmoid(seg(o, o + d)).astype(BF16)


def _inproj(xp, xs_pad, modp, mods, n1, w_in_b, tabs_p, tabs_s, *, nps, spb, nb):
    d = xp.shape[1]
    nrow = (nps + 1) * TT
    n_in = w_in_b.shape[1]
    pstep = lambda i: (jnp.minimum(i, nps - 1), 0)
    pbatch = lambda col: (lambda i: (jnp.minimum(i // spb, nb - 1), 0, col))
    tab_idx = lambda i: (jnp.where(i < nps, i % spb, 0), 0)
    out_cols = [(_RET_W, BF16)] * 4 + [(_SWA_QW, BF16), (_SWA_KW, F32), (_SWA_KW, F32), (d, BF16), (d, BF16)]
    return pl.pallas_call(
        functools.partial(_inproj_kernel, nps=nps),
        out_shape=[jax.ShapeDtypeStruct((nrow, c), t) for c, t in out_cols],
        grid=(nps + 1,),
        in_specs=[pl.BlockSpec((TT, d), pstep),
                  pl.BlockSpec((TT, d), lambda i: (0, 0)),
                  pl.BlockSpec((None, 1, d), pbatch(0)),
                  pl.BlockSpec((None, 1, d), pbatch(1)),
                  pl.BlockSpec((TT, d), lambda i: (0, 0)),
                  pl.BlockSpec((TT, d), lambda i: (0, 1)),
                  pl.BlockSpec((1, d), lambda i: (0, 0)),
                  pl.BlockSpec((d, n_in), lambda i: (0, 0))]
                 + [pl.BlockSpec((TT, LANES), tab_idx)] * 4
                 + [pl.BlockSpec((1, LANES), lambda i: (0, 0))] * 4,
        out_specs=[pl.BlockSpec((TT, c), lambda i: (i, 0)) for c, _ in out_cols],
        compiler_params=_cparams("arbitrary"),
    )(xp, xs_pad, modp, modp, mods, mods, n1, w_in_b, *tabs_p, *tabs_s)


RET_BLOCK = 512


def _ret_kernel(q_ref, k_ref, v_ref, g_ref, dm_ref, qd_ref, kd_ref, cd_ref, o_ref, st_ref, s_scr, *, nsteps):
    step = pl.program_id(1)

    @pl.when(step == 0)
    def _():
        s_scr[...] = jnp.zeros_like(s_scr)

    for h in range(RET_HEADS):
        sl = slice(h * RET_DK, (h + 1) * RET_DK)
        state = s_scr[h]
        q, k, v = q_ref[:, sl], k_ref[:, sl], v_ref[:, sl]
        att = _bdot_nt(q, k) * dm_ref[h]
        o = _bdot(att, v) + _bdot(q.astype(F32) * qd_ref[h], state)
        kd = (k.astype(F32) * kd_ref[h]).astype(BF16)
        kv = lax.dot_general(kd, v, (((0,), (0,)), ((), ())), preferred_element_type=F32)
        s_scr[h] = cd_ref[h] * state + kv
        o = o * lax.rsqrt(jnp.mean(o * o, axis=-1, keepdims=True) + NORM_EPS)
        o_ref[:, sl] = (o * g_ref[:, sl].astype(F32)).astype(BF16)

    @pl.when(step == nsteps - 1)
    def _():
        st_ref[...] = s_scr[...]


def _ret_tables(chunk):
    ld = jnp.log(1.0 - 2.0 ** (-5.0 - jnp.arange(RET_HEADS, dtype=F32)))
    idx = jnp.arange(chunk, dtype=F32)
    diff = idx[:, None] - idx[None, :]
    causal = diff >= 0
    dmask = jnp.where(causal[None], jnp.exp(ld[:, None, None] * jnp.where(causal, diff, 0.0)[None]), 0.0)
    k_dec = jnp.exp(ld[None, :] * (chunk - 1.0 - idx)[:, None])
    q_dec = jnp.exp(ld[None, :] * (idx + 1.0)[:, None])
    chunk_decay = jnp.exp(ld * chunk)
    bc = lambda t: jnp.broadcast_to(t.T[:, :, None], (RET_HEADS, chunk, RET_DV))
    cd = jnp.broadcast_to(chunk_decay[:, None, None], (RET_HEADS, 1, RET_DV))
    return dmask, bc(q_dec), bc(k_dec), cd, jnp.exp(ld)


def _retention_prompt(rq, rk, rv, rg, tabs, *, nb, t):
    rows = RET_BLOCK
    nsteps = t // rows
    dm, qd, kd, cd = tabs
    blk = lambda b, c: (b * nsteps + c, 0)
    full3 = lambda b, c: (0, 0, 0)
    return pl.pallas_call(
        functools.partial(_ret_kernel, nsteps=nsteps),
        out_shape=[jax.ShapeDtypeStruct((nb * t, _RET_W), BF16),
                   jax.ShapeDtypeStruct((nb, RET_HEADS, RET_DK, RET_DV), F32)],
        grid=(nb, nsteps),
        in_specs=[pl.BlockSpec((rows, _RET_W), blk)] * 4
                 + [pl.BlockSpec((RET_HEADS, rows, rows), full3)]
                 + [pl.BlockSpec((RET_HEADS, rows, RET_DV), full3)] * 2
                 + [pl.BlockSpec((RET_HEADS, 1, RET_DV), full3)],
        out_specs=[pl.BlockSpec((rows, _RET_W), blk),
                   pl.BlockSpec((None, RET_HEADS, RET_DK, RET_DV), lambda b, c: (b, 0, 0, 0))],
        scratch_shapes=[pltpu.VMEM((RET_HEADS, RET_DK, RET_DV), F32)],
        compiler_params=_cparams("arbitrary", "arbitrary"),
    )(rq, rk, rv, rg, dm, qd, kd, cd)


def _ret_sample_kernel(gam_ref, q_ref, k_ref, v_ref, g_ref, s0_ref, o_ref, st_ref, *, sb):
    gamma = gam_ref[pl.program_id(1)]
    q = q_ref[...].astype(F32)
    k = k_ref[...].astype(F32)
    v = v_ref[...].astype(F32)
    rows = sb * RET_DK
    s2 = s0_ref[...].reshape(rows, RET_DV)
    col_b = lax.broadcasted_iota(I32, (sb, rows), 1) // RET_DK
    row_b = lax.broadcasted_iota(I32, (sb, rows), 0)
    qexp = jnp.where(col_b == row_b, jnp.concatenate([q * gamma] * sb, axis=1), 0.0)
    o = jnp.sum(q * k, axis=-1, keepdims=True) * v + _bdot(qexp, s2)
    o = o * lax.rsqrt(jnp.mean(o * o, axis=-1, keepdims=True) + NORM_EPS)
    o_ref[...] = (o * g_ref[...].astype(F32)).astype(BF16)
    rep = (lax.broadcasted_iota(I32, (rows, sb), 0) // RET_DK == lax.broadcasted_iota(I32, (rows, sb), 1))
    rep = rep.astype(BF16)
    krep = _bdot(rep, k)
    vrep = _bdot(rep, v)
    eye = (lax.broadcasted_iota(I32, (rows, RET_DK), 0) % RET_DK == lax.broadcasted_iota(I32, (rows, RET_DK), 1))
    kcol = jnp.sum(jnp.where(eye, krep, 0.0), axis=-1, keepdims=True)
    st_ref[...] = (gamma * s2 + kcol * vrep).reshape(sb, RET_DK, RET_DV)


def _retention_sample(rq, rk, rv, rg, s0, gamma, *, row0, ns):
    sb = min(64, ns)
    base = row0 // sb
    blk = lambda i, h: (base + i, h)
    sblk = lambda i, h: (i, h, 0, 0)
    return pl.pallas_call(
        functools.partial(_ret_sample_kernel, sb=sb),
        out_shape=[jax.ShapeDtypeStruct((ns, _RET_W), BF16),
                   jax.ShapeDtypeStruct(s0.shape, F32)],
        grid=(ns // sb, RET_HEADS),
        in_specs=[pl.BlockSpec(memory_space=pltpu.SMEM)]
                 + [pl.BlockSpec((sb, RET_DK), blk)] * 4
                 + [pl.BlockSpec((sb, None, RET_DK, RET_DV), sblk)],
        out_specs=[pl.BlockSpec((sb, RET_DV), lambda i, h: (i, h)),
                   pl.BlockSpec((sb, None, RET_DK, RET_DV), sblk)],
        compiler_params=_cparams("arbitrary", "arbitrary"),
    )(gamma, rq, rk, rv, rg, s0)


def _sink_softmax(s, mask, sink):
    s = jnp.where(mask, s, -jnp.inf)
    m = jnp.maximum(jnp.max(s, axis=-1, keepdims=True), sink)
    p = jnp.exp(s - m)
    return p / (jnp.sum(p, axis=-1, keepdims=True) + jnp.exp(sink - m))


def _split_kv_heads(x):
    lo = lax.broadcasted_iota(I32, x.shape, 1) < SWA_HD
    h0_lo = jnp.where(lo, x, 0.0)
    h1_hi = jnp.where(lo, 0.0, x)
    return ((h0_lo, pltpu.roll(h0_lo, SWA_HD, 1)), (pltpu.roll(h1_hi, SWA_HD, 1), h1_hi))


SWA_STEP_BLOCKS = 4


def _swa_kernel(sink_ref, q_ref, kc_ref, kp_ref, vc_ref, vp_ref, o_ref):
    n = pl.program_id(1)
    c = WINDOW
    kk = jnp.concatenate([kp_ref[...], kc_ref[...]], axis=0)
    vv = jnp.concatenate([vp_ref[...], vc_ref[...]], axis=0)
    ks = [[a.astype(BF16) for a in pair] for pair in _split_kv_heads(kk)]
    vs = [[a.astype(BF16) for a in pair] for pair in _split_kv_heads(vv)]
    qi = lax.broadcasted_iota(I32, (2 * c, 2 * c), 0) % c
    ki = lax.broadcasted_iota(I32, (2 * c, 2 * c), 1)
    band = (ki > qi) & (ki <= qi + c)
    top = lax.broadcasted_iota(I32, (2 * c, 1), 0) < c
    for s in range(SWA_STEP_BLOCKS):
        rows = slice(s * c, (s + 1) * c)
        keys = slice(s * c, (s + 2) * c)
        mask = (band & ((ki >= c) | (n > 0))) if s == 0 else band
        for kvh in range(SWA_KV_HEADS):
            j0, j1 = 2 * kvh, 2 * kvh + 1
            q2 = jnp.concatenate([q_ref[rows, j0 * LANES:(j0 + 1) * LANES],
                                  q_ref[rows, j1 * LANES:(j1 + 1) * LANES]], axis=0)
            kcat = jnp.concatenate([ks[kvh][0][keys], ks[kvh][1][keys]], axis=0)
            vcat = jnp.concatenate([vs[kvh][0][keys], vs[kvh][1][keys]], axis=0)
            sc = lax.dot_general(q2, kcat, (((1,), (1,)), ((), ())), preferred_element_type=F32)
            ps = []
            for half in range(2):
                sink = jnp.where(top, sink_ref[2 * j0 + half], sink_ref[2 * j1 + half])
                ps.append(_sink_softmax(sc[:, half * 2 * c:(half + 1) * 2 * c], mask, sink).astype(BF16))
            o = jnp.dot(jnp.concatenate(ps, axis=1), vcat, preferred_element_type=F32)
            o_ref[rows, j0 * LANES:(j0 + 1) * LANES] = o[:c].astype(BF16)
            o_ref[rows, j1 * LANES:(j1 + 1) * LANES] = o[c:].astype(BF16)


def _swa_prompt(sq, sk, sv, sink, *, nb, t):
    rows = SWA_STEP_BLOCKS * WINDOW
    nsteps = t // rows
    nblk = t // WINDOW
    cur = lambda b, n: (b * nsteps + n, 0)
    prev = lambda b, n: (b * nblk + jnp.maximum(n * SWA_STEP_BLOCKS - 1, 0), 0)
    return pl.pallas_call(
        _swa_kernel,
        out_shape=jax.ShapeDtypeStruct((nb * t, _SWA_QW), BF16),
        grid=(nb, nsteps),
        in_specs=[pl.BlockSpec(memory_space=pltpu.SMEM),
                  pl.BlockSpec((rows, _SWA_QW), cur),
                  pl.BlockSpec((rows, _SWA_KW), cur),
                  pl.BlockSpec((WINDOW, _SWA_KW), prev),
                  pl.BlockSpec((rows, _SWA_KW), cur),
                  pl.BlockSpec((WINDOW, _SWA_KW), prev)],
        out_specs=pl.BlockSpec((rows, _SWA_QW), cur),
        compiler_params=_cparams("arbitrary", "arbitrary"),
    )(sink, sq, sk, sk, sv, sv)


def _swa_sample_kernel(sink_ref, q_ref, kn_ref, vn_ref, kc_ref, vc_ref, o_ref, ko_ref, vo_ref, *, sb, w):
    pad = jnp.zeros((LANES - sb, _SWA_KW), F32)
    knt = jnp.concatenate([kn_ref[...], pad], axis=0).T
    vnt = jnp.concatenate([vn_ref[...], pad], axis=0).T
    kall = jnp.concatenate([kc_ref[b] for b in range(sb)] + [knt], axis=1)
    vall = jnp.concatenate([vc_ref[b] for b in range(sb)] + [vnt], axis=1)
    ncol = sb * w + LANES
    lo = lax.broadcasted_iota(I32, (sb, LANES), 1) < SWA_HD
    group = SWA_HEADS // SWA_KV_HEADS
    pieces = []
    for h in range(SWA_HEADS):
        slab = q_ref[:, (h // 2) * LANES:(h // 2 + 1) * LANES].astype(F32)
        mine = jnp.where(lo, slab, 0.0) if h % 2 == 0 else jnp.where(lo, 0.0, slab)
        pieces.append(mine if (h % 2) == (h // group) else pltpu.roll(mine, SWA_HD, 1))
    qrows = jnp.concatenate(pieces, axis=0)
    nrow = SWA_HEADS * sb
    s = _bdot(qrows, kall)
    rb = lax.broadcasted_iota(I32, (nrow, ncol), 0) % sb
    ci = lax.broadcasted_iota(I32, (nrow, ncol), 1)
    in_cache = (ci < sb * w) & (ci // w == rb) & ((w - ci % w) < WINDOW)
    mask = in_cache | (ci == sb * w + rb)
    sink_col = jnp.concatenate([jnp.full((sb, 1), sink_ref[h], F32) for h in range(SWA_HEADS)], axis=0)
    p = _sink_softmax(s, mask, sink_col)
    o = _bdot_nt(p, vall)
    for j in range(SWA_HEADS // 2):
        acc = jnp.zeros((sb, LANES), F32)
        for half in range(2):
            h = 2 * j + half
            oh = o[h * sb:(h + 1) * sb]
            own = jnp.where(lo, oh, 0.0) if h // group == 0 else jnp.where(lo, 0.0, oh)
            acc = acc + (own if (h // group) == half else pltpu.roll(own, SWA_HD, 1))
        o_ref[:, j * LANES:(j + 1) * LANES] = acc.astype(BF16)
    newest = lax.broadcasted_iota(I32, (_SWA_KW, w), 1) == w - 1
    for b in range(sb):
        ko_ref[b] = jnp.where(newest, knt[:, b:b + 1], pltpu.roll(kc_ref[b], w - 1, 1))
        vo_ref[b] = jnp.where(newest, vnt[:, b:b + 1], pltpu.roll(vc_ref[b], w - 1, 1))


def _swa_sample(sq, sk, sv, cache_kt, cache_vt, sink, *, row0, ns):
    sb = min(16, ns)
    w = cache_kt.shape[2]
    base = row0 // sb
    blk = lambda i: (base + i, 0)
    cblk = lambda i: (i, 0, 0)
    cspec = pl.BlockSpec((sb, _SWA_KW, w), cblk)
    return pl.pallas_call(
        functools.partial(_swa_sample_kernel, sb=sb, w=w),
        out_shape=[jax.ShapeDtypeStruct((ns, _SWA_QW), BF16),
                   jax.ShapeDtypeStruct(cache_kt.shape, F32), jax.ShapeDtypeStruct(cache_vt.shape, F32)],
        grid=(ns // sb,),
        in_specs=[pl.BlockSpec(memory_space=pltpu.SMEM),
                  pl.BlockSpec((sb, _SWA_QW), blk),
                  pl.BlockSpec((sb, _SWA_KW), blk),
                  pl.BlockSpec((sb, _SWA_KW), blk),
                  cspec, cspec],
        out_specs=[pl.BlockSpec((sb, _SWA_QW), lambda i: (i, 0)), cspec, cspec],
        compiler_params=_cparams("arbitrary"),
    )(sink, sq, sk, sv, cache_kt, cache_vt)


def _route(logits):
    lane = lax.broadcasted_iota(I32, logits.shape, 1)
    big = jnp.int32(1 << 20)
    neg = -jnp.inf

    def top(mask):
        v = jnp.max(jnp.where(mask, logits, neg), axis=-1, keepdims=True)
        i = jnp.min(jnp.where(mask & (logits == v), lane, big), axis=-1, keepdims=True)
        return v, i

    gmask = lane < N_GROUPS
    gmax, gsel = top(gmask)
    p_group = 1.0 / jnp.sum(jnp.where(gmask, jnp.exp(logits - gmax), 0.0), axis=-1, keepdims=True)
    first = N_GROUPS + gsel * EXPERTS_PER_GROUP
    emask = (lane >= first) & (lane < first + EXPERTS_PER_GROUP)
    v1, i1 = top(emask)
    v2, i2 = top(emask & (lane != i1))
    t = jnp.exp(v2 - v1)
    w1 = p_group / (1.0 + t)
    return i1 - N_GROUPS, i2 - N_GROUPS, w1, w1 * t


def _plan_tile(e1, e2, valid, carry):
    lane = lax.broadcasted_iota(I32, (TM, LANES), 1)
    oh1 = ((lane == e1) & valid).astype(F32)
    oh2 = ((lane == e2) & valid).astype(F32)
    oh = oh1 + oh2
    tri = (lax.broadcasted_iota(I32, (TM, TM), 0) > lax.broadcasted_iota(I32, (TM, TM), 1)).astype(BF16)
    before = _bdot(tri, oh)
    cnt = jnp.sum(oh, axis=0, keepdims=True)
    units = jnp.maximum(jnp.floor((cnt + (RUN - 1)) * (1.0 / RUN)), 1.0)
    upper = (lax.broadcasted_iota(I32, (LANES, LANES), 0) < lax.broadcasted_iota(I32, (LANES, LANES), 1))
    lstart = RUN * _bdot(jnp.broadcast_to(units, (SUBLANES, LANES)), upper.astype(BF16))[0:1]
    slot = lstart + before
    lp1 = jnp.sum(oh1 * slot, axis=-1, keepdims=True)
    lp2 = jnp.sum(oh2 * slot, axis=-1, keepdims=True)
    vcol = valid[:, 0:1]
    lp = jnp.where(lane == 0, jnp.where(vcol, lp1, -1.0), jnp.where(lane == 1, jnp.where(vcol, lp2, -1.0), 0.0))
    base = carry[...]
    carry[...] = base + RUN * units
    return lp, (RUN * units).astype(I32), lstart.astype(I32), base.astype(I32)


def _outproj_kernel(gtp_ref, gts_ref, osp_ref, oss_ref, sa_ref, sb_ref, xp_ref, xs_ref, g1p_ref, shp_ref, scp_ref,
                    g1s_ref, shs_ref, scs_ref, n2_ref, wur_ref, wus_ref, wo_ref, wrh_ref, wrl_ref, br_ref,
                    x1_o, h2_o, rt_o, lp_o, cnt_o, ls_o, gb_o, carry, *, nps, ns):
    i = pl.program_id(0)
    is_s = i >= nps

    @pl.when(i == 0)
    def _():
        carry[...] = jnp.zeros_like(carry)

    routed = []
    for r in range(0, TT, TM):
        rows = slice(r, r + TM)
        x = jnp.where(is_s, xs_ref[rows], xp_ref[rows])
        g1 = jnp.where(is_s, g1s_ref[rows], g1p_ref[...])
        sh = jnp.where(is_s, shs_ref[rows], shp_ref[...])
        sc = jnp.where(is_s, scs_ref[rows], scp_ref[...])
        gated = jnp.where(is_s, gts_ref[rows], gtp_ref[rows])
        oswa = jnp.where(is_s, oss_ref[rows], osp_ref[rows])
        y_ret = jnp.dot(gated, wur_ref[...], preferred_element_type=F32)
        y_swa = jnp.dot(oswa, wus_ref[...], preferred_element_type=F32)
        merged = sa_ref[rows].astype(F32) * y_ret + sb_ref[rows].astype(F32) * y_swa
        x1 = x + g1 * jnp.dot(merged.astype(BF16), wo_ref[...], preferred_element_type=F32)
        x1_o[rows] = x1
        h2 = _rms(x1, n2_ref[...]) * (1.0 + sc) + sh
        hi = h2.astype(BF16)
        h2_o[rows] = hi
        lo = (h2 - hi.astype(F32)).astype(BF16)
        wrh = wrh_ref[...]
        logits = (jnp.dot(hi, wrh, preferred_element_type=F32) + jnp.dot(lo, wrh, preferred_element_type=F32)
                  + jnp.dot(hi, wrl_ref[...], preferred_element_type=F32) + br_ref[...])
        e1, e2, w1, w2 = _route(logits)
        lane = lax.broadcasted_iota(I32, logits.shape, 1)
        rt_o[rows] = jnp.where(lane == 2, w1, jnp.where(lane == 3, w2, 0.0))
        routed.append((e1, e2))
    for sub, (e1, e2) in enumerate(routed):
        row = lax.broadcasted_iota(I32, (TM, LANES), 0) + sub * TM
        valid = jnp.logical_not(is_s) | (row < ns)
        lp_o[sub * TM:(sub + 1) * TM], cnt_o[sub], ls_o[sub], gb_o[sub] = _plan_tile(e1, e2, valid, carry)


def _outproj(gated_p, gated_s, oswa_p, oswa_s, siga, sigb, xp, xs_pad, modp, mods, n2, wur, wus, wo, wr, br,
             *, nps, spb, nb, ns):
    d = xp.shape[1]
    sub = TT // TM
    nrow = (nps + 1) * TT
    row = lambda i: (i, 0)
    pstep = lambda i: (jnp.minimum(i, nps - 1), 0)
    pbatch = lambda col: (lambda i: (jnp.minimum(i // spb, nb - 1), 0, col))
    scol = lambda col: (lambda i: (0, col))
    const = lambda i: (0, 0)
    wr_hi = wr.astype(BF16)
    wr_lo = (wr - wr_hi.astype(F32)).astype(BF16)
    meta = jax.ShapeDtypeStruct(((nps + 1) * sub, 1, LANES), I32)
    mspec = pl.BlockSpec((sub, 1, LANES), lambda i: (i, 0, 0))
    return pl.pallas_call(
        functools.partial(_outproj_kernel, nps=nps, ns=ns),
        out_shape=[jax.ShapeDtypeStruct((nrow, d), F32),
                   jax.ShapeDtypeStruct((nrow, d), BF16),
                   jax.ShapeDtypeStruct((nrow, LANES), F32),
                   jax.ShapeDtypeStruct((nrow, LANES), F32), meta, meta, meta],
        grid=(nps + 1,),
        in_specs=[pl.BlockSpec((TT, _RET_W), pstep), pl.BlockSpec((TT, _RET_W), const),
                  pl.BlockSpec((TT, _SWA_QW), pstep), pl.BlockSpec((TT, _SWA_QW), const),
                  pl.BlockSpec((TT, d), row), pl.BlockSpec((TT, d), row),
                  pl.BlockSpec((TT, d), pstep), pl.BlockSpec((TT, d), const),
                  pl.BlockSpec((None, 1, d), pbatch(2)), pl.BlockSpec((None, 1, d), pbatch(3)),
                  pl.BlockSpec((None, 1, d), pbatch(4)),
                  pl.BlockSpec((TT, d), scol(2)), pl.BlockSpec((TT, d), scol(3)), pl.BlockSpec((TT, d), scol(4)),
                  pl.BlockSpec((1, d), const),
                  pl.BlockSpec(wur.shape, const), pl.BlockSpec(wus.shape, const), pl.BlockSpec(wo.shape, const),
                  pl.BlockSpec(wr.shape, const), pl.BlockSpec(wr.shape, const), pl.BlockSpec((1, LANES), const)],
        out_specs=[pl.BlockSpec((TT, d), row), pl.BlockSpec((TT, d), row), pl.BlockSpec((TT, LANES), row),
                   pl.BlockSpec((TT, LANES), row), mspec, mspec, mspec],
        scratch_shapes=[pltpu.VMEM((1, LANES), F32)],
        compiler_params=_cparams("arbitrary"),
    )(gated_p, gated_s, oswa_p, oswa_s, siga, sigb, xp, xs_pad, modp, modp, modp, mods, mods, mods, n2,
      wur, wus, wo, wr_hi, wr_lo, br)


def _aligned(v):
    return v if isinstance(v, int) else pl.multiple_of(v, RUN)


def _run_copy(src, dst, s_start, d_start, n, sem):
    s_start, d_start, n = _aligned(s_start), _aligned(d_start), _aligned(n)
    return pltpu.make_async_copy(src.at[pl.ds(s_start, n)], dst.at[pl.ds(d_start, n)], sem)


def _each_run(step, fn):
    for e in range(N_EXPERTS):
        fn(step * N_EXPERTS + e)


def _dispatch_kernel(cnt_ref, ls_ref, gd_ref, ps_ref, pn_ref, nu_ref, h_ref, lp_ref, xs_ref,
                     sorted_scr, zero_scr, sem, zsem, *, nt, maxt):
    i = pl.program_id(0)

    def pad(e):
        return _run_copy(zero_scr, xs_ref, 0, ps_ref[e], pn_ref[e], zsem)

    def tail(j):
        return _run_copy(zero_scr, xs_ref, 0, j * TE, TE, zsem)

    def each_pad(fn):
        def body(e, c):
            @pl.when(pn_ref[e] > 0)
            def _():
                fn(pad(e))
            return c
        lax.fori_loop(0, N_EXPERTS, body, 0)

    def each_tail(fn):
        def body(j, c):
            fn(tail(j))
            return c
        lax.fori_loop(nu_ref[0], maxt, body, 0)

    @pl.when(i == 0)
    def _():
        zero_scr[...] = jnp.zeros_like(zero_scr)
        each_pad(lambda cp: cp.start())
        each_tail(lambda cp: cp.start())

    lpt = lp_ref[...].T
    slot = lax.broadcasted_iota(I32, (SLOTS, TM), 0).astype(F32)
    perm = ((slot == lpt[0:1]) | (slot == lpt[1:2])).astype(BF16)
    sorted_scr[i % 2] = jnp.dot(perm, h_ref[...], preferred_element_type=F32).astype(BF16)

    def copy(step):
        return lambda k: _run_copy(sorted_scr.at[step % 2], xs_ref, ls_ref[k], gd_ref[k], cnt_ref[k], sem.at[step % 2])

    _each_run(i, lambda k: copy(i)(k).start())

    @pl.when(i > 0)
    def _():
        _each_run(i - 1, lambda k: copy(i - 1)(k).wait())

    @pl.when(i == nt - 1)
    def _():
        _each_run(i, lambda k: copy(i)(k).wait())
        each_pad(lambda cp: cp.wait())
        each_tail(lambda cp: cp.wait())


def _dispatch(cnt, ls, gd, ps, pn, nu, h2, lp, *, nt, maxt):
    d = h2.shape[1]
    return pl.pallas_call(
        functools.partial(_dispatch_kernel, nt=nt, maxt=maxt),
        out_shape=jax.ShapeDtypeStruct((maxt * TE, d), BF16),
        grid_spec=pltpu.PrefetchScalarGridSpec(
            num_scalar_prefetch=6,
            grid=(nt,),
            in_specs=[pl.BlockSpec((TM, d), lambda i, *_: (i, 0)),
                      pl.BlockSpec((TM, LANES), lambda i, *_: (i, 0))],
            out_specs=pl.BlockSpec(memory_space=pl.ANY),
            scratch_shapes=[pltpu.VMEM((2, SLOTS, d), BF16), pltpu.VMEM((TE, d), BF16),
                            pltpu.SemaphoreType.DMA((2,)), pltpu.SemaphoreType.DMA(())]),
        compiler_params=_cparams("arbitrary"),
    )(cnt, ls, gd, ps, pn, nu, h2, lp)


def _experts_kernel(te_ref, nu_ref, start_ref, nxt_ref, par_ref, x_ref, w1_hbm, w3_hbm, w2_hbm, y_ref,
                    w1f, w3f, w2f, w1b, w3b, w2b, sem):
    j = pl.program_id(0)
    e = te_ref[j]

    def fetch(ex, slot):
        return [pltpu.make_async_copy(src.at[ex], dst.at[slot], sem.at[slot, n])
                for n, (src, dst) in enumerate(((w1_hbm, w1f), (w3_hbm, w3f), (w2_hbm, w2f)))]

    @pl.when(j == 0)
    def _():
        for cp in fetch(e, par_ref[e]):
            cp.start()

    @pl.when((j == start_ref[e]) & (j < nu_ref[0]))
    def _():
        slot = par_ref[e]

        @pl.when(nxt_ref[e] >= 0)
        def _():
            for cp in fetch(nxt_ref[e], 1 - slot):
                cp.start()

        for cp in fetch(e, slot):
            cp.wait()
        w1b[...] = w1f[slot].astype(BF16)
        w3b[...] = w3f[slot].astype(BF16)
        w2b[...] = w2f[slot].astype(BF16)

    @pl.when(j < nu_ref[0])
    def _():
        for r in range(0, TE, TE_SUB):
            x = x_ref[r:r + TE_SUB]
            a = jnp.dot(x, w1b[...], preferred_element_type=F32)
            b = jnp.dot(x, w3b[...], preferred_element_type=F32)
            y = jnp.dot((_silu(a) * b).astype(BF16), w2b[...], preferred_element_type=F32)
            y_ref[r:r + TE_SUB] = y.astype(BF16)


def _experts(te, nu, start, nxt, par, xs, w1, w3, w2, *, maxt):
    d = xs.shape[1]
    f = w1.shape[2]
    used = lambda j, te, nu, *_: (jnp.minimum(j, nu[0] - 1), 0)
    hbm = pl.BlockSpec(memory_space=pl.ANY)
    return pl.pallas_call(
        _experts_kernel,
        out_shape=jax.ShapeDtypeStruct(xs.shape, BF16),
        grid_spec=pltpu.PrefetchScalarGridSpec(
            num_scalar_prefetch=5,
            grid=(maxt,),
            in_specs=[pl.BlockSpec((TE, d), used), hbm, hbm, hbm],
            out_specs=pl.BlockSpec((TE, d), used),
            scratch_shapes=[pltpu.VMEM((2, d, f), F32), pltpu.VMEM((2, d, f), F32), pltpu.VMEM((2, f, d), F32),
                            pltpu.VMEM((d, f), BF16), pltpu.VMEM((d, f), BF16), pltpu.VMEM((f, d), BF16),
                            pltpu.SemaphoreType.DMA((2, 3))]),
        input_output_aliases={5: 0},
        compiler_params=_cparams("arbitrary"),
    )(te, nu, start, nxt, par, xs, w1, w3, w2)


def _combine_kernel(cnt_ref, ls_ref, gd_ref, ys_ref, lp_ref, rt_ref, x1_ref, g2p_ref, g2s_ref, fg_ref,
                    yp_o, ys_o, ybuf, sem, *, npt):
    i = pl.program_id(0)
    nt = npt + 1

    def copy(step):
        return lambda k: _run_copy(ys_ref, ybuf.at[step % 2], gd_ref[k], ls_ref[k], cnt_ref[k], sem.at[step % 2])

    def fetch(step):
        ybuf[step % 2] = jnp.zeros(ybuf.shape[1:], BF16)
        _each_run(step, lambda k: copy(step)(k).start())

    @pl.when(i == 0)
    def _():
        fetch(0)

    @pl.when(i + 1 < nt)
    def _():
        fetch(i + 1)

    _each_run(i, lambda k: copy(i)(k).wait())

    yb = ybuf[i % 2]
    slot = lax.broadcasted_iota(I32, (TM, SLOTS), 1).astype(F32)
    lp = lp_ref[...]
    rt = rt_ref[...]

    def unsort(col):
        return jnp.dot((slot == lp[:, col:col + 1]).astype(BF16), yb, preferred_element_type=F32)

    moe = rt[:, 2:3] * unsort(0) + rt[:, 3:4] * unsort(1)
    g2 = jnp.where(i >= npt, g2s_ref[...], g2p_ref[...])
    y = _rms(x1_ref[...] + g2 * moe, fg_ref[...])

    @pl.when(i < npt)
    def _():
        yp_o[...] = y

    @pl.when(i >= npt)
    def _():
        ys_o[...] = y


def _combine(cnt, ls, gd, ys, lp, rt, x1, modp, mods, fg, *, npt, tpb, nb):
    d = x1.shape[1]
    nt = npt + 1
    row = lambda i, *_: (i, 0)
    return pl.pallas_call(
        functools.partial(_combine_kernel, npt=npt),
        out_shape=[jax.ShapeDtypeStruct((npt * TM, d), F32), jax.ShapeDtypeStruct((TM, d), F32)],
        grid_spec=pltpu.PrefetchScalarGridSpec(
            num_scalar_prefetch=3,
            grid=(nt,),
            in_specs=[pl.BlockSpec(memory_space=pl.ANY),
                      pl.BlockSpec((TM, LANES), row), pl.BlockSpec((TM, LANES), row), pl.BlockSpec((TM, d), row),
                      pl.BlockSpec((None, 1, d), lambda i, *_: (jnp.minimum(i // tpb, nb - 1), 0, 5)),
                      pl.BlockSpec((TM, d), lambda i, *_: (0, 5)),
                      pl.BlockSpec((1, d), lambda i, *_: (0, 0))],
            out_specs=[pl.BlockSpec((TM, d), lambda i, *_: (jnp.minimum(i, npt - 1), 0)),
                       pl.BlockSpec((TM, d), lambda i, *_: (0, 0))],
            scratch_shapes=[pltpu.VMEM((2, SLOTS, d), BF16), pltpu.SemaphoreType.DMA((2,))]),
        compiler_params=_cparams("arbitrary"),
    )(cnt, ls, gd, ys, lp, rt, x1, modp, mods, fg)


TAB_LO = 64


def _rotation_tables(t):
    inv_r = jnp.repeat(1.0 / (ROPE_THETA ** jnp.linspace(0.0, 1.0, RET_DK // 2, dtype=F32)), 2)
    sign_r = jnp.where(jnp.arange(RET_DK) % 2 == 0, -1.0, 1.0).astype(F32)
    inv_w = jnp.tile(ROPE_THETA ** (-jnp.arange(0, SWA_HD, 2, dtype=F32) / SWA_HD), LANES // (SWA_HD // 2))
    sign_w = jnp.where(jnp.arange(LANES) % SWA_HD < SWA_HD // 2, -1.0, 1.0).astype(F32)
    hi = (jnp.arange(t // TAB_LO, dtype=I32) * TAB_LO).astype(F32)[:, None]
    lo = jnp.arange(TAB_LO, dtype=I32).astype(F32)[:, None]
    past = jnp.full((1, 1), PAST_LEN, F32)

    def pair(inv, sign):
        a, b = hi * inv[None, :], lo * inv[None, :]
        ca, sa, cb, sb = jnp.cos(a)[:, None], jnp.sin(a)[:, None], jnp.cos(b)[None], jnp.sin(b)[None]
        cos = (ca * cb - sa * sb).reshape(t, LANES)
        sin = (sa * cb + ca * sb).reshape(t, LANES) * sign[None, :]
        ang = past * inv[None, :]
        return (cos, sin), (jnp.cos(ang), jnp.sin(ang) * sign[None, :])

    (pr, sr_), (pw, sw_) = pair(inv_r, sign_r), pair(inv_w, sign_w)
    return pr + pw, sr_ + sw_


def kernel(x_prompt, x_sample, c_prompt, c_sample, state_ret, cache_swa_k, cache_swa_v, w_ada, b_ada, norm1_g, norm2_g, w_in, w_up_ret, w_up_swa, w_o, sink, w_rg, b_rg, w_re, b_re, w1, w3, w2, final_g):
    nb, t, d = x_prompt.shape
    ns, dec_seq, _ = x_sample.shape
    depth = w_ada.shape[0]
    assert depth == 1 and dec_seq == 1, "single layer, one new token per sequence"
    assert t % TT == 0 and ns <= TM and ns % 16 == 0 and d % LANES == 0
    assert t % RET_BLOCK == 0 and t % (SWA_STEP_BLOCKS * WINDOW) == 0 and t % TAB_LO == 0
    assert N_GROUPS + N_EXPERTS <= LANES
    w = cache_swa_k.shape[2]
    tpb = t // TM
    npt = nb * tpb
    spb = t // TT
    nps = nb * spb
    nt = (nps + 1) * (TT // TM)
    np_rows = nb * t
    n_tok = np_rows + ns
    maxt = -(-(2 * n_tok + nt * N_EXPERTS * RUN + N_EXPERTS * (TE - 1)) // TE)

    xp = x_prompt.reshape(np_rows, d)
    xs_pad = jnp.pad(x_sample.reshape(ns, d), ((0, TT - ns), (0, 0)))

    c_all = jnp.concatenate([jnp.pad(c_sample, ((0, TT - ns), (0, 0))),
                             jnp.pad(c_prompt, ((0, SUBLANES - nb % SUBLANES), (0, 0)))])
    mods = _modulation(c_all, w_ada[0], b_ada[0])
    modp = mods[TT:TT + nb].reshape(nb, 1, 6 * d)

    tabs_p, tabs_s = _rotation_tables(t)
    rq, rk, rv, rg, sq, sk, sv, siga, sigb = _inproj(
        xp, xs_pad, modp, mods, norm1_g, w_in[0].astype(BF16), tabs_p, tabs_s, nps=nps, spb=spb, nb=nb)

    dm, qd, kd, cd, gamma = _ret_tables(RET_BLOCK)
    gated_p, st_p = _retention_prompt(rq, rk, rv, rg, (dm, qd, kd, cd), nb=nb, t=t)
    gated_s, st_s = _retention_sample(rq, rk, rv, rg, state_ret[0], gamma, row0=np_rows, ns=ns)
    oswa_p = _swa_prompt(sq, sk, sv, sink[0], nb=nb, t=t)
    to_t = lambda c: jnp.transpose(c[0], (0, 2, 3, 1)).reshape(ns, _SWA_KW, w)
    from_t = lambda c: jnp.transpose(c.reshape(ns, SWA_KV_HEADS, SWA_HD, w), (0, 3, 1, 2))[None]
    oswa_s, ks_new, vs_new = _swa_sample(sq, sk, sv, to_t(cache_swa_k), to_t(cache_swa_v), sink[0],
                                         row0=np_rows, ns=ns)
    gated_s = jnp.pad(gated_s, ((0, TT - ns), (0, 0)))
    oswa_s = jnp.pad(oswa_s, ((0, TT - ns), (0, 0)))

    wr = jnp.pad(jnp.concatenate([w_rg[0], w_re[0]], axis=1), ((0, 0), (0, LANES - N_GROUPS - N_EXPERTS)))
    br = jnp.pad(jnp.concatenate([b_rg[0], b_re[0]]), (0, LANES - N_GROUPS - N_EXPERTS)).reshape(1, LANES)
    x1, h2, rt, lp, cnt, ls, gb = _outproj(
        gated_p, gated_s, oswa_p, oswa_s, siga, sigb, xp, xs_pad, modp, mods, norm2_g,
        w_up_ret[0].astype(BF16), w_up_swa[0].astype(BF16), w_o[0].astype(BF16), wr, br,
        nps=nps, spb=spb, nb=nb, ns=ns)
    cnt = cnt[:, 0, :N_EXPERTS]
    ls = ls[:, 0, :N_EXPERTS]
    gb = gb[:, 0, :N_EXPERTS]
    seg = jnp.sum(cnt, axis=0)
    tiles = (seg + TE - 1) // TE
    tile_end = jnp.cumsum(tiles)
    row_start = (tile_end - tiles) * TE
    gd = (gb + row_start[None, :]).reshape(-1)
    n_used = tile_end[-1:]
    jj = jnp.minimum(jnp.arange(maxt, dtype=I32), n_used[0] - 1)
    te = jnp.minimum(jnp.sum((tile_end[None, :] <= jj[:, None]).astype(I32), axis=1), N_EXPERTS - 1)
    cnt = cnt.reshape(-1)
    ls = ls.reshape(-1)
    n_used = n_used.astype(I32)
    xs = _dispatch(cnt, ls, gd, row_start + seg, tiles * TE - seg, n_used, h2, lp, nt=nt, maxt=maxt)
    has = tiles > 0
    eidx = jnp.arange(N_EXPERTS, dtype=I32)
    later = jnp.where(has[None, :] & (eidx[None, :] > eidx[:, None]), eidx[None, :], N_EXPERTS)
    nxt_e = jnp.min(later, axis=1)
    nxt_e = jnp.where(nxt_e < N_EXPERTS, nxt_e, -1).astype(I32)
    par_e = ((jnp.cumsum(has.astype(I32)) - 1) % 2).astype(I32)
    ys = _experts(te, n_used, (tile_end - tiles).astype(I32), nxt_e, par_e, xs, w1[0], w3[0], w2[0], maxt=maxt)
    y_p, y_s = _combine(cnt, ls, gd, ys, lp, rt, x1, modp, mods, final_g.reshape(1, d), npt=npt, tpb=tpb, nb=nb)

    y_prompt = y_p.reshape(nb, t, d)
    y_sample = y_s[:ns].reshape(ns, 1, d)
    wk = min(WINDOW, t)
    last = lambda a: jnp.stack([a[(b + 1) * t - wk:(b + 1) * t] for b in range(nb)]).reshape(
        nb, wk, SWA_KV_HEADS, SWA_HD)
    skp, svp = last(sk), last(sv)
    return (y_prompt, y_sample, st_p[None], st_s[None], skp[None], svp[None], from_t(ks_new), from_t(vs_new))
```

```python
import functools

import jax
import jax.numpy as jnp
from jax import lax
from jax.experimental import pallas as pl
from jax.experimental.pallas import tpu as pltpu

F32 = jnp.float32
BF16 = jnp.bfloat16
I32 = jnp.int32

PAST_LEN = 8192
RET_HEADS = 4
RET_DK = 128
RET_DV = 128
RET_CHUNK = 128
SWA_HEADS = 8
SWA_KV_HEADS = 2
SWA_HD = 64
WINDOW = 128
ROPE_THETA = 10000.0
N_GROUPS = 4
EXPERTS_PER_GROUP = 8
N_EXPERTS = N_GROUPS * EXPERTS_PER_GROUP
D_EXPERT = 256
NORM_EPS = 1e-6

LANES = 128
SUBLANES = 8
TM = 256
TT = 2 * TM
RUN = 16
SLOTS = 2 * TM + N_EXPERTS * RUN
TE = 512
TE_SUB = 256
VMEM_LIMIT = 56 * 1024 * 1024

_RET_W = RET_HEADS * RET_DK
_SWA_QW = SWA_HEADS * SWA_HD
_SWA_KW = SWA_KV_HEADS * SWA_HD


def _cparams(*sem):
    return pltpu.CompilerParams(dimension_semantics=sem, vmem_limit_bytes=VMEM_LIMIT)


def _sigmoid(x):
    return 1.0 / (1.0 + jnp.exp(-x))


def _silu(x):
    return x * _sigmoid(x)


def _bdot(a, b):
    return jnp.dot(a.astype(BF16), b.astype(BF16), preferred_element_type=F32)


def _bdot_nt(a, b):
    return lax.dot_general(a.astype(BF16), b.astype(BF16), (((1,), (1,)), ((), ())), preferred_element_type=F32)


def _mod_kernel(c_ref, w_ref, b_ref, o_ref):
    o_ref[...] = _bdot(_silu(c_ref[...]), w_ref[...]) + b_ref[...]


def _modulation(c_all, w_ada, b_ada):
    rows, d = c_all.shape
    n = w_ada.shape[1]
    return pl.pallas_call(
        _mod_kernel,
        out_shape=jax.ShapeDtypeStruct((rows, n), F32),
        grid=(n // d,),
        in_specs=[pl.BlockSpec((rows, d), lambda j: (0, 0)),
                  pl.BlockSpec((d, d), lambda j: (0, j)),
                  pl.BlockSpec((1, d), lambda j: (0, j))],
        out_specs=pl.BlockSpec((rows, d), lambda j: (0, j)),
        compiler_params=_cparams("arbitrary"),
    )(c_all, w_ada, b_ada.reshape(1, n))


def _rms(x, g):
    return x * lax.rsqrt(jnp.mean(x * x, axis=-1, keepdims=True) + NORM_EPS) * g


def _pair_rotate(z, cos, sin_signed):
    n = z.shape[-1]
    lane = lax.broadcasted_iota(I32, z.shape, 1)
    partner = jnp.where((lane & 1) == 0, pltpu.roll(z, n - 1, 1), pltpu.roll(z, 1, 1))
    reps = n // LANES
    cos = jnp.concatenate([cos] * reps, axis=1) if reps > 1 else cos
    sin_signed = jnp.concatenate([sin_signed] * reps, axis=1) if reps > 1 else sin_signed
    return z * cos + partner * sin_signed


def _half_rotate(z, cos, sin_signed):
    n = z.shape[-1]
    half = SWA_HD // 2
    lane = lax.broadcasted_iota(I32, z.shape, 1)
    partner = jnp.where((lane & (SWA_HD - 1)) < half, pltpu.roll(z, n - half, 1), pltpu.roll(z, half, 1))
    reps = n // LANES
    cos = jnp.concatenate([cos] * reps, axis=1) if reps > 1 else cos
    sin_signed = jnp.concatenate([sin_signed] * reps, axis=1) if reps > 1 else sin_signed
    return z * cos + partner * sin_signed


def _inproj_kernel(xp_ref, xs_ref, shp_ref, scp_ref, shs_ref, scs_ref, n1_ref, w_ref,
                   crp_ref, srp_ref, cwp_ref, swp_ref, crs_ref, srs_ref, cws_ref, sws_ref,
                   rq_o, rk_o, rv_o, rg_o, sq_o, sk_o, sv_o, za_o, zb_o, *, nps):
    is_s = pl.program_id(0) >= nps
    d = xp_ref.shape[-1]
    subs = []
    for r in range(0, TT, TM):
        rows = slice(r, r + TM)
        x = jnp.where(is_s, xs_ref[rows], xp_ref[rows])
        sh = jnp.where(is_s, shs_ref[rows], shp_ref[...])
        sc = jnp.where(is_s, scs_ref[rows], scp_ref[...])
        h = (_rms(x, n1_ref[...]) * (1.0 + sc) + sh).astype(BF16)
        tabs = tuple(jnp.where(is_s, s_ref[...], p_ref[rows]) for s_ref, p_ref in
                     ((crs_ref, crp_ref), (srs_ref, srp_ref), (cws_ref, cwp_ref), (sws_ref, swp_ref)))
        subs.append((rows, h, tabs))

    def seg(h, a, b):
        return jnp.dot(h, w_ref[:, a:b], preferred_element_type=F32)

    o = 0
    for rows, h, (cr, sr, cw, sw) in subs:
        rq_o[rows] = _pair_rotate(seg(h, o, o + _RET_W), cr, sr).astype(BF16)
    o += _RET_W
    for rows, h, (cr, sr, cw, sw) in subs:
        rk_o[rows] = (_pair_rotate(seg(h, o, o + _RET_W), cr, sr) * (RET_DK ** -0.5)).astype(BF16)
    o += _RET_W
    for rows, h, _ in subs:
        rv_o[rows] = seg(h, o, o + _RET_W).astype(BF16)
    o += _RET_W
    for rows, h, _ in subs:
        rg_o[rows] = _silu(seg(h, o, o + _RET_W)).astype(BF16)
    o += _RET_W
    for rows, h, (cr, sr, cw, sw) in subs:
        sq_o[rows] = (_half_rotate(seg(h, o, o + _SWA_QW), cw, sw) * (SWA_HD ** -0.5)).astype(BF16)
    o += _SWA_QW
    for rows, h, (cr, sr, cw, sw) in subs:
        zkv = seg(h, o, o + 2 * _SWA_KW)
        sk_o[rows] = _half_rotate(zkv[:, :_SWA_KW], cw, sw)
        sv_o[rows] = zkv[:, _SWA_KW:]
    o += 2 * _SWA_KW
    for rows, h, _ in subs:
        za_o[rows] = _sigmoid(seg(h, o, o + d)).astype(BF16)
    o += d
    for rows, h, _ in subs:
        zb_o[rows] = _sigmoid(seg(h, o, o + d)).astype(BF16)


def _inproj(xp, xs_pad, modp, mods, n1, w_in_b, tabs_p, tabs_s, *, nps, spb, nb):
    d = xp.shape[1]
    nrow = (nps + 1) * TT
    n_in = w_in_b.shape[1]
    pstep = lambda i: (jnp.minimum(i, nps - 1), 0)
    pbatch = lambda col: (lambda i: (jnp.minimum(i // spb, nb - 1), 0, col))
    tab_idx = lambda i: (jnp.where(i < nps, i % spb, 0), 0)
    out_cols = [(_RET_W, BF16)] * 4 + [(_SWA_QW, BF16), (_SWA_KW, F32), (_SWA_KW, F32), (d, BF16), (d, BF16)]
    return pl.pallas_call(
        functools.partial(_inproj_kernel, nps=nps),
        out_shape=[jax.ShapeDtypeStruct((nrow, c), t) for c, t in out_cols],
        grid=(nps + 1,),
        in_specs=[pl.BlockSpec((TT, d), pstep),
                  pl.BlockSpec((TT, d), lambda i: (0, 0)),
                  pl.BlockSpec((None, 1, d), pbatch(0)),
                  pl.BlockSpec((None, 1, d), pbatch(1)),
                  pl.BlockSpec((TT, d), lambda i: (0, 0)),
                  pl.BlockSpec((TT, d), lambda i: (0, 1)),
                  pl.BlockSpec((1, d), lambda i: (0, 0)),
                  pl.BlockSpec((d, n_in), lambda i: (0, 0))]
                 + [pl.BlockSpec((TT, LANES), tab_idx)] * 4
                 + [pl.BlockSpec((1, LANES), lambda i: (0, 0))] * 4,
        out_specs=[pl.BlockSpec((TT, c), lambda i: (i, 0)) for c, _ in out_cols],
        compiler_params=_cparams("arbitrary"),
    )(xp, xs_pad, modp, modp, mods, mods, n1, w_in_b, *tabs_p, *tabs_s)


RET_BLOCK = 512


def _ret_kernel(q_ref, k_ref, v_ref, g_ref, dm_ref, qd_ref, kd_ref, cd_ref, o_ref, st_ref, s_scr, *, nsteps):
    step = pl.program_id(1)

    @pl.when(step == 0)
    def _():
        s_scr[...] = jnp.zeros_like(s_scr)

    for h in range(RET_HEADS):
        sl = slice(h * RET_DK, (h + 1) * RET_DK)
        state = s_scr[h]
        q, k, v = q_ref[:, sl], k_ref[:, sl], v_ref[:, sl]
        att = _bdot_nt(q, k) * dm_ref[h]
        o = _bdot(att, v) + _bdot(q.astype(F32) * qd_ref[h], state)
        kd = (k.astype(F32) * kd_ref[h]).astype(BF16)
        kv = lax.dot_general(kd, v, (((0,), (0,)), ((), ())), preferred_element_type=F32)
        s_scr[h] = cd_ref[h] * state + kv
        o = o * lax.rsqrt(jnp.mean(o * o, axis=-1, keepdims=True) + NORM_EPS)
        o_ref[:, sl] = (o * g_ref[:, sl].astype(F32)).astype(BF16)

    @pl.when(step == nsteps - 1)
    def _():
        st_ref[...] = s_scr[...]


def _ret_tables(chunk):
    ld = jnp.log(1.0 - 2.0 ** (-5.0 - jnp.arange(RET_HEADS, dtype=F32)))
    idx = jnp.arange(chunk, dtype=F32)
    diff = idx[:, None] - idx[None, :]
    causal = diff >= 0
    dmask = jnp.where(causal[None], jnp.exp(ld[:, None, None] * jnp.where(causal, diff, 0.0)[None]), 0.0)
    k_dec = jnp.exp(ld[None, :] * (chunk - 1.0 - idx)[:, None])
    q_dec = jnp.exp(ld[None, :] * (idx + 1.0)[:, None])
    chunk_decay = jnp.exp(ld * chunk)
    bc = lambda t: jnp.broadcast_to(t.T[:, :, None], (RET_HEADS, chunk, RET_DV))
    cd = jnp.broadcast_to(chunk_decay[:, None, None], (RET_HEADS, 1, RET_DV))
    return dmask, bc(q_dec), bc(k_dec), cd, jnp.exp(ld)


def _retention_prompt(rq, rk, rv, rg, tabs, *, nb, t):
    rows = RET_BLOCK
    nsteps = t // rows
    dm, qd, kd, cd = tabs
    blk = lambda b, c: (b * nsteps + c, 0)
    full3 = lambda b, c: (0, 0, 0)
    return pl.pallas_call(
        functools.partial(_ret_kernel, nsteps=nsteps),
        out_shape=[jax.ShapeDtypeStruct((nb * t, _RET_W), BF16),
                   jax.ShapeDtypeStruct((nb, RET_HEADS, RET_DK, RET_DV), F32)],
        grid=(nb, nsteps),
        in_specs=[pl.BlockSpec((rows, _RET_W), blk)] * 4
                 + [pl.BlockSpec((RET_HEADS, rows, rows), full3)]
                 + [pl.BlockSpec((RET_HEADS, rows, RET_DV), full3)] * 2
                 + [pl.BlockSpec((RET_HEADS, 1, RET_DV), full3)],
        out_specs=[pl.BlockSpec((rows, _RET_W), blk),
                   pl.BlockSpec((None, RET_HEADS, RET_DK, RET_DV), lambda b, c: (b, 0, 0, 0))],
        scratch_shapes=[pltpu.VMEM((RET_HEADS, RET_DK, RET_DV), F32)],
        compiler_params=_cparams("arbitrary", "arbitrary"),
    )(rq, rk, rv, rg, dm, qd, kd, cd)


def _ret_sample_kernel(gam_ref, q_ref, k_ref, v_ref, g_ref, s0_ref, o_ref, st_ref, *, sb):
    gamma = gam_ref[pl.program_id(1)]
    q = q_ref[...].astype(F32)
    k = k_ref[...].astype(F32)
    v = v_ref[...].astype(F32)
    rows = sb * RET_DK
    s2 = s0_ref[...].reshape(rows, RET_DV)
    col_b = lax.broadcasted_iota(I32, (sb, rows), 1) // RET_DK
    row_b = lax.broadcasted_iota(I32, (sb, rows), 0)
    qexp = jnp.where(col_b == row_b, jnp.concatenate([q * gamma] * sb, axis=1), 0.0)
    o = jnp.sum(q * k, axis=-1, keepdims=True) * v + _bdot(qexp, s2)
    o = o * lax.rsqrt(jnp.mean(o * o, axis=-1, keepdims=True) + NORM_EPS)
    o_ref[...] = (o * g_ref[...].astype(F32)).astype(BF16)
    rep = (lax.broadcasted_iota(I32, (rows, sb), 0) // RET_DK == lax.broadcasted_iota(I32, (rows, sb), 1))
    rep = rep.astype(BF16)
    krep = _bdot(rep, k)
    vrep = _bdot(rep, v)
    eye = (lax.broadcasted_iota(I32, (rows, RET_DK), 0) % RET_DK == lax.broadcasted_iota(I32, (rows, RET_DK), 1))
    kcol = jnp.sum(jnp.where(eye, krep, 0.0), axis=-1, keepdims=True)
    st_ref[...] = (gamma * s2 + kcol * vrep).reshape(sb, RET_DK, RET_DV)


def _retention_sample(rq, rk, rv, rg, s0, gamma, *, row0, ns):
    sb = min(64, ns)
    base = row0 // sb
    blk = lambda i, h: (base + i, h)
    sblk = lambda i, h: (i, h, 0, 0)
    return pl.pallas_call(
        functools.partial(_ret_sample_kernel, sb=sb),
        out_shape=[jax.ShapeDtypeStruct((ns, _RET_W), BF16),
                   jax.ShapeDtypeStruct(s0.shape, F32)],
        grid=(ns // sb, RET_HEADS),
        in_specs=[pl.BlockSpec(memory_space=pltpu.SMEM)]
                 + [pl.BlockSpec((sb, RET_DK), blk)] * 4
                 + [pl.BlockSpec((sb, None, RET_DK, RET_DV), sblk)],
        out_specs=[pl.BlockSpec((sb, RET_DV), lambda i, h: (i, h)),
                   pl.BlockSpec((sb, None, RET_DK, RET_DV), sblk)],
        compiler_params=_cparams("arbitrary", "arbitrary"),
    )(gamma, rq, rk, rv, rg, s0)


def _sink_softmax(s, mask, sink):
    if mask is not None:
        s = jnp.where(mask, s, -jnp.inf)
    m = jnp.maximum(jnp.max(s, axis=-1, keepdims=True), sink)
    p = jnp.exp(s - m)
    return p / (jnp.sum(p, axis=-1, keepdims=True) + jnp.exp(sink - m))


def _split_kv_heads(x):
    lo = lax.broadcasted_iota(I32, x.shape, 1) < SWA_HD
    h0_lo = jnp.where(lo, x, 0.0)
    h1_hi = jnp.where(lo, 0.0, x)
    return ((h0_lo, pltpu.roll(h0_lo, SWA_HD, 1)), (pltpu.roll(h1_hi, SWA_HD, 1), h1_hi))


SWA_STEP_BLOCKS = 4


def _swa_kernel(sink_ref, q_ref, kc_ref, kp_ref, vc_ref, vp_ref, o_ref):
    n = pl.program_id(1)
    c = WINDOW
    kk = jnp.concatenate([kp_ref[...], kc_ref[...]], axis=0)
    vv = jnp.concatenate([vp_ref[...], vc_ref[...]], axis=0)
    ks = [[a.astype(BF16) for a in pair] for pair in _split_kv_heads(kk)]
    vs = [[a.astype(BF16) for a in pair] for pair in _split_kv_heads(vv)]
    qi = lax.broadcasted_iota(I32, (2 * c, c), 0) % c
    ki = lax.broadcasted_iota(I32, (2 * c, c), 1)
    from_prev = ki > qi
    top = lax.broadcasted_iota(I32, (2 * c, 1), 0) < c
    units = [(s, kvh) for s in range(SWA_STEP_BLOCKS) for kvh in range(SWA_KV_HEADS)]
    scores = []
    for s, kvh in units:
        rows, keys = slice(s * c, (s + 1) * c), slice(s * c, (s + 2) * c)
        q2 = jnp.concatenate([q_ref[rows, 2 * kvh * LANES:(2 * kvh + 1) * LANES],
                              q_ref[rows, (2 * kvh + 1) * LANES:(2 * kvh + 2) * LANES]], axis=0)
        kcat = jnp.concatenate([ks[kvh][0][keys], ks[kvh][1][keys]], axis=0)
        scores.append(lax.dot_general(q2, kcat, (((1,), (1,)), ((), ())), preferred_element_type=F32))
    probs = []
    for (s, kvh), sc in zip(units, scores):
        ps = []
        for half in range(2):
            sink = jnp.where(top, sink_ref[4 * kvh + half], sink_ref[4 * kvh + 2 + half])
            s_prev = sc[:, half * 2 * c:half * 2 * c + c]
            s_own = sc[:, half * 2 * c + c:(half + 1) * 2 * c]
            if s == 0:
                s_prev = jnp.where(n > 0, s_prev, -jnp.inf)
            p = _sink_softmax(jnp.where(from_prev, s_prev, s_own), None, sink)
            ps += [jnp.where(from_prev, p, 0.0).astype(BF16), jnp.where(from_prev, 0.0, p).astype(BF16)]
        probs.append(jnp.concatenate(ps, axis=1))
    for (s, kvh), p in zip(units, probs):
        rows, keys = slice(s * c, (s + 1) * c), slice(s * c, (s + 2) * c)
        vcat = jnp.concatenate([vs[kvh][0][keys], vs[kvh][1][keys]], axis=0)
        o = jnp.dot(p, vcat, preferred_element_type=F32)
        o_ref[rows, 2 * kvh * LANES:(2 * kvh + 1) * LANES] = o[:c].astype(BF16)
        o_ref[rows, (2 * kvh + 1) * LANES:(2 * kvh + 2) * LANES] = o[c:].astype(BF16)


def _swa_prompt(sq, sk, sv, sink, *, nb, t):
    rows = SWA_STEP_BLOCKS * WINDOW
    nsteps = t // rows
    nblk = t // WINDOW
    cur = lambda b, n: (b * nsteps + n, 0)
    prev = lambda b, n: (b * nblk + jnp.maximum(n * SWA_STEP_BLOCKS - 1, 0), 0)
    return pl.pallas_call(
        _swa_kernel,
        out_shape=jax.ShapeDtypeStruct((nb * t, _SWA_QW), BF16),
        grid=(nb, nsteps),
        in_specs=[pl.BlockSpec(memory_space=pltpu.SMEM),
                  pl.BlockSpec((rows, _SWA_QW), cur),
                  pl.BlockSpec((rows, _SWA_KW), cur),
                  pl.BlockSpec((WINDOW, _SWA_KW), prev),
                  pl.BlockSpec((rows, _SWA_KW), cur),
                  pl.BlockSpec((WINDOW, _SWA_KW), prev)],
        out_specs=pl.BlockSpec((rows, _SWA_QW), cur),
        compiler_params=_cparams("arbitrary", "arbitrary"),
    )(sink, sq, sk, sk, sv, sv)


def _swa_sample_kernel(sink_ref, q_ref, kn_ref, vn_ref, kc_ref, vc_ref, o_ref, ko_ref, vo_ref, *, sb, w):
    pad = jnp.zeros((LANES - sb, _SWA_KW), F32)
    knt = jnp.concatenate([kn_ref[...], pad], axis=0).T
    vnt = jnp.concatenate([vn_ref[...], pad], axis=0).T
    kall = jnp.concatenate([kc_ref[b] for b in range(sb)] + [knt], axis=1)
    vall = jnp.concatenate([vc_ref[b] for b in range(sb)] + [vnt], axis=1)
    ncol = sb * w + LANES
    lo = lax.broadcasted_iota(I32, (sb, LANES), 1) < SWA_HD
    group = SWA_HEADS // SWA_KV_HEADS
    pieces = []
    for h in range(SWA_HEADS):
        slab = q_ref[:, (h // 2) * LANES:(h // 2 + 1) * LANES].astype(F32)
        mine = jnp.where(lo, slab, 0.0) if h % 2 == 0 else jnp.where(lo, 0.0, slab)
        pieces.append(mine if (h % 2) == (h // group) else pltpu.roll(mine, SWA_HD, 1))
    qrows = jnp.concatenate(pieces, axis=0)
    nrow = SWA_HEADS * sb
    s = _bdot(qrows, kall)
    rb = lax.broadcasted_iota(I32, (nrow, ncol), 0) % sb
    ci = lax.broadcasted_iota(I32, (nrow, ncol), 1)
    in_cache = (ci < sb * w) & (ci // w == rb) & ((w - ci % w) < WINDOW)
    mask = in_cache | (ci == sb * w + rb)
    sink_col = jnp.concatenate([jnp.full((sb, 1), sink_ref[h], F32) for h in range(SWA_HEADS)], axis=0)
    p = _sink_softmax(s, mask, sink_col)
    o = _bdot_nt(p, vall)
    for j in range(SWA_HEADS // 2):
        acc = jnp.zeros((sb, LANES), F32)
        for half in range(2):
            h = 2 * j + half
            oh = o[h * sb:(h + 1) * sb]
            own = jnp.where(lo, oh, 0.0) if h // group == 0 else jnp.where(lo, 0.0, oh)
            acc = acc + (own if (h // group) == half else pltpu.roll(own, SWA_HD, 1))
        o_ref[:, j * LANES:(j + 1) * LANES] = acc.astype(BF16)
    newest = lax.broadcasted_iota(I32, (_SWA_KW, w), 1) == w - 1
    for b in range(sb):
        ko_ref[b] = jnp.where(newest, knt[:, b:b + 1], pltpu.roll(kc_ref[b], w - 1, 1))
        vo_ref[b] = jnp.where(newest, vnt[:, b:b + 1], pltpu.roll(vc_ref[b], w - 1, 1))


def _swa_sample(sq, sk, sv, cache_kt, cache_vt, sink, *, row0, ns):
    sb = min(16, ns)
    w = cache_kt.shape[2]
    base = row0 // sb
    blk = lambda i: (base + i, 0)
    cblk = lambda i: (i, 0, 0)
    cspec = pl.BlockSpec((sb, _SWA_KW, w), cblk)
    return pl.pallas_call(
        functools.partial(_swa_sample_kernel, sb=sb, w=w),
        out_shape=[jax.ShapeDtypeStruct((ns, _SWA_QW), BF16),
                   jax.ShapeDtypeStruct(cache_kt.shape, F32), jax.ShapeDtypeStruct(cache_vt.shape, F32)],
        grid=(ns // sb,),
        in_specs=[pl.BlockSpec(memory_space=pltpu.SMEM),
                  pl.BlockSpec((sb, _SWA_QW), blk),
                  pl.BlockSpec((sb, _SWA_KW), blk),
                  pl.BlockSpec((sb, _SWA_KW), blk),
                  cspec, cspec],
        out_specs=[pl.BlockSpec((sb, _SWA_QW), lambda i: (i, 0)), cspec, cspec],
        compiler_params=_cparams("arbitrary"),
    )(sink, sq, sk, sv, cache_kt, cache_vt)


def _route(logits):
    lane = lax.broadcasted_iota(I32, logits.shape, 1)
    big = jnp.int32(1 << 20)
    neg = -jnp.inf

    def top(mask):
        v = jnp.max(jnp.where(mask, logits, neg), axis=-1, keepdims=True)
        i = jnp.min(jnp.where(mask & (logits == v), lane, big), axis=-1, keepdims=True)
        return v, i

    gmask = lane < N_GROUPS
    gmax, gsel = top(gmask)
    p_group = 1.0 / jnp.sum(jnp.where(gmask, jnp.exp(logits - gmax), 0.0), axis=-1, keepdims=True)
    first = N_GROUPS + gsel * EXPERTS_PER_GROUP
    emask = (lane >= first) & (lane < first + EXPERTS_PER_GROUP)
    v1, i1 = top(emask)
    v2, i2 = top(emask & (lane != i1))
    t = jnp.exp(v2 - v1)
    w1 = p_group / (1.0 + t)
    return i1 - N_GROUPS, i2 - N_GROUPS, w1, w1 * t


def _plan_tile(e1, e2, valid, carry):
    lane = lax.broadcasted_iota(I32, (TM, LANES), 1)
    oh1 = ((lane == e1) & valid).astype(F32)
    oh2 = ((lane == e2) & valid).astype(F32)
    oh = oh1 + oh2
    tri = (lax.broadcasted_iota(I32, (TM, TM), 0) > lax.broadcasted_iota(I32, (TM, TM), 1)).astype(BF16)
    before = _bdot(tri, oh)
    cnt = jnp.sum(oh, axis=0, keepdims=True)
    units = jnp.maximum(jnp.floor((cnt + (RUN - 1)) * (1.0 / RUN)), 1.0)
    upper = (lax.broadcasted_iota(I32, (LANES, LANES), 0) < lax.broadcasted_iota(I32, (LANES, LANES), 1))
    lstart = RUN * _bdot(jnp.broadcast_to(units, (SUBLANES, LANES)), upper.astype(BF16))[0:1]
    slot = lstart + before
    lp1 = jnp.sum(oh1 * slot, axis=-1, keepdims=True)
    lp2 = jnp.sum(oh2 * slot, axis=-1, keepdims=True)
    vcol = valid[:, 0:1]
    lp = jnp.where(lane == 0, jnp.where(vcol, lp1, -1.0), jnp.where(lane == 1, jnp.where(vcol, lp2, -1.0), 0.0))
    base = carry[...]
    carry[...] = base + RUN * units
    return lp, (RUN * units).astype(I32), lstart.astype(I32), base.astype(I32)


def _outproj_kernel(gtp_ref, gts_ref, osp_ref, oss_ref, sa_ref, sb_ref, xp_ref, xs_ref, g1p_ref, shp_ref, scp_ref,
                    g1s_ref, shs_ref, scs_ref, n2_ref, wur_ref, wus_ref, wo_ref, wrh_ref, wrl_ref, br_ref,
                    x1_o, h2_o, rt_o, lp_o, cnt_o, ls_o, gb_o, carry, *, nps, ns):
    i = pl.program_id(0)
    is_s = i >= nps

    @pl.when(i == 0)
    def _():
        carry[...] = jnp.zeros_like(carry)

    subs = [slice(r, r + TM) for r in range(0, TT, TM)]
    merged = []
    for rows in subs:
        gated = jnp.where(is_s, gts_ref[rows], gtp_ref[rows])
        oswa = jnp.where(is_s, oss_ref[rows], osp_ref[rows])
        y_ret = jnp.dot(gated, wur_ref[...], preferred_element_type=F32)
        y_swa = jnp.dot(oswa, wus_ref[...], preferred_element_type=F32)
        merged.append((sa_ref[rows].astype(F32) * y_ret + sb_ref[rows].astype(F32) * y_swa).astype(BF16))
    hs = []
    for rows, m in zip(subs, merged):
        x = jnp.where(is_s, xs_ref[rows], xp_ref[rows])
        g1 = jnp.where(is_s, g1s_ref[rows], g1p_ref[...])
        sh = jnp.where(is_s, shs_ref[rows], shp_ref[...])
        sc = jnp.where(is_s, scs_ref[rows], scp_ref[...])
        x1 = x + g1 * jnp.dot(m, wo_ref[...], preferred_element_type=F32)
        x1_o[rows] = x1
        h2 = _rms(x1, n2_ref[...]) * (1.0 + sc) + sh
        hi = h2.astype(BF16)
        h2_o[rows] = hi
        hs.append((hi, (h2 - hi.astype(F32)).astype(BF16)))
    routed = []
    for rows, (hi, lo) in zip(subs, hs):
        wrh = wrh_ref[...]
        logits = (jnp.dot(hi, wrh, preferred_element_type=F32) + jnp.dot(lo, wrh, preferred_element_type=F32)
                  + jnp.dot(hi, wrl_ref[...], preferred_element_type=F32) + br_ref[...])
        e1, e2, w1, w2 = _route(logits)
        lane = lax.broadcasted_iota(I32, logits.shape, 1)
        rt_o[rows] = jnp.where(lane == 2, w1, jnp.where(lane == 3, w2, 0.0))
        routed.append((e1, e2))
    for sub, (e1, e2) in enumerate(routed):
        row = lax.broadcasted_iota(I32, (TM, LANES), 0) + sub * TM
        valid = jnp.logical_not(is_s) | (row < ns)
        lp_o[sub * TM:(sub + 1) * TM], cnt_o[sub], ls_o[sub], gb_o[sub] = _plan_tile(e1, e2, valid, carry)


def _outproj(gated_p, gated_s, oswa_p, oswa_s, siga, sigb, xp, xs_pad, modp, mods, n2, wur, wus, wo, wr, br,
             *, nps, spb, nb, ns):
    d = xp.shape[1]
    sub = TT // TM
    nrow = (nps + 1) * TT
    row = lambda i: (i, 0)
    pstep = lambda i: (jnp.minimum(i, nps - 1), 0)
    pbatch = lambda col: (lambda i: (jnp.minimum(i // spb, nb - 1), 0, col))
    scol = lambda col: (lambda i: (0, col))
    const = lambda i: (0, 0)
    wr_hi = wr.astype(BF16)
    wr_lo = (wr - wr_hi.astype(F32)).astype(BF16)
    meta = jax.ShapeDtypeStruct(((nps + 1) * sub, 1, LANES), I32)
    mspec = pl.BlockSpec((sub, 1, LANES), lambda i: (i, 0, 0))
    return pl.pallas_call(
        functools.partial(_outproj_kernel, nps=nps, ns=ns),
        out_shape=[jax.ShapeDtypeStruct((nrow, d), F32),
                   jax.ShapeDtypeStruct((nrow, d), BF16),
                   jax.ShapeDtypeStruct((nrow, LANES), F32),
                   jax.ShapeDtypeStruct((nrow, LANES), F32), meta, meta, meta],
        grid=(nps + 1,),
        in_specs=[pl.BlockSpec((TT, _RET_W), pstep), pl.BlockSpec((TT, _RET_W), const),
                  pl.BlockSpec((TT, _SWA_QW), pstep), pl.BlockSpec((TT, _SWA_QW), const),
                  pl.BlockSpec((TT, d), row), pl.BlockSpec((TT, d), row),
                  pl.BlockSpec((TT, d), pstep), pl.BlockSpec((TT, d), const),
                  pl.BlockSpec((None, 1, d), pbatch(2)), pl.BlockSpec((None, 1, d), pbatch(3)),
                  pl.BlockSpec((None, 1, d), pbatch(4)),
                  pl.BlockSpec((TT, d), scol(2)), pl.BlockSpec((TT, d), scol(3)), pl.BlockSpec((TT, d), scol(4)),
                  pl.BlockSpec((1, d), const),
                  pl.BlockSpec(wur.shape, const), pl.BlockSpec(wus.shape, const), pl.BlockSpec(wo.shape, const),
                  pl.BlockSpec(wr.shape, const), pl.BlockSpec(wr.shape, const), pl.BlockSpec((1, LANES), const)],
        out_specs=[pl.BlockSpec((TT, d), row), pl.BlockSpec((TT, d), row), pl.BlockSpec((TT, LANES), row),
                   pl.BlockSpec((TT, LANES), row), mspec, mspec, mspec],
        scratch_shapes=[pltpu.VMEM((1, LANES), F32)],
        compiler_params=_cparams("arbitrary"),
    )(gated_p, gated_s, oswa_p, oswa_s, siga, sigb, xp, xs_pad, modp, modp, modp, mods, mods, mods, n2,
      wur, wus, wo, wr_hi, wr_lo, br)


def _aligned(v):
    return v if isinstance(v, int) else pl.multiple_of(v, RUN)


def _run_copy(src, dst, s_start, d_start, n, sem):
    s_start, d_start, n = _aligned(s_start), _aligned(d_start), _aligned(n)
    return pltpu.make_async_copy(src.at[pl.ds(s_start, n)], dst.at[pl.ds(d_start, n)], sem)


def _each_run(step, fn):
    for e in range(N_EXPERTS):
        fn(step * N_EXPERTS + e)


def _dispatch_kernel(cnt_ref, ls_ref, gd_ref, ps_ref, pn_ref, nu_ref, h_ref, lp_ref, xs_ref,
                     sorted_scr, zero_scr, sem, zsem, *, nt, maxt):
    i = pl.program_id(0)

    def pad(e):
        return _run_copy(zero_scr, xs_ref, 0, ps_ref[e], pn_ref[e], zsem)

    def tail(j):
        return _run_copy(zero_scr, xs_ref, 0, j * TE, TE, zsem)

    def each_pad(fn):
        def body(e, c):
            @pl.when(pn_ref[e] > 0)
            def _():
                fn(pad(e))
            return c
        lax.fori_loop(0, N_EXPERTS, body, 0)

    def each_tail(fn):
        def body(j, c):
            fn(tail(j))
            return c
        lax.fori_loop(nu_ref[0], maxt, body, 0)

    @pl.when(i == 0)
    def _():
        zero_scr[...] = jnp.zeros_like(zero_scr)
        each_pad(lambda cp: cp.start())
        each_tail(lambda cp: cp.start())

    lpt = lp_ref[...].T
    slot = lax.broadcasted_iota(I32, (SLOTS, TM), 0).astype(F32)
    perm = ((slot == lpt[0:1]) | (slot == lpt[1:2])).astype(BF16)
    sorted_scr[i % 2] = jnp.dot(perm, h_ref[...], preferred_element_type=F32).astype(BF16)

    def copy(step):
        return lambda k: _run_copy(sorted_scr.at[step % 2], xs_ref, ls_ref[k], gd_ref[k], cnt_ref[k], sem.at[step % 2])

    _each_run(i, lambda k: copy(i)(k).start())

    @pl.when(i > 0)
    def _():
        _each_run(i - 1, lambda k: copy(i - 1)(k).wait())

    @pl.when(i == nt - 1)
    def _():
        _each_run(i, lambda k: copy(i)(k).wait())
        each_pad(lambda cp: cp.wait())
        each_tail(lambda cp: cp.wait())


def _dispatch(cnt, ls, gd, ps, pn, nu, h2, lp, *, nt, maxt):
    d = h2.shape[1]
    return pl.pallas_call(
        functools.partial(_dispatch_kernel, nt=nt, maxt=maxt),
        out_shape=jax.ShapeDtypeStruct((maxt * TE, d), BF16),
        grid_spec=pltpu.PrefetchScalarGridSpec(
            num_scalar_prefetch=6,
            grid=(nt,),
            in_specs=[pl.BlockSpec((TM, d), lambda i, *_: (i, 0)),
                      pl.BlockSpec((TM, LANES), lambda i, *_: (i, 0))],
            out_specs=pl.BlockSpec(memory_space=pl.ANY),
            scratch_shapes=[pltpu.VMEM((2, SLOTS, d), BF16), pltpu.VMEM((TE, d), BF16),
                            pltpu.SemaphoreType.DMA((2,)), pltpu.SemaphoreType.DMA(())]),
        compiler_params=_cparams("arbitrary"),
    )(cnt, ls, gd, ps, pn, nu, h2, lp)


def _experts_kernel(te_ref, nu_ref, start_ref, nxt_ref, par_ref, x_ref, w1_hbm, w3_hbm, w2_hbm, y_ref,
                    w1f, w3f, w2f, w1b, w3b, w2b, sem):
    j = pl.program_id(0)
    e = te_ref[j]

    def fetch(ex, slot):
        return [pltpu.make_async_copy(src.at[ex], dst.at[slot], sem.at[slot, n])
                for n, (src, dst) in enumerate(((w1_hbm, w1f), (w3_hbm, w3f), (w2_hbm, w2f)))]

    @pl.when(j == 0)
    def _():
        for cp in fetch(e, par_ref[e]):
            cp.start()

    @pl.when((j == start_ref[e]) & (j < nu_ref[0]))
    def _():
        slot = par_ref[e]

        @pl.when(nxt_ref[e] >= 0)
        def _():
            for cp in fetch(nxt_ref[e], 1 - slot):
                cp.start()

        for cp in fetch(e, slot):
            cp.wait()
        w1b[...] = w1f[slot].astype(BF16)
        w3b[...] = w3f[slot].astype(BF16)
        w2b[...] = w2f[slot].astype(BF16)

    @pl.when(j < nu_ref[0])
    def _():
        subs = [slice(r, r + TE_SUB) for r in range(0, TE, TE_SUB)]
        ab = [(jnp.dot(x_ref[rows], w1b[...], preferred_element_type=F32),
               jnp.dot(x_ref[rows], w3b[...], preferred_element_type=F32)) for rows in subs]
        hid = [(_silu(a) * b).astype(BF16) for a, b in ab]
        for rows, h in zip(subs, hid):
            y_ref[rows] = jnp.dot(h, w2b[...], preferred_element_type=F32).astype(BF16)


def _experts(te, nu, start, nxt, par, xs, w1, w3, w2, *, maxt):
    d = xs.shape[1]
    f = w1.shape[2]
    used = lambda j, te, nu, *_: (jnp.minimum(j, nu[0] - 1), 0)
    hbm = pl.BlockSpec(memory_space=pl.ANY)
    return pl.pallas_call(
        _experts_kernel,
        out_shape=jax.ShapeDtypeStruct(xs.shape, BF16),
        grid_spec=pltpu.PrefetchScalarGridSpec(
            num_scalar_prefetch=5,
            grid=(maxt,),
            in_specs=[pl.BlockSpec((TE, d), used), hbm, hbm, hbm],
            out_specs=pl.BlockSpec((TE, d), used),
            scratch_shapes=[pltpu.VMEM((2, d, f), F32), pltpu.VMEM((2, d, f), F32), pltpu.VMEM((2, f, d), F32),
                            pltpu.VMEM((d, f), BF16), pltpu.VMEM((d, f), BF16), pltpu.VMEM((f, d), BF16),
                            pltpu.SemaphoreType.DMA((2, 3))]),
        input_output_aliases={5: 0},
        compiler_params=_cparams("arbitrary"),
    )(te, nu, start, nxt, par, xs, w1, w3, w2)


def _combine_kernel(cnt_ref, ls_ref, gd_ref, ys_ref, lp_ref, rt_ref, x1_ref, g2p_ref, g2s_ref, fg_ref,
                    yp_o, ys_o, ybuf, sem, *, npt):
    i = pl.program_id(0)
    nt = npt + 1

    def copy(step):
        return lambda k: _run_copy(ys_ref, ybuf.at[step % 2], gd_ref[k], ls_ref[k], cnt_ref[k], sem.at[step % 2])

    def fetch(step):
        ybuf[step % 2] = jnp.zeros(ybuf.shape[1:], BF16)
        _each_run(step, lambda k: copy(step)(k).start())

    @pl.when(i == 0)
    def _():
        fetch(0)

    @pl.when(i + 1 < nt)
    def _():
        fetch(i + 1)

    _each_run(i, lambda k: copy(i)(k).wait())

    yb = ybuf[i % 2]
    slot = lax.broadcasted_iota(I32, (TM, SLOTS), 1).astype(F32)
    lp = lp_ref[...]
    rt = rt_ref[...]

    def unsort(col):
        return jnp.dot((slot == lp[:, col:col + 1]).astype(BF16), yb, preferred_element_type=F32)

    moe = rt[:, 2:3] * unsort(0) + rt[:, 3:4] * unsort(1)
    g2 = jnp.where(i >= npt, g2s_ref[...], g2p_ref[...])
    y = _rms(x1_ref[...] + g2 * moe, fg_ref[...])

    @pl.when(i < npt)
    def _():
        yp_o[...] = y

    @pl.when(i >= npt)
    def _():
        ys_o[...] = y


def _combine(cnt, ls, gd, ys, lp, rt, x1, modp, mods, fg, *, npt, tpb, nb):
    d = x1.shape[1]
    nt = npt + 1
    row = lambda i, *_: (i, 0)
    return pl.pallas_call(
        functools.partial(_combine_kernel, npt=npt),
        out_shape=[jax.ShapeDtypeStruct((npt * TM, d), F32), jax.ShapeDtypeStruct((TM, d), F32)],
        grid_spec=pltpu.PrefetchScalarGridSpec(
            num_scalar_prefetch=3,
            grid=(nt,),
            in_specs=[pl.BlockSpec(memory_space=pl.ANY),
                      pl.BlockSpec((TM, LANES), row), pl.BlockSpec((TM, LANES), row), pl.BlockSpec((TM, d), row),
                      pl.BlockSpec((None, 1, d), lambda i, *_: (jnp.minimum(i // tpb, nb - 1), 0, 5)),
                      pl.BlockSpec((TM, d), lambda i, *_: (0, 5)),
                      pl.BlockSpec((1, d), lambda i, *_: (0, 0))],
            out_specs=[pl.BlockSpec((TM, d), lambda i, *_: (jnp.minimum(i, npt - 1), 0)),
                       pl.BlockSpec((TM, d), lambda i, *_: (0, 0))],
            scratch_shapes=[pltpu.VMEM((2, SLOTS, d), BF16), pltpu.SemaphoreType.DMA((2,))]),
        compiler_params=_cparams("arbitrary"),
    )(cnt, ls, gd, ys, lp, rt, x1, modp, mods, fg)


TAB_LO = 64


def _rotation_tables(t):
    inv_r = jnp.repeat(1.0 / (ROPE_THETA ** jnp.linspace(0.0, 1.0, RET_DK // 2, dtype=F32)), 2)
    sign_r = jnp.where(jnp.arange(RET_DK) % 2 == 0, -1.0, 1.0).astype(F32)
    inv_w = jnp.tile(ROPE_THETA ** (-jnp.arange(0, SWA_HD, 2, dtype=F32) / SWA_HD), LANES // (SWA_HD // 2))
    sign_w = jnp.where(jnp.arange(LANES) % SWA_HD < SWA_HD // 2, -1.0, 1.0).astype(F32)
    hi = (jnp.arange(t // TAB_LO, dtype=I32) * TAB_LO).astype(F32)[:, None]
    lo = jnp.arange(TAB_LO, dtype=I32).astype(F32)[:, None]
    past = jnp.full((1, 1), PAST_LEN, F32)

    def pair(inv, sign):
        a, b = hi * inv[None, :], lo * inv[None, :]
        ca, sa, cb, sb = jnp.cos(a)[:, None], jnp.sin(a)[:, None], jnp.cos(b)[None], jnp.sin(b)[None]
        cos = (ca * cb - sa * sb).reshape(t, LANES)
        sin = (sa * cb + ca * sb).reshape(t, LANES) * sign[None, :]
        ang = past * inv[None, :]
        return (cos, sin), (jnp.cos(ang), jnp.sin(ang) * sign[None, :])

    (pr, sr_), (pw, sw_) = pair(inv_r, sign_r), pair(inv_w, sign_w)
    return pr + pw, sr_ + sw_


def kernel(x_prompt, x_sample, c_prompt, c_sample, state_ret, cache_swa_k, cache_swa_v, w_ada, b_ada, norm1_g, norm2_g, w_in, w_up_ret, w_up_swa, w_o, sink, w_rg, b_rg, w_re, b_re, w1, w3, w2, final_g):
    nb, t, d = x_prompt.shape
    ns, dec_seq, _ = x_sample.shape
    depth = w_ada.shape[0]
    assert depth == 1 and dec_seq == 1, "single layer, one new token per sequence"
    assert t % TT == 0 and ns <= TM and ns % 16 == 0 and d % LANES == 0
    assert t % RET_BLOCK == 0 and t % (SWA_STEP_BLOCKS * WINDOW) == 0 and t % TAB_LO == 0
    assert N_GROUPS + N_EXPERTS <= LANES
    w = cache_swa_k.shape[2]
    tpb = t // TM
    npt = nb * tpb
    spb = t // TT
    nps = nb * spb
    nt = (nps + 1) * (TT // TM)
    np_rows = nb * t
    n_tok = np_rows + ns
    maxt = -(-(2 * n_tok + nt * N_EXPERTS * RUN + N_EXPERTS * (TE - 1)) // TE)

    xp = x_prompt.reshape(np_rows, d)
    xs_pad = jnp.pad(x_sample.reshape(ns, d), ((0, TT - ns), (0, 0)))

    c_all = jnp.concatenate([jnp.pad(c_sample, ((0, TT - ns), (0, 0))),
                             jnp.pad(c_prompt, ((0, SUBLANES - nb % SUBLANES), (0, 0)))])
    mods = _modulation(c_all, w_ada[0], b_ada[0])
    modp = mods[TT:TT + nb].reshape(nb, 1, 6 * d)

    tabs_p, tabs_s = _rotation_tables(t)
    rq, rk, rv, rg, sq, sk, sv, siga, sigb = _inproj(
        xp, xs_pad, modp, mods, norm1_g, w_in[0].astype(BF16), tabs_p, tabs_s, nps=nps, spb=spb, nb=nb)

    dm, qd, kd, cd, gamma = _ret_tables(RET_BLOCK)
    gated_p, st_p = _retention_prompt(rq, rk, rv, rg, (dm, qd, kd, cd), nb=nb, t=t)
    gated_s, st_s = _retention_sample(rq, rk, rv, rg, state_ret[0], gamma, row0=np_rows, ns=ns)
    oswa_p = _swa_prompt(sq, sk, sv, sink[0], nb=nb, t=t)
    to_t = lambda c: jnp.transpose(c[0], (0, 2, 3, 1)).reshape(ns, _SWA_KW, w)
    from_t = lambda c: jnp.transpose(c.reshape(ns, SWA_KV_HEADS, SWA_HD, w), (0, 3, 1, 2))[None]
    oswa_s, ks_new, vs_new = _swa_sample(sq, sk, sv, to_t(cache_swa_k), to_t(cache_swa_v), sink[0],
                                         row0=np_rows, ns=ns)
    gated_s = jnp.pad(gated_s, ((0, TT - ns), (0, 0)))
    oswa_s = jnp.pad(oswa_s, ((0, TT - ns), (0, 0)))

    wr = jnp.pad(jnp.concatenate([w_rg[0], w_re[0]], axis=1), ((0, 0), (0, LANES - N_GROUPS - N_EXPERTS)))
    br = jnp.pad(jnp.concatenate([b_rg[0], b_re[0]]), (0, LANES - N_GROUPS - N_EXPERTS)).reshape(1, LANES)
    x1, h2, rt, lp, cnt, ls, gb = _outproj(
        gated_p, gated_s, oswa_p, oswa_s, siga, sigb, xp, xs_pad, modp, mods, norm2_g,
        w_up_ret[0].astype(BF16), w_up_swa[0].astype(BF16), w_o[0].astype(BF16), wr, br,
        nps=nps, spb=spb, nb=nb, ns=ns)
    cnt = cnt[:, 0, :N_EXPERTS]
    ls = ls[:, 0, :N_EXPERTS]
    gb = gb[:, 0, :N_EXPERTS]
    seg = jnp.sum(cnt, axis=0)
    tiles = (seg + TE - 1) // TE
    tile_end = jnp.cumsum(tiles)
    row_start = (tile_end - tiles) * TE
    gd = (gb + row_start[None, :]).reshape(-1)
    n_used = tile_end[-1:]
    jj = jnp.minimum(jnp.arange(maxt, dtype=I32), n_used[0] - 1)
    te = jnp.minimum(jnp.sum((tile_end[None, :] <= jj[:, None]).astype(I32), axis=1), N_EXPERTS - 1)
    cnt = cnt.reshape(-1)
    ls = ls.reshape(-1)
    n_used = n_used.astype(I32)
    xs = _dispatch(cnt, ls, gd, row_start + seg, tiles * TE - seg, n_used, h2, lp, nt=nt, maxt=maxt)
    has = tiles > 0
    eidx = jnp.arange(N_EXPERTS, dtype=I32)
    later = jnp.where(has[None, :] & (eidx[None, :] > eidx[:, None]), eidx[None, :], N_EXPERTS)
    nxt_e = jnp.min(later, axis=1)
    nxt_e = jnp.where(nxt_e < N_EXPERTS, nxt_e, -1).astype(I32)
    par_e = ((jnp.cumsum(has.astype(I32)) - 1) % 2).astype(I32)
    ys = _experts(te, n_used, (tile_end - tiles).astype(I32), nxt_e, par_e, xs, w1[0], w3[0], w2[0], maxt=maxt)
    y_p, y_s = _combine(cnt, ls, gd, ys, lp, rt, x1, modp, mods, final_g.reshape(1, d), npt=npt, tpb=tpb, nb=nb)

    y_prompt = y_p.reshape(nb, t, d)
    y_sample = y_s[:ns].reshape(ns, 1, d)
    wk = min(WINDOW, t)
    last = lambda a: jnp.stack([a[(b + 1) * t - wk:(b + 1) * t] for b in range(nb)]).reshape(
        nb, wk, SWA_KV_HEADS, SWA_HD)
    skp, svp = last(sk), last(sv)
    return (y_prompt, y_sample, st_p[None], st_s[None], skp[None], svp[None], from_t(ks_new), from_t(vs_new))
```

```python
import functools

import jax
import jax.numpy as jnp
from jax import lax
from jax.experimental import pallas as pl
from jax.experimental.pallas import tpu as pltpu

F32 = jnp.float32
BF16 = jnp.bfloat16
I32 = jnp.int32

PAST_LEN = 8192
RET_HEADS = 4
RET_DK = 128
RET_DV = 128
RET_CHUNK = 128
SWA_HEADS = 8
SWA_KV_HEADS = 2
SWA_HD = 64
WINDOW = 128
ROPE_THETA = 10000.0
N_GROUPS = 4
EXPERTS_PER_GROUP = 8
N_EXPERTS = N_GROUPS * EXPERTS_PER_GROUP
D_EXPERT = 256
NORM_EPS = 1e-6

LANES = 128
SUBLANES = 8
TM = 256
TT = 2 * TM
RUN = 16
SLOTS = 2 * TM + N_EXPERTS * RUN
TE = 512
TE_SUB = 256
VMEM_LIMIT = 56 * 1024 * 1024

_RET_W = RET_HEADS * RET_DK
_SWA_QW = SWA_HEADS * SWA_HD
_SWA_KW = SWA_KV_HEADS * SWA_HD


def _cparams(*sem):
    return pltpu.CompilerParams(dimension_semantics=sem, vmem_limit_bytes=VMEM_LIMIT)


def _sigmoid(x):
    return 1.0 / (1.0 + jnp.exp(-x))


def _silu(x):
    return x * _sigmoid(x)


def _bdot(a, b):
    return jnp.dot(a.astype(BF16), b.astype(BF16), preferred_element_type=F32)


def _bdot_nt(a, b):
    return lax.dot_general(a.astype(BF16), b.astype(BF16), (((1,), (1,)), ((), ())), preferred_element_type=F32)


def _mod_kernel(c_ref, w_ref, b_ref, o_ref):
    o_ref[...] = _bdot(_silu(c_ref[...]), w_ref[...]) + b_ref[...]


def _modulation(c_all, w_ada, b_ada):
    rows, d = c_all.shape
    n = w_ada.shape[1]
    return pl.pallas_call(
        _mod_kernel,
        out_shape=jax.ShapeDtypeStruct((rows, n), F32),
        grid=(n // d,),
        in_specs=[pl.BlockSpec((rows, d), lambda j: (0, 0)),
                  pl.BlockSpec((d, d), lambda j: (0, j)),
                  pl.BlockSpec((1, d), lambda j: (0, j))],
        out_specs=pl.BlockSpec((rows, d), lambda j: (0, j)),
        compiler_params=_cparams("arbitrary"),
    )(c_all, w_ada, b_ada.reshape(1, n))


def _rms(x, g):
    return x * lax.rsqrt(jnp.mean(x * x, axis=-1, keepdims=True) + NORM_EPS) * g


def _pair_rotate(z, cos, sin_signed):
    n = z.shape[-1]
    lane = lax.broadcasted_iota(I32, z.shape, 1)
    partner = jnp.where((lane & 1) == 0, pltpu.roll(z, n - 1, 1), pltpu.roll(z, 1, 1))
    reps = n // LANES
    cos = jnp.concatenate([cos] * reps, axis=1) if reps > 1 else cos
    sin_signed = jnp.concatenate([sin_signed] * reps, axis=1) if reps > 1 else sin_signed
    return z * cos + partner * sin_signed


def _half_rotate(z, cos, sin_signed):
    n = z.shape[-1]
    half = SWA_HD // 2
    lane = lax.broadcasted_iota(I32, z.shape, 1)
    partner = jnp.where((lane & (SWA_HD - 1)) < half, pltpu.roll(z, n - half, 1), pltpu.roll(z, half, 1))
    reps = n // LANES
    cos = jnp.concatenate([cos] * reps, axis=1) if reps > 1 else cos
    sin_signed = jnp.concatenate([sin_signed] * reps, axis=1) if reps > 1 else sin_signed
    return z * cos + partner * sin_signed


def _inproj_kernel(xp_ref, xs_ref, shp_ref, scp_ref, shs_ref, scs_ref, n1_ref, w_ref,
                   crp_ref, srp_ref, cwp_ref, swp_ref, crs_ref, srs_ref, cws_ref, sws_ref,
                   rq_o, rk_o, rv_o, rg_o, sq_o, sk_o, sv_o, za_o, zb_o, *, nps):
    is_s = pl.program_id(0) >= nps
    d = xp_ref.shape[-1]
    subs = []
    for r in range(0, TT, TM):
        rows = slice(r, r + TM)
        x = jnp.where(is_s, xs_ref[rows], xp_ref[rows])
        sh = jnp.where(is_s, shs_ref[rows], shp_ref[...])
        sc = jnp.where(is_s, scs_ref[rows], scp_ref[...])
        h = (_rms(x, n1_ref[...]) * (1.0 + sc) + sh).astype(BF16)
        tabs = tuple(jnp.where(is_s, s_ref[...], p_ref[rows]) for s_ref, p_ref in
                     ((crs_ref, crp_ref), (srs_ref, srp_ref), (cws_ref, cwp_ref), (sws_ref, swp_ref)))
        subs.append((rows, h, tabs))

    def seg(h, a, b):
        return jnp.dot(h, w_ref[:, a:b], preferred_element_type=F32)

    o = 0
    for rows, h, (cr, sr, cw, sw) in subs:
        rq_o[rows] = _pair_rotate(seg(h, o, o + _RET_W), cr, sr).astype(BF16)
    o += _RET_W
    for rows, h, (cr, sr, cw, sw) in subs:
        rk_o[rows] = (_pair_rotate(seg(h, o, o + _RET_W), cr, sr) * (RET_DK ** -0.5)).astype(BF16)
    o += _RET_W
    for rows, h, _ in subs:
        rv_o[rows] = seg(h, o, o + _RET_W).astype(BF16)
    o += _RET_W
    for rows, h, _ in subs:
        rg_o[rows] = _silu(seg(h, o, o + _RET_W)).astype(BF16)
    o += _RET_W
    for rows, h, (cr, sr, cw, sw) in subs:
        sq_o[rows] = (_half_rotate(seg(h, o, o + _SWA_QW), cw, sw) * (SWA_HD ** -0.5)).astype(BF16)
    o += _SWA_QW
    for rows, h, (cr, sr, cw, sw) in subs:
        zkv = seg(h, o, o + 2 * _SWA_KW)
        sk_o[rows] = _half_rotate(zkv[:, :_SWA_KW], cw, sw)
        sv_o[rows] = zkv[:, _SWA_KW:]
    o += 2 * _SWA_KW
    for rows, h, _ in subs:
        za_o[rows] = _sigmoid(seg(h, o, o + d)).astype(BF16)
    o += d
    for rows, h, _ in subs:
        zb_o[rows] = _sigmoid(seg(h, o, o + d)).astype(BF16)


def _inproj(xp, xs_pad, modp, mods, n1, w_in_b, tabs_p, tabs_s, *, nps, spb, nb):
    d = xp.shape[1]
    nrow = (nps + 1) * TT
    n_in = w_in_b.shape[1]
    pstep = lambda i: (jnp.minimum(i, nps - 1), 0)
    pbatch = lambda col: (lambda i: (jnp.minimum(i // spb, nb - 1), 0, col))
    tab_idx = lambda i: (jnp.where(i < nps, i % spb, 0), 0)
    out_cols = [(_RET_W, BF16)] * 4 + [(_SWA_QW, BF16), (_SWA_KW, F32), (_SWA_KW, F32), (d, BF16), (d, BF16)]
    return pl.pallas_call(
        functools.partial(_inproj_kernel, nps=nps),
        out_shape=[jax.ShapeDtypeStruct((nrow, c), t) for c, t in out_cols],
        grid=(nps + 1,),
        in_specs=[pl.BlockSpec((TT, d), pstep),
                  pl.BlockSpec((TT, d), lambda i: (0, 0)),
                  pl.BlockSpec((None, 1, d), pbatch(0)),
                  pl.BlockSpec((None, 1, d), pbatch(1)),
                  pl.BlockSpec((TT, d), lambda i: (0, 0)),
                  pl.BlockSpec((TT, d), lambda i: (0, 1)),
                  pl.BlockSpec((1, d), lambda i: (0, 0)),
                  pl.BlockSpec((d, n_in), lambda i: (0, 0))]
                 + [pl.BlockSpec((TT, LANES), tab_idx)] * 4
                 + [pl.BlockSpec((1, LANES), lambda i: (0, 0))] * 4,
        out_specs=[pl.BlockSpec((TT, c), lambda i: (i, 0)) for c, _ in out_cols],
        compiler_params=_cparams("arbitrary"),
    )(xp, xs_pad, modp, modp, mods, mods, n1, w_in_b, *tabs_p, *tabs_s)


RET_BLOCK = 512


def _ret_kernel(q_ref, k_ref, v_ref, g_ref, dm_ref, qd_ref, kd_ref, cd_ref, o_ref, st_ref, s_scr, *, nsteps):
    step = pl.program_id(1)

    @pl.when(step == 0)
    def _():
        s_scr[...] = jnp.zeros_like(s_scr)

    for h in range(RET_HEADS):
        sl = slice(h * RET_DK, (h + 1) * RET_DK)
        state = s_scr[h]
        q, k, v = q_ref[:, sl], k_ref[:, sl], v_ref[:, sl]
        att = _bdot_nt(q, k) * dm_ref[h]
        o = _bdot(att, v) + _bdot(q.astype(F32) * qd_ref[h], state)
        kd = (k.astype(F32) * kd_ref[h]).astype(BF16)
        kv = lax.dot_general(kd, v, (((0,), (0,)), ((), ())), preferred_element_type=F32)
        s_scr[h] = cd_ref[h] * state + kv
        o = o * lax.rsqrt(jnp.mean(o * o, axis=-1, keepdims=True) + NORM_EPS)
        o_ref[:, sl] = (o * g_ref[:, sl].astype(F32)).astype(BF16)

    @pl.when(step == nsteps - 1)
    def _():
        st_ref[...] = s_scr[...]


def _ret_tables(chunk):
    ld = jnp.log(1.0 - 2.0 ** (-5.0 - jnp.arange(RET_HEADS, dtype=F32)))
    idx = jnp.arange(chunk, dtype=F32)
    diff = idx[:, None] - idx[None, :]
    causal = diff >= 0
    dmask = jnp.where(causal[None], jnp.exp(ld[:, None, None] * jnp.where(causal, diff, 0.0)[None]), 0.0)
    k_dec = jnp.exp(ld[None, :] * (chunk - 1.0 - idx)[:, None])
    q_dec = jnp.exp(ld[None, :] * (idx + 1.0)[:, None])
    chunk_decay = jnp.exp(ld * chunk)
    bc = lambda t: jnp.broadcast_to(t.T[:, :, None], (RET_HEADS, chunk, RET_DV))
    cd = jnp.broadcast_to(chunk_decay[:, None, None], (RET_HEADS, 1, RET_DV))
    return dmask, bc(q_dec), bc(k_dec), cd, jnp.exp(ld)


def _retention_prompt(rq, rk, rv, rg, tabs, *, nb, t):
    rows = RET_BLOCK
    nsteps = t // rows
    dm, qd, kd, cd = tabs
    blk = lambda b, c: (b * nsteps + c, 0)
    full3 = lambda b, c: (0, 0, 0)
    return pl.pallas_call(
        functools.partial(_ret_kernel, nsteps=nsteps),
        out_shape=[jax.ShapeDtypeStruct((nb * t, _RET_W), BF16),
                   jax.ShapeDtypeStruct((nb, RET_HEADS, RET_DK, RET_DV), F32)],
        grid=(nb, nsteps),
        in_specs=[pl.BlockSpec((rows, _RET_W), blk)] * 4
                 + [pl.BlockSpec((RET_HEADS, rows, rows), full3)]
                 + [pl.BlockSpec((RET_HEADS, rows, RET_DV), full3)] * 2
                 + [pl.BlockSpec((RET_HEADS, 1, RET_DV), full3)],
        out_specs=[pl.BlockSpec((rows, _RET_W), blk),
                   pl.BlockSpec((None, RET_HEADS, RET_DK, RET_DV), lambda b, c: (b, 0, 0, 0))],
        scratch_shapes=[pltpu.VMEM((RET_HEADS, RET_DK, RET_DV), F32)],
        compiler_params=_cparams("arbitrary", "arbitrary"),
    )(rq, rk, rv, rg, dm, qd, kd, cd)


def _ret_sample_kernel(gam_ref, q_ref, k_ref, v_ref, g_ref, s0_ref, o_ref, st_ref, *, sb):
    gamma = gam_ref[pl.program_id(1)]
    q = q_ref[...].astype(F32)
    k = k_ref[...].astype(F32)
    v = v_ref[...].astype(F32)
    rows = sb * RET_DK
    s2 = s0_ref[...].reshape(rows, RET_DV)
    col_b = lax.broadcasted_iota(I32, (sb, rows), 1) // RET_DK
    row_b = lax.broadcasted_iota(I32, (sb, rows), 0)
    qexp = jnp.where(col_b == row_b, jnp.concatenate([q * gamma] * sb, axis=1), 0.0)
    o = jnp.sum(q * k, axis=-1, keepdims=True) * v + _bdot(qexp, s2)
    o = o * lax.rsqrt(jnp.mean(o * o, axis=-1, keepdims=True) + NORM_EPS)
    o_ref[...] = (o * g_ref[...].astype(F32)).astype(BF16)
    rep = (lax.broadcasted_iota(I32, (rows, sb), 0) // RET_DK == lax.broadcasted_iota(I32, (rows, sb), 1))
    rep = rep.astype(BF16)
    krep = _bdot(rep, k)
    vrep = _bdot(rep, v)
    eye = (lax.broadcasted_iota(I32, (rows, RET_DK), 0) % RET_DK == lax.broadcasted_iota(I32, (rows, RET_DK), 1))
    kcol = jnp.sum(jnp.where(eye, krep, 0.0), axis=-1, keepdims=True)
    st_ref[...] = (gamma * s2 + kcol * vrep).reshape(sb, RET_DK, RET_DV)


def _retention_sample(rq, rk, rv, rg, s0, gamma, *, row0, ns):
    sb = min(64, ns)
    base = row0 // sb
    blk = lambda i, h: (base + i, h)
    sblk = lambda i, h: (i, h, 0, 0)
    return pl.pallas_call(
        functools.partial(_ret_sample_kernel, sb=sb),
        out_shape=[jax.ShapeDtypeStruct((ns, _RET_W), BF16),
                   jax.ShapeDtypeStruct(s0.shape, F32)],
        grid=(ns // sb, RET_HEADS),
        in_specs=[pl.BlockSpec(memory_space=pltpu.SMEM)]
                 + [pl.BlockSpec((sb, RET_DK), blk)] * 4
                 + [pl.BlockSpec((sb, None, RET_DK, RET_DV), sblk)],
        out_specs=[pl.BlockSpec((sb, RET_DV), lambda i, h: (i, h)),
                   pl.BlockSpec((sb, None, RET_DK, RET_DV), sblk)],
        compiler_params=_cparams("arbitrary", "arbitrary"),
    )(gamma, rq, rk, rv, rg, s0)


def _sink_softmax(s, mask, sink):
    if mask is not None:
        s = jnp.where(mask, s, -jnp.inf)
    m = jnp.maximum(jnp.max(s, axis=-1, keepdims=True), sink)
    p = jnp.exp(s - m)
    return p / (jnp.sum(p, axis=-1, keepdims=True) + jnp.exp(sink - m))


def _split_kv_heads(x):
    lo = lax.broadcasted_iota(I32, x.shape, 1) < SWA_HD
    h0_lo = jnp.where(lo, x, 0.0)
    h1_hi = jnp.where(lo, 0.0, x)
    return ((h0_lo, pltpu.roll(h0_lo, SWA_HD, 1)), (pltpu.roll(h1_hi, SWA_HD, 1), h1_hi))


SWA_STEP_BLOCKS = 4


def _swa_kernel(sink_ref, q_ref, kc_ref, kp_ref, vc_ref, vp_ref, o_ref):
    n = pl.program_id(1)
    c = WINDOW
    kk = jnp.concatenate([kp_ref[...], kc_ref[...]], axis=0)
    vv = jnp.concatenate([vp_ref[...], vc_ref[...]], axis=0)
    ks = [[a.astype(BF16) for a in pair] for pair in _split_kv_heads(kk)]
    vs = [[a.astype(BF16) for a in pair] for pair in _split_kv_heads(vv)]
    qi = lax.broadcasted_iota(I32, (2 * c, c), 0) % c
    ki = lax.broadcasted_iota(I32, (2 * c, c), 1)
    from_prev = ki > qi
    top = lax.broadcasted_iota(I32, (2 * c, 1), 0) < c
    units = [(s, kvh) for s in range(SWA_STEP_BLOCKS) for kvh in range(SWA_KV_HEADS)]
    scores = []
    for s, kvh in units:
        rows, keys = slice(s * c, (s + 1) * c), slice(s * c, (s + 2) * c)
        q2 = jnp.concatenate([q_ref[rows, 2 * kvh * LANES:(2 * kvh + 1) * LANES],
                              q_ref[rows, (2 * kvh + 1) * LANES:(2 * kvh + 2) * LANES]], axis=0)
        kcat = jnp.concatenate([ks[kvh][0][keys], ks[kvh][1][keys]], axis=0)
        scores.append(lax.dot_general(q2, kcat, (((1,), (1,)), ((), ())), preferred_element_type=F32))
    probs = []
    for (s, kvh), sc in zip(units, scores):
        ps = []
        for half in range(2):
            sink = jnp.where(top, sink_ref[4 * kvh + half], sink_ref[4 * kvh + 2 + half])
            s_prev = sc[:, half * 2 * c:half * 2 * c + c]
            s_own = sc[:, half * 2 * c + c:(half + 1) * 2 * c]
            if s == 0:
                s_prev = jnp.where(n > 0, s_prev, -jnp.inf)
            p = _sink_softmax(jnp.where(from_prev, s_prev, s_own), None, sink)
            ps += [jnp.where(from_prev, p, 0.0).astype(BF16), jnp.where(from_prev, 0.0, p).astype(BF16)]
        probs.append(jnp.concatenate(ps, axis=1))
    for (s, kvh), p in zip(units, probs):
        rows, keys = slice(s * c, (s + 1) * c), slice(s * c, (s + 2) * c)
        vcat = jnp.concatenate([vs[kvh][0][keys], vs[kvh][1][keys]], axis=0)
        o = jnp.dot(p, vcat, preferred_element_type=F32)
        o_ref[rows, 2 * kvh * LANES:(2 * kvh + 1) * LANES] = o[:c].astype(BF16)
        o_ref[rows, (2 * kvh + 1) * LANES:(2 * kvh + 2) * LANES] = o[c:].astype(BF16)


def _swa_prompt(sq, sk, sv, sink, *, nb, t):
    rows = SWA_STEP_BLOCKS * WINDOW
    nsteps = t // rows
    nblk = t // WINDOW
    cur = lambda b, n: (b * nsteps + n, 0)
    prev = lambda b, n: (b * nblk + jnp.maximum(n * SWA_STEP_BLOCKS - 1, 0), 0)
    return pl.pallas_call(
        _swa_kernel,
        out_shape=jax.ShapeDtypeStruct((nb * t, _SWA_QW), BF16),
        grid=(nb, nsteps),
        in_specs=[pl.BlockSpec(memory_space=pltpu.SMEM),
                  pl.BlockSpec((rows, _SWA_QW), cur),
                  pl.BlockSpec((rows, _SWA_KW), cur),
                  pl.BlockSpec((WINDOW, _SWA_KW), prev),
                  pl.BlockSpec((rows, _SWA_KW), cur),
                  pl.BlockSpec((WINDOW, _SWA_KW), prev)],
        out_specs=pl.BlockSpec((rows, _SWA_QW), cur),
        compiler_params=_cparams("arbitrary", "arbitrary"),
    )(sink, sq, sk, sk, sv, sv)


def _swa_sample_kernel(sink_ref, q_ref, kn_ref, vn_ref, kc_ref, vc_ref, o_ref, ko_ref, vo_ref, *, sb, w):
    pad = jnp.zeros((LANES - sb, _SWA_KW), F32)
    knt = jnp.concatenate([kn_ref[...], pad], axis=0).T
    vnt = jnp.concatenate([vn_ref[...], pad], axis=0).T
    kall = jnp.concatenate([kc_ref[b] for b in range(sb)] + [knt], axis=1)
    vall = jnp.concatenate([vc_ref[b] for b in range(sb)] + [vnt], axis=1)
    ncol = sb * w + LANES
    lo = lax.broadcasted_iota(I32, (sb, LANES), 1) < SWA_HD
    group = SWA_HEADS // SWA_KV_HEADS
    pieces = []
    for h in range(SWA_HEADS):
        slab = q_ref[:, (h // 2) * LANES:(h // 2 + 1) * LANES].astype(F32)
        mine = jnp.where(lo, slab, 0.0) if h % 2 == 0 else jnp.where(lo, 0.0, slab)
        pieces.append(mine if (h % 2) == (h // group) else pltpu.roll(mine, SWA_HD, 1))
    qrows = jnp.concatenate(pieces, axis=0)
    nrow = SWA_HEADS * sb
    s = _bdot(qrows, kall)
    rb = lax.broadcasted_iota(I32, (nrow, ncol), 0) % sb
    ci = lax.broadcasted_iota(I32, (nrow, ncol), 1)
    in_cache = (ci < sb * w) & (ci // w == rb) & ((w - ci % w) < WINDOW)
    mask = in_cache | (ci == sb * w + rb)
    sink_col = jnp.concatenate([jnp.full((sb, 1), sink_ref[h], F32) for h in range(SWA_HEADS)], axis=0)
    p = _sink_softmax(s, mask, sink_col)
    o = _bdot_nt(p, vall)
    for j in range(SWA_HEADS // 2):
        acc = jnp.zeros((sb, LANES), F32)
        for half in range(2):
            h = 2 * j + half
            oh = o[h * sb:(h + 1) * sb]
            own = jnp.where(lo, oh, 0.0) if h // group == 0 else jnp.where(lo, 0.0, oh)
            acc = acc + (own if (h // group) == half else pltpu.roll(own, SWA_HD, 1))
        o_ref[:, j * LANES:(j + 1) * LANES] = acc.astype(BF16)
    newest = lax.broadcasted_iota(I32, (_SWA_KW, w), 1) == w - 1
    for b in range(sb):
        ko_ref[b] = jnp.where(newest, knt[:, b:b + 1], pltpu.roll(kc_ref[b], w - 1, 1))
        vo_ref[b] = jnp.where(newest, vnt[:, b:b + 1], pltpu.roll(vc_ref[b], w - 1, 1))


def _swa_sample(sq, sk, sv, cache_kt, cache_vt, sink, *, row0, ns):
    sb = min(16, ns)
    w = cache_kt.shape[2]
    base = row0 // sb
    blk = lambda i: (base + i, 0)
    cblk = lambda i: (i, 0, 0)
    cspec = pl.BlockSpec((sb, _SWA_KW, w), cblk)
    return pl.pallas_call(
        functools.partial(_swa_sample_kernel, sb=sb, w=w),
        out_shape=[jax.ShapeDtypeStruct((ns, _SWA_QW), BF16),
                   jax.ShapeDtypeStruct(cache_kt.shape, F32), jax.ShapeDtypeStruct(cache_vt.shape, F32)],
        grid=(ns // sb,),
        in_specs=[pl.BlockSpec(memory_space=pltpu.SMEM),
                  pl.BlockSpec((sb, _SWA_QW), blk),
                  pl.BlockSpec((sb, _SWA_KW), blk),
                  pl.BlockSpec((sb, _SWA_KW), blk),
                  cspec, cspec],
        out_specs=[pl.BlockSpec((sb, _SWA_QW), lambda i: (i, 0)), cspec, cspec],
        compiler_params=_cparams("arbitrary"),
    )(sink, sq, sk, sv, cache_kt, cache_vt)


def _route(logits):
    lane = lax.broadcasted_iota(I32, logits.shape, 1).astype(F32)
    big = float(1 << 20)
    neg = -jnp.inf

    def top(mask):
        v = jnp.max(jnp.where(mask, logits, neg), axis=-1, keepdims=True)
        i = jnp.min(jnp.where(mask & (logits == v), lane, big), axis=-1, keepdims=True)
        return v, i

    gmask = lane < N_GROUPS
    gmax, gsel = top(gmask)
    p_group = 1.0 / jnp.sum(jnp.where(gmask, jnp.exp(logits - gmax), 0.0), axis=-1, keepdims=True)
    first = N_GROUPS + gsel * EXPERTS_PER_GROUP
    emask = (lane >= first) & (lane < first + EXPERTS_PER_GROUP)
    v1, i1 = top(emask)
    v2, i2 = top(emask & (lane != i1))
    t = jnp.exp(v2 - v1)
    w1 = p_group / (1.0 + t)
    return (i1 - N_GROUPS).astype(I32), (i2 - N_GROUPS).astype(I32), w1, w1 * t


def _plan_tile(e1, e2, valid, carry):
    lane = lax.broadcasted_iota(I32, (TM, LANES), 1)
    oh1 = ((lane == e1) & valid).astype(F32)
    oh2 = ((lane == e2) & valid).astype(F32)
    oh = oh1 + oh2
    tri = (lax.broadcasted_iota(I32, (TM, TM), 0) > lax.broadcasted_iota(I32, (TM, TM), 1)).astype(BF16)
    before = _bdot(tri, oh)
    cnt = jnp.sum(oh, axis=0, keepdims=True)
    units = jnp.maximum(jnp.floor((cnt + (RUN - 1)) * (1.0 / RUN)), 1.0)
    upper = (lax.broadcasted_iota(I32, (LANES, LANES), 0) < lax.broadcasted_iota(I32, (LANES, LANES), 1))
    lstart = RUN * _bdot(jnp.broadcast_to(units, (SUBLANES, LANES)), upper.astype(BF16))[0:1]
    slot = lstart + before
    lp1 = jnp.sum(oh1 * slot, axis=-1, keepdims=True)
    lp2 = jnp.sum(oh2 * slot, axis=-1, keepdims=True)
    vcol = valid[:, 0:1]
    lp = jnp.where(lane == 0, jnp.where(vcol, lp1, -1.0), jnp.where(lane == 1, jnp.where(vcol, lp2, -1.0), 0.0))
    base = carry[...]
    carry[...] = base + RUN * units
    return lp, (RUN * units).astype(I32), lstart.astype(I32), base.astype(I32)


def _outproj_kernel(gtp_ref, gts_ref, osp_ref, oss_ref, sa_ref, sb_ref, xp_ref, xs_ref, g1p_ref, shp_ref, scp_ref,
                    g1s_ref, shs_ref, scs_ref, n2_ref, wur_ref, wus_ref, wo_ref, wrh_ref, wrl_ref, br_ref,
                    x1_o, h2_o, rt_o, lp_o, cnt_o, ls_o, gb_o, carry, *, nps, ns):
    i = pl.program_id(0)
    is_s = i >= nps

    @pl.when(i == 0)
    def _():
        carry[...] = jnp.zeros_like(carry)

    subs = [slice(r, r + TM) for r in range(0, TT, TM)]
    merged = []
    for rows in subs:
        gated = jnp.where(is_s, gts_ref[rows], gtp_ref[rows])
        oswa = jnp.where(is_s, oss_ref[rows], osp_ref[rows])
        y_ret = jnp.dot(gated, wur_ref[...], preferred_element_type=F32)
        y_swa = jnp.dot(oswa, wus_ref[...], preferred_element_type=F32)
        merged.append((sa_ref[rows].astype(F32) * y_ret + sb_ref[rows].astype(F32) * y_swa).astype(BF16))
    hs = []
    for rows, m in zip(subs, merged):
        x = jnp.where(is_s, xs_ref[rows], xp_ref[rows])
        g1 = jnp.where(is_s, g1s_ref[rows], g1p_ref[...])
        sh = jnp.where(is_s, shs_ref[rows], shp_ref[...])
        sc = jnp.where(is_s, scs_ref[rows], scp_ref[...])
        x1 = x + g1 * jnp.dot(m, wo_ref[...], preferred_element_type=F32)
        x1_o[rows] = x1
        h2 = _rms(x1, n2_ref[...]) * (1.0 + sc) + sh
        hi = h2.astype(BF16)
        h2_o[rows] = hi
        hs.append((hi, (h2 - hi.astype(F32)).astype(BF16)))
    routed = []
    for rows, (hi, lo) in zip(subs, hs):
        wrh = wrh_ref[...]
        logits = (jnp.dot(hi, wrh, preferred_element_type=F32) + jnp.dot(lo, wrh, preferred_element_type=F32)
                  + jnp.dot(hi, wrl_ref[...], preferred_element_type=F32) + br_ref[...])
        e1, e2, w1, w2 = _route(logits)
        lane = lax.broadcasted_iota(I32, logits.shape, 1)
        rt_o[rows] = jnp.where(lane == 2, w1, jnp.where(lane == 3, w2, 0.0))
        routed.append((e1, e2))
    for sub, (e1, e2) in enumerate(routed):
        row = lax.broadcasted_iota(I32, (TM, LANES), 0) + sub * TM
        valid = jnp.logical_not(is_s) | (row < ns)
        lp_o[sub * TM:(sub + 1) * TM], cnt_o[sub], ls_o[sub], gb_o[sub] = _plan_tile(e1, e2, valid, carry)


def _outproj(gated_p, gated_s, oswa_p, oswa_s, siga, sigb, xp, xs_pad, modp, mods, n2, wur, wus, wo, wr, br,
             *, nps, spb, nb, ns):
    d = xp.shape[1]
    sub = TT // TM
    nrow = (nps + 1) * TT
    row = lambda i: (i, 0)
    pstep = lambda i: (jnp.minimum(i, nps - 1), 0)
    pbatch = lambda col: (lambda i: (jnp.minimum(i // spb, nb - 1), 0, col))
    scol = lambda col: (lambda i: (0, col))
    const = lambda i: (0, 0)
    wr_hi = wr.astype(BF16)
    wr_lo = (wr - wr_hi.astype(F32)).astype(BF16)
    meta = jax.ShapeDtypeStruct(((nps + 1) * sub, 1, LANES), I32)
    mspec = pl.BlockSpec((sub, 1, LANES), lambda i: (i, 0, 0))
    return pl.pallas_call(
        functools.partial(_outproj_kernel, nps=nps, ns=ns),
        out_shape=[jax.ShapeDtypeStruct((nrow, d), F32),
                   jax.ShapeDtypeStruct((nrow, d), BF16),
                   jax.ShapeDtypeStruct((nrow, LANES), F32),
                   jax.ShapeDtypeStruct((nrow, LANES), F32), meta, meta, meta],
        grid=(nps + 1,),
        in_specs=[pl.BlockSpec((TT, _RET_W), pstep), pl.BlockSpec((TT, _RET_W), const),
                  pl.BlockSpec((TT, _SWA_QW), pstep), pl.BlockSpec((TT, _SWA_QW), const),
                  pl.BlockSpec((TT, d), row), pl.BlockSpec((TT, d), row),
                  pl.BlockSpec((TT, d), pstep), pl.BlockSpec((TT, d), const),
                  pl.BlockSpec((None, 1, d), pbatch(2)), pl.BlockSpec((None, 1, d), pbatch(3)),
                  pl.BlockSpec((None, 1, d), pbatch(4)),
                  pl.BlockSpec((TT, d), scol(2)), pl.BlockSpec((TT, d), scol(3)), pl.BlockSpec((TT, d), scol(4)),
                  pl.BlockSpec((1, d), const),
                  pl.BlockSpec(wur.shape, const), pl.BlockSpec(wus.shape, const), pl.BlockSpec(wo.shape, const),
                  pl.BlockSpec(wr.shape, const), pl.BlockSpec(wr.shape, const), pl.BlockSpec((1, LANES), const)],
        out_specs=[pl.BlockSpec((TT, d), row), pl.BlockSpec((TT, d), row), pl.BlockSpec((TT, LANES), row),
                   pl.BlockSpec((TT, LANES), row), mspec, mspec, mspec],
        scratch_shapes=[pltpu.VMEM((1, LANES), F32)],
        compiler_params=_cparams("arbitrary"),
    )(gated_p, gated_s, oswa_p, oswa_s, siga, sigb, xp, xs_pad, modp, modp, modp, mods, mods, mods, n2,
      wur, wus, wo, wr_hi, wr_lo, br)


def _aligned(v):
    return v if isinstance(v, int) else pl.multiple_of(v, RUN)


def _run_copy(src, dst, s_start, d_start, n, sem):
    s_start, d_start, n = _aligned(s_start), _aligned(d_start), _aligned(n)
    return pltpu.make_async_copy(src.at[pl.ds(s_start, n)], dst.at[pl.ds(d_start, n)], sem)


def _tile_rows(cnt_ref, ls_ref, step):
    last = step * N_EXPERTS + N_EXPERTS - 1
    return ls_ref[last] + cnt_ref[last]


def _each_run(step, fn):
    for e in range(N_EXPERTS):
        fn(step * N_EXPERTS + e)


def _dispatch_kernel(cnt_ref, ls_ref, gd_ref, ps_ref, pn_ref, nu_ref, h_ref, lp_ref, xs_ref,
                     sorted_scr, zero_scr, sem, zsem, *, nt, maxt):
    i = pl.program_id(0)

    def pad(e):
        return _run_copy(zero_scr, xs_ref, 0, ps_ref[e], pn_ref[e], zsem)

    def tail(j):
        return _run_copy(zero_scr, xs_ref, 0, j * TE, TE, zsem)

    def each_pad(fn):
        def body(e, c):
            @pl.when(pn_ref[e] > 0)
            def _():
                fn(pad(e))
            return c
        lax.fori_loop(0, N_EXPERTS, body, 0)

    def each_tail(fn):
        def body(j, c):
            fn(tail(j))
            return c
        lax.fori_loop(nu_ref[0], maxt, body, 0)

    @pl.when(i == 0)
    def _():
        zero_scr[...] = jnp.zeros_like(zero_scr)
        each_pad(lambda cp: cp.start())
        each_tail(lambda cp: cp.start())

    lpt = lp_ref[...].T
    slot = lax.broadcasted_iota(I32, (SLOTS, TM), 0).astype(F32)
    perm = ((slot == lpt[0:1]) | (slot == lpt[1:2])).astype(BF16)
    sorted_scr[i % 2] = jnp.dot(perm, h_ref[...], preferred_element_type=F32).astype(BF16)

    def copy(step):
        return lambda k: _run_copy(sorted_scr.at[step % 2], xs_ref, ls_ref[k], gd_ref[k], cnt_ref[k], sem.at[step % 2])

    def wait_all(step):
        _run_copy(sorted_scr.at[step % 2], xs_ref, 0, 0, _tile_rows(cnt_ref, ls_ref, step), sem.at[step % 2]).wait()

    _each_run(i, lambda k: copy(i)(k).start())

    @pl.when(i > 0)
    def _():
        wait_all(i - 1)

    @pl.when(i == nt - 1)
    def _():
        wait_all(i)
        each_pad(lambda cp: cp.wait())
        each_tail(lambda cp: cp.wait())


def _dispatch(cnt, ls, gd, ps, pn, nu, h2, lp, *, nt, maxt):
    d = h2.shape[1]
    return pl.pallas_call(
        functools.partial(_dispatch_kernel, nt=nt, maxt=maxt),
        out_shape=jax.ShapeDtypeStruct((maxt * TE, d), BF16),
        grid_spec=pltpu.PrefetchScalarGridSpec(
            num_scalar_prefetch=6,
            grid=(nt,),
            in_specs=[pl.BlockSpec((TM, d), lambda i, *_: (i, 0)),
                      pl.BlockSpec((TM, LANES), lambda i, *_: (i, 0))],
            out_specs=pl.BlockSpec(memory_space=pl.ANY),
            scratch_shapes=[pltpu.VMEM((2, SLOTS, d), BF16), pltpu.VMEM((TE, d), BF16),
                            pltpu.SemaphoreType.DMA((2,)), pltpu.SemaphoreType.DMA(())]),
        compiler_params=_cparams("arbitrary"),
    )(cnt, ls, gd, ps, pn, nu, h2, lp)


def _experts_kernel(te_ref, nu_ref, start_ref, nxt_ref, par_ref, x_ref, w1_hbm, w3_hbm, w2_hbm, y_ref,
                    w1f, w3f, w2f, w1b, w3b, w2b, sem):
    j = pl.program_id(0)
    e = te_ref[j]

    def fetch(ex, slot):
        return [pltpu.make_async_copy(src.at[ex], dst.at[slot], sem.at[slot, n])
                for n, (src, dst) in enumerate(((w1_hbm, w1f), (w3_hbm, w3f), (w2_hbm, w2f)))]

    @pl.when(j == 0)
    def _():
        for cp in fetch(e, par_ref[e]):
            cp.start()

    @pl.when((j == start_ref[e]) & (j < nu_ref[0]))
    def _():
        slot = par_ref[e]

        @pl.when(nxt_ref[e] >= 0)
        def _():
            for cp in fetch(nxt_ref[e], 1 - slot):
                cp.start()

        for cp in fetch(e, slot):
            cp.wait()
        w1b[...] = w1f[slot].astype(BF16)
        w3b[...] = w3f[slot].astype(BF16)
        w2b[...] = w2f[slot].astype(BF16)

    @pl.when(j < nu_ref[0])
    def _():
        subs = [slice(r, r + TE_SUB) for r in range(0, TE, TE_SUB)]
        ab = [(jnp.dot(x_ref[rows], w1b[...], preferred_element_type=F32),
               jnp.dot(x_ref[rows], w3b[...], preferred_element_type=F32)) for rows in subs]
        hid = [(_silu(a) * b).astype(BF16) for a, b in ab]
        for rows, h in zip(subs, hid):
            y_ref[rows] = jnp.dot(h, w2b[...], preferred_element_type=F32).astype(BF16)


def _experts(te, nu, start, nxt, par, xs, w1, w3, w2, *, maxt):
    d = xs.shape[1]
    f = w1.shape[2]
    used = lambda j, te, nu, *_: (jnp.minimum(j, nu[0] - 1), 0)
    hbm = pl.BlockSpec(memory_space=pl.ANY)
    return pl.pallas_call(
        _experts_kernel,
        out_shape=jax.ShapeDtypeStruct(xs.shape, BF16),
        grid_spec=pltpu.PrefetchScalarGridSpec(
            num_scalar_prefetch=5,
            grid=(maxt,),
            in_specs=[pl.BlockSpec((TE, d), used), hbm, hbm, hbm],
            out_specs=pl.BlockSpec((TE, d), used),
            scratch_shapes=[pltpu.VMEM((2, d, f), F32), pltpu.VMEM((2, d, f), F32), pltpu.VMEM((2, f, d), F32),
                            pltpu.VMEM((d, f), BF16), pltpu.VMEM((d, f), BF16), pltpu.VMEM((f, d), BF16),
                            pltpu.SemaphoreType.DMA((2, 3))]),
        input_output_aliases={5: 0},
        compiler_params=_cparams("arbitrary"),
    )(te, nu, start, nxt, par, xs, w1, w3, w2)


def _combine_kernel(cnt_ref, ls_ref, gd_ref, ys_ref, lp_ref, rt_ref, x1_ref, g2p_ref, g2s_ref, fg_ref,
                    yp_o, ys_o, ybuf, sem, *, npt):
    i = pl.program_id(0)
    nt = npt + 1

    def fetch(step):
        ybuf[step % 2] = jnp.zeros(ybuf.shape[1:], BF16)
        _each_run(step, lambda k: _run_copy(ys_ref, ybuf.at[step % 2], gd_ref[k], ls_ref[k], cnt_ref[k],
                                            sem.at[step % 2]).start())

    @pl.when(i == 0)
    def _():
        fetch(0)

    @pl.when(i + 1 < nt)
    def _():
        fetch(i + 1)

    _run_copy(ys_ref, ybuf.at[i % 2], 0, 0, _tile_rows(cnt_ref, ls_ref, i), sem.at[i % 2]).wait()

    yb = ybuf[i % 2]
    slot = lax.broadcasted_iota(I32, (TM, SLOTS), 1).astype(F32)
    lp = lp_ref[...]
    rt = rt_ref[...]

    def unsort(col):
        return jnp.dot((slot == lp[:, col:col + 1]).astype(BF16), yb, preferred_element_type=F32)

    moe = rt[:, 2:3] * unsort(0) + rt[:, 3:4] * unsort(1)
    g2 = jnp.where(i >= npt, g2s_ref[...], g2p_ref[...])
    y = _rms(x1_ref[...] + g2 * moe, fg_ref[...])

    @pl.when(i < npt)
    def _():
        yp_o[...] = y

    @pl.when(i >= npt)
    def _():
        ys_o[...] = y


def _combine(cnt, ls, gd, ys, lp, rt, x1, modp, mods, fg, *, npt, tpb, nb):
    d = x1.shape[1]
    nt = npt + 1
    row = lambda i, *_: (i, 0)
    return pl.pallas_call(
        functools.partial(_combine_kernel, npt=npt),
        out_shape=[jax.ShapeDtypeStruct((npt * TM, d), F32), jax.ShapeDtypeStruct((TM, d), F32)],
        grid_spec=pltpu.PrefetchScalarGridSpec(
            num_scalar_prefetch=3,
            grid=(nt,),
            in_specs=[pl.BlockSpec(memory_space=pl.ANY),
                      pl.BlockSpec((TM, LANES), row), pl.BlockSpec((TM, LANES), row), pl.BlockSpec((TM, d), row),
                      pl.BlockSpec((None, 1, d), lambda i, *_: (jnp.minimum(i // tpb, nb - 1), 0, 5)),
                      pl.BlockSpec((TM, d), lambda i, *_: (0, 5)),
                      pl.BlockSpec((1, d), lambda i, *_: (0, 0))],
            out_specs=[pl.BlockSpec((TM, d), lambda i, *_: (jnp.minimum(i, npt - 1), 0)),
                       pl.BlockSpec((TM, d), lambda i, *_: (0, 0))],
            scratch_shapes=[pltpu.VMEM((2, SLOTS, d), BF16), pltpu.SemaphoreType.DMA((2,))]),
        compiler_params=_cparams("arbitrary"),
    )(cnt, ls, gd, ys, lp, rt, x1, modp, mods, fg)


TAB_LO = 64


def _rotation_tables(t):
    inv_r = jnp.repeat(1.0 / (ROPE_THETA ** jnp.linspace(0.0, 1.0, RET_DK // 2, dtype=F32)), 2)
    sign_r = jnp.where(jnp.arange(RET_DK) % 2 == 0, -1.0, 1.0).astype(F32)
    inv_w = jnp.tile(ROPE_THETA ** (-jnp.arange(0, SWA_HD, 2, dtype=F32) / SWA_HD), LANES // (SWA_HD // 2))
    sign_w = jnp.where(jnp.arange(LANES) % SWA_HD < SWA_HD // 2, -1.0, 1.0).astype(F32)
    hi = (jnp.arange(t // TAB_LO, dtype=I32) * TAB_LO).astype(F32)[:, None]
    lo = jnp.arange(TAB_LO, dtype=I32).astype(F32)[:, None]
    past = jnp.full((1, 1), PAST_LEN, F32)

    def pair(inv, sign):
        a, b = hi * inv[None, :], lo * inv[None, :]
        ca, sa, cb, sb = jnp.cos(a)[:, None], jnp.sin(a)[:, None], jnp.cos(b)[None], jnp.sin(b)[None]
        cos = (ca * cb - sa * sb).reshape(t, LANES)
        sin = (sa * cb + ca * sb).reshape(t, LANES) * sign[None, :]
        ang = past * inv[None, :]
        return (cos, sin), (jnp.cos(ang), jnp.sin(ang) * sign[None, :])

    (pr, sr_), (pw, sw_) = pair(inv_r, sign_r), pair(inv_w, sign_w)
    return pr + pw, sr_ + sw_


def kernel(x_prompt, x_sample, c_prompt, c_sample, state_ret, cache_swa_k, cache_swa_v, w_ada, b_ada, norm1_g, norm2_g, w_in, w_up_ret, w_up_swa, w_o, sink, w_rg, b_rg, w_re, b_re, w1, w3, w2, final_g):
    nb, t, d = x_prompt.shape
    ns, dec_seq, _ = x_sample.shape
    depth = w_ada.shape[0]
    assert depth == 1 and dec_seq == 1, "single layer, one new token per sequence"
    assert t % TT == 0 and ns <= TM and ns % 16 == 0 and d % LANES == 0
    assert t % RET_BLOCK == 0 and t % (SWA_STEP_BLOCKS * WINDOW) == 0 and t % TAB_LO == 0
    assert N_GROUPS + N_EXPERTS <= LANES
    w = cache_swa_k.shape[2]
    tpb = t // TM
    npt = nb * tpb
    spb = t // TT
    nps = nb * spb
    nt = (nps + 1) * (TT // TM)
    np_rows = nb * t
    n_tok = np_rows + ns
    maxt = -(-(2 * n_tok + nt * N_EXPERTS * RUN + N_EXPERTS * (TE - 1)) // TE)

    xp = x_prompt.reshape(np_rows, d)
    xs_pad = jnp.pad(x_sample.reshape(ns, d), ((0, TT - ns), (0, 0)))

    c_all = jnp.concatenate([jnp.pad(c_sample, ((0, TT - ns), (0, 0))),
                             jnp.pad(c_prompt, ((0, SUBLANES - nb % SUBLANES), (0, 0)))])
    mods = _modulation(c_all, w_ada[0], b_ada[0])
    modp = mods[TT:TT + nb].reshape(nb, 1, 6 * d)

    tabs_p, tabs_s = _rotation_tables(t)
    rq, rk, rv, rg, sq, sk, sv, siga, sigb = _inproj(
        xp, xs_pad, modp, mods, norm1_g, w_in[0].astype(BF16), tabs_p, tabs_s, nps=nps, spb=spb, nb=nb)

    dm, qd, kd, cd, gamma = _ret_tables(RET_BLOCK)
    gated_p, st_p = _retention_prompt(rq, rk, rv, rg, (dm, qd, kd, cd), nb=nb, t=t)
    gated_s, st_s = _retention_sample(rq, rk, rv, rg, state_ret[0], gamma, row0=np_rows, ns=ns)
    oswa_p = _swa_prompt(sq, sk, sv, sink[0], nb=nb, t=t)
    to_t = lambda c: jnp.transpose(c[0], (0, 2, 3, 1)).reshape(ns, _SWA_KW, w)
    from_t = lambda c: jnp.transpose(c.reshape(ns, SWA_KV_HEADS, SWA_HD, w), (0, 3, 1, 2))[None]
    oswa_s, ks_new, vs_new = _swa_sample(sq, sk, sv, to_t(cache_swa_k), to_t(cache_swa_v), sink[0],
                                         row0=np_rows, ns=ns)
    gated_s = jnp.pad(gated_s, ((0, TT - ns), (0, 0)))
    oswa_s = jnp.pad(oswa_s, ((0, TT - ns), (0, 0)))

    wr = jnp.pad(jnp.concatenate([w_rg[0], w_re[0]], axis=1), ((0, 0), (0, LANES - N_GROUPS - N_EXPERTS)))
    br = jnp.pad(jnp.concatenate([b_rg[0], b_re[0]]), (0, LANES - N_GROUPS - N_EXPERTS)).reshape(1, LANES)
    x1, h2, rt, lp, cnt, ls, gb = _outproj(
        gated_p, gated_s, oswa_p, oswa_s, siga, sigb, xp, xs_pad, modp, mods, norm2_g,
        w_up_ret[0].astype(BF16), w_up_swa[0].astype(BF16), w_o[0].astype(BF16), wr, br,
        nps=nps, spb=spb, nb=nb, ns=ns)
    cnt = cnt[:, 0, :N_EXPERTS]
    ls = ls[:, 0, :N_EXPERTS]
    gb = gb[:, 0, :N_EXPERTS]
    seg = jnp.sum(cnt, axis=0)
    tiles = (seg + TE - 1) // TE
    tile_end = jnp.cumsum(tiles)
    row_start = (tile_end - tiles) * TE
    gd = (gb + row_start[None, :]).reshape(-1)
    n_used = tile_end[-1:]
    jj = jnp.minimum(jnp.arange(maxt, dtype=I32), n_used[0] - 1)
    te = jnp.minimum(jnp.sum((tile_end[None, :] <= jj[:, None]).astype(I32), axis=1), N_EXPERTS - 1)
    cnt = cnt.reshape(-1)
    ls = ls.reshape(-1)
    n_used = n_used.astype(I32)
    xs = _dispatch(cnt, ls, gd, row_start + seg, tiles * TE - seg, n_used, h2, lp, nt=nt, maxt=maxt)
    has = tiles > 0
    eidx = jnp.arange(N_EXPERTS, dtype=I32)
    later = jnp.where(has[None, :] & (eidx[None, :] > eidx[:, None]), eidx[None, :], N_EXPERTS)
    nxt_e = jnp.min(later, axis=1)
    nxt_e = jnp.where(nxt_e < N_EXPERTS, nxt_e, -1).astype(I32)
    par_e = ((jnp.cumsum(has.astype(I32)) - 1) % 2).astype(I32)
    ys = _experts(te, n_used, (tile_end - tiles).astype(I32), nxt_e, par_e, xs, w1[0], w3[0], w2[0], maxt=maxt)
    y_p, y_s = _combine(cnt, ls, gd, ys, lp, rt, x1, modp, mods, final_g.reshape(1, d), npt=npt, tpb=tpb, nb=nb)

    y_prompt = y_p.reshape(nb, t, d)
    y_sample = y_s[:ns].reshape(ns, 1, d)
    wk = min(WINDOW, t)
    last = lambda a: jnp.stack([a[(b + 1) * t - wk:(b + 1) * t] for b in range(nb)]).reshape(
        nb, wk, SWA_KV_HEADS, SWA_HD)
    skp, svp = last(sk), last(sv)
    return (y_prompt, y_sample, st_p[None], st_s[None], skp[None], svp[None], from_t(ks_new), from_t(vs_new))
```

```python
import functools

import jax
import jax.numpy as jnp
from jax import lax
from jax.experimental import pallas as pl
from jax.experimental.pallas import tpu as pltpu

F32 = jnp.float32
BF16 = jnp.bfloat16
I32 = jnp.int32

PAST_LEN = 8192
RET_HEADS = 4
RET_DK = 128
RET_DV = 128
RET_CHUNK = 128
SWA_HEADS = 8
SWA_KV_HEADS = 2
SWA_HD = 64
WINDOW = 128
ROPE_THETA = 10000.0
N_GROUPS = 4
EXPERTS_PER_GROUP = 8
N_EXPERTS = N_GROUPS * EXPERTS_PER_GROUP
D_EXPERT = 256
NORM_EPS = 1e-6

LANES = 128
SUBLANES = 8
TM = 256
TT = 2 * TM
RUN = 16
SLOTS = 2 * TM + N_EXPERTS * RUN
TE = 512
TE_SUB = 256
VMEM_LIMIT = 56 * 1024 * 1024

_RET_W = RET_HEADS * RET_DK
_SWA_QW = SWA_HEADS * SWA_HD
_SWA_KW = SWA_KV_HEADS * SWA_HD


def _cparams(*sem):
    return pltpu.CompilerParams(dimension_semantics=sem, vmem_limit_bytes=VMEM_LIMIT)


def _sigmoid(x):
    return 1.0 / (1.0 + jnp.exp(-x))


def _silu(x):
    return x * _sigmoid(x)


def _bdot(a, b):
    return jnp.dot(a.astype(BF16), b.astype(BF16), preferred_element_type=F32)


def _bdot_nt(a, b):
    return lax.dot_general(a.astype(BF16), b.astype(BF16), (((1,), (1,)), ((), ())), preferred_element_type=F32)


def _mod_kernel(c_ref, w_ref, b_ref, o_ref):
    o_ref[...] = _bdot(_silu(c_ref[...]), w_ref[...]) + b_ref[...]


def _modulation(c_all, w_ada, b_ada):
    rows, d = c_all.shape
    n = w_ada.shape[1]
    return pl.pallas_call(
        _mod_kernel,
        out_shape=jax.ShapeDtypeStruct((rows, n), F32),
        grid=(n // d,),
        in_specs=[pl.BlockSpec((rows, d), lambda j: (0, 0)),
                  pl.BlockSpec((d, d), lambda j: (0, j)),
                  pl.BlockSpec((1, d), lambda j: (0, j))],
        out_specs=pl.BlockSpec((rows, d), lambda j: (0, j)),
        compiler_params=_cparams("arbitrary"),
    )(c_all, w_ada, b_ada.reshape(1, n))


def _rms(x, g):
    return x * lax.rsqrt(jnp.mean(x * x, axis=-1, keepdims=True) + NORM_EPS) * g


def _pair_rotate(z, cos, sin_signed):
    n = z.shape[-1]
    lane = lax.broadcasted_iota(I32, z.shape, 1)
    partner = jnp.where((lane & 1) == 0, pltpu.roll(z, n - 1, 1), pltpu.roll(z, 1, 1))
    reps = n // LANES
    cos = jnp.concatenate([cos] * reps, axis=1) if reps > 1 else cos
    sin_signed = jnp.concatenate([sin_signed] * reps, axis=1) if reps > 1 else sin_signed
    return z * cos + partner * sin_signed


def _half_rotate(z, cos, sin_signed):
    n = z.shape[-1]
    half = SWA_HD // 2
    lane = lax.broadcasted_iota(I32, z.shape, 1)
    partner = jnp.where((lane & (SWA_HD - 1)) < half, pltpu.roll(z, n - half, 1), pltpu.roll(z, half, 1))
    reps = n // LANES
    cos = jnp.concatenate([cos] * reps, axis=1) if reps > 1 else cos
    sin_signed = jnp.concatenate([sin_signed] * reps, axis=1) if reps > 1 else sin_signed
    return z * cos + partner * sin_signed


def _inproj_step(x_of, sh_of, sc_of, tabs_of, n1_ref, w_ref, outs):
    rq_o, rk_o, rv_o, rg_o, sq_o, sk_o, sv_o, za_o, zb_o = outs
    d = w_ref.shape[0]
    subs = []
    for r in range(0, TT, TM):
        rows = slice(r, r + TM)
        h = (_rms(x_of(rows), n1_ref[...]) * (1.0 + sc_of(rows)) + sh_of(rows)).astype(BF16)
        subs.append((rows, h, tabs_of(rows)))

    def seg(h, a, b):
        return jnp.dot(h, w_ref[:, a:b], preferred_element_type=F32)

    o = 0
    for rows, h, (cr, sr, cw, sw) in subs:
        rq_o[rows] = _pair_rotate(seg(h, o, o + _RET_W), cr, sr).astype(BF16)
    o += _RET_W
    for rows, h, (cr, sr, cw, sw) in subs:
        rk_o[rows] = (_pair_rotate(seg(h, o, o + _RET_W), cr, sr) * (RET_DK ** -0.5)).astype(BF16)
    o += _RET_W
    for rows, h, _ in subs:
        rv_o[rows] = seg(h, o, o + _RET_W).astype(BF16)
    o += _RET_W
    for rows, h, _ in subs:
        rg_o[rows] = _silu(seg(h, o, o + _RET_W)).astype(BF16)
    o += _RET_W
    for rows, h, (cr, sr, cw, sw) in subs:
        sq_o[rows] = (_half_rotate(seg(h, o, o + _SWA_QW), cw, sw) * (SWA_HD ** -0.5)).astype(BF16)
    o += _SWA_QW
    for rows, h, (cr, sr, cw, sw) in subs:
        zkv = seg(h, o, o + 2 * _SWA_KW)
        sk_o[rows] = _half_rotate(zkv[:, :_SWA_KW], cw, sw)
        sv_o[rows] = zkv[:, _SWA_KW:]
    o += 2 * _SWA_KW
    for rows, h, _ in subs:
        za_o[rows] = _sigmoid(seg(h, o, o + d)).astype(BF16)
    o += d
    for rows, h, _ in subs:
        zb_o[rows] = _sigmoid(seg(h, o, o + d)).astype(BF16)


def _inproj_kernel(xp_ref, xs_ref, shp_ref, scp_ref, shs_ref, scs_ref, n1_ref, w_ref,
                   crp_ref, srp_ref, cwp_ref, swp_ref, crs_ref, srs_ref, cws_ref, sws_ref, *outs, nps):
    is_s = pl.program_id(0) >= nps

    @pl.when(jnp.logical_not(is_s))
    def _():
        _inproj_step(lambda rows: xp_ref[rows], lambda rows: shp_ref[...], lambda rows: scp_ref[...],
                     lambda rows: (crp_ref[rows], srp_ref[rows], cwp_ref[rows], swp_ref[rows]),
                     n1_ref, w_ref, outs)

    @pl.when(is_s)
    def _():
        _inproj_step(lambda rows: xs_ref[rows], lambda rows: shs_ref[rows], lambda rows: scs_ref[rows],
                     lambda rows: (crs_ref[...], srs_ref[...], cws_ref[...], sws_ref[...]),
                     n1_ref, w_ref, outs)


def _inproj(xp, xs_pad, modp, mods, n1, w_in_b, tabs_p, tabs_s, *, nps, spb, nb):
    d = xp.shape[1]
    nrow = (nps + 1) * TT
    n_in = w_in_b.shape[1]
    pstep = lambda i: (jnp.minimum(i, nps - 1), 0)
    pbatch = lambda col: (lambda i: (jnp.minimum(i // spb, nb - 1), 0, col))
    tab_idx = lambda i: (jnp.where(i < nps, i % spb, 0), 0)
    out_cols = [(_RET_W, BF16)] * 4 + [(_SWA_QW, BF16), (_SWA_KW, F32), (_SWA_KW, F32), (d, BF16), (d, BF16)]
    return pl.pallas_call(
        functools.partial(_inproj_kernel, nps=nps),
        out_shape=[jax.ShapeDtypeStruct((nrow, c), t) for c, t in out_cols],
        grid=(nps + 1,),
        in_specs=[pl.BlockSpec((TT, d), pstep),
                  pl.BlockSpec((TT, d), lambda i: (0, 0)),
                  pl.BlockSpec((None, 1, d), pbatch(0)),
                  pl.BlockSpec((None, 1, d), pbatch(1)),
                  pl.BlockSpec((TT, d), lambda i: (0, 0)),
                  pl.BlockSpec((TT, d), lambda i: (0, 1)),
                  pl.BlockSpec((1, d), lambda i: (0, 0)),
                  pl.BlockSpec((d, n_in), lambda i: (0, 0))]
                 + [pl.BlockSpec((TT, LANES), tab_idx)] * 4
                 + [pl.BlockSpec((1, LANES), lambda i: (0, 0))] * 4,
        out_specs=[pl.BlockSpec((TT, c), lambda i: (i, 0)) for c, _ in out_cols],
        compiler_params=_cparams("arbitrary"),
    )(xp, xs_pad, modp, modp, mods, mods, n1, w_in_b, *tabs_p, *tabs_s)


RET_BLOCK = 512


def _ret_kernel(q_ref, k_ref, v_ref, g_ref, dm_ref, qd_ref, kd_ref, cd_ref, o_ref, st_ref, s_scr, *, nsteps):
    step = pl.program_id(1)

    @pl.when(step == 0)
    def _():
        s_scr[...] = jnp.zeros_like(s_scr)

    for h in range(RET_HEADS):
        sl = slice(h * RET_DK, (h + 1) * RET_DK)
        state = s_scr[h]
        q, k, v = q_ref[:, sl], k_ref[:, sl], v_ref[:, sl]
        att = _bdot_nt(q, k) * dm_ref[h]
        o = _bdot(att, v) + _bdot(q.astype(F32) * qd_ref[h], state)
        kd = (k.astype(F32) * kd_ref[h]).astype(BF16)
        kv = lax.dot_general(kd, v, (((0,), (0,)), ((), ())), preferred_element_type=F32)
        s_scr[h] = cd_ref[h] * state + kv
        o = o * lax.rsqrt(jnp.mean(o * o, axis=-1, keepdims=True) + NORM_EPS)
        o_ref[:, sl] = (o * g_ref[:, sl].astype(F32)).astype(BF16)

    @pl.when(step == nsteps - 1)
    def _():
        st_ref[...] = s_scr[...]


def _ret_tables(chunk):
    ld = jnp.log(1.0 - 2.0 ** (-5.0 - jnp.arange(RET_HEADS, dtype=F32)))
    idx = jnp.arange(chunk, dtype=F32)
    diff = idx[:, None] - idx[None, :]
    causal = diff >= 0
    dmask = jnp.where(causal[None], jnp.exp(ld[:, None, None] * jnp.where(causal, diff, 0.0)[None]), 0.0)
    k_dec = jnp.exp(ld[None, :] * (chunk - 1.0 - idx)[:, None])
    q_dec = jnp.exp(ld[None, :] * (idx + 1.0)[:, None])
    chunk_decay = jnp.exp(ld * chunk)
    bc = lambda t: jnp.broadcast_to(t.T[:, :, None], (RET_HEADS, chunk, RET_DV))
    cd = jnp.broadcast_to(chunk_decay[:, None, None], (RET_HEADS, 1, RET_DV))
    return dmask, bc(q_dec), bc(k_dec), cd, jnp.exp(ld)


def _retention_prompt(rq, rk, rv, rg, tabs, *, nb, t):
    rows = RET_BLOCK
    nsteps = t // rows
    dm, qd, kd, cd = tabs
    blk = lambda b, c: (b * nsteps + c, 0)
    full3 = lambda b, c: (0, 0, 0)
    return pl.pallas_call(
        functools.partial(_ret_kernel, nsteps=nsteps),
        out_shape=[jax.ShapeDtypeStruct((nb * t, _RET_W), BF16),
                   jax.ShapeDtypeStruct((nb, RET_HEADS, RET_DK, RET_DV), F32)],
        grid=(nb, nsteps),
        in_specs=[pl.BlockSpec((rows, _RET_W), blk)] * 4
                 + [pl.BlockSpec((RET_HEADS, rows, rows), full3)]
                 + [pl.BlockSpec((RET_HEADS, rows, RET_DV), full3)] * 2
                 + [pl.BlockSpec((RET_HEADS, 1, RET_DV), full3)],
        out_specs=[pl.BlockSpec((rows, _RET_W), blk),
                   pl.BlockSpec((None, RET_HEADS, RET_DK, RET_DV), lambda b, c: (b, 0, 0, 0))],
        scratch_shapes=[pltpu.VMEM((RET_HEADS, RET_DK, RET_DV), F32)],
        compiler_params=_cparams("arbitrary", "arbitrary"),
    )(rq, rk, rv, rg, dm, qd, kd, cd)


def _ret_sample_kernel(gam_ref, q_ref, k_ref, v_ref, g_ref, s0_ref, o_ref, st_ref, *, sb):
    gamma = gam_ref[pl.program_id(1)]
    q = q_ref[...].astype(F32)
    k = k_ref[...].astype(F32)
    v = v_ref[...].astype(F32)
    rows = sb * RET_DK
    s2 = s0_ref[...].reshape(rows, RET_DV)
    col_b = lax.broadcasted_iota(I32, (sb, rows), 1) // RET_DK
    row_b = lax.broadcasted_iota(I32, (sb, rows), 0)
    qexp = jnp.where(col_b == row_b, jnp.concatenate([q * gamma] * sb, axis=1), 0.0)
    o = jnp.sum(q * k, axis=-1, keepdims=True) * v + _bdot(qexp, s2)
    o = o * lax.rsqrt(jnp.mean(o * o, axis=-1, keepdims=True) + NORM_EPS)
    o_ref[...] = (o * g_ref[...].astype(F32)).astype(BF16)
    rep = (lax.broadcasted_iota(I32, (rows, sb), 0) // RET_DK == lax.broadcasted_iota(I32, (rows, sb), 1))
    rep = rep.astype(BF16)
    krep = _bdot(rep, k)
    vrep = _bdot(rep, v)
    eye = (lax.broadcasted_iota(I32, (rows, RET_DK), 0) % RET_DK == lax.broadcasted_iota(I32, (rows, RET_DK), 1))
    kcol = jnp.sum(jnp.where(eye, krep, 0.0), axis=-1, keepdims=True)
    st_ref[...] = (gamma * s2 + kcol * vrep).reshape(sb, RET_DK, RET_DV)


def _retention_sample(rq, rk, rv, rg, s0, gamma, *, row0, ns):
    sb = min(64, ns)
    base = row0 // sb
    blk = lambda i, h: (base + i, h)
    sblk = lambda i, h: (i, h, 0, 0)
    return pl.pallas_call(
        functools.partial(_ret_sample_kernel, sb=sb),
        out_shape=[jax.ShapeDtypeStruct((ns, _RET_W), BF16),
                   jax.ShapeDtypeStruct(s0.shape, F32)],
        grid=(ns // sb, RET_HEADS),
        in_specs=[pl.BlockSpec(memory_space=pltpu.SMEM)]
                 + [pl.BlockSpec((sb, RET_DK), blk)] * 4
                 + [pl.BlockSpec((sb, None, RET_DK, RET_DV), sblk)],
        out_specs=[pl.BlockSpec((sb, RET_DV), lambda i, h: (i, h)),
                   pl.BlockSpec((sb, None, RET_DK, RET_DV), sblk)],
        compiler_params=_cparams("arbitrary", "arbitrary"),
    )(gamma, rq, rk, rv, rg, s0)


def _sink_softmax(s, mask, sink):
    if mask is not None:
        s = jnp.where(mask, s, -jnp.inf)
    m = jnp.maximum(jnp.max(s, axis=-1, keepdims=True), sink)
    p = jnp.exp(s - m)
    return p / (jnp.sum(p, axis=-1, keepdims=True) + jnp.exp(sink - m))


def _split_kv_heads(x):
    lo = lax.broadcasted_iota(I32, x.shape, 1) < SWA_HD
    h0_lo = jnp.where(lo, x, 0.0)
    h1_hi = jnp.where(lo, 0.0, x)
    return ((h0_lo, pltpu.roll(h0_lo, SWA_HD, 1)), (pltpu.roll(h1_hi, SWA_HD, 1), h1_hi))


SWA_STEP_BLOCKS = 4


def _swa_kernel(sink_ref, q_ref, kc_ref, kp_ref, vc_ref, vp_ref, o_ref):
    n = pl.program_id(1)
    c = WINDOW
    kk = jnp.concatenate([kp_ref[...], kc_ref[...]], axis=0)
    vv = jnp.concatenate([vp_ref[...], vc_ref[...]], axis=0)
    ks = [[a.astype(BF16) for a in pair] for pair in _split_kv_heads(kk)]
    vs = [[a.astype(BF16) for a in pair] for pair in _split_kv_heads(vv)]
    qi = lax.broadcasted_iota(I32, (2 * c, c), 0) % c
    ki = lax.broadcasted_iota(I32, (2 * c, c), 1)
    from_prev = ki > qi
    top = lax.broadcasted_iota(I32, (2 * c, 1), 0) < c
    units = [(s, kvh) for s in range(SWA_STEP_BLOCKS) for kvh in range(SWA_KV_HEADS)]
    scores = []
    for s, kvh in units:
        rows, keys = slice(s * c, (s + 1) * c), slice(s * c, (s + 2) * c)
        q2 = jnp.concatenate([q_ref[rows, 2 * kvh * LANES:(2 * kvh + 1) * LANES],
                              q_ref[rows, (2 * kvh + 1) * LANES:(2 * kvh + 2) * LANES]], axis=0)
        kcat = jnp.concatenate([ks[kvh][0][keys], ks[kvh][1][keys]], axis=0)
        scores.append(lax.dot_general(q2, kcat, (((1,), (1,)), ((), ())), preferred_element_type=F32))
    probs = []
    for (s, kvh), sc in zip(units, scores):
        ps = []
        for half in range(2):
            sink = jnp.where(top, sink_ref[4 * kvh + half], sink_ref[4 * kvh + 2 + half])
            s_prev = sc[:, half * 2 * c:half * 2 * c + c]
            s_own = sc[:, half * 2 * c + c:(half + 1) * 2 * c]
            if s == 0:
                s_prev = jnp.where(n > 0, s_prev, -jnp.inf)
            p = _sink_softmax(jnp.where(from_prev, s_prev, s_own), None, sink)
            ps += [jnp.where(from_prev, p, 0.0).astype(BF16), jnp.where(from_prev, 0.0, p).astype(BF16)]
        probs.append(jnp.concatenate(ps, axis=1))
    for (s, kvh), p in zip(units, probs):
        rows, keys = slice(s * c, (s + 1) * c), slice(s * c, (s + 2) * c)
        vcat = jnp.concatenate([vs[kvh][0][keys], vs[kvh][1][keys]], axis=0)
        o = jnp.dot(p, vcat, preferred_element_type=F32)
        o_ref[rows, 2 * kvh * LANES:(2 * kvh + 1) * LANES] = o[:c].astype(BF16)
        o_ref[rows, (2 * kvh + 1) * LANES:(2 * kvh + 2) * LANES] = o[c:].astype(BF16)


def _swa_prompt(sq, sk, sv, sink, *, nb, t):
    rows = SWA_STEP_BLOCKS * WINDOW
    nsteps = t // rows
    nblk = t // WINDOW
    cur = lambda b, n: (b * nsteps + n, 0)
    prev = lambda b, n: (b * nblk + jnp.maximum(n * SWA_STEP_BLOCKS - 1, 0), 0)
    return pl.pallas_call(
        _swa_kernel,
        out_shape=jax.ShapeDtypeStruct((nb * t, _SWA_QW), BF16),
        grid=(nb, nsteps),
        in_specs=[pl.BlockSpec(memory_space=pltpu.SMEM),
                  pl.BlockSpec((rows, _SWA_QW), cur),
                  pl.BlockSpec((rows, _SWA_KW), cur),
                  pl.BlockSpec((WINDOW, _SWA_KW), prev),
                  pl.BlockSpec((rows, _SWA_KW), cur),
                  pl.BlockSpec((WINDOW, _SWA_KW), prev)],
        out_specs=pl.BlockSpec((rows, _SWA_QW), cur),
        compiler_params=_cparams("arbitrary", "arbitrary"),
    )(sink, sq, sk, sk, sv, sv)


def _swa_sample_kernel(sink_ref, q_ref, kn_ref, vn_ref, kc_ref, vc_ref, o_ref, ko_ref, vo_ref, *, sb, w):
    pad = jnp.zeros((LANES - sb, _SWA_KW), F32)
    knt = jnp.concatenate([kn_ref[...], pad], axis=0).T
    vnt = jnp.concatenate([vn_ref[...], pad], axis=0).T
    kall = jnp.concatenate([kc_ref[b] for b in range(sb)] + [knt], axis=1)
    vall = jnp.concatenate([vc_ref[b] for b in range(sb)] + [vnt], axis=1)
    ncol = sb * w + LANES
    lo = lax.broadcasted_iota(I32, (sb, LANES), 1) < SWA_HD
    group = SWA_HEADS // SWA_KV_HEADS
    pieces = []
    for h in range(SWA_HEADS):
        slab = q_ref[:, (h // 2) * LANES:(h // 2 + 1) * LANES].astype(F32)
        mine = jnp.where(lo, slab, 0.0) if h % 2 == 0 else jnp.where(lo, 0.0, slab)
        pieces.append(mine if (h % 2) == (h // group) else pltpu.roll(mine, SWA_HD, 1))
    qrows = jnp.concatenate(pieces, axis=0)
    nrow = SWA_HEADS * sb
    s = _bdot(qrows, kall)
    rb = lax.broadcasted_iota(I32, (nrow, ncol), 0) % sb
    ci = lax.broadcasted_iota(I32, (nrow, ncol), 1)
    in_cache = (ci < sb * w) & (ci // w == rb) & ((w - ci % w) < WINDOW)
    mask = in_cache | (ci == sb * w + rb)
    sink_col = jnp.concatenate([jnp.full((sb, 1), sink_ref[h], F32) for h in range(SWA_HEADS)], axis=0)
    p = _sink_softmax(s, mask, sink_col)
    o = _bdot_nt(p, vall)
    for j in range(SWA_HEADS // 2):
        acc = jnp.zeros((sb, LANES), F32)
        for half in range(2):
            h = 2 * j + half
            oh = o[h * sb:(h + 1) * sb]
            own = jnp.where(lo, oh, 0.0) if h // group == 0 else jnp.where(lo, 0.0, oh)
            acc = acc + (own if (h // group) == half else pltpu.roll(own, SWA_HD, 1))
        o_ref[:, j * LANES:(j + 1) * LANES] = acc.astype(BF16)
    newest = lax.broadcasted_iota(I32, (_SWA_KW, w), 1) == w - 1
    for b in range(sb):
        ko_ref[b] = jnp.where(newest, knt[:, b:b + 1], pltpu.roll(kc_ref[b], w - 1, 1))
        vo_ref[b] = jnp.where(newest, vnt[:, b:b + 1], pltpu.roll(vc_ref[b], w - 1, 1))


def _swa_sample(sq, sk, sv, cache_kt, cache_vt, sink, *, row0, ns):
    sb = min(16, ns)
    w = cache_kt.shape[2]
    base = row0 // sb
    blk = lambda i: (base + i, 0)
    cblk = lambda i: (i, 0, 0)
    cspec = pl.BlockSpec((sb, _SWA_KW, w), cblk)
    return pl.pallas_call(
        functools.partial(_swa_sample_kernel, sb=sb, w=w),
        out_shape=[jax.ShapeDtypeStruct((ns, _SWA_QW), BF16),
                   jax.ShapeDtypeStruct(cache_kt.shape, F32), jax.ShapeDtypeStruct(cache_vt.shape, F32)],
        grid=(ns // sb,),
        in_specs=[pl.BlockSpec(memory_space=pltpu.SMEM),
                  pl.BlockSpec((sb, _SWA_QW), blk),
                  pl.BlockSpec((sb, _SWA_KW), blk),
                  pl.BlockSpec((sb, _SWA_KW), blk),
                  cspec, cspec],
        out_specs=[pl.BlockSpec((sb, _SWA_QW), lambda i: (i, 0)), cspec, cspec],
        compiler_params=_cparams("arbitrary"),
    )(sink, sq, sk, sv, cache_kt, cache_vt)


def _route(logits):
    lane = lax.broadcasted_iota(I32, logits.shape, 1).astype(F32)
    big = float(1 << 20)
    neg = -jnp.inf

    def top(mask):
        v = jnp.max(jnp.where(mask, logits, neg), axis=-1, keepdims=True)
        i = jnp.min(jnp.where(mask & (logits == v), lane, big), axis=-1, keepdims=True)
        return v, i

    gmask = lane < N_GROUPS
    gmax, gsel = top(gmask)
    p_group = 1.0 / jnp.sum(jnp.where(gmask, jnp.exp(logits - gmax), 0.0), axis=-1, keepdims=True)
    first = N_GROUPS + gsel * EXPERTS_PER_GROUP
    emask = (lane >= first) & (lane < first + EXPERTS_PER_GROUP)
    v1, i1 = top(emask)
    v2, i2 = top(emask & (lane != i1))
    t = jnp.exp(v2 - v1)
    w1 = p_group / (1.0 + t)
    return (i1 - N_GROUPS).astype(I32), (i2 - N_GROUPS).astype(I32), w1, w1 * t


def _plan_tile(e1, e2, valid, carry):
    lane = lax.broadcasted_iota(I32, (TM, LANES), 1)
    oh1 = ((lane == e1) & valid).astype(F32)
    oh2 = ((lane == e2) & valid).astype(F32)
    oh = oh1 + oh2
    tri = (lax.broadcasted_iota(I32, (TM, TM), 0) > lax.broadcasted_iota(I32, (TM, TM), 1)).astype(BF16)
    before = _bdot(tri, oh)
    cnt = jnp.sum(oh, axis=0, keepdims=True)
    units = jnp.maximum(jnp.floor((cnt + (RUN - 1)) * (1.0 / RUN)), 1.0)
    upper = (lax.broadcasted_iota(I32, (LANES, LANES), 0) < lax.broadcasted_iota(I32, (LANES, LANES), 1))
    lstart = RUN * _bdot(jnp.broadcast_to(units, (SUBLANES, LANES)), upper.astype(BF16))[0:1]
    slot = lstart + before
    lp1 = jnp.sum(oh1 * slot, axis=-1, keepdims=True)
    lp2 = jnp.sum(oh2 * slot, axis=-1, keepdims=True)
    vcol = valid[:, 0:1]
    lp = jnp.where(lane == 0, jnp.where(vcol, lp1, -1.0), jnp.where(lane == 1, jnp.where(vcol, lp2, -1.0), 0.0))
    base = carry[...]
    carry[...] = base + RUN * units
    return lp, (RUN * units).astype(I32), lstart.astype(I32), base.astype(I32)


def _outproj_step(gated_of, oswa_of, x_of, g1_of, sh_of, sc_of, n_valid, sa_ref, sb_ref, n2_ref, wur_ref, wus_ref,
                  wo_ref, wrc_ref, br_ref, x1_o, h2_o, rt_o, lp_o, cnt_o, ls_o, gb_o, carry):
    subs = [slice(r, r + TM) for r in range(0, TT, TM)]
    merged = []
    for rows in subs:
        y_ret = jnp.dot(gated_of(rows), wur_ref[...], preferred_element_type=F32)
        y_swa = jnp.dot(oswa_of(rows), wus_ref[...], preferred_element_type=F32)
        merged.append((sa_ref[rows].astype(F32) * y_ret + sb_ref[rows].astype(F32) * y_swa).astype(BF16))
    hs = []
    for rows, m in zip(subs, merged):
        x1 = x_of(rows) + g1_of(rows) * jnp.dot(m, wo_ref[...], preferred_element_type=F32)
        x1_o[rows] = x1
        h2 = _rms(x1, n2_ref[...]) * (1.0 + sc_of(rows)) + sh_of(rows)
        hi = h2.astype(BF16)
        h2_o[rows] = hi
        hs.append((hi, (h2 - hi.astype(F32)).astype(BF16)))
    routed = []
    for rows, (hi, lo) in zip(subs, hs):
        both = jnp.dot(hi, wrc_ref[...], preferred_element_type=F32)
        logits = (both[:, :LANES] + both[:, LANES:]
                  + jnp.dot(lo, wrc_ref[:, :LANES], preferred_element_type=F32) + br_ref[...])
        e1, e2, w1, w2 = _route(logits)
        lane = lax.broadcasted_iota(I32, logits.shape, 1)
        rt_o[rows] = jnp.where(lane == 2, w1, jnp.where(lane == 3, w2, 0.0))
        routed.append((e1, e2))
    for sub, (e1, e2) in enumerate(routed):
        row = lax.broadcasted_iota(I32, (TM, LANES), 0) + sub * TM
        valid = (row >= 0) if n_valid is None else (row < n_valid)
        lp_o[sub * TM:(sub + 1) * TM], cnt_o[sub], ls_o[sub], gb_o[sub] = _plan_tile(e1, e2, valid, carry)


def _outproj_kernel(gtp_ref, gts_ref, osp_ref, oss_ref, sa_ref, sb_ref, xp_ref, xs_ref, g1p_ref, shp_ref, scp_ref,
                    g1s_ref, shs_ref, scs_ref, *rest, nps, ns):
    i = pl.program_id(0)
    is_s = i >= nps
    carry = rest[-1]

    @pl.when(i == 0)
    def _():
        carry[...] = jnp.zeros_like(carry)

    @pl.when(jnp.logical_not(is_s))
    def _():
        _outproj_step(lambda rows: gtp_ref[rows], lambda rows: osp_ref[rows], lambda rows: xp_ref[rows],
                      lambda rows: g1p_ref[...], lambda rows: shp_ref[...], lambda rows: scp_ref[...], None,
                      sa_ref, sb_ref, *rest)

    @pl.when(is_s)
    def _():
        _outproj_step(lambda rows: gts_ref[rows], lambda rows: oss_ref[rows], lambda rows: xs_ref[rows],
                      lambda rows: g1s_ref[rows], lambda rows: shs_ref[rows], lambda rows: scs_ref[rows], ns,
                      sa_ref, sb_ref, *rest)


def _outproj(gated_p, gated_s, oswa_p, oswa_s, siga, sigb, xp, xs_pad, modp, mods, n2, wur, wus, wo, wr, br,
             *, nps, spb, nb, ns):
    d = xp.shape[1]
    sub = TT // TM
    nrow = (nps + 1) * TT
    row = lambda i: (i, 0)
    pstep = lambda i: (jnp.minimum(i, nps - 1), 0)
    pbatch = lambda col: (lambda i: (jnp.minimum(i // spb, nb - 1), 0, col))
    scol = lambda col: (lambda i: (0, col))
    const = lambda i: (0, 0)
    wr_hi = wr.astype(BF16)
    wrc = jnp.concatenate([wr_hi, (wr - wr_hi.astype(F32)).astype(BF16)], axis=1)
    meta = jax.ShapeDtypeStruct(((nps + 1) * sub, 1, LANES), I32)
    mspec = pl.BlockSpec((sub, 1, LANES), lambda i: (i, 0, 0))
    return pl.pallas_call(
        functools.partial(_outproj_kernel, nps=nps, ns=ns),
        out_shape=[jax.ShapeDtypeStruct((nrow, d), F32),
                   jax.ShapeDtypeStruct((nrow, d), BF16),
                   jax.ShapeDtypeStruct((nrow, LANES), F32),
                   jax.ShapeDtypeStruct((nrow, LANES), F32), meta, meta, meta],
        grid=(nps + 1,),
        in_specs=[pl.BlockSpec((TT, _RET_W), pstep), pl.BlockSpec((TT, _RET_W), const),
                  pl.BlockSpec((TT, _SWA_QW), pstep), pl.BlockSpec((TT, _SWA_QW), const),
                  pl.BlockSpec((TT, d), row), pl.BlockSpec((TT, d), row),
                  pl.BlockSpec((TT, d), pstep), pl.BlockSpec((TT, d), const),
                  pl.BlockSpec((None, 1, d), pbatch(2)), pl.BlockSpec((None, 1, d), pbatch(3)),
                  pl.BlockSpec((None, 1, d), pbatch(4)),
                  pl.BlockSpec((TT, d), scol(2)), pl.BlockSpec((TT, d), scol(3)), pl.BlockSpec((TT, d), scol(4)),
                  pl.BlockSpec((1, d), const),
                  pl.BlockSpec(wur.shape, const), pl.BlockSpec(wus.shape, const), pl.BlockSpec(wo.shape, const),
                  pl.BlockSpec(wrc.shape, const), pl.BlockSpec((1, LANES), const)],
        out_specs=[pl.BlockSpec((TT, d), row), pl.BlockSpec((TT, d), row), pl.BlockSpec((TT, LANES), row),
                   pl.BlockSpec((TT, LANES), row), mspec, mspec, mspec],
        scratch_shapes=[pltpu.VMEM((1, LANES), F32)],
        compiler_params=_cparams("arbitrary"),
    )(gated_p, gated_s, oswa_p, oswa_s, siga, sigb, xp, xs_pad, modp, modp, modp, mods, mods, mods, n2,
      wur, wus, wo, wrc, br)


def _aligned(v):
    return v if isinstance(v, int) else pl.multiple_of(v, RUN)


def _run_copy(src, dst, s_start, d_start, n, sem):
    s_start, d_start, n = _aligned(s_start), _aligned(d_start), _aligned(n)
    return pltpu.make_async_copy(src.at[pl.ds(s_start, n)], dst.at[pl.ds(d_start, n)], sem)


def _tile_rows(cnt_ref, ls_ref, step):
    last = step * N_EXPERTS + N_EXPERTS - 1
    return ls_ref[last] + cnt_ref[last]


def _each_run(step, fn):
    for e in range(N_EXPERTS):
        fn(step * N_EXPERTS + e)


def _dispatch_kernel(cnt_ref, ls_ref, gd_ref, ps_ref, pn_ref, nu_ref, h_ref, lp_ref, xs_ref,
                     sorted_scr, zero_scr, sem, zsem, *, nt, maxt):
    i = pl.program_id(0)

    def pad(e):
        return _run_copy(zero_scr, xs_ref, 0, ps_ref[e], pn_ref[e], zsem)

    def tail(j):
        return _run_copy(zero_scr, xs_ref, 0, j * TE, TE, zsem)

    def each_pad(fn):
        def body(e, c):
            @pl.when(pn_ref[e] > 0)
            def _():
                fn(pad(e))
            return c
        lax.fori_loop(0, N_EXPERTS, body, 0)

    def each_tail(fn):
        def body(j, c):
            fn(tail(j))
            return c
        lax.fori_loop(nu_ref[0], maxt, body, 0)

    @pl.when(i == 0)
    def _():
        zero_scr[...] = jnp.zeros_like(zero_scr)
        each_pad(lambda cp: cp.start())
        each_tail(lambda cp: cp.start())

    lpt = lp_ref[...].T
    slot = lax.broadcasted_iota(I32, (SLOTS, TM), 0).astype(F32)
    perm = ((slot == lpt[0:1]) | (slot == lpt[1:2])).astype(BF16)
    sorted_scr[i % 2] = jnp.dot(perm, h_ref[...], preferred_element_type=F32).astype(BF16)

    def copy(step):
        return lambda k: _run_copy(sorted_scr.at[step % 2], xs_ref, ls_ref[k], gd_ref[k], cnt_ref[k], sem.at[step % 2])

    def wait_all(step):
        _run_copy(sorted_scr.at[step % 2], xs_ref, 0, 0, _tile_rows(cnt_ref, ls_ref, step), sem.at[step % 2]).wait()

    _each_run(i, lambda k: copy(i)(k).start())

    @pl.when(i > 0)
    def _():
        wait_all(i - 1)

    @pl.when(i == nt - 1)
    def _():
        wait_all(i)
        each_pad(lambda cp: cp.wait())
        each_tail(lambda cp: cp.wait())


def _dispatch(cnt, ls, gd, ps, pn, nu, h2, lp, *, nt, maxt):
    d = h2.shape[1]
    return pl.pallas_call(
        functools.partial(_dispatch_kernel, nt=nt, maxt=maxt),
        out_shape=jax.ShapeDtypeStruct((maxt * TE, d), BF16),
        grid_spec=pltpu.PrefetchScalarGridSpec(
            num_scalar_prefetch=6,
            grid=(nt,),
            in_specs=[pl.BlockSpec((TM, d), lambda i, *_: (i, 0)),
                      pl.BlockSpec((TM, LANES), lambda i, *_: (i, 0))],
            out_specs=pl.BlockSpec(memory_space=pl.ANY),
            scratch_shapes=[pltpu.VMEM((2, SLOTS, d), BF16), pltpu.VMEM((TE, d), BF16),
                            pltpu.SemaphoreType.DMA((2,)), pltpu.SemaphoreType.DMA(())]),
        compiler_params=_cparams("arbitrary"),
    )(cnt, ls, gd, ps, pn, nu, h2, lp)


def _experts_kernel(te_ref, nu_ref, start_ref, nxt_ref, par_ref, x_ref, w1_hbm, w3_hbm, w2_hbm, y_ref,
                    w1f, w3f, w2f, w1b, w3b, w2b, sem):
    j = pl.program_id(0)
    e = te_ref[j]

    def fetch(ex, slot):
        return [pltpu.make_async_copy(src.at[ex], dst.at[slot], sem.at[slot, n])
                for n, (src, dst) in enumerate(((w1_hbm, w1f), (w3_hbm, w3f), (w2_hbm, w2f)))]

    @pl.when(j == 0)
    def _():
        for cp in fetch(e, par_ref[e]):
            cp.start()

    @pl.when((j == start_ref[e]) & (j < nu_ref[0]))
    def _():
        slot = par_ref[e]

        @pl.when(nxt_ref[e] >= 0)
        def _():
            for cp in fetch(nxt_ref[e], 1 - slot):
                cp.start()

        for cp in fetch(e, slot):
            cp.wait()
        w1b[...] = w1f[slot].astype(BF16)
        w3b[...] = w3f[slot].astype(BF16)
        w2b[...] = w2f[slot].astype(BF16)

    @pl.when(j < nu_ref[0])
    def _():
        subs = [slice(r, r + TE_SUB) for r in range(0, TE, TE_SUB)]
        ab = [(jnp.dot(x_ref[rows], w1b[...], preferred_element_type=F32),
               jnp.dot(x_ref[rows], w3b[...], preferred_element_type=F32)) for rows in subs]
        hid = [(_silu(a) * b).astype(BF16) for a, b in ab]
        for rows, h in zip(subs, hid):
            y_ref[rows] = jnp.dot(h, w2b[...], preferred_element_type=F32).astype(BF16)


def _experts(te, nu, start, nxt, par, xs, w1, w3, w2, *, maxt):
    d = xs.shape[1]
    f = w1.shape[2]
    used = lambda j, te, nu, *_: (jnp.minimum(j, nu[0] - 1), 0)
    hbm = pl.BlockSpec(memory_space=pl.ANY)
    return pl.pallas_call(
        _experts_kernel,
        out_shape=jax.ShapeDtypeStruct(xs.shape, BF16),
        grid_spec=pltpu.PrefetchScalarGridSpec(
            num_scalar_prefetch=5,
            grid=(maxt,),
            in_specs=[pl.BlockSpec((TE, d), used), hbm, hbm, hbm],
            out_specs=pl.BlockSpec((TE, d), used),
            scratch_shapes=[pltpu.VMEM((2, d, f), F32), pltpu.VMEM((2, d, f), F32), pltpu.VMEM((2, f, d), F32),
                            pltpu.VMEM((d, f), BF16), pltpu.VMEM((d, f), BF16), pltpu.VMEM((f, d), BF16),
                            pltpu.SemaphoreType.DMA((2, 3))]),
        input_output_aliases={5: 0},
        compiler_params=_cparams("arbitrary"),
    )(te, nu, start, nxt, par, xs, w1, w3, w2)


def _combine_kernel(cnt_ref, ls_ref, gd_ref, ys_ref, lp_ref, rt_ref, x1_ref, g2p_ref, g2s_ref, fg_ref,
                    yp_o, ys_o, ybuf, sem, *, npt):
    i = pl.program_id(0)
    nt = npt + 1

    def fetch(step):
        ybuf[step % 2] = jnp.zeros(ybuf.shape[1:], BF16)
        _each_run(step, lambda k: _run_copy(ys_ref, ybuf.at[step % 2], gd_ref[k], ls_ref[k], cnt_ref[k],
                                            sem.at[step % 2]).start())

    @pl.when(i == 0)
    def _():
        fetch(0)

    @pl.when(i + 1 < nt)
    def _():
        fetch(i + 1)

    _run_copy(ys_ref, ybuf.at[i % 2], 0, 0, _tile_rows(cnt_ref, ls_ref, i), sem.at[i % 2]).wait()

    yb = ybuf[i % 2]
    slot = lax.broadcasted_iota(I32, (TM, SLOTS), 1).astype(F32)
    lp = lp_ref[...]
    rt = rt_ref[...]

    def unsort(col):
        return jnp.dot((slot == lp[:, col:col + 1]).astype(BF16), yb, preferred_element_type=F32)

    moe = rt[:, 2:3] * unsort(0) + rt[:, 3:4] * unsort(1)
    g2 = jnp.where(i >= npt, g2s_ref[...], g2p_ref[...])
    y = _rms(x1_ref[...] + g2 * moe, fg_ref[...])

    @pl.when(i < npt)
    def _():
        yp_o[...] = y

    @pl.when(i >= npt)
    def _():
        ys_o[...] = y


def _combine(cnt, ls, gd, ys, lp, rt, x1, modp, mods, fg, *, npt, tpb, nb):
    d = x1.shape[1]
    nt = npt + 1
    row = lambda i, *_: (i, 0)
    return pl.pallas_call(
        functools.partial(_combine_kernel, npt=npt),
        out_shape=[jax.ShapeDtypeStruct((npt * TM, d), F32), jax.ShapeDtypeStruct((TM, d), F32)],
        grid_spec=pltpu.PrefetchScalarGridSpec(
            num_scalar_prefetch=3,
            grid=(nt,),
            in_specs=[pl.BlockSpec(memory_space=pl.ANY),
                      pl.BlockSpec((TM, LANES), row), pl.BlockSpec((TM, LANES), row), pl.BlockSpec((TM, d), row),
                      pl.BlockSpec((None, 1, d), lambda i, *_: (jnp.minimum(i // tpb, nb - 1), 0, 5)),
                      pl.BlockSpec((TM, d), lambda i, *_: (0, 5)),
                      pl.BlockSpec((1, d), lambda i, *_: (0, 0))],
            out_specs=[pl.BlockSpec((TM, d), lambda i, *_: (jnp.minimum(i, npt - 1), 0)),
                       pl.BlockSpec((TM, d), lambda i, *_: (0, 0))],
            scratch_shapes=[pltpu.VMEM((2, SLOTS, d), BF16), pltpu.SemaphoreType.DMA((2,))]),
        compiler_params=_cparams("arbitrary"),
    )(cnt, ls, gd, ys, lp, rt, x1, modp, mods, fg)


TAB_LO = 64


def _rotation_tables(t):
    inv_r = jnp.repeat(1.0 / (ROPE_THETA ** jnp.linspace(0.0, 1.0, RET_DK // 2, dtype=F32)), 2)
    sign_r = jnp.where(jnp.arange(RET_DK) % 2 == 0, -1.0, 1.0).astype(F32)
    inv_w = jnp.tile(ROPE_THETA ** (-jnp.arange(0, SWA_HD, 2, dtype=F32) / SWA_HD), LANES // (SWA_HD // 2))
    sign_w = jnp.where(jnp.arange(LANES) % SWA_HD < SWA_HD // 2, -1.0, 1.0).astype(F32)
    hi = (jnp.arange(t // TAB_LO, dtype=I32) * TAB_LO).astype(F32)[:, None]
    lo = jnp.arange(TAB_LO, dtype=I32).astype(F32)[:, None]
    past = jnp.full((1, 1), PAST_LEN, F32)

    def pair(inv, sign):
        a, b = hi * inv[None, :], lo * inv[None, :]
        ca, sa, cb, sb = jnp.cos(a)[:, None], jnp.sin(a)[:, None], jnp.cos(b)[None], jnp.sin(b)[None]
        cos = (ca * cb - sa * sb).reshape(t, LANES)
        sin = (sa * cb + ca * sb).reshape(t, LANES) * sign[None, :]
        ang = past * inv[None, :]
        return (cos, sin), (jnp.cos(ang), jnp.sin(ang) * sign[None, :])

    (pr, sr_), (pw, sw_) = pair(inv_r, sign_r), pair(inv_w, sign_w)
    return pr + pw, sr_ + sw_


def kernel(x_prompt, x_sample, c_prompt, c_sample, state_ret, cache_swa_k, cache_swa_v, w_ada, b_ada, norm1_g, norm2_g, w_in, w_up_ret, w_up_swa, w_o, sink, w_rg, b_rg, w_re, b_re, w1, w3, w2, final_g):
    nb, t, d = x_prompt.shape
    ns, dec_seq, _ = x_sample.shape
    depth = w_ada.shape[0]
    assert depth == 1 and dec_seq == 1, "single layer, one new token per sequence"
    assert t % TT == 0 and ns <= TM and ns % 16 == 0 and d % LANES == 0
    assert t % RET_BLOCK == 0 and t % (SWA_STEP_BLOCKS * WINDOW) == 0 and t % TAB_LO == 0
    assert N_GROUPS + N_EXPERTS <= LANES
    w = cache_swa_k.shape[2]
    tpb = t // TM
    npt = nb * tpb
    spb = t // TT
    nps = nb * spb
    nt = (nps + 1) * (TT // TM)
    np_rows = nb * t
    n_tok = np_rows + ns
    maxt = -(-(2 * n_tok + nt * N_EXPERTS * RUN + N_EXPERTS * (TE - 1)) // TE)

    xp = x_prompt.reshape(np_rows, d)
    xs_pad = jnp.pad(x_sample.reshape(ns, d), ((0, TT - ns), (0, 0)))

    c_all = jnp.concatenate([jnp.pad(c_sample, ((0, TT - ns), (0, 0))),
                             jnp.pad(c_prompt, ((0, SUBLANES - nb % SUBLANES), (0, 0)))])
    mods = _modulation(c_all, w_ada[0], b_ada[0])
    modp = mods[TT:TT + nb].reshape(nb, 1, 6 * d)

    tabs_p, tabs_s = _rotation_tables(t)
    rq, rk, rv, rg, sq, sk, sv, siga, sigb = _inproj(
        xp, xs_pad, modp, mods, norm1_g, w_in[0].astype(BF16), tabs_p, tabs_s, nps=nps, spb=spb, nb=nb)

    dm, qd, kd, cd, gamma = _ret_tables(RET_BLOCK)
    gated_p, st_p = _retention_prompt(rq, rk, rv, rg, (dm, qd, kd, cd), nb=nb, t=t)
    gated_s, st_s = _retention_sample(rq, rk, rv, rg, state_ret[0], gamma, row0=np_rows, ns=ns)
    oswa_p = _swa_prompt(sq, sk, sv, sink[0], nb=nb, t=t)
    to_t = lambda c: jnp.transpose(c[0], (0, 2, 3, 1)).reshape(ns, _SWA_KW, w)
    from_t = lambda c: jnp.transpose(c.reshape(ns, SWA_KV_HEADS, SWA_HD, w), (0, 3, 1, 2))[None]
    oswa_s, ks_new, vs_new = _swa_sample(sq, sk, sv, to_t(cache_swa_k), to_t(cache_swa_v), sink[0],
                                         row0=np_rows, ns=ns)
    gated_s = jnp.pad(gated_s, ((0, TT - ns), (0, 0)))
    oswa_s = jnp.pad(oswa_s, ((0, TT - ns), (0, 0)))

    wr = jnp.pad(jnp.concatenate([w_rg[0], w_re[0]], axis=1), ((0, 0), (0, LANES - N_GROUPS - N_EXPERTS)))
    br = jnp.pad(jnp.concatenate([b_rg[0], b_re[0]]), (0, LANES - N_GROUPS - N_EXPERTS)).reshape(1, LANES)
    x1, h2, rt, lp, cnt, ls, gb = _outproj(
        gated_p, gated_s, oswa_p, oswa_s, siga, sigb, xp, xs_pad, modp, mods, norm2_g,
        w_up_ret[0].astype(BF16), w_up_swa[0].astype(BF16), w_o[0].astype(BF16), wr, br,
        nps=nps, spb=spb, nb=nb, ns=ns)
    cnt = cnt[:, 0, :N_EXPERTS]
    ls = ls[:, 0, :N_EXPERTS]
    gb = gb[:, 0, :N_EXPERTS]
    seg = jnp.sum(cnt, axis=0)
    tiles = (seg + TE - 1) // TE
    tile_end = jnp.cumsum(tiles)
    row_start = (tile_end - tiles) * TE
    gd = (gb + row_start[None, :]).reshape(-1)
    n_used = tile_end[-1:]
    jj = jnp.minimum(jnp.arange(maxt, dtype=I32), n_used[0] - 1)
    te = jnp.minimum(jnp.sum((tile_end[None, :] <= jj[:, None]).astype(I32), axis=1), N_EXPERTS - 1)
    cnt = cnt.reshape(-1)
    ls = ls.reshape(-1)
    n_used = n_used.astype(I32)
    xs = _dispatch(cnt, ls, gd, row_start + seg, tiles * TE - seg, n_used, h2, lp, nt=nt, maxt=maxt)
    has = tiles > 0
    eidx = jnp.arange(N_EXPERTS, dtype=I32)
    later = jnp.where(has[None, :] & (eidx[None, :] > eidx[:, None]), eidx[None, :], N_EXPERTS)
    nxt_e = jnp.min(later, axis=1)
    nxt_e = jnp.where(nxt_e < N_EXPERTS, nxt_e, -1).astype(I32)
    par_e = ((jnp.cumsum(has.astype(I32)) - 1) % 2).astype(I32)
    ys = _experts(te, n_used, (tile_end - tiles).astype(I32), nxt_e, par_e, xs, w1[0], w3[0], w2[0], maxt=maxt)
    y_p, y_s = _combine(cnt, ls, gd, ys, lp, rt, x1, modp, mods, final_g.reshape(1, d), npt=npt, tpb=tpb, nb=nb)

    y_prompt = y_p.reshape(nb, t, d)
    y_sample = y_s[:ns].reshape(ns, 1, d)
    wk = min(WINDOW, t)
    last = lambda a: jnp.stack([a[(b + 1) * t - wk:(b + 1) * t] for b in range(nb)]).reshape(
        nb, wk, SWA_KV_HEADS, SWA_HD)
    skp, svp = last(sk), last(sv)
    return (y_prompt, y_sample, st_p[None], st_s[None], skp[None], svp[None], from_t(ks_new), from_t(vs_new))
```

```python
import functools

import jax
import jax.numpy as jnp
from jax import lax
from jax.experimental import pallas as pl
from jax.experimental.pallas import tpu as pltpu

F32 = jnp.float32
BF16 = jnp.bfloat16
I32 = jnp.int32

PAST_LEN = 8192
RET_HEADS = 4
RET_DK = 128
RET_DV = 128
RET_CHUNK = 128
SWA_HEADS = 8
SWA_KV_HEADS = 2
SWA_HD = 64
WINDOW = 128
ROPE_THETA = 10000.0
N_GROUPS = 4
EXPERTS_PER_GROUP = 8
N_EXPERTS = N_GROUPS * EXPERTS_PER_GROUP
D_EXPERT = 256
NORM_EPS = 1e-6

LANES = 128
SUBLANES = 8
TM = 256
TT = 2 * TM
RUN = 16
SLOTS = 2 * TM + N_EXPERTS * RUN
TE = 512
TE_SUB = 256
VMEM_LIMIT = 56 * 1024 * 1024

_RET_W = RET_HEADS * RET_DK
_SWA_QW = SWA_HEADS * SWA_HD
_SWA_KW = SWA_KV_HEADS * SWA_HD


def _cparams(*sem):
    return pltpu.CompilerParams(dimension_semantics=sem, vmem_limit_bytes=VMEM_LIMIT)


def _sigmoid(x):
    return 1.0 / (1.0 + jnp.exp(-x))


def _silu(x):
    return x * _sigmoid(x)


def _bdot(a, b):
    return jnp.dot(a.astype(BF16), b.astype(BF16), preferred_element_type=F32)


def _bdot_nt(a, b):
    return lax.dot_general(a.astype(BF16), b.astype(BF16), (((1,), (1,)), ((), ())), preferred_element_type=F32)


def _mod_kernel(c_ref, w_ref, b_ref, o_ref):
    o_ref[...] = _bdot(_silu(c_ref[...]), w_ref[...]) + b_ref[...]


def _modulation(c_all, w_ada, b_ada):
    rows, d = c_all.shape
    n = w_ada.shape[1]
    bn = 4 * LANES
    return pl.pallas_call(
        _mod_kernel,
        out_shape=jax.ShapeDtypeStruct((rows, n), F32),
        grid=(n // bn,),
        in_specs=[pl.BlockSpec((rows, d), lambda j: (0, 0)),
                  pl.BlockSpec((d, bn), lambda j: (0, j)),
                  pl.BlockSpec((1, bn), lambda j: (0, j))],
        out_specs=pl.BlockSpec((rows, bn), lambda j: (0, j)),
        compiler_params=_cparams("arbitrary"),
    )(c_all, w_ada, b_ada.reshape(1, n))


def _rms(x, g):
    return x * lax.rsqrt(jnp.mean(x * x, axis=-1, keepdims=True) + NORM_EPS) * g


def _pair_rotate(z, cos, sin_signed):
    n = z.shape[-1]
    lane = lax.broadcasted_iota(I32, z.shape, 1)
    partner = jnp.where((lane & 1) == 0, pltpu.roll(z, n - 1, 1), pltpu.roll(z, 1, 1))
    reps = n // LANES
    cos = jnp.concatenate([cos] * reps, axis=1) if reps > 1 else cos
    sin_signed = jnp.concatenate([sin_signed] * reps, axis=1) if reps > 1 else sin_signed
    return z * cos + partner * sin_signed


def _half_rotate(z, cos, sin_signed):
    n = z.shape[-1]
    half = SWA_HD // 2
    lane = lax.broadcasted_iota(I32, z.shape, 1)
    partner = jnp.where((lane & (SWA_HD - 1)) < half, pltpu.roll(z, n - half, 1), pltpu.roll(z, half, 1))
    reps = n // LANES
    cos = jnp.concatenate([cos] * reps, axis=1) if reps > 1 else cos
    sin_signed = jnp.concatenate([sin_signed] * reps, axis=1) if reps > 1 else sin_signed
    return z * cos + partner * sin_signed


def _inproj_step(x_of, sh_of, sc_of, tabs_of, n1_ref, w_ref, outs):
    rq_o, rk_o, rv_o, rg_o, sq_o, sk_o, sv_o, za_o, zb_o = outs
    d = w_ref.shape[0]
    subs = []
    for r in range(0, TT, TM):
        rows = slice(r, r + TM)
        h = (_rms(x_of(rows), n1_ref[...]) * (1.0 + sc_of(rows)) + sh_of(rows)).astype(BF16)
        subs.append((rows, h, tabs_of(rows)))

    def seg(h, a, b):
        return jnp.dot(h, w_ref[:, a:b], preferred_element_type=F32)

    o = 0
    for rows, h, (cr, sr, cw, sw) in subs:
        rq_o[rows] = _pair_rotate(seg(h, o, o + _RET_W), cr, sr).astype(BF16)
    o += _RET_W
    for rows, h, (cr, sr, cw, sw) in subs:
        rk_o[rows] = (_pair_rotate(seg(h, o, o + _RET_W), cr, sr) * (RET_DK ** -0.5)).astype(BF16)
    o += _RET_W
    for rows, h, _ in subs:
        rv_o[rows] = seg(h, o, o + _RET_W).astype(BF16)
    o += _RET_W
    for rows, h, _ in subs:
        rg_o[rows] = _silu(seg(h, o, o + _RET_W)).astype(BF16)
    o += _RET_W
    for rows, h, (cr, sr, cw, sw) in subs:
        sq_o[rows] = (_half_rotate(seg(h, o, o + _SWA_QW), cw, sw) * (SWA_HD ** -0.5)).astype(BF16)
    o += _SWA_QW
    for rows, h, (cr, sr, cw, sw) in subs:
        zkv = seg(h, o, o + 2 * _SWA_KW)
        sk_o[rows] = _half_rotate(zkv[:, :_SWA_KW], cw, sw)
        sv_o[rows] = zkv[:, _SWA_KW:]
    o += 2 * _SWA_KW
    for rows, h, _ in subs:
        za_o[rows] = _sigmoid(seg(h, o, o + d)).astype(BF16)
    o += d
    for rows, h, _ in subs:
        zb_o[rows] = _sigmoid(seg(h, o, o + d)).astype(BF16)


def _inproj_kernel(xp_ref, xs_ref, shp_ref, scp_ref, shs_ref, scs_ref, n1_ref, w_ref,
                   crp_ref, srp_ref, cwp_ref, swp_ref, crs_ref, srs_ref, cws_ref, sws_ref, *outs, nps):
    is_s = pl.program_id(0) >= nps

    @pl.when(jnp.logical_not(is_s))
    def _():
        _inproj_step(lambda rows: xp_ref[rows], lambda rows: shp_ref[...], lambda rows: scp_ref[...],
                     lambda rows: (crp_ref[rows], srp_ref[rows], cwp_ref[rows], swp_ref[rows]),
                     n1_ref, w_ref, outs)

    @pl.when(is_s)
    def _():
        _inproj_step(lambda rows: xs_ref[rows], lambda rows: shs_ref[rows], lambda rows: scs_ref[rows],
                     lambda rows: (crs_ref[...], srs_ref[...], cws_ref[...], sws_ref[...]),
                     n1_ref, w_ref, outs)


def _inproj(xp, xs_pad, modp, mods, n1, w_in_b, tabs_p, tabs_s, *, nps, spb, nb):
    d = xp.shape[1]
    nrow = (nps + 1) * TT
    n_in = w_in_b.shape[1]
    pstep = lambda i: (jnp.minimum(i, nps - 1), 0)
    pbatch = lambda col: (lambda i: (jnp.minimum(i // spb, nb - 1), 0, col))
    tab_idx = lambda i: (jnp.where(i < nps, i % spb, 0), 0)
    out_cols = [(_RET_W, BF16)] * 4 + [(_SWA_QW, BF16), (_SWA_KW, F32), (_SWA_KW, F32), (d, BF16), (d, BF16)]
    return pl.pallas_call(
        functools.partial(_inproj_kernel, nps=nps),
        out_shape=[jax.ShapeDtypeStruct((nrow, c), t) for c, t in out_cols],
        grid=(nps + 1,),
        in_specs=[pl.BlockSpec((TT, d), pstep),
                  pl.BlockSpec((TT, d), lambda i: (0, 0)),
                  pl.BlockSpec((None, 1, d), pbatch(0)),
                  pl.BlockSpec((None, 1, d), pbatch(1)),
                  pl.BlockSpec((TT, d), lambda i: (0, 0)),
                  pl.BlockSpec((TT, d), lambda i: (0, 1)),
                  pl.BlockSpec((1, d), lambda i: (0, 0)),
                  pl.BlockSpec((d, n_in), lambda i: (0, 0))]
                 + [pl.BlockSpec((TT, LANES), tab_idx)] * 4
                 + [pl.BlockSpec((1, LANES), lambda i: (0, 0))] * 4,
        out_specs=[pl.BlockSpec((TT, c), lambda i: (i, 0)) for c, _ in out_cols],
        compiler_params=_cparams("arbitrary"),
    )(xp, xs_pad, modp, modp, mods, mods, n1, w_in_b, *tabs_p, *tabs_s)


RET_BLOCK = 512


def _ret_kernel(q_ref, k_ref, v_ref, g_ref, dm_ref, qd_ref, kd_ref, cd_ref, o_ref, st_ref, s_scr, *, nsteps):
    step = pl.program_id(1)

    @pl.when(step == 0)
    def _():
        s_scr[...] = jnp.zeros_like(s_scr)

    for h in range(RET_HEADS):
        sl = slice(h * RET_DK, (h + 1) * RET_DK)
        state = s_scr[h]
        q, k, v = q_ref[:, sl], k_ref[:, sl], v_ref[:, sl]
        att = _bdot_nt(q, k) * dm_ref[h]
        o = _bdot(att, v) + _bdot(q.astype(F32) * qd_ref[h], state)
        kd = (k.astype(F32) * kd_ref[h]).astype(BF16)
        kv = lax.dot_general(kd, v, (((0,), (0,)), ((), ())), preferred_element_type=F32)
        s_scr[h] = cd_ref[h] * state + kv
        o = o * lax.rsqrt(jnp.mean(o * o, axis=-1, keepdims=True) + NORM_EPS)
        o_ref[:, sl] = (o * g_ref[:, sl].astype(F32)).astype(BF16)

    @pl.when(step == nsteps - 1)
    def _():
        st_ref[...] = s_scr[...]


def _ret_tables(chunk):
    ld = jnp.log(1.0 - 2.0 ** (-5.0 - jnp.arange(RET_HEADS, dtype=F32)))
    idx = jnp.arange(chunk, dtype=F32)
    diff = idx[:, None] - idx[None, :]
    causal = diff >= 0
    dmask = jnp.where(causal[None], jnp.exp(ld[:, None, None] * jnp.where(causal, diff, 0.0)[None]), 0.0)
    k_dec = jnp.exp(ld[None, :] * (chunk - 1.0 - idx)[:, None])
    q_dec = jnp.exp(ld[None, :] * (idx + 1.0)[:, None])
    chunk_decay = jnp.exp(ld * chunk)
    bc = lambda t: jnp.broadcast_to(t.T[:, :, None], (RET_HEADS, chunk, RET_DV))
    cd = jnp.broadcast_to(chunk_decay[:, None, None], (RET_HEADS, 1, RET_DV))
    return dmask, bc(q_dec), bc(k_dec), cd, jnp.exp(ld)


def _retention_prompt(rq, rk, rv, rg, tabs, *, nb, t):
    rows = RET_BLOCK
    nsteps = t // rows
    dm, qd, kd, cd = tabs
    blk = lambda b, c: (b * nsteps + c, 0)
    full3 = lambda b, c: (0, 0, 0)
    return pl.pallas_call(
        functools.partial(_ret_kernel, nsteps=nsteps),
        out_shape=[jax.ShapeDtypeStruct((nb * t, _RET_W), BF16),
                   jax.ShapeDtypeStruct((nb, RET_HEADS, RET_DK, RET_DV), F32)],
        grid=(nb, nsteps),
        in_specs=[pl.BlockSpec((rows, _RET_W), blk)] * 4
                 + [pl.BlockSpec((RET_HEADS, rows, rows), full3)]
                 + [pl.BlockSpec((RET_HEADS, rows, RET_DV), full3)] * 2
                 + [pl.BlockSpec((RET_HEADS, 1, RET_DV), full3)],
        out_specs=[pl.BlockSpec((rows, _RET_W), blk),
                   pl.BlockSpec((None, RET_HEADS, RET_DK, RET_DV), lambda b, c: (b, 0, 0, 0))],
        scratch_shapes=[pltpu.VMEM((RET_HEADS, RET_DK, RET_DV), F32)],
        compiler_params=_cparams("arbitrary", "arbitrary"),
    )(rq, rk, rv, rg, dm, qd, kd, cd)


def _ret_sample_kernel(gam_ref, q_ref, k_ref, v_ref, g_ref, s0_ref, o_ref, st_ref, *, sb):
    gamma = gam_ref[pl.program_id(1)]
    q = q_ref[...].astype(F32)
    k = k_ref[...].astype(F32)
    v = v_ref[...].astype(F32)
    rows = sb * RET_DK
    s2 = s0_ref[...].reshape(rows, RET_DV)
    col_b = lax.broadcasted_iota(I32, (sb, rows), 1) // RET_DK
    row_b = lax.broadcasted_iota(I32, (sb, rows), 0)
    qexp = jnp.where(col_b == row_b, jnp.concatenate([q * gamma] * sb, axis=1), 0.0)
    o = jnp.sum(q * k, axis=-1, keepdims=True) * v + _bdot(qexp, s2)
    o = o * lax.rsqrt(jnp.mean(o * o, axis=-1, keepdims=True) + NORM_EPS)
    o_ref[...] = (o * g_ref[...].astype(F32)).astype(BF16)
    rep = (lax.broadcasted_iota(I32, (rows, sb), 0) // RET_DK == lax.broadcasted_iota(I32, (rows, sb), 1))
    rep = rep.astype(BF16)
    krep = _bdot(rep, k)
    vrep = _bdot(rep, v)
    eye = (lax.broadcasted_iota(I32, (rows, RET_DK), 0) % RET_DK == lax.broadcasted_iota(I32, (rows, RET_DK), 1))
    kcol = jnp.sum(jnp.where(eye, krep, 0.0), axis=-1, keepdims=True)
    st_ref[...] = (gamma * s2 + kcol * vrep).reshape(sb, RET_DK, RET_DV)


def _retention_sample(rq, rk, rv, rg, s0, gamma, *, row0, ns):
    sb = min(64, ns)
    base = row0 // sb
    blk = lambda i, h: (base + i, h)
    sblk = lambda i, h: (i, h, 0, 0)
    return pl.pallas_call(
        functools.partial(_ret_sample_kernel, sb=sb),
        out_shape=[jax.ShapeDtypeStruct((ns, _RET_W), BF16),
                   jax.ShapeDtypeStruct(s0.shape, F32)],
        grid=(ns // sb, RET_HEADS),
        in_specs=[pl.BlockSpec(memory_space=pltpu.SMEM)]
                 + [pl.BlockSpec((sb, RET_DK), blk)] * 4
                 + [pl.BlockSpec((sb, None, RET_DK, RET_DV), sblk)],
        out_specs=[pl.BlockSpec((sb, RET_DV), lambda i, h: (i, h)),
                   pl.BlockSpec((sb, None, RET_DK, RET_DV), sblk)],
        compiler_params=_cparams("arbitrary", "arbitrary"),
    )(gamma, rq, rk, rv, rg, s0)


def _sink_softmax(s, mask, sink):
    if mask is not None:
        s = jnp.where(mask, s, -jnp.inf)
    m = jnp.maximum(jnp.max(s, axis=-1, keepdims=True), sink)
    p = jnp.exp(s - m)
    return p / (jnp.sum(p, axis=-1, keepdims=True) + jnp.exp(sink - m))


def _split_kv_heads(x):
    lo = lax.broadcasted_iota(I32, x.shape, 1) < SWA_HD
    h0_lo = jnp.where(lo, x, 0.0)
    h1_hi = jnp.where(lo, 0.0, x)
    return ((h0_lo, pltpu.roll(h0_lo, SWA_HD, 1)), (pltpu.roll(h1_hi, SWA_HD, 1), h1_hi))


SWA_STEP_BLOCKS = 4


def _swa_kernel(sink_ref, q_ref, kc_ref, kp_ref, vc_ref, vp_ref, o_ref):
    n = pl.program_id(1)
    c = WINDOW
    kk = jnp.concatenate([kp_ref[...], kc_ref[...]], axis=0)
    vv = jnp.concatenate([vp_ref[...], vc_ref[...]], axis=0)
    ks = [[a.astype(BF16) for a in pair] for pair in _split_kv_heads(kk)]
    vs = [[a.astype(BF16) for a in pair] for pair in _split_kv_heads(vv)]
    qi = lax.broadcasted_iota(I32, (2 * c, c), 0) % c
    ki = lax.broadcasted_iota(I32, (2 * c, c), 1)
    from_prev = ki > qi
    top = lax.broadcasted_iota(I32, (2 * c, 1), 0) < c
    units = [(s, kvh) for s in range(SWA_STEP_BLOCKS) for kvh in range(SWA_KV_HEADS)]
    scores = []
    for s, kvh in units:
        rows, keys = slice(s * c, (s + 1) * c), slice(s * c, (s + 2) * c)
        q2 = jnp.concatenate([q_ref[rows, 2 * kvh * LANES:(2 * kvh + 1) * LANES],
                              q_ref[rows, (2 * kvh + 1) * LANES:(2 * kvh + 2) * LANES]], axis=0)
        kcat = jnp.concatenate([ks[kvh][0][keys], ks[kvh][1][keys]], axis=0)
        scores.append(lax.dot_general(q2, kcat, (((1,), (1,)), ((), ())), preferred_element_type=F32))
    probs = []
    for (s, kvh), sc in zip(units, scores):
        ps = []
        for half in range(2):
            sink = jnp.where(top, sink_ref[4 * kvh + half], sink_ref[4 * kvh + 2 + half])
            s_prev = sc[:, half * 2 * c:half * 2 * c + c]
            s_own = sc[:, half * 2 * c + c:(half + 1) * 2 * c]
            if s == 0:
                s_prev = jnp.where(n > 0, s_prev, -jnp.inf)
            p = _sink_softmax(jnp.where(from_prev, s_prev, s_own), None, sink)
            ps += [jnp.where(from_prev, p, 0.0).astype(BF16), jnp.where(from_prev, 0.0, p).astype(BF16)]
        probs.append(jnp.concatenate(ps, axis=1))
    for (s, kvh), p in zip(units, probs):
        rows, keys = slice(s * c, (s + 1) * c), slice(s * c, (s + 2) * c)
        vcat = jnp.concatenate([vs[kvh][0][keys], vs[kvh][1][keys]], axis=0)
        o = jnp.dot(p, vcat, preferred_element_type=F32)
        o_ref[rows, 2 * kvh * LANES:(2 * kvh + 1) * LANES] = o[:c].astype(BF16)
        o_ref[rows, (2 * kvh + 1) * LANES:(2 * kvh + 2) * LANES] = o[c:].astype(BF16)


def _swa_prompt(sq, sk, sv, sink, *, nb, t):
    rows = SWA_STEP_BLOCKS * WINDOW
    nsteps = t // rows
    nblk = t // WINDOW
    cur = lambda b, n: (b * nsteps + n, 0)
    prev = lambda b, n: (b * nblk + jnp.maximum(n * SWA_STEP_BLOCKS - 1, 0), 0)
    return pl.pallas_call(
        _swa_kernel,
        out_shape=jax.ShapeDtypeStruct((nb * t, _SWA_QW), BF16),
        grid=(nb, nsteps),
        in_specs=[pl.BlockSpec(memory_space=pltpu.SMEM),
                  pl.BlockSpec((rows, _SWA_QW), cur),
                  pl.BlockSpec((rows, _SWA_KW), cur),
                  pl.BlockSpec((WINDOW, _SWA_KW), prev),
                  pl.BlockSpec((rows, _SWA_KW), cur),
                  pl.BlockSpec((WINDOW, _SWA_KW), prev)],
        out_specs=pl.BlockSpec((rows, _SWA_QW), cur),
        compiler_params=_cparams("arbitrary", "arbitrary"),
    )(sink, sq, sk, sk, sv, sv)


def _swa_sample_kernel(sink_ref, q_ref, kn_ref, vn_ref, kc_ref, vc_ref, o_ref, ko_ref, vo_ref, *, sb, w):
    pad = jnp.zeros((LANES - sb, _SWA_KW), F32)
    knt = jnp.concatenate([kn_ref[...], pad], axis=0).T
    vnt = jnp.concatenate([vn_ref[...], pad], axis=0).T
    kall = jnp.concatenate([kc_ref[b] for b in range(sb)] + [knt], axis=1)
    vall = jnp.concatenate([vc_ref[b] for b in range(sb)] + [vnt], axis=1)
    ncol = sb * w + LANES
    lo = lax.broadcasted_iota(I32, (sb, LANES), 1) < SWA_HD
    group = SWA_HEADS // SWA_KV_HEADS
    pieces = []
    for h in range(SWA_HEADS):
        slab = q_ref[:, (h // 2) * LANES:(h // 2 + 1) * LANES].astype(F32)
        mine = jnp.where(lo, slab, 0.0) if h % 2 == 0 else jnp.where(lo, 0.0, slab)
        pieces.append(mine if (h % 2) == (h // group) else pltpu.roll(mine, SWA_HD, 1))
    qrows = jnp.concatenate(pieces, axis=0)
    nrow = SWA_HEADS * sb
    s = _bdot(qrows, kall)
    rb = lax.broadcasted_iota(I32, (nrow, ncol), 0) % sb
    ci = lax.broadcasted_iota(I32, (nrow, ncol), 1)
    in_cache = (ci < sb * w) & (ci // w == rb) & ((w - ci % w) < WINDOW)
    mask = in_cache | (ci == sb * w + rb)
    sink_col = jnp.concatenate([jnp.full((sb, 1), sink_ref[h], F32) for h in range(SWA_HEADS)], axis=0)
    p = _sink_softmax(s, mask, sink_col)
    o = _bdot_nt(p, vall)
    for j in range(SWA_HEADS // 2):
        acc = jnp.zeros((sb, LANES), F32)
        for half in range(2):
            h = 2 * j + half
            oh = o[h * sb:(h + 1) * sb]
            own = jnp.where(lo, oh, 0.0) if h // group == 0 else jnp.where(lo, 0.0, oh)
            acc = acc + (own if (h // group) == half else pltpu.roll(own, SWA_HD, 1))
        o_ref[:, j * LANES:(j + 1) * LANES] = acc.astype(BF16)
    newest = lax.broadcasted_iota(I32, (_SWA_KW, w), 1) == w - 1
    for b in range(sb):
        ko_ref[b] = jnp.where(newest, knt[:, b:b + 1], pltpu.roll(kc_ref[b], w - 1, 1))
        vo_ref[b] = jnp.where(newest, vnt[:, b:b + 1], pltpu.roll(vc_ref[b], w - 1, 1))


def _swa_sample(sq, sk, sv, cache_kt, cache_vt, sink, *, row0, ns):
    sb = min(16, ns)
    w = cache_kt.shape[2]
    base = row0 // sb
    blk = lambda i: (base + i, 0)
    cblk = lambda i: (i, 0, 0)
    cspec = pl.BlockSpec((sb, _SWA_KW, w), cblk)
    return pl.pallas_call(
        functools.partial(_swa_sample_kernel, sb=sb, w=w),
        out_shape=[jax.ShapeDtypeStruct((ns, _SWA_QW), BF16),
                   jax.ShapeDtypeStruct(cache_kt.shape, F32), jax.ShapeDtypeStruct(cache_vt.shape, F32)],
        grid=(ns // sb,),
        in_specs=[pl.BlockSpec(memory_space=pltpu.SMEM),
                  pl.BlockSpec((sb, _SWA_QW), blk),
                  pl.BlockSpec((sb, _SWA_KW), blk),
                  pl.BlockSpec((sb, _SWA_KW), blk),
                  cspec, cspec],
        out_specs=[pl.BlockSpec((sb, _SWA_QW), lambda i: (i, 0)), cspec, cspec],
        compiler_params=_cparams("arbitrary"),
    )(sink, sq, sk, sv, cache_kt, cache_vt)


def _route(logits):
    lane = lax.broadcasted_iota(I32, logits.shape, 1).astype(F32)
    big = float(1 << 20)
    neg = -jnp.inf

    def top(mask):
        v = jnp.max(jnp.where(mask, logits, neg), axis=-1, keepdims=True)
        i = jnp.min(jnp.where(mask & (logits == v), lane, big), axis=-1, keepdims=True)
        return v, i

    gmask = lane < N_GROUPS
    gmax, gsel = top(gmask)
    p_group = 1.0 / jnp.sum(jnp.where(gmask, jnp.exp(logits - gmax), 0.0), axis=-1, keepdims=True)
    first = N_GROUPS + gsel * EXPERTS_PER_GROUP
    emask = (lane >= first) & (lane < first + EXPERTS_PER_GROUP)
    v1, i1 = top(emask)
    v2, i2 = top(emask & (lane != i1))
    t = jnp.exp(v2 - v1)
    w1 = p_group / (1.0 + t)
    return (i1 - N_GROUPS).astype(I32), (i2 - N_GROUPS).astype(I32), w1, w1 * t


W_PIECES = 3


def _weight_pieces(w1, w2):
    lane = lax.broadcasted_iota(I32, (w1.shape[0], LANES), 1)
    out = jnp.zeros((w1.shape[0], LANES), F32)
    for base, w in ((0, w1), (W_PIECES, w2)):
        rest = w
        for p in range(W_PIECES):
            piece = rest.astype(BF16).astype(F32)
            out = jnp.where(lane == base + p, piece, out)
            rest = rest - piece
    return out


def _plan_tile(e1, e2, valid, carry):
    lane = lax.broadcasted_iota(I32, (TM, LANES), 1)
    oh1 = ((lane == e1) & valid).astype(F32)
    oh2 = ((lane == e2) & valid).astype(F32)
    oh = oh1 + oh2
    tri = (lax.broadcasted_iota(I32, (TM, TM), 0) > lax.broadcasted_iota(I32, (TM, TM), 1)).astype(BF16)
    before = _bdot(tri, oh)
    cnt = jnp.sum(oh, axis=0, keepdims=True)
    units = jnp.maximum(jnp.floor((cnt + (RUN - 1)) * (1.0 / RUN)), 1.0)
    upper = (lax.broadcasted_iota(I32, (LANES, LANES), 0) < lax.broadcasted_iota(I32, (LANES, LANES), 1))
    lstart = RUN * _bdot(jnp.broadcast_to(units, (SUBLANES, LANES)), upper.astype(BF16))[0:1]
    slot = lstart + before
    lp1 = jnp.sum(oh1 * slot, axis=-1, keepdims=True)
    lp2 = jnp.sum(oh2 * slot, axis=-1, keepdims=True)
    vcol = valid[:, 0:1]
    lp = jnp.where(lane == 0, jnp.where(vcol, lp1, -1.0), jnp.where(lane == 1, jnp.where(vcol, lp2, -1.0), 0.0))
    base = carry[...]
    carry[...] = base + RUN * units
    return lp, (RUN * units).astype(I32), lstart.astype(I32), base.astype(I32)


def _outproj_step(gated_of, oswa_of, x_of, g1_of, sh_of, sc_of, n_valid, sa_ref, sb_ref, n2_ref, wur_ref, wus_ref,
                  wo_ref, wrc_ref, br_ref, x1_o, h2_o, rt_o, lp_o, cnt_o, ls_o, gb_o, carry):
    subs = [slice(r, r + TM) for r in range(0, TT, TM)]
    merged = []
    for rows in subs:
        y_ret = jnp.dot(gated_of(rows), wur_ref[...], preferred_element_type=F32)
        y_swa = jnp.dot(oswa_of(rows), wus_ref[...], preferred_element_type=F32)
        merged.append((sa_ref[rows].astype(F32) * y_ret + sb_ref[rows].astype(F32) * y_swa).astype(BF16))
    hs = []
    for rows, m in zip(subs, merged):
        x1 = x_of(rows) + g1_of(rows) * jnp.dot(m, wo_ref[...], preferred_element_type=F32)
        x1_o[rows] = x1
        h2 = _rms(x1, n2_ref[...]) * (1.0 + sc_of(rows)) + sh_of(rows)
        hi = h2.astype(BF16)
        h2_o[rows] = hi
        hs.append((hi, (h2 - hi.astype(F32)).astype(BF16)))
    routed = []
    for rows, (hi, lo) in zip(subs, hs):
        both = jnp.dot(hi, wrc_ref[...], preferred_element_type=F32)
        logits = (both[:, :LANES] + both[:, LANES:]
                  + jnp.dot(lo, wrc_ref[:, :LANES], preferred_element_type=F32) + br_ref[...])
        e1, e2, w1, w2 = _route(logits)
        rt_o[rows] = _weight_pieces(w1, w2)
        routed.append((e1, e2))
    for sub, (e1, e2) in enumerate(routed):
        row = lax.broadcasted_iota(I32, (TM, LANES), 0) + sub * TM
        valid = (row >= 0) if n_valid is None else (row < n_valid)
        lp_o[sub * TM:(sub + 1) * TM], cnt_o[sub], ls_o[sub], gb_o[sub] = _plan_tile(e1, e2, valid, carry)


def _outproj_kernel(gtp_ref, gts_ref, osp_ref, oss_ref, sa_ref, sb_ref, xp_ref, xs_ref, g1p_ref, shp_ref, scp_ref,
                    g1s_ref, shs_ref, scs_ref, *rest, nps, ns):
    i = pl.program_id(0)
    is_s = i >= nps
    carry = rest[-1]

    @pl.when(i == 0)
    def _():
        carry[...] = jnp.zeros_like(carry)

    @pl.when(jnp.logical_not(is_s))
    def _():
        _outproj_step(lambda rows: gtp_ref[rows], lambda rows: osp_ref[rows], lambda rows: xp_ref[rows],
                      lambda rows: g1p_ref[...], lambda rows: shp_ref[...], lambda rows: scp_ref[...], None,
                      sa_ref, sb_ref, *rest)

    @pl.when(is_s)
    def _():
        _outproj_step(lambda rows: gts_ref[rows], lambda rows: oss_ref[rows], lambda rows: xs_ref[rows],
                      lambda rows: g1s_ref[rows], lambda rows: shs_ref[rows], lambda rows: scs_ref[rows], ns,
                      sa_ref, sb_ref, *rest)


def _outproj(gated_p, gated_s, oswa_p, oswa_s, siga, sigb, xp, xs_pad, modp, mods, n2, wur, wus, wo, wr, br,
             *, nps, spb, nb, ns):
    d = xp.shape[1]
    sub = TT // TM
    nrow = (nps + 1) * TT
    row = lambda i: (i, 0)
    pstep = lambda i: (jnp.minimum(i, nps - 1), 0)
    pbatch = lambda col: (lambda i: (jnp.minimum(i // spb, nb - 1), 0, col))
    scol = lambda col: (lambda i: (0, col))
    const = lambda i: (0, 0)
    wr_hi = wr.astype(BF16)
    wrc = jnp.concatenate([wr_hi, (wr - wr_hi.astype(F32)).astype(BF16)], axis=1)
    meta = jax.ShapeDtypeStruct(((nps + 1) * sub, 1, LANES), I32)
    mspec = pl.BlockSpec((sub, 1, LANES), lambda i: (i, 0, 0))
    return pl.pallas_call(
        functools.partial(_outproj_kernel, nps=nps, ns=ns),
        out_shape=[jax.ShapeDtypeStruct((nrow, d), F32),
                   jax.ShapeDtypeStruct((nrow, d), BF16),
                   jax.ShapeDtypeStruct((nrow, LANES), F32),
                   jax.ShapeDtypeStruct((nrow, LANES), F32), meta, meta, meta],
        grid=(nps + 1,),
        in_specs=[pl.BlockSpec((TT, _RET_W), pstep), pl.BlockSpec((TT, _RET_W), const),
                  pl.BlockSpec((TT, _SWA_QW), pstep), pl.BlockSpec((TT, _SWA_QW), const),
                  pl.BlockSpec((TT, d), row), pl.BlockSpec((TT, d), row),
                  pl.BlockSpec((TT, d), pstep), pl.BlockSpec((TT, d), const),
                  pl.BlockSpec((None, 1, d), pbatch(2)), pl.BlockSpec((None, 1, d), pbatch(3)),
                  pl.BlockSpec((None, 1, d), pbatch(4)),
                  pl.BlockSpec((TT, d), scol(2)), pl.BlockSpec((TT, d), scol(3)), pl.BlockSpec((TT, d), scol(4)),
                  pl.BlockSpec((1, d), const),
                  pl.BlockSpec(wur.shape, const), pl.BlockSpec(wus.shape, const), pl.BlockSpec(wo.shape, const),
                  pl.BlockSpec(wrc.shape, const), pl.BlockSpec((1, LANES), const)],
        out_specs=[pl.BlockSpec((TT, d), row), pl.BlockSpec((TT, d), row), pl.BlockSpec((TT, LANES), row),
                   pl.BlockSpec((TT, LANES), row), mspec, mspec, mspec],
        scratch_shapes=[pltpu.VMEM((1, LANES), F32)],
        compiler_params=_cparams("arbitrary"),
    )(gated_p, gated_s, oswa_p, oswa_s, siga, sigb, xp, xs_pad, modp, modp, modp, mods, mods, mods, n2,
      wur, wus, wo, wrc, br)


def _aligned(v):
    return v if isinstance(v, int) else pl.multiple_of(v, RUN)


def _run_copy(src, dst, s_start, d_start, n, sem):
    s_start, d_start, n = _aligned(s_start), _aligned(d_start), _aligned(n)
    return pltpu.make_async_copy(src.at[pl.ds(s_start, n)], dst.at[pl.ds(d_start, n)], sem)


def _tile_rows(cnt_ref, ls_ref, step):
    last = step * N_EXPERTS + N_EXPERTS - 1
    return ls_ref[last] + cnt_ref[last]


def _each_run(step, fn):
    for e in range(N_EXPERTS):
        fn(step * N_EXPERTS + e)


def _dispatch_kernel(cnt_ref, ls_ref, gd_ref, ps_ref, pn_ref, nu_ref, h_ref, lp_ref, rt_ref, xs_ref,
                     sorted_scr, zero_scr, sem, zsem, *, nt, maxt):
    i = pl.program_id(0)

    def pad(e):
        return _run_copy(zero_scr, xs_ref, 0, ps_ref[e], pn_ref[e], zsem)

    def tail(j):
        return _run_copy(zero_scr, xs_ref, 0, j * TE, TE, zsem)

    def each_pad(fn):
        def body(e, c):
            @pl.when(pn_ref[e] > 0)
            def _():
                fn(pad(e))
            return c
        lax.fori_loop(0, N_EXPERTS, body, 0)

    def each_tail(fn):
        def body(j, c):
            fn(tail(j))
            return c
        lax.fori_loop(nu_ref[0], maxt, body, 0)

    @pl.when(i == 0)
    def _():
        zero_scr[...] = jnp.zeros_like(zero_scr)
        each_pad(lambda cp: cp.start())
        each_tail(lambda cp: cp.start())

    lpt = lp_ref[...].T
    slot = lax.broadcasted_iota(I32, (SLOTS, TM), 0).astype(F32)
    first, second = slot == lpt[0:1], slot == lpt[1:2]
    d = h_ref.shape[1]
    sorted_scr[i % 2, :, :d] = jnp.dot((first | second).astype(BF16), h_ref[...],
                                      preferred_element_type=F32).astype(BF16)
    pieces = rt_ref[...].astype(BF16)
    lane = lax.broadcasted_iota(I32, (SLOTS, LANES), 1)
    sorted_scr[i % 2, :, d:] = jnp.where(lane < W_PIECES,
                                        jnp.dot(first.astype(BF16), pieces, preferred_element_type=F32),
                                        jnp.dot(second.astype(BF16), pieces, preferred_element_type=F32)).astype(BF16)

    def copy(step):
        return lambda k: _run_copy(sorted_scr.at[step % 2], xs_ref, ls_ref[k], gd_ref[k], cnt_ref[k], sem.at[step % 2])

    def wait_all(step):
        _run_copy(sorted_scr.at[step % 2], xs_ref, 0, 0, _tile_rows(cnt_ref, ls_ref, step), sem.at[step % 2]).wait()

    _each_run(i, lambda k: copy(i)(k).start())

    @pl.when(i > 0)
    def _():
        wait_all(i - 1)

    @pl.when(i == nt - 1)
    def _():
        wait_all(i)
        each_pad(lambda cp: cp.wait())
        each_tail(lambda cp: cp.wait())


def _dispatch(cnt, ls, gd, ps, pn, nu, h2, lp, rt, *, nt, maxt):
    d = h2.shape[1]
    wide = d + LANES
    return pl.pallas_call(
        functools.partial(_dispatch_kernel, nt=nt, maxt=maxt),
        out_shape=jax.ShapeDtypeStruct((maxt * TE, wide), BF16),
        grid_spec=pltpu.PrefetchScalarGridSpec(
            num_scalar_prefetch=6,
            grid=(nt,),
            in_specs=[pl.BlockSpec((TM, d), lambda i, *_: (i, 0)),
                      pl.BlockSpec((TM, LANES), lambda i, *_: (i, 0)),
                      pl.BlockSpec((TM, LANES), lambda i, *_: (i, 0))],
            out_specs=pl.BlockSpec(memory_space=pl.ANY),
            scratch_shapes=[pltpu.VMEM((2, SLOTS, wide), BF16), pltpu.VMEM((TE, wide), BF16),
                            pltpu.SemaphoreType.DMA((2,)), pltpu.SemaphoreType.DMA(())]),
        compiler_params=_cparams("arbitrary"),
    )(cnt, ls, gd, ps, pn, nu, h2, lp, rt)


def _experts_kernel(te_ref, nu_ref, start_ref, nxt_ref, par_ref, x_ref, w1_hbm, w3_hbm, w2_hbm, y_ref,
                    w1f, w3f, w2f, w1b, w3b, w2b, sem):
    j = pl.program_id(0)
    e = te_ref[j]

    def fetch(ex, slot):
        return [pltpu.make_async_copy(src.at[ex], dst.at[slot], sem.at[slot, n])
                for n, (src, dst) in enumerate(((w1_hbm, w1f), (w3_hbm, w3f), (w2_hbm, w2f)))]

    @pl.when(j == 0)
    def _():
        for cp in fetch(e, par_ref[e]):
            cp.start()

    @pl.when((j == start_ref[e]) & (j < nu_ref[0]))
    def _():
        slot = par_ref[e]

        @pl.when(nxt_ref[e] >= 0)
        def _():
            for cp in fetch(nxt_ref[e], 1 - slot):
                cp.start()

        for cp in fetch(e, slot):
            cp.wait()
        w1b[...] = w1f[slot].astype(BF16)
        w3b[...] = w3f[slot].astype(BF16)
        w2b[...] = w2f[slot].astype(BF16)

    @pl.when(j < nu_ref[0])
    def _():
        d = w1b.shape[0]
        subs = [slice(r, r + TE_SUB) for r in range(0, TE, TE_SUB)]
        ab = [(jnp.dot(x_ref[rows, :d], w1b[...], preferred_element_type=F32),
               jnp.dot(x_ref[rows, :d], w3b[...], preferred_element_type=F32)) for rows in subs]
        wrow = [jnp.sum(x_ref[rows, d:].astype(F32), axis=-1, keepdims=True) for rows in subs]
        hid = [(_silu(a) * b * w).astype(BF16) for (a, b), w in zip(ab, wrow)]
        for rows, h in zip(subs, hid):
            y_ref[rows, :d] = jnp.dot(h, w2b[...], preferred_element_type=F32).astype(BF16)
            y_ref[rows, d:] = x_ref[rows, d:]


def _experts(te, nu, start, nxt, par, xs, w1, w3, w2, *, maxt):
    wide = xs.shape[1]
    d, f = w1.shape[1], w1.shape[2]
    used = lambda j, te, nu, *_: (jnp.minimum(j, nu[0] - 1), 0)
    hbm = pl.BlockSpec(memory_space=pl.ANY)
    return pl.pallas_call(
        _experts_kernel,
        out_shape=jax.ShapeDtypeStruct(xs.shape, BF16),
        grid_spec=pltpu.PrefetchScalarGridSpec(
            num_scalar_prefetch=5,
            grid=(maxt,),
            in_specs=[pl.BlockSpec((TE, wide), used), hbm, hbm, hbm],
            out_specs=pl.BlockSpec((TE, wide), used),
            scratch_shapes=[pltpu.VMEM((2, d, f), F32), pltpu.VMEM((2, d, f), F32), pltpu.VMEM((2, f, d), F32),
                            pltpu.VMEM((d, f), BF16), pltpu.VMEM((d, f), BF16), pltpu.VMEM((f, d), BF16),
                            pltpu.SemaphoreType.DMA((2, 3))]),
        input_output_aliases={5: 0},
        compiler_params=_cparams("arbitrary"),
    )(te, nu, start, nxt, par, xs, w1, w3, w2)


def _combine_kernel(cnt_ref, ls_ref, gd_ref, ys_ref, lp_ref, x1_ref, g2p_ref, g2s_ref, fg_ref,
                    yp_o, ys_o, ybuf, sem, *, npt):
    i = pl.program_id(0)
    nt = npt + 1

    def fetch(step):
        ybuf[step % 2] = jnp.zeros(ybuf.shape[1:], BF16)
        _each_run(step, lambda k: _run_copy(ys_ref, ybuf.at[step % 2], gd_ref[k], ls_ref[k], cnt_ref[k],
                                            sem.at[step % 2]).start())

    @pl.when(i == 0)
    def _():
        fetch(0)

    @pl.when(i + 1 < nt)
    def _():
        fetch(i + 1)

    _run_copy(ys_ref, ybuf.at[i % 2], 0, 0, _tile_rows(cnt_ref, ls_ref, i), sem.at[i % 2]).wait()

    d = x1_ref.shape[1]
    slot = lax.broadcasted_iota(I32, (TM, SLOTS), 1).astype(F32)
    lp = lp_ref[...]
    both = ((slot == lp[:, 0:1]) | (slot == lp[:, 1:2])).astype(BF16)
    moe = jnp.dot(both, ybuf[i % 2, :, :d], preferred_element_type=F32)
    g2 = jnp.where(i >= npt, g2s_ref[...], g2p_ref[...])
    y = _rms(x1_ref[...] + g2 * moe, fg_ref[...])

    @pl.when(i < npt)
    def _():
        yp_o[...] = y

    @pl.when(i >= npt)
    def _():
        ys_o[...] = y


def _combine(cnt, ls, gd, ys, lp, x1, modp, mods, fg, *, npt, tpb, nb):
    d = x1.shape[1]
    nt = npt + 1
    row = lambda i, *_: (i, 0)
    return pl.pallas_call(
        functools.partial(_combine_kernel, npt=npt),
        out_shape=[jax.ShapeDtypeStruct((npt * TM, d), F32), jax.ShapeDtypeStruct((TM, d), F32)],
        grid_spec=pltpu.PrefetchScalarGridSpec(
            num_scalar_prefetch=3,
            grid=(nt,),
            in_specs=[pl.BlockSpec(memory_space=pl.ANY),
                      pl.BlockSpec((TM, LANES), row), pl.BlockSpec((TM, d), row),
                      pl.BlockSpec((None, 1, d), lambda i, *_: (jnp.minimum(i // tpb, nb - 1), 0, 5)),
                      pl.BlockSpec((TM, d), lambda i, *_: (0, 5)),
                      pl.BlockSpec((1, d), lambda i, *_: (0, 0))],
            out_specs=[pl.BlockSpec((TM, d), lambda i, *_: (jnp.minimum(i, npt - 1), 0)),
                       pl.BlockSpec((TM, d), lambda i, *_: (0, 0))],
            scratch_shapes=[pltpu.VMEM((2, SLOTS, ys.shape[1]), BF16), pltpu.SemaphoreType.DMA((2,))]),
        compiler_params=_cparams("arbitrary"),
    )(cnt, ls, gd, ys, lp, x1, modp, mods, fg)


TAB_LO = 64


def _rotation_tables(t):
    inv_r = jnp.repeat(1.0 / (ROPE_THETA ** jnp.linspace(0.0, 1.0, RET_DK // 2, dtype=F32)), 2)
    sign_r = jnp.where(jnp.arange(RET_DK) % 2 == 0, -1.0, 1.0).astype(F32)
    inv_w = jnp.tile(ROPE_THETA ** (-jnp.arange(0, SWA_HD, 2, dtype=F32) / SWA_HD), LANES // (SWA_HD // 2))
    sign_w = jnp.where(jnp.arange(LANES) % SWA_HD < SWA_HD // 2, -1.0, 1.0).astype(F32)
    hi = (jnp.arange(t // TAB_LO, dtype=I32) * TAB_LO).astype(F32)[:, None]
    lo = jnp.arange(TAB_LO, dtype=I32).astype(F32)[:, None]
    past = jnp.full((1, 1), PAST_LEN, F32)

    def pair(inv, sign):
        a, b = hi * inv[None, :], lo * inv[None, :]
        ca, sa, cb, sb = jnp.cos(a)[:, None], jnp.sin(a)[:, None], jnp.cos(b)[None], jnp.sin(b)[None]
        cos = (ca * cb - sa * sb).reshape(t, LANES)
        sin = (sa * cb + ca * sb).reshape(t, LANES) * sign[None, :]
        ang = past * inv[None, :]
        return (cos, sin), (jnp.cos(ang), jnp.sin(ang) * sign[None, :])

    (pr, sr_), (pw, sw_) = pair(inv_r, sign_r), pair(inv_w, sign_w)
    return pr + pw, sr_ + sw_


def kernel(x_prompt, x_sample, c_prompt, c_sample, state_ret, cache_swa_k, cache_swa_v, w_ada, b_ada, norm1_g, norm2_g, w_in, w_up_ret, w_up_swa, w_o, sink, w_rg, b_rg, w_re, b_re, w1, w3, w2, final_g):
    nb, t, d = x_prompt.shape
    ns, dec_seq, _ = x_sample.shape
    depth = w_ada.shape[0]
    assert depth == 1 and dec_seq == 1, "single layer, one new token per sequence"
    assert t % TT == 0 and ns <= TM and ns % 16 == 0 and d % LANES == 0
    assert t % RET_BLOCK == 0 and t % (SWA_STEP_BLOCKS * WINDOW) == 0 and t % TAB_LO == 0
    assert N_GROUPS + N_EXPERTS <= LANES
    w = cache_swa_k.shape[2]
    tpb = t // TM
    npt = nb * tpb
    spb = t // TT
    nps = nb * spb
    nt = (nps + 1) * (TT // TM)
    np_rows = nb * t
    n_tok = np_rows + ns
    maxt = -(-(2 * n_tok + nt * N_EXPERTS * RUN + N_EXPERTS * (TE - 1)) // TE)

    xp = x_prompt.reshape(np_rows, d)
    xs_pad = jnp.pad(x_sample.reshape(ns, d), ((0, TT - ns), (0, 0)))

    c_all = jnp.concatenate([jnp.pad(c_sample, ((0, TT - ns), (0, 0))),
                             jnp.pad(c_prompt, ((0, SUBLANES - nb % SUBLANES), (0, 0)))])
    mods = _modulation(c_all, w_ada[0], b_ada[0])
    modp = mods[TT:TT + nb].reshape(nb, 1, 6 * d)

    tabs_p, tabs_s = _rotation_tables(t)
    rq, rk, rv, rg, sq, sk, sv, siga, sigb = _inproj(
        xp, xs_pad, modp, mods, norm1_g, w_in[0].astype(BF16), tabs_p, tabs_s, nps=nps, spb=spb, nb=nb)

    dm, qd, kd, cd, gamma = _ret_tables(RET_BLOCK)
    gated_p, st_p = _retention_prompt(rq, rk, rv, rg, (dm, qd, kd, cd), nb=nb, t=t)
    gated_s, st_s = _retention_sample(rq, rk, rv, rg, state_ret[0], gamma, row0=np_rows, ns=ns)
    oswa_p = _swa_prompt(sq, sk, sv, sink[0], nb=nb, t=t)
    to_t = lambda c: jnp.transpose(c[0], (0, 2, 3, 1)).reshape(ns, _SWA_KW, w)
    from_t = lambda c: jnp.transpose(c.reshape(ns, SWA_KV_HEADS, SWA_HD, w), (0, 3, 1, 2))[None]
    oswa_s, ks_new, vs_new = _swa_sample(sq, sk, sv, to_t(cache_swa_k), to_t(cache_swa_v), sink[0],
                                         row0=np_rows, ns=ns)
    gated_s = jnp.pad(gated_s, ((0, TT - ns), (0, 0)))
    oswa_s = jnp.pad(oswa_s, ((0, TT - ns), (0, 0)))

    wr = jnp.pad(jnp.concatenate([w_rg[0], w_re[0]], axis=1), ((0, 0), (0, LANES - N_GROUPS - N_EXPERTS)))
    br = jnp.pad(jnp.concatenate([b_rg[0], b_re[0]]), (0, LANES - N_GROUPS - N_EXPERTS)).reshape(1, LANES)
    x1, h2, rt, lp, cnt, ls, gb = _outproj(
        gated_p, gated_s, oswa_p, oswa_s, siga, sigb, xp, xs_pad, modp, mods, norm2_g,
        w_up_ret[0].astype(BF16), w_up_swa[0].astype(BF16), w_o[0].astype(BF16), wr, br,
        nps=nps, spb=spb, nb=nb, ns=ns)
    cnt = cnt[:, 0, :N_EXPERTS]
    ls = ls[:, 0, :N_EXPERTS]
    gb = gb[:, 0, :N_EXPERTS]
    seg = jnp.sum(cnt, axis=0)
    tiles = (seg + TE - 1) // TE
    tile_end = jnp.cumsum(tiles)
    row_start = (tile_end - tiles) * TE
    gd = (gb + row_start[None, :]).reshape(-1)
    n_used = tile_end[-1:]
    jj = jnp.minimum(jnp.arange(maxt, dtype=I32), n_used[0] - 1)
    te = jnp.minimum(jnp.sum((tile_end[None, :] <= jj[:, None]).astype(I32), axis=1), N_EXPERTS - 1)
    cnt = cnt.reshape(-1)
    ls = ls.reshape(-1)
    n_used = n_used.astype(I32)
    xs = _dispatch(cnt, ls, gd, row_start + seg, tiles * TE - seg, n_used, h2, lp, rt, nt=nt, maxt=maxt)
    has = tiles > 0
    eidx = jnp.arange(N_EXPERTS, dtype=I32)
    later = jnp.where(has[None, :] & (eidx[None, :] > eidx[:, None]), eidx[None, :], N_EXPERTS)
    nxt_e = jnp.min(later, axis=1)
    nxt_e = jnp.where(nxt_e < N_EXPERTS, nxt_e, -1).astype(I32)
    par_e = ((jnp.cumsum(has.astype(I32)) - 1) % 2).astype(I32)
    ys = _experts(te, n_used, (tile_end - tiles).astype(I32), nxt_e, par_e, xs, w1[0], w3[0], w2[0], maxt=maxt)
    y_p, y_s = _combine(cnt, ls, gd, ys, lp, x1, modp, mods, final_g.reshape(1, d), npt=npt, tpb=tpb, nb=nb)

    y_prompt = y_p.reshape(nb, t, d)
    y_sample = y_s[:ns].reshape(ns, 1, d)
    wk = min(WINDOW, t)
    last = lambda a: jnp.stack([a[(b + 1) * t - wk:(b + 1) * t] for b in range(nb)]).reshape(
        nb, wk, SWA_KV_HEADS, SWA_HD)
    skp, svp = last(sk), last(sv)
    return (y_prompt, y_sample, st_p[None], st_s[None], skp[None], svp[None], from_t(ks_new), from_t(vs_new))
```

```python
import functools

import jax
import jax.numpy as jnp
from jax import lax
from jax.experimental import pallas as pl
from jax.experimental.pallas import tpu as pltpu

F32 = jnp.float32
BF16 = jnp.bfloat16
I32 = jnp.int32

PAST_LEN = 8192
RET_HEADS = 4
RET_DK = 128
RET_DV = 128
RET_CHUNK = 128
SWA_HEADS = 8
SWA_KV_HEADS = 2
SWA_HD = 64
WINDOW = 128
ROPE_THETA = 10000.0
N_GROUPS = 4
EXPERTS_PER_GROUP = 8
N_EXPERTS = N_GROUPS * EXPERTS_PER_GROUP
D_EXPERT = 256
NORM_EPS = 1e-6

LANES = 128
SUBLANES = 8
TM = 256
TT = 2 * TM
RUN = 16
SLOTS = 2 * TM + N_EXPERTS * RUN
TE = 512
TE_SUB = 256
VMEM_LIMIT = 56 * 1024 * 1024

_RET_W = RET_HEADS * RET_DK
_SWA_QW = SWA_HEADS * SWA_HD
_SWA_KW = SWA_KV_HEADS * SWA_HD


def _cparams(*sem):
    return pltpu.CompilerParams(dimension_semantics=sem, vmem_limit_bytes=VMEM_LIMIT)


def _sigmoid(x):
    return 1.0 / (1.0 + jnp.exp(-x))


def _silu(x):
    return x * _sigmoid(x)


def _bdot(a, b):
    return jnp.dot(a.astype(BF16), b.astype(BF16), preferred_element_type=F32)


def _bdot_nt(a, b):
    return lax.dot_general(a.astype(BF16), b.astype(BF16), (((1,), (1,)), ((), ())), preferred_element_type=F32)


def _mod_kernel(c_ref, w_ref, b_ref, o_ref):
    o_ref[...] = _bdot(_silu(c_ref[...]), w_ref[...]) + b_ref[...]


def _modulation(c_all, w_ada, b_ada):
    rows, d = c_all.shape
    n = w_ada.shape[1]
    return pl.pallas_call(
        _mod_kernel,
        out_shape=jax.ShapeDtypeStruct((rows, n), F32),
        grid=(n // d,),
        in_specs=[pl.BlockSpec((rows, d), lambda j: (0, 0)),
                  pl.BlockSpec((d, d), lambda j: (0, j)),
                  pl.BlockSpec((1, d), lambda j: (0, j))],
        out_specs=pl.BlockSpec((rows, d), lambda j: (0, j)),
        compiler_params=_cparams("arbitrary"),
    )(c_all, w_ada, b_ada.reshape(1, n))


def _rms(x, g):
    return x * lax.rsqrt(jnp.mean(x * x, axis=-1, keepdims=True) + NORM_EPS) * g


def _pair_rotate(z, cos, sin_signed):
    n = z.shape[-1]
    lane = lax.broadcasted_iota(I32, z.shape, 1)
    partner = jnp.where((lane & 1) == 0, pltpu.roll(z, n - 1, 1), pltpu.roll(z, 1, 1))
    reps = n // LANES
    cos = jnp.concatenate([cos] * reps, axis=1) if reps > 1 else cos
    sin_signed = jnp.concatenate([sin_signed] * reps, axis=1) if reps > 1 else sin_signed
    return z * cos + partner * sin_signed


def _half_rotate(z, cos, sin_signed):
    n = z.shape[-1]
    half = SWA_HD // 2
    lane = lax.broadcasted_iota(I32, z.shape, 1)
    partner = jnp.where((lane & (SWA_HD - 1)) < half, pltpu.roll(z, n - half, 1), pltpu.roll(z, half, 1))
    reps = n // LANES
    cos = jnp.concatenate([cos] * reps, axis=1) if reps > 1 else cos
    sin_signed = jnp.concatenate([sin_signed] * reps, axis=1) if reps > 1 else sin_signed
    return z * cos + partner * sin_signed


def _inproj_step(x_of, sh_of, sc_of, tabs_of, n1_ref, w_ref, outs):
    ret_o, sq_o, skv_o, sig_o = outs
    d = w_ref.shape[0]
    subs = []
    for r in range(0, TT, TM):
        rows = slice(r, r + TM)
        h = (_rms(x_of(rows), n1_ref[...]) * (1.0 + sc_of(rows)) + sh_of(rows)).astype(BF16)
        subs.append((rows, h, tabs_of(rows)))

    def seg(h, a, b):
        return jnp.dot(h, w_ref[:, a:b], preferred_element_type=F32)

    o = 0
    for rows, h, (cr, sr, cw, sw) in subs:
        ret_o[rows, 0:_RET_W] = _pair_rotate(seg(h, o, o + _RET_W), cr, sr).astype(BF16)
    o += _RET_W
    for rows, h, (cr, sr, cw, sw) in subs:
        ret_o[rows, _RET_W:2 * _RET_W] = (_pair_rotate(seg(h, o, o + _RET_W), cr, sr)
                                          * (RET_DK ** -0.5)).astype(BF16)
    o += _RET_W
    for rows, h, _ in subs:
        ret_o[rows, 2 * _RET_W:3 * _RET_W] = seg(h, o, o + _RET_W).astype(BF16)
    o += _RET_W
    for rows, h, _ in subs:
        ret_o[rows, 3 * _RET_W:4 * _RET_W] = _silu(seg(h, o, o + _RET_W)).astype(BF16)
    o += _RET_W
    for rows, h, (cr, sr, cw, sw) in subs:
        sq_o[rows] = (_half_rotate(seg(h, o, o + _SWA_QW), cw, sw) * (SWA_HD ** -0.5)).astype(BF16)
    o += _SWA_QW
    for rows, h, (cr, sr, cw, sw) in subs:
        zkv = seg(h, o, o + 2 * _SWA_KW)
        skv_o[rows, :_SWA_KW] = _half_rotate(zkv[:, :_SWA_KW], cw, sw)
        skv_o[rows, _SWA_KW:] = zkv[:, _SWA_KW:]
    o += 2 * _SWA_KW
    for rows, h, _ in subs:
        sig_o[rows, :d] = _sigmoid(seg(h, o, o + d)).astype(BF16)
    o += d
    for rows, h, _ in subs:
        sig_o[rows, d:] = _sigmoid(seg(h, o, o + d)).astype(BF16)


def _inproj_kernel(xp_ref, xs_ref, shp_ref, scp_ref, shs_ref, scs_ref, n1_ref, w_ref, tp_ref, ts_ref, *outs, nps):
    is_s = pl.program_id(0) >= nps
    quarters = [slice(q * LANES, (q + 1) * LANES) for q in range(4)]

    @pl.when(jnp.logical_not(is_s))
    def _():
        _inproj_step(lambda rows: xp_ref[rows], lambda rows: shp_ref[...], lambda rows: scp_ref[...],
                     lambda rows: tuple(tp_ref[rows, q] for q in quarters),
                     n1_ref, w_ref, outs)

    @pl.when(is_s)
    def _():
        _inproj_step(lambda rows: xs_ref[rows], lambda rows: shs_ref[rows], lambda rows: scs_ref[rows],
                     lambda rows: tuple(ts_ref[:, q] for q in quarters),
                     n1_ref, w_ref, outs)


def _inproj(xp, xs_pad, modp, mods, n1, w_in_b, tabs_p, tabs_s, *, nps, spb, nb):
    d = xp.shape[1]
    nrow = (nps + 1) * TT
    n_in = w_in_b.shape[1]
    pstep = lambda i: (jnp.minimum(i, nps - 1), 0)
    pbatch = lambda col: (lambda i: (jnp.minimum(i // spb, nb - 1), 0, col))
    tab_idx = lambda i: (jnp.where(i < nps, i % spb, 0), 0)
    out_cols = [(4 * _RET_W, BF16), (_SWA_QW, BF16), (2 * _SWA_KW, F32), (2 * d, BF16)]
    return pl.pallas_call(
        functools.partial(_inproj_kernel, nps=nps),
        out_shape=[jax.ShapeDtypeStruct((nrow, c), t) for c, t in out_cols],
        grid=(nps + 1,),
        in_specs=[pl.BlockSpec((TT, d), pstep),
                  pl.BlockSpec((TT, d), lambda i: (0, 0)),
                  pl.BlockSpec((None, 1, d), pbatch(0)),
                  pl.BlockSpec((None, 1, d), pbatch(1)),
                  pl.BlockSpec((TT, d), lambda i: (0, 0)),
                  pl.BlockSpec((TT, d), lambda i: (0, 1)),
                  pl.BlockSpec((1, d), lambda i: (0, 0)),
                  pl.BlockSpec((d, n_in), lambda i: (0, 0))]
                 + [pl.BlockSpec((TT, 4 * LANES), tab_idx), pl.BlockSpec((1, 4 * LANES), lambda i: (0, 0))],
        out_specs=[pl.BlockSpec((TT, c), lambda i: (i, 0)) for c, _ in out_cols],
        compiler_params=_cparams("arbitrary"),
    )(xp, xs_pad, modp, modp, mods, mods, n1, w_in_b, tabs_p, tabs_s)


RET_BLOCK = 512


def _ret_kernel(q_ref, k_ref, v_ref, g_ref, dm_ref, qd_ref, kd_ref, cd_ref, o_ref, st_ref, s_scr, *, nsteps):
    step = pl.program_id(1)

    @pl.when(step == 0)
    def _():
        s_scr[...] = jnp.zeros_like(s_scr)

    for h in range(RET_HEADS):
        sl = slice(h * RET_DK, (h + 1) * RET_DK)
        state = s_scr[h]
        q, k, v = q_ref[:, sl], k_ref[:, sl], v_ref[:, sl]
        att = _bdot_nt(q, k) * dm_ref[h]
        o = _bdot(att, v) + _bdot(q.astype(F32) * qd_ref[h], state)
        kd = (k.astype(F32) * kd_ref[h]).astype(BF16)
        kv = lax.dot_general(kd, v, (((0,), (0,)), ((), ())), preferred_element_type=F32)
        s_scr[h] = cd_ref[h] * state + kv
        o = o * lax.rsqrt(jnp.mean(o * o, axis=-1, keepdims=True) + NORM_EPS)
        o_ref[:, sl] = (o * g_ref[:, sl].astype(F32)).astype(BF16)

    @pl.when(step == nsteps - 1)
    def _():
        st_ref[...] = s_scr[...]


def _ret_tables(chunk):
    ld = jnp.log(1.0 - 2.0 ** (-5.0 - jnp.arange(RET_HEADS, dtype=F32)))
    idx = jnp.arange(chunk, dtype=F32)
    diff = idx[:, None] - idx[None, :]
    causal = diff >= 0
    dmask = jnp.where(causal[None], jnp.exp(ld[:, None, None] * jnp.where(causal, diff, 0.0)[None]), 0.0)
    k_dec = jnp.exp(ld[None, :] * (chunk - 1.0 - idx)[:, None])
    q_dec = jnp.exp(ld[None, :] * (idx + 1.0)[:, None])
    chunk_decay = jnp.exp(ld * chunk)
    bc = lambda t: jnp.broadcast_to(t.T[:, :, None], (RET_HEADS, chunk, RET_DV))
    cd = jnp.broadcast_to(chunk_decay[:, None, None], (RET_HEADS, 1, RET_DV))
    return dmask, bc(q_dec), bc(k_dec), cd, jnp.exp(ld)


def _retention_prompt(ret4, tabs, *, nb, t):
    rows = RET_BLOCK
    nsteps = t // rows
    dm, qd, kd, cd = tabs
    blk = lambda b, c: (b * nsteps + c, 0)
    full3 = lambda b, c: (0, 0, 0)
    return pl.pallas_call(
        functools.partial(_ret_kernel, nsteps=nsteps),
        out_shape=[jax.ShapeDtypeStruct((nb * t, _RET_W), BF16),
                   jax.ShapeDtypeStruct((nb, RET_HEADS, RET_DK, RET_DV), F32)],
        grid=(nb, nsteps),
        in_specs=[pl.BlockSpec((rows, _RET_W), lambda b, c, col=col: (b * nsteps + c, col)) for col in range(4)]
                 + [pl.BlockSpec((RET_HEADS, rows, rows), full3)]
                 + [pl.BlockSpec((RET_HEADS, rows, RET_DV), full3)] * 2
                 + [pl.BlockSpec((RET_HEADS, 1, RET_DV), full3)],
        out_specs=[pl.BlockSpec((rows, _RET_W), blk),
                   pl.BlockSpec((None, RET_HEADS, RET_DK, RET_DV), lambda b, c: (b, 0, 0, 0))],
        scratch_shapes=[pltpu.VMEM((RET_HEADS, RET_DK, RET_DV), F32)],
        compiler_params=_cparams("arbitrary", "arbitrary"),
    )(ret4, ret4, ret4, ret4, dm, qd, kd, cd)


def _ret_sample_kernel(gam_ref, q_ref, k_ref, v_ref, g_ref, s0_ref, o_ref, st_ref, *, sb):
    gamma = gam_ref[pl.program_id(1)]
    q = q_ref[...].astype(F32)
    k = k_ref[...].astype(F32)
    v = v_ref[...].astype(F32)
    rows = sb * RET_DK
    s2 = s0_ref[...].reshape(rows, RET_DV)
    col_b = lax.broadcasted_iota(I32, (sb, rows), 1) // RET_DK
    row_b = lax.broadcasted_iota(I32, (sb, rows), 0)
    qexp = jnp.where(col_b == row_b, jnp.concatenate([q * gamma] * sb, axis=1), 0.0)
    o = jnp.sum(q * k, axis=-1, keepdims=True) * v + _bdot(qexp, s2)
    o = o * lax.rsqrt(jnp.mean(o * o, axis=-1, keepdims=True) + NORM_EPS)
    o_ref[...] = (o * g_ref[...].astype(F32)).astype(BF16)
    rep = (lax.broadcasted_iota(I32, (rows, sb), 0) // RET_DK == lax.broadcasted_iota(I32, (rows, sb), 1))
    rep = rep.astype(BF16)
    krep = _bdot(rep, k)
    vrep = _bdot(rep, v)
    eye = (lax.broadcasted_iota(I32, (rows, RET_DK), 0) % RET_DK == lax.broadcasted_iota(I32, (rows, RET_DK), 1))
    kcol = jnp.sum(jnp.where(eye, krep, 0.0), axis=-1, keepdims=True)
    st_ref[...] = (gamma * s2 + kcol * vrep).reshape(sb, RET_DK, RET_DV)


def _retention_sample(ret4, s0, gamma, *, row0, ns):
    sb = min(64, ns)
    base = row0 // sb
    sblk = lambda i, h: (i, h, 0, 0)
    return pl.pallas_call(
        functools.partial(_ret_sample_kernel, sb=sb),
        out_shape=[jax.ShapeDtypeStruct((ns, _RET_W), BF16),
                   jax.ShapeDtypeStruct(s0.shape, F32)],
        grid=(ns // sb, RET_HEADS),
        in_specs=[pl.BlockSpec(memory_space=pltpu.SMEM)]
                 + [pl.BlockSpec((sb, RET_DK), lambda i, h, col=col: (base + i, col * RET_HEADS + h))
                    for col in range(4)]
                 + [pl.BlockSpec((sb, None, RET_DK, RET_DV), sblk)],
        out_specs=[pl.BlockSpec((sb, RET_DV), lambda i, h: (i, h)),
                   pl.BlockSpec((sb, None, RET_DK, RET_DV), sblk)],
        compiler_params=_cparams("arbitrary", "arbitrary"),
    )(gamma, ret4, ret4, ret4, ret4, s0)


def _sink_softmax(s, mask, sink):
    if mask is not None:
        s = jnp.where(mask, s, -jnp.inf)
    m = jnp.maximum(jnp.max(s, axis=-1, keepdims=True), sink)
    p = jnp.exp(s - m)
    return p / (jnp.sum(p, axis=-1, keepdims=True) + jnp.exp(sink - m))


def _split_kv_heads(x):
    lo = lax.broadcasted_iota(I32, x.shape, 1) < SWA_HD
    h0_lo = jnp.where(lo, x, 0.0)
    h1_hi = jnp.where(lo, 0.0, x)
    return ((h0_lo, pltpu.roll(h0_lo, SWA_HD, 1)), (pltpu.roll(h1_hi, SWA_HD, 1), h1_hi))


SWA_STEP_BLOCKS = 4


def _swa_kernel(sink_ref, q_ref, kc_ref, kp_ref, vc_ref, vp_ref, o_ref):
    n = pl.program_id(1)
    c = WINDOW
    kk = jnp.concatenate([kp_ref[...], kc_ref[...]], axis=0)
    vv = jnp.concatenate([vp_ref[...], vc_ref[...]], axis=0)
    ks = [[a.astype(BF16) for a in pair] for pair in _split_kv_heads(kk)]
    vs = [[a.astype(BF16) for a in pair] for pair in _split_kv_heads(vv)]
    qi = lax.broadcasted_iota(I32, (2 * c, c), 0) % c
    ki = lax.broadcasted_iota(I32, (2 * c, c), 1)
    from_prev = ki > qi
    top = lax.broadcasted_iota(I32, (2 * c, 1), 0) < c
    units = [(s, kvh) for s in range(SWA_STEP_BLOCKS) for kvh in range(SWA_KV_HEADS)]
    scores = []
    for s, kvh in units:
        rows, keys = slice(s * c, (s + 1) * c), slice(s * c, (s + 2) * c)
        q2 = jnp.concatenate([q_ref[rows, 2 * kvh * LANES:(2 * kvh + 1) * LANES],
                              q_ref[rows, (2 * kvh + 1) * LANES:(2 * kvh + 2) * LANES]], axis=0)
        kcat = jnp.concatenate([ks[kvh][0][keys], ks[kvh][1][keys]], axis=0)
        scores.append(lax.dot_general(q2, kcat, (((1,), (1,)), ((), ())), preferred_element_type=F32))
    probs = []
    for (s, kvh), sc in zip(units, scores):
        ps = []
        for half in range(2):
            sink = jnp.where(top, sink_ref[4 * kvh + half], sink_ref[4 * kvh + 2 + half])
            s_prev = sc[:, half * 2 * c:half * 2 * c + c]
            s_own = sc[:, half * 2 * c + c:(half + 1) * 2 * c]
            if s == 0:
                s_prev = jnp.where(n > 0, s_prev, -jnp.inf)
            p = _sink_softmax(jnp.where(from_prev, s_prev, s_own), None, sink)
            ps += [jnp.where(from_prev, p, 0.0).astype(BF16), jnp.where(from_prev, 0.0, p).astype(BF16)]
        probs.append(jnp.concatenate(ps, axis=1))
    for (s, kvh), p in zip(units, probs):
        rows, keys = slice(s * c, (s + 1) * c), slice(s * c, (s + 2) * c)
        vcat = jnp.concatenate([vs[kvh][0][keys], vs[kvh][1][keys]], axis=0)
        o = jnp.dot(p, vcat, preferred_element_type=F32)
        o_ref[rows, 2 * kvh * LANES:(2 * kvh + 1) * LANES] = o[:c].astype(BF16)
        o_ref[rows, (2 * kvh + 1) * LANES:(2 * kvh + 2) * LANES] = o[c:].astype(BF16)


def _swa_prompt(sq, skv, sink, *, nb, t):
    rows = SWA_STEP_BLOCKS * WINDOW
    nsteps = t // rows
    nblk = t // WINDOW
    cur = lambda b, n: (b * nsteps + n, 0)
    curc = lambda col: (lambda b, n: (b * nsteps + n, col))
    prevc = lambda col: (lambda b, n: (b * nblk + jnp.maximum(n * SWA_STEP_BLOCKS - 1, 0), col))
    return pl.pallas_call(
        _swa_kernel,
        out_shape=jax.ShapeDtypeStruct((nb * t, _SWA_QW), BF16),
        grid=(nb, nsteps),
        in_specs=[pl.BlockSpec(memory_space=pltpu.SMEM),
                  pl.BlockSpec((rows, _SWA_QW), cur),
                  pl.BlockSpec((rows, _SWA_KW), curc(0)),
                  pl.BlockSpec((WINDOW, _SWA_KW), prevc(0)),
                  pl.BlockSpec((rows, _SWA_KW), curc(1)),
                  pl.BlockSpec((WINDOW, _SWA_KW), prevc(1))],
        out_specs=pl.BlockSpec((rows, _SWA_QW), cur),
        compiler_params=_cparams("arbitrary", "arbitrary"),
    )(sink, sq, skv, skv, skv, skv)


def _swa_sample_kernel(sink_ref, q_ref, kn_ref, vn_ref, kc_ref, vc_ref, o_ref, ko_ref, vo_ref, *, sb, w):
    pad = jnp.zeros((LANES - sb, _SWA_KW), F32)
    knt = jnp.concatenate([kn_ref[...], pad], axis=0).T
    vnt = jnp.concatenate([vn_ref[...], pad], axis=0).T
    kall = jnp.concatenate([kc_ref[b] for b in range(sb)] + [knt], axis=1)
    vall = jnp.concatenate([vc_ref[b] for b in range(sb)] + [vnt], axis=1)
    ncol = sb * w + LANES
    lo = lax.broadcasted_iota(I32, (sb, LANES), 1) < SWA_HD
    group = SWA_HEADS // SWA_KV_HEADS
    pieces = []
    for h in range(SWA_HEADS):
        slab = q_ref[:, (h // 2) * LANES:(h // 2 + 1) * LANES].astype(F32)
        mine = jnp.where(lo, slab, 0.0) if h % 2 == 0 else jnp.where(lo, 0.0, slab)
        pieces.append(mine if (h % 2) == (h // group) else pltpu.roll(mine, SWA_HD, 1))
    qrows = jnp.concatenate(pieces, axis=0)
    nrow = SWA_HEADS * sb
    s = _bdot(qrows, kall)
    rb = lax.broadcasted_iota(I32, (nrow, ncol), 0) % sb
    ci = lax.broadcasted_iota(I32, (nrow, ncol), 1)
    in_cache = (ci < sb * w) & (ci // w == rb) & ((w - ci % w) < WINDOW)
    mask = in_cache | (ci == sb * w + rb)
    sink_col = jnp.concatenate([jnp.full((sb, 1), sink_ref[h], F32) for h in range(SWA_HEADS)], axis=0)
    p = _sink_softmax(s, mask, sink_col)
    o = _bdot_nt(p, vall)
    for j in range(SWA_HEADS // 2):
        acc = jnp.zeros((sb, LANES), F32)
        for half in range(2):
            h = 2 * j + half
            oh = o[h * sb:(h + 1) * sb]
            own = jnp.where(lo, oh, 0.0) if h // group == 0 else jnp.where(lo, 0.0, oh)
            acc = acc + (own if (h // group) == half else pltpu.roll(own, SWA_HD, 1))
        o_ref[:, j * LANES:(j + 1) * LANES] = acc.astype(BF16)
    newest = lax.broadcasted_iota(I32, (_SWA_KW, w), 1) == w - 1
    for b in range(sb):
        ko_ref[b] = jnp.where(newest, knt[:, b:b + 1], pltpu.roll(kc_ref[b], w - 1, 1))
        vo_ref[b] = jnp.where(newest, vnt[:, b:b + 1], pltpu.roll(vc_ref[b], w - 1, 1))


def _swa_sample(sq, skv, cache_kt, cache_vt, sink, *, row0, ns):
    sb = min(16, ns)
    w = cache_kt.shape[2]
    base = row0 // sb
    blk = lambda i: (base + i, 0)
    cblk = lambda i: (i, 0, 0)
    cspec = pl.BlockSpec((sb, _SWA_KW, w), cblk)
    return pl.pallas_call(
        functools.partial(_swa_sample_kernel, sb=sb, w=w),
        out_shape=[jax.ShapeDtypeStruct((ns, _SWA_QW), BF16),
                   jax.ShapeDtypeStruct(cache_kt.shape, F32), jax.ShapeDtypeStruct(cache_vt.shape, F32)],
        grid=(ns // sb,),
        in_specs=[pl.BlockSpec(memory_space=pltpu.SMEM),
                  pl.BlockSpec((sb, _SWA_QW), blk),
                  pl.BlockSpec((sb, _SWA_KW), lambda i: (base + i, 0)),
                  pl.BlockSpec((sb, _SWA_KW), lambda i: (base + i, 1)),
                  cspec, cspec],
        out_specs=[pl.BlockSpec((sb, _SWA_QW), lambda i: (i, 0)), cspec, cspec],
        compiler_params=_cparams("arbitrary"),
    )(sink, sq, skv, skv, cache_kt, cache_vt)


def _route(logits):
    lane = lax.broadcasted_iota(I32, logits.shape, 1).astype(F32)
    big = float(1 << 20)
    neg = -jnp.inf

    def top(mask):
        v = jnp.max(jnp.where(mask, logits, neg), axis=-1, keepdims=True)
        i = jnp.min(jnp.where(mask & (logits == v), lane, big), axis=-1, keepdims=True)
        return v, i

    gmask = lane < N_GROUPS
    gmax, gsel = top(gmask)
    p_group = 1.0 / jnp.sum(jnp.where(gmask, jnp.exp(logits - gmax), 0.0), axis=-1, keepdims=True)
    first = N_GROUPS + gsel * EXPERTS_PER_GROUP
    emask = (lane >= first) & (lane < first + EXPERTS_PER_GROUP)
    v1, i1 = top(emask)
    v2, i2 = top(emask & (lane != i1))
    t = jnp.exp(v2 - v1)
    w1 = p_group / (1.0 + t)
    return (i1 - N_GROUPS).astype(I32), (i2 - N_GROUPS).astype(I32), w1, w1 * t


def _plan_tile(e1, e2, valid, carry):
    lane = lax.broadcasted_iota(I32, (TM, LANES), 1)
    oh1 = ((lane == e1) & valid).astype(F32)
    oh2 = ((lane == e2) & valid).astype(F32)
    oh = oh1 + oh2
    tri = (lax.broadcasted_iota(I32, (TM, TM), 0) > lax.broadcasted_iota(I32, (TM, TM), 1)).astype(BF16)
    before = _bdot(tri, oh)
    cnt = jnp.sum(oh, axis=0, keepdims=True)
    units = jnp.maximum(jnp.floor((cnt + (RUN - 1)) * (1.0 / RUN)), 1.0)
    upper = (lax.broadcasted_iota(I32, (LANES, LANES), 0) < lax.broadcasted_iota(I32, (LANES, LANES), 1))
    lstart = RUN * _bdot(jnp.broadcast_to(units, (SUBLANES, LANES)), upper.astype(BF16))[0:1]
    slot = lstart + before
    lp1 = jnp.sum(oh1 * slot, axis=-1, keepdims=True)
    lp2 = jnp.sum(oh2 * slot, axis=-1, keepdims=True)
    vcol = valid[:, 0:1]
    lp = jnp.where(lane == 0, jnp.where(vcol, lp1, -1.0), jnp.where(lane == 1, jnp.where(vcol, lp2, -1.0), 0.0))
    base = carry[...]
    carry[...] = base + RUN * units
    return lp, (RUN * units).astype(I32), lstart.astype(I32), base.astype(I32)


def _outproj_step(gated_of, oswa_of, x_of, g1_of, sh_of, sc_of, n_valid, sig_ref, n2_ref, wur_ref, wus_ref,
                  wo_ref, wrc_ref, br_ref, x1_o, h2_o, rt_o, lp_o, cnt_o, ls_o, gb_o, carry):
    subs = [slice(r, r + TM) for r in range(0, TT, TM)]
    merged = []
    for rows in subs:
        y_ret = jnp.dot(gated_of(rows), wur_ref[...], preferred_element_type=F32)
        y_swa = jnp.dot(oswa_of(rows), wus_ref[...], preferred_element_type=F32)
        d = y_ret.shape[1]
        merged.append((sig_ref[rows, :d].astype(F32) * y_ret + sig_ref[rows, d:].astype(F32) * y_swa).astype(BF16))
    hs = []
    for rows, m in zip(subs, merged):
        x1 = x_of(rows) + g1_of(rows) * jnp.dot(m, wo_ref[...], preferred_element_type=F32)
        x1_o[rows] = x1
        h2 = _rms(x1, n2_ref[...]) * (1.0 + sc_of(rows)) + sh_of(rows)
        hi = h2.astype(BF16)
        h2_o[rows] = hi
        hs.append((hi, (h2 - hi.astype(F32)).astype(BF16)))
    routed = []
    for rows, (hi, lo) in zip(subs, hs):
        both = jnp.dot(hi, wrc_ref[...], preferred_element_type=F32)
        logits = (both[:, :LANES] + both[:, LANES:]
                  + jnp.dot(lo, wrc_ref[:, :LANES], preferred_element_type=F32) + br_ref[...])
        e1, e2, w1, w2 = _route(logits)
        lane = lax.broadcasted_iota(I32, logits.shape, 1)
        rt_o[rows] = jnp.where(lane == 2, w1, jnp.where(lane == 3, w2, 0.0))
        routed.append((e1, e2))
    for sub, (e1, e2) in enumerate(routed):
        row = lax.broadcasted_iota(I32, (TM, LANES), 0) + sub * TM
        valid = (row >= 0) if n_valid is None else (row < n_valid)
        lp_o[sub * TM:(sub + 1) * TM], cnt_o[sub], ls_o[sub], gb_o[sub] = _plan_tile(e1, e2, valid, carry)


def _outproj_kernel(gtp_ref, gts_ref, osp_ref, oss_ref, sig_ref, xp_ref, xs_ref, g1p_ref, shp_ref, scp_ref,
                    g1s_ref, shs_ref, scs_ref, *rest, nps, ns):
    i = pl.program_id(0)
    is_s = i >= nps
    carry = rest[-1]

    @pl.when(i == 0)
    def _():
        carry[...] = jnp.zeros_like(carry)

    @pl.when(jnp.logical_not(is_s))
    def _():
        _outproj_step(lambda rows: gtp_ref[rows], lambda rows: osp_ref[rows], lambda rows: xp_ref[rows],
                      lambda rows: g1p_ref[...], lambda rows: shp_ref[...], lambda rows: scp_ref[...], None,
                      sig_ref, *rest)

    @pl.when(is_s)
    def _():
        _outproj_step(lambda rows: gts_ref[rows], lambda rows: oss_ref[rows], lambda rows: xs_ref[rows],
                      lambda rows: g1s_ref[rows], lambda rows: shs_ref[rows], lambda rows: scs_ref[rows], ns,
                      sig_ref, *rest)


def _outproj(gated_p, gated_s, oswa_p, oswa_s, sig2, xp, xs_pad, modp, mods, n2, wur, wus, wo, wr, br,
             *, nps, spb, nb, ns):
    d = xp.shape[1]
    sub = TT // TM
    nrow = (nps + 1) * TT
    row = lambda i: (i, 0)
    pstep = lambda i: (jnp.minimum(i, nps - 1), 0)
    pbatch = lambda col: (lambda i: (jnp.minimum(i // spb, nb - 1), 0, col))
    scol = lambda col: (lambda i: (0, col))
    const = lambda i: (0, 0)
    wr_hi = wr.astype(BF16)
    wrc = jnp.concatenate([wr_hi, (wr - wr_hi.astype(F32)).astype(BF16)], axis=1)
    meta = jax.ShapeDtypeStruct(((nps + 1) * sub, 1, LANES), I32)
    mspec = pl.BlockSpec((sub, 1, LANES), lambda i: (i, 0, 0))
    return pl.pallas_call(
        functools.partial(_outproj_kernel, nps=nps, ns=ns),
        out_shape=[jax.ShapeDtypeStruct((nrow, d), F32),
                   jax.ShapeDtypeStruct((nrow, d), BF16),
                   jax.ShapeDtypeStruct((nrow, LANES), F32),
                   jax.ShapeDtypeStruct((nrow, LANES), F32), meta, meta, meta],
        grid=(nps + 1,),
        in_specs=[pl.BlockSpec((TT, _RET_W), pstep), pl.BlockSpec((TT, _RET_W), const),
                  pl.BlockSpec((TT, _SWA_QW), pstep), pl.BlockSpec((TT, _SWA_QW), const),
                  pl.BlockSpec((TT, 2 * d), row),
                  pl.BlockSpec((TT, d), pstep), pl.BlockSpec((TT, d), const),
                  pl.BlockSpec((None, 1, d), pbatch(2)), pl.BlockSpec((None, 1, d), pbatch(3)),
                  pl.BlockSpec((None, 1, d), pbatch(4)),
                  pl.BlockSpec((TT, d), scol(2)), pl.BlockSpec((TT, d), scol(3)), pl.BlockSpec((TT, d), scol(4)),
                  pl.BlockSpec((1, d), const),
                  pl.BlockSpec(wur.shape, const), pl.BlockSpec(wus.shape, const), pl.BlockSpec(wo.shape, const),
                  pl.BlockSpec(wrc.shape, const), pl.BlockSpec((1, LANES), const)],
        out_specs=[pl.BlockSpec((TT, d), row), pl.BlockSpec((TT, d), row), pl.BlockSpec((TT, LANES), row),
                   pl.BlockSpec((TT, LANES), row), mspec, mspec, mspec],
        scratch_shapes=[pltpu.VMEM((1, LANES), F32)],
        compiler_params=_cparams("arbitrary"),
    )(gated_p, gated_s, oswa_p, oswa_s, sig2, xp, xs_pad, modp, modp, modp, mods, mods, mods, n2,
      wur, wus, wo, wrc, br)


def _units(rows):
    return rows.reshape(rows.shape[0] // RUN, RUN, rows.shape[1])


def _run_copy(src, dst, s_start, d_start, n, sem):
    return pltpu.make_async_copy(src.at[pl.ds(s_start, n)], dst.at[pl.ds(d_start, n)], sem)


def _tile_rows(cnt_ref, ls_ref, step):
    last = step * N_EXPERTS + N_EXPERTS - 1
    return ls_ref[last] + cnt_ref[last]


def _each_run(step, fn):
    for e in range(N_EXPERTS):
        fn(step * N_EXPERTS + e)


def _dispatch_kernel(cnt_ref, ls_ref, gd_ref, ps_ref, pn_ref, nu_ref, h_ref, lp_ref, xs_ref,
                     sorted_scr, zero_scr, sem, zsem, *, nt, maxt):
    i = pl.program_id(0)

    def pad(e):
        return _run_copy(zero_scr, xs_ref, 0, ps_ref[e], pn_ref[e], zsem)

    def tail(j):
        return _run_copy(zero_scr, xs_ref, 0, j * (TE // RUN), TE // RUN, zsem)

    def each_pad(fn):
        def body(e, c):
            @pl.when(pn_ref[e] > 0)
            def _():
                fn(pad(e))
            return c
        lax.fori_loop(0, N_EXPERTS, body, 0)

    def each_tail(fn):
        def body(j, c):
            fn(tail(j))
            return c
        lax.fori_loop(nu_ref[0], maxt, body, 0)

    @pl.when(i == 0)
    def _():
        zero_scr[...] = jnp.zeros_like(zero_scr)
        each_pad(lambda cp: cp.start())
        each_tail(lambda cp: cp.start())

    lpt = lp_ref[...].T
    slot = lax.broadcasted_iota(I32, (SLOTS, TM), 0).astype(F32)
    perm = ((slot == lpt[0:1]) | (slot == lpt[1:2])).astype(BF16)
    sorted_scr[i % 2] = _units(jnp.dot(perm, h_ref[...], preferred_element_type=F32).astype(BF16))

    def copy(step):
        return lambda k: _run_copy(sorted_scr.at[step % 2], xs_ref, ls_ref[k], gd_ref[k], cnt_ref[k], sem.at[step % 2])

    def wait_all(step):
        _run_copy(sorted_scr.at[step % 2], xs_ref, 0, 0, _tile_rows(cnt_ref, ls_ref, step), sem.at[step % 2]).wait()

    _each_run(i, lambda k: copy(i)(k).start())

    @pl.when(i > 0)
    def _():
        wait_all(i - 1)

    @pl.when(i == nt - 1)
    def _():
        wait_all(i)
        each_pad(lambda cp: cp.wait())
        each_tail(lambda cp: cp.wait())


def _dispatch(cnt, ls, gd, ps, pn, nu, h2, lp, *, nt, maxt):
    d = h2.shape[1]
    return pl.pallas_call(
        functools.partial(_dispatch_kernel, nt=nt, maxt=maxt),
        out_shape=jax.ShapeDtypeStruct((maxt * TE // RUN, RUN, d), BF16),
        grid_spec=pltpu.PrefetchScalarGridSpec(
            num_scalar_prefetch=6,
            grid=(nt,),
            in_specs=[pl.BlockSpec((TM, d), lambda i, *_: (i, 0)),
                      pl.BlockSpec((TM, LANES), lambda i, *_: (i, 0))],
            out_specs=pl.BlockSpec(memory_space=pl.ANY),
            scratch_shapes=[pltpu.VMEM((2, SLOTS // RUN, RUN, d), BF16), pltpu.VMEM((TE // RUN, RUN, d), BF16),
                            pltpu.SemaphoreType.DMA((2,)), pltpu.SemaphoreType.DMA(())]),
        compiler_params=_cparams("arbitrary"),
    )(cnt, ls, gd, ps, pn, nu, h2, lp)


def _experts_kernel(te_ref, nu_ref, start_ref, nxt_ref, par_ref, x_ref, w1_hbm, w3_hbm, w2_hbm, y_ref,
                    w1f, w3f, w2f, w1b, w3b, w2b, sem):
    j = pl.program_id(0)
    e = te_ref[j]

    def fetch(ex, slot):
        return [pltpu.make_async_copy(src.at[ex], dst.at[slot], sem.at[slot, n])
                for n, (src, dst) in enumerate(((w1_hbm, w1f), (w3_hbm, w3f), (w2_hbm, w2f)))]

    @pl.when(j == 0)
    def _():
        for cp in fetch(e, par_ref[e]):
            cp.start()

    @pl.when((j == start_ref[e]) & (j < nu_ref[0]))
    def _():
        slot = par_ref[e]

        @pl.when(nxt_ref[e] >= 0)
        def _():
            for cp in fetch(nxt_ref[e], 1 - slot):
                cp.start()

        for cp in fetch(e, slot):
            cp.wait()
        w1b[...] = w1f[slot].astype(BF16)
        w3b[...] = w3f[slot].astype(BF16)
        w2b[...] = w2f[slot].astype(BF16)

    @pl.when(j < nu_ref[0])
    def _():
        subs = [slice(r // RUN, (r + TE_SUB) // RUN) for r in range(0, TE, TE_SUB)]
        xs = [x_ref[u].reshape(TE_SUB, x_ref.shape[-1]) for u in subs]
        ab = [(jnp.dot(x, w1b[...], preferred_element_type=F32),
               jnp.dot(x, w3b[...], preferred_element_type=F32)) for x in xs]
        hid = [(_silu(a) * b).astype(BF16) for a, b in ab]
        for u, h in zip(subs, hid):
            y_ref[u] = _units(jnp.dot(h, w2b[...], preferred_element_type=F32).astype(BF16))


def _experts(te, nu, start, nxt, par, xs, w1, w3, w2, *, maxt):
    d = xs.shape[2]
    f = w1.shape[2]
    used = lambda j, te, nu, *_: (jnp.minimum(j, nu[0] - 1), 0, 0)
    hbm = pl.BlockSpec(memory_space=pl.ANY)
    return pl.pallas_call(
        _experts_kernel,
        out_shape=jax.ShapeDtypeStruct(xs.shape, BF16),
        grid_spec=pltpu.PrefetchScalarGridSpec(
            num_scalar_prefetch=5,
            grid=(maxt,),
            in_specs=[pl.BlockSpec((TE // RUN, RUN, d), used), hbm, hbm, hbm],
            out_specs=pl.BlockSpec((TE // RUN, RUN, d), used),
            scratch_shapes=[pltpu.VMEM((2, d, f), F32), pltpu.VMEM((2, d, f), F32), pltpu.VMEM((2, f, d), F32),
                            pltpu.VMEM((d, f), BF16), pltpu.VMEM((d, f), BF16), pltpu.VMEM((f, d), BF16),
                            pltpu.SemaphoreType.DMA((2, 3))]),
        input_output_aliases={5: 0},
        compiler_params=_cparams("arbitrary"),
    )(te, nu, start, nxt, par, xs, w1, w3, w2)


def _combine_kernel(cnt_ref, ls_ref, gd_ref, ys_ref, lp_ref, rt_ref, x1_ref, g2p_ref, g2s_ref, fg_ref,
                    yp_o, ys_o, ybuf, sem, *, npt):
    i = pl.program_id(0)
    nt = npt + 1

    def fetch(step):
        ybuf[step % 2] = jnp.zeros(ybuf.shape[1:], BF16)
        _each_run(step, lambda k: _run_copy(ys_ref, ybuf.at[step % 2], gd_ref[k], ls_ref[k], cnt_ref[k],
                                            sem.at[step % 2]).start())

    @pl.when(i == 0)
    def _():
        fetch(0)

    @pl.when(i + 1 < nt)
    def _():
        fetch(i + 1)

    _run_copy(ys_ref, ybuf.at[i % 2], 0, 0, _tile_rows(cnt_ref, ls_ref, i), sem.at[i % 2]).wait()

    yb = ybuf[i % 2].reshape(SLOTS, x1_ref.shape[1])
    slot = lax.broadcasted_iota(I32, (TM, SLOTS), 1).astype(F32)
    lp = lp_ref[...]
    rt = rt_ref[...]

    def unsort(col):
        return jnp.dot((slot == lp[:, col:col + 1]).astype(BF16), yb, preferred_element_type=F32)

    moe = rt[:, 2:3] * unsort(0) + rt[:, 3:4] * unsort(1)
    g2 = jnp.where(i >= npt, g2s_ref[...], g2p_ref[...])
    y = _rms(x1_ref[...] + g2 * moe, fg_ref[...])

    @pl.when(i < npt)
    def _():
        yp_o[...] = y

    @pl.when(i >= npt)
    def _():
        ys_o[...] = y


def _combine(cnt, ls, gd, ys, lp, rt, x1, modp, mods, fg, *, npt, tpb, nb):
    d = x1.shape[1]
    nt = npt + 1
    row = lambda i, *_: (i, 0)
    return pl.pallas_call(
        functools.partial(_combine_kernel, npt=npt),
        out_shape=[jax.ShapeDtypeStruct((npt * TM, d), F32), jax.ShapeDtypeStruct((TM, d), F32)],
        grid_spec=pltpu.PrefetchScalarGridSpec(
            num_scalar_prefetch=3,
            grid=(nt,),
            in_specs=[pl.BlockSpec(memory_space=pl.ANY),
                      pl.BlockSpec((TM, LANES), row), pl.BlockSpec((TM, LANES), row), pl.BlockSpec((TM, d), row),
                      pl.BlockSpec((None, 1, d), lambda i, *_: (jnp.minimum(i // tpb, nb - 1), 0, 5)),
                      pl.BlockSpec((TM, d), lambda i, *_: (0, 5)),
                      pl.BlockSpec((1, d), lambda i, *_: (0, 0))],
            out_specs=[pl.BlockSpec((TM, d), lambda i, *_: (jnp.minimum(i, npt - 1), 0)),
                       pl.BlockSpec((TM, d), lambda i, *_: (0, 0))],
            scratch_shapes=[pltpu.VMEM((2, SLOTS // RUN, RUN, d), BF16), pltpu.SemaphoreType.DMA((2,))]),
        compiler_params=_cparams("arbitrary"),
    )(cnt, ls, gd, ys, lp, rt, x1, modp, mods, fg)


TAB_LO = 64


def _rotation_tables(t):
    inv_r = jnp.repeat(1.0 / (ROPE_THETA ** jnp.linspace(0.0, 1.0, RET_DK // 2, dtype=F32)), 2)
    sign_r = jnp.where(jnp.arange(RET_DK) % 2 == 0, -1.0, 1.0).astype(F32)
    inv_w = jnp.tile(ROPE_THETA ** (-jnp.arange(0, SWA_HD, 2, dtype=F32) / SWA_HD), LANES // (SWA_HD // 2))
    sign_w = jnp.where(jnp.arange(LANES) % SWA_HD < SWA_HD // 2, -1.0, 1.0).astype(F32)
    hi = (jnp.arange(t // TAB_LO, dtype=I32) * TAB_LO).astype(F32)[:, None]
    lo = jnp.arange(TAB_LO, dtype=I32).astype(F32)[:, None]
    past = jnp.full((1, 1), PAST_LEN, F32)

    def pair(inv, sign):
        a, b = hi * inv[None, :], lo * inv[None, :]
        ca, sa, cb, sb = jnp.cos(a)[:, None], jnp.sin(a)[:, None], jnp.cos(b)[None], jnp.sin(b)[None]
        cos = (ca * cb - sa * sb).reshape(t, LANES)
        sin = (sa * cb + ca * sb).reshape(t, LANES) * sign[None, :]
        ang = past * inv[None, :]
        return (cos, sin), (jnp.cos(ang), jnp.sin(ang) * sign[None, :])

    (pr, sr_), (pw, sw_) = pair(inv_r, sign_r), pair(inv_w, sign_w)
    return jnp.concatenate(pr + pw, axis=1), jnp.concatenate(sr_ + sw_, axis=1)


def kernel(x_prompt, x_sample, c_prompt, c_sample, state_ret, cache_swa_k, cache_swa_v, w_ada, b_ada, norm1_g, norm2_g, w_in, w_up_ret, w_up_swa, w_o, sink, w_rg, b_rg, w_re, b_re, w1, w3, w2, final_g):
    nb, t, d = x_prompt.shape
    ns, dec_seq, _ = x_sample.shape
    depth = w_ada.shape[0]
    assert depth == 1 and dec_seq == 1, "single layer, one new token per sequence"
    assert t % TT == 0 and ns <= TM and ns % 16 == 0 and d % LANES == 0
    assert t % RET_BLOCK == 0 and t % (SWA_STEP_BLOCKS * WINDOW) == 0 and t % TAB_LO == 0
    assert N_GROUPS + N_EXPERTS <= LANES
    w = cache_swa_k.shape[2]
    tpb = t // TM
    npt = nb * tpb
    spb = t // TT
    nps = nb * spb
    nt = (nps + 1) * (TT // TM)
    np_rows = nb * t
    n_tok = np_rows + ns
    maxt = -(-(2 * n_tok + nt * N_EXPERTS * RUN + N_EXPERTS * (TE - 1)) // TE)

    xp = x_prompt.reshape(np_rows, d)
    xs_pad = jnp.pad(x_sample.reshape(ns, d), ((0, TT - ns), (0, 0)))

    c_all = jnp.concatenate([jnp.pad(c_sample, ((0, TT - ns), (0, 0))),
                             jnp.pad(c_prompt, ((0, SUBLANES - nb % SUBLANES), (0, 0)))])
    mods = _modulation(c_all, w_ada[0], b_ada[0])
    modp = mods[TT:TT + nb].reshape(nb, 1, 6 * d)

    tabs_p, tabs_s = _rotation_tables(t)
    ret4, sq, skv, sig2 = _inproj(
        xp, xs_pad, modp, mods, norm1_g, w_in[0].astype(BF16), tabs_p, tabs_s, nps=nps, spb=spb, nb=nb)

    dm, qd, kd, cd, gamma = _ret_tables(RET_BLOCK)
    gated_p, st_p = _retention_prompt(ret4, (dm, qd, kd, cd), nb=nb, t=t)
    gated_s, st_s = _retention_sample(ret4, state_ret[0], gamma, row0=np_rows, ns=ns)
    oswa_p = _swa_prompt(sq, skv, sink[0], nb=nb, t=t)
    to_t = lambda c: jnp.transpose(c[0], (0, 2, 3, 1)).reshape(ns, _SWA_KW, w)
    from_t = lambda c: jnp.transpose(c.reshape(ns, SWA_KV_HEADS, SWA_HD, w), (0, 3, 1, 2))[None]
    oswa_s, ks_new, vs_new = _swa_sample(sq, skv, to_t(cache_swa_k), to_t(cache_swa_v), sink[0],
                                         row0=np_rows, ns=ns)
    gated_s = jnp.pad(gated_s, ((0, TT - ns), (0, 0)))
    oswa_s = jnp.pad(oswa_s, ((0, TT - ns), (0, 0)))

    wr = jnp.pad(jnp.concatenate([w_rg[0], w_re[0]], axis=1), ((0, 0), (0, LANES - N_GROUPS - N_EXPERTS)))
    br = jnp.pad(jnp.concatenate([b_rg[0], b_re[0]]), (0, LANES - N_GROUPS - N_EXPERTS)).reshape(1, LANES)
    x1, h2, rt, lp, cnt, ls, gb = _outproj(
        gated_p, gated_s, oswa_p, oswa_s, sig2, xp, xs_pad, modp, mods, norm2_g,
        w_up_ret[0].astype(BF16), w_up_swa[0].astype(BF16), w_o[0].astype(BF16), wr, br,
        nps=nps, spb=spb, nb=nb, ns=ns)
    cnt = cnt[:, 0, :N_EXPERTS]
    ls = ls[:, 0, :N_EXPERTS]
    gb = gb[:, 0, :N_EXPERTS]
    seg = jnp.sum(cnt, axis=0)
    tiles = (seg + TE - 1) // TE
    tile_end = jnp.cumsum(tiles)
    row_start = (tile_end - tiles) * TE
    gd = (gb + row_start[None, :]).reshape(-1)
    n_used = tile_end[-1:]
    jj = jnp.minimum(jnp.arange(maxt, dtype=I32), n_used[0] - 1)
    te = jnp.minimum(jnp.sum((tile_end[None, :] <= jj[:, None]).astype(I32), axis=1), N_EXPERTS - 1)
    cnt, ls, gd = cnt.reshape(-1) // RUN, ls.reshape(-1) // RUN, gd // RUN
    n_used = n_used.astype(I32)
    xs = _dispatch(cnt, ls, gd, (row_start + seg) // RUN, (tiles * TE - seg) // RUN, n_used, h2, lp,
                   nt=nt, maxt=maxt)
    has = tiles > 0
    eidx = jnp.arange(N_EXPERTS, dtype=I32)
    later = jnp.where(has[None, :] & (eidx[None, :] > eidx[:, None]), eidx[None, :], N_EXPERTS)
    nxt_e = jnp.min(later, axis=1)
    nxt_e = jnp.where(nxt_e < N_EXPERTS, nxt_e, -1).astype(I32)
    par_e = ((jnp.cumsum(has.astype(I32)) - 1) % 2).astype(I32)
    ys = _experts(te, n_used, (tile_end - tiles).astype(I32), nxt_e, par_e, xs, w1[0], w3[0], w2[0], maxt=maxt)
    y_p, y_s = _combine(cnt, ls, gd, ys, lp, rt, x1, modp, mods, final_g.reshape(1, d), npt=npt, tpb=tpb, nb=nb)

    y_prompt = y_p.reshape(nb, t, d)
    y_sample = y_s[:ns].reshape(ns, 1, d)
    wk = min(WINDOW, t)
    tails = jnp.stack([skv[(b + 1) * t - wk:(b + 1) * t] for b in range(nb)])
    skp, svp = (tails[:, :, part * _SWA_KW:(part + 1) * _SWA_KW].reshape(nb, wk, SWA_KV_HEADS, SWA_HD)
                for part in range(2))
    return (y_prompt, y_sample, st_p[None], st_s[None], skp[None], svp[None], from_t(ks_new), from_t(vs_new))
```

```python
import functools
import math

import jax
import jax.numpy as jnp
from jax import lax
from jax.experimental import pallas as pl
from jax.experimental.pallas import tpu as pltpu

F32 = jnp.float32
BF16 = jnp.bfloat16
I32 = jnp.int32

PAST_LEN = 8192
RET_HEADS = 4
RET_DK = 128
RET_DV = 128
RET_CHUNK = 128
SWA_HEADS = 8
SWA_KV_HEADS = 2
SWA_HD = 64
WINDOW = 128
ROPE_THETA = 10000.0
N_GROUPS = 4
EXPERTS_PER_GROUP = 8
N_EXPERTS = N_GROUPS * EXPERTS_PER_GROUP
D_EXPERT = 256
NORM_EPS = 1e-6

LANES = 128
SUBLANES = 8
TM = 256
TT = 2 * TM
RUN = 16
SLOTS = 2 * TM + N_EXPERTS * RUN
TE = 512
TE_SUB = 256
VMEM_LIMIT = 56 * 1024 * 1024
TAB_LO = 64

_RET_W = RET_HEADS * RET_DK
_SWA_QW = SWA_HEADS * SWA_HD
_SWA_KW = SWA_KV_HEADS * SWA_HD


def _cparams(*sem):
    return pltpu.CompilerParams(dimension_semantics=sem, vmem_limit_bytes=VMEM_LIMIT)


def _sigmoid(x):
    return 1.0 / (1.0 + jnp.exp(-x))


def _silu(x):
    return x * _sigmoid(x)


def _bdot(a, b):
    return jnp.dot(a.astype(BF16), b.astype(BF16), preferred_element_type=F32)


def _bdot_nt(a, b):
    return lax.dot_general(a.astype(BF16), b.astype(BF16), (((1,), (1,)), ((), ())), preferred_element_type=F32)


def _mod_kernel(c_ref, w_ref, b_ref, o_ref):
    o_ref[...] = _bdot(_silu(c_ref[...]), w_ref[...]) + b_ref[...]


def _modulation(c_all, w_ada, b_ada):
    rows, d = c_all.shape
    n = w_ada.shape[1]
    return pl.pallas_call(
        _mod_kernel,
        out_shape=jax.ShapeDtypeStruct((rows, n), F32),
        grid=(n // d,),
        in_specs=[pl.BlockSpec((rows, d), lambda j: (0, 0)),
                  pl.BlockSpec((d, d), lambda j: (0, j)),
                  pl.BlockSpec((1, d), lambda j: (0, j))],
        out_specs=pl.BlockSpec((rows, d), lambda j: (0, j)),
        compiler_params=_cparams("arbitrary"),
    )(c_all, w_ada, b_ada.reshape(1, n))


def _rms(x, g):
    return x * lax.rsqrt(jnp.mean(x * x, axis=-1, keepdims=True) + NORM_EPS) * g


def _pair_rotate(z, cos, sin_signed):
    n = z.shape[-1]
    lane = lax.broadcasted_iota(I32, z.shape, 1)
    partner = jnp.where((lane & 1) == 0, pltpu.roll(z, n - 1, 1), pltpu.roll(z, 1, 1))
    reps = n // LANES
    cos = jnp.concatenate([cos] * reps, axis=1) if reps > 1 else cos
    sin_signed = jnp.concatenate([sin_signed] * reps, axis=1) if reps > 1 else sin_signed
    return z * cos + partner * sin_signed


def _half_rotate(z, cos, sin_signed):
    n = z.shape[-1]
    half = SWA_HD // 2
    lane = lax.broadcasted_iota(I32, z.shape, 1)
    partner = jnp.where((lane & (SWA_HD - 1)) < half, pltpu.roll(z, n - half, 1), pltpu.roll(z, half, 1))
    reps = n // LANES
    cos = jnp.concatenate([cos] * reps, axis=1) if reps > 1 else cos
    sin_signed = jnp.concatenate([sin_signed] * reps, axis=1) if reps > 1 else sin_signed
    return z * cos + partner * sin_signed


def _inproj_step(x_of, sh_of, sc_of, tabs_of, n1_ref, w_ref, outs):
    ret_o, sq_o, skv_o, sig_o = outs
    d = w_ref.shape[0]
    subs = []
    for r in range(0, TT, TM):
        rows = slice(r, r + TM)
        h = (_rms(x_of(rows), n1_ref[...]) * (1.0 + sc_of(rows)) + sh_of(rows)).astype(BF16)
        subs.append((rows, h, tabs_of(rows)))

    def seg(h, a, b):
        return jnp.dot(h, w_ref[:, a:b], preferred_element_type=F32)

    o = 0
    for rows, h, (cr, sr, cw, sw) in subs:
        ret_o[rows, 0:_RET_W] = _pair_rotate(seg(h, o, o + _RET_W), cr, sr).astype(BF16)
    o += _RET_W
    for rows, h, (cr, sr, cw, sw) in subs:
        ret_o[rows, _RET_W:2 * _RET_W] = (_pair_rotate(seg(h, o, o + _RET_W), cr, sr)
                                          * (RET_DK ** -0.5)).astype(BF16)
    o += _RET_W
    for rows, h, _ in subs:
        ret_o[rows, 2 * _RET_W:3 * _RET_W] = seg(h, o, o + _RET_W).astype(BF16)
    o += _RET_W
    for rows, h, _ in subs:
        ret_o[rows, 3 * _RET_W:4 * _RET_W] = _silu(seg(h, o, o + _RET_W)).astype(BF16)
    o += _RET_W
    for rows, h, (cr, sr, cw, sw) in subs:
        sq_o[rows] = (_half_rotate(seg(h, o, o + _SWA_QW), cw, sw) * (SWA_HD ** -0.5)).astype(BF16)
    o += _SWA_QW
    for rows, h, (cr, sr, cw, sw) in subs:
        zkv = seg(h, o, o + 2 * _SWA_KW)
        skv_o[rows, :_SWA_KW] = _half_rotate(zkv[:, :_SWA_KW], cw, sw)
        skv_o[rows, _SWA_KW:] = zkv[:, _SWA_KW:]
    o += 2 * _SWA_KW
    for rows, h, _ in subs:
        sig_o[rows, :d] = _sigmoid(seg(h, o, o + d)).astype(BF16)
    o += d
    for rows, h, _ in subs:
        sig_o[rows, d:] = _sigmoid(seg(h, o, o + d)).astype(BF16)


def _prompt_tables(th_ref, tl_ref, rows):
    tabs = []
    for rot in range(2):
        cb, sb, cbs, sbs = (tl_ref[:, (4 * rot + q) * LANES:(4 * rot + q + 1) * LANES] for q in range(4))
        cos, sin = [], []
        for g in range(rows.start // TAB_LO, rows.stop // TAB_LO):
            ca = th_ref[g:g + 1, 2 * rot * LANES:(2 * rot + 1) * LANES]
            sa = th_ref[g:g + 1, (2 * rot + 1) * LANES:(2 * rot + 2) * LANES]
            cos.append(ca * cb - sa * sb)
            sin.append(sa * cbs + ca * sbs)
        tabs += [jnp.concatenate(cos, axis=0), jnp.concatenate(sin, axis=0)]
    return tuple(tabs)


def _inproj_kernel(xp_ref, xs_ref, shp_ref, scp_ref, shs_ref, scs_ref, n1_ref, w_ref, th_ref, tl_ref, ts_ref,
                   *outs, nps):
    is_s = pl.program_id(0) >= nps

    @pl.when(jnp.logical_not(is_s))
    def _():
        _inproj_step(lambda rows: xp_ref[rows], lambda rows: shp_ref[...], lambda rows: scp_ref[...],
                     functools.partial(_prompt_tables, th_ref, tl_ref),
                     n1_ref, w_ref, outs)

    @pl.when(is_s)
    def _():
        _inproj_step(lambda rows: xs_ref[rows], lambda rows: shs_ref[rows], lambda rows: scs_ref[rows],
                     lambda rows: tuple(ts_ref[:, q * LANES:(q + 1) * LANES] for q in range(4)),
                     n1_ref, w_ref, outs)


def _inproj(xp, xs_pad, modp, mods, n1, w_in_b, tab_hi, tab_lo, tab_s, *, nps, spb, nb):
    d = xp.shape[1]
    nrow = (nps + 1) * TT
    n_in = w_in_b.shape[1]
    pstep = lambda i: (jnp.minimum(i, nps - 1), 0)
    pbatch = lambda col: (lambda i: (jnp.minimum(i // spb, nb - 1), 0, col))
    tab_idx = lambda i: (jnp.where(i < nps, i % spb, 0), 0)
    out_cols = [(4 * _RET_W, BF16), (_SWA_QW, BF16), (2 * _SWA_KW, F32), (2 * d, BF16)]
    return pl.pallas_call(
        functools.partial(_inproj_kernel, nps=nps),
        out_shape=[jax.ShapeDtypeStruct((nrow, c), t) for c, t in out_cols],
        grid=(nps + 1,),
        in_specs=[pl.BlockSpec((TT, d), pstep),
                  pl.BlockSpec((TT, d), lambda i: (0, 0)),
                  pl.BlockSpec((None, 1, d), pbatch(0)),
                  pl.BlockSpec((None, 1, d), pbatch(1)),
                  pl.BlockSpec((TT, d), lambda i: (0, 0)),
                  pl.BlockSpec((TT, d), lambda i: (0, 1)),
                  pl.BlockSpec((1, d), lambda i: (0, 0)),
                  pl.BlockSpec((d, n_in), lambda i: (0, 0))]
                 + [pl.BlockSpec((TT // TAB_LO, 4 * LANES), tab_idx),
                    pl.BlockSpec(tab_lo.shape, lambda i: (0, 0)), pl.BlockSpec(tab_s.shape, lambda i: (0, 0))],
        out_specs=[pl.BlockSpec((TT, c), lambda i: (i, 0)) for c, _ in out_cols],
        compiler_params=_cparams("arbitrary"),
    )(xp, xs_pad, modp, modp, mods, mods, n1, w_in_b, tab_hi, tab_lo, tab_s)


RET_BLOCK = 512


def _ret_kernel(q_ref, k_ref, v_ref, g_ref, o_ref, st_ref, s_scr, dm_scr, qd_scr, kd_scr, *, nsteps):
    step = pl.program_id(1)
    rows = q_ref.shape[0]
    log_gamma = [math.log(1.0 - 2.0 ** (-5.0 - h)) for h in range(RET_HEADS)]

    @pl.when((pl.program_id(0) == 0) & (step == 0))
    def _():
        diff = (lax.broadcasted_iota(I32, (rows, rows), 0) - lax.broadcasted_iota(I32, (rows, rows), 1)).astype(F32)
        idx = lax.broadcasted_iota(I32, (rows, RET_DV), 0).astype(F32)
        for h, ld in enumerate(log_gamma):
            dm_scr[h] = jnp.where(diff >= 0, jnp.exp(ld * jnp.maximum(diff, 0.0)), 0.0)
            qd_scr[h] = jnp.exp(ld * (idx + 1.0))
            kd_scr[h] = jnp.exp(ld * (rows - 1.0 - idx))

    @pl.when(step == 0)
    def _():
        s_scr[...] = jnp.zeros_like(s_scr)

    for h in range(RET_HEADS):
        sl = slice(h * RET_DK, (h + 1) * RET_DK)
        state = s_scr[h]
        q, k, v = q_ref[:, sl], k_ref[:, sl], v_ref[:, sl]
        att = _bdot_nt(q, k) * dm_scr[h]
        o = _bdot(att, v) + _bdot(q.astype(F32) * qd_scr[h], state)
        kd = (k.astype(F32) * kd_scr[h]).astype(BF16)
        kv = lax.dot_general(kd, v, (((0,), (0,)), ((), ())), preferred_element_type=F32)
        s_scr[h] = math.exp(log_gamma[h] * rows) * state + kv
        o = o * lax.rsqrt(jnp.mean(o * o, axis=-1, keepdims=True) + NORM_EPS)
        o_ref[:, sl] = (o * g_ref[:, sl].astype(F32)).astype(BF16)

    @pl.when(step == nsteps - 1)
    def _():
        st_ref[...] = s_scr[...]


def _retention_prompt(ret4, *, nb, t):
    rows = RET_BLOCK
    nsteps = t // rows
    blk = lambda b, c: (b * nsteps + c, 0)
    return pl.pallas_call(
        functools.partial(_ret_kernel, nsteps=nsteps),
        out_shape=[jax.ShapeDtypeStruct((nb * t, _RET_W), BF16),
                   jax.ShapeDtypeStruct((nb, RET_HEADS, RET_DK, RET_DV), F32)],
        grid=(nb, nsteps),
        in_specs=[pl.BlockSpec((rows, _RET_W), lambda b, c, col=col: (b * nsteps + c, col)) for col in range(4)],
        out_specs=[pl.BlockSpec((rows, _RET_W), blk),
                   pl.BlockSpec((None, RET_HEADS, RET_DK, RET_DV), lambda b, c: (b, 0, 0, 0))],
        scratch_shapes=[pltpu.VMEM((RET_HEADS, RET_DK, RET_DV), F32), pltpu.VMEM((RET_HEADS, rows, rows), F32),
                        pltpu.VMEM((RET_HEADS, rows, RET_DV), F32), pltpu.VMEM((RET_HEADS, rows, RET_DV), F32)],
        compiler_params=_cparams("arbitrary", "arbitrary"),
    )(ret4, ret4, ret4, ret4)


def _ret_sample_kernel(gam_ref, q_ref, k_ref, v_ref, g_ref, s0_ref, o_ref, st_ref, *, sb):
    gamma = gam_ref[pl.program_id(1)]
    q = q_ref[...].astype(F32)
    k = k_ref[...].astype(F32)
    v = v_ref[...].astype(F32)
    rows = sb * RET_DK
    s2 = s0_ref[...].reshape(rows, RET_DV)
    col_b = lax.broadcasted_iota(I32, (sb, rows), 1) // RET_DK
    row_b = lax.broadcasted_iota(I32, (sb, rows), 0)
    qexp = jnp.where(col_b == row_b, jnp.concatenate([q * gamma] * sb, axis=1), 0.0)
    o = jnp.sum(q * k, axis=-1, keepdims=True) * v + _bdot(qexp, s2)
    o = o * lax.rsqrt(jnp.mean(o * o, axis=-1, keepdims=True) + NORM_EPS)
    o_ref[...] = (o * g_ref[...].astype(F32)).astype(BF16)
    rep = (lax.broadcasted_iota(I32, (rows, sb), 0) // RET_DK == lax.broadcasted_iota(I32, (rows, sb), 1))
    rep = rep.astype(BF16)
    krep = _bdot(rep, k)
    vrep = _bdot(rep, v)
    eye = (lax.broadcasted_iota(I32, (rows, RET_DK), 0) % RET_DK == lax.broadcasted_iota(I32, (rows, RET_DK), 1))
    kcol = jnp.sum(jnp.where(eye, krep, 0.0), axis=-1, keepdims=True)
    st_ref[...] = (gamma * s2 + kcol * vrep).reshape(sb, RET_DK, RET_DV)


def _retention_sample(ret4, s0, gamma, *, row0, ns):
    sb = min(64, ns)
    base = row0 // sb
    sblk = lambda i, h: (i, h, 0, 0)
    return pl.pallas_call(
        functools.partial(_ret_sample_kernel, sb=sb),
        out_shape=[jax.ShapeDtypeStruct((ns, _RET_W), BF16),
                   jax.ShapeDtypeStruct(s0.shape, F32)],
        grid=(ns // sb, RET_HEADS),
        in_specs=[pl.BlockSpec(memory_space=pltpu.SMEM)]
                 + [pl.BlockSpec((sb, RET_DK), lambda i, h, col=col: (base + i, col * RET_HEADS + h))
                    for col in range(4)]
                 + [pl.BlockSpec((sb, None, RET_DK, RET_DV), sblk)],
        out_specs=[pl.BlockSpec((sb, RET_DV), lambda i, h: (i, h)),
                   pl.BlockSpec((sb, None, RET_DK, RET_DV), sblk)],
        compiler_params=_cparams("arbitrary", "arbitrary"),
    )(gamma, ret4, ret4, ret4, ret4, s0)


def _sink_softmax(s, mask, sink):
    if mask is not None:
        s = jnp.where(mask, s, -jnp.inf)
    m = jnp.maximum(jnp.max(s, axis=-1, keepdims=True), sink)
    p = jnp.exp(s - m)
    return p / (jnp.sum(p, axis=-1, keepdims=True) + jnp.exp(sink - m))


def _split_kv_heads(x):
    lo = lax.broadcasted_iota(I32, x.shape, 1) < SWA_HD
    h0_lo = jnp.where(lo, x, 0.0)
    h1_hi = jnp.where(lo, 0.0, x)
    return ((h0_lo, pltpu.roll(h0_lo, SWA_HD, 1)), (pltpu.roll(h1_hi, SWA_HD, 1), h1_hi))


SWA_STEP_BLOCKS = 4


def _swa_kernel(sink_ref, q_ref, kc_ref, kp_ref, vc_ref, vp_ref, o_ref):
    n = pl.program_id(1)
    c = WINDOW
    kk = jnp.concatenate([kp_ref[...], kc_ref[...]], axis=0)
    vv = jnp.concatenate([vp_ref[...], vc_ref[...]], axis=0)
    ks = [[a.astype(BF16) for a in pair] for pair in _split_kv_heads(kk)]
    vs = [[a.astype(BF16) for a in pair] for pair in _split_kv_heads(vv)]
    qi = lax.broadcasted_iota(I32, (2 * c, c), 0) % c
    ki = lax.broadcasted_iota(I32, (2 * c, c), 1)
    from_prev = ki > qi
    top = lax.broadcasted_iota(I32, (2 * c, 1), 0) < c
    units = [(s, kvh) for s in range(SWA_STEP_BLOCKS) for kvh in range(SWA_KV_HEADS)]
    scores = []
    for s, kvh in units:
        rows, keys = slice(s * c, (s + 1) * c), slice(s * c, (s + 2) * c)
        q2 = jnp.concatenate([q_ref[rows, 2 * kvh * LANES:(2 * kvh + 1) * LANES],
                              q_ref[rows, (2 * kvh + 1) * LANES:(2 * kvh + 2) * LANES]], axis=0)
        kcat = jnp.concatenate([ks[kvh][0][keys], ks[kvh][1][keys]], axis=0)
        scores.append(lax.dot_general(q2, kcat, (((1,), (1,)), ((), ())), preferred_element_type=F32))
    probs = []
    for (s, kvh), sc in zip(units, scores):
        ps = []
        for half in range(2):
            sink = jnp.where(top, sink_ref[4 * kvh + half], sink_ref[4 * kvh + 2 + half])
            s_prev = sc[:, half * 2 * c:half * 2 * c + c]
            s_own = sc[:, half * 2 * c + c:(half + 1) * 2 * c]
            if s == 0:
                s_prev = jnp.where(n > 0, s_prev, -jnp.inf)
            p = _sink_softmax(jnp.where(from_prev, s_prev, s_own), None, sink)
            ps += [jnp.where(from_prev, p, 0.0).astype(BF16), jnp.where(from_prev, 0.0, p).astype(BF16)]
        probs.append(jnp.concatenate(ps, axis=1))
    for (s, kvh), p in zip(units, probs):
        rows, keys = slice(s * c, (s + 1) * c), slice(s * c, (s + 2) * c)
        vcat = jnp.concatenate([vs[kvh][0][keys], vs[kvh][1][keys]], axis=0)
        o = jnp.dot(p, vcat, preferred_element_type=F32)
        o_ref[rows, 2 * kvh * LANES:(2 * kvh + 1) * LANES] = o[:c].astype(BF16)
        o_ref[rows, (2 * kvh + 1) * LANES:(2 * kvh + 2) * LANES] = o[c:].astype(BF16)


def _swa_prompt(sq, skv, sink, *, nb, t):
    rows = SWA_STEP_BLOCKS * WINDOW
    nsteps = t // rows
    nblk = t // WINDOW
    cur = lambda b, n: (b * nsteps + n, 0)
    curc = lambda col: (lambda b, n: (b * nsteps + n, col))
    prevc = lambda col: (lambda b, n: (b * nblk + jnp.maximum(n * SWA_STEP_BLOCKS - 1, 0), col))
    return pl.pallas_call(
        _swa_kernel,
        out_shape=jax.ShapeDtypeStruct((nb * t, _SWA_QW), BF16),
        grid=(nb, nsteps),
        in_specs=[pl.BlockSpec(memory_space=pltpu.SMEM),
                  pl.BlockSpec((rows, _SWA_QW), cur),
                  pl.BlockSpec((rows, _SWA_KW), curc(0)),
                  pl.BlockSpec((WINDOW, _SWA_KW), prevc(0)),
                  pl.BlockSpec((rows, _SWA_KW), curc(1)),
                  pl.BlockSpec((WINDOW, _SWA_KW), prevc(1))],
        out_specs=pl.BlockSpec((rows, _SWA_QW), cur),
        compiler_params=_cparams("arbitrary", "arbitrary"),
    )(sink, sq, skv, skv, skv, skv)


def _swa_sample_kernel(sink_ref, q_ref, kn_ref, vn_ref, kc_ref, vc_ref, o_ref, ko_ref, vo_ref, *, sb, w):
    pad = jnp.zeros((LANES - sb, _SWA_KW), F32)
    knt = jnp.concatenate([kn_ref[...], pad], axis=0).T
    vnt = jnp.concatenate([vn_ref[...], pad], axis=0).T
    kall = jnp.concatenate([kc_ref[b] for b in range(sb)] + [knt], axis=1)
    vall = jnp.concatenate([vc_ref[b] for b in range(sb)] + [vnt], axis=1)
    ncol = sb * w + LANES
    lo = lax.broadcasted_iota(I32, (sb, LANES), 1) < SWA_HD
    group = SWA_HEADS // SWA_KV_HEADS
    pieces = []
    for h in range(SWA_HEADS):
        slab = q_ref[:, (h // 2) * LANES:(h // 2 + 1) * LANES].astype(F32)
        mine = jnp.where(lo, slab, 0.0) if h % 2 == 0 else jnp.where(lo, 0.0, slab)
        pieces.append(mine if (h % 2) == (h // group) else pltpu.roll(mine, SWA_HD, 1))
    qrows = jnp.concatenate(pieces, axis=0)
    nrow = SWA_HEADS * sb
    s = _bdot(qrows, kall)
    rb = lax.broadcasted_iota(I32, (nrow, ncol), 0) % sb
    ci = lax.broadcasted_iota(I32, (nrow, ncol), 1)
    in_cache = (ci < sb * w) & (ci // w == rb) & ((w - ci % w) < WINDOW)
    mask = in_cache | (ci == sb * w + rb)
    sink_col = jnp.concatenate([jnp.full((sb, 1), sink_ref[h], F32) for h in range(SWA_HEADS)], axis=0)
    p = _sink_softmax(s, mask, sink_col)
    o = _bdot_nt(p, vall)
    for j in range(SWA_HEADS // 2):
        acc = jnp.zeros((sb, LANES), F32)
        for half in range(2):
            h = 2 * j + half
            oh = o[h * sb:(h + 1) * sb]
            own = jnp.where(lo, oh, 0.0) if h // group == 0 else jnp.where(lo, 0.0, oh)
            acc = acc + (own if (h // group) == half else pltpu.roll(own, SWA_HD, 1))
        o_ref[:, j * LANES:(j + 1) * LANES] = acc.astype(BF16)
    newest = lax.broadcasted_iota(I32, (_SWA_KW, w), 1) == w - 1
    for b in range(sb):
        ko_ref[b] = jnp.where(newest, knt[:, b:b + 1], pltpu.roll(kc_ref[b], w - 1, 1))
        vo_ref[b] = jnp.where(newest, vnt[:, b:b + 1], pltpu.roll(vc_ref[b], w - 1, 1))


def _swa_sample(sq, skv, cache_kt, cache_vt, sink, *, row0, ns):
    sb = min(16, ns)
    w = cache_kt.shape[2]
    base = row0 // sb
    blk = lambda i: (base + i, 0)
    cblk = lambda i: (i, 0, 0)
    cspec = pl.BlockSpec((sb, _SWA_KW, w), cblk)
    return pl.pallas_call(
        functools.partial(_swa_sample_kernel, sb=sb, w=w),
        out_shape=[jax.ShapeDtypeStruct((ns, _SWA_QW), BF16),
                   jax.ShapeDtypeStruct(cache_kt.shape, F32), jax.ShapeDtypeStruct(cache_vt.shape, F32)],
        grid=(ns // sb,),
        in_specs=[pl.BlockSpec(memory_space=pltpu.SMEM),
                  pl.BlockSpec((sb, _SWA_QW), blk),
                  pl.BlockSpec((sb, _SWA_KW), lambda i: (base + i, 0)),
                  pl.BlockSpec((sb, _SWA_KW), lambda i: (base + i, 1)),
                  cspec, cspec],
        out_specs=[pl.BlockSpec((sb, _SWA_QW), lambda i: (i, 0)), cspec, cspec],
        compiler_params=_cparams("arbitrary"),
    )(sink, sq, skv, skv, cache_kt, cache_vt)


def _route(logits):
    lane = lax.broadcasted_iota(I32, logits.shape, 1).astype(F32)
    big = float(1 << 20)
    neg = -jnp.inf

    def top(mask):
        v = jnp.max(jnp.where(mask, logits, neg), axis=-1, keepdims=True)
        i = jnp.min(jnp.where(mask & (logits == v), lane, big), axis=-1, keepdims=True)
        return v, i

    gmask = lane < N_GROUPS
    gmax, gsel = top(gmask)
    p_group = 1.0 / jnp.sum(jnp.where(gmask, jnp.exp(logits - gmax), 0.0), axis=-1, keepdims=True)
    first = N_GROUPS + gsel * EXPERTS_PER_GROUP
    emask = (lane >= first) & (lane < first + EXPERTS_PER_GROUP)
    v1, i1 = top(emask)
    v2, i2 = top(emask & (lane != i1))
    t = jnp.exp(v2 - v1)
    w1 = p_group / (1.0 + t)
    return (i1 - N_GROUPS).astype(I32), (i2 - N_GROUPS).astype(I32), w1, w1 * t


def _plan_tile(e1, e2, valid, carry):
    lane = lax.broadcasted_iota(I32, (TM, LANES), 1)
    oh1 = ((lane == e1) & valid).astype(F32)
    oh2 = ((lane == e2) & valid).astype(F32)
    oh = oh1 + oh2
    tri = (lax.broadcasted_iota(I32, (TM, TM), 0) > lax.broadcasted_iota(I32, (TM, TM), 1)).astype(BF16)
    before = _bdot(tri, oh)
    cnt = jnp.sum(oh, axis=0, keepdims=True)
    units = jnp.maximum(jnp.floor((cnt + (RUN - 1)) * (1.0 / RUN)), 1.0)
    upper = (lax.broadcasted_iota(I32, (LANES, LANES), 0) < lax.broadcasted_iota(I32, (LANES, LANES), 1))
    lstart = RUN * _bdot(jnp.broadcast_to(units, (SUBLANES, LANES)), upper.astype(BF16))[0:1]
    slot = lstart + before
    lp1 = jnp.sum(oh1 * slot, axis=-1, keepdims=True)
    lp2 = jnp.sum(oh2 * slot, axis=-1, keepdims=True)
    vcol = valid[:, 0:1]
    lp = jnp.where(lane == 0, jnp.where(vcol, lp1, -1.0), jnp.where(lane == 1, jnp.where(vcol, lp2, -1.0), 0.0))
    base = carry[...]
    carry[...] = base + RUN * units
    return lp, (RUN * units).astype(I32), lstart.astype(I32), base.astype(I32)


def _outproj_step(gated_of, oswa_of, x_of, g1_of, sh_of, sc_of, n_valid, sig_ref, n2_ref, wur_ref, wus_ref,
                  wo_ref, wrc_ref, br_ref, x1_o, h2_o, rt_o, lp_o, cnt_o, ls_o, gb_o, carry):
    subs = [slice(r, r + TM) for r in range(0, TT, TM)]
    merged = []
    for rows in subs:
        y_ret = jnp.dot(gated_of(rows), wur_ref[...], preferred_element_type=F32)
        y_swa = jnp.dot(oswa_of(rows), wus_ref[...], preferred_element_type=F32)
        d = y_ret.shape[1]
        merged.append((sig_ref[rows, :d].astype(F32) * y_ret + sig_ref[rows, d:].astype(F32) * y_swa).astype(BF16))
    hs = []
    for rows, m in zip(subs, merged):
        x1 = x_of(rows) + g1_of(rows) * jnp.dot(m, wo_ref[...], preferred_element_type=F32)
        x1_o[rows] = x1
        h2 = _rms(x1, n2_ref[...]) * (1.0 + sc_of(rows)) + sh_of(rows)
        hi = h2.astype(BF16)
        h2_o[rows] = hi
        hs.append((hi, (h2 - hi.astype(F32)).astype(BF16)))
    routed = []
    for rows, (hi, lo) in zip(subs, hs):
        both = jnp.dot(hi, wrc_ref[...], preferred_element_type=F32)
        logits = (both[:, :LANES] + both[:, LANES:]
                  + jnp.dot(lo, wrc_ref[:, :LANES], preferred_element_type=F32) + br_ref[...])
        e1, e2, w1, w2 = _route(logits)
        lane = lax.broadcasted_iota(I32, logits.shape, 1)
        rt_o[rows] = jnp.where(lane == 2, w1, jnp.where(lane == 3, w2, 0.0))
        routed.append((e1, e2))
    for sub, (e1, e2) in enumerate(routed):
        row = lax.broadcasted_iota(I32, (TM, LANES), 0) + sub * TM
        valid = (row >= 0) if n_valid is None else (row < n_valid)
        lp_o[sub * TM:(sub + 1) * TM], cnt_o[sub], ls_o[sub], gb_o[sub] = _plan_tile(e1, e2, valid, carry)


def _outproj_kernel(gtp_ref, gts_ref, osp_ref, oss_ref, sig_ref, xp_ref, xs_ref, g1p_ref, shp_ref, scp_ref,
                    g1s_ref, shs_ref, scs_ref, *rest, nps, ns):
    i = pl.program_id(0)
    is_s = i >= nps
    carry = rest[-1]

    @pl.when(i == 0)
    def _():
        carry[...] = jnp.zeros_like(carry)

    @pl.when(jnp.logical_not(is_s))
    def _():
        _outproj_step(lambda rows: gtp_ref[rows], lambda rows: osp_ref[rows], lambda rows: xp_ref[rows],
                      lambda rows: g1p_ref[...], lambda rows: shp_ref[...], lambda rows: scp_ref[...], None,
                      sig_ref, *rest)

    @pl.when(is_s)
    def _():
        _outproj_step(lambda rows: gts_ref[rows], lambda rows: oss_ref[rows], lambda rows: xs_ref[rows],
                      lambda rows: g1s_ref[rows], lambda rows: shs_ref[rows], lambda rows: scs_ref[rows], ns,
                      sig_ref, *rest)


def _outproj(gated_p, gated_s, oswa_p, oswa_s, sig2, xp, xs_pad, modp, mods, n2, wur, wus, wo, wr, br,
             *, nps, spb, nb, ns):
    d = xp.shape[1]
    sub = TT // TM
    nrow = (nps + 1) * TT
    row = lambda i: (i, 0)
    pstep = lambda i: (jnp.minimum(i, nps - 1), 0)
    pbatch = lambda col: (lambda i: (jnp.minimum(i // spb, nb - 1), 0, col))
    scol = lambda col: (lambda i: (0, col))
    const = lambda i: (0, 0)
    wr_hi = wr.astype(BF16)
    wrc = jnp.concatenate([wr_hi, (wr - wr_hi.astype(F32)).astype(BF16)], axis=1)
    meta = jax.ShapeDtypeStruct(((nps + 1) * sub, 1, LANES), I32)
    mspec = pl.BlockSpec((sub, 1, LANES), lambda i: (i, 0, 0))
    return pl.pallas_call(
        functools.partial(_outproj_kernel, nps=nps, ns=ns),
        out_shape=[jax.ShapeDtypeStruct((nrow, d), F32),
                   jax.ShapeDtypeStruct((nrow, d), BF16),
                   jax.ShapeDtypeStruct((nrow, LANES), F32),
                   jax.ShapeDtypeStruct((nrow, LANES), F32), meta, meta, meta],
        grid=(nps + 1,),
        in_specs=[pl.BlockSpec((TT, _RET_W), pstep), pl.BlockSpec((TT, _RET_W), const),
                  pl.BlockSpec((TT, _SWA_QW), pstep), pl.BlockSpec((TT, _SWA_QW), const),
                  pl.BlockSpec((TT, 2 * d), row),
                  pl.BlockSpec((TT, d), pstep), pl.BlockSpec((TT, d), const),
                  pl.BlockSpec((None, 1, d), pbatch(2)), pl.BlockSpec((None, 1, d), pbatch(3)),
                  pl.BlockSpec((None, 1, d), pbatch(4)),
                  pl.BlockSpec((TT, d), scol(2)), pl.BlockSpec((TT, d), scol(3)), pl.BlockSpec((TT, d), scol(4)),
                  pl.BlockSpec((1, d), const),
                  pl.BlockSpec(wur.shape, const), pl.BlockSpec(wus.shape, const), pl.BlockSpec(wo.shape, const),
                  pl.BlockSpec(wrc.shape, const), pl.BlockSpec((1, LANES), const)],
        out_specs=[pl.BlockSpec((TT, d), row), pl.BlockSpec((TT, d), row), pl.BlockSpec((TT, LANES), row),
                   pl.BlockSpec((TT, LANES), row), mspec, mspec, mspec],
        scratch_shapes=[pltpu.VMEM((1, LANES), F32)],
        compiler_params=_cparams("arbitrary"),
    )(gated_p, gated_s, oswa_p, oswa_s, sig2, xp, xs_pad, modp, modp, modp, mods, mods, mods, n2,
      wur, wus, wo, wrc, br)


def _units(rows):
    return rows.reshape(rows.shape[0] // RUN, RUN, rows.shape[1])


def _run_copy(src, dst, s_start, d_start, n, sem):
    return pltpu.make_async_copy(src.at[pl.ds(s_start, n)], dst.at[pl.ds(d_start, n)], sem)


def _tile_rows(cnt_ref, ls_ref, step):
    last = step * N_EXPERTS + N_EXPERTS - 1
    return ls_ref[last] + cnt_ref[last]


def _each_run(step, fn):
    for e in range(N_EXPERTS):
        fn(step * N_EXPERTS + e)


def _dispatch_kernel(cnt_ref, ls_ref, gd_ref, ps_ref, pn_ref, nu_ref, h_ref, lp_ref, xs_ref,
                     sorted_scr, zero_scr, sem, zsem, *, nt, maxt):
    i = pl.program_id(0)

    def pad(e):
        return _run_copy(zero_scr, xs_ref, 0, ps_ref[e], pn_ref[e], zsem)

    def tail(j):
        return _run_copy(zero_scr, xs_ref, 0, j * (TE // RUN), TE // RUN, zsem)

    def each_pad(fn):
        def body(e, c):
            @pl.when(pn_ref[e] > 0)
            def _():
                fn(pad(e))
            return c
        lax.fori_loop(0, N_EXPERTS, body, 0)

    def each_tail(fn):
        def body(j, c):
            fn(tail(j))
            return c
        lax.fori_loop(nu_ref[0], maxt, body, 0)

    @pl.when(i == 0)
    def _():
        zero_scr[...] = jnp.zeros_like(zero_scr)
        each_pad(lambda cp: cp.start())
        each_tail(lambda cp: cp.start())

    lpt = lp_ref[...].T
    slot = lax.broadcasted_iota(I32, (SLOTS, TM), 0).astype(F32)
    perm = ((slot == lpt[0:1]) | (slot == lpt[1:2])).astype(BF16)
    sorted_scr[i % 2] = _units(jnp.dot(perm, h_ref[...], preferred_element_type=F32).astype(BF16))

    def copy(step):
        return lambda k: _run_copy(sorted_scr.at[step % 2], xs_ref, ls_ref[k], gd_ref[k], cnt_ref[k], sem.at[step % 2])

    def wait_all(step):
        _run_copy(sorted_scr.at[step % 2], xs_ref, 0, 0, _tile_rows(cnt_ref, ls_ref, step), sem.at[step % 2]).wait()

    _each_run(i, lambda k: copy(i)(k).start())

    @pl.when(i > 0)
    def _():
        wait_all(i - 1)

    @pl.when(i == nt - 1)
    def _():
        wait_all(i)
        each_pad(lambda cp: cp.wait())
        each_tail(lambda cp: cp.wait())


def _dispatch(cnt, ls, gd, ps, pn, nu, h2, lp, *, nt, maxt):
    d = h2.shape[1]
    return pl.pallas_call(
        functools.partial(_dispatch_kernel, nt=nt, maxt=maxt),
        out_shape=jax.ShapeDtypeStruct((maxt * TE // RUN, RUN, d), BF16),
        grid_spec=pltpu.PrefetchScalarGridSpec(
            num_scalar_prefetch=6,
            grid=(nt,),
            in_specs=[pl.BlockSpec((TM, d), lambda i, *_: (i, 0)),
                      pl.BlockSpec((TM, LANES), lambda i, *_: (i, 0))],
            out_specs=pl.BlockSpec(memory_space=pl.ANY),
            scratch_shapes=[pltpu.VMEM((2, SLOTS // RUN, RUN, d), BF16), pltpu.VMEM((TE // RUN, RUN, d), BF16),
                            pltpu.SemaphoreType.DMA((2,)), pltpu.SemaphoreType.DMA(())]),
        compiler_params=_cparams("arbitrary"),
    )(cnt, ls, gd, ps, pn, nu, h2, lp)


def _experts_kernel(te_ref, nu_ref, start_ref, nxt_ref, par_ref, x_ref, w1_hbm, w3_hbm, w2_hbm, y_ref,
                    w1f, w3f, w2f, w1b, w3b, w2b, sem):
    j = pl.program_id(0)
    e = te_ref[j]

    def fetch(ex, slot):
        return [pltpu.make_async_copy(src.at[ex], dst.at[slot], sem.at[slot, n])
                for n, (src, dst) in enumerate(((w1_hbm, w1f), (w3_hbm, w3f), (w2_hbm, w2f)))]

    @pl.when(j == 0)
    def _():
        for cp in fetch(e, par_ref[e]):
            cp.start()

    @pl.when((j == start_ref[e]) & (j < nu_ref[0]))
    def _():
        slot = par_ref[e]

        @pl.when(nxt_ref[e] >= 0)
        def _():
            for cp in fetch(nxt_ref[e], 1 - slot):
                cp.start()

        for cp in fetch(e, slot):
            cp.wait()
        w1b[...] = w1f[slot].astype(BF16)
        w3b[...] = w3f[slot].astype(BF16)
        w2b[...] = w2f[slot].astype(BF16)

    @pl.when(j < nu_ref[0])
    def _():
        subs = [slice(r // RUN, (r + TE_SUB) // RUN) for r in range(0, TE, TE_SUB)]
        xs = [x_ref[u].reshape(TE_SUB, x_ref.shape[-1]) for u in subs]
        ab = [(jnp.dot(x, w1b[...], preferred_element_type=F32),
               jnp.dot(x, w3b[...], preferred_element_type=F32)) for x in xs]
        hid = [(_silu(a) * b).astype(BF16) for a, b in ab]
        for u, h in zip(subs, hid):
            y_ref[u] = _units(jnp.dot(h, w2b[...], preferred_element_type=F32).astype(BF16))


def _experts(te, nu, start, nxt, par, xs, w1, w3, w2, *, maxt):
    d = xs.shape[2]
    f = w1.shape[2]
    used = lambda j, te, nu, *_: (jnp.minimum(j, nu[0] - 1), 0, 0)
    hbm = pl.BlockSpec(memory_space=pl.ANY)
    return pl.pallas_call(
        _experts_kernel,
        out_shape=jax.ShapeDtypeStruct(xs.shape, BF16),
        grid_spec=pltpu.PrefetchScalarGridSpec(
            num_scalar_prefetch=5,
            grid=(maxt,),
            in_specs=[pl.BlockSpec((TE // RUN, RUN, d), used), hbm, hbm, hbm],
            out_specs=pl.BlockSpec((TE // RUN, RUN, d), used),
            scratch_shapes=[pltpu.VMEM((2, d, f), F32), pltpu.VMEM((2, d, f), F32), pltpu.VMEM((2, f, d), F32),
                            pltpu.VMEM((d, f), BF16), pltpu.VMEM((d, f), BF16), pltpu.VMEM((f, d), BF16),
                            pltpu.SemaphoreType.DMA((2, 3))]),
        input_output_aliases={5: 0},
        compiler_params=_cparams("arbitrary"),
    )(te, nu, start, nxt, par, xs, w1, w3, w2)


def _combine_kernel(cnt_ref, ls_ref, gd_ref, ys_ref, lp_ref, rt_ref, x1_ref, g2p_ref, g2s_ref, fg_ref,
                    yp_o, ys_o, ybuf, acc, sem, *, npt):
    i = pl.program_id(0)
    nt = npt + 1

    def fetch(step):
        ybuf[step % 2] = jnp.zeros(ybuf.shape[1:], BF16)
        _each_run(step, lambda k: _run_copy(ys_ref, ybuf.at[step % 2], gd_ref[k], ls_ref[k], cnt_ref[k],
                                            sem.at[step % 2]).start())

    @pl.when(i == 0)
    def _():
        fetch(0)

    @pl.when(i + 1 < nt)
    def _():
        fetch(i + 1)

    _run_copy(ys_ref, ybuf.at[i % 2], 0, 0, _tile_rows(cnt_ref, ls_ref, i), sem.at[i % 2]).wait()

    lp = lp_ref[...]
    rt = rt_ref[...]

    def moe(u0, u1):
        yb = ybuf[i % 2, u0:u1].reshape((u1 - u0) * RUN, x1_ref.shape[1])
        slot = (lax.broadcasted_iota(I32, (TM, (u1 - u0) * RUN), 1) + u0 * RUN).astype(F32)
        unsort = lambda col: jnp.dot((slot == lp[:, col:col + 1]).astype(BF16), yb, preferred_element_type=F32)
        return rt[:, 2:3] * unsort(0) + rt[:, 3:4] * unsort(1)

    head = (SLOTS - TM) // RUN
    acc[...] = moe(0, head)

    @pl.when(_tile_rows(cnt_ref, ls_ref, i) > head)
    def _():
        acc[...] += moe(head, SLOTS // RUN)

    g2 = jnp.where(i >= npt, g2s_ref[...], g2p_ref[...])
    y = _rms(x1_ref[...] + g2 * acc[...], fg_ref[...])

    @pl.when(i < npt)
    def _():
        yp_o[...] = y

    @pl.when(i >= npt)
    def _():
        ys_o[...] = y


def _combine(cnt, ls, gd, ys, lp, rt, x1, modp, mods, fg, *, npt, tpb, nb):
    d = x1.shape[1]
    nt = npt + 1
    row = lambda i, *_: (i, 0)
    return pl.pallas_call(
        functools.partial(_combine_kernel, npt=npt),
        out_shape=[jax.ShapeDtypeStruct((npt * TM, d), F32), jax.ShapeDtypeStruct((TM, d), F32)],
        grid_spec=pltpu.PrefetchScalarGridSpec(
            num_scalar_prefetch=3,
            grid=(nt,),
            in_specs=[pl.BlockSpec(memory_space=pl.ANY),
                      pl.BlockSpec((TM, LANES), row), pl.BlockSpec((TM, LANES), row), pl.BlockSpec((TM, d), row),
                      pl.BlockSpec((None, 1, d), lambda i, *_: (jnp.minimum(i // tpb, nb - 1), 0, 5)),
                      pl.BlockSpec((TM, d), lambda i, *_: (0, 5)),
                      pl.BlockSpec((1, d), lambda i, *_: (0, 0))],
            out_specs=[pl.BlockSpec((TM, d), lambda i, *_: (jnp.minimum(i, npt - 1), 0)),
                       pl.BlockSpec((TM, d), lambda i, *_: (0, 0))],
            scratch_shapes=[pltpu.VMEM((2, SLOTS // RUN, RUN, d), BF16), pltpu.VMEM((TM, d), F32),
                            pltpu.SemaphoreType.DMA((2,))]),
        compiler_params=_cparams("arbitrary"),
    )(cnt, ls, gd, ys, lp, rt, x1, modp, mods, fg)


def _rotation_tables(t):
    inv_r = jnp.repeat(1.0 / (ROPE_THETA ** jnp.linspace(0.0, 1.0, RET_DK // 2, dtype=F32)), 2)
    sign_r = jnp.where(jnp.arange(RET_DK) % 2 == 0, -1.0, 1.0).astype(F32)
    inv_w = jnp.tile(ROPE_THETA ** (-jnp.arange(0, SWA_HD, 2, dtype=F32) / SWA_HD), LANES // (SWA_HD // 2))
    sign_w = jnp.where(jnp.arange(LANES) % SWA_HD < SWA_HD // 2, -1.0, 1.0).astype(F32)
    hi = (jnp.arange(t // TAB_LO, dtype=I32) * TAB_LO).astype(F32)[:, None]
    lo = jnp.arange(TAB_LO, dtype=I32).astype(F32)[:, None]
    past = jnp.full((1, 1), PAST_LEN, F32)
    his, los, sample = [], [], []
    for inv, sign in ((inv_r[None, :], sign_r[None, :]), (inv_w[None, :], sign_w[None, :])):
        a, b, p = hi * inv, lo * inv, past * inv
        his += [jnp.cos(a), jnp.sin(a)]
        los += [jnp.cos(b), jnp.sin(b), jnp.cos(b) * sign, jnp.sin(b) * sign]
        sample += [jnp.cos(p), jnp.sin(p) * sign]
    return tuple(jnp.concatenate(parts, axis=1) for parts in (his, los, sample))


def kernel(x_prompt, x_sample, c_prompt, c_sample, state_ret, cache_swa_k, cache_swa_v, w_ada, b_ada, norm1_g, norm2_g, w_in, w_up_ret, w_up_swa, w_o, sink, w_rg, b_rg, w_re, b_re, w1, w3, w2, final_g):
    nb, t, d = x_prompt.shape
    ns, dec_seq, _ = x_sample.shape
    depth = w_ada.shape[0]
    assert depth == 1 and dec_seq == 1, "single layer, one new token per sequence"
    assert t % TT == 0 and ns <= TM and ns % 16 == 0 and d % LANES == 0
    assert t % RET_BLOCK == 0 and t % (SWA_STEP_BLOCKS * WINDOW) == 0 and TT % (SUBLANES * TAB_LO) == 0
    assert N_GROUPS + N_EXPERTS <= LANES
    w = cache_swa_k.shape[2]
    tpb = t // TM
    npt = nb * tpb
    spb = t // TT
    nps = nb * spb
    nt = (nps + 1) * (TT // TM)
    np_rows = nb * t
    n_tok = np_rows + ns
    maxt = -(-(2 * n_tok + nt * N_EXPERTS * RUN + N_EXPERTS * (TE - 1)) // TE)

    xp = x_prompt.reshape(np_rows, d)
    xs_pad = jnp.pad(x_sample.reshape(ns, d), ((0, TT - ns), (0, 0)))

    c_all = jnp.concatenate([jnp.pad(c_sample, ((0, TT - ns), (0, 0))),
                             jnp.pad(c_prompt, ((0, SUBLANES - nb % SUBLANES), (0, 0)))])
    mods = _modulation(c_all, w_ada[0], b_ada[0])
    modp = mods[TT:TT + nb].reshape(nb, 1, 6 * d)

    ret4, sq, skv, sig2 = _inproj(
        xp, xs_pad, modp, mods, norm1_g, w_in[0].astype(BF16), *_rotation_tables(t), nps=nps, spb=spb, nb=nb)

    gamma = jnp.asarray([1.0 - 2.0 ** (-5.0 - h) for h in range(RET_HEADS)], F32)
    gated_p, st_p = _retention_prompt(ret4, nb=nb, t=t)
    gated_s, st_s = _retention_sample(ret4, state_ret[0], gamma, row0=np_rows, ns=ns)
    oswa_p = _swa_prompt(sq, skv, sink[0], nb=nb, t=t)
    to_t = lambda c: jnp.transpose(c[0], (0, 2, 3, 1)).reshape(ns, _SWA_KW, w)
    from_t = lambda c: jnp.transpose(c.reshape(ns, SWA_KV_HEADS, SWA_HD, w), (0, 3, 1, 2))[None]
    oswa_s, ks_new, vs_new = _swa_sample(sq, skv, to_t(cache_swa_k), to_t(cache_swa_v), sink[0],
                                         row0=np_rows, ns=ns)
    gated_s = jnp.pad(gated_s, ((0, TT - ns), (0, 0)))
    oswa_s = jnp.pad(oswa_s, ((0, TT - ns), (0, 0)))

    wr = jnp.pad(jnp.concatenate([w_rg[0], w_re[0]], axis=1), ((0, 0), (0, LANES - N_GROUPS - N_EXPERTS)))
    br = jnp.pad(jnp.concatenate([b_rg[0], b_re[0]]), (0, LANES - N_GROUPS - N_EXPERTS)).reshape(1, LANES)
    x1, h2, rt, lp, cnt, ls, gb = _outproj(
        gated_p, gated_s, oswa_p, oswa_s, sig2, xp, xs_pad, modp, mods, norm2_g,
        w_up_ret[0].astype(BF16), w_up_swa[0].astype(BF16), w_o[0].astype(BF16), wr, br,
        nps=nps, spb=spb, nb=nb, ns=ns)
    cnt = cnt[:, 0, :N_EXPERTS]
    ls = ls[:, 0, :N_EXPERTS]
    gb = gb[:, 0, :N_EXPERTS]
    seg = jnp.sum(cnt, axis=0)
    tiles = (seg + TE - 1) // TE
    tile_end = jnp.cumsum(tiles)
    row_start = (tile_end - tiles) * TE
    gd = (gb + row_start[None, :]).reshape(-1)
    n_used = tile_end[-1:]
    jj = jnp.minimum(jnp.arange(maxt, dtype=I32), n_used[0] - 1)
    te = jnp.minimum(jnp.sum((tile_end[None, :] <= jj[:, None]).astype(I32), axis=1), N_EXPERTS - 1)
    cnt, ls, gd = cnt.reshape(-1) // RUN, ls.reshape(-1) // RUN, gd // RUN
    n_used = n_used.astype(I32)
    xs = _dispatch(cnt, ls, gd, (row_start + seg) // RUN, (tiles * TE - seg) // RUN, n_used, h2, lp,
                   nt=nt, maxt=maxt)
    has = tiles > 0
    eidx = jnp.arange(N_EXPERTS, dtype=I32)
    later = jnp.where(has[None, :] & (eidx[None, :] > eidx[:, None]), eidx[None, :], N_EXPERTS)
    nxt_e = jnp.min(later, axis=1)
    nxt_e = jnp.where(nxt_e < N_EXPERTS, nxt_e, -1).astype(I32)
    par_e = ((jnp.cumsum(has.astype(I32)) - 1) % 2).astype(I32)
    ys = _experts(te, n_used, (tile_end - tiles).astype(I32), nxt_e, par_e, xs, w1[0], w3[0], w2[0], maxt=maxt)
    y_p, y_s = _combine(cnt, ls, gd, ys, lp, rt, x1, modp, mods, final_g.reshape(1, d), npt=npt, tpb=tpb, nb=nb)

    y_prompt = y_p.reshape(nb, t, d)
    y_sample = y_s[:ns].reshape(ns, 1, d)
    wk = min(WINDOW, t)
    tails = jnp.stack([skv[(b + 1) * t - wk:(b + 1) * t] for b in range(nb)])
    skp, svp = (tails[:, :, part * _SWA_KW:(part + 1) * _SWA_KW].reshape(nb, wk, SWA_KV_HEADS, SWA_HD)
                for part in range(2))
    return (y_prompt, y_sample, st_p[None], st_s[None], skp[None], svp[None], from_t(ks_new), from_t(vs_new))
```

```python
import functools
import math

import jax
import jax.numpy as jnp
from jax import lax
from jax.experimental import pallas as pl
from jax.experimental.pallas import tpu as pltpu

F32 = jnp.float32
BF16 = jnp.bfloat16
I32 = jnp.int32

PAST_LEN = 8192
RET_HEADS = 4
RET_DK = 128
RET_DV = 128
RET_CHUNK = 128
SWA_HEADS = 8
SWA_KV_HEADS = 2
SWA_HD = 64
WINDOW = 128
ROPE_THETA = 10000.0
N_GROUPS = 4
EXPERTS_PER_GROUP = 8
N_EXPERTS = N_GROUPS * EXPERTS_PER_GROUP
D_EXPERT = 256
NORM_EPS = 1e-6

LANES = 128
SUBLANES = 8
TM = 256
TT = 2 * TM
RUN = 16
SLOTS = 2 * TM + N_EXPERTS * RUN
TE = 512
TE_SUB = 256
VMEM_LIMIT = 56 * 1024 * 1024
TAB_LO = 64

_RET_W = RET_HEADS * RET_DK
_SWA_QW = SWA_HEADS * SWA_HD
_SWA_KW = SWA_KV_HEADS * SWA_HD


def _cparams(*sem):
    return pltpu.CompilerParams(dimension_semantics=sem, vmem_limit_bytes=VMEM_LIMIT)


def _sigmoid(x):
    return 1.0 / (1.0 + jnp.exp(-x))


def _silu(x):
    return x * _sigmoid(x)


def _bdot(a, b):
    return jnp.dot(a.astype(BF16), b.astype(BF16), preferred_element_type=F32)


def _bdot_nt(a, b):
    return lax.dot_general(a.astype(BF16), b.astype(BF16), (((1,), (1,)), ((), ())), preferred_element_type=F32)


def _mod_kernel(c_ref, w_ref, b_ref, o_ref):
    o_ref[...] = _bdot(_silu(c_ref[...]), w_ref[...]) + b_ref[...]


def _modulation(c_all, w_ada, b_ada):
    rows, d = c_all.shape
    n = w_ada.shape[1]
    return pl.pallas_call(
        _mod_kernel,
        out_shape=jax.ShapeDtypeStruct((rows, n), F32),
        grid=(n // d,),
        in_specs=[pl.BlockSpec((rows, d), lambda j: (0, 0)),
                  pl.BlockSpec((d, d), lambda j: (0, j)),
                  pl.BlockSpec((1, d), lambda j: (0, j))],
        out_specs=pl.BlockSpec((rows, d), lambda j: (0, j)),
        compiler_params=_cparams("arbitrary"),
    )(c_all, w_ada, b_ada.reshape(1, n))


def _rms(x, g):
    return x * lax.rsqrt(jnp.mean(x * x, axis=-1, keepdims=True) + NORM_EPS) * g


def _pair_rotate(z, cos, sin_signed):
    n = z.shape[-1]
    lane = lax.broadcasted_iota(I32, z.shape, 1)
    partner = jnp.where((lane & 1) == 0, pltpu.roll(z, n - 1, 1), pltpu.roll(z, 1, 1))
    reps = n // LANES
    cos = jnp.concatenate([cos] * reps, axis=1) if reps > 1 else cos
    sin_signed = jnp.concatenate([sin_signed] * reps, axis=1) if reps > 1 else sin_signed
    return z * cos + partner * sin_signed


def _half_rotate(z, cos, sin_signed):
    n = z.shape[-1]
    half = SWA_HD // 2
    lane = lax.broadcasted_iota(I32, z.shape, 1)
    partner = jnp.where((lane & (SWA_HD - 1)) < half, pltpu.roll(z, n - half, 1), pltpu.roll(z, half, 1))
    reps = n // LANES
    cos = jnp.concatenate([cos] * reps, axis=1) if reps > 1 else cos
    sin_signed = jnp.concatenate([sin_signed] * reps, axis=1) if reps > 1 else sin_signed
    return z * cos + partner * sin_signed


def _inproj_step(x_of, sh_of, sc_of, tabs_of, n1_ref, w_ref, outs):
    ret_o, sq_o, skv_o, sig_o = outs
    d = w_ref.shape[0]
    subs = []
    for r in range(0, TT, TM):
        rows = slice(r, r + TM)
        h = (_rms(x_of(rows), n1_ref[...]) * (1.0 + sc_of(rows)) + sh_of(rows)).astype(BF16)
        subs.append((rows, h, tabs_of(rows)))

    def seg(h, a, b):
        return jnp.dot(h, w_ref[:, a:b], preferred_element_type=F32)

    o = 0
    for rows, h, (cr, sr, cw, sw) in subs:
        ret_o[rows, 0:_RET_W] = _pair_rotate(seg(h, o, o + _RET_W), cr, sr).astype(BF16)
    o += _RET_W
    for rows, h, (cr, sr, cw, sw) in subs:
        ret_o[rows, _RET_W:2 * _RET_W] = (_pair_rotate(seg(h, o, o + _RET_W), cr, sr)
                                          * (RET_DK ** -0.5)).astype(BF16)
    o += _RET_W
    for rows, h, _ in subs:
        ret_o[rows, 2 * _RET_W:3 * _RET_W] = seg(h, o, o + _RET_W).astype(BF16)
    o += _RET_W
    for rows, h, _ in subs:
        ret_o[rows, 3 * _RET_W:4 * _RET_W] = _silu(seg(h, o, o + _RET_W)).astype(BF16)
    o += _RET_W
    for rows, h, (cr, sr, cw, sw) in subs:
        sq_o[rows] = (_half_rotate(seg(h, o, o + _SWA_QW), cw, sw) * (SWA_HD ** -0.5)).astype(BF16)
    o += _SWA_QW
    for rows, h, (cr, sr, cw, sw) in subs:
        zkv = seg(h, o, o + 2 * _SWA_KW)
        skv_o[rows, :_SWA_KW] = _half_rotate(zkv[:, :_SWA_KW], cw, sw)
        skv_o[rows, _SWA_KW:] = zkv[:, _SWA_KW:]
    o += 2 * _SWA_KW
    for rows, h, _ in subs:
        sig_o[rows, :d] = _sigmoid(seg(h, o, o + d)).astype(BF16)
    o += d
    for rows, h, _ in subs:
        sig_o[rows, d:] = _sigmoid(seg(h, o, o + d)).astype(BF16)


def _prompt_tables(th_ref, tl_ref, rows):
    tabs = []
    for rot in range(2):
        cb, sb, cbs, sbs = (tl_ref[:, (4 * rot + q) * LANES:(4 * rot + q + 1) * LANES] for q in range(4))
        cos, sin = [], []
        for g in range(rows.start // TAB_LO, rows.stop // TAB_LO):
            ca = th_ref[g:g + 1, 2 * rot * LANES:(2 * rot + 1) * LANES]
            sa = th_ref[g:g + 1, (2 * rot + 1) * LANES:(2 * rot + 2) * LANES]
            cos.append(ca * cb - sa * sb)
            sin.append(sa * cbs + ca * sbs)
        tabs += [jnp.concatenate(cos, axis=0), jnp.concatenate(sin, axis=0)]
    return tuple(tabs)


def _inproj_kernel(xp_ref, xs_ref, shp_ref, scp_ref, shs_ref, scs_ref, n1_ref, w_ref, th_ref, tl_ref, ts_ref,
                   *outs, nps):
    is_s = pl.program_id(0) >= nps

    @pl.when(jnp.logical_not(is_s))
    def _():
        _inproj_step(lambda rows: xp_ref[rows], lambda rows: shp_ref[...], lambda rows: scp_ref[...],
                     functools.partial(_prompt_tables, th_ref, tl_ref),
                     n1_ref, w_ref, outs)

    @pl.when(is_s)
    def _():
        _inproj_step(lambda rows: xs_ref[rows], lambda rows: shs_ref[rows], lambda rows: scs_ref[rows],
                     lambda rows: tuple(ts_ref[:, q * LANES:(q + 1) * LANES] for q in range(4)),
                     n1_ref, w_ref, outs)


def _inproj(xp, xs_pad, modp, mods, n1, w_in_b, tab_hi, tab_lo, tab_s, *, nps, spb, nb):
    d = xp.shape[1]
    nrow = (nps + 1) * TT
    n_in = w_in_b.shape[1]
    pstep = lambda i: (jnp.minimum(i, nps - 1), 0)
    pbatch = lambda col: (lambda i: (jnp.minimum(i // spb, nb - 1), 0, col))
    tab_idx = lambda i: (jnp.where(i < nps, i % spb, 0), 0)
    out_cols = [(4 * _RET_W, BF16), (_SWA_QW, BF16), (2 * _SWA_KW, F32), (2 * d, BF16)]
    return pl.pallas_call(
        functools.partial(_inproj_kernel, nps=nps),
        out_shape=[jax.ShapeDtypeStruct((nrow, c), t) for c, t in out_cols],
        grid=(nps + 1,),
        in_specs=[pl.BlockSpec((TT, d), pstep),
                  pl.BlockSpec((TT, d), lambda i: (0, 0)),
                  pl.BlockSpec((None, 1, d), pbatch(0)),
                  pl.BlockSpec((None, 1, d), pbatch(1)),
                  pl.BlockSpec((TT, d), lambda i: (0, 0)),
                  pl.BlockSpec((TT, d), lambda i: (0, 1)),
                  pl.BlockSpec((1, d), lambda i: (0, 0)),
                  pl.BlockSpec((d, n_in), lambda i: (0, 0))]
                 + [pl.BlockSpec((TT // TAB_LO, 4 * LANES), tab_idx),
                    pl.BlockSpec(tab_lo.shape, lambda i: (0, 0)), pl.BlockSpec(tab_s.shape, lambda i: (0, 0))],
        out_specs=[pl.BlockSpec((TT, c), lambda i: (i, 0)) for c, _ in out_cols],
        compiler_params=_cparams("arbitrary"),
    )(xp, xs_pad, modp, modp, mods, mods, n1, w_in_b, tab_hi, tab_lo, tab_s)


RET_BLOCK = 512


def _ret_kernel(q_ref, k_ref, v_ref, g_ref, o_ref, st_ref, s_scr, dm_scr, qd_scr, kd_scr, *, nsteps):
    step = pl.program_id(1)
    rows = q_ref.shape[0]
    log_gamma = [math.log(1.0 - 2.0 ** (-5.0 - h)) for h in range(RET_HEADS)]

    @pl.when((pl.program_id(0) == 0) & (step == 0))
    def _():
        diff = (lax.broadcasted_iota(I32, (rows, rows), 0) - lax.broadcasted_iota(I32, (rows, rows), 1)).astype(F32)
        idx = lax.broadcasted_iota(I32, (rows, RET_DV), 0).astype(F32)
        for h, ld in enumerate(log_gamma):
            dm_scr[h] = jnp.where(diff >= 0, jnp.exp(ld * jnp.maximum(diff, 0.0)), 0.0)
            qd_scr[h] = jnp.exp(ld * (idx + 1.0))
            kd_scr[h] = jnp.exp(ld * (rows - 1.0 - idx))

    @pl.when(step == 0)
    def _():
        s_scr[...] = jnp.zeros_like(s_scr)

    for h in range(RET_HEADS):
        sl = slice(h * RET_DK, (h + 1) * RET_DK)
        state = s_scr[h]
        q, k, v = q_ref[:, sl], k_ref[:, sl], v_ref[:, sl]
        att = _bdot_nt(q, k) * dm_scr[h]
        o = _bdot(att, v) + _bdot(q.astype(F32) * qd_scr[h], state)
        kd = (k.astype(F32) * kd_scr[h]).astype(BF16)
        kv = lax.dot_general(kd, v, (((0,), (0,)), ((), ())), preferred_element_type=F32)
        s_scr[h] = math.exp(log_gamma[h] * rows) * state + kv
        o = o * lax.rsqrt(jnp.mean(o * o, axis=-1, keepdims=True) + NORM_EPS)
        o_ref[:, sl] = (o * g_ref[:, sl].astype(F32)).astype(BF16)

    @pl.when(step == nsteps - 1)
    def _():
        st_ref[...] = s_scr[...]


def _retention_prompt(ret4, *, nb, t):
    rows = RET_BLOCK
    nsteps = t // rows
    blk = lambda b, c: (b * nsteps + c, 0)
    return pl.pallas_call(
        functools.partial(_ret_kernel, nsteps=nsteps),
        out_shape=[jax.ShapeDtypeStruct((nb * t, _RET_W), BF16),
                   jax.ShapeDtypeStruct((nb, RET_HEADS, RET_DK, RET_DV), F32)],
        grid=(nb, nsteps),
        in_specs=[pl.BlockSpec((rows, _RET_W), lambda b, c, col=col: (b * nsteps + c, col)) for col in range(4)],
        out_specs=[pl.BlockSpec((rows, _RET_W), blk),
                   pl.BlockSpec((None, RET_HEADS, RET_DK, RET_DV), lambda b, c: (b, 0, 0, 0))],
        scratch_shapes=[pltpu.VMEM((RET_HEADS, RET_DK, RET_DV), F32), pltpu.VMEM((RET_HEADS, rows, rows), F32),
                        pltpu.VMEM((RET_HEADS, rows, RET_DV), F32), pltpu.VMEM((RET_HEADS, rows, RET_DV), F32)],
        compiler_params=_cparams("arbitrary", "arbitrary"),
    )(ret4, ret4, ret4, ret4)


def _ret_sample_kernel(gam_ref, q_ref, k_ref, v_ref, g_ref, s0_ref, o_ref, st_ref, *, sb):
    gamma = gam_ref[pl.program_id(1)]
    q = q_ref[...].astype(F32)
    k = k_ref[...].astype(F32)
    v = v_ref[...].astype(F32)
    rows = sb * RET_DK
    s2 = s0_ref[...].reshape(rows, RET_DV)
    col_b = lax.broadcasted_iota(I32, (sb, rows), 1) // RET_DK
    row_b = lax.broadcasted_iota(I32, (sb, rows), 0)
    qexp = jnp.where(col_b == row_b, jnp.concatenate([q * gamma] * sb, axis=1), 0.0)
    o = jnp.sum(q * k, axis=-1, keepdims=True) * v + _bdot(qexp, s2)
    o = o * lax.rsqrt(jnp.mean(o * o, axis=-1, keepdims=True) + NORM_EPS)
    o_ref[...] = (o * g_ref[...].astype(F32)).astype(BF16)
    rep = (lax.broadcasted_iota(I32, (rows, sb), 0) // RET_DK == lax.broadcasted_iota(I32, (rows, sb), 1))
    rep = rep.astype(BF16)
    krep = _bdot(rep, k)
    vrep = _bdot(rep, v)
    eye = (lax.broadcasted_iota(I32, (rows, RET_DK), 0) % RET_DK == lax.broadcasted_iota(I32, (rows, RET_DK), 1))
    kcol = jnp.sum(jnp.where(eye, krep, 0.0), axis=-1, keepdims=True)
    st_ref[...] = (gamma * s2 + kcol * vrep).reshape(sb, RET_DK, RET_DV)


def _retention_sample(ret4, s0, gamma, *, row0, ns):
    sb = min(64, ns)
    base = row0 // sb
    sblk = lambda i, h: (i, h, 0, 0)
    return pl.pallas_call(
        functools.partial(_ret_sample_kernel, sb=sb),
        out_shape=[jax.ShapeDtypeStruct((ns, _RET_W), BF16),
                   jax.ShapeDtypeStruct(s0.shape, F32)],
        grid=(ns // sb, RET_HEADS),
        in_specs=[pl.BlockSpec(memory_space=pltpu.SMEM)]
                 + [pl.BlockSpec((sb, RET_DK), lambda i, h, col=col: (base + i, col * RET_HEADS + h))
                    for col in range(4)]
                 + [pl.BlockSpec((sb, None, RET_DK, RET_DV), sblk)],
        out_specs=[pl.BlockSpec((sb, RET_DV), lambda i, h: (i, h)),
                   pl.BlockSpec((sb, None, RET_DK, RET_DV), sblk)],
        compiler_params=_cparams("arbitrary", "arbitrary"),
    )(gamma, ret4, ret4, ret4, ret4, s0)


def _sink_softmax(s, mask, sink):
    if mask is not None:
        s = jnp.where(mask, s, -jnp.inf)
    m = jnp.maximum(jnp.max(s, axis=-1, keepdims=True), sink)
    p = jnp.exp(s - m)
    return p / (jnp.sum(p, axis=-1, keepdims=True) + jnp.exp(sink - m))


def _split_kv_heads(x):
    lo = lax.broadcasted_iota(I32, x.shape, 1) < SWA_HD
    h0_lo = jnp.where(lo, x, 0.0)
    h1_hi = jnp.where(lo, 0.0, x)
    return ((h0_lo, pltpu.roll(h0_lo, SWA_HD, 1)), (pltpu.roll(h1_hi, SWA_HD, 1), h1_hi))


SWA_STEP_BLOCKS = 4


def _swa_kernel(sink_ref, q_ref, kc_ref, kp_ref, vc_ref, vp_ref, o_ref):
    n = pl.program_id(1)
    c = WINDOW
    kk = jnp.concatenate([kp_ref[...], kc_ref[...]], axis=0)
    vv = jnp.concatenate([vp_ref[...], vc_ref[...]], axis=0)
    ks = [[a.astype(BF16) for a in pair] for pair in _split_kv_heads(kk)]
    vs = [[a.astype(BF16) for a in pair] for pair in _split_kv_heads(vv)]
    qi = lax.broadcasted_iota(I32, (2 * c, c), 0) % c
    ki = lax.broadcasted_iota(I32, (2 * c, c), 1)
    from_prev = ki > qi
    top = lax.broadcasted_iota(I32, (2 * c, 1), 0) < c
    units = [(s, kvh) for s in range(SWA_STEP_BLOCKS) for kvh in range(SWA_KV_HEADS)]
    scores = []
    for s, kvh in units:
        rows, keys = slice(s * c, (s + 1) * c), slice(s * c, (s + 2) * c)
        q2 = jnp.concatenate([q_ref[rows, 2 * kvh * LANES:(2 * kvh + 1) * LANES],
                              q_ref[rows, (2 * kvh + 1) * LANES:(2 * kvh + 2) * LANES]], axis=0)
        kcat = jnp.concatenate([ks[kvh][0][keys], ks[kvh][1][keys]], axis=0)
        scores.append(lax.dot_general(q2, kcat, (((1,), (1,)), ((), ())), preferred_element_type=F32))
    probs = []
    for (s, kvh), sc in zip(units, scores):
        ps = []
        for half in range(2):
            sink = jnp.where(top, sink_ref[4 * kvh + half], sink_ref[4 * kvh + 2 + half])
            s_prev = sc[:, half * 2 * c:half * 2 * c + c]
            s_own = sc[:, half * 2 * c + c:(half + 1) * 2 * c]
            if s == 0:
                s_prev = jnp.where(n > 0, s_prev, -jnp.inf)
            p = _sink_softmax(jnp.where(from_prev, s_prev, s_own), None, sink)
            ps += [jnp.where(from_prev, p, 0.0).astype(BF16), jnp.where(from_prev, 0.0, p).astype(BF16)]
        probs.append(jnp.concatenate(ps, axis=1))
    for (s, kvh), p in zip(units, probs):
        rows, keys = slice(s * c, (s + 1) * c), slice(s * c, (s + 2) * c)
        vcat = jnp.concatenate([vs[kvh][0][keys], vs[kvh][1][keys]], axis=0)
        o = jnp.dot(p, vcat, preferred_element_type=F32)
        o_ref[rows, 2 * kvh * LANES:(2 * kvh + 1) * LANES] = o[:c].astype(BF16)
        o_ref[rows, (2 * kvh + 1) * LANES:(2 * kvh + 2) * LANES] = o[c:].astype(BF16)


def _swa_prompt(sq, skv, sink, *, nb, t):
    rows = SWA_STEP_BLOCKS * WINDOW
    nsteps = t // rows
    nblk = t // WINDOW
    cur = lambda b, n: (b * nsteps + n, 0)
    curc = lambda col: (lambda b, n: (b * nsteps + n, col))
    prevc = lambda col: (lambda b, n: (b * nblk + jnp.maximum(n * SWA_STEP_BLOCKS - 1, 0), col))
    return pl.pallas_call(
        _swa_kernel,
        out_shape=jax.ShapeDtypeStruct((nb * t, _SWA_QW), BF16),
        grid=(nb, nsteps),
        in_specs=[pl.BlockSpec(memory_space=pltpu.SMEM),
                  pl.BlockSpec((rows, _SWA_QW), cur),
                  pl.BlockSpec((rows, _SWA_KW), curc(0)),
                  pl.BlockSpec((WINDOW, _SWA_KW), prevc(0)),
                  pl.BlockSpec((rows, _SWA_KW), curc(1)),
                  pl.BlockSpec((WINDOW, _SWA_KW), prevc(1))],
        out_specs=pl.BlockSpec((rows, _SWA_QW), cur),
        compiler_params=_cparams("arbitrary", "arbitrary"),
    )(sink, sq, skv, skv, skv, skv)


def _swa_sample_kernel(sink_ref, q_ref, kn_ref, vn_ref, kc_ref, vc_ref, o_ref, ko_ref, vo_ref, *, sb, w):
    pad = jnp.zeros((LANES - sb, _SWA_KW), F32)
    knt = jnp.concatenate([kn_ref[...], pad], axis=0).T
    vnt = jnp.concatenate([vn_ref[...], pad], axis=0).T
    kall = jnp.concatenate([kc_ref[b] for b in range(sb)] + [knt], axis=1)
    vall = jnp.concatenate([vc_ref[b] for b in range(sb)] + [vnt], axis=1)
    ncol = sb * w + LANES
    lo = lax.broadcasted_iota(I32, (sb, LANES), 1) < SWA_HD
    group = SWA_HEADS // SWA_KV_HEADS
    pieces = []
    for h in range(SWA_HEADS):
        slab = q_ref[:, (h // 2) * LANES:(h // 2 + 1) * LANES].astype(F32)
        mine = jnp.where(lo, slab, 0.0) if h % 2 == 0 else jnp.where(lo, 0.0, slab)
        pieces.append(mine if (h % 2) == (h // group) else pltpu.roll(mine, SWA_HD, 1))
    qrows = jnp.concatenate(pieces, axis=0)
    nrow = SWA_HEADS * sb
    s = _bdot(qrows, kall)
    rb = lax.broadcasted_iota(I32, (nrow, ncol), 0) % sb
    ci = lax.broadcasted_iota(I32, (nrow, ncol), 1)
    in_cache = (ci < sb * w) & (ci // w == rb) & ((w - ci % w) < WINDOW)
    mask = in_cache | (ci == sb * w + rb)
    sink_col = jnp.concatenate([jnp.full((sb, 1), sink_ref[h], F32) for h in range(SWA_HEADS)], axis=0)
    p = _sink_softmax(s, mask, sink_col)
    o = _bdot_nt(p, vall)
    for j in range(SWA_HEADS // 2):
        acc = jnp.zeros((sb, LANES), F32)
        for half in range(2):
            h = 2 * j + half
            oh = o[h * sb:(h + 1) * sb]
            own = jnp.where(lo, oh, 0.0) if h // group == 0 else jnp.where(lo, 0.0, oh)
            acc = acc + (own if (h // group) == half else pltpu.roll(own, SWA_HD, 1))
        o_ref[:, j * LANES:(j + 1) * LANES] = acc.astype(BF16)
    newest = lax.broadcasted_iota(I32, (_SWA_KW, w), 1) == w - 1
    for b in range(sb):
        ko_ref[b] = jnp.where(newest, knt[:, b:b + 1], pltpu.roll(kc_ref[b], w - 1, 1))
        vo_ref[b] = jnp.where(newest, vnt[:, b:b + 1], pltpu.roll(vc_ref[b], w - 1, 1))


def _swa_sample(sq, skv, cache_kt, cache_vt, sink, *, row0, ns):
    sb = min(16, ns)
    w = cache_kt.shape[2]
    base = row0 // sb
    blk = lambda i: (base + i, 0)
    cblk = lambda i: (i, 0, 0)
    cspec = pl.BlockSpec((sb, _SWA_KW, w), cblk)
    return pl.pallas_call(
        functools.partial(_swa_sample_kernel, sb=sb, w=w),
        out_shape=[jax.ShapeDtypeStruct((ns, _SWA_QW), BF16),
                   jax.ShapeDtypeStruct(cache_kt.shape, F32), jax.ShapeDtypeStruct(cache_vt.shape, F32)],
        grid=(ns // sb,),
        in_specs=[pl.BlockSpec(memory_space=pltpu.SMEM),
                  pl.BlockSpec((sb, _SWA_QW), blk),
                  pl.BlockSpec((sb, _SWA_KW), lambda i: (base + i, 0)),
                  pl.BlockSpec((sb, _SWA_KW), lambda i: (base + i, 1)),
                  cspec, cspec],
        out_specs=[pl.BlockSpec((sb, _SWA_QW), lambda i: (i, 0)), cspec, cspec],
        compiler_params=_cparams("arbitrary"),
    )(sink, sq, skv, skv, cache_kt, cache_vt)


def _route(logits):
    lane = lax.broadcasted_iota(I32, logits.shape, 1).astype(F32)
    big = float(1 << 20)
    neg = -jnp.inf

    def top(mask):
        v = jnp.max(jnp.where(mask, logits, neg), axis=-1, keepdims=True)
        i = jnp.min(jnp.where(mask & (logits == v), lane, big), axis=-1, keepdims=True)
        return v, i

    gmask = lane < N_GROUPS
    gmax, gsel = top(gmask)
    p_group = 1.0 / jnp.sum(jnp.where(gmask, jnp.exp(logits - gmax), 0.0), axis=-1, keepdims=True)
    first = N_GROUPS + gsel * EXPERTS_PER_GROUP
    emask = (lane >= first) & (lane < first + EXPERTS_PER_GROUP)
    v1, i1 = top(emask)
    v2, i2 = top(emask & (lane != i1))
    t = jnp.exp(v2 - v1)
    w1 = p_group / (1.0 + t)
    return (i1 - N_GROUPS).astype(I32), (i2 - N_GROUPS).astype(I32), w1, w1 * t


def _plan_tile(e1, e2, valid, carry):
    lane = lax.broadcasted_iota(I32, (TM, LANES), 1)
    oh1 = ((lane == e1) & valid).astype(F32)
    oh2 = ((lane == e2) & valid).astype(F32)
    oh = oh1 + oh2
    tri = (lax.broadcasted_iota(I32, (TM, TM), 0) > lax.broadcasted_iota(I32, (TM, TM), 1)).astype(BF16)
    before = _bdot(tri, oh)
    cnt = jnp.sum(oh, axis=0, keepdims=True)
    units = jnp.maximum(jnp.floor((cnt + (RUN - 1)) * (1.0 / RUN)), 1.0)
    upper = (lax.broadcasted_iota(I32, (LANES, LANES), 0) < lax.broadcasted_iota(I32, (LANES, LANES), 1))
    lstart = RUN * _bdot(jnp.broadcast_to(units, (SUBLANES, LANES)), upper.astype(BF16))[0:1]
    slot = lstart + before
    lp1 = jnp.sum(oh1 * slot, axis=-1, keepdims=True)
    lp2 = jnp.sum(oh2 * slot, axis=-1, keepdims=True)
    vcol = valid[:, 0:1]
    lp = jnp.where(lane == 0, jnp.where(vcol, lp1, -1.0), jnp.where(lane == 1, jnp.where(vcol, lp2, -1.0), 0.0))
    base = carry[...]
    carry[...] = base + RUN * units
    return lp, (RUN * units).astype(I32), lstart.astype(I32), base.astype(I32)


def _outproj_step(gated_of, oswa_of, x_of, g1_of, sh_of, sc_of, n_valid, sig_ref, n2_ref, wur_ref, wus_ref,
                  wo_ref, wrc_ref, br_ref, x1_o, h2_o, plan_o, cnt_o, ls_o, gb_o, carry):
    subs = [slice(r, r + TM) for r in range(0, TT, TM)]
    merged = []
    for rows in subs:
        y_ret = jnp.dot(gated_of(rows), wur_ref[...], preferred_element_type=F32)
        y_swa = jnp.dot(oswa_of(rows), wus_ref[...], preferred_element_type=F32)
        d = y_ret.shape[1]
        merged.append((sig_ref[rows, :d].astype(F32) * y_ret + sig_ref[rows, d:].astype(F32) * y_swa).astype(BF16))
    hs = []
    for rows, m in zip(subs, merged):
        x1 = x_of(rows) + g1_of(rows) * jnp.dot(m, wo_ref[...], preferred_element_type=F32)
        x1_o[rows] = x1
        h2 = _rms(x1, n2_ref[...]) * (1.0 + sc_of(rows)) + sh_of(rows)
        hi = h2.astype(BF16)
        h2_o[rows] = hi
        hs.append((hi, (h2 - hi.astype(F32)).astype(BF16)))
    routed = []
    for rows, (hi, lo) in zip(subs, hs):
        both = jnp.dot(hi, wrc_ref[...], preferred_element_type=F32)
        logits = (both[:, :LANES] + both[:, LANES:]
                  + jnp.dot(lo, wrc_ref[:, :LANES], preferred_element_type=F32) + br_ref[...])
        e1, e2, w1, w2 = _route(logits)
        lane = lax.broadcasted_iota(I32, logits.shape, 1)
        routed.append((e1, e2, jnp.where(lane == 2, w1, jnp.where(lane == 3, w2, 0.0))))
    for sub, (e1, e2, weights) in enumerate(routed):
        row = lax.broadcasted_iota(I32, (TM, LANES), 0) + sub * TM
        valid = (row >= 0) if n_valid is None else (row < n_valid)
        lp, cnt_o[sub], ls_o[sub], gb_o[sub] = _plan_tile(e1, e2, valid, carry)
        plan_o[:, sub * TM:(sub + 1) * TM] = (lp + weights).T[0:SUBLANES]


def _outproj_kernel(gtp_ref, gts_ref, osp_ref, oss_ref, sig_ref, xp_ref, xs_ref, g1p_ref, shp_ref, scp_ref,
                    g1s_ref, shs_ref, scs_ref, *rest, nps, ns):
    i = pl.program_id(0)
    is_s = i >= nps
    carry = rest[-1]

    @pl.when(i == 0)
    def _():
        carry[...] = jnp.zeros_like(carry)

    @pl.when(jnp.logical_not(is_s))
    def _():
        _outproj_step(lambda rows: gtp_ref[rows], lambda rows: osp_ref[rows], lambda rows: xp_ref[rows],
                      lambda rows: g1p_ref[...], lambda rows: shp_ref[...], lambda rows: scp_ref[...], None,
                      sig_ref, *rest)

    @pl.when(is_s)
    def _():
        _outproj_step(lambda rows: gts_ref[rows], lambda rows: oss_ref[rows], lambda rows: xs_ref[rows],
                      lambda rows: g1s_ref[rows], lambda rows: shs_ref[rows], lambda rows: scs_ref[rows], ns,
                      sig_ref, *rest)


def _outproj(gated_p, gated_s, oswa_p, oswa_s, sig2, xp, xs_pad, modp, mods, n2, wur, wus, wo, wr, br,
             *, nps, spb, nb, ns):
    d = xp.shape[1]
    sub = TT // TM
    nrow = (nps + 1) * TT
    row = lambda i: (i, 0)
    pstep = lambda i: (jnp.minimum(i, nps - 1), 0)
    pbatch = lambda col: (lambda i: (jnp.minimum(i // spb, nb - 1), 0, col))
    scol = lambda col: (lambda i: (0, col))
    const = lambda i: (0, 0)
    wr_hi = wr.astype(BF16)
    wrc = jnp.concatenate([wr_hi, (wr - wr_hi.astype(F32)).astype(BF16)], axis=1)
    meta = jax.ShapeDtypeStruct(((nps + 1) * sub, 1, LANES), I32)
    mspec = pl.BlockSpec((sub, 1, LANES), lambda i: (i, 0, 0))
    return pl.pallas_call(
        functools.partial(_outproj_kernel, nps=nps, ns=ns),
        out_shape=[jax.ShapeDtypeStruct((nrow, d), F32),
                   jax.ShapeDtypeStruct((nrow, d), BF16),
                   jax.ShapeDtypeStruct((SUBLANES, nrow), F32), meta, meta, meta],
        grid=(nps + 1,),
        in_specs=[pl.BlockSpec((TT, _RET_W), pstep), pl.BlockSpec((TT, _RET_W), const),
                  pl.BlockSpec((TT, _SWA_QW), pstep), pl.BlockSpec((TT, _SWA_QW), const),
                  pl.BlockSpec((TT, 2 * d), row),
                  pl.BlockSpec((TT, d), pstep), pl.BlockSpec((TT, d), const),
                  pl.BlockSpec((None, 1, d), pbatch(2)), pl.BlockSpec((None, 1, d), pbatch(3)),
                  pl.BlockSpec((None, 1, d), pbatch(4)),
                  pl.BlockSpec((TT, d), scol(2)), pl.BlockSpec((TT, d), scol(3)), pl.BlockSpec((TT, d), scol(4)),
                  pl.BlockSpec((1, d), const),
                  pl.BlockSpec(wur.shape, const), pl.BlockSpec(wus.shape, const), pl.BlockSpec(wo.shape, const),
                  pl.BlockSpec(wrc.shape, const), pl.BlockSpec((1, LANES), const)],
        out_specs=[pl.BlockSpec((TT, d), row), pl.BlockSpec((TT, d), row),
                   pl.BlockSpec((SUBLANES, TT), lambda i: (0, i)), mspec, mspec, mspec],
        scratch_shapes=[pltpu.VMEM((1, LANES), F32)],
        compiler_params=_cparams("arbitrary"),
    )(gated_p, gated_s, oswa_p, oswa_s, sig2, xp, xs_pad, modp, modp, modp, mods, mods, mods, n2,
      wur, wus, wo, wrc, br)


def _units(rows):
    return rows.reshape(rows.shape[0] // RUN, RUN, rows.shape[1])


def _run_copy(src, dst, s_start, d_start, n, sem):
    return pltpu.make_async_copy(src.at[pl.ds(s_start, n)], dst.at[pl.ds(d_start, n)], sem)


def _tile_rows(cnt_ref, ls_ref, step):
    last = step * N_EXPERTS + N_EXPERTS - 1
    return ls_ref[last] + cnt_ref[last]


def _each_run(step, fn):
    for e in range(N_EXPERTS):
        fn(step * N_EXPERTS + e)


def _dispatch_kernel(cnt_ref, ls_ref, gd_ref, ps_ref, pn_ref, nu_ref, h_ref, lp_ref, xs_ref,
                     sorted_scr, zero_scr, sem, zsem, *, nt, maxt):
    i = pl.program_id(0)

    def pad(e):
        return _run_copy(zero_scr, xs_ref, 0, ps_ref[e], pn_ref[e], zsem)

    def tail(j):
        return _run_copy(zero_scr, xs_ref, 0, j * (TE // RUN), TE // RUN, zsem)

    def each_pad(fn):
        def body(e, c):
            @pl.when(pn_ref[e] > 0)
            def _():
                fn(pad(e))
            return c
        lax.fori_loop(0, N_EXPERTS, body, 0)

    def each_tail(fn):
        def body(j, c):
            fn(tail(j))
            return c
        lax.fori_loop(nu_ref[0], maxt, body, 0)

    @pl.when(i == 0)
    def _():
        zero_scr[...] = jnp.zeros_like(zero_scr)
        each_pad(lambda cp: cp.start())
        each_tail(lambda cp: cp.start())

    lpt = lp_ref[...]
    slot = lax.broadcasted_iota(I32, (SLOTS, TM), 0).astype(F32)
    perm = ((slot == lpt[0:1]) | (slot == lpt[1:2])).astype(BF16)
    sorted_scr[i % 2] = _units(jnp.dot(perm, h_ref[...], preferred_element_type=F32).astype(BF16))

    def copy(step):
        return lambda k: _run_copy(sorted_scr.at[step % 2], xs_ref, ls_ref[k], gd_ref[k], cnt_ref[k], sem.at[step % 2])

    def wait_all(step):
        _run_copy(sorted_scr.at[step % 2], xs_ref, 0, 0, _tile_rows(cnt_ref, ls_ref, step), sem.at[step % 2]).wait()

    _each_run(i, lambda k: copy(i)(k).start())

    @pl.when(i > 0)
    def _():
        wait_all(i - 1)

    @pl.when(i == nt - 1)
    def _():
        wait_all(i)
        each_pad(lambda cp: cp.wait())
        each_tail(lambda cp: cp.wait())


def _dispatch(cnt, ls, gd, ps, pn, nu, h2, lp, *, nt, maxt):
    d = h2.shape[1]
    return pl.pallas_call(
        functools.partial(_dispatch_kernel, nt=nt, maxt=maxt),
        out_shape=jax.ShapeDtypeStruct((maxt * TE // RUN, RUN, d), BF16),
        grid_spec=pltpu.PrefetchScalarGridSpec(
            num_scalar_prefetch=6,
            grid=(nt,),
            in_specs=[pl.BlockSpec((TM, d), lambda i, *_: (i, 0)),
                      pl.BlockSpec((SUBLANES, TM), lambda i, *_: (0, i))],
            out_specs=pl.BlockSpec(memory_space=pl.ANY),
            scratch_shapes=[pltpu.VMEM((2, SLOTS // RUN, RUN, d), BF16), pltpu.VMEM((TE // RUN, RUN, d), BF16),
                            pltpu.SemaphoreType.DMA((2,)), pltpu.SemaphoreType.DMA(())]),
        compiler_params=_cparams("arbitrary"),
    )(cnt, ls, gd, ps, pn, nu, h2, lp)


def _experts_kernel(te_ref, nu_ref, start_ref, nxt_ref, par_ref, x_ref, w1_hbm, w3_hbm, w2_hbm, y_ref,
                    w1f, w3f, w2f, w1b, w3b, w2b, sem):
    j = pl.program_id(0)
    e = te_ref[j]

    def fetch(ex, slot):
        return [pltpu.make_async_copy(src.at[ex], dst.at[slot], sem.at[slot, n])
                for n, (src, dst) in enumerate(((w1_hbm, w1f), (w3_hbm, w3f), (w2_hbm, w2f)))]

    @pl.when(j == 0)
    def _():
        for cp in fetch(e, par_ref[e]):
            cp.start()

    @pl.when((j == start_ref[e]) & (j < nu_ref[0]))
    def _():
        slot = par_ref[e]

        @pl.when(nxt_ref[e] >= 0)
        def _():
            for cp in fetch(nxt_ref[e], 1 - slot):
                cp.start()

        for cp in fetch(e, slot):
            cp.wait()
        w1b[...] = w1f[slot].astype(BF16)
        w3b[...] = w3f[slot].astype(BF16)
        w2b[...] = w2f[slot].astype(BF16)

    @pl.when(j < nu_ref[0])
    def _():
        subs = [slice(r // RUN, (r + TE_SUB) // RUN) for r in range(0, TE, TE_SUB)]
        xs = [x_ref[u].reshape(TE_SUB, x_ref.shape[-1]) for u in subs]
        ab = [(jnp.dot(x, w1b[...], preferred_element_type=F32),
               jnp.dot(x, w3b[...], preferred_element_type=F32)) for x in xs]
        hid = [(_silu(a) * b).astype(BF16) for a, b in ab]
        for u, h in zip(subs, hid):
            y_ref[u] = _units(jnp.dot(h, w2b[...], preferred_element_type=F32).astype(BF16))


def _experts(te, nu, start, nxt, par, xs, w1, w3, w2, *, maxt):
    d = xs.shape[2]
    f = w1.shape[2]
    used = lambda j, te, nu, *_: (jnp.minimum(j, nu[0] - 1), 0, 0)
    hbm = pl.BlockSpec(memory_space=pl.ANY)
    return pl.pallas_call(
        _experts_kernel,
        out_shape=jax.ShapeDtypeStruct(xs.shape, BF16),
        grid_spec=pltpu.PrefetchScalarGridSpec(
            num_scalar_prefetch=5,
            grid=(maxt,),
            in_specs=[pl.BlockSpec((TE // RUN, RUN, d), used), hbm, hbm, hbm],
            out_specs=pl.BlockSpec((TE // RUN, RUN, d), used),
            scratch_shapes=[pltpu.VMEM((2, d, f), F32), pltpu.VMEM((2, d, f), F32), pltpu.VMEM((2, f, d), F32),
                            pltpu.VMEM((d, f), BF16), pltpu.VMEM((d, f), BF16), pltpu.VMEM((f, d), BF16),
                            pltpu.SemaphoreType.DMA((2, 3))]),
        input_output_aliases={5: 0},
        compiler_params=_cparams("arbitrary"),
    )(te, nu, start, nxt, par, xs, w1, w3, w2)


def _combine_kernel(cnt_ref, ls_ref, gd_ref, ys_ref, plan_ref, x1_ref, g2p_ref, g2s_ref, fg_ref,
                    yp_o, ys_o, ybuf, sem, *, npt):
    i = pl.program_id(0)
    nt = npt + 1

    def fetch(step):
        ybuf[step % 2] = jnp.zeros(ybuf.shape[1:], BF16)
        _each_run(step, lambda k: _run_copy(ys_ref, ybuf.at[step % 2], gd_ref[k], ls_ref[k], cnt_ref[k],
                                            sem.at[step % 2]).start())

    @pl.when(i == 0)
    def _():
        fetch(0)

    @pl.when(i + 1 < nt)
    def _():
        fetch(i + 1)

    _run_copy(ys_ref, ybuf.at[i % 2], 0, 0, _tile_rows(cnt_ref, ls_ref, i), sem.at[i % 2]).wait()

    yb = ybuf[i % 2].reshape(SLOTS, x1_ref.shape[1])
    slot = lax.broadcasted_iota(I32, (TM, SLOTS), 1).astype(F32)
    lp = rt = jnp.concatenate([plan_ref[...], jnp.zeros((LANES - SUBLANES, TM), F32)], axis=0).T

    def unsort(col):
        return jnp.dot((slot == lp[:, col:col + 1]).astype(BF16), yb, preferred_element_type=F32)

    moe = rt[:, 2:3] * unsort(0) + rt[:, 3:4] * unsort(1)
    g2 = jnp.where(i >= npt, g2s_ref[...], g2p_ref[...])
    y = _rms(x1_ref[...] + g2 * moe, fg_ref[...])

    @pl.when(i < npt)
    def _():
        yp_o[...] = y

    @pl.when(i >= npt)
    def _():
        ys_o[...] = y


def _combine(cnt, ls, gd, ys, plan, x1, modp, mods, fg, *, npt, tpb, nb):
    d = x1.shape[1]
    nt = npt + 1
    row = lambda i, *_: (i, 0)
    return pl.pallas_call(
        functools.partial(_combine_kernel, npt=npt),
        out_shape=[jax.ShapeDtypeStruct((npt * TM, d), F32), jax.ShapeDtypeStruct((TM, d), F32)],
        grid_spec=pltpu.PrefetchScalarGridSpec(
            num_scalar_prefetch=3,
            grid=(nt,),
            in_specs=[pl.BlockSpec(memory_space=pl.ANY),
                      pl.BlockSpec((SUBLANES, TM), lambda i, *_: (0, i)), pl.BlockSpec((TM, d), row),
                      pl.BlockSpec((None, 1, d), lambda i, *_: (jnp.minimum(i // tpb, nb - 1), 0, 5)),
                      pl.BlockSpec((TM, d), lambda i, *_: (0, 5)),
                      pl.BlockSpec((1, d), lambda i, *_: (0, 0))],
            out_specs=[pl.BlockSpec((TM, d), lambda i, *_: (jnp.minimum(i, npt - 1), 0)),
                       pl.BlockSpec((TM, d), lambda i, *_: (0, 0))],
            scratch_shapes=[pltpu.VMEM((2, SLOTS // RUN, RUN, d), BF16), pltpu.SemaphoreType.DMA((2,))]),
        compiler_params=_cparams("arbitrary"),
    )(cnt, ls, gd, ys, plan, x1, modp, mods, fg)


def _rotation_tables(t):
    inv_r = jnp.repeat(1.0 / (ROPE_THETA ** jnp.linspace(0.0, 1.0, RET_DK // 2, dtype=F32)), 2)
    sign_r = jnp.where(jnp.arange(RET_DK) % 2 == 0, -1.0, 1.0).astype(F32)
    inv_w = jnp.tile(ROPE_THETA ** (-jnp.arange(0, SWA_HD, 2, dtype=F32) / SWA_HD), LANES // (SWA_HD // 2))
    sign_w = jnp.where(jnp.arange(LANES) % SWA_HD < SWA_HD // 2, -1.0, 1.0).astype(F32)
    hi = (jnp.arange(t // TAB_LO, dtype=I32) * TAB_LO).astype(F32)[:, None]
    lo = jnp.arange(TAB_LO, dtype=I32).astype(F32)[:, None]
    past = jnp.full((1, 1), PAST_LEN, F32)
    his, los, sample = [], [], []
    for inv, sign in ((inv_r[None, :], sign_r[None, :]), (inv_w[None, :], sign_w[None, :])):
        a, b, p = hi * inv, lo * inv, past * inv
        his += [jnp.cos(a), jnp.sin(a)]
        los += [jnp.cos(b), jnp.sin(b), jnp.cos(b) * sign, jnp.sin(b) * sign]
        sample += [jnp.cos(p), jnp.sin(p) * sign]
    return tuple(jnp.concatenate(parts, axis=1) for parts in (his, los, sample))


def kernel(x_prompt, x_sample, c_prompt, c_sample, state_ret, cache_swa_k, cache_swa_v, w_ada, b_ada, norm1_g, norm2_g, w_in, w_up_ret, w_up_swa, w_o, sink, w_rg, b_rg, w_re, b_re, w1, w3, w2, final_g):
    nb, t, d = x_prompt.shape
    ns, dec_seq, _ = x_sample.shape
    depth = w_ada.shape[0]
    assert depth == 1 and dec_seq == 1, "single layer, one new token per sequence"
    assert t % TT == 0 and ns <= TM and ns % 16 == 0 and d % LANES == 0
    assert t % RET_BLOCK == 0 and t % (SWA_STEP_BLOCKS * WINDOW) == 0 and TT % (SUBLANES * TAB_LO) == 0
    assert N_GROUPS + N_EXPERTS <= LANES
    w = cache_swa_k.shape[2]
    tpb = t // TM
    npt = nb * tpb
    spb = t // TT
    nps = nb * spb
    nt = (nps + 1) * (TT // TM)
    np_rows = nb * t
    n_tok = np_rows + ns
    maxt = -(-(2 * n_tok + nt * N_EXPERTS * RUN + N_EXPERTS * (TE - 1)) // TE)

    xp = x_prompt.reshape(np_rows, d)
    xs_pad = jnp.pad(x_sample.reshape(ns, d), ((0, TT - ns), (0, 0)))

    c_all = jnp.concatenate([jnp.pad(c_sample, ((0, TT - ns), (0, 0))),
                             jnp.pad(c_prompt, ((0, SUBLANES - nb % SUBLANES), (0, 0)))])
    mods = _modulation(c_all, w_ada[0], b_ada[0])
    modp = mods[TT:TT + nb].reshape(nb, 1, 6 * d)

    ret4, sq, skv, sig2 = _inproj(
        xp, xs_pad, modp, mods, norm1_g, w_in[0].astype(BF16), *_rotation_tables(t), nps=nps, spb=spb, nb=nb)

    gamma = jnp.asarray([1.0 - 2.0 ** (-5.0 - h) for h in range(RET_HEADS)], F32)
    gated_p, st_p = _retention_prompt(ret4, nb=nb, t=t)
    gated_s, st_s = _retention_sample(ret4, state_ret[0], gamma, row0=np_rows, ns=ns)
    oswa_p = _swa_prompt(sq, skv, sink[0], nb=nb, t=t)
    to_t = lambda c: jnp.transpose(c[0], (0, 2, 3, 1)).reshape(ns, _SWA_KW, w)
    from_t = lambda c: jnp.transpose(c.reshape(ns, SWA_KV_HEADS, SWA_HD, w), (0, 3, 1, 2))[None]
    oswa_s, ks_new, vs_new = _swa_sample(sq, skv, to_t(cache_swa_k), to_t(cache_swa_v), sink[0],
                                         row0=np_rows, ns=ns)
    gated_s = jnp.pad(gated_s, ((0, TT - ns), (0, 0)))
    oswa_s = jnp.pad(oswa_s, ((0, TT - ns), (0, 0)))

    wr = jnp.pad(jnp.concatenate([w_rg[0], w_re[0]], axis=1), ((0, 0), (0, LANES - N_GROUPS - N_EXPERTS)))
    br = jnp.pad(jnp.concatenate([b_rg[0], b_re[0]]), (0, LANES - N_GROUPS - N_EXPERTS)).reshape(1, LANES)
    x1, h2, plan, cnt, ls, gb = _outproj(
        gated_p, gated_s, oswa_p, oswa_s, sig2, xp, xs_pad, modp, mods, norm2_g,
        w_up_ret[0].astype(BF16), w_up_swa[0].astype(BF16), w_o[0].astype(BF16), wr, br,
        nps=nps, spb=spb, nb=nb, ns=ns)
    cnt = cnt[:, 0, :N_EXPERTS]
    ls = ls[:, 0, :N_EXPERTS]
    gb = gb[:, 0, :N_EXPERTS]
    seg = jnp.sum(cnt, axis=0)
    tiles = (seg + TE - 1) // TE
    tile_end = jnp.cumsum(tiles)
    row_start = (tile_end - tiles) * TE
    gd = (gb + row_start[None, :]).reshape(-1)
    n_used = tile_end[-1:]
    jj = jnp.minimum(jnp.arange(maxt, dtype=I32), n_used[0] - 1)
    te = jnp.minimum(jnp.sum((tile_end[None, :] <= jj[:, None]).astype(I32), axis=1), N_EXPERTS - 1)
    cnt, ls, gd = cnt.reshape(-1) // RUN, ls.reshape(-1) // RUN, gd // RUN
    n_used = n_used.astype(I32)
    xs = _dispatch(cnt, ls, gd, (row_start + seg) // RUN, (tiles * TE - seg) // RUN, n_used, h2, plan,
                   nt=nt, maxt=maxt)
    has = tiles > 0
    eidx = jnp.arange(N_EXPERTS, dtype=I32)
    later = jnp.where(has[None, :] & (eidx[None, :] > eidx[:, None]), eidx[None, :], N_EXPERTS)
    nxt_e = jnp.min(later, axis=1)
    nxt_e = jnp.where(nxt_e < N_EXPERTS, nxt_e, -1).astype(I32)
    par_e = ((jnp.cumsum(has.astype(I32)) - 1) % 2).astype(I32)
    ys = _experts(te, n_used, (tile_end - tiles).astype(I32), nxt_e, par_e, xs, w1[0], w3[0], w2[0], maxt=maxt)
    y_p, y_s = _combine(cnt, ls, gd, ys, plan, x1, modp, mods, final_g.reshape(1, d), npt=npt, tpb=tpb, nb=nb)

    y_prompt = y_p.reshape(nb, t, d)
    y_sample = y_s[:ns].reshape(ns, 1, d)
    wk = min(WINDOW, t)
    tails = jnp.stack([skv[(b + 1) * t - wk:(b + 1) * t] for b in range(nb)])
    skp, svp = (tails[:, :, part * _SWA_KW:(part + 1) * _SWA_KW].reshape(nb, wk, SWA_KV_HEADS, SWA_HD)
                for part in range(2))
    return (y_prompt, y_sample, st_p[None], st_s[None], skp[None], svp[None], from_t(ks_new), from_t(vs_new))
```

```python
import functools
import math

import jax
import jax.numpy as jnp
from jax import lax
from jax.experimental import pallas as pl
from jax.experimental.pallas import tpu as pltpu

F32 = jnp.float32
BF16 = jnp.bfloat16
I32 = jnp.int32

PAST_LEN = 8192
RET_HEADS = 4
RET_DK = 128
RET_DV = 128
RET_CHUNK = 128
SWA_HEADS = 8
SWA_KV_HEADS = 2
SWA_HD = 64
WINDOW = 128
ROPE_THETA = 10000.0
N_GROUPS = 4
EXPERTS_PER_GROUP = 8
N_EXPERTS = N_GROUPS * EXPERTS_PER_GROUP
D_EXPERT = 256
NORM_EPS = 1e-6

LANES = 128
SUBLANES = 8
TM = 256
TT = 2 * TM
RUN = 16
SLOTS = 2 * TM + N_EXPERTS * RUN
TE = 512
TE_SUB = 256
VMEM_LIMIT = 56 * 1024 * 1024
TAB_LO = 64

_RET_W = RET_HEADS * RET_DK
_SWA_QW = SWA_HEADS * SWA_HD
_SWA_KW = SWA_KV_HEADS * SWA_HD


def _cparams(*sem):
    return pltpu.CompilerParams(dimension_semantics=sem, vmem_limit_bytes=VMEM_LIMIT)


def _sigmoid(x):
    return 1.0 / (1.0 + jnp.exp(-x))


def _silu(x):
    return x * _sigmoid(x)


def _bdot(a, b):
    return jnp.dot(a.astype(BF16), b.astype(BF16), preferred_element_type=F32)


def _bdot_nt(a, b):
    return lax.dot_general(a.astype(BF16), b.astype(BF16), (((1,), (1,)), ((), ())), preferred_element_type=F32)


def _mod_kernel(c_ref, w_ref, b_ref, o_ref):
    o_ref[...] = _bdot(_silu(c_ref[...]), w_ref[...]) + b_ref[...]


def _modulation(c_all, w_ada, b_ada):
    rows, d = c_all.shape
    n = w_ada.shape[1]
    return pl.pallas_call(
        _mod_kernel,
        out_shape=jax.ShapeDtypeStruct((rows, n), F32),
        grid=(n // d,),
        in_specs=[pl.BlockSpec((rows, d), lambda j: (0, 0)),
                  pl.BlockSpec((d, d), lambda j: (0, j)),
                  pl.BlockSpec((1, d), lambda j: (0, j))],
        out_specs=pl.BlockSpec((rows, d), lambda j: (0, j)),
        compiler_params=_cparams("arbitrary"),
    )(c_all, w_ada, b_ada.reshape(1, n))


def _rms(x, g):
    return x * lax.rsqrt(jnp.mean(x * x, axis=-1, keepdims=True) + NORM_EPS) * g


def _pair_rotate(z, cos, sin_signed):
    n = z.shape[-1]
    lane = lax.broadcasted_iota(I32, z.shape, 1)
    partner = jnp.where((lane & 1) == 0, pltpu.roll(z, n - 1, 1), pltpu.roll(z, 1, 1))
    reps = n // LANES
    cos = jnp.concatenate([cos] * reps, axis=1) if reps > 1 else cos
    sin_signed = jnp.concatenate([sin_signed] * reps, axis=1) if reps > 1 else sin_signed
    return z * cos + partner * sin_signed


def _half_rotate(z, cos, sin_signed):
    n = z.shape[-1]
    half = SWA_HD // 2
    lane = lax.broadcasted_iota(I32, z.shape, 1)
    partner = jnp.where((lane & (SWA_HD - 1)) < half, pltpu.roll(z, n - half, 1), pltpu.roll(z, half, 1))
    reps = n // LANES
    cos = jnp.concatenate([cos] * reps, axis=1) if reps > 1 else cos
    sin_signed = jnp.concatenate([sin_signed] * reps, axis=1) if reps > 1 else sin_signed
    return z * cos + partner * sin_signed


def _inproj_step(x_of, sh_of, sc_of, tabs_of, n1_ref, w_ref, outs):
    ret_o, sq_o, skv_o, sig_o = outs
    d = w_ref.shape[0]
    subs = []
    for r in range(0, TT, TM):
        rows = slice(r, r + TM)
        h = (_rms(x_of(rows), n1_ref[...]) * (1.0 + sc_of(rows)) + sh_of(rows)).astype(BF16)
        subs.append((rows, h, tabs_of(rows)))

    def seg(h, a, b):
        return jnp.dot(h, w_ref[:, a:b], preferred_element_type=F32)

    o = 0
    for rows, h, (cr, sr, cw, sw) in subs:
        ret_o[rows, 0:_RET_W] = _pair_rotate(seg(h, o, o + _RET_W), cr, sr).astype(BF16)
    o += _RET_W
    for rows, h, (cr, sr, cw, sw) in subs:
        ret_o[rows, _RET_W:2 * _RET_W] = (_pair_rotate(seg(h, o, o + _RET_W), cr, sr)
                                          * (RET_DK ** -0.5)).astype(BF16)
    o += _RET_W
    for rows, h, _ in subs:
        ret_o[rows, 2 * _RET_W:3 * _RET_W] = seg(h, o, o + _RET_W).astype(BF16)
    o += _RET_W
    for rows, h, _ in subs:
        ret_o[rows, 3 * _RET_W:4 * _RET_W] = _silu(seg(h, o, o + _RET_W)).astype(BF16)
    o += _RET_W
    for rows, h, (cr, sr, cw, sw) in subs:
        sq_o[rows] = (_half_rotate(seg(h, o, o + _SWA_QW), cw, sw) * (SWA_HD ** -0.5)).astype(BF16)
    o += _SWA_QW
    for rows, h, (cr, sr, cw, sw) in subs:
        zkv = seg(h, o, o + 2 * _SWA_KW)
        skv_o[rows, :_SWA_KW] = _half_rotate(zkv[:, :_SWA_KW], cw, sw)
        skv_o[rows, _SWA_KW:] = zkv[:, _SWA_KW:]
    o += 2 * _SWA_KW
    for rows, h, _ in subs:
        sig_o[rows, :d] = _sigmoid(seg(h, o, o + d)).astype(BF16)
    o += d
    for rows, h, _ in subs:
        sig_o[rows, d:] = _sigmoid(seg(h, o, o + d)).astype(BF16)


def _prompt_tables(th_ref, tl_ref, rows):
    tabs = []
    for rot in range(2):
        cb, sb, cbs, sbs = (tl_ref[:, (4 * rot + q) * LANES:(4 * rot + q + 1) * LANES] for q in range(4))
        cos, sin = [], []
        for g in range(rows.start // TAB_LO, rows.stop // TAB_LO):
            ca = th_ref[g:g + 1, 2 * rot * LANES:(2 * rot + 1) * LANES]
            sa = th_ref[g:g + 1, (2 * rot + 1) * LANES:(2 * rot + 2) * LANES]
            cos.append(ca * cb - sa * sb)
            sin.append(sa * cbs + ca * sbs)
        tabs += [jnp.concatenate(cos, axis=0), jnp.concatenate(sin, axis=0)]
    return tuple(tabs)


def _inproj_kernel(xp_ref, xs_ref, shp_ref, scp_ref, shs_ref, scs_ref, n1_ref, w_ref, th_ref, tl_ref, ts_ref,
                   *outs, nps):
    is_s = pl.program_id(0) >= nps

    @pl.when(jnp.logical_not(is_s))
    def _():
        _inproj_step(lambda rows: xp_ref[rows], lambda rows: shp_ref[...], lambda rows: scp_ref[...],
                     functools.partial(_prompt_tables, th_ref, tl_ref),
                     n1_ref, w_ref, outs)

    @pl.when(is_s)
    def _():
        _inproj_step(lambda rows: xs_ref[rows], lambda rows: shs_ref[rows], lambda rows: scs_ref[rows],
                     lambda rows: tuple(ts_ref[:, q * LANES:(q + 1) * LANES] for q in range(4)),
                     n1_ref, w_ref, outs)


def _inproj(xp, xs_pad, modp, mods, n1, w_in_b, tab_hi, tab_lo, tab_s, *, nps, spb, nb):
    d = xp.shape[1]
    nrow = (nps + 1) * TT
    n_in = w_in_b.shape[1]
    pstep = lambda i: (jnp.minimum(i, nps - 1), 0)
    pbatch = lambda col: (lambda i: (jnp.minimum(i // spb, nb - 1), 0, col))
    tab_idx = lambda i: (jnp.where(i < nps, i % spb, 0), 0)
    out_cols = [(4 * _RET_W, BF16), (_SWA_QW, BF16), (2 * _SWA_KW, F32), (2 * d, BF16)]
    return pl.pallas_call(
        functools.partial(_inproj_kernel, nps=nps),
        out_shape=[jax.ShapeDtypeStruct((nrow, c), t) for c, t in out_cols],
        grid=(nps + 1,),
        in_specs=[pl.BlockSpec((TT, d), pstep),
                  pl.BlockSpec((TT, d), lambda i: (0, 0)),
                  pl.BlockSpec((None, 1, d), pbatch(0)),
                  pl.BlockSpec((None, 1, d), pbatch(1)),
                  pl.BlockSpec((TT, d), lambda i: (0, 0)),
                  pl.BlockSpec((TT, d), lambda i: (0, 1)),
                  pl.BlockSpec((1, d), lambda i: (0, 0)),
                  pl.BlockSpec((d, n_in), lambda i: (0, 0))]
                 + [pl.BlockSpec((TT // TAB_LO, 4 * LANES), tab_idx),
                    pl.BlockSpec(tab_lo.shape, lambda i: (0, 0)), pl.BlockSpec(tab_s.shape, lambda i: (0, 0))],
        out_specs=[pl.BlockSpec((TT, c), lambda i: (i, 0)) for c, _ in out_cols],
        compiler_params=_cparams("arbitrary"),
    )(xp, xs_pad, modp, modp, mods, mods, n1, w_in_b, tab_hi, tab_lo, tab_s)


RET_BLOCK = 512


def _ret_kernel(q_ref, k_ref, v_ref, g_ref, o_ref, st_ref, s_scr, dm_scr, qd_scr, kd_scr, *, nsteps):
    step = pl.program_id(1)
    rows = q_ref.shape[0]
    log_gamma = [math.log(1.0 - 2.0 ** (-5.0 - h)) for h in range(RET_HEADS)]

    @pl.when((pl.program_id(0) == 0) & (step == 0))
    def _():
        diff = (lax.broadcasted_iota(I32, (rows, rows), 0) - lax.broadcasted_iota(I32, (rows, rows), 1)).astype(F32)
        idx = lax.broadcasted_iota(I32, (rows, RET_DV), 0).astype(F32)
        for h, ld in enumerate(log_gamma):
            dm_scr[h] = jnp.where(diff >= 0, jnp.exp(ld * jnp.maximum(diff, 0.0)), 0.0)
            qd_scr[h] = jnp.exp(ld * (idx + 1.0))
            kd_scr[h] = jnp.exp(ld * (rows - 1.0 - idx))

    @pl.when(step == 0)
    def _():
        s_scr[...] = jnp.zeros_like(s_scr)

    for h in range(RET_HEADS):
        sl = slice(h * RET_DK, (h + 1) * RET_DK)
        state = s_scr[h]
        q, k, v = q_ref[:, sl], k_ref[:, sl], v_ref[:, sl]
        att = _bdot_nt(q, k) * dm_scr[h]
        o = _bdot(att, v) + _bdot(q.astype(F32) * qd_scr[h], state)
        kd = (k.astype(F32) * kd_scr[h]).astype(BF16)
        kv = lax.dot_general(kd, v, (((0,), (0,)), ((), ())), preferred_element_type=F32)
        s_scr[h] = math.exp(log_gamma[h] * rows) * state + kv
        o = o * lax.rsqrt(jnp.mean(o * o, axis=-1, keepdims=True) + NORM_EPS)
        o_ref[:, sl] = (o * g_ref[:, sl].astype(F32)).astype(BF16)

    @pl.when(step == nsteps - 1)
    def _():
        st_ref[...] = s_scr[...]


def _retention_prompt(ret4, *, nb, t):
    rows = RET_BLOCK
    nsteps = t // rows
    blk = lambda b, c: (b * nsteps + c, 0)
    return pl.pallas_call(
        functools.partial(_ret_kernel, nsteps=nsteps),
        out_shape=[jax.ShapeDtypeStruct((nb * t, _RET_W), BF16),
                   jax.ShapeDtypeStruct((nb, RET_HEADS, RET_DK, RET_DV), F32)],
        grid=(nb, nsteps),
        in_specs=[pl.BlockSpec((rows, _RET_W), lambda b, c, col=col: (b * nsteps + c, col)) for col in range(4)],
        out_specs=[pl.BlockSpec((rows, _RET_W), blk),
                   pl.BlockSpec((None, RET_HEADS, RET_DK, RET_DV), lambda b, c: (b, 0, 0, 0))],
        scratch_shapes=[pltpu.VMEM((RET_HEADS, RET_DK, RET_DV), F32), pltpu.VMEM((RET_HEADS, rows, rows), F32),
                        pltpu.VMEM((RET_HEADS, rows, RET_DV), F32), pltpu.VMEM((RET_HEADS, rows, RET_DV), F32)],
        compiler_params=_cparams("arbitrary", "arbitrary"),
    )(ret4, ret4, ret4, ret4)


def _ret_sample_kernel(gam_ref, q_ref, k_ref, v_ref, g_ref, s0_ref, o_ref, st_ref, *, sb):
    gamma = gam_ref[pl.program_id(1)]
    q = q_ref[...].astype(F32)
    k = k_ref[...].astype(F32)
    v = v_ref[...].astype(F32)
    rows = sb * RET_DK
    s2 = s0_ref[...].reshape(rows, RET_DV)
    col_b = lax.broadcasted_iota(I32, (sb, rows), 1) // RET_DK
    row_b = lax.broadcasted_iota(I32, (sb, rows), 0)
    qexp = jnp.where(col_b == row_b, jnp.concatenate([q * gamma] * sb, axis=1), 0.0)
    o = jnp.sum(q * k, axis=-1, keepdims=True) * v + _bdot(qexp, s2)
    o = o * lax.rsqrt(jnp.mean(o * o, axis=-1, keepdims=True) + NORM_EPS)
    o_ref[...] = (o * g_ref[...].astype(F32)).astype(BF16)
    rep = (lax.broadcasted_iota(I32, (rows, sb), 0) // RET_DK == lax.broadcasted_iota(I32, (rows, sb), 1))
    rep = rep.astype(BF16)
    krep = _bdot(rep, k)
    vrep = _bdot(rep, v)
    eye = (lax.broadcasted_iota(I32, (rows, RET_DK), 0) % RET_DK == lax.broadcasted_iota(I32, (rows, RET_DK), 1))
    kcol = jnp.sum(jnp.where(eye, krep, 0.0), axis=-1, keepdims=True)
    st_ref[...] = (gamma * s2 + kcol * vrep).reshape(sb, RET_DK, RET_DV)


def _retention_sample(ret4, s0, gamma, *, row0, ns):
    sb = min(64, ns)
    base = row0 // sb
    sblk = lambda i, h: (i, h, 0, 0)
    return pl.pallas_call(
        functools.partial(_ret_sample_kernel, sb=sb),
        out_shape=[jax.ShapeDtypeStruct((ns, _RET_W), BF16),
                   jax.ShapeDtypeStruct(s0.shape, F32)],
        grid=(ns // sb, RET_HEADS),
        in_specs=[pl.BlockSpec(memory_space=pltpu.SMEM)]
                 + [pl.BlockSpec((sb, RET_DK), lambda i, h, col=col: (base + i, col * RET_HEADS + h))
                    for col in range(4)]
                 + [pl.BlockSpec((sb, None, RET_DK, RET_DV), sblk)],
        out_specs=[pl.BlockSpec((sb, RET_DV), lambda i, h: (i, h)),
                   pl.BlockSpec((sb, None, RET_DK, RET_DV), sblk)],
        compiler_params=_cparams("arbitrary", "arbitrary"),
    )(gamma, ret4, ret4, ret4, ret4, s0)


def _sink_softmax(s, mask, sink):
    if mask is not None:
        s = jnp.where(mask, s, -jnp.inf)
    m = jnp.maximum(jnp.max(s, axis=-1, keepdims=True), sink)
    p = jnp.exp(s - m)
    return p / (jnp.sum(p, axis=-1, keepdims=True) + jnp.exp(sink - m))


def _split_kv_heads(x):
    lo = lax.broadcasted_iota(I32, x.shape, 1) < SWA_HD
    h0_lo = jnp.where(lo, x, 0.0)
    h1_hi = jnp.where(lo, 0.0, x)
    return ((h0_lo, pltpu.roll(h0_lo, SWA_HD, 1)), (pltpu.roll(h1_hi, SWA_HD, 1), h1_hi))


SWA_STEP_BLOCKS = 4


def _swa_kernel(sink_ref, q_ref, kc_ref, kp_ref, vc_ref, vp_ref, o_ref):
    n = pl.program_id(1)
    c = WINDOW
    kk = jnp.concatenate([kp_ref[...], kc_ref[...]], axis=0)
    vv = jnp.concatenate([vp_ref[...], vc_ref[...]], axis=0)
    ks = [[a.astype(BF16) for a in pair] for pair in _split_kv_heads(kk)]
    vs = [[a.astype(BF16) for a in pair] for pair in _split_kv_heads(vv)]
    qi = lax.broadcasted_iota(I32, (2 * c, c), 0) % c
    ki = lax.broadcasted_iota(I32, (2 * c, c), 1)
    from_prev = ki > qi
    top = lax.broadcasted_iota(I32, (2 * c, 1), 0) < c
    units = [(s, kvh) for s in range(SWA_STEP_BLOCKS) for kvh in range(SWA_KV_HEADS)]
    scores = []
    for s, kvh in units:
        rows, keys = slice(s * c, (s + 1) * c), slice(s * c, (s + 2) * c)
        q2 = jnp.concatenate([q_ref[rows, 2 * kvh * LANES:(2 * kvh + 1) * LANES],
                              q_ref[rows, (2 * kvh + 1) * LANES:(2 * kvh + 2) * LANES]], axis=0)
        kcat = jnp.concatenate([ks[kvh][0][keys], ks[kvh][1][keys]], axis=0)
        scores.append(lax.dot_general(q2, kcat, (((1,), (1,)), ((), ())), preferred_element_type=F32))
    probs = []
    for (s, kvh), sc in zip(units, scores):
        ps = []
        for half in range(2):
            sink = jnp.where(top, sink_ref[4 * kvh + half], sink_ref[4 * kvh + 2 + half])
            s_prev = sc[:, half * 2 * c:half * 2 * c + c]
            s_own = sc[:, half * 2 * c + c:(half + 1) * 2 * c]
            if s == 0:
                s_prev = jnp.where(n > 0, s_prev, -jnp.inf)
            p = _sink_softmax(jnp.where(from_prev, s_prev, s_own), None, sink)
            ps += [jnp.where(from_prev, p, 0.0).astype(BF16), jnp.where(from_prev, 0.0, p).astype(BF16)]
        probs.append(jnp.concatenate(ps, axis=1))
    for (s, kvh), p in zip(units, probs):
        rows, keys = slice(s * c, (s + 1) * c), slice(s * c, (s + 2) * c)
        vcat = jnp.concatenate([vs[kvh][0][keys], vs[kvh][1][keys]], axis=0)
        o = jnp.dot(p, vcat, preferred_element_type=F32)
        o_ref[rows, 2 * kvh * LANES:(2 * kvh + 1) * LANES] = o[:c].astype(BF16)
        o_ref[rows, (2 * kvh + 1) * LANES:(2 * kvh + 2) * LANES] = o[c:].astype(BF16)


def _swa_prompt(sq, skv, sink, *, nb, t):
    rows = SWA_STEP_BLOCKS * WINDOW
    nsteps = t // rows
    nblk = t // WINDOW
    cur = lambda b, n: (b * nsteps + n, 0)
    curc = lambda col: (lambda b, n: (b * nsteps + n, col))
    prevc = lambda col: (lambda b, n: (b * nblk + jnp.maximum(n * SWA_STEP_BLOCKS - 1, 0), col))
    return pl.pallas_call(
        _swa_kernel,
        out_shape=jax.ShapeDtypeStruct((nb * t, _SWA_QW), BF16),
        grid=(nb, nsteps),
        in_specs=[pl.BlockSpec(memory_space=pltpu.SMEM),
                  pl.BlockSpec((rows, _SWA_QW), cur),
                  pl.BlockSpec((rows, _SWA_KW), curc(0)),
                  pl.BlockSpec((WINDOW, _SWA_KW), prevc(0)),
                  pl.BlockSpec((rows, _SWA_KW), curc(1)),
                  pl.BlockSpec((WINDOW, _SWA_KW), prevc(1))],
        out_specs=pl.BlockSpec((rows, _SWA_QW), cur),
        compiler_params=_cparams("arbitrary", "arbitrary"),
    )(sink, sq, skv, skv, skv, skv)


def _swa_sample_kernel(sink_ref, q_ref, kn_ref, vn_ref, kc_ref, vc_ref, o_ref, ko_ref, vo_ref, *, sb, w):
    pad = jnp.zeros((LANES - sb, _SWA_KW), F32)
    knt = jnp.concatenate([kn_ref[...], pad], axis=0).T
    vnt = jnp.concatenate([vn_ref[...], pad], axis=0).T
    kall = jnp.concatenate([kc_ref[b] for b in range(sb)] + [knt], axis=1)
    vall = jnp.concatenate([vc_ref[b] for b in range(sb)] + [vnt], axis=1)
    ncol = sb * w + LANES
    lo = lax.broadcasted_iota(I32, (sb, LANES), 1) < SWA_HD
    group = SWA_HEADS // SWA_KV_HEADS
    pieces = []
    for h in range(SWA_HEADS):
        slab = q_ref[:, (h // 2) * LANES:(h // 2 + 1) * LANES].astype(F32)
        mine = jnp.where(lo, slab, 0.0) if h % 2 == 0 else jnp.where(lo, 0.0, slab)
        pieces.append(mine if (h % 2) == (h // group) else pltpu.roll(mine, SWA_HD, 1))
    qrows = jnp.concatenate(pieces, axis=0)
    nrow = SWA_HEADS * sb
    s = _bdot(qrows, kall)
    rb = lax.broadcasted_iota(I32, (nrow, ncol), 0) % sb
    ci = lax.broadcasted_iota(I32, (nrow, ncol), 1)
    in_cache = (ci < sb * w) & (ci // w == rb) & ((w - ci % w) < WINDOW)
    mask = in_cache | (ci == sb * w + rb)
    sink_col = jnp.concatenate([jnp.full((sb, 1), sink_ref[h], F32) for h in range(SWA_HEADS)], axis=0)
    p = _sink_softmax(s, mask, sink_col)
    o = _bdot_nt(p, vall)
    for j in range(SWA_HEADS // 2):
        acc = jnp.zeros((sb, LANES), F32)
        for half in range(2):
            h = 2 * j + half
            oh = o[h * sb:(h + 1) * sb]
            own = jnp.where(lo, oh, 0.0) if h // group == 0 else jnp.where(lo, 0.0, oh)
            acc = acc + (own if (h // group) == half else pltpu.roll(own, SWA_HD, 1))
        o_ref[:, j * LANES:(j + 1) * LANES] = acc.astype(BF16)
    newest = lax.broadcasted_iota(I32, (_SWA_KW, w), 1) == w - 1
    for b in range(sb):
        ko_ref[b] = jnp.where(newest, knt[:, b:b + 1], pltpu.roll(kc_ref[b], w - 1, 1))
        vo_ref[b] = jnp.where(newest, vnt[:, b:b + 1], pltpu.roll(vc_ref[b], w - 1, 1))


def _swa_sample(sq, skv, cache_kt, cache_vt, sink, *, row0, ns):
    sb = min(16, ns)
    w = cache_kt.shape[2]
    base = row0 // sb
    blk = lambda i: (base + i, 0)
    cblk = lambda i: (i, 0, 0)
    cspec = pl.BlockSpec((sb, _SWA_KW, w), cblk)
    return pl.pallas_call(
        functools.partial(_swa_sample_kernel, sb=sb, w=w),
        out_shape=[jax.ShapeDtypeStruct((ns, _SWA_QW), BF16),
                   jax.ShapeDtypeStruct(cache_kt.shape, F32), jax.ShapeDtypeStruct(cache_vt.shape, F32)],
        grid=(ns // sb,),
        in_specs=[pl.BlockSpec(memory_space=pltpu.SMEM),
                  pl.BlockSpec((sb, _SWA_QW), blk),
                  pl.BlockSpec((sb, _SWA_KW), lambda i: (base + i, 0)),
                  pl.BlockSpec((sb, _SWA_KW), lambda i: (base + i, 1)),
                  cspec, cspec],
        out_specs=[pl.BlockSpec((sb, _SWA_QW), lambda i: (i, 0)), cspec, cspec],
        compiler_params=_cparams("arbitrary"),
    )(sink, sq, skv, skv, cache_kt, cache_vt)


def _route(logits):
    lane = lax.broadcasted_iota(I32, logits.shape, 1).astype(F32)
    big = float(1 << 20)
    neg = -jnp.inf

    def top(mask):
        v = jnp.max(jnp.where(mask, logits, neg), axis=-1, keepdims=True)
        i = jnp.min(jnp.where(mask & (logits == v), lane, big), axis=-1, keepdims=True)
        return v, i

    gmask = lane < N_GROUPS
    gmax, gsel = top(gmask)
    p_group = 1.0 / jnp.sum(jnp.where(gmask, jnp.exp(logits - gmax), 0.0), axis=-1, keepdims=True)
    first = N_GROUPS + gsel * EXPERTS_PER_GROUP
    emask = (lane >= first) & (lane < first + EXPERTS_PER_GROUP)
    v1, i1 = top(emask)
    v2, i2 = top(emask & (lane != i1))
    t = jnp.exp(v2 - v1)
    w1 = p_group / (1.0 + t)
    return (i1 - N_GROUPS).astype(I32), (i2 - N_GROUPS).astype(I32), w1, w1 * t


def _plan_tile(e1, e2, valid, carry):
    lane = lax.broadcasted_iota(I32, (TM, LANES), 1)
    oh1 = ((lane == e1) & valid).astype(F32)
    oh2 = ((lane == e2) & valid).astype(F32)
    oh = oh1 + oh2
    tri = (lax.broadcasted_iota(I32, (TM, TM), 0) > lax.broadcasted_iota(I32, (TM, TM), 1)).astype(BF16)
    before = _bdot(tri, oh)
    cnt = jnp.sum(oh, axis=0, keepdims=True)
    units = jnp.maximum(jnp.floor((cnt + (RUN - 1)) * (1.0 / RUN)), 1.0)
    upper = (lax.broadcasted_iota(I32, (LANES, LANES), 0) < lax.broadcasted_iota(I32, (LANES, LANES), 1))
    lstart = RUN * _bdot(jnp.broadcast_to(units, (SUBLANES, LANES)), upper.astype(BF16))[0:1]
    slot = lstart + before
    lp1 = jnp.sum(oh1 * slot, axis=-1, keepdims=True)
    lp2 = jnp.sum(oh2 * slot, axis=-1, keepdims=True)
    vcol = valid[:, 0:1]
    lp = jnp.where(lane == 0, jnp.where(vcol, lp1, -1.0), jnp.where(lane == 1, jnp.where(vcol, lp2, -1.0), 0.0))
    base = carry[...]
    carry[...] = base + RUN * units
    return lp, (RUN * units).astype(I32), lstart.astype(I32), base.astype(I32)


def _outproj_step(gated_of, oswa_of, x_of, g1_of, sh_of, sc_of, n_valid, sig_ref, n2_ref, wur_ref, wus_ref,
                  wo_ref, wrc_ref, br_ref, x1_o, h2_o, rl_o, lpt_o, cnt_o, ls_o, gb_o, carry):
    subs = [slice(r, r + TM) for r in range(0, TT, TM)]
    merged = []
    for rows in subs:
        y_ret = jnp.dot(gated_of(rows), wur_ref[...], preferred_element_type=F32)
        y_swa = jnp.dot(oswa_of(rows), wus_ref[...], preferred_element_type=F32)
        d = y_ret.shape[1]
        merged.append((sig_ref[rows, :d].astype(F32) * y_ret + sig_ref[rows, d:].astype(F32) * y_swa).astype(BF16))
    hs = []
    for rows, m in zip(subs, merged):
        x1 = x_of(rows) + g1_of(rows) * jnp.dot(m, wo_ref[...], preferred_element_type=F32)
        x1_o[rows] = x1
        h2 = _rms(x1, n2_ref[...]) * (1.0 + sc_of(rows)) + sh_of(rows)
        hi = h2.astype(BF16)
        h2_o[rows] = hi
        hs.append((hi, (h2 - hi.astype(F32)).astype(BF16)))
    routed = []
    for rows, (hi, lo) in zip(subs, hs):
        both = jnp.dot(hi, wrc_ref[...], preferred_element_type=F32)
        logits = (both[:, :LANES] + both[:, LANES:]
                  + jnp.dot(lo, wrc_ref[:, :LANES], preferred_element_type=F32) + br_ref[...])
        e1, e2, w1, w2 = _route(logits)
        lane = lax.broadcasted_iota(I32, logits.shape, 1)
        routed.append((e1, e2, jnp.where(lane == 2, w1, jnp.where(lane == 3, w2, 0.0))))
    for sub, (e1, e2, weights) in enumerate(routed):
        row = lax.broadcasted_iota(I32, (TM, LANES), 0) + sub * TM
        valid = (row >= 0) if n_valid is None else (row < n_valid)
        lp, cnt_o[sub], ls_o[sub], gb_o[sub] = _plan_tile(e1, e2, valid, carry)
        rl_o[sub * TM:(sub + 1) * TM] = lp + weights
        lpt_o[:, sub * TM:(sub + 1) * TM] = lp.T[0:SUBLANES]


def _outproj_kernel(gtp_ref, gts_ref, osp_ref, oss_ref, sig_ref, xp_ref, xs_ref, g1p_ref, shp_ref, scp_ref,
                    g1s_ref, shs_ref, scs_ref, *rest, nps, ns):
    i = pl.program_id(0)
    is_s = i >= nps
    carry = rest[-1]

    @pl.when(i == 0)
    def _():
        carry[...] = jnp.zeros_like(carry)

    @pl.when(jnp.logical_not(is_s))
    def _():
        _outproj_step(lambda rows: gtp_ref[rows], lambda rows: osp_ref[rows], lambda rows: xp_ref[rows],
                      lambda rows: g1p_ref[...], lambda rows: shp_ref[...], lambda rows: scp_ref[...], None,
                      sig_ref, *rest)

    @pl.when(is_s)
    def _():
        _outproj_step(lambda rows: gts_ref[rows], lambda rows: oss_ref[rows], lambda rows: xs_ref[rows],
                      lambda rows: g1s_ref[rows], lambda rows: shs_ref[rows], lambda rows: scs_ref[rows], ns,
                      sig_ref, *rest)


def _outproj(gated_p, gated_s, oswa_p, oswa_s, sig2, xp, xs_pad, modp, mods, n2, wur, wus, wo, wr, br,
             *, nps, spb, nb, ns):
    d = xp.shape[1]
    sub = TT // TM
    nrow = (nps + 1) * TT
    row = lambda i: (i, 0)
    pstep = lambda i: (jnp.minimum(i, nps - 1), 0)
    pbatch = lambda col: (lambda i: (jnp.minimum(i // spb, nb - 1), 0, col))
    scol = lambda col: (lambda i: (0, col))
    const = lambda i: (0, 0)
    wr_hi = wr.astype(BF16)
    wrc = jnp.concatenate([wr_hi, (wr - wr_hi.astype(F32)).astype(BF16)], axis=1)
    meta = jax.ShapeDtypeStruct(((nps + 1) * sub, 1, LANES), I32)
    mspec = pl.BlockSpec((sub, 1, LANES), lambda i: (i, 0, 0))
    return pl.pallas_call(
        functools.partial(_outproj_kernel, nps=nps, ns=ns),
        out_shape=[jax.ShapeDtypeStruct((nrow, d), F32),
                   jax.ShapeDtypeStruct((nrow, d), BF16),
                   jax.ShapeDtypeStruct((nrow, LANES), F32),
                   jax.ShapeDtypeStruct((SUBLANES, nrow), F32), meta, meta, meta],
        grid=(nps + 1,),
        in_specs=[pl.BlockSpec((TT, _RET_W), pstep), pl.BlockSpec((TT, _RET_W), const),
                  pl.BlockSpec((TT, _SWA_QW), pstep), pl.BlockSpec((TT, _SWA_QW), const),
                  pl.BlockSpec((TT, 2 * d), row),
                  pl.BlockSpec((TT, d), pstep), pl.BlockSpec((TT, d), const),
                  pl.BlockSpec((None, 1, d), pbatch(2)), pl.BlockSpec((None, 1, d), pbatch(3)),
                  pl.BlockSpec((None, 1, d), pbatch(4)),
                  pl.BlockSpec((TT, d), scol(2)), pl.BlockSpec((TT, d), scol(3)), pl.BlockSpec((TT, d), scol(4)),
                  pl.BlockSpec((1, d), const),
                  pl.BlockSpec(wur.shape, const), pl.BlockSpec(wus.shape, const), pl.BlockSpec(wo.shape, const),
                  pl.BlockSpec(wrc.shape, const), pl.BlockSpec((1, LANES), const)],
        out_specs=[pl.BlockSpec((TT, d), row), pl.BlockSpec((TT, d), row), pl.BlockSpec((TT, LANES), row),
                   pl.BlockSpec((SUBLANES, TT), lambda i: (0, i)), mspec, mspec, mspec],
        scratch_shapes=[pltpu.VMEM((1, LANES), F32)],
        compiler_params=_cparams("arbitrary"),
    )(gated_p, gated_s, oswa_p, oswa_s, sig2, xp, xs_pad, modp, modp, modp, mods, mods, mods, n2,
      wur, wus, wo, wrc, br)


def _units(rows):
    return rows.reshape(rows.shape[0] // RUN, RUN, rows.shape[1])


def _run_copy(src, dst, s_start, d_start, n, sem):
    return pltpu.make_async_copy(src.at[pl.ds(s_start, n)], dst.at[pl.ds(d_start, n)], sem)


def _tile_rows(cnt_ref, ls_ref, step):
    last = step * N_EXPERTS + N_EXPERTS - 1
    return ls_ref[last] + cnt_ref[last]


def _each_run(step, fn):
    for e in range(N_EXPERTS):
        fn(step * N_EXPERTS + e)


def _dispatch_kernel(cnt_ref, ls_ref, gd_ref, ps_ref, pn_ref, nu_ref, h_ref, lp_ref, xs_ref,
                     sorted_scr, zero_scr, sem, zsem, *, nt, maxt):
    i = pl.program_id(0)

    def pad(e):
        return _run_copy(zero_scr, xs_ref, 0, ps_ref[e], pn_ref[e], zsem)

    def tail(j):
        return _run_copy(zero_scr, xs_ref, 0, j * (TE // RUN), TE // RUN, zsem)

    def each_pad(fn):
        def body(e, c):
            @pl.when(pn_ref[e] > 0)
            def _():
                fn(pad(e))
            return c
        lax.fori_loop(0, N_EXPERTS, body, 0)

    def each_tail(fn):
        def body(j, c):
            fn(tail(j))
            return c
        lax.fori_loop(nu_ref[0], maxt, body, 0)

    @pl.when(i == 0)
    def _():
        zero_scr[...] = jnp.zeros_like(zero_scr)
        each_pad(lambda cp: cp.start())
        each_tail(lambda cp: cp.start())

    lpt = lp_ref[...]
    slot = lax.broadcasted_iota(I32, (SLOTS, TM), 0).astype(F32)
    perm = ((slot == lpt[0:1]) | (slot == lpt[1:2])).astype(BF16)
    sorted_scr[i % 2] = _units(jnp.dot(perm, h_ref[...], preferred_element_type=F32).astype(BF16))

    def copy(step):
        return lambda k: _run_copy(sorted_scr.at[step % 2], xs_ref, ls_ref[k], gd_ref[k], cnt_ref[k], sem.at[step % 2])

    def wait_all(step):
        _run_copy(sorted_scr.at[step % 2], xs_ref, 0, 0, _tile_rows(cnt_ref, ls_ref, step), sem.at[step % 2]).wait()

    _each_run(i, lambda k: copy(i)(k).start())

    @pl.when(i > 0)
    def _():
        wait_all(i - 1)

    @pl.when(i == nt - 1)
    def _():
        wait_all(i)
        each_pad(lambda cp: cp.wait())
        each_tail(lambda cp: cp.wait())


def _dispatch(cnt, ls, gd, ps, pn, nu, h2, lp, *, nt, maxt):
    d = h2.shape[1]
    return pl.pallas_call(
        functools.partial(_dispatch_kernel, nt=nt, maxt=maxt),
        out_shape=jax.ShapeDtypeStruct((maxt * TE // RUN, RUN, d), BF16),
        grid_spec=pltpu.PrefetchScalarGridSpec(
            num_scalar_prefetch=6,
            grid=(nt,),
            in_specs=[pl.BlockSpec((TM, d), lambda i, *_: (i, 0)),
                      pl.BlockSpec((SUBLANES, TM), lambda i, *_: (0, i))],
            out_specs=pl.BlockSpec(memory_space=pl.ANY),
            scratch_shapes=[pltpu.VMEM((2, SLOTS // RUN, RUN, d), BF16), pltpu.VMEM((TE // RUN, RUN, d), BF16),
                            pltpu.SemaphoreType.DMA((2,)), pltpu.SemaphoreType.DMA(())]),
        compiler_params=_cparams("arbitrary"),
    )(cnt, ls, gd, ps, pn, nu, h2, lp)


def _experts_kernel(te_ref, nu_ref, start_ref, nxt_ref, par_ref, x_ref, w1_hbm, w3_hbm, w2_hbm, y_ref,
                    w1f, w3f, w2f, w1b, w3b, w2b, sem):
    j = pl.program_id(0)
    e = te_ref[j]

    def fetch(ex, slot):
        return [pltpu.make_async_copy(src.at[ex], dst.at[slot], sem.at[slot, n])
                for n, (src, dst) in enumerate(((w1_hbm, w1f), (w3_hbm, w3f), (w2_hbm, w2f)))]

    @pl.when(j == 0)
    def _():
        for cp in fetch(e, par_ref[e]):
            cp.start()

    @pl.when((j == start_ref[e]) & (j < nu_ref[0]))
    def _():
        slot = par_ref[e]

        @pl.when(nxt_ref[e] >= 0)
        def _():
            for cp in fetch(nxt_ref[e], 1 - slot):
                cp.start(priority=1)

        for cp in fetch(e, slot):
            cp.wait()
        w1b[...] = w1f[slot].astype(BF16)
        w3b[...] = w3f[slot].astype(BF16)
        w2b[...] = w2f[slot].astype(BF16)

    @pl.when(j < nu_ref[0])
    def _():
        subs = [slice(r // RUN, (r + TE_SUB) // RUN) for r in range(0, TE, TE_SUB)]
        xs = [x_ref[u].reshape(TE_SUB, x_ref.shape[-1]) for u in subs]
        ab = [(jnp.dot(x, w1b[...], preferred_element_type=F32),
               jnp.dot(x, w3b[...], preferred_element_type=F32)) for x in xs]
        hid = [(_silu(a) * b).astype(BF16) for a, b in ab]
        for u, h in zip(subs, hid):
            y_ref[u] = _units(jnp.dot(h, w2b[...], preferred_element_type=F32).astype(BF16))


def _experts(te, nu, start, nxt, par, xs, w1, w3, w2, *, maxt):
    d = xs.shape[2]
    f = w1.shape[2]
    used = lambda j, te, nu, *_: (jnp.minimum(j, nu[0] - 1), 0, 0)
    hbm = pl.BlockSpec(memory_space=pl.ANY)
    return pl.pallas_call(
        _experts_kernel,
        out_shape=jax.ShapeDtypeStruct(xs.shape, BF16),
        grid_spec=pltpu.PrefetchScalarGridSpec(
            num_scalar_prefetch=5,
            grid=(maxt,),
            in_specs=[pl.BlockSpec((TE // RUN, RUN, d), used), hbm, hbm, hbm],
            out_specs=pl.BlockSpec((TE // RUN, RUN, d), used),
            scratch_shapes=[pltpu.VMEM((2, d, f), F32), pltpu.VMEM((2, d, f), F32), pltpu.VMEM((2, f, d), F32),
                            pltpu.VMEM((d, f), BF16), pltpu.VMEM((d, f), BF16), pltpu.VMEM((f, d), BF16),
                            pltpu.SemaphoreType.DMA((2, 3))]),
        input_output_aliases={5: 0},
        compiler_params=_cparams("arbitrary"),
    )(te, nu, start, nxt, par, xs, w1, w3, w2)


def _combine_kernel(cnt_ref, ls_ref, gd_ref, ys_ref, rl_ref, x1_ref, g2p_ref, g2s_ref, fg_ref,
                    yp_o, ys_o, ybuf, sem, *, npt):
    i = pl.program_id(0)
    nt = npt + 1

    def fetch(step):
        ybuf[step % 2] = jnp.zeros(ybuf.shape[1:], BF16)
        _each_run(step, lambda k: _run_copy(ys_ref, ybuf.at[step % 2], gd_ref[k], ls_ref[k], cnt_ref[k],
                                            sem.at[step % 2]).start())

    @pl.when(i == 0)
    def _():
        fetch(0)

    @pl.when(i + 1 < nt)
    def _():
        fetch(i + 1)

    _run_copy(ys_ref, ybuf.at[i % 2], 0, 0, _tile_rows(cnt_ref, ls_ref, i), sem.at[i % 2]).wait()

    yb = ybuf[i % 2].reshape(SLOTS, x1_ref.shape[1])
    slot = lax.broadcasted_iota(I32, (TM, SLOTS), 1).astype(F32)
    lp = rt = rl_ref[...]

    def unsort(col):
        return jnp.dot((slot == lp[:, col:col + 1]).astype(BF16), yb, preferred_element_type=F32)

    moe = rt[:, 2:3] * unsort(0) + rt[:, 3:4] * unsort(1)
    g2 = jnp.where(i >= npt, g2s_ref[...], g2p_ref[...])
    y = _rms(x1_ref[...] + g2 * moe, fg_ref[...])

    @pl.when(i < npt)
    def _():
        yp_o[...] = y

    @pl.when(i >= npt)
    def _():
        ys_o[...] = y


def _combine(cnt, ls, gd, ys, rl, x1, modp, mods, fg, *, npt, tpb, nb):
    d = x1.shape[1]
    nt = npt + 1
    row = lambda i, *_: (i, 0)
    return pl.pallas_call(
        functools.partial(_combine_kernel, npt=npt),
        out_shape=[jax.ShapeDtypeStruct((npt * TM, d), F32), jax.ShapeDtypeStruct((TM, d), F32)],
        grid_spec=pltpu.PrefetchScalarGridSpec(
            num_scalar_prefetch=3,
            grid=(nt,),
            in_specs=[pl.BlockSpec(memory_space=pl.ANY),
                      pl.BlockSpec((TM, LANES), row), pl.BlockSpec((TM, d), row),
                      pl.BlockSpec((None, 1, d), lambda i, *_: (jnp.minimum(i // tpb, nb - 1), 0, 5)),
                      pl.BlockSpec((TM, d), lambda i, *_: (0, 5)),
                      pl.BlockSpec((1, d), lambda i, *_: (0, 0))],
            out_specs=[pl.BlockSpec((TM, d), lambda i, *_: (jnp.minimum(i, npt - 1), 0)),
                       pl.BlockSpec((TM, d), lambda i, *_: (0, 0))],
            scratch_shapes=[pltpu.VMEM((2, SLOTS // RUN, RUN, d), BF16), pltpu.SemaphoreType.DMA((2,))]),
        compiler_params=_cparams("arbitrary"),
    )(cnt, ls, gd, ys, rl, x1, modp, mods, fg)


def _rotation_tables(t):
    inv_r = jnp.repeat(1.0 / (ROPE_THETA ** jnp.linspace(0.0, 1.0, RET_DK // 2, dtype=F32)), 2)
    sign_r = jnp.where(jnp.arange(RET_DK) % 2 == 0, -1.0, 1.0).astype(F32)
    inv_w = jnp.tile(ROPE_THETA ** (-jnp.arange(0, SWA_HD, 2, dtype=F32) / SWA_HD), LANES // (SWA_HD // 2))
    sign_w = jnp.where(jnp.arange(LANES) % SWA_HD < SWA_HD // 2, -1.0, 1.0).astype(F32)
    hi = (jnp.arange(t // TAB_LO, dtype=I32) * TAB_LO).astype(F32)[:, None]
    lo = jnp.arange(TAB_LO, dtype=I32).astype(F32)[:, None]
    past = jnp.full((1, 1), PAST_LEN, F32)
    his, los, sample = [], [], []
    for inv, sign in ((inv_r[None, :], sign_r[None, :]), (inv_w[None, :], sign_w[None, :])):
        a, b, p = hi * inv, lo * inv, past * inv
        his += [jnp.cos(a), jnp.sin(a)]
        los += [jnp.cos(b), jnp.sin(b), jnp.cos(b) * sign, jnp.sin(b) * sign]
        sample += [jnp.cos(p), jnp.sin(p) * sign]
    return tuple(jnp.concatenate(parts, axis=1) for parts in (his, los, sample))


def kernel(x_prompt, x_sample, c_prompt, c_sample, state_ret, cache_swa_k, cache_swa_v, w_ada, b_ada, norm1_g, norm2_g, w_in, w_up_ret, w_up_swa, w_o, sink, w_rg, b_rg, w_re, b_re, w1, w3, w2, final_g):
    nb, t, d = x_prompt.shape
    ns, dec_seq, _ = x_sample.shape
    depth = w_ada.shape[0]
    assert depth == 1 and dec_seq == 1, "single layer, one new token per sequence"
    assert t % TT == 0 and ns <= TM and ns % 16 == 0 and d % LANES == 0
    assert t % RET_BLOCK == 0 and t % (SWA_STEP_BLOCKS * WINDOW) == 0 and TT % (SUBLANES * TAB_LO) == 0
    assert N_GROUPS + N_EXPERTS <= LANES
    w = cache_swa_k.shape[2]
    tpb = t // TM
    npt = nb * tpb
    spb = t // TT
    nps = nb * spb
    nt = (nps + 1) * (TT // TM)
    np_rows = nb * t
    n_tok = np_rows + ns
    maxt = -(-(2 * n_tok + nt * N_EXPERTS * RUN + N_EXPERTS * (TE - 1)) // TE)

    xp = x_prompt.reshape(np_rows, d)
    xs_pad = jnp.pad(x_sample.reshape(ns, d), ((0, TT - ns), (0, 0)))

    c_all = jnp.concatenate([jnp.pad(c_sample, ((0, TT - ns), (0, 0))),
                             jnp.pad(c_prompt, ((0, SUBLANES - nb % SUBLANES), (0, 0)))])
    mods = _modulation(c_all, w_ada[0], b_ada[0])
    modp = mods[TT:TT + nb].reshape(nb, 1, 6 * d)

    ret4, sq, skv, sig2 = _inproj(
        xp, xs_pad, modp, mods, norm1_g, w_in[0].astype(BF16), *_rotation_tables(t), nps=nps, spb=spb, nb=nb)

    gamma = jnp.asarray([1.0 - 2.0 ** (-5.0 - h) for h in range(RET_HEADS)], F32)
    gated_p, st_p = _retention_prompt(ret4, nb=nb, t=t)
    gated_s, st_s = _retention_sample(ret4, state_ret[0], gamma, row0=np_rows, ns=ns)
    oswa_p = _swa_prompt(sq, skv, sink[0], nb=nb, t=t)
    to_t = lambda c: jnp.transpose(c[0], (0, 2, 3, 1)).reshape(ns, _SWA_KW, w)
    from_t = lambda c: jnp.transpose(c.reshape(ns, SWA_KV_HEADS, SWA_HD, w), (0, 3, 1, 2))[None]
    oswa_s, ks_new, vs_new = _swa_sample(sq, skv, to_t(cache_swa_k), to_t(cache_swa_v), sink[0],
                                         row0=np_rows, ns=ns)
    gated_s = jnp.pad(gated_s, ((0, TT - ns), (0, 0)))
    oswa_s = jnp.pad(oswa_s, ((0, TT - ns), (0, 0)))

    wr = jnp.pad(jnp.concatenate([w_rg[0], w_re[0]], axis=1), ((0, 0), (0, LANES - N_GROUPS - N_EXPERTS)))
    br = jnp.pad(jnp.concatenate([b_rg[0], b_re[0]]), (0, LANES - N_GROUPS - N_EXPERTS)).reshape(1, LANES)
    x1, h2, rl, lpt, cnt, ls, gb = _outproj(
        gated_p, gated_s, oswa_p, oswa_s, sig2, xp, xs_pad, modp, mods, norm2_g,
        w_up_ret[0].astype(BF16), w_up_swa[0].astype(BF16), w_o[0].astype(BF16), wr, br,
        nps=nps, spb=spb, nb=nb, ns=ns)
    cnt = cnt[:, 0, :N_EXPERTS]
    ls = ls[:, 0, :N_EXPERTS]
    gb = gb[:, 0, :N_EXPERTS]
    seg = jnp.sum(cnt, axis=0)
    tiles = (seg + TE - 1) // TE
    tile_end = jnp.cumsum(tiles)
    row_start = (tile_end - tiles) * TE
    gd = (gb + row_start[None, :]).reshape(-1)
    n_used = tile_end[-1:]
    jj = jnp.minimum(jnp.arange(maxt, dtype=I32), n_used[0] - 1)
    te = jnp.minimum(jnp.sum((tile_end[None, :] <= jj[:, None]).astype(I32), axis=1), N_EXPERTS - 1)
    cnt, ls, gd = cnt.reshape(-1) // RUN, ls.reshape(-1) // RUN, gd // RUN
    n_used = n_used.astype(I32)
    xs = _dispatch(cnt, ls, gd, (row_start + seg) // RUN, (tiles * TE - seg) // RUN, n_used, h2, lpt,
                   nt=nt, maxt=maxt)
    has = tiles > 0
    eidx = jnp.arange(N_EXPERTS, dtype=I32)
    later = jnp.where(has[None, :] & (eidx[None, :] > eidx[:, None]), eidx[None, :], N_EXPERTS)
    nxt_e = jnp.min(later, axis=1)
    nxt_e = jnp.where(nxt_e < N_EXPERTS, nxt_e, -1).astype(I32)
    par_e = ((jnp.cumsum(has.astype(I32)) - 1) % 2).astype(I32)
    ys = _experts(te, n_used, (tile_end - tiles).astype(I32), nxt_e, par_e, xs, w1[0], w3[0], w2[0], maxt=maxt)
    y_p, y_s = _combine(cnt, ls, gd, ys, rl, x1, modp, mods, final_g.reshape(1, d), npt=npt, tpb=tpb, nb=nb)

    y_prompt = y_p.reshape(nb, t, d)
    y_sample = y_s[:ns].reshape(ns, 1, d)
    wk = min(WINDOW, t)
    tails = jnp.stack([skv[(b + 1) * t - wk:(b + 1) * t] for b in range(nb)])
    skp, svp = (tails[:, :, part * _SWA_KW:(part + 1) * _SWA_KW].reshape(nb, wk, SWA_KV_HEADS, SWA_HD)
                for part in range(2))
    return (y_prompt, y_sample, st_p[None], st_s[None], skp[None], svp[None], from_t(ks_new), from_t(vs_new))
```
